```python
import jax, jax.numpy as jnp
from jax import lax
import numpy as np

D_MODEL = 1024
BATCH = 8
SEQ = 4096
DEPTH = 4

CHUNK = 64
N_MIXERS = 2
EPS = 1e-6

GDN_QK_HEADS = 8
GDN_V_HEADS = 16
GDN_HEAD_DIM = 128
GDN_QK_WIDTH = GDN_QK_HEADS * GDN_HEAD_DIM
GDN_V_WIDTH = GDN_V_HEADS * GDN_HEAD_DIM
GDN_CONV_CH = 2 * GDN_QK_WIDTH + GDN_V_WIDTH
GDN_IN_WIDTH = GDN_CONV_CH + GDN_V_WIDTH + 2 * GDN_V_HEADS
CONV_WIDTH = 4

FOX_HEADS = 16
FOX_HEAD_DIM = 64
FOX_WIDTH = FOX_HEADS * FOX_HEAD_DIM
FOX_IN_WIDTH = 4 * FOX_WIDTH + FOX_HEADS
Q_BLOCK = 128

N_LAYERS_A = (DEPTH + 1) // 2
N_LAYERS_B = DEPTH // 2

kernel_name = "hybrid_gdn_fox_adaln_trunk"


def rmsnorm(x, w):
    xf = x.astype(jnp.float32)
    y = xf * lax.rsqrt(jnp.mean(xf * xf, axis=-1, keepdims=True) + EPS)
    return (y * w.astype(jnp.float32)).astype(x.dtype)


def l2norm(x):
    return x * lax.rsqrt(jnp.sum(x * x, axis=-1, keepdims=True) + EPS)


def causal_depthwise_conv(x, w):
    C = x.shape[-1]
    return lax.conv_general_dilated(
        x, w[:, None, :].astype(x.dtype), window_strides=(1,),
        padding=[(CONV_WIDTH - 1, 0)], dimension_numbers=('NWC', 'WIO', 'NWC'),
        feature_group_count=C)


def gated_delta_rule(q, k, v, g, beta):
    B, S, H, DK = q.shape
    DV = v.shape[-1]
    N = S // CHUNK

    def to_chunks(t):
        t = jnp.moveaxis(t, 2, 1)
        return t.reshape(t.shape[:2] + (N, CHUNK) + t.shape[3:])

    q, k, v, g, beta = (to_chunks(t) for t in (q, k, v, g, beta))
    g = jnp.cumsum(g, axis=-1)
    idx = jnp.arange(CHUNK)
    lower = idx[:, None] >= idx[None, :]
    strict = idx[:, None] > idx[None, :]
    decay = jnp.exp(jnp.where(lower, g[..., :, None] - g[..., None, :], -jnp.inf))
    kb = k * beta[..., None]
    vb = v * beta[..., None]
    L = jnp.where(strict, jnp.einsum('bhncd,bhnsd->bhncs', kb, k) * decay, 0.0)
    eye = jnp.eye(CHUNK, dtype=q.dtype)
    T = lax.linalg.triangular_solve(eye + L, jnp.broadcast_to(eye, L.shape),
                                    left_side=True, lower=True, unit_diagonal=True)
    u = T @ vb
    w = T @ (kb * jnp.exp(g)[..., None])
    attn = jnp.einsum('bhncd,bhnsd->bhncs', q, k) * decay
    q_dec = q * jnp.exp(g)[..., None]
    k_dec = k * jnp.exp(g[..., -1:] - g)[..., None]
    g_last = jnp.exp(g[..., -1])
    xs = tuple(jnp.moveaxis(t, 2, 0) for t in (q_dec, k_dec, u, w, attn, g_last))

    def step(state, inp):
        q_n, k_n, u_n, w_n, a_n, gl_n = inp
        v_new = u_n - jnp.einsum('bhcd,bhde->bhce', w_n, state)
        o = (jnp.einsum('bhcd,bhde->bhce', q_n, state)
             + jnp.einsum('bhcs,bhse->bhce', a_n, v_new))
        state = state * gl_n[..., None, None] + jnp.einsum('bhcd,bhce->bhde', k_n, v_new)
        return state, o

    state0 = jnp.zeros((B, H, DK, DV), q.dtype)
    _, o = lax.scan(step, state0, xs)
    o = jnp.moveaxis(o, 0, 2).reshape(B, H, S, DV)
    return jnp.moveaxis(o, 1, 2)


def gdn_mixer(h, w_in, conv_w, A_log, dt_bias, norm_w, w_out):
    B, S, _ = h.shape
    proj = h @ w_in
    qkv, z, b, a = jnp.split(proj, [GDN_CONV_CH, GDN_CONV_CH + GDN_V_WIDTH,
                                    GDN_CONV_CH + GDN_V_WIDTH + GDN_V_HEADS], axis=-1)
    qkv = jax.nn.silu(causal_depthwise_conv(qkv, conv_w))
    q, k, v = jnp.split(qkv, [GDN_QK_WIDTH, 2 * GDN_QK_WIDTH], axis=-1)
    rep = GDN_V_HEADS // GDN_QK_HEADS
    q = l2norm(q.reshape(B, S, GDN_QK_HEADS, GDN_HEAD_DIM).astype(jnp.float32)) * GDN_HEAD_DIM ** -0.5
    k = l2norm(k.reshape(B, S, GDN_QK_HEADS, GDN_HEAD_DIM).astype(jnp.float32))
    q = jnp.repeat(q, rep, axis=2)
    k = jnp.repeat(k, rep, axis=2)
    v = v.reshape(B, S, GDN_V_HEADS, GDN_HEAD_DIM).astype(jnp.float32)
    beta = jax.nn.sigmoid(b.astype(jnp.float32))
    g = -jnp.exp(A_log.astype(jnp.float32)) * jax.nn.softplus(
        a.astype(jnp.float32) + dt_bias.astype(jnp.float32))
    o = gated_delta_rule(q, k, v, g, beta)
    zg = jax.nn.silu(z.reshape(B, S, GDN_V_HEADS, GDN_HEAD_DIM).astype(jnp.float32))
    o = rmsnorm(o, norm_w) * zg
    return o.reshape(B, S, GDN_V_WIDTH).astype(h.dtype) @ w_out


def forgetting_attention(q, k, v, cum):
    B, S, H, DH = q.shape
    NB = S // Q_BLOCK
    cum_k = jnp.moveaxis(cum, 1, 2)
    q_blocks = jnp.moveaxis(q.reshape(B, NB, Q_BLOCK, H, DH), 1, 0)
    c_blocks = jnp.moveaxis(cum_k.reshape(B, H, NB, Q_BLOCK), 2, 0)
    key_pos = jnp.arange(S)

    def block(inp):
        q_b, c_b, start = inp
        logits = jnp.einsum('bqhd,bkhd->bhqk', q_b, k)
        logits = logits + (c_b[..., :, None] - cum_k[..., None, :])
        q_pos = start + jnp.arange(Q_BLOCK)
        mask = key_pos[None, :] <= q_pos[:, None]
        p = jax.nn.softmax(jnp.where(mask, logits, -jnp.inf), axis=-1)
        return jnp.einsum('bhqk,bkhd->bqhd', p, v)

    o = lax.map(block, (q_blocks, c_blocks, jnp.arange(NB) * Q_BLOCK))
    return jnp.moveaxis(o, 0, 1).reshape(B, S, H, DH)


def fox_mixer(h, w_in, f_bias, qn_w, kn_w, w_out):
    B, S, _ = h.shape
    proj = h @ w_in
    q, k, v, z, f = jnp.split(proj, [FOX_WIDTH, 2 * FOX_WIDTH, 3 * FOX_WIDTH, 4 * FOX_WIDTH], axis=-1)
    shp = (B, S, FOX_HEADS, FOX_HEAD_DIM)
    q = rmsnorm(q.reshape(shp).astype(jnp.float32), qn_w) * FOX_HEAD_DIM ** -0.5
    k = rmsnorm(k.reshape(shp).astype(jnp.float32), kn_w)
    v = v.reshape(shp).astype(jnp.float32)
    log_f = jax.nn.log_sigmoid(f.astype(jnp.float32) + f_bias.astype(jnp.float32))
    cum = jnp.cumsum(log_f, axis=1)
    o = forgetting_attention(q, k, v, cum).reshape(B, S, FOX_WIDTH)
    o = o * jax.nn.silu(z.astype(jnp.float32))
    return o.astype(h.dtype) @ w_out


def _fwd_setup_inputs(seed: int = 0) -> dict:
    key = jax.random.key(seed)
    ks = jax.random.split(key, 20)
    nrm = jax.random.normal
    D = D_MODEL
    dt = jnp.exp(jax.random.uniform(ks[7], (N_LAYERS_A, GDN_V_HEADS),
                                    minval=np.log(1e-3), maxval=np.log(1e-1)))
    return {
        "x": nrm(ks[0], (BATCH, SEQ, D), jnp.float32),
        "c": nrm(ks[1], (BATCH, D), jnp.float32),
        "norm_w": 1.0 + 0.1 * nrm(ks[2], (DEPTH, D), jnp.float32),
        "ada_w": 0.5 * D ** -0.5 * nrm(ks[3], (DEPTH, D, 3 * D), jnp.float32),
        "ada_b": 0.02 * nrm(ks[4], (DEPTH, 3 * D), jnp.float32),
        "a_w_in": D ** -0.5 * nrm(ks[5], (N_LAYERS_A, D, GDN_IN_WIDTH), jnp.float32),
        "a_conv_w": CONV_WIDTH ** -0.5 * nrm(ks[6], (N_LAYERS_A, CONV_WIDTH, GDN_CONV_CH), jnp.float32),
        "a_A_log": jnp.log(jax.random.uniform(ks[8], (N_LAYERS_A, GDN_V_HEADS), minval=1.0, maxval=16.0)),
        "a_dt_bias": dt + jnp.log(-jnp.expm1(-dt)),
        "a_norm_w": 1.0 + 0.1 * nrm(ks[9], (N_LAYERS_A, GDN_HEAD_DIM), jnp.float32),
        "a_w_out": GDN_V_WIDTH ** -0.5 * nrm(ks[10], (N_LAYERS_A, GDN_V_WIDTH, D), jnp.float32),
        "b_w_in": D ** -0.5 * nrm(ks[11], (N_LAYERS_B, D, FOX_IN_WIDTH), jnp.float32),
        "b_f_bias": jax.random.uniform(ks[12], (N_LAYERS_B, FOX_HEADS), minval=1.0, maxval=5.0),
        "b_qn_w": 1.0 + 0.1 * nrm(ks[13], (N_LAYERS_B, FOX_HEAD_DIM), jnp.float32),
        "b_kn_w": 1.0 + 0.1 * nrm(ks[14], (N_LAYERS_B, FOX_HEAD_DIM), jnp.float32),
        "b_w_out": FOX_WIDTH ** -0.5 * nrm(ks[15], (N_LAYERS_B, FOX_WIDTH, D), jnp.float32),
        "final_norm_w": 1.0 + 0.1 * nrm(ks[16], (D,), jnp.float32),
    }


def _fwd_reference(x, c, norm_w, ada_w, ada_b, a_w_in, a_conv_w, a_A_log, a_dt_bias, a_norm_w,
              a_w_out, b_w_in, b_f_bias, b_qn_w, b_kn_w, b_w_out, final_norm_w):
    cond = jax.nn.silu(c)
    for i in range(DEPTH):
        mod = cond @ ada_w[i] + ada_b[i]
        shift, scale, gate = jnp.split(mod[:, None, :], 3, axis=-1)
        h = rmsnorm(x, norm_w[i]) * (1 + scale) + shift
        j = i // N_MIXERS
        if i % N_MIXERS == 0:
            y = gdn_mixer(h, a_w_in[j], a_conv_w[j], a_A_log[j], a_dt_bias[j], a_norm_w[j], a_w_out[j])
        else:
            y = fox_mixer(h, b_w_in[j], b_f_bias[j], b_qn_w[j], b_kn_w[j], b_w_out[j])
        x = x + gate * y
    return rmsnorm(x, final_norm_w)


import jax as _jax
import jax.numpy as _jnp

TWIN_FORMAT = 'train_step'
FWD_PARAMS = ['x', 'c', 'norm_w', 'ada_w', 'ada_b', 'a_w_in', 'a_conv_w', 'a_A_log', 'a_dt_bias', 'a_norm_w', 'a_w_out', 'b_w_in', 'b_f_bias', 'b_qn_w', 'b_kn_w', 'b_w_out', 'final_norm_w']
TWIN_WEIGHTS = ['norm_w', 'ada_w', 'ada_b', 'a_w_in', 'a_conv_w', 'a_A_log', 'a_dt_bias', 'a_norm_w', 'a_w_out', 'b_w_in', 'b_f_bias', 'b_qn_w', 'b_kn_w', 'b_w_out', 'final_norm_w']
TWIN_DIFF_INPUT = 'x'
TWIN_INPUTS = ['x', 'c', 'norm_w', 'ada_w', 'ada_b', 'a_w_in', 'a_conv_w', 'a_A_log', 'a_dt_bias', 'a_norm_w', 'a_w_out', 'b_w_in', 'b_f_bias', 'b_qn_w', 'b_kn_w', 'b_w_out', 'final_norm_w', 'loss_target', 'm_norm_w', 'm_ada_w', 'm_ada_b', 'm_a_w_in', 'm_a_conv_w', 'm_a_A_log', 'm_a_dt_bias', 'm_a_norm_w', 'm_a_w_out', 'm_b_w_in', 'm_b_f_bias', 'm_b_qn_w', 'm_b_kn_w', 'm_b_w_out', 'm_final_norm_w', 'v_norm_w', 'v_ada_w', 'v_ada_b', 'v_a_w_in', 'v_a_conv_w', 'v_a_A_log', 'v_a_dt_bias', 'v_a_norm_w', 'v_a_w_out', 'v_b_w_in', 'v_b_f_bias', 'v_b_qn_w', 'v_b_kn_w', 'v_b_w_out', 'v_final_norm_w']
TWIN_OUTPUTS = ['loss', 'grad_x', 'grad_norm_w', 'grad_ada_w', 'grad_ada_b', 'grad_a_w_in', 'grad_a_conv_w', 'grad_a_A_log', 'grad_a_dt_bias', 'grad_a_norm_w', 'grad_a_w_out', 'grad_b_w_in', 'grad_b_f_bias', 'grad_b_qn_w', 'grad_b_kn_w', 'grad_b_w_out', 'grad_final_norm_w', 'delta_norm_w', 'delta_ada_w', 'delta_ada_b', 'delta_a_w_in', 'delta_a_conv_w', 'delta_a_A_log', 'delta_a_dt_bias', 'delta_a_norm_w', 'delta_a_w_out', 'delta_b_w_in', 'delta_b_f_bias', 'delta_b_qn_w', 'delta_b_kn_w', 'delta_b_w_out', 'delta_final_norm_w', 'new_m_norm_w', 'new_m_ada_w', 'new_m_ada_b', 'new_m_a_w_in', 'new_m_a_conv_w', 'new_m_a_A_log', 'new_m_a_dt_bias', 'new_m_a_norm_w', 'new_m_a_w_out', 'new_m_b_w_in', 'new_m_b_f_bias', 'new_m_b_qn_w', 'new_m_b_kn_w', 'new_m_b_w_out', 'new_m_final_norm_w', 'new_v_norm_w', 'new_v_ada_w', 'new_v_ada_b', 'new_v_a_w_in', 'new_v_a_conv_w', 'new_v_a_A_log', 'new_v_a_dt_bias', 'new_v_a_norm_w', 'new_v_a_w_out', 'new_v_b_w_in', 'new_v_b_f_bias', 'new_v_b_qn_w', 'new_v_b_kn_w', 'new_v_b_w_out', 'new_v_final_norm_w']
TWIN_LEAF_KINDS = {'loss': 'loss', 'grad_x': 'grad_x', 'grad_norm_w': 'grad_w', 'grad_ada_w': 'grad_w', 'grad_ada_b': 'grad_w', 'grad_a_w_in': 'grad_w', 'grad_a_conv_w': 'grad_w', 'grad_a_A_log': 'grad_w', 'grad_a_dt_bias': 'grad_w', 'grad_a_norm_w': 'grad_w', 'grad_a_w_out': 'grad_w', 'grad_b_w_in': 'grad_w', 'grad_b_f_bias': 'grad_w', 'grad_b_qn_w': 'grad_w', 'grad_b_kn_w': 'grad_w', 'grad_b_w_out': 'grad_w', 'grad_final_norm_w': 'grad_w', 'delta_norm_w': 'delta_w', 'delta_ada_w': 'delta_w', 'delta_ada_b': 'delta_w', 'delta_a_w_in': 'delta_w', 'delta_a_conv_w': 'delta_w', 'delta_a_A_log': 'delta_w', 'delta_a_dt_bias': 'delta_w', 'delta_a_norm_w': 'delta_w', 'delta_a_w_out': 'delta_w', 'delta_b_w_in': 'delta_w', 'delta_b_f_bias': 'delta_w', 'delta_b_qn_w': 'delta_w', 'delta_b_kn_w': 'delta_w', 'delta_b_w_out': 'delta_w', 'delta_final_norm_w': 'delta_w', 'new_m_norm_w': 'new_m', 'new_m_ada_w': 'new_m', 'new_m_ada_b': 'new_m', 'new_m_a_w_in': 'new_m', 'new_m_a_conv_w': 'new_m', 'new_m_a_A_log': 'new_m', 'new_m_a_dt_bias': 'new_m', 'new_m_a_norm_w': 'new_m', 'new_m_a_w_out': 'new_m', 'new_m_b_w_in': 'new_m', 'new_m_b_f_bias': 'new_m', 'new_m_b_qn_w': 'new_m', 'new_m_b_kn_w': 'new_m', 'new_m_b_w_out': 'new_m', 'new_m_final_norm_w': 'new_m', 'new_v_norm_w': 'new_v', 'new_v_ada_w': 'new_v', 'new_v_ada_b': 'new_v', 'new_v_a_w_in': 'new_v', 'new_v_a_conv_w': 'new_v', 'new_v_a_A_log': 'new_v', 'new_v_a_dt_bias': 'new_v', 'new_v_a_norm_w': 'new_v', 'new_v_a_w_out': 'new_v', 'new_v_b_w_in': 'new_v', 'new_v_b_f_bias': 'new_v', 'new_v_b_qn_w': 'new_v', 'new_v_b_kn_w': 'new_v', 'new_v_b_w_out': 'new_v', 'new_v_final_norm_w': 'new_v'}


def _forward(args):
    return _fwd_reference(*[args[k] for k in FWD_PARAMS])


def _output_shape():
    out = _jax.eval_shape(lambda: _forward(_fwd_setup_inputs(0)))
    return out.shape, out.dtype

N_MICROBATCH = 1
ADAM_LR = 0.001
ADAM_B1 = 0.9
ADAM_B2 = 0.999
ADAM_EPS = 1e-08
ADAM_WD = 0.01
ADAM_STEP = 10
PER_EXAMPLE_BATCH_AXIS = {'x': 0, 'c': 0, 'loss_target': 0}
SHARED_INPUTS = []
_WEIGHT_DTYPES = {'norm_w': _jnp.float32, 'ada_w': _jnp.float32, 'ada_b': _jnp.float32, 'a_w_in': _jnp.float32, 'a_conv_w': _jnp.float32, 'a_A_log': _jnp.float32, 'a_dt_bias': _jnp.float32, 'a_norm_w': _jnp.float32, 'a_w_out': _jnp.float32, 'b_w_in': _jnp.float32, 'b_f_bias': _jnp.float32, 'b_qn_w': _jnp.float32, 'b_kn_w': _jnp.float32, 'b_w_out': _jnp.float32, 'final_norm_w': _jnp.float32}
MOMENT_SCALE = {'norm_w': 4.615187e-02, 'ada_w': 7.430760e-02, 'ada_b': 1.338369e-01, 'a_w_in': 2.570693e-02, 'a_conv_w': 2.555366e-02, 'a_A_log': 1.090362e-01, 'a_dt_bias': 1.034445e-01, 'a_norm_w': 1.070862e-01, 'a_w_out': 3.950158e-02, 'b_w_in': 1.542340e-02, 'b_f_bias': 6.879674e-02, 'b_qn_w': 4.021647e-02, 'b_kn_w': 4.125638e-02, 'b_w_out': 1.871412e-02, 'final_norm_w': 3.226961e+01}


def _to_microbatches(a, axis):
    t = _jnp.moveaxis(a, axis, 0)
    t = t.reshape((N_MICROBATCH, t.shape[0] // N_MICROBATCH) + t.shape[1:])
    return _jnp.moveaxis(t, 1, axis + 1)


def setup_inputs(seed: int = 0) -> dict:
    inp = _fwd_setup_inputs(seed)
    key = _jax.random.fold_in(_jax.random.key(seed), 7919)
    shape, _ = _output_shape()
    out = dict(inp)
    out["loss_target"] = _jax.random.normal(_jax.random.fold_in(key, 0), shape, _jnp.float32)
    for i, name in enumerate(TWIN_WEIGHTS):
        w = inp[name].astype(_jnp.float32)
        if MOMENT_SCALE is None:
            s = _jnp.sqrt(_jnp.mean(_jnp.square(w)) + 1e-30)
        else:
            s = MOMENT_SCALE[name]
        km, kv = _jax.random.split(_jax.random.fold_in(key, i + 1))
        out[name] = w
        out["m_" + name] = s * _jax.random.normal(km, w.shape, _jnp.float32)
        out["v_" + name] = (s * s) * _jax.random.uniform(kv, w.shape, _jnp.float32, 0.5, 1.5)
    if N_MICROBATCH > 1:
        for name, axis in PER_EXAMPLE_BATCH_AXIS.items():
            out[name] = _to_microbatches(out[name], axis)
    return {'x': out['x'], 'c': out['c'], 'norm_w': out['norm_w'], 'ada_w': out['ada_w'], 'ada_b': out['ada_b'], 'a_w_in': out['a_w_in'], 'a_conv_w': out['a_conv_w'], 'a_A_log': out['a_A_log'], 'a_dt_bias': out['a_dt_bias'], 'a_norm_w': out['a_norm_w'], 'a_w_out': out['a_w_out'], 'b_w_in': out['b_w_in'], 'b_f_bias': out['b_f_bias'], 'b_qn_w': out['b_qn_w'], 'b_kn_w': out['b_kn_w'], 'b_w_out': out['b_w_out'], 'final_norm_w': out['final_norm_w'], 'loss_target': out['loss_target'], 'm_norm_w': out['m_norm_w'], 'm_ada_w': out['m_ada_w'], 'm_ada_b': out['m_ada_b'], 'm_a_w_in': out['m_a_w_in'], 'm_a_conv_w': out['m_a_conv_w'], 'm_a_A_log': out['m_a_A_log'], 'm_a_dt_bias': out['m_a_dt_bias'], 'm_a_norm_w': out['m_a_norm_w'], 'm_a_w_out': out['m_a_w_out'], 'm_b_w_in': out['m_b_w_in'], 'm_b_f_bias': out['m_b_f_bias'], 'm_b_qn_w': out['m_b_qn_w'], 'm_b_kn_w': out['m_b_kn_w'], 'm_b_w_out': out['m_b_w_out'], 'm_final_norm_w': out['m_final_norm_w'], 'v_norm_w': out['v_norm_w'], 'v_ada_w': out['v_ada_w'], 'v_ada_b': out['v_ada_b'], 'v_a_w_in': out['v_a_w_in'], 'v_a_conv_w': out['v_a_conv_w'], 'v_a_A_log': out['v_a_A_log'], 'v_a_dt_bias': out['v_a_dt_bias'], 'v_a_norm_w': out['v_a_norm_w'], 'v_a_w_out': out['v_a_w_out'], 'v_b_w_in': out['v_b_w_in'], 'v_b_f_bias': out['v_b_f_bias'], 'v_b_qn_w': out['v_b_qn_w'], 'v_b_kn_w': out['v_b_kn_w'], 'v_b_w_out': out['v_b_w_out'], 'v_final_norm_w': out['v_final_norm_w']}


def _loss(weights, diff, rest, loss_target):
    with _jax.named_scope("forward"):
        args = {**rest, TWIN_DIFF_INPUT: diff, **{k: w.astype(_WEIGHT_DTYPES[k]) for k, w in weights.items()}}
        y = _forward(args)
    with _jax.named_scope("loss_head"):
        err = _jnp.square(y.astype(_jnp.float32) - loss_target)
        return 0.5 * _jnp.sum(_jnp.mean(err, axis=-1)) if err.ndim else 0.5 * err


def _adamw(w, g, m, v):
    m = ADAM_B1 * m + (1.0 - ADAM_B1) * g
    v = ADAM_B2 * v + (1.0 - ADAM_B2) * _jnp.square(g)
    m_hat = m / (1.0 - ADAM_B1 ** ADAM_STEP)
    v_hat = v / (1.0 - ADAM_B2 ** ADAM_STEP)
    delta = -ADAM_LR * (m_hat / (_jnp.sqrt(v_hat) + ADAM_EPS) + ADAM_WD * w)
    return delta, m, v


def reference(x, c, norm_w, ada_w, ada_b, a_w_in, a_conv_w, a_A_log, a_dt_bias, a_norm_w, a_w_out, b_w_in, b_f_bias, b_qn_w, b_kn_w, b_w_out, final_norm_w, loss_target, m_norm_w, m_ada_w, m_ada_b, m_a_w_in, m_a_conv_w, m_a_A_log, m_a_dt_bias, m_a_norm_w, m_a_w_out, m_b_w_in, m_b_f_bias, m_b_qn_w, m_b_kn_w, m_b_w_out, m_final_norm_w, v_norm_w, v_ada_w, v_ada_b, v_a_w_in, v_a_conv_w, v_a_A_log, v_a_dt_bias, v_a_norm_w, v_a_w_out, v_b_w_in, v_b_f_bias, v_b_qn_w, v_b_kn_w, v_b_w_out, v_final_norm_w):
    given = dict(x=x, c=c, norm_w=norm_w, ada_w=ada_w, ada_b=ada_b, a_w_in=a_w_in, a_conv_w=a_conv_w, a_A_log=a_A_log, a_dt_bias=a_dt_bias, a_norm_w=a_norm_w, a_w_out=a_w_out, b_w_in=b_w_in, b_f_bias=b_f_bias, b_qn_w=b_qn_w, b_kn_w=b_kn_w, b_w_out=b_w_out, final_norm_w=final_norm_w, loss_target=loss_target, m_norm_w=m_norm_w, m_ada_w=m_ada_w, m_ada_b=m_ada_b, m_a_w_in=m_a_w_in, m_a_conv_w=m_a_conv_w, m_a_A_log=m_a_A_log, m_a_dt_bias=m_a_dt_bias, m_a_norm_w=m_a_norm_w, m_a_w_out=m_a_w_out, m_b_w_in=m_b_w_in, m_b_f_bias=m_b_f_bias, m_b_qn_w=m_b_qn_w, m_b_kn_w=m_b_kn_w, m_b_w_out=m_b_w_out, m_final_norm_w=m_final_norm_w, v_norm_w=v_norm_w, v_ada_w=v_ada_w, v_ada_b=v_ada_b, v_a_w_in=v_a_w_in, v_a_conv_w=v_a_conv_w, v_a_A_log=v_a_A_log, v_a_dt_bias=v_a_dt_bias, v_a_norm_w=v_a_norm_w, v_a_w_out=v_a_w_out, v_b_w_in=v_b_w_in, v_b_f_bias=v_b_f_bias, v_b_qn_w=v_b_qn_w, v_b_kn_w=v_b_kn_w, v_b_w_out=v_b_w_out, v_final_norm_w=v_final_norm_w)
    weights = {n: given[n] for n in TWIN_WEIGHTS}
    shared = {n: given[n] for n in SHARED_INPUTS}
    per_example = {n: given[n] for n in ['x', 'c']}
    grad_fn = _jax.value_and_grad(_loss, argnums=(0, 1))

    def one_microbatch(ex, loss_target):
        ex = dict(ex)
        diff = ex.pop(TWIN_DIFF_INPUT)
        return grad_fn(weights, diff, {**shared, **ex}, loss_target)

    if N_MICROBATCH == 1:
        loss, (grad_w, grad_x) = one_microbatch(per_example, given["loss_target"])
    else:
        def body(carry, xs):
            loss_sum, grad_sum = carry
            l_k, (gw_k, gx_k) = one_microbatch(xs[0], xs[1])
            with _jax.named_scope("update"):
                return (loss_sum + l_k, _jax.tree.map(_jnp.add, grad_sum, gw_k)), gx_k

        init = (_jnp.zeros((), _jnp.float32), _jax.tree.map(_jnp.zeros_like, weights))
        (loss, grad_w), grad_x = _jax.lax.scan(body, init, (per_example, given["loss_target"]))
    with _jax.named_scope("update"):
        delta_w, new_m, new_v = {}, {}, {}
        for n in TWIN_WEIGHTS:
            delta_w[n], new_m[n], new_v[n] = _adamw(weights[n], grad_w[n], given["m_" + n], given["v_" + n])
    return (loss, grad_x, *[grad_w[n] for n in TWIN_WEIGHTS], *[delta_w[n] for n in TWIN_WEIGHTS],
            *[new_m[n] for n in TWIN_WEIGHTS], *[new_v[n] for n in TWIN_WEIGHTS])
```

```python
import functools

import jax
import jax.numpy as jnp
from jax import lax
from jax.experimental import pallas as pl
from jax.experimental.pallas import tpu as pltpu

f32 = jnp.float32
bf16 = jnp.bfloat16
SDS = jax.ShapeDtypeStruct

EPS = 1e-6
CHUNK = 64
HD = 128
GDN_QK_HEADS = 8
GDN_V_HEADS = 16
GDN_QK_W = GDN_QK_HEADS * HD
GDN_V_W = GDN_V_HEADS * HD
GDN_CONV = 2 * GDN_QK_W + GDN_V_W
GDN_IN = GDN_CONV + GDN_V_W + 2 * GDN_V_HEADS
GDN_IN_PAD = GDN_CONV + GDN_V_W + 256
GDN_TN = 640
FOX_H = 16
FOX_D = 64
FOX_W = FOX_H * FOX_D
FOX_IN = 4 * FOX_W + FOX_H
FOX_IN_PAD = 4 * FOX_W + 128
FOX_TN = 1408
FOX_PW = FOX_H * 128
NDEV = 8
MESH_AXES = ("x", "y", "c")
NEG = -1e30

ADAM_LR = 0.001
ADAM_B1 = 0.9
ADAM_B2 = 0.999
ADAM_EPS = 1e-08
ADAM_WD = 0.01
ADAM_STEP = 10

VMEM_LIMIT = 56 * 1024 * 1024


def _cp(sem=None):
    return pltpu.CompilerParams(dimension_semantics=sem, vmem_limit_bytes=VMEM_LIMIT)


def _bdot(a, b, dims):
    return lax.dot_general(a.astype(bf16), b.astype(bf16), (dims, ((), ())), preferred_element_type=f32)


def _nn(a, b):
    return _bdot(a, b, ((1,), (0,)))


def _nt(a, b):
    return _bdot(a, b, ((1,), (1,)))


def _tn(a, b):
    return _bdot(a, b, ((0,), (0,)))


def _hdot(a, b, dims=((1,), (0,))):
    return lax.dot_general(a, b, (dims, ((), ())), precision=lax.Precision.HIGHEST, preferred_element_type=f32)


@jax.custom_vjp
def _mm(a, b):
    return _nn(a, b)


_mm.defvjp(lambda a, b: (_nn(a, b), (a, b)), lambda r, g: (_nt(g, r[1]), _tn(r[0], g)))


@jax.custom_vjp
def _mm_nt(a, b):
    return _nt(a, b)


_mm_nt.defvjp(lambda a, b: (_nt(a, b), (a, b)), lambda r, g: (_nn(g, r[1]), _tn(g, r[0])))


@jax.custom_vjp
def _mm_tn(a, b):
    return _tn(a, b)


_mm_tn.defvjp(lambda a, b: (_tn(a, b), (a, b)), lambda r, g: (_nt(r[1], g), _nn(r[0], g)))


def _silu(x):
    return x * jax.nn.sigmoid(x)


def _rms_mod(x, nw, scale, shift):
    r = lax.rsqrt(jnp.mean(x * x, axis=-1, keepdims=True) + EPS)
    return (x * r * nw) * (1.0 + scale) + shift


def _rows(S, want):
    return min(want, S)


def inproj_fwd(x, nw, scale, shift, W, tn, name):
    S, D = x.shape
    N = W.shape[1]
    tm = _rows(S, 512)

    def body(x_ref, nw_ref, sc_ref, sh_ref, w_ref, proj_ref, h_ref):
        @pl.when(pl.program_id(1) == 0)
        def _():
            h_ref[...] = _rms_mod(x_ref[...], nw_ref[...], sc_ref[...], sh_ref[...]).astype(bf16)

        proj_ref[...] = jnp.dot(h_ref[...], w_ref[...], preferred_element_type=f32)

    vec = pl.BlockSpec((1, D), lambda i, j: (0, 0))
    return pl.pallas_call(
        body, name=name, grid=(S // tm, N // tn),
        in_specs=[pl.BlockSpec((tm, D), lambda i, j: (i, 0)), vec, vec, vec, pl.BlockSpec((D, tn), lambda i, j: (0, j))],
        out_specs=[pl.BlockSpec((tm, tn), lambda i, j: (i, j)), pl.BlockSpec((tm, D), lambda i, j: (i, 0))],
        out_shape=[SDS((S, N), f32), SDS((S, D), bf16)],
        compiler_params=_cp(("arbitrary", "arbitrary")),
    )(x, nw, scale, shift, W)


def inproj_bwd_x(x, nw, scale, shift, W, dproj, dx_res, tn, name):
    S, D = x.shape
    N = W.shape[1]
    tm = _rows(S, 512)
    nj = N // tn

    def body(x_ref, nw_ref, sc_ref, sh_ref, w_ref, dp_ref, dxr_ref, dx_ref, dnw_ref, dsc_ref, dsh_ref, acc):
        i, j = pl.program_id(0), pl.program_id(1)

        @pl.when(j == 0)
        def _():
            acc[...] = jnp.zeros_like(acc)

        @pl.when((i == 0) & (j == 0))
        def _():
            dnw_ref[...] = jnp.zeros_like(dnw_ref)
            dsc_ref[...] = jnp.zeros_like(dsc_ref)
            dsh_ref[...] = jnp.zeros_like(dsh_ref)

        acc[...] += _nt(dp_ref[...], w_ref[...])

        @pl.when(j == nj - 1)
        def _():
            _, vjp = jax.vjp(_rms_mod, x_ref[...], nw_ref[...], sc_ref[...], sh_ref[...])
            dx, dnw, dsc, dsh = vjp(acc[...])
            dx_ref[...] = dxr_ref[...] + dx
            dnw_ref[...] += dnw
            dsc_ref[...] += dsc
            dsh_ref[...] += dsh

    vec = pl.BlockSpec((1, D), lambda i, j: (0, 0))
    row = pl.BlockSpec((tm, D), lambda i, j: (i, 0))
    return pl.pallas_call(
        body, name=name, grid=(S // tm, nj),
        in_specs=[row, vec, vec, vec, pl.BlockSpec((D, tn), lambda i, j: (0, j)), pl.BlockSpec((tm, tn), lambda i, j: (i, j)), row],
        out_specs=[row, vec, vec, vec],
        out_shape=[SDS((S, D), f32), SDS((1, D), f32), SDS((1, D), f32), SDS((1, D), f32)],
        scratch_shapes=[pltpu.VMEM((tm, D), f32)],
        compiler_params=_cp(("arbitrary", "arbitrary")),
    )(x, nw, scale, shift, W, dproj, dx_res)


def matmul_tn(a, b, tn, name):
    S, K = a.shape
    N = b.shape[1]
    tm = _rows(S, 512)
    ni = S // tm

    def body(a_ref, b_ref, o_ref):
        @pl.when(pl.program_id(1) == 0)
        def _():
            o_ref[...] = jnp.zeros_like(o_ref)

        o_ref[...] += _tn(a_ref[...], b_ref[...])

    return pl.pallas_call(
        body, name=name, grid=(N // tn, ni),
        in_specs=[pl.BlockSpec((tm, K), lambda j, i: (i, 0)), pl.BlockSpec((tm, tn), lambda j, i: (i, j))],
        out_specs=pl.BlockSpec((K, tn), lambda j, i: (0, j)),
        out_shape=SDS((K, N), f32),
        compiler_params=_cp(("arbitrary", "arbitrary")),
    )(a, b)


def _conv_taps(xs, w, n_out):
    taps = []
    for j in range(4):
        s = 3 - j
        sh = xs if s == 0 else pltpu.roll(xs, s, axis=0)
        taps.append(sh[8:8 + n_out])
    conv = taps[0] * w[0] + taps[1] * w[1] + taps[2] * w[2] + taps[3] * w[3]
    return taps, conv


def _act_norm(conv, mul_norm, mul_plain):
    s = _silu(conv)
    r = lax.rsqrt(jnp.sum(s * s, axis=-1, keepdims=True) + EPS)
    return s * (mul_norm * r + mul_plain)


def _gdn_prep_mults(j):
    is_q = j < GDN_QK_HEADS
    is_k = (j >= GDN_QK_HEADS) & (j < 2 * GDN_QK_HEADS)
    mul_norm = jnp.where(is_q, HD ** -0.5, jnp.where(is_k, 1.0, 0.0)).astype(f32)
    mul_plain = jnp.where(is_q | is_k, 0.0, 1.0).astype(f32)
    return mul_norm, mul_plain


def gdn_prep_fwd(proj, conv_w, name):
    S = proj.shape[0]
    R = _rows(S, 512)

    def body(x_ref, w_ref, o_ref):
        mul_norm, mul_plain = _gdn_prep_mults(pl.program_id(0))
        w = [w_ref[j:j + 1, :] for j in range(4)]

        def piece(r, c):
            t0 = pl.multiple_of(r * R, R)
            cur = x_ref[pl.ds(t0, R), :]
            prev = x_ref[pl.ds(pl.multiple_of(jnp.maximum(t0 - 8, 0), 8), 8), :]
            prev = jnp.where(r == 0, 0.0, prev)
            _, conv = _conv_taps(jnp.concatenate([prev, cur], axis=0), w, R)
            o_ref[pl.ds(t0, R), :] = _act_norm(conv, mul_norm, mul_plain)
            return c

        lax.fori_loop(0, S // R, piece, 0)

    return pl.pallas_call(
        body, name=name, grid=(GDN_CONV // 128,),
        in_specs=[pl.BlockSpec((S, 128), lambda j: (0, j)), pl.BlockSpec((4, 128), lambda j: (0, j))],
        out_specs=pl.BlockSpec((S, 128), lambda j: (0, j)),
        out_shape=SDS((S, GDN_CONV), f32),
        compiler_params=_cp(("arbitrary",)),
    )(proj, conv_w)


def gdn_prep_bwd(proj, conv_w, dqkvc, name):
    S = proj.shape[0]
    R = _rows(S, 512)
    NP = S // R

    def body(x_ref, w_ref, dn_ref, dx_ref, dw_ref):
        mul_norm, mul_plain = _gdn_prep_mults(pl.program_id(0))
        w = [w_ref[j:j + 1, :] for j in range(4)]

        def piece(r, dw):
            t0 = pl.multiple_of(r * R, R)
            cur = x_ref[pl.ds(t0, R), :]
            prev = x_ref[pl.ds(pl.multiple_of(jnp.maximum(t0 - 8, 0), 8), 8), :]
            prev = jnp.where(r == 0, 0.0, prev)
            nxt0 = pl.multiple_of(jnp.minimum(t0 + R, S - 8), 8)
            nxt = x_ref[pl.ds(nxt0, 8), :]
            dn_cur = dn_ref[pl.ds(t0, R), :]
            dn_nxt = jnp.where(r == NP - 1, 0.0, dn_ref[pl.ds(nxt0, 8), :])
            xs = jnp.concatenate([prev, cur, nxt], axis=0)
            taps, conv = _conv_taps(xs, w, R + 8)
            dn = jnp.concatenate([dn_cur, dn_nxt], axis=0)
            _, vjp = jax.vjp(lambda c: _act_norm(c, mul_norm, mul_plain), conv)
            dxc = vjp(dn)[0]
            n = R + 8
            dx = dxc[0:R] * w[3]
            for j in range(3):
                s = 3 - j
                dx = dx + pltpu.roll(dxc, n - s, axis=0)[0:R] * w[j]
            dx_ref[pl.ds(t0, R), :] = dx
            return tuple(dw[j] + jnp.sum(dxc[0:R] * taps[j][0:R], axis=0, keepdims=True) for j in range(4))

        dw = lax.fori_loop(0, NP, piece, tuple(jnp.zeros((1, 128), f32) for _ in range(4)))
        for j in range(4):
            dw_ref[j:j + 1, :] = dw[j]

    col = pl.BlockSpec((S, 128), lambda j: (0, j))
    wsp = pl.BlockSpec((4, 128), lambda j: (0, j))
    return pl.pallas_call(
        body, name=name, grid=(GDN_CONV // 128,),
        in_specs=[col, wsp, col], out_specs=[col, wsp],
        out_shape=[SDS((S, GDN_CONV), f32), SDS((4, GDN_CONV), f32)],
        compiler_params=_cp(("arbitrary",)),
    )(proj, conv_w, dqkvc)


def _chunk_tril(R):
    ii = lax.broadcasted_iota(jnp.int32, (R, R), 0)
    jj = lax.broadcasted_iota(jnp.int32, (R, R), 1)
    return ((ii // CHUNK == jj // CHUNK) & (ii >= jj)).astype(f32)


def _gdn_gates(b, a, A_log, dt_bias, tril):
    beta = jax.nn.sigmoid(b)
    g = -jnp.exp(A_log) * jax.nn.softplus(a + dt_bias)
    return _hdot(tril, g), beta


_GDN_B_BLK = (GDN_CONV + GDN_V_W) // 128
_GDN_A_BLK = _GDN_B_BLK + 1


def gdn_gates_fwd(proj, A_log, dt_bias, name):
    S = proj.shape[0]
    R = _rows(S, 512)

    def body(b_ref, a_ref, al_ref, dt_ref, gc_ref, be_ref):
        gc, be = _gdn_gates(b_ref[...], a_ref[...], al_ref[...], dt_ref[...], _chunk_tril(R))
        gc_ref[...] = gc
        be_ref[...] = be

    vec = pl.BlockSpec((1, 128), lambda i: (0, 0))
    blk = pl.BlockSpec((R, 128), lambda i: (i, 0))
    return pl.pallas_call(
        body, name=name, grid=(S // R,),
        in_specs=[pl.BlockSpec((R, 128), lambda i: (i, _GDN_B_BLK)), pl.BlockSpec((R, 128), lambda i: (i, _GDN_A_BLK)), vec, vec],
        out_specs=[blk, blk], out_shape=[SDS((S, 128), f32), SDS((S, 128), f32)],
        compiler_params=_cp(("arbitrary",)),
    )(proj, proj, A_log, dt_bias)


def gdn_gates_bwd(proj, A_log, dt_bias, dgc, dbeta, name):
    S = proj.shape[0]
    R = _rows(S, 512)

    def body(b_ref, a_ref, al_ref, dt_ref, dgc_ref, dbe_ref, db_ref, da_ref, dal_ref, ddt_ref):
        @pl.when(pl.program_id(0) == 0)
        def _():
            dal_ref[...] = jnp.zeros_like(dal_ref)
            ddt_ref[...] = jnp.zeros_like(ddt_ref)

        tril = _chunk_tril(R)
        _, vjp = jax.vjp(lambda b, a, al, dt: _gdn_gates(b, a, al, dt, tril), b_ref[...], a_ref[...], al_ref[...], dt_ref[...])
        db, da, dal, ddt = vjp((dgc_ref[...], dbe_ref[...]))
        db_ref[...] = db
        da_ref[...] = da
        dal_ref[...] += dal
        ddt_ref[...] += ddt

    vec = pl.BlockSpec((1, 128), lambda i: (0, 0))
    blk = pl.BlockSpec((R, 128), lambda i: (i, 0))
    return pl.pallas_call(
        body, name=name, grid=(S // R,),
        in_specs=[pl.BlockSpec((R, 128), lambda i: (i, _GDN_B_BLK)), pl.BlockSpec((R, 128), lambda i: (i, _GDN_A_BLK)), vec, vec, blk, blk],
        out_specs=[blk, blk, vec, vec],
        out_shape=[SDS((S, 128), f32), SDS((S, 128), f32), SDS((1, 128), f32), SDS((1, 128), f32)],
        compiler_params=_cp(("arbitrary",)),
    )(proj, proj, A_log, dt_bias, dgc, dbeta)


def _neumann_inv(L):
    n = L.shape[0]
    eye = (lax.broadcasted_iota(jnp.int32, (n, n), 0) == lax.broadcasted_iota(jnp.int32, (n, n), 1)).astype(f32)
    P = eye - L
    M = L
    k = 1
    while 2 * k < n:
        M = _hdot(M, M)
        P = P + _hdot(P, M)
        k *= 2
    return P


@jax.custom_vjp
def _inv_given(L, T):
    return T


def _inv_given_bwd(T, ct):
    dL = -_hdot(_hdot(T, ct, ((0,), (0,))), T, ((1,), (1,)))
    return dL, jnp.zeros_like(T)


_inv_given.defvjp(lambda L, T: (T, T), _inv_given_bwd)


def _gdn_chunk(q, k, v, gcol, bcol, S0, T_given=None):
    C = q.shape[0]
    ii = lax.broadcasted_iota(jnp.int32, (C, C), 0)
    jj = lax.broadcasted_iota(jnp.int32, (C, C), 1)
    grow = jnp.sum(jnp.where(ii == jj, gcol, 0.0), axis=0, keepdims=True)
    dec = jnp.exp(jnp.where(ii >= jj, gcol - grow, NEG))
    last = lax.broadcasted_iota(jnp.int32, (C, 1), 0) == C - 1
    glast = jnp.sum(jnp.where(last, gcol, 0.0), axis=0, keepdims=True)
    eg = jnp.exp(gcol)
    kb = k * bcol
    vb = v * bcol
    L = jnp.where(ii > jj, _mm_nt(kb, k) * dec, 0.0)
    T = _neumann_inv(L) if T_given is None else _inv_given(L, T_given)
    u = _mm(T, vb)
    w = _mm(T, kb * eg)
    attn = _mm_nt(q, k) * dec
    vn = u - _mm(w, S0)
    o = _mm(q * eg, S0) + _mm(attn, vn)
    S1 = S0 * jnp.exp(glast) + _mm_tn(k * jnp.exp(glast - gcol), vn)
    return o, S1, T


def gdn_chunk_fwd(qkvc, gc, beta, name):
    S = qkvc.shape[0]
    NC = S // CHUNK

    def body(q_ref, k_ref, v_ref, gc_ref, be_ref, o_ref, st_ref, T_ref, state):
        @pl.when(pl.program_id(0) == 0)
        def _():
            state[...] = jnp.zeros_like(state)

        gcb = gc_ref[...]
        beb = be_ref[...]
        lane = lax.broadcasted_iota(jnp.int32, (CHUNK, 128), 1)

        def head_pair(hq, c):
            off = pl.multiple_of(hq * HD, HD)
            q = q_ref[:, pl.ds(off, HD)]
            k = k_ref[:, pl.ds(off, HD)]
            for r in range(GDN_V_HEADS // GDN_QK_HEADS):
                hv = hq * 2 + r
                offv = pl.multiple_of(hv * HD, HD)
                gcol = jnp.sum(jnp.where(lane == hv, gcb, 0.0), axis=1, keepdims=True)
                bcol = jnp.sum(jnp.where(lane == hv, beb, 0.0), axis=1, keepdims=True)
                S0 = state[hv]
                o, S1, T = _gdn_chunk(q, k, v_ref[:, pl.ds(offv, HD)], gcol, bcol, S0)
                o_ref[:, pl.ds(offv, HD)] = o
                st_ref[0, hv] = S0
                T_ref[0, hv] = T
                state[hv] = S1
            return c

        lax.fori_loop(0, GDN_QK_HEADS, head_pair, 0)

    return pl.pallas_call(
        body, name=name, grid=(NC,),
        in_specs=[pl.BlockSpec((CHUNK, GDN_QK_W), lambda n: (n, 0)), pl.BlockSpec((CHUNK, GDN_QK_W), lambda n: (n, 1)),
                  pl.BlockSpec((CHUNK, GDN_V_W), lambda n: (n, 1)), pl.BlockSpec((CHUNK, 128), lambda n: (n, 0)),
                  pl.BlockSpec((CHUNK, 128), lambda n: (n, 0))],
        out_specs=[pl.BlockSpec((CHUNK, GDN_V_W), lambda n: (n, 0)),
                   pl.BlockSpec((1, GDN_V_HEADS, HD, HD), lambda n: (n, 0, 0, 0)),
                   pl.BlockSpec((1, GDN_V_HEADS, CHUNK, CHUNK), lambda n: (n, 0, 0, 0))],
        out_shape=[SDS((S, GDN_V_W), f32), SDS((NC, GDN_V_HEADS, HD, HD), f32), SDS((NC, GDN_V_HEADS, CHUNK, CHUNK), f32)],
        scratch_shapes=[pltpu.VMEM((GDN_V_HEADS, HD, HD), f32)],
        compiler_params=_cp(("arbitrary",)),
    )(qkvc, qkvc, qkvc, gc, beta)


def gdn_chunk_bwd(qkvc, gc, beta, states, Ts, do, name):
    S = qkvc.shape[0]
    NC = S // CHUNK

    def body(q_ref, k_ref, v_ref, gc_ref, be_ref, st_ref, T_ref, do_ref, dq_ref, dk_ref, dv_ref, dgc_ref, dbe_ref, dstate):
        @pl.when(pl.program_id(0) == 0)
        def _():
            dstate[...] = jnp.zeros_like(dstate)

        gcb = gc_ref[...]
        beb = be_ref[...]
        lane = lax.broadcasted_iota(jnp.int32, (CHUNK, 128), 1)

        def head_pair(hq, carry):
            dgc_acc, dbe_acc = carry
            off = pl.multiple_of(hq * HD, HD)
            q = q_ref[:, pl.ds(off, HD)]
            k = k_ref[:, pl.ds(off, HD)]
            dq_acc = jnp.zeros((CHUNK, HD), f32)
            dk_acc = jnp.zeros((CHUNK, HD), f32)
            for r in range(GDN_V_HEADS // GDN_QK_HEADS):
                hv = hq * 2 + r
                offv = pl.multiple_of(hv * HD, HD)
                gcol = jnp.sum(jnp.where(lane == hv, gcb, 0.0), axis=1, keepdims=True)
                bcol = jnp.sum(jnp.where(lane == hv, beb, 0.0), axis=1, keepdims=True)
                T = T_ref[0, hv]
                _, vjp = jax.vjp(lambda q_, k_, v_, g_, b_, s_: _gdn_chunk(q_, k_, v_, g_, b_, s_, T)[:2],
                                 q, k, v_ref[:, pl.ds(offv, HD)], gcol, bcol, st_ref[0, hv])
                dq_, dk_, dv_, dg_, db_, dS0 = vjp((do_ref[:, pl.ds(offv, HD)], dstate[hv]))
                dq_acc = dq_acc + dq_
                dk_acc = dk_acc + dk_
                dv_ref[:, pl.ds(offv, HD)] = dv_
                dstate[hv] = dS0
                dgc_acc = dgc_acc + jnp.where(lane == hv, dg_, 0.0)
                dbe_acc = dbe_acc + jnp.where(lane == hv, db_, 0.0)
            dq_ref[:, pl.ds(off, HD)] = dq_acc
            dk_ref[:, pl.ds(off, HD)] = dk_acc
            return dgc_acc, dbe_acc

        z = jnp.zeros((CHUNK, 128), f32)
        dgc, dbe = lax.fori_loop(0, GDN_QK_HEADS, head_pair, (z, z))
        dgc_ref[...] = dgc
        dbe_ref[...] = dbe

    rv = lambda n: NC - 1 - n
    qs = pl.BlockSpec((CHUNK, GDN_QK_W), lambda n: (rv(n), 0))
    ks = pl.BlockSpec((CHUNK, GDN_QK_W), lambda n: (rv(n), 1))
    vs = pl.BlockSpec((CHUNK, GDN_V_W), lambda n: (rv(n), 1))
    g1 = pl.BlockSpec((CHUNK, 128), lambda n: (rv(n), 0))
    v0 = pl.BlockSpec((CHUNK, GDN_V_W), lambda n: (rv(n), 0))
    dq, dk, dv, dgc, dbe = pl.pallas_call(
        body, name=name, grid=(NC,),
        in_specs=[qs, ks, vs, g1, g1,
                  pl.BlockSpec((1, GDN_V_HEADS, HD, HD), lambda n: (rv(n), 0, 0, 0)),
                  pl.BlockSpec((1, GDN_V_HEADS, CHUNK, CHUNK), lambda n: (rv(n), 0, 0, 0)), v0],
        out_specs=[pl.BlockSpec((CHUNK, GDN_QK_W), lambda n: (rv(n), 0)), pl.BlockSpec((CHUNK, GDN_QK_W), lambda n: (rv(n), 0)), v0, g1, g1],
        out_shape=[SDS((S, GDN_QK_W), f32), SDS((S, GDN_QK_W), f32), SDS((S, GDN_V_W), f32), SDS((S, 128), f32), SDS((S, 128), f32)],
        scratch_shapes=[pltpu.VMEM((GDN_V_HEADS, HD, HD), f32)],
        compiler_params=_cp(("arbitrary",)),
    )(qkvc, qkvc, qkvc, gc, beta, states, Ts, do)
    return jnp.concatenate([dq, dk, dv], axis=1), dgc, dbe


def _gated_norm(o, z, nw):
    parts = []
    for h in range(GDN_V_HEADS):
        oh = o[:, h * HD:(h + 1) * HD]
        r = lax.rsqrt(jnp.mean(oh * oh, axis=-1, keepdims=True) + EPS)
        parts.append((oh * r * nw) * _silu(z[:, h * HD:(h + 1) * HD]))
    return jnp.concatenate(parts, axis=1)


def gdn_out_fwd(o, proj, nw, W, x, gate, name):
    S, D = x.shape
    tm = _rows(S, 256)

    def body(o_ref, z_ref, nw_ref, w_ref, x_ref, g_ref, xn_ref, y_ref, og_ref):
        og = _gated_norm(o_ref[...], z_ref[...], nw_ref[...]).astype(bf16)
        y = jnp.dot(og, w_ref[...], preferred_element_type=f32)
        og_ref[...] = og
        y_ref[...] = y
        xn_ref[...] = x_ref[...] + g_ref[...] * y

    row = pl.BlockSpec((tm, D), lambda i: (i, 0))
    wide = pl.BlockSpec((tm, GDN_V_W), lambda i: (i, 0))
    return pl.pallas_call(
        body, name=name, grid=(S // tm,),
        in_specs=[wide, pl.BlockSpec((tm, GDN_V_W), lambda i: (i, 2)), pl.BlockSpec((1, HD), lambda i: (0, 0)),
                  pl.BlockSpec((GDN_V_W, D), lambda i: (0, 0)), row, pl.BlockSpec((1, D), lambda i: (0, 0))],
        out_specs=[row, row, wide],
        out_shape=[SDS((S, D), f32), SDS((S, D), f32), SDS((S, GDN_V_W), bf16)],
        compiler_params=_cp(("arbitrary",)),
    )(o, proj, nw, W, x, gate)


def gdn_out_bwd(dxn, y, gate, o, proj, nw, W, name):
    S, D = dxn.shape
    tm = _rows(S, 256)

    def body(dx_ref, y_ref, g_ref, o_ref, z_ref, nw_ref, w_ref, dy_ref, dg_ref, do_ref, dz_ref, dnw_ref):
        @pl.when(pl.program_id(0) == 0)
        def _():
            dg_ref[...] = jnp.zeros_like(dg_ref)
            dnw_ref[...] = jnp.zeros_like(dnw_ref)

        dx = dx_ref[...]
        dy = dx * g_ref[...]
        dy_ref[...] = dy
        dg_ref[...] += jnp.sum(dx * y_ref[...], axis=0, keepdims=True)
        dog = _nt(dy, w_ref[...])
        _, vjp = jax.vjp(_gated_norm, o_ref[...], z_ref[...], nw_ref[...])
        do, dz, dnw = vjp(dog)
        do_ref[...] = do
        dz_ref[...] = dz
        dnw_ref[...] += dnw

    row = pl.BlockSpec((tm, D), lambda i: (i, 0))
    wide = pl.BlockSpec((tm, GDN_V_W), lambda i: (i, 0))
    vecd = pl.BlockSpec((1, D), lambda i: (0, 0))
    vech = pl.BlockSpec((1, HD), lambda i: (0, 0))
    return pl.pallas_call(
        body, name=name, grid=(S // tm,),
        in_specs=[row, row, vecd, wide, pl.BlockSpec((tm, GDN_V_W), lambda i: (i, 2)), vech, pl.BlockSpec((GDN_V_W, D), lambda i: (0, 0))],
        out_specs=[row, vecd, wide, wide, vech],
        out_shape=[SDS((S, D), f32), SDS((1, D), f32), SDS((S, GDN_V_W), f32), SDS((S, GDN_V_W), f32), SDS((1, HD), f32)],
        compiler_params=_cp(("arbitrary",)),
    )(dxn, y, gate, o, proj, nw, W)


def _rms_w(x, w):
    return (x * lax.rsqrt(jnp.mean(x * x, axis=-1, keepdims=True) + EPS)) * w


def _split3(c):
    hi = c.astype(bf16).astype(f32)
    r1 = c - hi
    mid = r1.astype(bf16).astype(f32)
    lo = (r1 - mid).astype(bf16).astype(f32)
    return hi, mid, lo


_FOX_F_BLK = 4 * FOX_W // 128


def fox_prep_fwd(proj, f_bias, qn_w, kn_w, name):
    S = proj.shape[0]
    tm = _rows(S, 256)

    def body(q_ref, k_ref, v_ref, f_ref, fb_ref, qw_ref, kw_ref, Q_ref, K_ref, V_ref, carry):
        @pl.when(pl.program_id(0) == 0)
        def _():
            carry[...] = jnp.zeros_like(carry)

        ii = lax.broadcasted_iota(jnp.int32, (tm, tm), 0)
        jj = lax.broadcasted_iota(jnp.int32, (tm, tm), 1)
        lf = jax.nn.log_sigmoid(f_ref[...] + fb_ref[...])
        cum = _hdot((ii >= jj).astype(f32), lf) + carry[...]
        carry[...] = cum[tm - 1:tm, :]
        lane = lax.broadcasted_iota(jnp.int32, (tm, FOX_D), 1)
        q, k, v = q_ref[...], k_ref[...], v_ref[...]
        for h in range(FOX_H):
            sl = slice(h * FOX_D, (h + 1) * FOX_D)
            hi, mid, lo = _split3(cum[:, h:h + 1])
            qn = _rms_w(q[:, sl], qw_ref[...]) * FOX_D ** -0.5
            kn = _rms_w(k[:, sl], kw_ref[...])
            eq = jnp.where(lane == 0, hi, jnp.where(lane == 1, mid, jnp.where(lane == 2, lo, jnp.where(lane < 6, 1.0, 0.0))))
            ek = jnp.where(lane < 3, 1.0, jnp.where(lane == 3, -hi, jnp.where(lane == 4, -mid, jnp.where(lane == 5, -lo, 0.0))))
            ev = jnp.where(lane == 0, 1.0, 0.0)
            Q_ref[:, h * 128:(h + 1) * 128] = jnp.concatenate([qn, eq], axis=1).astype(bf16)
            K_ref[:, h * 128:(h + 1) * 128] = jnp.concatenate([kn, ek], axis=1).astype(bf16)
            V_ref[:, h * 128:(h + 1) * 128] = jnp.concatenate([v[:, sl], ev], axis=1).astype(bf16)

    def colblk(c):
        return pl.BlockSpec((tm, FOX_W), lambda i: (i, c))

    pad = pl.BlockSpec((tm, FOX_PW), lambda i: (i, 0))
    return pl.pallas_call(
        body, name=name, grid=(S // tm,),
        in_specs=[colblk(0), colblk(1), colblk(2), pl.BlockSpec((tm, 128), lambda i: (i, _FOX_F_BLK)),
                  pl.BlockSpec((1, 128), lambda i: (0, 0)), pl.BlockSpec((1, FOX_D), lambda i: (0, 0)), pl.BlockSpec((1, FOX_D), lambda i: (0, 0))],
        out_specs=[pad, pad, pad],
        out_shape=[SDS((S, FOX_PW), bf16)] * 3,
        scratch_shapes=[pltpu.VMEM((1, 128), f32)],
        compiler_params=_cp(("arbitrary",)),
    )(proj, proj, proj, proj, f_bias, qn_w, kn_w)


def fox_prep_bwd(proj, f_bias, qn_w, kn_w, dQ, dK, dV, name):
    S = proj.shape[0]
    tm = _rows(S, 256)
    NB = S // tm

    def body(q_ref, k_ref, f_ref, fb_ref, qw_ref, kw_ref, dQ_ref, dK_ref, dV_ref,
             dq_ref, dk_ref, dv_ref, df_ref, dfb_ref, dqw_ref, dkw_ref, carry):
        @pl.when(pl.program_id(0) == 0)
        def _():
            carry[...] = jnp.zeros_like(carry)
            dfb_ref[...] = jnp.zeros_like(dfb_ref)
            dqw_ref[...] = jnp.zeros_like(dqw_ref)
            dkw_ref[...] = jnp.zeros_like(dkw_ref)

        q, k = q_ref[...], k_ref[...]
        lane128 = lax.broadcasted_iota(jnp.int32, (tm, 128), 1)
        dcum = jnp.zeros((tm, 128), f32)
        dqs, dks, dvs = [], [], []
        dqw = jnp.zeros((1, FOX_D), f32)
        dkw = jnp.zeros((1, FOX_D), f32)
        for h in range(FOX_H):
            sl = slice(h * FOX_D, (h + 1) * FOX_D)
            dQh = dQ_ref[:, h * 128:(h + 1) * 128]
            dKh = dK_ref[:, h * 128:(h + 1) * 128]
            _, vq = jax.vjp(lambda a, w: _rms_w(a, w) * FOX_D ** -0.5, q[:, sl], qw_ref[...])
            dqh, dw1 = vq(dQh[:, 0:FOX_D])
            _, vk = jax.vjp(_rms_w, k[:, sl], kw_ref[...])
            dkh, dw2 = vk(dKh[:, 0:FOX_D])
            dqs.append(dqh)
            dks.append(dkh)
            dvs.append(dV_ref[:, h * 128:h * 128 + FOX_D])
            dqw = dqw + dw1
            dkw = dkw + dw2
            dcum = dcum + jnp.where(lane128 == h, dQh[:, FOX_D:FOX_D + 1] - dKh[:, FOX_D + 3:FOX_D + 4], 0.0)
        dq_ref[...] = jnp.concatenate(dqs, axis=1)
        dk_ref[...] = jnp.concatenate(dks, axis=1)
        dv_ref[...] = jnp.concatenate(dvs, axis=1)
        ii = lax.broadcasted_iota(jnp.int32, (tm, tm), 0)
        jj = lax.broadcasted_iota(jnp.int32, (tm, tm), 1)
        dlf = _hdot((ii <= jj).astype(f32), dcum) + carry[...]
        carry[...] += jnp.sum(dcum, axis=0, keepdims=True)
        df = dlf * jax.nn.sigmoid(-(f_ref[...] + fb_ref[...]))
        df_ref[...] = df
        dfb_ref[...] += jnp.sum(df, axis=0, keepdims=True)
        dqw_ref[...] += dqw
        dkw_ref[...] += dkw

    rv = lambda i: NB - 1 - i

    def colblk(c):
        return pl.BlockSpec((tm, FOX_W), lambda i: (rv(i), c))

    pad = pl.BlockSpec((tm, FOX_PW), lambda i: (rv(i), 0))
    cmp_ = pl.BlockSpec((tm, FOX_W), lambda i: (rv(i), 0))
    fblk = pl.BlockSpec((tm, 128), lambda i: (rv(i), 0))
    v128 = pl.BlockSpec((1, 128), lambda i: (0, 0))
    v64 = pl.BlockSpec((1, FOX_D), lambda i: (0, 0))
    return pl.pallas_call(
        body, name=name, grid=(NB,),
        in_specs=[colblk(0), colblk(1), pl.BlockSpec((tm, 128), lambda i: (rv(i), _FOX_F_BLK)), v128, v64, v64, pad, pad, pad],
        out_specs=[cmp_, cmp_, cmp_, fblk, v128, v64, v64],
        out_shape=[SDS((S, FOX_W), f32)] * 3 + [SDS((S, 128), f32), SDS((1, 128), f32), SDS((1, FOX_D), f32), SDS((1, FOX_D), f32)],
        scratch_shapes=[pltpu.VMEM((1, 128), f32)],
        compiler_params=_cp(("arbitrary",)),
    )(proj, proj, proj, f_bias, qn_w, kn_w, dQ, dK, dV)


def _causal(i, j, t):
    row = i * t + lax.broadcasted_iota(jnp.int32, (t, t), 0)
    col = j * t + lax.broadcasted_iota(jnp.int32, (t, t), 1)
    return col <= row


def fox_attn_fwd(Q, K, V, name):
    S = Q.shape[0]
    t = _rows(S, 256)

    def body(q_ref, k_ref, v_ref, o_ref):
        i = pl.program_id(1)
        q = q_ref[...]

        def kv(j, c):
            m, acc = c
            j0 = pl.multiple_of(j * t, t)
            s = _nt(q, k_ref[pl.ds(j0, t), :])
            s = jnp.where(_causal(i, j, t), s, NEG)
            m_new = jnp.maximum(m, jnp.max(s, axis=1, keepdims=True))
            p = jnp.exp(s - m_new)
            acc = acc * jnp.exp(m - m_new) + _nn(p, v_ref[pl.ds(j0, t), :])
            return m_new, acc

        m, acc = lax.fori_loop(0, i + 1, kv, (jnp.full((t, 1), NEG, f32), jnp.zeros((t, 128), f32)))
        l = acc[:, FOX_D:FOX_D + 1]
        lane = lax.broadcasted_iota(jnp.int32, (t, 128), 1)
        o_ref[...] = jnp.where(lane == FOX_D, m + jnp.log(l), acc / l)

    blk = pl.BlockSpec((t, 128), lambda h, i: (i, h))
    seq = pl.BlockSpec((S, 128), lambda h, i: (0, h))
    return pl.pallas_call(
        body, name=name, grid=(FOX_H, S // t),
        in_specs=[blk, seq, seq], out_specs=blk, out_shape=SDS((S, FOX_PW), f32),
        compiler_params=_cp(("arbitrary", "arbitrary")),
    )(Q, K, V)


def fox_attn_bwd_q(Q, K, V, dO, O, name):
    S = Q.shape[0]
    t = _rows(S, 256)

    def body(q_ref, do_ref, o_ref, k_ref, v_ref, dq_ref):
        i = pl.program_id(1)
        q = q_ref[...]
        do = do_ref[...]
        lse = o_ref[:, FOX_D:FOX_D + 1]

        def kv(j, dq):
            j0 = pl.multiple_of(j * t, t)
            k = k_ref[pl.ds(j0, t), :]
            p = jnp.where(_causal(i, j, t), jnp.exp(_nt(q, k) - lse), 0.0)
            ds = p * _nt(do, v_ref[pl.ds(j0, t), :])
            return dq + _nn(ds, k)

        dq_ref[...] = lax.fori_loop(0, i + 1, kv, jnp.zeros((t, 128), f32))

    blk = pl.BlockSpec((t, 128), lambda h, i: (i, h))
    seq = pl.BlockSpec((S, 128), lambda h, i: (0, h))
    return pl.pallas_call(
        body, name=name, grid=(FOX_H, S // t),
        in_specs=[blk, blk, blk, seq, seq], out_specs=blk, out_shape=SDS((S, FOX_PW), f32),
        compiler_params=_cp(("arbitrary", "arbitrary")),
    )(Q, dO, O, K, V)


def fox_attn_bwd_kv(Q, K, V, dO, O, name):
    S = Q.shape[0]
    t = _rows(S, 256)
    nq = S // t

    def body(k_ref, v_ref, q_ref, do_ref, o_ref, dk_ref, dv_ref):
        j = pl.program_id(1)
        k = k_ref[...]
        v = v_ref[...]

        def qb(i, c):
            dk, dv = c
            i0 = pl.multiple_of(i * t, t)
            q = q_ref[pl.ds(i0, t), :]
            do = do_ref[pl.ds(i0, t), :]
            lse = o_ref[pl.ds(i0, t), FOX_D:FOX_D + 1]
            p = jnp.where(_causal(i, j, t), jnp.exp(_nt(q, k) - lse), 0.0)
            dv = dv + _tn(p, do)
            ds = p * _nt(do, v)
            return dk + _tn(ds, q), dv

        z = jnp.zeros((t, 128), f32)
        dk, dv = lax.fori_loop(j, nq, qb, (z, z))
        dk_ref[...] = dk
        dv_ref[...] = dv

    blk = pl.BlockSpec((t, 128), lambda h, j: (j, h))
    seq = pl.BlockSpec((S, 128), lambda h, j: (0, h))
    return pl.pallas_call(
        body, name=name, grid=(FOX_H, nq),
        in_specs=[blk, blk, seq, seq, seq], out_specs=[blk, blk],
        out_shape=[SDS((S, FOX_PW), f32), SDS((S, FOX_PW), f32)],
        compiler_params=_cp(("arbitrary", "arbitrary")),
    )(K, V, Q, dO, O)


def fox_out_fwd(O, proj, W, x, gate, name):
    S, D = x.shape
    tm = _rows(S, 256)

    def body(o_ref, z_ref, w_ref, x_ref, g_ref, xn_ref, y_ref, og_ref):
        z = z_ref[...]
        og = jnp.concatenate([o_ref[:, h * 128:h * 128 + FOX_D] * _silu(z[:, h * FOX_D:(h + 1) * FOX_D]) for h in range(FOX_H)],
                             axis=1).astype(bf16)
        y = jnp.dot(og, w_ref[...], preferred_element_type=f32)
        og_ref[...] = og
        y_ref[...] = y
        xn_ref[...] = x_ref[...] + g_ref[...] * y

    row = pl.BlockSpec((tm, D), lambda i: (i, 0))
    cmp_ = pl.BlockSpec((tm, FOX_W), lambda i: (i, 0))
    return pl.pallas_call(
        body, name=name, grid=(S // tm,),
        in_specs=[pl.BlockSpec((tm, FOX_PW), lambda i: (i, 0)), pl.BlockSpec((tm, FOX_W), lambda i: (i, 3)),
                  pl.BlockSpec((FOX_W, D), lambda i: (0, 0)), row, pl.BlockSpec((1, D), lambda i: (0, 0))],
        out_specs=[row, row, cmp_],
        out_shape=[SDS((S, D), f32), SDS((S, D), f32), SDS((S, FOX_W), bf16)],
        compiler_params=_cp(("arbitrary",)),
    )(O, proj, W, x, gate)


def fox_out_bwd(dxn, y, gate, O, proj, W, name):
    S, D = dxn.shape
    tm = _rows(S, 256)

    def body(dx_ref, y_ref, g_ref, o_ref, z_ref, w_ref, dy_ref, dg_ref, dO_ref, dz_ref):
        @pl.when(pl.program_id(0) == 0)
        def _():
            dg_ref[...] = jnp.zeros_like(dg_ref)

        dx = dx_ref[...]
        dy = dx * g_ref[...]
        dy_ref[...] = dy
        dg_ref[...] += jnp.sum(dx * y_ref[...], axis=0, keepdims=True)
        dog = _nt(dy, w_ref[...])
        z = z_ref[...]
        lane = lax.broadcasted_iota(jnp.int32, (tm, FOX_D), 1)
        dzs = []
        for h in range(FOX_H):
            sl = slice(h * FOX_D, (h + 1) * FOX_D)
            zh = z[:, sl]
            sg = jax.nn.sigmoid(zh)
            oh = o_ref[:, h * 128:h * 128 + FOX_D]
            doh = dog[:, sl] * (zh * sg)
            delta = jnp.sum(doh * oh, axis=1, keepdims=True)
            dO_ref[:, h * 128:(h + 1) * 128] = jnp.concatenate([doh, jnp.where(lane == 0, -delta, 0.0)], axis=1).astype(bf16)
            dzs.append(dog[:, sl] * oh * (sg * (1.0 + zh * (1.0 - sg))))
        dz_ref[...] = jnp.concatenate(dzs, axis=1)

    row = pl.BlockSpec((tm, D), lambda i: (i, 0))
    vecd = pl.BlockSpec((1, D), lambda i: (0, 0))
    pad = pl.BlockSpec((tm, FOX_PW), lambda i: (i, 0))
    return pl.pallas_call(
        body, name=name, grid=(S // tm,),
        in_specs=[row, row, vecd, pad, pl.BlockSpec((tm, FOX_W), lambda i: (i, 3)), pl.BlockSpec((FOX_W, D), lambda i: (0, 0))],
        out_specs=[row, vecd, pad, pl.BlockSpec((tm, FOX_W), lambda i: (i, 0))],
        out_shape=[SDS((S, D), f32), SDS((1, D), f32), SDS((S, FOX_PW), bf16), SDS((S, FOX_W), f32)],
        compiler_params=_cp(("arbitrary",)),
    )(dxn, y, gate, O, proj, W)


def final_loss(x, fw, target, name):
    S, D = x.shape
    tm = _rows(S, 512)

    def body(x_ref, w_ref, t_ref, l_ref, dx_ref, dw_ref):
        @pl.when(pl.program_id(0) == 0)
        def _():
            l_ref[...] = jnp.zeros_like(l_ref)
            dw_ref[...] = jnp.zeros_like(dw_ref)

        out, vjp = jax.vjp(_rms_w, x_ref[...], w_ref[...])
        err = out - t_ref[...]
        l_ref[...] += 0.5 * jnp.sum(jnp.sum(err * err, axis=1, keepdims=True) * (1.0 / D), axis=0, keepdims=True)
        dx, dw = vjp(err * (1.0 / D))
        dx_ref[...] = dx
        dw_ref[...] += dw

    row = pl.BlockSpec((tm, D), lambda i: (i, 0))
    vec = pl.BlockSpec((1, D), lambda i: (0, 0))
    return pl.pallas_call(
        body, name=name, grid=(S // tm,),
        in_specs=[row, vec, row], out_specs=[pl.BlockSpec((1, 128), lambda i: (0, 0)), row, vec],
        out_shape=[SDS((1, 128), f32), SDS((S, D), f32), SDS((1, D), f32)],
        compiler_params=_cp(("arbitrary",)),
    )(x, fw, target)


def ada_fwd(c_all, ada_w, name):
    L, D, n = ada_w.shape

    def body(c_ref, w_ref, o_ref):
        cond = jnp.concatenate([_silu(c_ref[...]), jnp.zeros((8, D), f32)], axis=0)
        o_ref[0] = _nn(cond, w_ref[0])[0:8]

    return pl.pallas_call(
        body, name=name, grid=(L,),
        in_specs=[pl.BlockSpec((NDEV, D), lambda l: (0, 0)), pl.BlockSpec((1, D, n), lambda l: (l, 0, 0))],
        out_specs=pl.BlockSpec((1, NDEV, n), lambda l: (l, 0, 0)),
        out_shape=SDS((L, NDEV, n), f32),
        compiler_params=_cp(("arbitrary",)),
    )(c_all, ada_w)


def ada_grad(c_all, dmod, name):
    L, _, n = dmod.shape
    D = c_all.shape[1]

    def body(c_ref, d_ref, o_ref):
        cond = jnp.concatenate([_silu(c_ref[...]), jnp.zeros((8, D), f32)], axis=0)
        dm = jnp.concatenate([d_ref[0], jnp.zeros((8, n), f32)], axis=0)
        o_ref[0] = _tn(cond, dm)

    return pl.pallas_call(
        body, name=name, grid=(L,),
        in_specs=[pl.BlockSpec((NDEV, D), lambda l: (0, 0)), pl.BlockSpec((1, NDEV, n), lambda l: (l, 0, 0))],
        out_specs=pl.BlockSpec((1, D, n), lambda l: (l, 0, 0)),
        out_shape=SDS((L, D, n), f32),
        compiler_params=_cp(("arbitrary",)),
    )(c_all, dmod)


def reduce_adam(parts, w, m, v, tr, name):
    n, R, C = parts.shape
    c1 = 1.0 / (1.0 - ADAM_B1 ** ADAM_STEP)
    c2 = 1.0 / (1.0 - ADAM_B2 ** ADAM_STEP)

    def body(p_ref, w_ref, m_ref, v_ref, g_ref, d_ref, nm_ref, nv_ref):
        g = p_ref[0].astype(f32)
        for s in range(1, n):
            g = g + p_ref[s].astype(f32)
        nm = ADAM_B1 * m_ref[...] + (1.0 - ADAM_B1) * g
        nv = ADAM_B2 * v_ref[...] + (1.0 - ADAM_B2) * (g * g)
        g_ref[...] = g
        nm_ref[...] = nm
        nv_ref[...] = nv
        d_ref[...] = -ADAM_LR * ((nm * c1) / (jnp.sqrt(nv * c2) + ADAM_EPS) + ADAM_WD * w_ref[...])

    blk = pl.BlockSpec((tr, C), lambda i: (i, 0))
    return pl.pallas_call(
        body, name=name, grid=(R // tr,),
        in_specs=[pl.BlockSpec((n, tr, C), lambda i: (0, i, 0)), blk, blk, blk],
        out_specs=[blk] * 4, out_shape=[SDS((R, C), f32)] * 4,
        compiler_params=_cp(("arbitrary",)),
    )(parts, w, m, v)


def _my_pos():
    return lax.axis_index("x"), lax.axis_index("y"), lax.axis_index("c")


def all_gather(x, name):
    R, C = x.shape

    def body(x_ref, out_ref, send_sems, recv_sems, local_sem):
        x_, y_, c_ = _my_pos()
        me, sibling = (x_, y_, c_), (x_, y_, 1 - c_)
        chips = [(1 - x_, y_), (x_, 1 - y_), (1 - x_, 1 - y_)]

        def rows(px, py, pc):
            return out_ref.at[4 * px + 2 * py + pc]

        def copy(k, block, to, src=None):
            return pltpu.make_async_remote_copy(
                src_ref=rows(*block) if src is None else src, dst_ref=rows(*block),
                send_sem=send_sems.at[k], recv_sem=recv_sems.at[k], device_id=to, device_id_type=pl.DeviceIdType.MESH)

        mine = pltpu.make_async_copy(x_ref, rows(*me), local_sem)
        mine.start()
        first = [copy(0, me, sibling, src=x_ref)]
        first += [copy(1 + j, me, (*chip, c_), src=x_ref) for j, chip in enumerate(chips)]
        for cp in first:
            cp.start()
        passed = [copy(4 + j, (*chip, c_), sibling) for j, chip in enumerate(chips)]
        for j, chip in enumerate(chips):
            copy(1 + j, (*chip, c_), me).wait_recv()
            passed[j].start()
        copy(0, sibling, me).wait_recv()
        for j, chip in enumerate(chips):
            copy(4 + j, (*chip, 1 - c_), me).wait_recv()
        for cp in first + passed:
            cp.wait_send()
        mine.wait()

    return pl.pallas_call(
        body, name=name, out_shape=SDS((NDEV, R, C), x.dtype),
        in_specs=[pl.BlockSpec(memory_space=pl.ANY)], out_specs=pl.BlockSpec(memory_space=pl.ANY),
        scratch_shapes=[pltpu.SemaphoreType.DMA((7,)), pltpu.SemaphoreType.DMA((7,)), pltpu.SemaphoreType.DMA],
    )(x)


def all_to_all(x, name):
    _, R, C = x.shape

    def body(x_ref, out_ref, send_sems, recv_sems, local_sem):
        x_, y_, c_ = _my_pos()
        me = 4 * x_ + 2 * y_ + c_
        local = pltpu.make_async_copy(x_ref.at[me], out_ref.at[me], local_sem)
        local.start()
        copies = []
        for rel in range(1, NDEV):
            px = (x_ + ((rel >> 2) & 1)) % 2
            py = (y_ + ((rel >> 1) & 1)) % 2
            pc = (c_ + (rel & 1)) % 2
            cp = pltpu.make_async_remote_copy(
                src_ref=x_ref.at[4 * px + 2 * py + pc], dst_ref=out_ref.at[me],
                send_sem=send_sems.at[rel - 1], recv_sem=recv_sems.at[rel - 1],
                device_id=(px, py, pc), device_id_type=pl.DeviceIdType.MESH)
            cp.start()
            copies.append(cp)
        for cp in copies:
            cp.wait()
        local.wait()

    return pl.pallas_call(
        body, name=name, out_shape=SDS(x.shape, x.dtype),
        in_specs=[pl.BlockSpec(memory_space=pl.ANY)], out_specs=pl.BlockSpec(memory_space=pl.ANY),
        scratch_shapes=[pltpu.SemaphoreType.DMA((7,)), pltpu.SemaphoreType.DMA((7,)), pltpu.SemaphoreType.DMA],
    )(x)


BIG = ("a_w_in", "a_conv_w", "a_w_out", "b_w_in", "b_w_out")
BIG_AXIS = {"a_w_in": 2, "a_conv_w": 2, "a_w_out": 1, "b_w_in": 2, "b_w_out": 1}
SMALL = ("norm_w", "ada_b", "a_A_log", "a_dt_bias", "a_norm_w", "b_f_bias", "b_qn_w", "b_kn_w", "final_norm_w")
PACK_C = 1024


def _pack(arrs, width):
    flat = jnp.concatenate([a.reshape(-1) for a in arrs])
    pad = (-flat.shape[0]) % (8 * width)
    if pad:
        flat = jnp.concatenate([flat, jnp.zeros((pad,), flat.dtype)])
    return flat.reshape(-1, width)


def _pack_small(arrs):
    rows = []
    for a in arrs:
        fl = a.reshape(-1)
        pad = (-fl.shape[0]) % 128
        if pad:
            fl = jnp.concatenate([fl, jnp.zeros((pad,), fl.dtype)])
        rows.append(fl)
    return _pack(rows, 128)


def _unpack(packed, shapes, align):
    flat = packed.reshape(-1)
    out, off = [], 0
    for shp in shapes:
        n = 1
        for d in shp:
            n *= d
        out.append(flat[off:off + n].reshape(shp))
        off += n + ((-n) % align)
    return out


def _full_from_gathered(g, shard_shape, axis):
    g = jnp.moveaxis(g, 0, axis)
    shp = list(shard_shape)
    shp[axis] *= NDEV
    return g.reshape(shp)


def _blocks_from_full(full, shard_shape, axis):
    shp = list(shard_shape)
    full = full.reshape(shp[:axis] + [NDEV, shp[axis]] + shp[axis + 1:])
    return jnp.moveaxis(full, axis, 0)


def _pad_a_w_in(w):
    D = w.shape[0]
    z = jnp.zeros((D, 112), w.dtype)
    n0 = GDN_CONV + GDN_V_W
    return jnp.concatenate([w[:, :n0], w[:, n0:n0 + 16], z, w[:, n0 + 16:n0 + 32], z], axis=1)


def _unpad_a_w_in(dw):
    n0 = GDN_CONV + GDN_V_W
    return jnp.concatenate([dw[:, :n0], dw[:, n0:n0 + 16], dw[:, n0 + 128:n0 + 144]], axis=1)


def _pad_lanes(v, n=128):
    v = v.reshape(1, -1)
    return jnp.concatenate([v, jnp.zeros((1, n - v.shape[1]), v.dtype)], axis=1)


def gdn_layer_fwd(x, mod, nw, W_in, conv_w, A_log, dt_bias, a_nw, W_out, tag):
    shift, scale, gate = mod
    proj, h = inproj_fwd(x, nw, scale, shift, W_in, GDN_TN, f"{tag}_inproj")
    qkvc = gdn_prep_fwd(proj, conv_w, f"{tag}_prep")
    gc, beta = gdn_gates_fwd(proj, A_log, dt_bias, f"{tag}_gates")
    o, states, Ts = gdn_chunk_fwd(qkvc, gc, beta, f"{tag}_chunk")
    x_new, y, og = gdn_out_fwd(o, proj, a_nw, W_out, x, gate, f"{tag}_out")
    return x_new, (x, proj, h, qkvc, gc, beta, o, states, Ts, y, og)


def gdn_layer_bwd(dxn, saved, mod, nw, W_in, conv_w, A_log, dt_bias, a_nw, W_out, tag):
    shift, scale, gate = mod
    x, proj, h, qkvc, gc, beta, o, states, Ts, y, og = saved
    dy, dgate, do, dz, da_nw = gdn_out_bwd(dxn, y, gate, o, proj, a_nw, W_out, f"{tag}_out_bwd")
    dW_out = matmul_tn(og, dy, 512, f"{tag}_dwout")
    dqkvc, dgc, dbeta = gdn_chunk_bwd(qkvc, gc, beta, states, Ts, do, f"{tag}_chunk_bwd")
    db, da, dA_log, ddt = gdn_gates_bwd(proj, A_log, dt_bias, dgc, dbeta, f"{tag}_gates_bwd")
    dqkv, dconv_w = gdn_prep_bwd(proj, conv_w, dqkvc, f"{tag}_prep_bwd")
    dproj = jnp.concatenate([dqkv, dz, db, da], axis=1)
    dW_in = matmul_tn(h, dproj, GDN_TN, f"{tag}_dwin")
    dx, dnw, dscale, dshift = inproj_bwd_x(x, nw, scale, shift, W_in, dproj, dxn, GDN_TN, f"{tag}_inproj_bwd")
    grads = dict(norm_w=dnw, W_in=dW_in, conv_w=dconv_w, A_log=dA_log[:, :16], dt_bias=ddt[:, :16], a_nw=da_nw, W_out=dW_out,
                 dmod=jnp.concatenate([dshift, dscale, dgate], axis=1))
    return dx, grads


def fox_layer_fwd(x, mod, nw, W_in, f_bias, qn_w, kn_w, W_out, tag):
    shift, scale, gate = mod
    proj, h = inproj_fwd(x, nw, scale, shift, W_in, FOX_TN, f"{tag}_inproj")
    Q, K, V = fox_prep_fwd(proj, f_bias, qn_w, kn_w, f"{tag}_prep")
    O = fox_attn_fwd(Q, K, V, f"{tag}_attn")
    x_new, y, og = fox_out_fwd(O, proj, W_out, x, gate, f"{tag}_out")
    return x_new, (x, proj, h, Q, K, V, O, y, og)


def fox_layer_bwd(dxn, saved, mod, nw, W_in, f_bias, qn_w, kn_w, W_out, tag):
    shift, scale, gate = mod
    x, proj, h, Q, K, V, O, y, og = saved
    dy, dgate, dO, dz = fox_out_bwd(dxn, y, gate, O, proj, W_out, f"{tag}_out_bwd")
    dW_out = matmul_tn(og, dy, 512, f"{tag}_dwout")
    dQ = fox_attn_bwd_q(Q, K, V, dO, O, f"{tag}_attn_bwd_q")
    dK, dV = fox_attn_bwd_kv(Q, K, V, dO, O, f"{tag}_attn_bwd_kv")
    dq, dk, dv, df, dfb, dqw, dkw = fox_prep_bwd(proj, f_bias, qn_w, kn_w, dQ, dK, dV, f"{tag}_prep_bwd")
    dproj = jnp.concatenate([dq, dk, dv, dz, df], axis=1)
    dW_in = matmul_tn(h, dproj, FOX_TN, f"{tag}_dwin")
    dx, dnw, dscale, dshift = inproj_bwd_x(x, nw, scale, shift, W_in, dproj, dxn, FOX_TN, f"{tag}_inproj_bwd")
    grads = dict(norm_w=dnw, W_in=dW_in, f_bias=dfb[:, :16], qn_w=dqw, kn_w=dkw, W_out=dW_out,
                 dmod=jnp.concatenate([dshift, dscale, dgate], axis=1))
    return dx, grads


def device_step(x, mod_all, norm_w, full, small, final_norm_w, target):
    D = x.shape[1]
    mods = [(mod_all[i:i + 1, 0:D], mod_all[i:i + 1, D:2 * D], mod_all[i:i + 1, 2 * D:3 * D]) for i in range(4)]

    def a_args(j):
        return (_pad_a_w_in(full["a_w_in"][j]), full["a_conv_w"][j], _pad_lanes(small["a_A_log"][j]), _pad_lanes(small["a_dt_bias"][j]),
                small["a_norm_w"][j:j + 1], full["a_w_out"][j])

    def b_args(j):
        w = full["b_w_in"][j]
        w = jnp.concatenate([w, jnp.zeros((D, FOX_IN_PAD - FOX_IN), w.dtype)], axis=1)
        return (w, _pad_lanes(small["b_f_bias"][j]), small["b_qn_w"][j:j + 1], small["b_kn_w"][j:j + 1], full["b_w_out"][j])

    saved = []
    for i in range(4):
        j = i // 2
        if i % 2 == 0:
            x, sv = gdn_layer_fwd(x, mods[i], norm_w[i:i + 1], *a_args(j), tag=f"L{i}")
        else:
            x, sv = fox_layer_fwd(x, mods[i], norm_w[i:i + 1], *b_args(j), tag=f"L{i}")
        saved.append(sv)
    loss, dx, dfw = final_loss(x, final_norm_w.reshape(1, D), target, "final_loss")
    lg = [None] * 4
    for i in reversed(range(4)):
        j = i // 2
        if i % 2 == 0:
            dx, lg[i] = gdn_layer_bwd(dx, saved[i], mods[i], norm_w[i:i + 1], *a_args(j), tag=f"L{i}")
        else:
            dx, lg[i] = fox_layer_bwd(dx, saved[i], mods[i], norm_w[i:i + 1], *b_args(j), tag=f"L{i}")
    g = dict(
        norm_w=jnp.concatenate([lg[i]["norm_w"] for i in range(4)], axis=0),
        dmod=jnp.concatenate([lg[i]["dmod"] for i in range(4)], axis=0),
        a_w_in=jnp.stack([_unpad_a_w_in(lg[i]["W_in"]) for i in (0, 2)]),
        a_conv_w=jnp.stack([lg[i]["conv_w"] for i in (0, 2)]),
        a_A_log=jnp.concatenate([lg[i]["A_log"] for i in (0, 2)], axis=0),
        a_dt_bias=jnp.concatenate([lg[i]["dt_bias"] for i in (0, 2)], axis=0),
        a_norm_w=jnp.concatenate([lg[i]["a_nw"] for i in (0, 2)], axis=0),
        a_w_out=jnp.stack([lg[i]["W_out"] for i in (0, 2)]),
        b_w_in=jnp.stack([lg[i]["W_in"][:, :FOX_IN] for i in (1, 3)]),
        b_f_bias=jnp.concatenate([lg[i]["f_bias"] for i in (1, 3)], axis=0),
        b_qn_w=jnp.concatenate([lg[i]["qn_w"] for i in (1, 3)], axis=0),
        b_kn_w=jnp.concatenate([lg[i]["kn_w"] for i in (1, 3)], axis=0),
        b_w_out=jnp.stack([lg[i]["W_out"] for i in (1, 3)]),
        final_norm_w=dfw.reshape(-1),
    )
    return loss[0, 0], dx, g


def kernel(x, c, norm_w, ada_w, ada_b, a_w_in, a_conv_w, a_A_log, a_dt_bias, a_norm_w, a_w_out, b_w_in, b_f_bias, b_qn_w, b_kn_w, b_w_out, final_norm_w, loss_target, m_norm_w, m_ada_w, m_ada_b, m_a_w_in, m_a_conv_w, m_a_A_log, m_a_dt_bias, m_a_norm_w, m_a_w_out, m_b_w_in, m_b_f_bias, m_b_qn_w, m_b_kn_w, m_b_w_out, m_final_norm_w, v_norm_w, v_ada_w, v_ada_b, v_a_w_in, v_a_conv_w, v_a_A_log, v_a_dt_bias, v_a_norm_w, v_a_w_out, v_b_w_in, v_b_f_bias, v_b_qn_w, v_b_kn_w, v_b_w_out, v_final_norm_w):
    W = dict(norm_w=norm_w, ada_w=ada_w, ada_b=ada_b, a_w_in=a_w_in, a_conv_w=a_conv_w, a_A_log=a_A_log, a_dt_bias=a_dt_bias,
             a_norm_w=a_norm_w, a_w_out=a_w_out, b_w_in=b_w_in, b_f_bias=b_f_bias, b_qn_w=b_qn_w, b_kn_w=b_kn_w, b_w_out=b_w_out,
             final_norm_w=final_norm_w)
    M = dict(norm_w=m_norm_w, ada_w=m_ada_w, ada_b=m_ada_b, a_w_in=m_a_w_in, a_conv_w=m_a_conv_w, a_A_log=m_a_A_log,
             a_dt_bias=m_a_dt_bias, a_norm_w=m_a_norm_w, a_w_out=m_a_w_out, b_w_in=m_b_w_in, b_f_bias=m_b_f_bias, b_qn_w=m_b_qn_w,
             b_kn_w=m_b_kn_w, b_w_out=m_b_w_out, final_norm_w=m_final_norm_w)
    V = dict(norm_w=v_norm_w, ada_w=v_ada_w, ada_b=v_ada_b, a_w_in=v_a_w_in, a_conv_w=v_a_conv_w, a_A_log=v_a_A_log,
             a_dt_bias=v_a_dt_bias, a_norm_w=v_a_norm_w, a_w_out=v_a_w_out, b_w_in=v_b_w_in, b_f_bias=v_b_f_bias, b_qn_w=v_b_qn_w,
             b_kn_w=v_b_kn_w, b_w_out=v_b_w_out, final_norm_w=v_final_norm_w)
    S, D = x.shape[1], x.shape[2]
    me = 4 * lax.axis_index("x") + 2 * lax.axis_index("y") + lax.axis_index("c")
    big_shapes = [W[n].shape for n in BIG]
    small_shapes = [W[n].shape for n in SMALL]

    c_all = all_gather(c.reshape(8, D // 8), "gather_c").reshape(NDEV, D)
    mod_part = ada_fwd(c_all, ada_w, "ada_fwd")
    n_ada = ada_w.shape[2]
    mod_g = all_gather(mod_part.reshape(4 * NDEV, n_ada), "gather_mod").reshape(NDEV, 4, NDEV, n_ada)
    mod_mine = lax.dynamic_index_in_dim(mod_g, me, axis=2, keepdims=False)
    mod_all = jnp.moveaxis(mod_mine, 0, 1).reshape(4, NDEV * n_ada) + ada_b

    gathered = all_gather(_pack([W[n].astype(bf16) for n in BIG], PACK_C), "gather_w")
    flat = gathered.reshape(NDEV, -1)
    full, off = {}, 0
    for n, shp in zip(BIG, big_shapes):
        cnt = 1
        for d in shp:
            cnt *= d
        full[n] = _full_from_gathered(flat[:, off:off + cnt].reshape((NDEV,) + shp), shp, BIG_AXIS[n])
        off += cnt
    conv_g = all_gather(a_conv_w.reshape(8, -1), "gather_conv").reshape((NDEV,) + a_conv_w.shape)
    full["a_conv_w"] = _full_from_gathered(conv_g, a_conv_w.shape, BIG_AXIS["a_conv_w"])

    loss, dx, g = device_step(x[0], mod_all, norm_w, full, W, final_norm_w, loss_target[0])
    loss = lax.psum(loss, MESH_AXES)

    g_small = dict(g, ada_b=g["dmod"])
    sp = _pack_small([g_small[n] for n in SMALL])
    sp_all = all_gather(sp, "gather_small")
    sw, sm, sv = (_pack_small([T[n] for n in SMALL]) for T in (W, M, V))
    sg, sd, snm, snv = (_unpack(t, small_shapes, 128) for t in reduce_adam(sp_all, sw, sm, sv, sp.shape[0], "adam_small"))

    off_b = 0
    for n, shp in zip(SMALL, small_shapes):
        if n == "ada_b":
            break
        cnt = 1
        for d in shp:
            cnt *= d
        off_b += cnt + ((-cnt) % 128)
    dmod_all = sp_all.reshape(NDEV, -1)[:, off_b:off_b + 4 * 3 * D].reshape(NDEV, 4, 3 * D)
    dmod_cols = lax.dynamic_slice_in_dim(dmod_all, me * n_ada, n_ada, axis=2)
    g_ada = ada_grad(c_all, jnp.moveaxis(dmod_cols, 0, 1), "ada_grad")
    r_ada = reduce_adam(g_ada.reshape(1, 4 * D, n_ada), *(T["ada_w"].reshape(4 * D, n_ada) for T in (W, M, V)), 512, "adam_ada")
    ag, ad, anm, anv = (t.reshape(ada_w.shape) for t in r_ada)

    blocks = jnp.concatenate([_blocks_from_full(g[n], W[n].shape, BIG_AXIS[n]).reshape(NDEV, -1) for n in BIG], axis=1)
    recv = all_to_all(blocks.astype(bf16).reshape(NDEV, -1, PACK_C), "scatter_grads")
    bw, bm, bv = (_pack([T[n] for n in BIG], PACK_C) for T in (W, M, V))
    bg, bd, bnm, bnv = (_unpack(t, big_shapes, 1) for t in reduce_adam(recv, bw, bm, bv, 176, "adam_big"))

    outs = {}
    for k, (sm_l, big_l, ada_t) in dict(grad=(sg, bg, ag), delta=(sd, bd, ad), new_m=(snm, bnm, anm), new_v=(snv, bnv, anv)).items():
        d = dict(zip(SMALL, sm_l))
        d.update(zip(BIG, big_l))
        d["ada_w"] = ada_t
        outs[k] = d
    order = ("norm_w", "ada_w", "ada_b", "a_w_in", "a_conv_w", "a_A_log", "a_dt_bias", "a_norm_w", "a_w_out", "b_w_in", "b_f_bias",
             "b_qn_w", "b_kn_w", "b_w_out", "final_norm_w")
    return (loss, dx[None], *[outs["grad"][n] for n in order], *[outs["delta"][n] for n in order],
            *[outs["new_m"][n] for n in order], *[outs["new_v"][n] for n in order])
```

```python
import functools

import jax
import jax.numpy as jnp
from jax import lax
from jax.experimental import pallas as pl
from jax.experimental.pallas import tpu as pltpu

f32 = jnp.float32
bf16 = jnp.bfloat16
SDS = jax.ShapeDtypeStruct

EPS = 1e-6
CHUNK = 64
HD = 128
GDN_QK_HEADS = 8
GDN_V_HEADS = 16
GDN_QK_W = GDN_QK_HEADS * HD
GDN_V_W = GDN_V_HEADS * HD
GDN_CONV = 2 * GDN_QK_W + GDN_V_W
GDN_IN = GDN_CONV + GDN_V_W + 2 * GDN_V_HEADS
GDN_IN_PAD = GDN_CONV + GDN_V_W + 256
GDN_TN = 640
FOX_H = 16
FOX_D = 64
FOX_W = FOX_H * FOX_D
FOX_IN = 4 * FOX_W + FOX_H
FOX_IN_PAD = 4 * FOX_W + 128
FOX_TN = 1408
FOX_PW = FOX_H * 128
NDEV = 8
MESH_AXES = ("x", "y", "c")
NEG = -1e30

ADAM_LR = 0.001
ADAM_B1 = 0.9
ADAM_B2 = 0.999
ADAM_EPS = 1e-08
ADAM_WD = 0.01
ADAM_STEP = 10

VMEM_LIMIT = 56 * 1024 * 1024


def _cp(sem=None):
    return pltpu.CompilerParams(dimension_semantics=sem, vmem_limit_bytes=VMEM_LIMIT)


def _bdot(a, b, dims):
    return lax.dot_general(a.astype(bf16), b.astype(bf16), (dims, ((), ())), preferred_element_type=f32)


def _nn(a, b):
    return _bdot(a, b, ((1,), (0,)))


def _nt(a, b):
    return _bdot(a, b, ((1,), (1,)))


def _tn(a, b):
    return _bdot(a, b, ((0,), (0,)))


def _hdot(a, b, dims=((1,), (0,))):
    return lax.dot_general(a, b, (dims, ((), ())), precision=lax.Precision.HIGHEST, preferred_element_type=f32)


def _split2(a):
    hi = a.astype(bf16)
    return hi, (a - hi.astype(f32)).astype(bf16)


def _tdot(a, b, dims=((1,), (0,))):
    ah, al = _split2(a)
    bh, bl = _split2(b)
    dn = (dims, ((), ()))
    return (lax.dot_general(ah, bh, dn, preferred_element_type=f32) + lax.dot_general(ah, bl, dn, preferred_element_type=f32)
            + lax.dot_general(al, bh, dn, preferred_element_type=f32))


@jax.custom_vjp
def _mm(a, b):
    return _nn(a, b)


_mm.defvjp(lambda a, b: (_nn(a, b), (a, b)), lambda r, g: (_nt(g, r[1]), _tn(r[0], g)))


@jax.custom_vjp
def _mm_nt(a, b):
    return _nt(a, b)


_mm_nt.defvjp(lambda a, b: (_nt(a, b), (a, b)), lambda r, g: (_nn(g, r[1]), _tn(g, r[0])))


@jax.custom_vjp
def _mm_tn(a, b):
    return _tn(a, b)


_mm_tn.defvjp(lambda a, b: (_tn(a, b), (a, b)), lambda r, g: (_nt(r[1], g), _nn(r[0], g)))


def _silu(x):
    return x * jax.nn.sigmoid(x)


def _rms_mod(x, nw, scale, shift):
    r = lax.rsqrt(jnp.mean(x * x, axis=-1, keepdims=True) + EPS)
    return (x * r * nw) * (1.0 + scale) + shift


def _rows(S, want):
    return min(want, S)


def inproj_fwd(x, nw, scale, shift, W, tn, name):
    S, D = x.shape
    N = W.shape[1]
    tm = _rows(S, 512)

    def body(x_ref, nw_ref, sc_ref, sh_ref, w_ref, proj_ref, h_ref):
        @pl.when(pl.program_id(1) == 0)
        def _():
            h_ref[...] = _rms_mod(x_ref[...], nw_ref[...], sc_ref[...], sh_ref[...]).astype(bf16)

        proj_ref[...] = jnp.dot(h_ref[...], w_ref[...], preferred_element_type=f32)

    vec = pl.BlockSpec((1, D), lambda i, j: (0, 0))
    return pl.pallas_call(
        body, name=name, grid=(S // tm, N // tn),
        in_specs=[pl.BlockSpec((tm, D), lambda i, j: (i, 0)), vec, vec, vec, pl.BlockSpec((D, tn), lambda i, j: (0, j))],
        out_specs=[pl.BlockSpec((tm, tn), lambda i, j: (i, j)), pl.BlockSpec((tm, D), lambda i, j: (i, 0))],
        out_shape=[SDS((S, N), f32), SDS((S, D), bf16)],
        compiler_params=_cp(("arbitrary", "arbitrary")),
    )(x, nw, scale, shift, W)


def inproj_bwd_x(x, nw, scale, shift, W, dproj, dx_res, tn, name):
    S, D = x.shape
    N = W.shape[1]
    tm = _rows(S, 512)
    nj = N // tn

    def body(x_ref, nw_ref, sc_ref, sh_ref, w_ref, dp_ref, dxr_ref, dx_ref, dnw_ref, dsc_ref, dsh_ref, acc):
        i, j = pl.program_id(0), pl.program_id(1)

        @pl.when(j == 0)
        def _():
            acc[...] = jnp.zeros_like(acc)

        @pl.when((i == 0) & (j == 0))
        def _():
            dnw_ref[...] = jnp.zeros_like(dnw_ref)
            dsc_ref[...] = jnp.zeros_like(dsc_ref)
            dsh_ref[...] = jnp.zeros_like(dsh_ref)

        acc[...] += _nt(dp_ref[...], w_ref[...])

        @pl.when(j == nj - 1)
        def _():
            _, vjp = jax.vjp(_rms_mod, x_ref[...], nw_ref[...], sc_ref[...], sh_ref[...])
            dx, dnw, dsc, dsh = vjp(acc[...])
            dx_ref[...] = dxr_ref[...] + dx
            dnw_ref[...] += dnw
            dsc_ref[...] += dsc
            dsh_ref[...] += dsh

    vec = pl.BlockSpec((1, D), lambda i, j: (0, 0))
    row = pl.BlockSpec((tm, D), lambda i, j: (i, 0))
    return pl.pallas_call(
        body, name=name, grid=(S // tm, nj),
        in_specs=[row, vec, vec, vec, pl.BlockSpec((D, tn), lambda i, j: (0, j)), pl.BlockSpec((tm, tn), lambda i, j: (i, j)), row],
        out_specs=[row, vec, vec, vec],
        out_shape=[SDS((S, D), f32), SDS((1, D), f32), SDS((1, D), f32), SDS((1, D), f32)],
        scratch_shapes=[pltpu.VMEM((tm, D), f32)],
        compiler_params=_cp(("arbitrary", "arbitrary")),
    )(x, nw, scale, shift, W, dproj, dx_res)


def matmul_tn(a, b, tn, name):
    S, K = a.shape
    N = b.shape[1]
    tm = _rows(S, 512)
    ni = S // tm

    def body(a_ref, b_ref, o_ref):
        @pl.when(pl.program_id(1) == 0)
        def _():
            o_ref[...] = jnp.zeros_like(o_ref)

        o_ref[...] += _tn(a_ref[...], b_ref[...])

    return pl.pallas_call(
        body, name=name, grid=(N // tn, ni),
        in_specs=[pl.BlockSpec((tm, K), lambda j, i: (i, 0)), pl.BlockSpec((tm, tn), lambda j, i: (i, j))],
        out_specs=pl.BlockSpec((K, tn), lambda j, i: (0, j)),
        out_shape=SDS((K, N), f32),
        compiler_params=_cp(("arbitrary", "arbitrary")),
    )(a, b)


def _conv_taps(xs, w, n_out):
    taps = []
    for j in range(4):
        s = 3 - j
        sh = xs if s == 0 else pltpu.roll(xs, s, axis=0)
        taps.append(sh[8:8 + n_out])
    conv = taps[0] * w[0] + taps[1] * w[1] + taps[2] * w[2] + taps[3] * w[3]
    return taps, conv


def _act_norm(conv, mul_norm, mul_plain):
    s = _silu(conv)
    r = lax.rsqrt(jnp.sum(s * s, axis=-1, keepdims=True) + EPS)
    return s * (mul_norm * r + mul_plain)


def _gdn_prep_mults(j):
    is_q = j < GDN_QK_HEADS
    is_k = (j >= GDN_QK_HEADS) & (j < 2 * GDN_QK_HEADS)
    mul_norm = jnp.where(is_q, HD ** -0.5, jnp.where(is_k, 1.0, 0.0)).astype(f32)
    mul_plain = jnp.where(is_q | is_k, 0.0, 1.0).astype(f32)
    return mul_norm, mul_plain


def gdn_prep_fwd(proj, conv_w, name):
    S = proj.shape[0]
    R = _rows(S, 512)

    def body(x_ref, w_ref, o_ref):
        mul_norm, mul_plain = _gdn_prep_mults(pl.program_id(0))
        w = [w_ref[j:j + 1, :] for j in range(4)]

        def piece(r, c):
            t0 = pl.multiple_of(r * R, R)
            cur = x_ref[pl.ds(t0, R), :]
            prev = x_ref[pl.ds(pl.multiple_of(jnp.maximum(t0 - 8, 0), 8), 8), :]
            prev = jnp.where(r == 0, 0.0, prev)
            _, conv = _conv_taps(jnp.concatenate([prev, cur], axis=0), w, R)
            o_ref[pl.ds(t0, R), :] = _act_norm(conv, mul_norm, mul_plain)
            return c

        lax.fori_loop(0, S // R, piece, 0)

    return pl.pallas_call(
        body, name=name, grid=(GDN_CONV // 128,),
        in_specs=[pl.BlockSpec((S, 128), lambda j: (0, j)), pl.BlockSpec((4, 128), lambda j: (0, j))],
        out_specs=pl.BlockSpec((S, 128), lambda j: (0, j)),
        out_shape=SDS((S, GDN_CONV), f32),
        compiler_params=_cp(("arbitrary",)),
    )(proj, conv_w)


def gdn_prep_bwd(proj, conv_w, dqkvc, name):
    S = proj.shape[0]
    R = _rows(S, 512)
    NP = S // R

    def body(x_ref, w_ref, dn_ref, dx_ref, dw_ref):
        mul_norm, mul_plain = _gdn_prep_mults(pl.program_id(0))
        w = [w_ref[j:j + 1, :] for j in range(4)]

        def piece(r, dw):
            t0 = pl.multiple_of(r * R, R)
            cur = x_ref[pl.ds(t0, R), :]
            prev = x_ref[pl.ds(pl.multiple_of(jnp.maximum(t0 - 8, 0), 8), 8), :]
            prev = jnp.where(r == 0, 0.0, prev)
            nxt0 = pl.multiple_of(jnp.minimum(t0 + R, S - 8), 8)
            nxt = x_ref[pl.ds(nxt0, 8), :]
            dn_cur = dn_ref[pl.ds(t0, R), :]
            dn_nxt = jnp.where(r == NP - 1, 0.0, dn_ref[pl.ds(nxt0, 8), :])
            xs = jnp.concatenate([prev, cur, nxt], axis=0)
            taps, conv = _conv_taps(xs, w, R + 8)
            dn = jnp.concatenate([dn_cur, dn_nxt], axis=0)
            _, vjp = jax.vjp(lambda c: _act_norm(c, mul_norm, mul_plain), conv)
            dxc = vjp(dn)[0]
            n = R + 8
            dx = dxc[0:R] * w[3]
            for j in range(3):
                s = 3 - j
                dx = dx + pltpu.roll(dxc, n - s, axis=0)[0:R] * w[j]
            dx_ref[pl.ds(t0, R), :] = dx
            return tuple(dw[j] + jnp.sum(dxc[0:R] * taps[j][0:R], axis=0, keepdims=True) for j in range(4))

        dw = lax.fori_loop(0, NP, piece, tuple(jnp.zeros((1, 128), f32) for _ in range(4)))
        for j in range(4):
            dw_ref[j:j + 1, :] = dw[j]

    col = pl.BlockSpec((S, 128), lambda j: (0, j))
    wsp = pl.BlockSpec((4, 128), lambda j: (0, j))
    return pl.pallas_call(
        body, name=name, grid=(GDN_CONV // 128,),
        in_specs=[col, wsp, col], out_specs=[col, wsp],
        out_shape=[SDS((S, GDN_CONV), f32), SDS((4, GDN_CONV), f32)],
        compiler_params=_cp(("arbitrary",)),
    )(proj, conv_w, dqkvc)


def _chunk_tril(R):
    ii = lax.broadcasted_iota(jnp.int32, (R, R), 0)
    jj = lax.broadcasted_iota(jnp.int32, (R, R), 1)
    return ((ii // CHUNK == jj // CHUNK) & (ii >= jj)).astype(f32)


def _gdn_gates(b, a, A_log, dt_bias, tril):
    beta = jax.nn.sigmoid(b)
    g = -jnp.exp(A_log) * jax.nn.softplus(a + dt_bias)
    return _hdot(tril, g), beta


_GDN_B_BLK = (GDN_CONV + GDN_V_W) // 128
_GDN_A_BLK = _GDN_B_BLK + 1


def gdn_gates_fwd(proj, A_log, dt_bias, name):
    S = proj.shape[0]
    R = _rows(S, 512)

    def body(b_ref, a_ref, al_ref, dt_ref, gc_ref, be_ref):
        gc, be = _gdn_gates(b_ref[...], a_ref[...], al_ref[...], dt_ref[...], _chunk_tril(R))
        gc_ref[...] = gc
        be_ref[...] = be

    vec = pl.BlockSpec((1, 128), lambda i: (0, 0))
    blk = pl.BlockSpec((R, 128), lambda i: (i, 0))
    return pl.pallas_call(
        body, name=name, grid=(S // R,),
        in_specs=[pl.BlockSpec((R, 128), lambda i: (i, _GDN_B_BLK)), pl.BlockSpec((R, 128), lambda i: (i, _GDN_A_BLK)), vec, vec],
        out_specs=[blk, blk], out_shape=[SDS((S, 128), f32), SDS((S, 128), f32)],
        compiler_params=_cp(("arbitrary",)),
    )(proj, proj, A_log, dt_bias)


def gdn_gates_bwd(proj, A_log, dt_bias, dgc, dbeta, name):
    S = proj.shape[0]
    R = _rows(S, 512)

    def body(b_ref, a_ref, al_ref, dt_ref, dgc_ref, dbe_ref, db_ref, da_ref, dal_ref, ddt_ref):
        @pl.when(pl.program_id(0) == 0)
        def _():
            dal_ref[...] = jnp.zeros_like(dal_ref)
            ddt_ref[...] = jnp.zeros_like(ddt_ref)

        tril = _chunk_tril(R)
        _, vjp = jax.vjp(lambda b, a, al, dt: _gdn_gates(b, a, al, dt, tril), b_ref[...], a_ref[...], al_ref[...], dt_ref[...])
        db, da, dal, ddt = vjp((dgc_ref[...], dbe_ref[...]))
        db_ref[...] = db
        da_ref[...] = da
        dal_ref[...] += dal
        ddt_ref[...] += ddt

    vec = pl.BlockSpec((1, 128), lambda i: (0, 0))
    blk = pl.BlockSpec((R, 128), lambda i: (i, 0))
    return pl.pallas_call(
        body, name=name, grid=(S // R,),
        in_specs=[pl.BlockSpec((R, 128), lambda i: (i, _GDN_B_BLK)), pl.BlockSpec((R, 128), lambda i: (i, _GDN_A_BLK)), vec, vec, blk, blk],
        out_specs=[blk, blk, vec, vec],
        out_shape=[SDS((S, 128), f32), SDS((S, 128), f32), SDS((1, 128), f32), SDS((1, 128), f32)],
        compiler_params=_cp(("arbitrary",)),
    )(proj, proj, A_log, dt_bias, dgc, dbeta)


@jax.custom_vjp
def _inv_given(L, T):
    return T


def _inv_given_bwd(T, ct):
    dL = -_nt(_tn(T, ct), T)
    return dL, jnp.zeros_like(T)


_inv_given.defvjp(lambda L, T: (T, T), _inv_given_bwd)


def _gdn_intra_pre(k, gcol, bcol):
    C = k.shape[0]
    ii = lax.broadcasted_iota(jnp.int32, (C, C), 0)
    jj = lax.broadcasted_iota(jnp.int32, (C, C), 1)
    grow = jnp.sum(jnp.where(ii == jj, gcol, 0.0), axis=0, keepdims=True)
    dec = jnp.exp(jnp.where(ii >= jj, gcol - grow, NEG))
    kb = k * bcol
    return dec, kb, jnp.where(ii > jj, _mm_nt(kb, k) * dec, 0.0)


def _gdn_intra_post(q, k, v, gcol, bcol, dec, kb, T):
    return _mm(T, v * bcol), _mm(T, kb * jnp.exp(gcol)), _mm_nt(q, k) * dec


def _gdn_intra(q, k, v, gcol, bcol, T_given):
    dec, kb, L = _gdn_intra_pre(k, gcol, bcol)
    return _gdn_intra_post(q, k, v, gcol, bcol, dec, kb, _inv_given(L, T_given))


def _neumann_inv_batched(Ls):
    n, C = len(Ls), Ls[0].shape[0]
    r0 = lax.broadcasted_iota(jnp.int32, (n * C, n * C), 0)
    c0 = lax.broadcasted_iota(jnp.int32, (n * C, n * C), 1)
    same = (r0 // C) == (c0 // C)

    def blockdiag(xcat):
        return jnp.where(same, jnp.concatenate([xcat] * n, axis=0), 0.0)

    M = jnp.concatenate(Ls, axis=1)
    eye = (lax.broadcasted_iota(jnp.int32, (C, n * C), 0) == (lax.broadcasted_iota(jnp.int32, (C, n * C), 1) & (C - 1))).astype(f32)
    P = eye - M
    k = 1
    while 2 * k < C:
        M = _tdot(M, blockdiag(M))
        P = P + _tdot(P, blockdiag(M))
        k *= 2
    return [P[:, h * C:(h + 1) * C] for h in range(n)]


def _gdn_scan(q, k, gcol, u, w, attn, S0):
    C = q.shape[0]
    last = lax.broadcasted_iota(jnp.int32, (C, 1), 0) == C - 1
    glast = jnp.sum(jnp.where(last, gcol, 0.0), axis=0, keepdims=True)
    vn = u - _mm(w, S0)
    o = _mm(q * jnp.exp(gcol), S0) + _mm(attn, vn)
    S1 = S0 * jnp.exp(glast) + _mm_tn(k * jnp.exp(glast - gcol), vn)
    return o, S1


GDN_HG = 4


def _head_col(blk, lane, hv):
    return jnp.sum(jnp.where(lane == hv, blk, 0.0), axis=1, keepdims=True)


def _gdn_specs(NC, rv=None):
    ix = (lambda n: n) if rv is None else rv
    qs = pl.BlockSpec((CHUNK, GDN_QK_W), lambda n: (ix(n), 0))
    ks = pl.BlockSpec((CHUNK, GDN_QK_W), lambda n: (ix(n), 1))
    vs = pl.BlockSpec((CHUNK, GDN_V_W), lambda n: (ix(n), 1))
    g1 = pl.BlockSpec((CHUNK, 128), lambda n: (ix(n), 0))
    wide = pl.BlockSpec((CHUNK, GDN_V_W), lambda n: (ix(n), 0))
    sq = pl.BlockSpec((1, GDN_V_HEADS, CHUNK, CHUNK), lambda n: (ix(n), 0, 0, 0))
    st = pl.BlockSpec((1, GDN_V_HEADS, HD, HD), lambda n: (ix(n), 0, 0, 0))
    return qs, ks, vs, g1, wide, sq, st


def _group_loop(body_fn, init):
    return lax.fori_loop(0, GDN_V_HEADS // GDN_HG, lambda g, c: body_fn(g * GDN_HG, c), init)


def gdn_intra_fwd(qkvc, gc, beta, name):
    S = qkvc.shape[0]
    NC = S // CHUNK

    def body(q_ref, k_ref, v_ref, gc_ref, be_ref, u_ref, w_ref, at_ref, T_ref):
        gcb = gc_ref[...]
        beb = be_ref[...]
        lane = lax.broadcasted_iota(jnp.int32, (CHUNK, 128), 1)

        def group(hv0, c):
            pre = []
            for r in range(GDN_HG):
                hv = hv0 + r
                off = pl.multiple_of((hv0 // 2 + r // 2) * HD, HD)
                gcol, bcol = _head_col(gcb, lane, hv), _head_col(beb, lane, hv)
                pre.append((off, gcol, bcol) + _gdn_intra_pre(k_ref[:, pl.ds(off, HD)], gcol, bcol))
            Ts = _neumann_inv_batched([p[5] for p in pre])
            for r in range(GDN_HG):
                hv = hv0 + r
                offv = pl.multiple_of(hv * HD, HD)
                off, gcol, bcol, dec, kb, _ = pre[r]
                u, w, attn = _gdn_intra_post(q_ref[:, pl.ds(off, HD)], k_ref[:, pl.ds(off, HD)], v_ref[:, pl.ds(offv, HD)],
                                             gcol, bcol, dec, kb, Ts[r])
                u_ref[:, pl.ds(offv, HD)] = u
                w_ref[:, pl.ds(offv, HD)] = w
                at_ref[0, hv] = attn
                T_ref[0, hv] = Ts[r]
            return c

        _group_loop(group, 0)

    qs, ks, vs, g1, wide, sq, _ = _gdn_specs(NC)
    return pl.pallas_call(
        body, name=name, grid=(NC,),
        in_specs=[qs, ks, vs, g1, g1], out_specs=[wide, wide, sq, sq],
        out_shape=[SDS((S, GDN_V_W), f32), SDS((S, GDN_V_W), f32),
                   SDS((NC, GDN_V_HEADS, CHUNK, CHUNK), f32), SDS((NC, GDN_V_HEADS, CHUNK, CHUNK), f32)],
        compiler_params=_cp(("arbitrary",)),
    )(qkvc, qkvc, qkvc, gc, beta)


def gdn_scan_fwd(qkvc, gc, u, w, attn, name):
    S = qkvc.shape[0]
    NC = S // CHUNK

    def body(q_ref, k_ref, gc_ref, u_ref, w_ref, at_ref, o_ref, st_ref, state):
        @pl.when(pl.program_id(0) == 0)
        def _():
            state[...] = jnp.zeros_like(state)

        gcb = gc_ref[...]
        lane = lax.broadcasted_iota(jnp.int32, (CHUNK, 128), 1)

        def group(hv0, c):
            for r in range(GDN_HG):
                hv = hv0 + r
                off = pl.multiple_of((hv0 // 2 + r // 2) * HD, HD)
                offv = pl.multiple_of(hv * HD, HD)
                S0 = state[hv]
                o, S1 = _gdn_scan(q_ref[:, pl.ds(off, HD)], k_ref[:, pl.ds(off, HD)], _head_col(gcb, lane, hv),
                                  u_ref[:, pl.ds(offv, HD)], w_ref[:, pl.ds(offv, HD)], at_ref[0, hv], S0)
                o_ref[:, pl.ds(offv, HD)] = o
                st_ref[0, hv] = S0
                state[hv] = S1
            return c

        _group_loop(group, 0)

    qs, ks, _, g1, wide, sq, st = _gdn_specs(NC)
    return pl.pallas_call(
        body, name=name, grid=(NC,),
        in_specs=[qs, ks, g1, wide, wide, sq], out_specs=[wide, st],
        out_shape=[SDS((S, GDN_V_W), f32), SDS((NC, GDN_V_HEADS, HD, HD), f32)],
        scratch_shapes=[pltpu.VMEM((GDN_V_HEADS, HD, HD), f32)],
        compiler_params=_cp(("arbitrary",)),
    )(qkvc, qkvc, gc, u, w, attn)


def gdn_scan_bwd(qkvc, gc, u, w, attn, states, do, name):
    S = qkvc.shape[0]
    NC = S // CHUNK

    def body(q_ref, k_ref, gc_ref, u_ref, w_ref, at_ref, st_ref, do_ref,
             dq_ref, dk_ref, dgc_ref, du_ref, dw_ref, dat_ref, dstate):
        @pl.when(pl.program_id(0) == 0)
        def _():
            dstate[...] = jnp.zeros_like(dstate)

        gcb = gc_ref[...]
        lane = lax.broadcasted_iota(jnp.int32, (CHUNK, 128), 1)

        def group(hv0, dgc_acc):
            dqk = [[jnp.zeros((CHUNK, HD), f32), jnp.zeros((CHUNK, HD), f32)] for _ in range(GDN_HG // 2)]
            for r in range(GDN_HG):
                hv = hv0 + r
                off = pl.multiple_of((hv0 // 2 + r // 2) * HD, HD)
                offv = pl.multiple_of(hv * HD, HD)
                _, vjp = jax.vjp(_gdn_scan, q_ref[:, pl.ds(off, HD)], k_ref[:, pl.ds(off, HD)], _head_col(gcb, lane, hv),
                                 u_ref[:, pl.ds(offv, HD)], w_ref[:, pl.ds(offv, HD)], at_ref[0, hv], st_ref[0, hv])
                dq_, dk_, dg_, du_, dw_, dat_, dS0 = vjp((do_ref[:, pl.ds(offv, HD)], dstate[hv]))
                dqk[r // 2][0] = dqk[r // 2][0] + dq_
                dqk[r // 2][1] = dqk[r // 2][1] + dk_
                du_ref[:, pl.ds(offv, HD)] = du_
                dw_ref[:, pl.ds(offv, HD)] = dw_
                dat_ref[0, hv] = dat_
                dstate[hv] = dS0
                dgc_acc = dgc_acc + jnp.where(lane == hv, dg_, 0.0)
            for p in range(GDN_HG // 2):
                off = pl.multiple_of((hv0 // 2 + p) * HD, HD)
                dq_ref[:, pl.ds(off, HD)] = dqk[p][0]
                dk_ref[:, pl.ds(off, HD)] = dqk[p][1]
            return dgc_acc

        dgc_ref[...] = _group_loop(group, jnp.zeros((CHUNK, 128), f32))

    qs, ks, _, g1, wide, sq, st = _gdn_specs(NC, lambda n: NC - 1 - n)
    dqs = pl.BlockSpec((CHUNK, GDN_QK_W), lambda n: (NC - 1 - n, 0))
    return pl.pallas_call(
        body, name=name, grid=(NC,),
        in_specs=[qs, ks, g1, wide, wide, sq, st, wide],
        out_specs=[dqs, dqs, g1, wide, wide, sq],
        out_shape=[SDS((S, GDN_QK_W), f32), SDS((S, GDN_QK_W), f32), SDS((S, 128), f32), SDS((S, GDN_V_W), f32),
                   SDS((S, GDN_V_W), f32), SDS((NC, GDN_V_HEADS, CHUNK, CHUNK), f32)],
        scratch_shapes=[pltpu.VMEM((GDN_V_HEADS, HD, HD), f32)],
        compiler_params=_cp(("arbitrary",)),
    )(qkvc, qkvc, gc, u, w, attn, states, do)


def gdn_intra_bwd(qkvc, gc, beta, Ts, du, dw, dattn, dq_s, dk_s, dgc_s, name):
    S = qkvc.shape[0]
    NC = S // CHUNK

    def body(q_ref, k_ref, v_ref, gc_ref, be_ref, T_ref, du_ref, dw_ref, dat_ref, dqs_ref, dks_ref, dgs_ref,
             dq_ref, dk_ref, dv_ref, dgc_ref, dbe_ref):
        gcb = gc_ref[...]
        beb = be_ref[...]
        lane = lax.broadcasted_iota(jnp.int32, (CHUNK, 128), 1)

        def group(hv0, carry):
            dgc_acc, dbe_acc = carry
            dqk = []
            for p in range(GDN_HG // 2):
                off = pl.multiple_of((hv0 // 2 + p) * HD, HD)
                dqk.append([dqs_ref[:, pl.ds(off, HD)], dks_ref[:, pl.ds(off, HD)]])
            for r in range(GDN_HG):
                hv = hv0 + r
                off = pl.multiple_of((hv0 // 2 + r // 2) * HD, HD)
                offv = pl.multiple_of(hv * HD, HD)
                T = T_ref[0, hv]
                _, vjp = jax.vjp(lambda q_, k_, v_, g_, b_: _gdn_intra(q_, k_, v_, g_, b_, T)[:3],
                                 q_ref[:, pl.ds(off, HD)], k_ref[:, pl.ds(off, HD)], v_ref[:, pl.ds(offv, HD)],
                                 _head_col(gcb, lane, hv), _head_col(beb, lane, hv))
                dq_, dk_, dv_, dg_, db_ = vjp((du_ref[:, pl.ds(offv, HD)], dw_ref[:, pl.ds(offv, HD)], dat_ref[0, hv]))
                dqk[r // 2][0] = dqk[r // 2][0] + dq_
                dqk[r // 2][1] = dqk[r // 2][1] + dk_
                dv_ref[:, pl.ds(offv, HD)] = dv_
                dgc_acc = dgc_acc + jnp.where(lane == hv, dg_, 0.0)
                dbe_acc = dbe_acc + jnp.where(lane == hv, db_, 0.0)
            for p in range(GDN_HG // 2):
                off = pl.multiple_of((hv0 // 2 + p) * HD, HD)
                dq_ref[:, pl.ds(off, HD)] = dqk[p][0]
                dk_ref[:, pl.ds(off, HD)] = dqk[p][1]
            return dgc_acc, dbe_acc

        dgc, dbe = _group_loop(group, (dgs_ref[...], jnp.zeros((CHUNK, 128), f32)))
        dgc_ref[...] = dgc
        dbe_ref[...] = dbe

    qs, ks, vs, g1, wide, sq, _ = _gdn_specs(NC)
    dqs = pl.BlockSpec((CHUNK, GDN_QK_W), lambda n: (n, 0))
    dq, dk, dv, dgc, dbe = pl.pallas_call(
        body, name=name, grid=(NC,),
        in_specs=[qs, ks, vs, g1, g1, sq, wide, wide, sq, dqs, dqs, g1],
        out_specs=[dqs, dqs, wide, g1, g1],
        out_shape=[SDS((S, GDN_QK_W), f32), SDS((S, GDN_QK_W), f32), SDS((S, GDN_V_W), f32), SDS((S, 128), f32), SDS((S, 128), f32)],
        compiler_params=_cp(("arbitrary",)),
    )(qkvc, qkvc, qkvc, gc, beta, Ts, du, dw, dattn, dq_s, dk_s, dgc_s)
    return jnp.concatenate([dq, dk, dv], axis=1), dgc, dbe


def _gated_norm(o, z, nw):
    parts = []
    for h in range(GDN_V_HEADS):
        oh = o[:, h * HD:(h + 1) * HD]
        r = lax.rsqrt(jnp.mean(oh * oh, axis=-1, keepdims=True) + EPS)
        parts.append((oh * r * nw) * _silu(z[:, h * HD:(h + 1) * HD]))
    return jnp.concatenate(parts, axis=1)


def gdn_out_fwd(o, proj, nw, W, x, gate, name):
    S, D = x.shape
    tm = _rows(S, 256)

    def body(o_ref, z_ref, nw_ref, w_ref, x_ref, g_ref, xn_ref, y_ref, og_ref):
        og = _gated_norm(o_ref[...], z_ref[...], nw_ref[...]).astype(bf16)
        y = jnp.dot(og, w_ref[...], preferred_element_type=f32)
        og_ref[...] = og
        y_ref[...] = y
        xn_ref[...] = x_ref[...] + g_ref[...] * y

    row = pl.BlockSpec((tm, D), lambda i: (i, 0))
    wide = pl.BlockSpec((tm, GDN_V_W), lambda i: (i, 0))
    return pl.pallas_call(
        body, name=name, grid=(S // tm,),
        in_specs=[wide, pl.BlockSpec((tm, GDN_V_W), lambda i: (i, 2)), pl.BlockSpec((1, HD), lambda i: (0, 0)),
                  pl.BlockSpec((GDN_V_W, D), lambda i: (0, 0)), row, pl.BlockSpec((1, D), lambda i: (0, 0))],
        out_specs=[row, row, wide],
        out_shape=[SDS((S, D), f32), SDS((S, D), f32), SDS((S, GDN_V_W), bf16)],
        compiler_params=_cp(("arbitrary",)),
    )(o, proj, nw, W, x, gate)


def gdn_out_bwd(dxn, y, gate, o, proj, nw, W, name):
    S, D = dxn.shape
    tm = _rows(S, 256)

    def body(dx_ref, y_ref, g_ref, o_ref, z_ref, nw_ref, w_ref, dy_ref, dg_ref, do_ref, dz_ref, dnw_ref):
        @pl.when(pl.program_id(0) == 0)
        def _():
            dg_ref[...] = jnp.zeros_like(dg_ref)
            dnw_ref[...] = jnp.zeros_like(dnw_ref)

        dx = dx_ref[...]
        dy = dx * g_ref[...]
        dy_ref[...] = dy
        dg_ref[...] += jnp.sum(dx * y_ref[...], axis=0, keepdims=True)
        dog = _nt(dy, w_ref[...])
        _, vjp = jax.vjp(_gated_norm, o_ref[...], z_ref[...], nw_ref[...])
        do, dz, dnw = vjp(dog)
        do_ref[...] = do
        dz_ref[...] = dz
        dnw_ref[...] += dnw

    row = pl.BlockSpec((tm, D), lambda i: (i, 0))
    wide = pl.BlockSpec((tm, GDN_V_W), lambda i: (i, 0))
    vecd = pl.BlockSpec((1, D), lambda i: (0, 0))
    vech = pl.BlockSpec((1, HD), lambda i: (0, 0))
    return pl.pallas_call(
        body, name=name, grid=(S // tm,),
        in_specs=[row, row, vecd, wide, pl.BlockSpec((tm, GDN_V_W), lambda i: (i, 2)), vech, pl.BlockSpec((GDN_V_W, D), lambda i: (0, 0))],
        out_specs=[row, vecd, wide, wide, vech],
        out_shape=[SDS((S, D), f32), SDS((1, D), f32), SDS((S, GDN_V_W), f32), SDS((S, GDN_V_W), f32), SDS((1, HD), f32)],
        compiler_params=_cp(("arbitrary",)),
    )(dxn, y, gate, o, proj, nw, W)


def _rms_w(x, w):
    return (x * lax.rsqrt(jnp.mean(x * x, axis=-1, keepdims=True) + EPS)) * w


def _split3(c):
    hi = c.astype(bf16).astype(f32)
    r1 = c - hi
    mid = r1.astype(bf16).astype(f32)
    lo = (r1 - mid).astype(bf16).astype(f32)
    return hi, mid, lo


_FOX_F_BLK = 4 * FOX_W // 128


def fox_prep_fwd(proj, f_bias, qn_w, kn_w, name):
    S = proj.shape[0]
    tm = _rows(S, 256)

    def body(q_ref, k_ref, v_ref, f_ref, fb_ref, qw_ref, kw_ref, Q_ref, K_ref, V_ref, carry):
        @pl.when(pl.program_id(0) == 0)
        def _():
            carry[...] = jnp.zeros_like(carry)

        ii = lax.broadcasted_iota(jnp.int32, (tm, tm), 0)
        jj = lax.broadcasted_iota(jnp.int32, (tm, tm), 1)
        lf = jax.nn.log_sigmoid(f_ref[...] + fb_ref[...])
        cum = _hdot((ii >= jj).astype(f32), lf) + carry[...]
        carry[...] = cum[tm - 1:tm, :]
        lane = lax.broadcasted_iota(jnp.int32, (tm, FOX_D), 1)
        q, k, v = q_ref[...], k_ref[...], v_ref[...]
        for h in range(FOX_H):
            sl = slice(h * FOX_D, (h + 1) * FOX_D)
            hi, mid, lo = _split3(cum[:, h:h + 1])
            qn = _rms_w(q[:, sl], qw_ref[...]) * FOX_D ** -0.5
            kn = _rms_w(k[:, sl], kw_ref[...])
            eq = jnp.where(lane == 0, hi, jnp.where(lane == 1, mid, jnp.where(lane == 2, lo, jnp.where(lane < 6, 1.0, 0.0))))
            ek = jnp.where(lane < 3, 1.0, jnp.where(lane == 3, -hi, jnp.where(lane == 4, -mid, jnp.where(lane == 5, -lo, 0.0))))
            ev = jnp.where(lane == 0, 1.0, 0.0)
            Q_ref[:, h * 128:(h + 1) * 128] = jnp.concatenate([qn, eq], axis=1).astype(bf16)
            K_ref[:, h * 128:(h + 1) * 128] = jnp.concatenate([kn, ek], axis=1).astype(bf16)
            V_ref[:, h * 128:(h + 1) * 128] = jnp.concatenate([v[:, sl], ev], axis=1).astype(bf16)

    def colblk(c):
        return pl.BlockSpec((tm, FOX_W), lambda i: (i, c))

    pad = pl.BlockSpec((tm, FOX_PW), lambda i: (i, 0))
    return pl.pallas_call(
        body, name=name, grid=(S // tm,),
        in_specs=[colblk(0), colblk(1), colblk(2), pl.BlockSpec((tm, 128), lambda i: (i, _FOX_F_BLK)),
                  pl.BlockSpec((1, 128), lambda i: (0, 0)), pl.BlockSpec((1, FOX_D), lambda i: (0, 0)), pl.BlockSpec((1, FOX_D), lambda i: (0, 0))],
        out_specs=[pad, pad, pad],
        out_shape=[SDS((S, FOX_PW), bf16)] * 3,
        scratch_shapes=[pltpu.VMEM((1, 128), f32)],
        compiler_params=_cp(("arbitrary",)),
    )(proj, proj, proj, proj, f_bias, qn_w, kn_w)


def fox_prep_bwd(proj, f_bias, qn_w, kn_w, dQ, dK, dV, name):
    S = proj.shape[0]
    tm = _rows(S, 256)
    NB = S // tm

    def body(q_ref, k_ref, f_ref, fb_ref, qw_ref, kw_ref, dQ_ref, dK_ref, dV_ref,
             dq_ref, dk_ref, dv_ref, df_ref, dfb_ref, dqw_ref, dkw_ref, carry):
        @pl.when(pl.program_id(0) == 0)
        def _():
            carry[...] = jnp.zeros_like(carry)
            dfb_ref[...] = jnp.zeros_like(dfb_ref)
            dqw_ref[...] = jnp.zeros_like(dqw_ref)
            dkw_ref[...] = jnp.zeros_like(dkw_ref)

        q, k = q_ref[...], k_ref[...]
        lane128 = lax.broadcasted_iota(jnp.int32, (tm, 128), 1)
        dcum = jnp.zeros((tm, 128), f32)
        dqs, dks, dvs = [], [], []
        dqw = jnp.zeros((1, FOX_D), f32)
        dkw = jnp.zeros((1, FOX_D), f32)
        for h in range(FOX_H):
            sl = slice(h * FOX_D, (h + 1) * FOX_D)
            dQh = dQ_ref[:, h * 128:(h + 1) * 128]
            dKh = dK_ref[:, h * 128:(h + 1) * 128]
            _, vq = jax.vjp(lambda a, w: _rms_w(a, w) * FOX_D ** -0.5, q[:, sl], qw_ref[...])
            dqh, dw1 = vq(dQh[:, 0:FOX_D])
            _, vk = jax.vjp(_rms_w, k[:, sl], kw_ref[...])
            dkh, dw2 = vk(dKh[:, 0:FOX_D])
            dqs.append(dqh)
            dks.append(dkh)
            dvs.append(dV_ref[:, h * 128:h * 128 + FOX_D])
            dqw = dqw + dw1
            dkw = dkw + dw2
            dcum = dcum + jnp.where(lane128 == h, dQh[:, FOX_D:FOX_D + 1] - dKh[:, FOX_D + 3:FOX_D + 4], 0.0)
        dq_ref[...] = jnp.concatenate(dqs, axis=1)
        dk_ref[...] = jnp.concatenate(dks, axis=1)
        dv_ref[...] = jnp.concatenate(dvs, axis=1)
        ii = lax.broadcasted_iota(jnp.int32, (tm, tm), 0)
        jj = lax.broadcasted_iota(jnp.int32, (tm, tm), 1)
        dlf = _hdot((ii <= jj).astype(f32), dcum) + carry[...]
        carry[...] += jnp.sum(dcum, axis=0, keepdims=True)
        df = dlf * jax.nn.sigmoid(-(f_ref[...] + fb_ref[...]))
        df_ref[...] = df
        dfb_ref[...] += jnp.sum(df, axis=0, keepdims=True)
        dqw_ref[...] += dqw
        dkw_ref[...] += dkw

    rv = lambda i: NB - 1 - i

    def colblk(c):
        return pl.BlockSpec((tm, FOX_W), lambda i: (rv(i), c))

    pad = pl.BlockSpec((tm, FOX_PW), lambda i: (rv(i), 0))
    cmp_ = pl.BlockSpec((tm, FOX_W), lambda i: (rv(i), 0))
    fblk = pl.BlockSpec((tm, 128), lambda i: (rv(i), 0))
    v128 = pl.BlockSpec((1, 128), lambda i: (0, 0))
    v64 = pl.BlockSpec((1, FOX_D), lambda i: (0, 0))
    return pl.pallas_call(
        body, name=name, grid=(NB,),
        in_specs=[colblk(0), colblk(1), pl.BlockSpec((tm, 128), lambda i: (rv(i), _FOX_F_BLK)), v128, v64, v64, pad, pad, pad],
        out_specs=[cmp_, cmp_, cmp_, fblk, v128, v64, v64],
        out_shape=[SDS((S, FOX_W), f32)] * 3 + [SDS((S, 128), f32), SDS((1, 128), f32), SDS((1, FOX_D), f32), SDS((1, FOX_D), f32)],
        scratch_shapes=[pltpu.VMEM((1, 128), f32)],
        compiler_params=_cp(("arbitrary",)),
    )(proj, proj, proj, f_bias, qn_w, kn_w, dQ, dK, dV)


def _diag_mask(t):
    return lax.broadcasted_iota(jnp.int32, (t, t), 1) <= lax.broadcasted_iota(jnp.int32, (t, t), 0)


def fox_attn_fwd(Q, K, V, name):
    S = Q.shape[0]
    t = _rows(S, 512)

    def body(q_ref, k_ref, v_ref, o_ref, m_sc, acc_sc):
        i = pl.program_id(1)
        q = q_ref[...]
        m_sc[...] = jnp.full_like(m_sc, NEG)
        acc_sc[...] = jnp.zeros_like(acc_sc)

        def tile(j, diag):
            j0 = pl.multiple_of(j * t, t)
            s = _nt(q, k_ref[pl.ds(j0, t), :])
            if diag:
                s = jnp.where(_diag_mask(t), s, NEG)
            m = m_sc[...]
            m_new = jnp.maximum(m, jnp.max(s, axis=1, keepdims=True))
            p = jnp.exp(s - m_new)
            acc_sc[...] = acc_sc[...] * jnp.exp(m - m_new) + _nn(p, v_ref[pl.ds(j0, t), :])
            m_sc[...] = m_new

        def off_diag(j, c):
            tile(j, False)
            return c

        lax.fori_loop(0, i, off_diag, 0)
        tile(i, True)
        acc = acc_sc[...]
        l = acc[:, FOX_D:FOX_D + 1]
        lane = lax.broadcasted_iota(jnp.int32, (t, 128), 1)
        o_ref[...] = jnp.where(lane == FOX_D, m_sc[...] + jnp.log(l), acc / l)

    blk = pl.BlockSpec((t, 128), lambda h, i: (i, h))
    seq = pl.BlockSpec((S, 128), lambda h, i: (0, h))
    return pl.pallas_call(
        body, name=name, grid=(FOX_H, S // t),
        in_specs=[blk, seq, seq], out_specs=blk, out_shape=SDS((S, FOX_PW), f32),
        scratch_shapes=[pltpu.VMEM((t, 1), f32), pltpu.VMEM((t, 128), f32)],
        compiler_params=_cp(("arbitrary", "arbitrary")),
    )(Q, K, V)


def fox_attn_bwd(Q, K, V, dO, O, name):
    S = Q.shape[0]
    t = _rows(S, 512)
    nq = S // t

    def body(k_ref, v_ref, q_ref, do_ref, o_ref, dq_ref, dk_ref, dv_ref):
        j = pl.program_id(1)

        @pl.when(j == 0)
        def _():
            dq_ref[...] = jnp.zeros_like(dq_ref)

        dk_ref[...] = jnp.zeros_like(dk_ref)
        dv_ref[...] = jnp.zeros_like(dv_ref)
        k = k_ref[...]
        v = v_ref[...]

        def tile(i, diag):
            i0 = pl.multiple_of(i * t, t)
            q = q_ref[pl.ds(i0, t), :]
            do = do_ref[pl.ds(i0, t), :]
            s = _nt(q, k) - o_ref[pl.ds(i0, t), FOX_D:FOX_D + 1]
            if diag:
                s = jnp.where(_diag_mask(t), s, NEG)
            p = jnp.exp(s)
            dv_ref[...] += _tn(p, do)
            ds = (p * _nt(do, v)).astype(bf16)
            dk_ref[...] += _tn(ds, q)
            dq_ref[pl.ds(i0, t), :] += _nn(ds, k)

        tile(j, True)

        def off_diag(i, c):
            tile(i, False)
            return c

        lax.fori_loop(j + 1, nq, off_diag, 0)

    blk = pl.BlockSpec((t, 128), lambda h, j: (j, h))
    seq = pl.BlockSpec((S, 128), lambda h, j: (0, h))
    return pl.pallas_call(
        body, name=name, grid=(FOX_H, nq),
        in_specs=[blk, blk, seq, seq, seq], out_specs=[seq, blk, blk],
        out_shape=[SDS((S, FOX_PW), f32)] * 3,
        compiler_params=_cp(("arbitrary", "arbitrary")),
    )(K, V, Q, dO, O)


def fox_out_fwd(O, proj, W, x, gate, name):
    S, D = x.shape
    tm = _rows(S, 256)

    def body(o_ref, z_ref, w_ref, x_ref, g_ref, xn_ref, y_ref, og_ref):
        z = z_ref[...]
        og = jnp.concatenate([o_ref[:, h * 128:h * 128 + FOX_D] * _silu(z[:, h * FOX_D:(h + 1) * FOX_D]) for h in range(FOX_H)],
                             axis=1).astype(bf16)
        y = jnp.dot(og, w_ref[...], preferred_element_type=f32)
        og_ref[...] = og
        y_ref[...] = y
        xn_ref[...] = x_ref[...] + g_ref[...] * y

    row = pl.BlockSpec((tm, D), lambda i: (i, 0))
    cmp_ = pl.BlockSpec((tm, FOX_W), lambda i: (i, 0))
    return pl.pallas_call(
        body, name=name, grid=(S // tm,),
        in_specs=[pl.BlockSpec((tm, FOX_PW), lambda i: (i, 0)), pl.BlockSpec((tm, FOX_W), lambda i: (i, 3)),
                  pl.BlockSpec((FOX_W, D), lambda i: (0, 0)), row, pl.BlockSpec((1, D), lambda i: (0, 0))],
        out_specs=[row, row, cmp_],
        out_shape=[SDS((S, D), f32), SDS((S, D), f32), SDS((S, FOX_W), bf16)],
        compiler_params=_cp(("arbitrary",)),
    )(O, proj, W, x, gate)


def fox_out_bwd(dxn, y, gate, O, proj, W, name):
    S, D = dxn.shape
    tm = _rows(S, 256)

    def body(dx_ref, y_ref, g_ref, o_ref, z_ref, w_ref, dy_ref, dg_ref, dO_ref, dz_ref):
        @pl.when(pl.program_id(0) == 0)
        def _():
            dg_ref[...] = jnp.zeros_like(dg_ref)

        dx = dx_ref[...]
        dy = dx * g_ref[...]
        dy_ref[...] = dy
        dg_ref[...] += jnp.sum(dx * y_ref[...], axis=0, keepdims=True)
        dog = _nt(dy, w_ref[...])
        z = z_ref[...]
        lane = lax.broadcasted_iota(jnp.int32, (tm, FOX_D), 1)
        dzs = []
        for h in range(FOX_H):
            sl = slice(h * FOX_D, (h + 1) * FOX_D)
            zh = z[:, sl]
            sg = jax.nn.sigmoid(zh)
            oh = o_ref[:, h * 128:h * 128 + FOX_D]
            doh = dog[:, sl] * (zh * sg)
            delta = jnp.sum(doh * oh, axis=1, keepdims=True)
            dO_ref[:, h * 128:(h + 1) * 128] = jnp.concatenate([doh, jnp.where(lane == 0, -delta, 0.0)], axis=1).astype(bf16)
            dzs.append(dog[:, sl] * oh * (sg * (1.0 + zh * (1.0 - sg))))
        dz_ref[...] = jnp.concatenate(dzs, axis=1)

    row = pl.BlockSpec((tm, D), lambda i: (i, 0))
    vecd = pl.BlockSpec((1, D), lambda i: (0, 0))
    pad = pl.BlockSpec((tm, FOX_PW), lambda i: (i, 0))
    return pl.pallas_call(
        body, name=name, grid=(S // tm,),
        in_specs=[row, row, vecd, pad, pl.BlockSpec((tm, FOX_W), lambda i: (i, 3)), pl.BlockSpec((FOX_W, D), lambda i: (0, 0))],
        out_specs=[row, vecd, pad, pl.BlockSpec((tm, FOX_W), lambda i: (i, 0))],
        out_shape=[SDS((S, D), f32), SDS((1, D), f32), SDS((S, FOX_PW), bf16), SDS((S, FOX_W), f32)],
        compiler_params=_cp(("arbitrary",)),
    )(dxn, y, gate, O, proj, W)


def final_loss(x, fw, target, name):
    S, D = x.shape
    tm = _rows(S, 512)

    def body(x_ref, w_ref, t_ref, l_ref, dx_ref, dw_ref):
        @pl.when(pl.program_id(0) == 0)
        def _():
            l_ref[...] = jnp.zeros_like(l_ref)
            dw_ref[...] = jnp.zeros_like(dw_ref)

        out, vjp = jax.vjp(_rms_w, x_ref[...], w_ref[...])
        err = out - t_ref[...]
        l_ref[...] += 0.5 * jnp.sum(jnp.sum(err * err, axis=1, keepdims=True) * (1.0 / D), axis=0, keepdims=True)
        dx, dw = vjp(err * (1.0 / D))
        dx_ref[...] = dx
        dw_ref[...] += dw

    row = pl.BlockSpec((tm, D), lambda i: (i, 0))
    vec = pl.BlockSpec((1, D), lambda i: (0, 0))
    return pl.pallas_call(
        body, name=name, grid=(S // tm,),
        in_specs=[row, vec, row], out_specs=[pl.BlockSpec((1, 128), lambda i: (0, 0)), row, vec],
        out_shape=[SDS((1, 128), f32), SDS((S, D), f32), SDS((1, D), f32)],
        compiler_params=_cp(("arbitrary",)),
    )(x, fw, target)


def ada_fwd(c_all, ada_w, name):
    L, D, n = ada_w.shape

    def body(c_ref, w_ref, o_ref):
        cond = jnp.concatenate([_silu(c_ref[...]), jnp.zeros((8, D), f32)], axis=0)
        o_ref[0] = _nn(cond, w_ref[0])[0:8]

    return pl.pallas_call(
        body, name=name, grid=(L,),
        in_specs=[pl.BlockSpec((NDEV, D), lambda l: (0, 0)), pl.BlockSpec((1, D, n), lambda l: (l, 0, 0))],
        out_specs=pl.BlockSpec((1, NDEV, n), lambda l: (l, 0, 0)),
        out_shape=SDS((L, NDEV, n), f32),
        compiler_params=_cp(("arbitrary",)),
    )(c_all, ada_w)


def ada_grad(c_all, dmod, name):
    L, _, n = dmod.shape
    D = c_all.shape[1]

    def body(c_ref, d_ref, o_ref):
        cond = jnp.concatenate([_silu(c_ref[...]), jnp.zeros((8, D), f32)], axis=0)
        dm = jnp.concatenate([d_ref[0], jnp.zeros((8, n), f32)], axis=0)
        o_ref[0] = _tn(cond, dm)

    return pl.pallas_call(
        body, name=name, grid=(L,),
        in_specs=[pl.BlockSpec((NDEV, D), lambda l: (0, 0)), pl.BlockSpec((1, NDEV, n), lambda l: (l, 0, 0))],
        out_specs=pl.BlockSpec((1, D, n), lambda l: (l, 0, 0)),
        out_shape=SDS((L, D, n), f32),
        compiler_params=_cp(("arbitrary",)),
    )(c_all, dmod)


def reduce_adam(parts, w, m, v, tr, name):
    n, R, C = parts.shape
    c1 = 1.0 / (1.0 - ADAM_B1 ** ADAM_STEP)
    c2 = 1.0 / (1.0 - ADAM_B2 ** ADAM_STEP)

    def body(p_ref, w_ref, m_ref, v_ref, g_ref, d_ref, nm_ref, nv_ref):
        g = p_ref[0].astype(f32)
        for s in range(1, n):
            g = g + p_ref[s].astype(f32)
        nm = ADAM_B1 * m_ref[...] + (1.0 - ADAM_B1) * g
        nv = ADAM_B2 * v_ref[...] + (1.0 - ADAM_B2) * (g * g)
        g_ref[...] = g
        nm_ref[...] = nm
        nv_ref[...] = nv
        d_ref[...] = -ADAM_LR * ((nm * c1) / (jnp.sqrt(nv * c2) + ADAM_EPS) + ADAM_WD * w_ref[...])

    blk = pl.BlockSpec((tr, C), lambda i: (i, 0))
    return pl.pallas_call(
        body, name=name, grid=(R // tr,),
        in_specs=[pl.BlockSpec((n, tr, C), lambda i: (0, i, 0)), blk, blk, blk],
        out_specs=[blk] * 4, out_shape=[SDS((R, C), f32)] * 4,
        compiler_params=_cp(("arbitrary",)),
    )(parts, w, m, v)


def _my_pos():
    return lax.axis_index("x"), lax.axis_index("y"), lax.axis_index("c")


def all_gather(xs, name):
    n = len(xs)

    def body(*refs):
        x_refs, out_refs = refs[:n], refs[n:2 * n]
        send_sems, recv_sems, local_sems = refs[2 * n:]
        x_, y_, c_ = _my_pos()
        me, sibling = (x_, y_, c_), (x_, y_, 1 - c_)
        chips = [(1 - x_, y_), (x_, 1 - y_), (1 - x_, 1 - y_)]

        def rows(a, px, py, pc):
            return out_refs[a].at[4 * px + 2 * py + pc]

        def copy(a, k, block, to, own=False):
            return pltpu.make_async_remote_copy(
                src_ref=x_refs[a] if own else rows(a, *block), dst_ref=rows(a, *block),
                send_sem=send_sems.at[k, a], recv_sem=recv_sems.at[k, a], device_id=to, device_id_type=pl.DeviceIdType.MESH)

        mine = [pltpu.make_async_copy(x_refs[a], rows(a, *me), local_sems.at[a]) for a in range(n)]
        for cp in mine:
            cp.start()
        first = []
        for a in range(n):
            first.append(copy(a, 0, me, sibling, own=True))
            first += [copy(a, 1 + j, me, (*chip, c_), own=True) for j, chip in enumerate(chips)]
        for cp in first:
            cp.start()
        passed = []
        for j, chip in enumerate(chips):
            for a in range(n):
                copy(a, 1 + j, (*chip, c_), me).wait_recv()
                cp = copy(a, 4 + j, (*chip, c_), sibling)
                cp.start()
                passed.append(cp)
        for a in range(n):
            copy(a, 0, sibling, me).wait_recv()
            for j, chip in enumerate(chips):
                copy(a, 4 + j, (*chip, 1 - c_), me).wait_recv()
        for cp in first + passed:
            cp.wait_send()
        for cp in mine:
            cp.wait()

    any_ = pl.BlockSpec(memory_space=pl.ANY)
    return pl.pallas_call(
        body, name=name, out_shape=[SDS((NDEV,) + x.shape, x.dtype) for x in xs],
        in_specs=[any_] * n, out_specs=[any_] * n,
        scratch_shapes=[pltpu.SemaphoreType.DMA((7, n)), pltpu.SemaphoreType.DMA((7, n)), pltpu.SemaphoreType.DMA((n,))],
    )(*xs)


def all_to_all(xs, name):
    n = len(xs)

    def body(*refs):
        x_refs, out_refs = refs[:n], refs[n:2 * n]
        send_sems, recv_sems, local_sems = refs[2 * n:]
        x_, y_, c_ = _my_pos()
        me = 4 * x_ + 2 * y_ + c_
        local = [pltpu.make_async_copy(x_refs[a].at[me], out_refs[a].at[me], local_sems.at[a]) for a in range(n)]
        for cp in local:
            cp.start()
        copies = []
        for rel in range(1, NDEV):
            px = (x_ + ((rel >> 2) & 1)) % 2
            py = (y_ + ((rel >> 1) & 1)) % 2
            pc = (c_ + (rel & 1)) % 2
            for a in range(n):
                cp = pltpu.make_async_remote_copy(
                    src_ref=x_refs[a].at[4 * px + 2 * py + pc], dst_ref=out_refs[a].at[me],
                    send_sem=send_sems.at[rel - 1, a], recv_sem=recv_sems.at[rel - 1, a],
                    device_id=(px, py, pc), device_id_type=pl.DeviceIdType.MESH)
                cp.start()
                copies.append(cp)
        for cp in copies:
            cp.wait()
        for cp in local:
            cp.wait()

    any_ = pl.BlockSpec(memory_space=pl.ANY)
    return pl.pallas_call(
        body, name=name, out_shape=[SDS(x.shape, x.dtype) for x in xs],
        in_specs=[any_] * n, out_specs=[any_] * n,
        scratch_shapes=[pltpu.SemaphoreType.DMA((7, n)), pltpu.SemaphoreType.DMA((7, n)), pltpu.SemaphoreType.DMA((n,))],
    )(*xs)


GDN_COLS = ((0, GDN_CONV + GDN_V_W, 0), (GDN_CONV + GDN_V_W, GDN_CONV + GDN_V_W + 16, GDN_CONV + GDN_V_W),
            (GDN_CONV + GDN_V_W + 16, GDN_IN, GDN_CONV + GDN_V_W + 128))
FOX_COLS = ((0, FOX_IN, 0),)


def _col_pieces(d, per, cols):
    lo, hi = per * d, per * (d + 1)
    out = []
    for a, b, dst in cols:
        s, e = max(lo, a), min(hi, b)
        if s < e:
            out.append((s - lo, e - s, dst + s - a))
    return out


def cols_from_blocks(g, cols, n_out, name):
    _, L, R, C = g.shape
    tr = min(256, R)

    def body(g_ref, o_ref):
        o_ref[...] = jnp.zeros_like(o_ref)
        for d in range(NDEV):
            for off, ln, dst in _col_pieces(d, C, cols):
                o_ref[0, :, dst:dst + ln] = g_ref[d, 0, :, off:off + ln]

    return pl.pallas_call(
        body, name=name, grid=(L, R // tr),
        in_specs=[pl.BlockSpec((NDEV, 1, tr, C), lambda l, i: (0, l, i, 0))],
        out_specs=pl.BlockSpec((1, tr, n_out), lambda l, i: (l, i, 0)),
        out_shape=SDS((L, R, n_out), g.dtype),
        compiler_params=_cp(("arbitrary", "arbitrary")),
    )(g)


def blocks_from_cols(dw, C, cols, name):
    R, n_in = dw.shape
    tr = min(256, R)

    def body(x_ref, o_ref):
        for d in range(NDEV):
            for off, ln, src in _col_pieces(d, C, cols):
                o_ref[d, :, off:off + ln] = x_ref[:, src:src + ln].astype(bf16)

    return pl.pallas_call(
        body, name=name, grid=(R // tr,),
        in_specs=[pl.BlockSpec((tr, n_in), lambda i: (i, 0))],
        out_specs=pl.BlockSpec((NDEV, tr, C), lambda i: (0, i, 0)),
        out_shape=SDS((NDEV, R, C), bf16),
        compiler_params=_cp(("arbitrary",)),
    )(dw)


BIG = ("a_w_in", "a_conv_w", "a_w_out", "b_w_in", "b_w_out")
SMALL = ("norm_w", "ada_b", "a_A_log", "a_dt_bias", "a_norm_w", "b_f_bias", "b_qn_w", "b_kn_w", "final_norm_w")


def _pack_small(arrs):
    rows = []
    for a in arrs:
        fl = a.reshape(-1)
        pad = (-fl.shape[0]) % 128
        if pad:
            fl = jnp.concatenate([fl, jnp.zeros((pad,), fl.dtype)])
        rows.append(fl)
    flat = jnp.concatenate(rows)
    pad = (-flat.shape[0]) % (8 * 128)
    if pad:
        flat = jnp.concatenate([flat, jnp.zeros((pad,), flat.dtype)])
    return flat.reshape(-1, 128)


def _unpack(packed, shapes, align):
    flat = packed.reshape(-1)
    out, off = [], 0
    for shp in shapes:
        n = 1
        for d in shp:
            n *= d
        out.append(flat[off:off + n].reshape(shp))
        off += n + ((-n) % align)
    return out


def _full_from_gathered(g, shard_shape, axis):
    g = jnp.moveaxis(g, 0, axis)
    shp = list(shard_shape)
    shp[axis] *= NDEV
    return g.reshape(shp)


def _blocks_from_full(full, shard_shape, axis):
    shp = list(shard_shape)
    full = full.reshape(shp[:axis] + [NDEV, shp[axis]] + shp[axis + 1:])
    return jnp.moveaxis(full, axis, 0)


def _pad_lanes(v, n=128):
    v = v.reshape(1, -1)
    return jnp.concatenate([v, jnp.zeros((1, n - v.shape[1]), v.dtype)], axis=1)


def gdn_layer_fwd(x, mod, nw, W_in, conv_w, A_log, dt_bias, a_nw, W_out, tag):
    shift, scale, gate = mod
    proj, h = inproj_fwd(x, nw, scale, shift, W_in, GDN_TN, f"{tag}_inproj")
    qkvc = gdn_prep_fwd(proj, conv_w, f"{tag}_prep")
    gc, beta = gdn_gates_fwd(proj, A_log, dt_bias, f"{tag}_gates")
    u, w, attn, Ts = gdn_intra_fwd(qkvc, gc, beta, f"{tag}_intra")
    o, states = gdn_scan_fwd(qkvc, gc, u, w, attn, f"{tag}_scan")
    x_new, y, og = gdn_out_fwd(o, proj, a_nw, W_out, x, gate, f"{tag}_out")
    return x_new, (x, proj, h, qkvc, gc, beta, o, states, Ts, y, og, u, w, attn)


def gdn_layer_bwd(dxn, saved, mod, nw, W_in, conv_w, A_log, dt_bias, a_nw, W_out, tag):
    shift, scale, gate = mod
    x, proj, h, qkvc, gc, beta, o, states, Ts, y, og, u, w, attn = saved
    dy, dgate, do, dz, da_nw = gdn_out_bwd(dxn, y, gate, o, proj, a_nw, W_out, f"{tag}_out_bwd")
    dW_out = matmul_tn(og, dy, 512, f"{tag}_dwout")
    dq_s, dk_s, dgc_s, du, dw, dattn = gdn_scan_bwd(qkvc, gc, u, w, attn, states, do, f"{tag}_scan_bwd")
    dqkvc, dgc, dbeta = gdn_intra_bwd(qkvc, gc, beta, Ts, du, dw, dattn, dq_s, dk_s, dgc_s, f"{tag}_intra_bwd")
    db, da, dA_log, ddt = gdn_gates_bwd(proj, A_log, dt_bias, dgc, dbeta, f"{tag}_gates_bwd")
    dqkv, dconv_w = gdn_prep_bwd(proj, conv_w, dqkvc, f"{tag}_prep_bwd")
    dproj = jnp.concatenate([dqkv, dz, db, da], axis=1)
    dW_in = matmul_tn(h, dproj, GDN_TN, f"{tag}_dwin")
    dx, dnw, dscale, dshift = inproj_bwd_x(x, nw, scale, shift, W_in, dproj, dxn, GDN_TN, f"{tag}_inproj_bwd")
    grads = dict(norm_w=dnw, W_in=dW_in, conv_w=dconv_w, A_log=dA_log[:, :16], dt_bias=ddt[:, :16], a_nw=da_nw, W_out=dW_out,
                 dmod=jnp.concatenate([dshift, dscale, dgate], axis=1))
    return dx, grads


def fox_layer_fwd(x, mod, nw, W_in, f_bias, qn_w, kn_w, W_out, tag):
    shift, scale, gate = mod
    proj, h = inproj_fwd(x, nw, scale, shift, W_in, FOX_TN, f"{tag}_inproj")
    Q, K, V = fox_prep_fwd(proj, f_bias, qn_w, kn_w, f"{tag}_prep")
    O = fox_attn_fwd(Q, K, V, f"{tag}_attn")
    x_new, y, og = fox_out_fwd(O, proj, W_out, x, gate, f"{tag}_out")
    return x_new, (x, proj, h, Q, K, V, O, y, og)


def fox_layer_bwd(dxn, saved, mod, nw, W_in, f_bias, qn_w, kn_w, W_out, tag):
    shift, scale, gate = mod
    x, proj, h, Q, K, V, O, y, og = saved
    dy, dgate, dO, dz = fox_out_bwd(dxn, y, gate, O, proj, W_out, f"{tag}_out_bwd")
    dW_out = matmul_tn(og, dy, 512, f"{tag}_dwout")
    dQ, dK, dV = fox_attn_bwd(Q, K, V, dO, O, f"{tag}_attn_bwd")
    dq, dk, dv, df, dfb, dqw, dkw = fox_prep_bwd(proj, f_bias, qn_w, kn_w, dQ, dK, dV, f"{tag}_prep_bwd")
    dproj = jnp.concatenate([dq, dk, dv, dz, df], axis=1)
    dW_in = matmul_tn(h, dproj, FOX_TN, f"{tag}_dwin")
    dx, dnw, dscale, dshift = inproj_bwd_x(x, nw, scale, shift, W_in, dproj, dxn, FOX_TN, f"{tag}_inproj_bwd")
    grads = dict(norm_w=dnw, W_in=dW_in, f_bias=dfb[:, :16], qn_w=dqw, kn_w=dkw, W_out=dW_out,
                 dmod=jnp.concatenate([dshift, dscale, dgate], axis=1))
    return dx, grads


def device_step(x, mod_all, norm_w, full, small, final_norm_w, target):
    D = x.shape[1]
    mods = [(mod_all[i:i + 1, 0:D], mod_all[i:i + 1, D:2 * D], mod_all[i:i + 1, 2 * D:3 * D]) for i in range(4)]

    def a_args(j):
        return (full["a_w_in"][j], full["a_conv_w"][j], _pad_lanes(small["a_A_log"][j]), _pad_lanes(small["a_dt_bias"][j]),
                small["a_norm_w"][j:j + 1], full["a_w_out"][j])

    def b_args(j):
        return (full["b_w_in"][j], _pad_lanes(small["b_f_bias"][j]), small["b_qn_w"][j:j + 1], small["b_kn_w"][j:j + 1],
                full["b_w_out"][j])

    saved = []
    for i in range(4):
        j = i // 2
        if i % 2 == 0:
            x, sv = gdn_layer_fwd(x, mods[i], norm_w[i:i + 1], *a_args(j), tag=f"L{i}")
        else:
            x, sv = fox_layer_fwd(x, mods[i], norm_w[i:i + 1], *b_args(j), tag=f"L{i}")
        saved.append(sv)
    loss, dx, dfw = final_loss(x, final_norm_w.reshape(1, D), target, "final_loss")
    lg = [None] * 4
    for i in reversed(range(4)):
        j = i // 2
        if i % 2 == 0:
            dx, lg[i] = gdn_layer_bwd(dx, saved[i], mods[i], norm_w[i:i + 1], *a_args(j), tag=f"L{i}")
        else:
            dx, lg[i] = fox_layer_bwd(dx, saved[i], mods[i], norm_w[i:i + 1], *b_args(j), tag=f"L{i}")
    g = dict(
        norm_w=jnp.concatenate([lg[i]["norm_w"] for i in range(4)], axis=0),
        dmod=jnp.concatenate([lg[i]["dmod"] for i in range(4)], axis=0),
        a_w_in=[lg[i]["W_in"] for i in (0, 2)],
        a_conv_w=jnp.stack([lg[i]["conv_w"] for i in (0, 2)]),
        a_A_log=jnp.concatenate([lg[i]["A_log"] for i in (0, 2)], axis=0),
        a_dt_bias=jnp.concatenate([lg[i]["dt_bias"] for i in (0, 2)], axis=0),
        a_norm_w=jnp.concatenate([lg[i]["a_nw"] for i in (0, 2)], axis=0),
        a_w_out=jnp.stack([lg[i]["W_out"] for i in (0, 2)]),
        b_w_in=[lg[i]["W_in"] for i in (1, 3)],
        b_f_bias=jnp.concatenate([lg[i]["f_bias"] for i in (1, 3)], axis=0),
        b_qn_w=jnp.concatenate([lg[i]["qn_w"] for i in (1, 3)], axis=0),
        b_kn_w=jnp.concatenate([lg[i]["kn_w"] for i in (1, 3)], axis=0),
        b_w_out=jnp.stack([lg[i]["W_out"] for i in (1, 3)]),
        final_norm_w=dfw.reshape(-1),
    )
    return loss[0, 0], dx, g


def kernel(x, c, norm_w, ada_w, ada_b, a_w_in, a_conv_w, a_A_log, a_dt_bias, a_norm_w, a_w_out, b_w_in, b_f_bias, b_qn_w, b_kn_w, b_w_out, final_norm_w, loss_target, m_norm_w, m_ada_w, m_ada_b, m_a_w_in, m_a_conv_w, m_a_A_log, m_a_dt_bias, m_a_norm_w, m_a_w_out, m_b_w_in, m_b_f_bias, m_b_qn_w, m_b_kn_w, m_b_w_out, m_final_norm_w, v_norm_w, v_ada_w, v_ada_b, v_a_w_in, v_a_conv_w, v_a_A_log, v_a_dt_bias, v_a_norm_w, v_a_w_out, v_b_w_in, v_b_f_bias, v_b_qn_w, v_b_kn_w, v_b_w_out, v_final_norm_w):
    W = dict(norm_w=norm_w, ada_w=ada_w, ada_b=ada_b, a_w_in=a_w_in, a_conv_w=a_conv_w, a_A_log=a_A_log, a_dt_bias=a_dt_bias,
             a_norm_w=a_norm_w, a_w_out=a_w_out, b_w_in=b_w_in, b_f_bias=b_f_bias, b_qn_w=b_qn_w, b_kn_w=b_kn_w, b_w_out=b_w_out,
             final_norm_w=final_norm_w)
    M = dict(norm_w=m_norm_w, ada_w=m_ada_w, ada_b=m_ada_b, a_w_in=m_a_w_in, a_conv_w=m_a_conv_w, a_A_log=m_a_A_log,
             a_dt_bias=m_a_dt_bias, a_norm_w=m_a_norm_w, a_w_out=m_a_w_out, b_w_in=m_b_w_in, b_f_bias=m_b_f_bias, b_qn_w=m_b_qn_w,
             b_kn_w=m_b_kn_w, b_w_out=m_b_w_out, final_norm_w=m_final_norm_w)
    V = dict(norm_w=v_norm_w, ada_w=v_ada_w, ada_b=v_ada_b, a_w_in=v_a_w_in, a_conv_w=v_a_conv_w, a_A_log=v_a_A_log,
             a_dt_bias=v_a_dt_bias, a_norm_w=v_a_norm_w, a_w_out=v_a_w_out, b_w_in=v_b_w_in, b_f_bias=v_b_f_bias, b_qn_w=v_b_qn_w,
             b_kn_w=v_b_kn_w, b_w_out=v_b_w_out, final_norm_w=v_final_norm_w)
    S, D = x.shape[1], x.shape[2]
    me = 4 * lax.axis_index("x") + 2 * lax.axis_index("y") + lax.axis_index("c")
    small_shapes = [W[n].shape for n in SMALL]
    n_ain, n_bin = a_w_in.shape[2], b_w_in.shape[2]

    gath = all_gather([a_w_in.astype(bf16).reshape(-1, n_ain), a_w_out.astype(bf16).reshape(-1, D),
                       b_w_in.astype(bf16).reshape(-1, n_bin), b_w_out.astype(bf16).reshape(-1, D),
                       a_conv_w.reshape(8, -1), c.reshape(8, D // 8)], "gather_w")
    full = dict(
        a_w_in=cols_from_blocks(gath[0].reshape((NDEV,) + a_w_in.shape), GDN_COLS, GDN_IN_PAD, "a_w_in_cols"),
        b_w_in=cols_from_blocks(gath[2].reshape((NDEV,) + b_w_in.shape), FOX_COLS, FOX_IN_PAD, "b_w_in_cols"),
        a_w_out=_full_from_gathered(gath[1].reshape((NDEV,) + a_w_out.shape), a_w_out.shape, 1),
        b_w_out=_full_from_gathered(gath[3].reshape((NDEV,) + b_w_out.shape), b_w_out.shape, 1),
        a_conv_w=_full_from_gathered(gath[4].reshape((NDEV,) + a_conv_w.shape), a_conv_w.shape, 2))
    c_all = gath[5].reshape(NDEV, D)

    mod_part = ada_fwd(c_all, ada_w, "ada_fwd")
    n_ada = ada_w.shape[2]
    mod_g = all_gather([mod_part.reshape(4 * NDEV, n_ada)], "gather_mod")[0].reshape(NDEV, 4, NDEV, n_ada)
    mod_mine = lax.dynamic_index_in_dim(mod_g, me, axis=2, keepdims=False)
    mod_all = jnp.moveaxis(mod_mine, 0, 1).reshape(4, NDEV * n_ada) + ada_b

    loss, dx, g = device_step(x[0], mod_all, norm_w, full, W, final_norm_w, loss_target[0])
    loss = lax.psum(loss, MESH_AXES)

    g_small = dict(g, ada_b=g["dmod"])
    sp = _pack_small([g_small[n] for n in SMALL])
    sp_all = all_gather([sp], "gather_small")[0]
    sw, sm, sv = (_pack_small([T[n] for n in SMALL]) for T in (W, M, V))
    sg, sd, snm, snv = (_unpack(t, small_shapes, 128) for t in reduce_adam(sp_all, sw, sm, sv, sp.shape[0], "adam_small"))

    off_b = 0
    for n, shp in zip(SMALL, small_shapes):
        if n == "ada_b":
            break
        cnt = 1
        for d in shp:
            cnt *= d
        off_b += cnt + ((-cnt) % 128)
    dmod_all = sp_all.reshape(NDEV, -1)[:, off_b:off_b + 4 * 3 * D].reshape(NDEV, 4, 3 * D)
    dmod_cols = lax.dynamic_slice_in_dim(dmod_all, me * n_ada, n_ada, axis=2)
    g_ada = ada_grad(c_all, jnp.moveaxis(dmod_cols, 0, 1), "ada_grad")
    r_ada = reduce_adam(g_ada.reshape(1, 4 * D, n_ada), *(T["ada_w"].reshape(4 * D, n_ada) for T in (W, M, V)), 512, "adam_ada")
    ag, ad, anm, anv = (t.reshape(ada_w.shape) for t in r_ada)

    blocks = dict(
        a_w_in=jnp.stack([blocks_from_cols(g["a_w_in"][l], n_ain, GDN_COLS, f"a_w_in_blocks{l}") for l in range(2)], axis=1),
        b_w_in=jnp.stack([blocks_from_cols(g["b_w_in"][l], n_bin, FOX_COLS, f"b_w_in_blocks{l}") for l in range(2)], axis=1),
        a_w_out=_blocks_from_full(g["a_w_out"], a_w_out.shape, 1).astype(bf16),
        b_w_out=_blocks_from_full(g["b_w_out"], b_w_out.shape, 1).astype(bf16),
        a_conv_w=_blocks_from_full(g["a_conv_w"], a_conv_w.shape, 2))
    views = {n: (NDEV, -1, W[n].shape[-1]) for n in BIG}
    recv = all_to_all([blocks[n].reshape(views[n]) for n in BIG], "scatter_grads")
    big = {}
    for n, r in zip(BIG, recv):
        res = reduce_adam(r, *(T[n].reshape(r.shape[1:]) for T in (W, M, V)), min(256, r.shape[1]), f"adam_{n}")
        big[n] = [t.reshape(W[n].shape) for t in res]

    outs = {}
    for idx, (k, sm_l, ada_t) in enumerate((("grad", sg, ag), ("delta", sd, ad), ("new_m", snm, anm), ("new_v", snv, anv))):
        d = dict(zip(SMALL, sm_l))
        d.update({n: big[n][idx] for n in BIG})
        d["ada_w"] = ada_t
        outs[k] = d
    order = ("norm_w", "ada_w", "ada_b", "a_w_in", "a_conv_w", "a_A_log", "a_dt_bias", "a_norm_w", "a_w_out", "b_w_in", "b_f_bias",
             "b_qn_w", "b_kn_w", "b_w_out", "final_norm_w")
    return (loss, dx[None], *[outs["grad"][n] for n in order], *[outs["delta"][n] for n in order],
            *[outs["new_m"][n] for n in order], *[outs["new_v"][n] for n in order])
```

```python
import functools

import jax
import jax.numpy as jnp
from jax import lax
from jax.experimental import pallas as pl
from jax.experimental.pallas import tpu as pltpu

f32 = jnp.float32
bf16 = jnp.bfloat16
SDS = jax.ShapeDtypeStruct

EPS = 1e-6
CHUNK = 64
HD = 128
GDN_QK_HEADS = 8
GDN_V_HEADS = 16
GDN_QK_W = GDN_QK_HEADS * HD
GDN_V_W = GDN_V_HEADS * HD
GDN_CONV = 2 * GDN_QK_W + GDN_V_W
GDN_IN = GDN_CONV + GDN_V_W + 2 * GDN_V_HEADS
GDN_IN_PAD = GDN_CONV + GDN_V_W + 256
GDN_TN = 640
FOX_H = 16
FOX_D = 64
FOX_W = FOX_H * FOX_D
FOX_IN = 4 * FOX_W + FOX_H
FOX_IN_PAD = 4 * FOX_W + 128
FOX_TN = 1408
FOX_PW = FOX_H * 128
NDEV = 8
MESH_AXES = ("x", "y", "c")
NEG = -1e30

ADAM_LR = 0.001
ADAM_B1 = 0.9
ADAM_B2 = 0.999
ADAM_EPS = 1e-08
ADAM_WD = 0.01
ADAM_STEP = 10

VMEM_LIMIT = 56 * 1024 * 1024


def _cp(sem=None):
    return pltpu.CompilerParams(dimension_semantics=sem, vmem_limit_bytes=VMEM_LIMIT)


def _bdot(a, b, dims):
    return lax.dot_general(a.astype(bf16), b.astype(bf16), (dims, ((), ())), preferred_element_type=f32)


def _nn(a, b):
    return _bdot(a, b, ((1,), (0,)))


def _nt(a, b):
    return _bdot(a, b, ((1,), (1,)))


def _tn(a, b):
    return _bdot(a, b, ((0,), (0,)))


def _hdot(a, b, dims=((1,), (0,))):
    return lax.dot_general(a, b, (dims, ((), ())), precision=lax.Precision.HIGHEST, preferred_element_type=f32)


def _split2(a):
    hi = a.astype(bf16)
    return hi, (a - hi.astype(f32)).astype(bf16)


def _tdot(a, b, dims=((1,), (0,))):
    ah, al = _split2(a)
    bh, bl = _split2(b)
    dn = (dims, ((), ()))
    return (lax.dot_general(ah, bh, dn, preferred_element_type=f32) + lax.dot_general(ah, bl, dn, preferred_element_type=f32)
            + lax.dot_general(al, bh, dn, preferred_element_type=f32))


@jax.custom_vjp
def _mm(a, b):
    return _nn(a, b)


_mm.defvjp(lambda a, b: (_nn(a, b), (a, b)), lambda r, g: (_nt(g, r[1]), _tn(r[0], g)))


@jax.custom_vjp
def _mm_nt(a, b):
    return _nt(a, b)


_mm_nt.defvjp(lambda a, b: (_nt(a, b), (a, b)), lambda r, g: (_nn(g, r[1]), _tn(g, r[0])))


@jax.custom_vjp
def _mm_tn(a, b):
    return _tn(a, b)


_mm_tn.defvjp(lambda a, b: (_tn(a, b), (a, b)), lambda r, g: (_nt(r[1], g), _nn(r[0], g)))


def _silu(x):
    return x * jax.nn.sigmoid(x)


def _rms_mod(x, nw, scale, shift):
    r = lax.rsqrt(jnp.mean(x * x, axis=-1, keepdims=True) + EPS)
    return (x * r * nw) * (1.0 + scale) + shift


def _rows(S, want):
    return min(want, S)


def inproj_fwd(x, nw, scale, shift, W, tn, name):
    S, D = x.shape
    N = W.shape[1]
    tm = _rows(S, 512)

    def body(x_ref, nw_ref, sc_ref, sh_ref, w_ref, proj_ref, h_ref):
        @pl.when(pl.program_id(1) == 0)
        def _():
            h_ref[...] = _rms_mod(x_ref[...], nw_ref[...], sc_ref[...], sh_ref[...]).astype(bf16)

        proj_ref[...] = jnp.dot(h_ref[...], w_ref[...], preferred_element_type=f32)

    vec = pl.BlockSpec((1, D), lambda i, j: (0, 0))
    return pl.pallas_call(
        body, name=name, grid=(S // tm, N // tn),
        in_specs=[pl.BlockSpec((tm, D), lambda i, j: (i, 0)), vec, vec, vec, pl.BlockSpec((D, tn), lambda i, j: (0, j))],
        out_specs=[pl.BlockSpec((tm, tn), lambda i, j: (i, j)), pl.BlockSpec((tm, D), lambda i, j: (i, 0))],
        out_shape=[SDS((S, N), f32), SDS((S, D), bf16)],
        compiler_params=_cp(("arbitrary", "arbitrary")),
    )(x, nw, scale, shift, W)


def inproj_bwd_x(x, nw, scale, shift, W, dproj, dx_res, tn, name):
    S, D = x.shape
    N = W.shape[1]
    tm = _rows(S, 512)
    nj = N // tn

    def body(x_ref, nw_ref, sc_ref, sh_ref, w_ref, dp_ref, dxr_ref, dx_ref, dnw_ref, dsc_ref, dsh_ref, acc):
        i, j = pl.program_id(0), pl.program_id(1)

        @pl.when(j == 0)
        def _():
            acc[...] = jnp.zeros_like(acc)

        @pl.when((i == 0) & (j == 0))
        def _():
            dnw_ref[...] = jnp.zeros_like(dnw_ref)
            dsc_ref[...] = jnp.zeros_like(dsc_ref)
            dsh_ref[...] = jnp.zeros_like(dsh_ref)

        acc[...] += _nt(dp_ref[...], w_ref[...])

        @pl.when(j == nj - 1)
        def _():
            _, vjp = jax.vjp(_rms_mod, x_ref[...], nw_ref[...], sc_ref[...], sh_ref[...])
            dx, dnw, dsc, dsh = vjp(acc[...])
            dx_ref[...] = dxr_ref[...] + dx
            dnw_ref[...] += dnw
            dsc_ref[...] += dsc
            dsh_ref[...] += dsh

    vec = pl.BlockSpec((1, D), lambda i, j: (0, 0))
    row = pl.BlockSpec((tm, D), lambda i, j: (i, 0))
    return pl.pallas_call(
        body, name=name, grid=(S // tm, nj),
        in_specs=[row, vec, vec, vec, pl.BlockSpec((D, tn), lambda i, j: (0, j)), pl.BlockSpec((tm, tn), lambda i, j: (i, j)), row],
        out_specs=[row, vec, vec, vec],
        out_shape=[SDS((S, D), f32), SDS((1, D), f32), SDS((1, D), f32), SDS((1, D), f32)],
        scratch_shapes=[pltpu.VMEM((tm, D), f32)],
        compiler_params=_cp(("arbitrary", "arbitrary")),
    )(x, nw, scale, shift, W, dproj, dx_res)


def matmul_tn(a, b, tn, name):
    S, K = a.shape
    N = b.shape[1]
    tm = _rows(S, 512)
    ni = S // tm

    def body(a_ref, b_ref, o_ref):
        @pl.when(pl.program_id(1) == 0)
        def _():
            o_ref[...] = jnp.zeros_like(o_ref)

        o_ref[...] += _tn(a_ref[...], b_ref[...])

    return pl.pallas_call(
        body, name=name, grid=(N // tn, ni),
        in_specs=[pl.BlockSpec((tm, K), lambda j, i: (i, 0)), pl.BlockSpec((tm, tn), lambda j, i: (i, j))],
        out_specs=pl.BlockSpec((K, tn), lambda j, i: (0, j)),
        out_shape=SDS((K, N), f32),
        compiler_params=_cp(("arbitrary", "arbitrary")),
    )(a, b)


def _conv_taps(xs, w, n_out):
    taps = []
    for j in range(4):
        s = 3 - j
        sh = xs if s == 0 else pltpu.roll(xs, s, axis=0)
        taps.append(sh[8:8 + n_out])
    conv = taps[0] * w[0] + taps[1] * w[1] + taps[2] * w[2] + taps[3] * w[3]
    return taps, conv


def _act_norm(conv, mul_norm, mul_plain):
    s = _silu(conv)
    r = lax.rsqrt(jnp.sum(s * s, axis=-1, keepdims=True) + EPS)
    return s * (mul_norm * r + mul_plain)


def _gdn_prep_mults(j):
    is_q = j < GDN_QK_HEADS
    is_k = (j >= GDN_QK_HEADS) & (j < 2 * GDN_QK_HEADS)
    mul_norm = jnp.where(is_q, HD ** -0.5, jnp.where(is_k, 1.0, 0.0)).astype(f32)
    mul_plain = jnp.where(is_q | is_k, 0.0, 1.0).astype(f32)
    return mul_norm, mul_plain


def gdn_prep_fwd(proj, conv_w, name):
    S = proj.shape[0]
    R = _rows(S, 512)

    def body(x_ref, w_ref, o_ref):
        mul_norm, mul_plain = _gdn_prep_mults(pl.program_id(0))
        w = [w_ref[j:j + 1, :] for j in range(4)]

        def piece(r, c):
            t0 = pl.multiple_of(r * R, R)
            cur = x_ref[pl.ds(t0, R), :]
            prev = x_ref[pl.ds(pl.multiple_of(jnp.maximum(t0 - 8, 0), 8), 8), :]
            prev = jnp.where(r == 0, 0.0, prev)
            _, conv = _conv_taps(jnp.concatenate([prev, cur], axis=0), w, R)
            o_ref[pl.ds(t0, R), :] = _act_norm(conv, mul_norm, mul_plain)
            return c

        lax.fori_loop(0, S // R, piece, 0)

    return pl.pallas_call(
        body, name=name, grid=(GDN_CONV // 128,),
        in_specs=[pl.BlockSpec((S, 128), lambda j: (0, j)), pl.BlockSpec((4, 128), lambda j: (0, j))],
        out_specs=pl.BlockSpec((S, 128), lambda j: (0, j)),
        out_shape=SDS((S, GDN_CONV), f32),
        compiler_params=_cp(("arbitrary",)),
    )(proj, conv_w)


def gdn_prep_bwd(proj, conv_w, dqkvc, name):
    S = proj.shape[0]
    R = _rows(S, 512)
    NP = S // R

    def body(x_ref, w_ref, dn_ref, dx_ref, dw_ref):
        mul_norm, mul_plain = _gdn_prep_mults(pl.program_id(0))
        w = [w_ref[j:j + 1, :] for j in range(4)]

        def piece(r, dw):
            t0 = pl.multiple_of(r * R, R)
            cur = x_ref[pl.ds(t0, R), :]
            prev = x_ref[pl.ds(pl.multiple_of(jnp.maximum(t0 - 8, 0), 8), 8), :]
            prev = jnp.where(r == 0, 0.0, prev)
            nxt0 = pl.multiple_of(jnp.minimum(t0 + R, S - 8), 8)
            nxt = x_ref[pl.ds(nxt0, 8), :]
            dn_cur = dn_ref[pl.ds(t0, R), :]
            dn_nxt = jnp.where(r == NP - 1, 0.0, dn_ref[pl.ds(nxt0, 8), :])
            xs = jnp.concatenate([prev, cur, nxt], axis=0)
            taps, conv = _conv_taps(xs, w, R + 8)
            dn = jnp.concatenate([dn_cur, dn_nxt], axis=0)
            _, vjp = jax.vjp(lambda c: _act_norm(c, mul_norm, mul_plain), conv)
            dxc = vjp(dn)[0]
            n = R + 8
            dx = dxc[0:R] * w[3]
            for j in range(3):
                s = 3 - j
                dx = dx + pltpu.roll(dxc, n - s, axis=0)[0:R] * w[j]
            dx_ref[pl.ds(t0, R), :] = dx
            return tuple(dw[j] + jnp.sum(dxc[0:R] * taps[j][0:R], axis=0, keepdims=True) for j in range(4))

        dw = lax.fori_loop(0, NP, piece, tuple(jnp.zeros((1, 128), f32) for _ in range(4)))
        for j in range(4):
            dw_ref[j:j + 1, :] = dw[j]

    col = pl.BlockSpec((S, 128), lambda j: (0, j))
    wsp = pl.BlockSpec((4, 128), lambda j: (0, j))
    return pl.pallas_call(
        body, name=name, grid=(GDN_CONV // 128,),
        in_specs=[col, wsp, col], out_specs=[col, wsp],
        out_shape=[SDS((S, GDN_CONV), f32), SDS((4, GDN_CONV), f32)],
        compiler_params=_cp(("arbitrary",)),
    )(proj, conv_w, dqkvc)


def _chunk_tril(R):
    ii = lax.broadcasted_iota(jnp.int32, (R, R), 0)
    jj = lax.broadcasted_iota(jnp.int32, (R, R), 1)
    return ((ii // CHUNK == jj // CHUNK) & (ii >= jj)).astype(f32)


def _gdn_gates(b, a, A_log, dt_bias, tril):
    beta = jax.nn.sigmoid(b)
    g = -jnp.exp(A_log) * jax.nn.softplus(a + dt_bias)
    return _hdot(tril, g), beta


_GDN_B_BLK = (GDN_CONV + GDN_V_W) // 128
_GDN_A_BLK = _GDN_B_BLK + 1


def gdn_gates_fwd(proj, A_log, dt_bias, name):
    S = proj.shape[0]
    R = _rows(S, 512)

    def body(b_ref, a_ref, al_ref, dt_ref, gc_ref, be_ref):
        gc, be = _gdn_gates(b_ref[...], a_ref[...], al_ref[...], dt_ref[...], _chunk_tril(R))
        gc_ref[...] = gc
        be_ref[...] = be

    vec = pl.BlockSpec((1, 128), lambda i: (0, 0))
    blk = pl.BlockSpec((R, 128), lambda i: (i, 0))
    return pl.pallas_call(
        body, name=name, grid=(S // R,),
        in_specs=[pl.BlockSpec((R, 128), lambda i: (i, _GDN_B_BLK)), pl.BlockSpec((R, 128), lambda i: (i, _GDN_A_BLK)), vec, vec],
        out_specs=[blk, blk], out_shape=[SDS((S, 128), f32), SDS((S, 128), f32)],
        compiler_params=_cp(("arbitrary",)),
    )(proj, proj, A_log, dt_bias)


def gdn_gates_bwd(proj, A_log, dt_bias, dgc, dbeta, name):
    S = proj.shape[0]
    R = _rows(S, 512)

    def body(b_ref, a_ref, al_ref, dt_ref, dgc_ref, dbe_ref, db_ref, da_ref, dal_ref, ddt_ref):
        @pl.when(pl.program_id(0) == 0)
        def _():
            dal_ref[...] = jnp.zeros_like(dal_ref)
            ddt_ref[...] = jnp.zeros_like(ddt_ref)

        tril = _chunk_tril(R)
        _, vjp = jax.vjp(lambda b, a, al, dt: _gdn_gates(b, a, al, dt, tril), b_ref[...], a_ref[...], al_ref[...], dt_ref[...])
        db, da, dal, ddt = vjp((dgc_ref[...], dbe_ref[...]))
        db_ref[...] = db
        da_ref[...] = da
        dal_ref[...] += dal
        ddt_ref[...] += ddt

    vec = pl.BlockSpec((1, 128), lambda i: (0, 0))
    blk = pl.BlockSpec((R, 128), lambda i: (i, 0))
    return pl.pallas_call(
        body, name=name, grid=(S // R,),
        in_specs=[pl.BlockSpec((R, 128), lambda i: (i, _GDN_B_BLK)), pl.BlockSpec((R, 128), lambda i: (i, _GDN_A_BLK)), vec, vec, blk, blk],
        out_specs=[blk, blk, vec, vec],
        out_shape=[SDS((S, 128), f32), SDS((S, 128), f32), SDS((1, 128), f32), SDS((1, 128), f32)],
        compiler_params=_cp(("arbitrary",)),
    )(proj, proj, A_log, dt_bias, dgc, dbeta)


@jax.custom_vjp
def _inv_given(L, T):
    return T


def _inv_given_bwd(T, ct):
    dL = -_nt(_tn(T, ct), T)
    return dL, jnp.zeros_like(T)


_inv_given.defvjp(lambda L, T: (T, T), _inv_given_bwd)


def _gdn_intra_pre(k, gcol, bcol):
    C = k.shape[0]
    ii = lax.broadcasted_iota(jnp.int32, (C, C), 0)
    jj = lax.broadcasted_iota(jnp.int32, (C, C), 1)
    grow = jnp.sum(jnp.where(ii == jj, gcol, 0.0), axis=0, keepdims=True)
    dec = jnp.exp(jnp.where(ii >= jj, gcol - grow, NEG))
    kb = k * bcol
    return dec, kb, jnp.where(ii > jj, _mm_nt(kb, k) * dec, 0.0)


def _gdn_intra_post(q, k, v, gcol, bcol, dec, kb, T):
    return _mm(T, v * bcol), _mm(T, kb * jnp.exp(gcol)), _mm_nt(q, k) * dec


def _gdn_intra(q, k, v, gcol, bcol, T_given):
    dec, kb, L = _gdn_intra_pre(k, gcol, bcol)
    return _gdn_intra_post(q, k, v, gcol, bcol, dec, kb, _inv_given(L, T_given))


def _neumann_inv_batched(Ls):
    n, C = len(Ls), Ls[0].shape[0]
    r0 = lax.broadcasted_iota(jnp.int32, (n * C, n * C), 0)
    c0 = lax.broadcasted_iota(jnp.int32, (n * C, n * C), 1)
    same = (r0 // C) == (c0 // C)

    def blockdiag(xcat):
        return jnp.where(same, jnp.concatenate([xcat] * n, axis=0), 0.0)

    M = jnp.concatenate(Ls, axis=1)
    eye = (lax.broadcasted_iota(jnp.int32, (C, n * C), 0) == (lax.broadcasted_iota(jnp.int32, (C, n * C), 1) & (C - 1))).astype(f32)
    P = eye - M
    k = 1
    while 2 * k < C:
        M = _tdot(M, blockdiag(M))
        P = P + _tdot(P, blockdiag(M))
        k *= 2
    return [P[:, h * C:(h + 1) * C] for h in range(n)]


def _gdn_scan(q, k, gcol, u, w, attn, S0):
    C = q.shape[0]
    last = lax.broadcasted_iota(jnp.int32, (C, 1), 0) == C - 1
    glast = jnp.sum(jnp.where(last, gcol, 0.0), axis=0, keepdims=True)
    vn = u - _mm(w, S0)
    o = _mm(q * jnp.exp(gcol), S0) + _mm(attn, vn)
    S1 = S0 * jnp.exp(glast) + _mm_tn(k * jnp.exp(glast - gcol), vn)
    return o, S1


GDN_HG = 16


def _head_col(blk, lane, hv):
    return jnp.sum(jnp.where(lane == hv, blk, 0.0), axis=1, keepdims=True)


def _gdn_specs(NC, rv=None):
    ix = (lambda n: n) if rv is None else rv
    qs = pl.BlockSpec((CHUNK, GDN_QK_W), lambda n: (ix(n), 0))
    ks = pl.BlockSpec((CHUNK, GDN_QK_W), lambda n: (ix(n), 1))
    vs = pl.BlockSpec((CHUNK, GDN_V_W), lambda n: (ix(n), 1))
    g1 = pl.BlockSpec((CHUNK, 128), lambda n: (ix(n), 0))
    wide = pl.BlockSpec((CHUNK, GDN_V_W), lambda n: (ix(n), 0))
    sq = pl.BlockSpec((1, GDN_V_HEADS, CHUNK, CHUNK), lambda n: (ix(n), 0, 0, 0))
    st = pl.BlockSpec((1, GDN_V_HEADS, HD, HD), lambda n: (ix(n), 0, 0, 0))
    return qs, ks, vs, g1, wide, sq, st


def _group_loop(body_fn, init):
    if GDN_HG == GDN_V_HEADS:
        return body_fn(0, init)
    return lax.fori_loop(0, GDN_V_HEADS // GDN_HG, lambda g, c: body_fn(g * GDN_HG, c), init)


def _lane_off(head):
    return head * HD if isinstance(head, int) else pl.multiple_of(head * HD, HD)


def gdn_intra_fwd(qkvc, gc, beta, name):
    S = qkvc.shape[0]
    NC = S // CHUNK

    def body(q_ref, k_ref, v_ref, gc_ref, be_ref, u_ref, w_ref, at_ref, T_ref):
        gcb = gc_ref[...]
        beb = be_ref[...]
        lane = lax.broadcasted_iota(jnp.int32, (CHUNK, 128), 1)

        def group(hv0, c):
            pre = []
            for r in range(GDN_HG):
                hv = hv0 + r
                off = _lane_off(hv0 // 2 + r // 2)
                gcol, bcol = _head_col(gcb, lane, hv), _head_col(beb, lane, hv)
                pre.append((off, gcol, bcol) + _gdn_intra_pre(k_ref[:, pl.ds(off, HD)], gcol, bcol))
            Ts = []
            for b in range(0, GDN_HG, 4):
                Ts += _neumann_inv_batched([p[5] for p in pre[b:b + 4]])
            for r in range(GDN_HG):
                hv = hv0 + r
                offv = _lane_off(hv)
                off, gcol, bcol, dec, kb, _ = pre[r]
                u, w, attn = _gdn_intra_post(q_ref[:, pl.ds(off, HD)], k_ref[:, pl.ds(off, HD)], v_ref[:, pl.ds(offv, HD)],
                                             gcol, bcol, dec, kb, Ts[r])
                u_ref[:, pl.ds(offv, HD)] = u
                w_ref[:, pl.ds(offv, HD)] = w
                at_ref[0, hv] = attn
                T_ref[0, hv] = Ts[r]
            return c

        _group_loop(group, 0)

    qs, ks, vs, g1, wide, sq, _ = _gdn_specs(NC)
    return pl.pallas_call(
        body, name=name, grid=(NC,),
        in_specs=[qs, ks, vs, g1, g1], out_specs=[wide, wide, sq, sq],
        out_shape=[SDS((S, GDN_V_W), f32), SDS((S, GDN_V_W), f32),
                   SDS((NC, GDN_V_HEADS, CHUNK, CHUNK), f32), SDS((NC, GDN_V_HEADS, CHUNK, CHUNK), f32)],
        compiler_params=_cp(("arbitrary",)),
    )(qkvc, qkvc, qkvc, gc, beta)


def gdn_scan_fwd(qkvc, gc, u, w, attn, name):
    S = qkvc.shape[0]
    NC = S // CHUNK

    def body(q_ref, k_ref, gc_ref, u_ref, w_ref, at_ref, o_ref, st_ref, state):
        @pl.when(pl.program_id(0) == 0)
        def _():
            state[...] = jnp.zeros_like(state)

        gcb = gc_ref[...]
        lane = lax.broadcasted_iota(jnp.int32, (CHUNK, 128), 1)

        def group(hv0, c):
            for r in range(GDN_HG):
                hv = hv0 + r
                off = _lane_off(hv0 // 2 + r // 2)
                offv = _lane_off(hv)
                S0 = state[hv]
                o, S1 = _gdn_scan(q_ref[:, pl.ds(off, HD)], k_ref[:, pl.ds(off, HD)], _head_col(gcb, lane, hv),
                                  u_ref[:, pl.ds(offv, HD)], w_ref[:, pl.ds(offv, HD)], at_ref[0, hv], S0)
                o_ref[:, pl.ds(offv, HD)] = o
                st_ref[0, hv] = S0
                state[hv] = S1
            return c

        _group_loop(group, 0)

    qs, ks, _, g1, wide, sq, st = _gdn_specs(NC)
    return pl.pallas_call(
        body, name=name, grid=(NC,),
        in_specs=[qs, ks, g1, wide, wide, sq], out_specs=[wide, st],
        out_shape=[SDS((S, GDN_V_W), f32), SDS((NC, GDN_V_HEADS, HD, HD), f32)],
        scratch_shapes=[pltpu.VMEM((GDN_V_HEADS, HD, HD), f32)],
        compiler_params=_cp(("arbitrary",)),
    )(qkvc, qkvc, gc, u, w, attn)


def gdn_scan_bwd(qkvc, gc, u, w, attn, states, do, name):
    S = qkvc.shape[0]
    NC = S // CHUNK

    def body(q_ref, k_ref, gc_ref, u_ref, w_ref, at_ref, st_ref, do_ref,
             dq_ref, dk_ref, dgc_ref, du_ref, dw_ref, dat_ref, dstate):
        @pl.when(pl.program_id(0) == 0)
        def _():
            dstate[...] = jnp.zeros_like(dstate)

        gcb = gc_ref[...]
        lane = lax.broadcasted_iota(jnp.int32, (CHUNK, 128), 1)

        def group(hv0, dgc_acc):
            dqk = [[jnp.zeros((CHUNK, HD), f32), jnp.zeros((CHUNK, HD), f32)] for _ in range(GDN_HG // 2)]
            for r in range(GDN_HG):
                hv = hv0 + r
                off = _lane_off(hv0 // 2 + r // 2)
                offv = _lane_off(hv)
                _, vjp = jax.vjp(_gdn_scan, q_ref[:, pl.ds(off, HD)], k_ref[:, pl.ds(off, HD)], _head_col(gcb, lane, hv),
                                 u_ref[:, pl.ds(offv, HD)], w_ref[:, pl.ds(offv, HD)], at_ref[0, hv], st_ref[0, hv])
                dq_, dk_, dg_, du_, dw_, dat_, dS0 = vjp((do_ref[:, pl.ds(offv, HD)], dstate[hv]))
                dqk[r // 2][0] = dqk[r // 2][0] + dq_
                dqk[r // 2][1] = dqk[r // 2][1] + dk_
                du_ref[:, pl.ds(offv, HD)] = du_
                dw_ref[:, pl.ds(offv, HD)] = dw_
                dat_ref[0, hv] = dat_
                dstate[hv] = dS0
                dgc_acc = dgc_acc + jnp.where(lane == hv, dg_, 0.0)
            for p in range(GDN_HG // 2):
                off = _lane_off(hv0 // 2 + p)
                dq_ref[:, pl.ds(off, HD)] = dqk[p][0]
                dk_ref[:, pl.ds(off, HD)] = dqk[p][1]
            return dgc_acc

        dgc_ref[...] = _group_loop(group, jnp.zeros((CHUNK, 128), f32))

    qs, ks, _, g1, wide, sq, st = _gdn_specs(NC, lambda n: NC - 1 - n)
    dqs = pl.BlockSpec((CHUNK, GDN_QK_W), lambda n: (NC - 1 - n, 0))
    return pl.pallas_call(
        body, name=name, grid=(NC,),
        in_specs=[qs, ks, g1, wide, wide, sq, st, wide],
        out_specs=[dqs, dqs, g1, wide, wide, sq],
        out_shape=[SDS((S, GDN_QK_W), f32), SDS((S, GDN_QK_W), f32), SDS((S, 128), f32), SDS((S, GDN_V_W), f32),
                   SDS((S, GDN_V_W), f32), SDS((NC, GDN_V_HEADS, CHUNK, CHUNK), f32)],
        scratch_shapes=[pltpu.VMEM((GDN_V_HEADS, HD, HD), f32)],
        compiler_params=_cp(("arbitrary",)),
    )(qkvc, qkvc, gc, u, w, attn, states, do)


def gdn_intra_bwd(qkvc, gc, beta, Ts, du, dw, dattn, dq_s, dk_s, dgc_s, name):
    S = qkvc.shape[0]
    NC = S // CHUNK

    def body(q_ref, k_ref, v_ref, gc_ref, be_ref, T_ref, du_ref, dw_ref, dat_ref, dqs_ref, dks_ref, dgs_ref,
             dq_ref, dk_ref, dv_ref, dgc_ref, dbe_ref):
        gcb = gc_ref[...]
        beb = be_ref[...]
        lane = lax.broadcasted_iota(jnp.int32, (CHUNK, 128), 1)

        def group(hv0, carry):
            dgc_acc, dbe_acc = carry
            dqk = []
            for p in range(GDN_HG // 2):
                off = _lane_off(hv0 // 2 + p)
                dqk.append([dqs_ref[:, pl.ds(off, HD)], dks_ref[:, pl.ds(off, HD)]])
            for r in range(GDN_HG):
                hv = hv0 + r
                off = _lane_off(hv0 // 2 + r // 2)
                offv = _lane_off(hv)
                T = T_ref[0, hv]
                _, vjp = jax.vjp(lambda q_, k_, v_, g_, b_: _gdn_intra(q_, k_, v_, g_, b_, T)[:3],
                                 q_ref[:, pl.ds(off, HD)], k_ref[:, pl.ds(off, HD)], v_ref[:, pl.ds(offv, HD)],
                                 _head_col(gcb, lane, hv), _head_col(beb, lane, hv))
                dq_, dk_, dv_, dg_, db_ = vjp((du_ref[:, pl.ds(offv, HD)], dw_ref[:, pl.ds(offv, HD)], dat_ref[0, hv]))
                dqk[r // 2][0] = dqk[r // 2][0] + dq_
                dqk[r // 2][1] = dqk[r // 2][1] + dk_
                dv_ref[:, pl.ds(offv, HD)] = dv_
                dgc_acc = dgc_acc + jnp.where(lane == hv, dg_, 0.0)
                dbe_acc = dbe_acc + jnp.where(lane == hv, db_, 0.0)
            for p in range(GDN_HG // 2):
                off = _lane_off(hv0 // 2 + p)
                dq_ref[:, pl.ds(off, HD)] = dqk[p][0]
                dk_ref[:, pl.ds(off, HD)] = dqk[p][1]
            return dgc_acc, dbe_acc

        dgc, dbe = _group_loop(group, (dgs_ref[...], jnp.zeros((CHUNK, 128), f32)))
        dgc_ref[...] = dgc
        dbe_ref[...] = dbe

    qs, ks, vs, g1, wide, sq, _ = _gdn_specs(NC)
    dqs = pl.BlockSpec((CHUNK, GDN_QK_W), lambda n: (n, 0))
    dq, dk, dv, dgc, dbe = pl.pallas_call(
        body, name=name, grid=(NC,),
        in_specs=[qs, ks, vs, g1, g1, sq, wide, wide, sq, dqs, dqs, g1],
        out_specs=[dqs, dqs, wide, g1, g1],
        out_shape=[SDS((S, GDN_QK_W), f32), SDS((S, GDN_QK_W), f32), SDS((S, GDN_V_W), f32), SDS((S, 128), f32), SDS((S, 128), f32)],
        compiler_params=_cp(("arbitrary",)),
    )(qkvc, qkvc, qkvc, gc, beta, Ts, du, dw, dattn, dq_s, dk_s, dgc_s)
    return jnp.concatenate([dq, dk, dv], axis=1), dgc, dbe


def _gated_norm(o, z, nw):
    parts = []
    for h in range(GDN_V_HEADS):
        oh = o[:, h * HD:(h + 1) * HD]
        r = lax.rsqrt(jnp.mean(oh * oh, axis=-1, keepdims=True) + EPS)
        parts.append((oh * r * nw) * _silu(z[:, h * HD:(h + 1) * HD]))
    return jnp.concatenate(parts, axis=1)


def gdn_out_fwd(o, proj, nw, W, x, gate, name):
    S, D = x.shape
    tm = _rows(S, 256)

    def body(o_ref, z_ref, nw_ref, w_ref, x_ref, g_ref, xn_ref, y_ref, og_ref):
        og = _gated_norm(o_ref[...], z_ref[...], nw_ref[...]).astype(bf16)
        y = jnp.dot(og, w_ref[...], preferred_element_type=f32)
        og_ref[...] = og
        y_ref[...] = y
        xn_ref[...] = x_ref[...] + g_ref[...] * y

    row = pl.BlockSpec((tm, D), lambda i: (i, 0))
    wide = pl.BlockSpec((tm, GDN_V_W), lambda i: (i, 0))
    return pl.pallas_call(
        body, name=name, grid=(S // tm,),
        in_specs=[wide, pl.BlockSpec((tm, GDN_V_W), lambda i: (i, 2)), pl.BlockSpec((1, HD), lambda i: (0, 0)),
                  pl.BlockSpec((GDN_V_W, D), lambda i: (0, 0)), row, pl.BlockSpec((1, D), lambda i: (0, 0))],
        out_specs=[row, row, wide],
        out_shape=[SDS((S, D), f32), SDS((S, D), f32), SDS((S, GDN_V_W), bf16)],
        compiler_params=_cp(("arbitrary",)),
    )(o, proj, nw, W, x, gate)


def gdn_out_bwd(dxn, y, gate, o, proj, nw, W, name):
    S, D = dxn.shape
    tm = _rows(S, 256)

    def body(dx_ref, y_ref, g_ref, o_ref, z_ref, nw_ref, w_ref, dy_ref, dg_ref, do_ref, dz_ref, dnw_ref):
        @pl.when(pl.program_id(0) == 0)
        def _():
            dg_ref[...] = jnp.zeros_like(dg_ref)
            dnw_ref[...] = jnp.zeros_like(dnw_ref)

        dx = dx_ref[...]
        dy = dx * g_ref[...]
        dy_ref[...] = dy
        dg_ref[...] += jnp.sum(dx * y_ref[...], axis=0, keepdims=True)
        dog = _nt(dy, w_ref[...])
        _, vjp = jax.vjp(_gated_norm, o_ref[...], z_ref[...], nw_ref[...])
        do, dz, dnw = vjp(dog)
        do_ref[...] = do
        dz_ref[...] = dz
        dnw_ref[...] += dnw

    row = pl.BlockSpec((tm, D), lambda i: (i, 0))
    wide = pl.BlockSpec((tm, GDN_V_W), lambda i: (i, 0))
    vecd = pl.BlockSpec((1, D), lambda i: (0, 0))
    vech = pl.BlockSpec((1, HD), lambda i: (0, 0))
    return pl.pallas_call(
        body, name=name, grid=(S // tm,),
        in_specs=[row, row, vecd, wide, pl.BlockSpec((tm, GDN_V_W), lambda i: (i, 2)), vech, pl.BlockSpec((GDN_V_W, D), lambda i: (0, 0))],
        out_specs=[row, vecd, wide, wide, vech],
        out_shape=[SDS((S, D), f32), SDS((1, D), f32), SDS((S, GDN_V_W), f32), SDS((S, GDN_V_W), f32), SDS((1, HD), f32)],
        compiler_params=_cp(("arbitrary",)),
    )(dxn, y, gate, o, proj, nw, W)


def _rms_w(x, w):
    return (x * lax.rsqrt(jnp.mean(x * x, axis=-1, keepdims=True) + EPS)) * w


def _split3(c):
    hi = c.astype(bf16).astype(f32)
    r1 = c - hi
    mid = r1.astype(bf16).astype(f32)
    lo = (r1 - mid).astype(bf16).astype(f32)
    return hi, mid, lo


_FOX_F_BLK = 4 * FOX_W // 128


def fox_prep_fwd(proj, f_bias, qn_w, kn_w, name):
    S = proj.shape[0]
    tm = _rows(S, 256)

    def body(q_ref, k_ref, v_ref, f_ref, fb_ref, qw_ref, kw_ref, Q_ref, K_ref, V_ref, carry):
        @pl.when(pl.program_id(0) == 0)
        def _():
            carry[...] = jnp.zeros_like(carry)

        ii = lax.broadcasted_iota(jnp.int32, (tm, tm), 0)
        jj = lax.broadcasted_iota(jnp.int32, (tm, tm), 1)
        lf = jax.nn.log_sigmoid(f_ref[...] + fb_ref[...])
        cum = _hdot((ii >= jj).astype(f32), lf) + carry[...]
        carry[...] = cum[tm - 1:tm, :]
        lane = lax.broadcasted_iota(jnp.int32, (tm, FOX_D), 1)
        q, k, v = q_ref[...], k_ref[...], v_ref[...]
        for h in range(FOX_H):
            sl = slice(h * FOX_D, (h + 1) * FOX_D)
            hi, mid, lo = _split3(cum[:, h:h + 1])
            qn = _rms_w(q[:, sl], qw_ref[...]) * FOX_D ** -0.5
            kn = _rms_w(k[:, sl], kw_ref[...])
            eq = jnp.where(lane == 0, hi, jnp.where(lane == 1, mid, jnp.where(lane == 2, lo, jnp.where(lane < 6, 1.0, 0.0))))
            ek = jnp.where(lane < 3, 1.0, jnp.where(lane == 3, -hi, jnp.where(lane == 4, -mid, jnp.where(lane == 5, -lo, 0.0))))
            ev = jnp.where(lane == 0, 1.0, 0.0)
            Q_ref[:, h * 128:(h + 1) * 128] = jnp.concatenate([qn, eq], axis=1).astype(bf16)
            K_ref[:, h * 128:(h + 1) * 128] = jnp.concatenate([kn, ek], axis=1).astype(bf16)
            V_ref[:, h * 128:(h + 1) * 128] = jnp.concatenate([v[:, sl], ev], axis=1).astype(bf16)

    def colblk(c):
        return pl.BlockSpec((tm, FOX_W), lambda i: (i, c))

    pad = pl.BlockSpec((tm, FOX_PW), lambda i: (i, 0))
    return pl.pallas_call(
        body, name=name, grid=(S // tm,),
        in_specs=[colblk(0), colblk(1), colblk(2), pl.BlockSpec((tm, 128), lambda i: (i, _FOX_F_BLK)),
                  pl.BlockSpec((1, 128), lambda i: (0, 0)), pl.BlockSpec((1, FOX_D), lambda i: (0, 0)), pl.BlockSpec((1, FOX_D), lambda i: (0, 0))],
        out_specs=[pad, pad, pad],
        out_shape=[SDS((S, FOX_PW), bf16)] * 3,
        scratch_shapes=[pltpu.VMEM((1, 128), f32)],
        compiler_params=_cp(("arbitrary",)),
    )(proj, proj, proj, proj, f_bias, qn_w, kn_w)


def fox_prep_bwd(proj, f_bias, qn_w, kn_w, dQ, dK, dV, name):
    S = proj.shape[0]
    tm = _rows(S, 256)
    NB = S // tm

    def body(q_ref, k_ref, f_ref, fb_ref, qw_ref, kw_ref, dQ_ref, dK_ref, dV_ref,
             dq_ref, dk_ref, dv_ref, df_ref, dfb_ref, dqw_ref, dkw_ref, carry):
        @pl.when(pl.program_id(0) == 0)
        def _():
            carry[...] = jnp.zeros_like(carry)
            dfb_ref[...] = jnp.zeros_like(dfb_ref)
            dqw_ref[...] = jnp.zeros_like(dqw_ref)
            dkw_ref[...] = jnp.zeros_like(dkw_ref)

        q, k = q_ref[...], k_ref[...]
        lane128 = lax.broadcasted_iota(jnp.int32, (tm, 128), 1)
        dcum = jnp.zeros((tm, 128), f32)
        dqs, dks, dvs = [], [], []
        dqw = jnp.zeros((1, FOX_D), f32)
        dkw = jnp.zeros((1, FOX_D), f32)
        for h in range(FOX_H):
            sl = slice(h * FOX_D, (h + 1) * FOX_D)
            dQh = dQ_ref[:, h * 128:(h + 1) * 128]
            dKh = dK_ref[:, h * 128:(h + 1) * 128]
            _, vq = jax.vjp(lambda a, w: _rms_w(a, w) * FOX_D ** -0.5, q[:, sl], qw_ref[...])
            dqh, dw1 = vq(dQh[:, 0:FOX_D])
            _, vk = jax.vjp(_rms_w, k[:, sl], kw_ref[...])
            dkh, dw2 = vk(dKh[:, 0:FOX_D])
            dqs.append(dqh)
            dks.append(dkh)
            dvs.append(dV_ref[:, h * 128:h * 128 + FOX_D])
            dqw = dqw + dw1
            dkw = dkw + dw2
            dcum = dcum + jnp.where(lane128 == h, dQh[:, FOX_D:FOX_D + 1] - dKh[:, FOX_D + 3:FOX_D + 4], 0.0)
        dq_ref[...] = jnp.concatenate(dqs, axis=1)
        dk_ref[...] = jnp.concatenate(dks, axis=1)
        dv_ref[...] = jnp.concatenate(dvs, axis=1)
        ii = lax.broadcasted_iota(jnp.int32, (tm, tm), 0)
        jj = lax.broadcasted_iota(jnp.int32, (tm, tm), 1)
        dlf = _hdot((ii <= jj).astype(f32), dcum) + carry[...]
        carry[...] += jnp.sum(dcum, axis=0, keepdims=True)
        df = dlf * jax.nn.sigmoid(-(f_ref[...] + fb_ref[...]))
        df_ref[...] = df
        dfb_ref[...] += jnp.sum(df, axis=0, keepdims=True)
        dqw_ref[...] += dqw
        dkw_ref[...] += dkw

    rv = lambda i: NB - 1 - i

    def colblk(c):
        return pl.BlockSpec((tm, FOX_W), lambda i: (rv(i), c))

    pad = pl.BlockSpec((tm, FOX_PW), lambda i: (rv(i), 0))
    cmp_ = pl.BlockSpec((tm, FOX_W), lambda i: (rv(i), 0))
    fblk = pl.BlockSpec((tm, 128), lambda i: (rv(i), 0))
    v128 = pl.BlockSpec((1, 128), lambda i: (0, 0))
    v64 = pl.BlockSpec((1, FOX_D), lambda i: (0, 0))
    return pl.pallas_call(
        body, name=name, grid=(NB,),
        in_specs=[colblk(0), colblk(1), pl.BlockSpec((tm, 128), lambda i: (rv(i), _FOX_F_BLK)), v128, v64, v64, pad, pad, pad],
        out_specs=[cmp_, cmp_, cmp_, fblk, v128, v64, v64],
        out_shape=[SDS((S, FOX_W), f32)] * 3 + [SDS((S, 128), f32), SDS((1, 128), f32), SDS((1, FOX_D), f32), SDS((1, FOX_D), f32)],
        scratch_shapes=[pltpu.VMEM((1, 128), f32)],
        compiler_params=_cp(("arbitrary",)),
    )(proj, proj, proj, f_bias, qn_w, kn_w, dQ, dK, dV)


def _diag_mask(t):
    return lax.broadcasted_iota(jnp.int32, (t, t), 1) <= lax.broadcasted_iota(jnp.int32, (t, t), 0)


def fox_attn_fwd(Q, K, V, name):
    S = Q.shape[0]
    t = _rows(S, 512)

    def body(q_ref, k_ref, v_ref, o_ref, m_sc, acc_sc):
        i = pl.program_id(1)
        q = q_ref[...]
        m_sc[...] = jnp.full_like(m_sc, NEG)
        acc_sc[...] = jnp.zeros_like(acc_sc)

        def tile(j, diag):
            j0 = pl.multiple_of(j * t, t)
            s = _nt(q, k_ref[pl.ds(j0, t), :])
            if diag:
                s = jnp.where(_diag_mask(t), s, NEG)
            m = m_sc[...]
            m_new = jnp.maximum(m, jnp.max(s, axis=1, keepdims=True))
            p = jnp.exp(s - m_new)
            acc_sc[...] = acc_sc[...] * jnp.exp(m - m_new) + _nn(p, v_ref[pl.ds(j0, t), :])
            m_sc[...] = m_new

        def off_diag(j, c):
            tile(j, False)
            return c

        lax.fori_loop(0, i, off_diag, 0)
        tile(i, True)
        acc = acc_sc[...]
        l = acc[:, FOX_D:FOX_D + 1]
        lane = lax.broadcasted_iota(jnp.int32, (t, 128), 1)
        o_ref[...] = jnp.where(lane == FOX_D, m_sc[...] + jnp.log(l), acc / l)

    blk = pl.BlockSpec((t, 128), lambda h, i: (i, h))
    seq = pl.BlockSpec((S, 128), lambda h, i: (0, h))
    return pl.pallas_call(
        body, name=name, grid=(FOX_H, S // t),
        in_specs=[blk, seq, seq], out_specs=blk, out_shape=SDS((S, FOX_PW), f32),
        scratch_shapes=[pltpu.VMEM((t, 1), f32), pltpu.VMEM((t, 128), f32)],
        compiler_params=_cp(("arbitrary", "arbitrary")),
    )(Q, K, V)


def fox_attn_bwd(Q, K, V, dO, O, name):
    S = Q.shape[0]
    t = _rows(S, 512)
    nq = S // t

    def body(k_ref, v_ref, q_ref, do_ref, o_ref, dq_ref, dk_ref, dv_ref):
        j = pl.program_id(1)

        @pl.when(j == 0)
        def _():
            dq_ref[...] = jnp.zeros_like(dq_ref)

        dk_ref[...] = jnp.zeros_like(dk_ref)
        dv_ref[...] = jnp.zeros_like(dv_ref)
        k = k_ref[...]
        v = v_ref[...]

        def tile(i, diag):
            i0 = pl.multiple_of(i * t, t)
            q = q_ref[pl.ds(i0, t), :]
            do = do_ref[pl.ds(i0, t), :]
            s = _nt(q, k) - o_ref[pl.ds(i0, t), FOX_D:FOX_D + 1]
            if diag:
                s = jnp.where(_diag_mask(t), s, NEG)
            p = jnp.exp(s)
            dv_ref[...] += _tn(p, do)
            ds = (p * _nt(do, v)).astype(bf16)
            dk_ref[...] += _tn(ds, q)
            dq_ref[pl.ds(i0, t), :] += _nn(ds, k)

        tile(j, True)

        def off_diag(i, c):
            tile(i, False)
            return c

        lax.fori_loop(j + 1, nq, off_diag, 0)

    blk = pl.BlockSpec((t, 128), lambda h, j: (j, h))
    seq = pl.BlockSpec((S, 128), lambda h, j: (0, h))
    return pl.pallas_call(
        body, name=name, grid=(FOX_H, nq),
        in_specs=[blk, blk, seq, seq, seq], out_specs=[seq, blk, blk],
        out_shape=[SDS((S, FOX_PW), f32)] * 3,
        compiler_params=_cp(("arbitrary", "arbitrary")),
    )(K, V, Q, dO, O)


def fox_out_fwd(O, proj, W, x, gate, name):
    S, D = x.shape
    tm = _rows(S, 256)

    def body(o_ref, z_ref, w_ref, x_ref, g_ref, xn_ref, y_ref, og_ref):
        z = z_ref[...]
        og = jnp.concatenate([o_ref[:, h * 128:h * 128 + FOX_D] * _silu(z[:, h * FOX_D:(h + 1) * FOX_D]) for h in range(FOX_H)],
                             axis=1).astype(bf16)
        y = jnp.dot(og, w_ref[...], preferred_element_type=f32)
        og_ref[...] = og
        y_ref[...] = y
        xn_ref[...] = x_ref[...] + g_ref[...] * y

    row = pl.BlockSpec((tm, D), lambda i: (i, 0))
    cmp_ = pl.BlockSpec((tm, FOX_W), lambda i: (i, 0))
    return pl.pallas_call(
        body, name=name, grid=(S // tm,),
        in_specs=[pl.BlockSpec((tm, FOX_PW), lambda i: (i, 0)), pl.BlockSpec((tm, FOX_W), lambda i: (i, 3)),
                  pl.BlockSpec((FOX_W, D), lambda i: (0, 0)), row, pl.BlockSpec((1, D), lambda i: (0, 0))],
        out_specs=[row, row, cmp_],
        out_shape=[SDS((S, D), f32), SDS((S, D), f32), SDS((S, FOX_W), bf16)],
        compiler_params=_cp(("arbitrary",)),
    )(O, proj, W, x, gate)


def fox_out_bwd(dxn, y, gate, O, proj, W, name):
    S, D = dxn.shape
    tm = _rows(S, 256)

    def body(dx_ref, y_ref, g_ref, o_ref, z_ref, w_ref, dy_ref, dg_ref, dO_ref, dz_ref):
        @pl.when(pl.program_id(0) == 0)
        def _():
            dg_ref[...] = jnp.zeros_like(dg_ref)

        dx = dx_ref[...]
        dy = dx * g_ref[...]
        dy_ref[...] = dy
        dg_ref[...] += jnp.sum(dx * y_ref[...], axis=0, keepdims=True)
        dog = _nt(dy, w_ref[...])
        z = z_ref[...]
        lane = lax.broadcasted_iota(jnp.int32, (tm, FOX_D), 1)
        dzs = []
        for h in range(FOX_H):
            sl = slice(h * FOX_D, (h + 1) * FOX_D)
            zh = z[:, sl]
            sg = jax.nn.sigmoid(zh)
            oh = o_ref[:, h * 128:h * 128 + FOX_D]
            doh = dog[:, sl] * (zh * sg)
            delta = jnp.sum(doh * oh, axis=1, keepdims=True)
            dO_ref[:, h * 128:(h + 1) * 128] = jnp.concatenate([doh, jnp.where(lane == 0, -delta, 0.0)], axis=1).astype(bf16)
            dzs.append(dog[:, sl] * oh * (sg * (1.0 + zh * (1.0 - sg))))
        dz_ref[...] = jnp.concatenate(dzs, axis=1)

    row = pl.BlockSpec((tm, D), lambda i: (i, 0))
    vecd = pl.BlockSpec((1, D), lambda i: (0, 0))
    pad = pl.BlockSpec((tm, FOX_PW), lambda i: (i, 0))
    return pl.pallas_call(
        body, name=name, grid=(S // tm,),
        in_specs=[row, row, vecd, pad, pl.BlockSpec((tm, FOX_W), lambda i: (i, 3)), pl.BlockSpec((FOX_W, D), lambda i: (0, 0))],
        out_specs=[row, vecd, pad, pl.BlockSpec((tm, FOX_W), lambda i: (i, 0))],
        out_shape=[SDS((S, D), f32), SDS((1, D), f32), SDS((S, FOX_PW), bf16), SDS((S, FOX_W), f32)],
        compiler_params=_cp(("arbitrary",)),
    )(dxn, y, gate, O, proj, W)


def final_loss(x, fw, target, name):
    S, D = x.shape
    tm = _rows(S, 512)

    def body(x_ref, w_ref, t_ref, l_ref, dx_ref, dw_ref):
        @pl.when(pl.program_id(0) == 0)
        def _():
            l_ref[...] = jnp.zeros_like(l_ref)
            dw_ref[...] = jnp.zeros_like(dw_ref)

        out, vjp = jax.vjp(_rms_w, x_ref[...], w_ref[...])
        err = out - t_ref[...]
        l_ref[...] += 0.5 * jnp.sum(jnp.sum(err * err, axis=1, keepdims=True) * (1.0 / D), axis=0, keepdims=True)
        dx, dw = vjp(err * (1.0 / D))
        dx_ref[...] = dx
        dw_ref[...] += dw

    row = pl.BlockSpec((tm, D), lambda i: (i, 0))
    vec = pl.BlockSpec((1, D), lambda i: (0, 0))
    return pl.pallas_call(
        body, name=name, grid=(S // tm,),
        in_specs=[row, vec, row], out_specs=[pl.BlockSpec((1, 128), lambda i: (0, 0)), row, vec],
        out_shape=[SDS((1, 128), f32), SDS((S, D), f32), SDS((1, D), f32)],
        compiler_params=_cp(("arbitrary",)),
    )(x, fw, target)


def ada_fwd(c_all, ada_w, name):
    L, D, n = ada_w.shape

    def body(c_ref, w_ref, o_ref):
        cond = jnp.concatenate([_silu(c_ref[...]), jnp.zeros((8, D), f32)], axis=0)
        o_ref[0] = _nn(cond, w_ref[0])[0:8]

    return pl.pallas_call(
        body, name=name, grid=(L,),
        in_specs=[pl.BlockSpec((NDEV, D), lambda l: (0, 0)), pl.BlockSpec((1, D, n), lambda l: (l, 0, 0))],
        out_specs=pl.BlockSpec((1, NDEV, n), lambda l: (l, 0, 0)),
        out_shape=SDS((L, NDEV, n), f32),
        compiler_params=_cp(("arbitrary",)),
    )(c_all, ada_w)


def ada_grad(c_all, dmod, name):
    L, _, n = dmod.shape
    D = c_all.shape[1]

    def body(c_ref, d_ref, o_ref):
        cond = jnp.concatenate([_silu(c_ref[...]), jnp.zeros((8, D), f32)], axis=0)
        dm = jnp.concatenate([d_ref[0], jnp.zeros((8, n), f32)], axis=0)
        o_ref[0] = _tn(cond, dm)

    return pl.pallas_call(
        body, name=name, grid=(L,),
        in_specs=[pl.BlockSpec((NDEV, D), lambda l: (0, 0)), pl.BlockSpec((1, NDEV, n), lambda l: (l, 0, 0))],
        out_specs=pl.BlockSpec((1, D, n), lambda l: (l, 0, 0)),
        out_shape=SDS((L, D, n), f32),
        compiler_params=_cp(("arbitrary",)),
    )(c_all, dmod)


def reduce_adam(parts, w, m, v, tr, name):
    n, R, C = parts.shape
    c1 = 1.0 / (1.0 - ADAM_B1 ** ADAM_STEP)
    c2 = 1.0 / (1.0 - ADAM_B2 ** ADAM_STEP)

    def body(p_ref, w_ref, m_ref, v_ref, g_ref, d_ref, nm_ref, nv_ref):
        g = p_ref[0].astype(f32)
        for s in range(1, n):
            g = g + p_ref[s].astype(f32)
        nm = ADAM_B1 * m_ref[...] + (1.0 - ADAM_B1) * g
        nv = ADAM_B2 * v_ref[...] + (1.0 - ADAM_B2) * (g * g)
        g_ref[...] = g
        nm_ref[...] = nm
        nv_ref[...] = nv
        d_ref[...] = -ADAM_LR * ((nm * c1) / (jnp.sqrt(nv * c2) + ADAM_EPS) + ADAM_WD * w_ref[...])

    blk = pl.BlockSpec((tr, C), lambda i: (i, 0))
    return pl.pallas_call(
        body, name=name, grid=(R // tr,),
        in_specs=[pl.BlockSpec((n, tr, C), lambda i: (0, i, 0)), blk, blk, blk],
        out_specs=[blk] * 4, out_shape=[SDS((R, C), f32)] * 4,
        compiler_params=_cp(("arbitrary",)),
    )(parts, w, m, v)


def _my_pos():
    return lax.axis_index("x"), lax.axis_index("y"), lax.axis_index("c")


def all_gather(xs, name):
    n = len(xs)

    def body(*refs):
        x_refs, out_refs = refs[:n], refs[n:2 * n]
        send_sems, recv_sems, local_sems = refs[2 * n:]
        x_, y_, c_ = _my_pos()
        me, sibling = (x_, y_, c_), (x_, y_, 1 - c_)
        chips = [(1 - x_, y_), (x_, 1 - y_), (1 - x_, 1 - y_)]

        def rows(a, px, py, pc):
            return out_refs[a].at[4 * px + 2 * py + pc]

        def copy(a, k, block, to, own=False):
            return pltpu.make_async_remote_copy(
                src_ref=x_refs[a] if own else rows(a, *block), dst_ref=rows(a, *block),
                send_sem=send_sems.at[k, a], recv_sem=recv_sems.at[k, a], device_id=to, device_id_type=pl.DeviceIdType.MESH)

        mine = [pltpu.make_async_copy(x_refs[a], rows(a, *me), local_sems.at[a]) for a in range(n)]
        for cp in mine:
            cp.start()
        first = []
        for a in range(n):
            first.append(copy(a, 0, me, sibling, own=True))
            first += [copy(a, 1 + j, me, (*chip, c_), own=True) for j, chip in enumerate(chips)]
        for cp in first:
            cp.start()
        passed = []
        for j, chip in enumerate(chips):
            for a in range(n):
                copy(a, 1 + j, (*chip, c_), me).wait_recv()
                cp = copy(a, 4 + j, (*chip, c_), sibling)
                cp.start()
                passed.append(cp)
        for a in range(n):
            copy(a, 0, sibling, me).wait_recv()
            for j, chip in enumerate(chips):
                copy(a, 4 + j, (*chip, 1 - c_), me).wait_recv()
        for cp in first + passed:
            cp.wait_send()
        for cp in mine:
            cp.wait()

    any_ = pl.BlockSpec(memory_space=pl.ANY)
    return pl.pallas_call(
        body, name=name, out_shape=[SDS((NDEV,) + x.shape, x.dtype) for x in xs],
        in_specs=[any_] * n, out_specs=[any_] * n,
        scratch_shapes=[pltpu.SemaphoreType.DMA((7, n)), pltpu.SemaphoreType.DMA((7, n)), pltpu.SemaphoreType.DMA((n,))],
    )(*xs)


def all_to_all(xs, name):
    n = len(xs)

    def body(*refs):
        x_refs, out_refs = refs[:n], refs[n:2 * n]
        send_sems, recv_sems, local_sems = refs[2 * n:]
        x_, y_, c_ = _my_pos()
        me = 4 * x_ + 2 * y_ + c_
        local = [pltpu.make_async_copy(x_refs[a].at[me], out_refs[a].at[me], local_sems.at[a]) for a in range(n)]
        for cp in local:
            cp.start()
        copies = []
        for rel in range(1, NDEV):
            px = (x_ + ((rel >> 2) & 1)) % 2
            py = (y_ + ((rel >> 1) & 1)) % 2
            pc = (c_ + (rel & 1)) % 2
            for a in range(n):
                cp = pltpu.make_async_remote_copy(
                    src_ref=x_refs[a].at[4 * px + 2 * py + pc], dst_ref=out_refs[a].at[me],
                    send_sem=send_sems.at[rel - 1, a], recv_sem=recv_sems.at[rel - 1, a],
                    device_id=(px, py, pc), device_id_type=pl.DeviceIdType.MESH)
                cp.start()
                copies.append(cp)
        for cp in copies:
            cp.wait()
        for cp in local:
            cp.wait()

    any_ = pl.BlockSpec(memory_space=pl.ANY)
    return pl.pallas_call(
        body, name=name, out_shape=[SDS(x.shape, x.dtype) for x in xs],
        in_specs=[any_] * n, out_specs=[any_] * n,
        scratch_shapes=[pltpu.SemaphoreType.DMA((7, n)), pltpu.SemaphoreType.DMA((7, n)), pltpu.SemaphoreType.DMA((n,))],
    )(*xs)


GDN_COLS = ((0, GDN_CONV + GDN_V_W, 0), (GDN_CONV + GDN_V_W, GDN_CONV + GDN_V_W + 16, GDN_CONV + GDN_V_W),
            (GDN_CONV + GDN_V_W + 16, GDN_IN, GDN_CONV + GDN_V_W + 128))
FOX_COLS = ((0, FOX_IN, 0),)


def _col_pieces(d, per, cols):
    lo, hi = per * d, per * (d + 1)
    out = []
    for a, b, dst in cols:
        s, e = max(lo, a), min(hi, b)
        if s < e:
            out.append((s - lo, e - s, dst + s - a))
    return out


def cols_from_blocks(g, cols, n_out, name):
    _, L, R, C = g.shape
    tr = min(256, R)

    def body(g_ref, o_ref):
        o_ref[...] = jnp.zeros_like(o_ref)
        for d in range(NDEV):
            for off, ln, dst in _col_pieces(d, C, cols):
                o_ref[0, :, dst:dst + ln] = g_ref[d, 0, :, off:off + ln]

    return pl.pallas_call(
        body, name=name, grid=(L, R // tr),
        in_specs=[pl.BlockSpec((NDEV, 1, tr, C), lambda l, i: (0, l, i, 0))],
        out_specs=pl.BlockSpec((1, tr, n_out), lambda l, i: (l, i, 0)),
        out_shape=SDS((L, R, n_out), g.dtype),
        compiler_params=_cp(("arbitrary", "arbitrary")),
    )(g)


def blocks_from_cols(dw, C, cols, name):
    R, n_in = dw.shape
    tr = min(256, R)

    def body(x_ref, o_ref):
        for d in range(NDEV):
            for off, ln, src in _col_pieces(d, C, cols):
                o_ref[d, :, off:off + ln] = x_ref[:, src:src + ln].astype(bf16)

    return pl.pallas_call(
        body, name=name, grid=(R // tr,),
        in_specs=[pl.BlockSpec((tr, n_in), lambda i: (i, 0))],
        out_specs=pl.BlockSpec((NDEV, tr, C), lambda i: (0, i, 0)),
        out_shape=SDS((NDEV, R, C), bf16),
        compiler_params=_cp(("arbitrary",)),
    )(dw)


BIG = ("a_w_in", "a_conv_w", "a_w_out", "b_w_in", "b_w_out")
SMALL = ("norm_w", "ada_b", "a_A_log", "a_dt_bias", "a_norm_w", "b_f_bias", "b_qn_w", "b_kn_w", "final_norm_w")


def _pack_small(arrs):
    rows = []
    for a in arrs:
        fl = a.reshape(-1)
        pad = (-fl.shape[0]) % 128
        if pad:
            fl = jnp.concatenate([fl, jnp.zeros((pad,), fl.dtype)])
        rows.append(fl)
    flat = jnp.concatenate(rows)
    pad = (-flat.shape[0]) % (8 * 128)
    if pad:
        flat = jnp.concatenate([flat, jnp.zeros((pad,), flat.dtype)])
    return flat.reshape(-1, 128)


def _unpack(packed, shapes, align):
    flat = packed.reshape(-1)
    out, off = [], 0
    for shp in shapes:
        n = 1
        for d in shp:
            n *= d
        out.append(flat[off:off + n].reshape(shp))
        off += n + ((-n) % align)
    return out


def _full_from_gathered(g, shard_shape, axis):
    g = jnp.moveaxis(g, 0, axis)
    shp = list(shard_shape)
    shp[axis] *= NDEV
    return g.reshape(shp)


def _blocks_from_full(full, shard_shape, axis):
    shp = list(shard_shape)
    full = full.reshape(shp[:axis] + [NDEV, shp[axis]] + shp[axis + 1:])
    return jnp.moveaxis(full, axis, 0)


def _pad_lanes(v, n=128):
    v = v.reshape(1, -1)
    return jnp.concatenate([v, jnp.zeros((1, n - v.shape[1]), v.dtype)], axis=1)


def gdn_layer_fwd(x, mod, nw, W_in, conv_w, A_log, dt_bias, a_nw, W_out, tag):
    shift, scale, gate = mod
    proj, h = inproj_fwd(x, nw, scale, shift, W_in, GDN_TN, f"{tag}_inproj")
    qkvc = gdn_prep_fwd(proj, conv_w, f"{tag}_prep")
    gc, beta = gdn_gates_fwd(proj, A_log, dt_bias, f"{tag}_gates")
    u, w, attn, Ts = gdn_intra_fwd(qkvc, gc, beta, f"{tag}_intra")
    o, states = gdn_scan_fwd(qkvc, gc, u, w, attn, f"{tag}_scan")
    x_new, y, og = gdn_out_fwd(o, proj, a_nw, W_out, x, gate, f"{tag}_out")
    return x_new, (x, proj, h, qkvc, gc, beta, o, states, Ts, y, og, u, w, attn)


def gdn_layer_bwd(dxn, saved, mod, nw, W_in, conv_w, A_log, dt_bias, a_nw, W_out, tag):
    shift, scale, gate = mod
    x, proj, h, qkvc, gc, beta, o, states, Ts, y, og, u, w, attn = saved
    dy, dgate, do, dz, da_nw = gdn_out_bwd(dxn, y, gate, o, proj, a_nw, W_out, f"{tag}_out_bwd")
    dW_out = matmul_tn(og, dy, 512, f"{tag}_dwout")
    dq_s, dk_s, dgc_s, du, dw, dattn = gdn_scan_bwd(qkvc, gc, u, w, attn, states, do, f"{tag}_scan_bwd")
    dqkvc, dgc, dbeta = gdn_intra_bwd(qkvc, gc, beta, Ts, du, dw, dattn, dq_s, dk_s, dgc_s, f"{tag}_intra_bwd")
    db, da, dA_log, ddt = gdn_gates_bwd(proj, A_log, dt_bias, dgc, dbeta, f"{tag}_gates_bwd")
    dqkv, dconv_w = gdn_prep_bwd(proj, conv_w, dqkvc, f"{tag}_prep_bwd")
    dproj = jnp.concatenate([dqkv, dz, db, da], axis=1)
    dW_in = matmul_tn(h, dproj, GDN_TN, f"{tag}_dwin")
    dx, dnw, dscale, dshift = inproj_bwd_x(x, nw, scale, shift, W_in, dproj, dxn, GDN_TN, f"{tag}_inproj_bwd")
    grads = dict(norm_w=dnw, W_in=dW_in, conv_w=dconv_w, A_log=dA_log[:, :16], dt_bias=ddt[:, :16], a_nw=da_nw, W_out=dW_out,
                 dmod=jnp.concatenate([dshift, dscale, dgate], axis=1))
    return dx, grads


def fox_layer_fwd(x, mod, nw, W_in, f_bias, qn_w, kn_w, W_out, tag):
    shift, scale, gate = mod
    proj, h = inproj_fwd(x, nw, scale, shift, W_in, FOX_TN, f"{tag}_inproj")
    Q, K, V = fox_prep_fwd(proj, f_bias, qn_w, kn_w, f"{tag}_prep")
    O = fox_attn_fwd(Q, K, V, f"{tag}_attn")
    x_new, y, og = fox_out_fwd(O, proj, W_out, x, gate, f"{tag}_out")
    return x_new, (x, proj, h, Q, K, V, O, y, og)


def fox_layer_bwd(dxn, saved, mod, nw, W_in, f_bias, qn_w, kn_w, W_out, tag):
    shift, scale, gate = mod
    x, proj, h, Q, K, V, O, y, og = saved
    dy, dgate, dO, dz = fox_out_bwd(dxn, y, gate, O, proj, W_out, f"{tag}_out_bwd")
    dW_out = matmul_tn(og, dy, 512, f"{tag}_dwout")
    dQ, dK, dV = fox_attn_bwd(Q, K, V, dO, O, f"{tag}_attn_bwd")
    dq, dk, dv, df, dfb, dqw, dkw = fox_prep_bwd(proj, f_bias, qn_w, kn_w, dQ, dK, dV, f"{tag}_prep_bwd")
    dproj = jnp.concatenate([dq, dk, dv, dz, df], axis=1)
    dW_in = matmul_tn(h, dproj, FOX_TN, f"{tag}_dwin")
    dx, dnw, dscale, dshift = inproj_bwd_x(x, nw, scale, shift, W_in, dproj, dxn, FOX_TN, f"{tag}_inproj_bwd")
    grads = dict(norm_w=dnw, W_in=dW_in, f_bias=dfb[:, :16], qn_w=dqw, kn_w=dkw, W_out=dW_out,
                 dmod=jnp.concatenate([dshift, dscale, dgate], axis=1))
    return dx, grads


def device_step(x, mod_all, norm_w, full, small, final_norm_w, target):
    D = x.shape[1]
    mods = [(mod_all[i:i + 1, 0:D], mod_all[i:i + 1, D:2 * D], mod_all[i:i + 1, 2 * D:3 * D]) for i in range(4)]

    def a_args(j):
        return (full["a_w_in"][j], full["a_conv_w"][j], _pad_lanes(small["a_A_log"][j]), _pad_lanes(small["a_dt_bias"][j]),
                small["a_norm_w"][j:j + 1], full["a_w_out"][j])

    def b_args(j):
        return (full["b_w_in"][j], _pad_lanes(small["b_f_bias"][j]), small["b_qn_w"][j:j + 1], small["b_kn_w"][j:j + 1],
                full["b_w_out"][j])

    saved = []
    for i in range(4):
        j = i // 2
        if i % 2 == 0:
            x, sv = gdn_layer_fwd(x, mods[i], norm_w[i:i + 1], *a_args(j), tag=f"L{i}")
        else:
            x, sv = fox_layer_fwd(x, mods[i], norm_w[i:i + 1], *b_args(j), tag=f"L{i}")
        saved.append(sv)
    loss, dx, dfw = final_loss(x, final_norm_w.reshape(1, D), target, "final_loss")
    lg = [None] * 4
    for i in reversed(range(4)):
        j = i // 2
        if i % 2 == 0:
            dx, lg[i] = gdn_layer_bwd(dx, saved[i], mods[i], norm_w[i:i + 1], *a_args(j), tag=f"L{i}")
        else:
            dx, lg[i] = fox_layer_bwd(dx, saved[i], mods[i], norm_w[i:i + 1], *b_args(j), tag=f"L{i}")
    g = dict(
        norm_w=jnp.concatenate([lg[i]["norm_w"] for i in range(4)], axis=0),
        dmod=jnp.concatenate([lg[i]["dmod"] for i in range(4)], axis=0),
        a_w_in=[lg[i]["W_in"] for i in (0, 2)],
        a_conv_w=jnp.stack([lg[i]["conv_w"] for i in (0, 2)]),
        a_A_log=jnp.concatenate([lg[i]["A_log"] for i in (0, 2)], axis=0),
        a_dt_bias=jnp.concatenate([lg[i]["dt_bias"] for i in (0, 2)], axis=0),
        a_norm_w=jnp.concatenate([lg[i]["a_nw"] for i in (0, 2)], axis=0),
        a_w_out=jnp.stack([lg[i]["W_out"] for i in (0, 2)]),
        b_w_in=[lg[i]["W_in"] for i in (1, 3)],
        b_f_bias=jnp.concatenate([lg[i]["f_bias"] for i in (1, 3)], axis=0),
        b_qn_w=jnp.concatenate([lg[i]["qn_w"] for i in (1, 3)], axis=0),
        b_kn_w=jnp.concatenate([lg[i]["kn_w"] for i in (1, 3)], axis=0),
        b_w_out=jnp.stack([lg[i]["W_out"] for i in (1, 3)]),
        final_norm_w=dfw.reshape(-1),
    )
    return loss[0, 0], dx, g


def kernel(x, c, norm_w, ada_w, ada_b, a_w_in, a_conv_w, a_A_log, a_dt_bias, a_norm_w, a_w_out, b_w_in, b_f_bias, b_qn_w, b_kn_w, b_w_out, final_norm_w, loss_target, m_norm_w, m_ada_w, m_ada_b, m_a_w_in, m_a_conv_w, m_a_A_log, m_a_dt_bias, m_a_norm_w, m_a_w_out, m_b_w_in, m_b_f_bias, m_b_qn_w, m_b_kn_w, m_b_w_out, m_final_norm_w, v_norm_w, v_ada_w, v_ada_b, v_a_w_in, v_a_conv_w, v_a_A_log, v_a_dt_bias, v_a_norm_w, v_a_w_out, v_b_w_in, v_b_f_bias, v_b_qn_w, v_b_kn_w, v_b_w_out, v_final_norm_w):
    W = dict(norm_w=norm_w, ada_w=ada_w, ada_b=ada_b, a_w_in=a_w_in, a_conv_w=a_conv_w, a_A_log=a_A_log, a_dt_bias=a_dt_bias,
             a_norm_w=a_norm_w, a_w_out=a_w_out, b_w_in=b_w_in, b_f_bias=b_f_bias, b_qn_w=b_qn_w, b_kn_w=b_kn_w, b_w_out=b_w_out,
             final_norm_w=final_norm_w)
    M = dict(norm_w=m_norm_w, ada_w=m_ada_w, ada_b=m_ada_b, a_w_in=m_a_w_in, a_conv_w=m_a_conv_w, a_A_log=m_a_A_log,
             a_dt_bias=m_a_dt_bias, a_norm_w=m_a_norm_w, a_w_out=m_a_w_out, b_w_in=m_b_w_in, b_f_bias=m_b_f_bias, b_qn_w=m_b_qn_w,
             b_kn_w=m_b_kn_w, b_w_out=m_b_w_out, final_norm_w=m_final_norm_w)
    V = dict(norm_w=v_norm_w, ada_w=v_ada_w, ada_b=v_ada_b, a_w_in=v_a_w_in, a_conv_w=v_a_conv_w, a_A_log=v_a_A_log,
             a_dt_bias=v_a_dt_bias, a_norm_w=v_a_norm_w, a_w_out=v_a_w_out, b_w_in=v_b_w_in, b_f_bias=v_b_f_bias, b_qn_w=v_b_qn_w,
             b_kn_w=v_b_kn_w, b_w_out=v_b_w_out, final_norm_w=v_final_norm_w)
    S, D = x.shape[1], x.shape[2]
    me = 4 * lax.axis_index("x") + 2 * lax.axis_index("y") + lax.axis_index("c")
    small_shapes = [W[n].shape for n in SMALL]
    n_ain, n_bin = a_w_in.shape[2], b_w_in.shape[2]

    gath = all_gather([a_w_in.astype(bf16).reshape(-1, n_ain), a_w_out.astype(bf16).reshape(-1, D),
                       b_w_in.astype(bf16).reshape(-1, n_bin), b_w_out.astype(bf16).reshape(-1, D),
                       a_conv_w.reshape(8, -1), c.reshape(8, D // 8)], "gather_w")
    full = dict(
        a_w_in=cols_from_blocks(gath[0].reshape((NDEV,) + a_w_in.shape), GDN_COLS, GDN_IN_PAD, "a_w_in_cols"),
        b_w_in=cols_from_blocks(gath[2].reshape((NDEV,) + b_w_in.shape), FOX_COLS, FOX_IN_PAD, "b_w_in_cols"),
        a_w_out=_full_from_gathered(gath[1].reshape((NDEV,) + a_w_out.shape), a_w_out.shape, 1),
        b_w_out=_full_from_gathered(gath[3].reshape((NDEV,) + b_w_out.shape), b_w_out.shape, 1),
        a_conv_w=_full_from_gathered(gath[4].reshape((NDEV,) + a_conv_w.shape), a_conv_w.shape, 2))
    c_all = gath[5].reshape(NDEV, D)

    mod_part = ada_fwd(c_all, ada_w, "ada_fwd")
    n_ada = ada_w.shape[2]
    mod_g = all_gather([mod_part.reshape(4 * NDEV, n_ada)], "gather_mod")[0].reshape(NDEV, 4, NDEV, n_ada)
    mod_mine = lax.dynamic_index_in_dim(mod_g, me, axis=2, keepdims=False)
    mod_all = jnp.moveaxis(mod_mine, 0, 1).reshape(4, NDEV * n_ada) + ada_b

    loss, dx, g = device_step(x[0], mod_all, norm_w, full, W, final_norm_w, loss_target[0])
    loss = lax.psum(loss, MESH_AXES)

    g_small = dict(g, ada_b=g["dmod"])
    sp = _pack_small([g_small[n] for n in SMALL])
    sp_all = all_gather([sp], "gather_small")[0]
    sw, sm, sv = (_pack_small([T[n] for n in SMALL]) for T in (W, M, V))
    sg, sd, snm, snv = (_unpack(t, small_shapes, 128) for t in reduce_adam(sp_all, sw, sm, sv, sp.shape[0], "adam_small"))

    off_b = 0
    for n, shp in zip(SMALL, small_shapes):
        if n == "ada_b":
            break
        cnt = 1
        for d in shp:
            cnt *= d
        off_b += cnt + ((-cnt) % 128)
    dmod_all = sp_all.reshape(NDEV, -1)[:, off_b:off_b + 4 * 3 * D].reshape(NDEV, 4, 3 * D)
    dmod_cols = lax.dynamic_slice_in_dim(dmod_all, me * n_ada, n_ada, axis=2)
    g_ada = ada_grad(c_all, jnp.moveaxis(dmod_cols, 0, 1), "ada_grad")
    r_ada = reduce_adam(g_ada.reshape(1, 4 * D, n_ada), *(T["ada_w"].reshape(4 * D, n_ada) for T in (W, M, V)), 512, "adam_ada")
    ag, ad, anm, anv = (t.reshape(ada_w.shape) for t in r_ada)

    blocks = dict(
        a_w_in=jnp.stack([blocks_from_cols(g["a_w_in"][l], n_ain, GDN_COLS, f"a_w_in_blocks{l}") for l in range(2)], axis=1),
        b_w_in=jnp.stack([blocks_from_cols(g["b_w_in"][l], n_bin, FOX_COLS, f"b_w_in_blocks{l}") for l in range(2)], axis=1),
        a_w_out=_blocks_from_full(g["a_w_out"], a_w_out.shape, 1).astype(bf16),
        b_w_out=_blocks_from_full(g["b_w_out"], b_w_out.shape, 1).astype(bf16),
        a_conv_w=_blocks_from_full(g["a_conv_w"], a_conv_w.shape, 2))
    views = {n: (NDEV, -1, W[n].shape[-1]) for n in BIG}
    recv = all_to_all([blocks[n].reshape(views[n]) for n in BIG], "scatter_grads")
    big = {}
    for n, r in zip(BIG, recv):
        res = reduce_adam(r, *(T[n].reshape(r.shape[1:]) for T in (W, M, V)), min(256, r.shape[1]), f"adam_{n}")
        big[n] = [t.reshape(W[n].shape) for t in res]

    outs = {}
    for idx, (k, sm_l, ada_t) in enumerate((("grad", sg, ag), ("delta", sd, ad), ("new_m", snm, anm), ("new_v", snv, anv))):
        d = dict(zip(SMALL, sm_l))
        d.update({n: big[n][idx] for n in BIG})
        d["ada_w"] = ada_t
        outs[k] = d
    order = ("norm_w", "ada_w", "ada_b", "a_w_in", "a_conv_w", "a_A_log", "a_dt_bias", "a_norm_w", "a_w_out", "b_w_in", "b_f_bias",
             "b_qn_w", "b_kn_w", "b_w_out", "final_norm_w")
    return (loss, dx[None], *[outs["grad"][n] for n in order], *[outs["delta"][n] for n in order],
            *[outs["new_m"][n] for n in order], *[outs["new_v"][n] for n in order])
```

```python
import functools

import jax
import jax.numpy as jnp
from jax import lax
from jax.experimental import pallas as pl
from jax.experimental.pallas import tpu as pltpu

f32 = jnp.float32
bf16 = jnp.bfloat16
SDS = jax.ShapeDtypeStruct

EPS = 1e-6
CHUNK = 64
HD = 128
GDN_QK_HEADS = 8
GDN_V_HEADS = 16
GDN_QK_W = GDN_QK_HEADS * HD
GDN_V_W = GDN_V_HEADS * HD
GDN_CONV = 2 * GDN_QK_W + GDN_V_W
GDN_IN = GDN_CONV + GDN_V_W + 2 * GDN_V_HEADS
GDN_IN_PAD = GDN_CONV + GDN_V_W + 256
GDN_TN = 640
FOX_H = 16
FOX_D = 64
FOX_W = FOX_H * FOX_D
FOX_IN = 4 * FOX_W + FOX_H
FOX_IN_PAD = 4 * FOX_W + 128
FOX_TN = 1408
FOX_PW = FOX_H * 128
NDEV = 8
MESH_AXES = ("x", "y", "c")
NEG = -1e30

ADAM_LR = 0.001
ADAM_B1 = 0.9
ADAM_B2 = 0.999
ADAM_EPS = 1e-08
ADAM_WD = 0.01
ADAM_STEP = 10

VMEM_LIMIT = 56 * 1024 * 1024


def _cp(sem=None):
    return pltpu.CompilerParams(dimension_semantics=sem, vmem_limit_bytes=VMEM_LIMIT)


def _bdot(a, b, dims):
    return lax.dot_general(a.astype(bf16), b.astype(bf16), (dims, ((), ())), preferred_element_type=f32)


def _nn(a, b):
    return _bdot(a, b, ((1,), (0,)))


def _nt(a, b):
    return _bdot(a, b, ((1,), (1,)))


def _tn(a, b):
    return _bdot(a, b, ((0,), (0,)))


def _hdot(a, b, dims=((1,), (0,))):
    return lax.dot_general(a, b, (dims, ((), ())), precision=lax.Precision.HIGHEST, preferred_element_type=f32)


def _split2(a):
    hi = a.astype(bf16)
    return hi, (a - hi.astype(f32)).astype(bf16)


def _dot3(a, b):
    (ah, al), (bh, bl) = a, b
    return (jnp.dot(ah, bh, preferred_element_type=f32) + jnp.dot(ah, bl, preferred_element_type=f32)
            + jnp.dot(al, bh, preferred_element_type=f32))


@jax.custom_vjp
def _mm(a, b):
    return _nn(a, b)


_mm.defvjp(lambda a, b: (_nn(a, b), (a, b)), lambda r, g: (_nt(g, r[1]), _tn(r[0], g)))


@jax.custom_vjp
def _mm_nt(a, b):
    return _nt(a, b)


_mm_nt.defvjp(lambda a, b: (_nt(a, b), (a, b)), lambda r, g: (_nn(g, r[1]), _tn(g, r[0])))


@jax.custom_vjp
def _mm_tn(a, b):
    return _tn(a, b)


_mm_tn.defvjp(lambda a, b: (_tn(a, b), (a, b)), lambda r, g: (_nt(r[1], g), _nn(r[0], g)))


def _silu(x):
    return x * jax.nn.sigmoid(x)


def _rms_mod(x, nw, scale, shift):
    r = lax.rsqrt(jnp.mean(x * x, axis=-1, keepdims=True) + EPS)
    return (x * r * nw) * (1.0 + scale) + shift


def _rows(S, want):
    return min(want, S)


def inproj_fwd(x, nw, scale, shift, W, tn, name):
    S, D = x.shape
    N = W.shape[1]
    tm = _rows(S, 512)

    def body(x_ref, nw_ref, sc_ref, sh_ref, w_ref, proj_ref, h_ref):
        @pl.when(pl.program_id(1) == 0)
        def _():
            h_ref[...] = _rms_mod(x_ref[...], nw_ref[...], sc_ref[...], sh_ref[...]).astype(bf16)

        proj_ref[...] = jnp.dot(h_ref[...], w_ref[...], preferred_element_type=f32)

    vec = pl.BlockSpec((1, D), lambda i, j: (0, 0))
    return pl.pallas_call(
        body, name=name, grid=(S // tm, N // tn),
        in_specs=[pl.BlockSpec((tm, D), lambda i, j: (i, 0)), vec, vec, vec, pl.BlockSpec((D, tn), lambda i, j: (0, j))],
        out_specs=[pl.BlockSpec((tm, tn), lambda i, j: (i, j)), pl.BlockSpec((tm, D), lambda i, j: (i, 0))],
        out_shape=[SDS((S, N), f32), SDS((S, D), bf16)],
        compiler_params=_cp(("arbitrary", "arbitrary")),
    )(x, nw, scale, shift, W)


def inproj_bwd_x(x, nw, scale, shift, W, dproj, dx_res, tn, name):
    S, D = x.shape
    N = W.shape[1]
    tm = _rows(S, 512)
    nj = N // tn

    def body(x_ref, nw_ref, sc_ref, sh_ref, w_ref, dp_ref, dxr_ref, dx_ref, dnw_ref, dsc_ref, dsh_ref, acc):
        i, j = pl.program_id(0), pl.program_id(1)

        @pl.when(j == 0)
        def _():
            acc[...] = jnp.zeros_like(acc)

        @pl.when((i == 0) & (j == 0))
        def _():
            dnw_ref[...] = jnp.zeros_like(dnw_ref)
            dsc_ref[...] = jnp.zeros_like(dsc_ref)
            dsh_ref[...] = jnp.zeros_like(dsh_ref)

        acc[...] += _nt(dp_ref[...], w_ref[...])

        @pl.when(j == nj - 1)
        def _():
            _, vjp = jax.vjp(_rms_mod, x_ref[...], nw_ref[...], sc_ref[...], sh_ref[...])
            dx, dnw, dsc, dsh = vjp(acc[...])
            dx_ref[...] = dxr_ref[...] + dx
            dnw_ref[...] += dnw
            dsc_ref[...] += dsc
            dsh_ref[...] += dsh

    vec = pl.BlockSpec((1, D), lambda i, j: (0, 0))
    row = pl.BlockSpec((tm, D), lambda i, j: (i, 0))
    return pl.pallas_call(
        body, name=name, grid=(S // tm, nj),
        in_specs=[row, vec, vec, vec, pl.BlockSpec((D, tn), lambda i, j: (0, j)), pl.BlockSpec((tm, tn), lambda i, j: (i, j)), row],
        out_specs=[row, vec, vec, vec],
        out_shape=[SDS((S, D), f32), SDS((1, D), f32), SDS((1, D), f32), SDS((1, D), f32)],
        scratch_shapes=[pltpu.VMEM((tm, D), f32)],
        compiler_params=_cp(("arbitrary", "arbitrary")),
    )(x, nw, scale, shift, W, dproj, dx_res)


def matmul_tn(a, b, tn, name):
    S, K = a.shape
    N = b.shape[1]
    tm = _rows(S, 512)
    ni = S // tm

    def body(a_ref, b_ref, o_ref):
        @pl.when(pl.program_id(1) == 0)
        def _():
            o_ref[...] = jnp.zeros_like(o_ref)

        o_ref[...] += _tn(a_ref[...], b_ref[...])

    return pl.pallas_call(
        body, name=name, grid=(N // tn, ni),
        in_specs=[pl.BlockSpec((tm, K), lambda j, i: (i, 0)), pl.BlockSpec((tm, tn), lambda j, i: (i, j))],
        out_specs=pl.BlockSpec((K, tn), lambda j, i: (0, j)),
        out_shape=SDS((K, N), f32),
        compiler_params=_cp(("arbitrary", "arbitrary")),
    )(a, b)


def _conv_taps(xs, w, n_out):
    taps = []
    for j in range(4):
        s = 3 - j
        sh = xs if s == 0 else pltpu.roll(xs, s, axis=0)
        taps.append(sh[8:8 + n_out])
    conv = taps[0] * w[0] + taps[1] * w[1] + taps[2] * w[2] + taps[3] * w[3]
    return taps, conv


def _act_norm(conv, mul_norm, mul_plain):
    s = _silu(conv)
    r = lax.rsqrt(jnp.sum(s * s, axis=-1, keepdims=True) + EPS)
    return s * (mul_norm * r + mul_plain)


def _gdn_prep_mults(j):
    is_q = j < GDN_QK_HEADS
    is_k = (j >= GDN_QK_HEADS) & (j < 2 * GDN_QK_HEADS)
    mul_norm = jnp.where(is_q, HD ** -0.5, jnp.where(is_k, 1.0, 0.0)).astype(f32)
    mul_plain = jnp.where(is_q | is_k, 0.0, 1.0).astype(f32)
    return mul_norm, mul_plain


def gdn_prep_fwd(proj, conv_w, name):
    S = proj.shape[0]
    R = _rows(S, 512)

    def body(x_ref, w_ref, o_ref):
        mul_norm, mul_plain = _gdn_prep_mults(pl.program_id(0))
        w = [w_ref[j:j + 1, :] for j in range(4)]

        def piece(r, c):
            t0 = pl.multiple_of(r * R, R)
            cur = x_ref[pl.ds(t0, R), :]
            prev = x_ref[pl.ds(pl.multiple_of(jnp.maximum(t0 - 8, 0), 8), 8), :]
            prev = jnp.where(r == 0, 0.0, prev)
            _, conv = _conv_taps(jnp.concatenate([prev, cur], axis=0), w, R)
            o_ref[pl.ds(t0, R), :] = _act_norm(conv, mul_norm, mul_plain)
            return c

        lax.fori_loop(0, S // R, piece, 0)

    return pl.pallas_call(
        body, name=name, grid=(GDN_CONV // 128,),
        in_specs=[pl.BlockSpec((S, 128), lambda j: (0, j)), pl.BlockSpec((4, 128), lambda j: (0, j))],
        out_specs=pl.BlockSpec((S, 128), lambda j: (0, j)),
        out_shape=SDS((S, GDN_CONV), f32),
        compiler_params=_cp(("arbitrary",)),
    )(proj, conv_w)


def gdn_prep_bwd(proj, conv_w, dqkvc, name):
    S = proj.shape[0]
    R = _rows(S, 512)
    NP = S // R

    def body(x_ref, w_ref, dn_ref, dx_ref, dw_ref):
        mul_norm, mul_plain = _gdn_prep_mults(pl.program_id(0))
        w = [w_ref[j:j + 1, :] for j in range(4)]

        def piece(r, dw):
            t0 = pl.multiple_of(r * R, R)
            cur = x_ref[pl.ds(t0, R), :]
            prev = x_ref[pl.ds(pl.multiple_of(jnp.maximum(t0 - 8, 0), 8), 8), :]
            prev = jnp.where(r == 0, 0.0, prev)
            nxt0 = pl.multiple_of(jnp.minimum(t0 + R, S - 8), 8)
            nxt = x_ref[pl.ds(nxt0, 8), :]
            dn_cur = dn_ref[pl.ds(t0, R), :]
            dn_nxt = jnp.where(r == NP - 1, 0.0, dn_ref[pl.ds(nxt0, 8), :])
            xs = jnp.concatenate([prev, cur, nxt], axis=0)
            taps, conv = _conv_taps(xs, w, R + 8)
            dn = jnp.concatenate([dn_cur, dn_nxt], axis=0)
            _, vjp = jax.vjp(lambda c: _act_norm(c, mul_norm, mul_plain), conv)
            dxc = vjp(dn)[0]
            n = R + 8
            dx = dxc[0:R] * w[3]
            for j in range(3):
                s = 3 - j
                dx = dx + pltpu.roll(dxc, n - s, axis=0)[0:R] * w[j]
            dx_ref[pl.ds(t0, R), :] = dx
            return tuple(dw[j] + jnp.sum(dxc[0:R] * taps[j][0:R], axis=0, keepdims=True) for j in range(4))

        dw = lax.fori_loop(0, NP, piece, tuple(jnp.zeros((1, 128), f32) for _ in range(4)))
        for j in range(4):
            dw_ref[j:j + 1, :] = dw[j]

    col = pl.BlockSpec((S, 128), lambda j: (0, j))
    wsp = pl.BlockSpec((4, 128), lambda j: (0, j))
    return pl.pallas_call(
        body, name=name, grid=(GDN_CONV // 128,),
        in_specs=[col, wsp, col], out_specs=[col, wsp],
        out_shape=[SDS((S, GDN_CONV), f32), SDS((4, GDN_CONV), f32)],
        compiler_params=_cp(("arbitrary",)),
    )(proj, conv_w, dqkvc)


def _chunk_tril(R):
    ii = lax.broadcasted_iota(jnp.int32, (R, R), 0)
    jj = lax.broadcasted_iota(jnp.int32, (R, R), 1)
    return ((ii // CHUNK == jj // CHUNK) & (ii >= jj)).astype(f32)


def _gdn_gates(b, a, A_log, dt_bias, tril):
    beta = jax.nn.sigmoid(b)
    g = -jnp.exp(A_log) * jax.nn.softplus(a + dt_bias)
    return _hdot(tril, g), beta


_GDN_B_BLK = (GDN_CONV + GDN_V_W) // 128
_GDN_A_BLK = _GDN_B_BLK + 1


def gdn_gates_fwd(proj, A_log, dt_bias, name):
    S = proj.shape[0]
    R = _rows(S, 512)

    def body(b_ref, a_ref, al_ref, dt_ref, gc_ref, be_ref):
        gc, be = _gdn_gates(b_ref[...], a_ref[...], al_ref[...], dt_ref[...], _chunk_tril(R))
        gc_ref[...] = gc
        be_ref[...] = be

    vec = pl.BlockSpec((1, 128), lambda i: (0, 0))
    blk = pl.BlockSpec((R, 128), lambda i: (i, 0))
    return pl.pallas_call(
        body, name=name, grid=(S // R,),
        in_specs=[pl.BlockSpec((R, 128), lambda i: (i, _GDN_B_BLK)), pl.BlockSpec((R, 128), lambda i: (i, _GDN_A_BLK)), vec, vec],
        out_specs=[blk, blk], out_shape=[SDS((S, 128), f32), SDS((S, 128), f32)],
        compiler_params=_cp(("arbitrary",)),
    )(proj, proj, A_log, dt_bias)


def gdn_gates_bwd(proj, A_log, dt_bias, dgc, dbeta, name):
    S = proj.shape[0]
    R = _rows(S, 512)

    def body(b_ref, a_ref, al_ref, dt_ref, dgc_ref, dbe_ref, db_ref, da_ref, dal_ref, ddt_ref):
        @pl.when(pl.program_id(0) == 0)
        def _():
            dal_ref[...] = jnp.zeros_like(dal_ref)
            ddt_ref[...] = jnp.zeros_like(ddt_ref)

        tril = _chunk_tril(R)
        _, vjp = jax.vjp(lambda b, a, al, dt: _gdn_gates(b, a, al, dt, tril), b_ref[...], a_ref[...], al_ref[...], dt_ref[...])
        db, da, dal, ddt = vjp((dgc_ref[...], dbe_ref[...]))
        db_ref[...] = db
        da_ref[...] = da
        dal_ref[...] += dal
        ddt_ref[...] += ddt

    vec = pl.BlockSpec((1, 128), lambda i: (0, 0))
    blk = pl.BlockSpec((R, 128), lambda i: (i, 0))
    return pl.pallas_call(
        body, name=name, grid=(S // R,),
        in_specs=[pl.BlockSpec((R, 128), lambda i: (i, _GDN_B_BLK)), pl.BlockSpec((R, 128), lambda i: (i, _GDN_A_BLK)), vec, vec, blk, blk],
        out_specs=[blk, blk, vec, vec],
        out_shape=[SDS((S, 128), f32), SDS((S, 128), f32), SDS((1, 128), f32), SDS((1, 128), f32)],
        compiler_params=_cp(("arbitrary",)),
    )(proj, proj, A_log, dt_bias, dgc, dbeta)


@jax.custom_vjp
def _inv_given(L, T):
    return T


def _inv_given_bwd(T, ct):
    dL = -_nt(_tn(T, ct), T)
    return dL, jnp.zeros_like(T)


_inv_given.defvjp(lambda L, T: (T, T), _inv_given_bwd)


REP = GDN_V_HEADS // GDN_QK_HEADS


def _gdn_intra_all(qs, ks, vs, gcols, bcols, Ts=None):
    H = len(vs)
    C = vs[0].shape[0]
    ii = lax.broadcasted_iota(jnp.int32, (C, C), 0)
    jj = lax.broadcasted_iota(jnp.int32, (C, C), 1)
    grows = [jnp.sum(jnp.where(ii == jj, g, 0.0), axis=0, keepdims=True) for g in gcols]
    decs = [jnp.exp(jnp.where(ii >= jj, gcols[h] - grows[h], NEG)) for h in range(H)]
    kbs = [ks[h // REP] * bcols[h] for h in range(H)]
    As = [_mm_nt(kbs[h], ks[h // REP]) for h in range(H)]
    Ls = [jnp.where(ii > jj, As[h] * decs[h], 0.0) for h in range(H)]
    if Ts is None:
        T = _neumann_inv_batched(Ls)
    else:
        T = [_inv_given(Ls[h], Ts[h]) for h in range(H)]
    us = [_mm(T[h], vs[h] * bcols[h]) for h in range(H)]
    ws = [_mm(T[h], kbs[h] * jnp.exp(gcols[h])) for h in range(H)]
    qk = [_mm_nt(qs[p], ks[p]) for p in range(H // REP)]
    return us, ws, [qk[h // REP] * decs[h] for h in range(H)], T


def _neumann_inv_batched(Ls):
    n, C = 4, Ls[0].shape[0]
    r0 = lax.broadcasted_iota(jnp.int32, (n * C, n * C), 0)
    c0 = lax.broadcasted_iota(jnp.int32, (n * C, n * C), 1)
    same = (r0 // C) == (c0 // C)

    def blockdiag(split):
        return tuple(jnp.where(same, jnp.concatenate([x] * n, axis=0), jnp.zeros((), bf16)) for x in split)

    Ms = [jnp.concatenate(Ls[b:b + n], axis=1) for b in range(0, len(Ls), n)]
    eye = (lax.broadcasted_iota(jnp.int32, (C, n * C), 0) == (lax.broadcasted_iota(jnp.int32, (C, n * C), 1) & (C - 1))).astype(f32)
    Ps = [eye - M for M in Ms]
    Ss = [_split2(M) for M in Ms]
    Bs = [blockdiag(S) for S in Ss]
    k = 1
    while 2 * k < C:
        Ss = [_split2(_dot3(S, B)) for S, B in zip(Ss, Bs)]
        Bs = [blockdiag(S) for S in Ss]
        Ps = [P + _dot3(_split2(P), B) for P, B in zip(Ps, Bs)]
        k *= 2
    return [P[:, h * C:(h + 1) * C] for P in Ps for h in range(n)]


def _gdn_scan_all(qs, ks, gcols, us, ws, attns, S0s):
    H = len(us)
    C = us[0].shape[0]
    last = lax.broadcasted_iota(jnp.int32, (C, 1), 0) == C - 1
    glast = [jnp.sum(jnp.where(last, g, 0.0), axis=0, keepdims=True) for g in gcols]
    wS = [_mm(ws[h], S0s[h]) for h in range(H)]
    qS = [_mm(qs[h // REP] * jnp.exp(gcols[h]), S0s[h]) for h in range(H)]
    vn = [us[h] - wS[h] for h in range(H)]
    av = [_mm(attns[h], vn[h]) for h in range(H)]
    kv = [_mm_tn(ks[h // REP] * jnp.exp(glast[h] - gcols[h]), vn[h]) for h in range(H)]
    return [qS[h] + av[h] for h in range(H)], [S0s[h] * jnp.exp(glast[h]) + kv[h] for h in range(H)]


def _head_cols(blk):
    lane = lax.broadcasted_iota(jnp.int32, blk.shape, 1)
    return [jnp.sum(jnp.where(lane == h, blk, 0.0), axis=1, keepdims=True) for h in range(GDN_V_HEADS)]


def _head_lanes(cols):
    lane = lax.broadcasted_iota(jnp.int32, (cols[0].shape[0], 128), 1)
    out = jnp.zeros((cols[0].shape[0], 128), f32)
    for h, c in enumerate(cols):
        out = out + jnp.where(lane == h, c, 0.0)
    return out


def _heads(ref, n):
    return [ref[:, h * HD:(h + 1) * HD] for h in range(n)]


def _gdn_specs(NC, rv=None):
    ix = (lambda n: n) if rv is None else rv
    qs = pl.BlockSpec((CHUNK, GDN_QK_W), lambda n: (ix(n), 0))
    ks = pl.BlockSpec((CHUNK, GDN_QK_W), lambda n: (ix(n), 1))
    vs = pl.BlockSpec((CHUNK, GDN_V_W), lambda n: (ix(n), 1))
    g1 = pl.BlockSpec((CHUNK, 128), lambda n: (ix(n), 0))
    wide = pl.BlockSpec((CHUNK, GDN_V_W), lambda n: (ix(n), 0))
    sq = pl.BlockSpec((1, GDN_V_HEADS, CHUNK, CHUNK), lambda n: (ix(n), 0, 0, 0))
    st = pl.BlockSpec((1, GDN_V_HEADS, HD, HD), lambda n: (ix(n), 0, 0, 0))
    return qs, ks, vs, g1, wide, sq, st


def gdn_intra_fwd(qkvc, gc, beta, name):
    S = qkvc.shape[0]
    NC = S // CHUNK

    def body(q_ref, k_ref, v_ref, gc_ref, be_ref, u_ref, w_ref, at_ref, T_ref):
        us, ws, attns, Ts = _gdn_intra_all(_heads(q_ref, GDN_QK_HEADS), _heads(k_ref, GDN_QK_HEADS), _heads(v_ref, GDN_V_HEADS),
                                           _head_cols(gc_ref[...]), _head_cols(be_ref[...]))
        for h in range(GDN_V_HEADS):
            u_ref[:, h * HD:(h + 1) * HD] = us[h]
            w_ref[:, h * HD:(h + 1) * HD] = ws[h]
            at_ref[0, h] = attns[h]
            T_ref[0, h] = Ts[h]

    qs, ks, vs, g1, wide, sq, _ = _gdn_specs(NC)
    return pl.pallas_call(
        body, name=name, grid=(NC,),
        in_specs=[qs, ks, vs, g1, g1], out_specs=[wide, wide, sq, sq],
        out_shape=[SDS((S, GDN_V_W), f32), SDS((S, GDN_V_W), f32),
                   SDS((NC, GDN_V_HEADS, CHUNK, CHUNK), f32), SDS((NC, GDN_V_HEADS, CHUNK, CHUNK), f32)],
        compiler_params=_cp(("arbitrary",)),
    )(qkvc, qkvc, qkvc, gc, beta)


def gdn_scan_fwd(qkvc, gc, u, w, attn, name):
    S = qkvc.shape[0]
    NC = S // CHUNK

    def body(q_ref, k_ref, gc_ref, u_ref, w_ref, at_ref, o_ref, st_ref, state):
        @pl.when(pl.program_id(0) == 0)
        def _():
            state[...] = jnp.zeros_like(state)

        S0s = [state[h] for h in range(GDN_V_HEADS)]
        os_, S1s = _gdn_scan_all(_heads(q_ref, GDN_QK_HEADS), _heads(k_ref, GDN_QK_HEADS), _head_cols(gc_ref[...]),
                                 _heads(u_ref, GDN_V_HEADS), _heads(w_ref, GDN_V_HEADS),
                                 [at_ref[0, h] for h in range(GDN_V_HEADS)], S0s)
        for h in range(GDN_V_HEADS):
            o_ref[:, h * HD:(h + 1) * HD] = os_[h]
            st_ref[0, h] = S0s[h]
            state[h] = S1s[h]

    qs, ks, _, g1, wide, sq, st = _gdn_specs(NC)
    return pl.pallas_call(
        body, name=name, grid=(NC,),
        in_specs=[qs, ks, g1, wide, wide, sq], out_specs=[wide, st],
        out_shape=[SDS((S, GDN_V_W), f32), SDS((NC, GDN_V_HEADS, HD, HD), f32)],
        scratch_shapes=[pltpu.VMEM((GDN_V_HEADS, HD, HD), f32)],
        compiler_params=_cp(("arbitrary",)),
    )(qkvc, qkvc, gc, u, w, attn)


def gdn_scan_bwd(qkvc, gc, u, w, attn, states, do, name):
    S = qkvc.shape[0]
    NC = S // CHUNK

    def body(q_ref, k_ref, gc_ref, u_ref, w_ref, at_ref, st_ref, do_ref,
             dq_ref, dk_ref, dgc_ref, du_ref, dw_ref, dat_ref, dstate):
        @pl.when(pl.program_id(0) == 0)
        def _():
            dstate[...] = jnp.zeros_like(dstate)

        VH = range(GDN_V_HEADS)
        _, vjp = jax.vjp(_gdn_scan_all, _heads(q_ref, GDN_QK_HEADS), _heads(k_ref, GDN_QK_HEADS), _head_cols(gc_ref[...]),
                         _heads(u_ref, GDN_V_HEADS), _heads(w_ref, GDN_V_HEADS), [at_ref[0, h] for h in VH],
                         [st_ref[0, h] for h in VH])
        dqs, dks, dgs, dus, dws, dats, dS0s = vjp((_heads(do_ref, GDN_V_HEADS), [dstate[h] for h in VH]))
        for p in range(GDN_QK_HEADS):
            dq_ref[:, p * HD:(p + 1) * HD] = dqs[p]
            dk_ref[:, p * HD:(p + 1) * HD] = dks[p]
        for h in VH:
            du_ref[:, h * HD:(h + 1) * HD] = dus[h]
            dw_ref[:, h * HD:(h + 1) * HD] = dws[h]
            dat_ref[0, h] = dats[h]
            dstate[h] = dS0s[h]
        dgc_ref[...] = _head_lanes(dgs)

    qs, ks, _, g1, wide, sq, st = _gdn_specs(NC, lambda n: NC - 1 - n)
    dqs = pl.BlockSpec((CHUNK, GDN_QK_W), lambda n: (NC - 1 - n, 0))
    return pl.pallas_call(
        body, name=name, grid=(NC,),
        in_specs=[qs, ks, g1, wide, wide, sq, st, wide],
        out_specs=[dqs, dqs, g1, wide, wide, sq],
        out_shape=[SDS((S, GDN_QK_W), f32), SDS((S, GDN_QK_W), f32), SDS((S, 128), f32), SDS((S, GDN_V_W), f32),
                   SDS((S, GDN_V_W), f32), SDS((NC, GDN_V_HEADS, CHUNK, CHUNK), f32)],
        scratch_shapes=[pltpu.VMEM((GDN_V_HEADS, HD, HD), f32)],
        compiler_params=_cp(("arbitrary",)),
    )(qkvc, qkvc, gc, u, w, attn, states, do)


def gdn_intra_bwd(qkvc, gc, beta, Ts, du, dw, dattn, dq_s, dk_s, dgc_s, name):
    S = qkvc.shape[0]
    NC = S // CHUNK

    def body(q_ref, k_ref, v_ref, gc_ref, be_ref, T_ref, du_ref, dw_ref, dat_ref, dqs_ref, dks_ref, dgs_ref,
             dq_ref, dk_ref, dv_ref, dgc_ref, dbe_ref):
        VH = range(GDN_V_HEADS)
        Ts = [T_ref[0, h] for h in VH]
        _, vjp = jax.vjp(lambda q_, k_, v_, g_, b_: _gdn_intra_all(q_, k_, v_, g_, b_, Ts)[:3],
                         _heads(q_ref, GDN_QK_HEADS), _heads(k_ref, GDN_QK_HEADS), _heads(v_ref, GDN_V_HEADS),
                         _head_cols(gc_ref[...]), _head_cols(be_ref[...]))
        dqs, dks, dvs, dgs, dbs = vjp((_heads(du_ref, GDN_V_HEADS), _heads(dw_ref, GDN_V_HEADS), [dat_ref[0, h] for h in VH]))
        for p in range(GDN_QK_HEADS):
            dq_ref[:, p * HD:(p + 1) * HD] = dqs[p] + dqs_ref[:, p * HD:(p + 1) * HD]
            dk_ref[:, p * HD:(p + 1) * HD] = dks[p] + dks_ref[:, p * HD:(p + 1) * HD]
        for h in VH:
            dv_ref[:, h * HD:(h + 1) * HD] = dvs[h]
        dgc_ref[...] = _head_lanes(dgs) + dgs_ref[...]
        dbe_ref[...] = _head_lanes(dbs)

    qs, ks, vs, g1, wide, sq, _ = _gdn_specs(NC)
    dqs = pl.BlockSpec((CHUNK, GDN_QK_W), lambda n: (n, 0))
    dq, dk, dv, dgc, dbe = pl.pallas_call(
        body, name=name, grid=(NC,),
        in_specs=[qs, ks, vs, g1, g1, sq, wide, wide, sq, dqs, dqs, g1],
        out_specs=[dqs, dqs, wide, g1, g1],
        out_shape=[SDS((S, GDN_QK_W), f32), SDS((S, GDN_QK_W), f32), SDS((S, GDN_V_W), f32), SDS((S, 128), f32), SDS((S, 128), f32)],
        compiler_params=_cp(("arbitrary",)),
    )(qkvc, qkvc, qkvc, gc, beta, Ts, du, dw, dattn, dq_s, dk_s, dgc_s)
    return jnp.concatenate([dq, dk, dv], axis=1), dgc, dbe


def _gated_norm(o, z, nw):
    parts = []
    for h in range(GDN_V_HEADS):
        oh = o[:, h * HD:(h + 1) * HD]
        r = lax.rsqrt(jnp.mean(oh * oh, axis=-1, keepdims=True) + EPS)
        parts.append((oh * r * nw) * _silu(z[:, h * HD:(h + 1) * HD]))
    return jnp.concatenate(parts, axis=1)


def gdn_out_fwd(o, proj, nw, W, x, gate, name):
    S, D = x.shape
    tm = _rows(S, 256)

    def body(o_ref, z_ref, nw_ref, w_ref, x_ref, g_ref, xn_ref, y_ref, og_ref):
        og = _gated_norm(o_ref[...], z_ref[...], nw_ref[...]).astype(bf16)
        y = jnp.dot(og, w_ref[...], preferred_element_type=f32)
        og_ref[...] = og
        y_ref[...] = y
        xn_ref[...] = x_ref[...] + g_ref[...] * y

    row = pl.BlockSpec((tm, D), lambda i: (i, 0))
    wide = pl.BlockSpec((tm, GDN_V_W), lambda i: (i, 0))
    return pl.pallas_call(
        body, name=name, grid=(S // tm,),
        in_specs=[wide, pl.BlockSpec((tm, GDN_V_W), lambda i: (i, 2)), pl.BlockSpec((1, HD), lambda i: (0, 0)),
                  pl.BlockSpec((GDN_V_W, D), lambda i: (0, 0)), row, pl.BlockSpec((1, D), lambda i: (0, 0))],
        out_specs=[row, row, wide],
        out_shape=[SDS((S, D), f32), SDS((S, D), f32), SDS((S, GDN_V_W), bf16)],
        compiler_params=_cp(("arbitrary",)),
    )(o, proj, nw, W, x, gate)


def gdn_out_bwd(dxn, y, gate, o, proj, nw, W, name):
    S, D = dxn.shape
    tm = _rows(S, 256)

    def body(dx_ref, y_ref, g_ref, o_ref, z_ref, nw_ref, w_ref, dy_ref, dg_ref, do_ref, dz_ref, dnw_ref):
        @pl.when(pl.program_id(0) == 0)
        def _():
            dg_ref[...] = jnp.zeros_like(dg_ref)
            dnw_ref[...] = jnp.zeros_like(dnw_ref)

        dx = dx_ref[...]
        dy = dx * g_ref[...]
        dy_ref[...] = dy
        dg_ref[...] += jnp.sum(dx * y_ref[...], axis=0, keepdims=True)
        dog = _nt(dy, w_ref[...])
        _, vjp = jax.vjp(_gated_norm, o_ref[...], z_ref[...], nw_ref[...])
        do, dz, dnw = vjp(dog)
        do_ref[...] = do
        dz_ref[...] = dz
        dnw_ref[...] += dnw

    row = pl.BlockSpec((tm, D), lambda i: (i, 0))
    wide = pl.BlockSpec((tm, GDN_V_W), lambda i: (i, 0))
    vecd = pl.BlockSpec((1, D), lambda i: (0, 0))
    vech = pl.BlockSpec((1, HD), lambda i: (0, 0))
    return pl.pallas_call(
        body, name=name, grid=(S // tm,),
        in_specs=[row, row, vecd, wide, pl.BlockSpec((tm, GDN_V_W), lambda i: (i, 2)), vech, pl.BlockSpec((GDN_V_W, D), lambda i: (0, 0))],
        out_specs=[row, vecd, wide, wide, vech],
        out_shape=[SDS((S, D), f32), SDS((1, D), f32), SDS((S, GDN_V_W), f32), SDS((S, GDN_V_W), f32), SDS((1, HD), f32)],
        compiler_params=_cp(("arbitrary",)),
    )(dxn, y, gate, o, proj, nw, W)


def _rms_w(x, w):
    return (x * lax.rsqrt(jnp.mean(x * x, axis=-1, keepdims=True) + EPS)) * w


def _split3(c):
    hi = c.astype(bf16).astype(f32)
    r1 = c - hi
    mid = r1.astype(bf16).astype(f32)
    lo = (r1 - mid).astype(bf16).astype(f32)
    return hi, mid, lo


_FOX_F_BLK = 4 * FOX_W // 128


def fox_prep_fwd(proj, f_bias, qn_w, kn_w, name):
    S = proj.shape[0]
    tm = _rows(S, 256)

    def body(q_ref, k_ref, v_ref, f_ref, fb_ref, qw_ref, kw_ref, Q_ref, K_ref, V_ref, carry):
        @pl.when(pl.program_id(0) == 0)
        def _():
            carry[...] = jnp.zeros_like(carry)

        ii = lax.broadcasted_iota(jnp.int32, (tm, tm), 0)
        jj = lax.broadcasted_iota(jnp.int32, (tm, tm), 1)
        lf = jax.nn.log_sigmoid(f_ref[...] + fb_ref[...])
        cum = _hdot((ii >= jj).astype(f32), lf) + carry[...]
        carry[...] = cum[tm - 1:tm, :]
        lane = lax.broadcasted_iota(jnp.int32, (tm, FOX_D), 1)
        q, k, v = q_ref[...], k_ref[...], v_ref[...]
        for h in range(FOX_H):
            sl = slice(h * FOX_D, (h + 1) * FOX_D)
            hi, mid, lo = _split3(cum[:, h:h + 1])
            qn = _rms_w(q[:, sl], qw_ref[...]) * FOX_D ** -0.5
            kn = _rms_w(k[:, sl], kw_ref[...])
            eq = jnp.where(lane == 0, hi, jnp.where(lane == 1, mid, jnp.where(lane == 2, lo, jnp.where(lane < 6, 1.0, 0.0))))
            ek = jnp.where(lane < 3, 1.0, jnp.where(lane == 3, -hi, jnp.where(lane == 4, -mid, jnp.where(lane == 5, -lo, 0.0))))
            ev = jnp.where(lane == 0, 1.0, 0.0)
            Q_ref[:, h * 128:(h + 1) * 128] = jnp.concatenate([qn, eq], axis=1).astype(bf16)
            K_ref[:, h * 128:(h + 1) * 128] = jnp.concatenate([kn, ek], axis=1).astype(bf16)
            V_ref[:, h * 128:(h + 1) * 128] = jnp.concatenate([v[:, sl], ev], axis=1).astype(bf16)

    def colblk(c):
        return pl.BlockSpec((tm, FOX_W), lambda i: (i, c))

    pad = pl.BlockSpec((tm, FOX_PW), lambda i: (i, 0))
    return pl.pallas_call(
        body, name=name, grid=(S // tm,),
        in_specs=[colblk(0), colblk(1), colblk(2), pl.BlockSpec((tm, 128), lambda i: (i, _FOX_F_BLK)),
                  pl.BlockSpec((1, 128), lambda i: (0, 0)), pl.BlockSpec((1, FOX_D), lambda i: (0, 0)), pl.BlockSpec((1, FOX_D), lambda i: (0, 0))],
        out_specs=[pad, pad, pad],
        out_shape=[SDS((S, FOX_PW), bf16)] * 3,
        scratch_shapes=[pltpu.VMEM((1, 128), f32)],
        compiler_params=_cp(("arbitrary",)),
    )(proj, proj, proj, proj, f_bias, qn_w, kn_w)


def fox_prep_bwd(proj, f_bias, qn_w, kn_w, dQ, dK, dV, name):
    S = proj.shape[0]
    tm = _rows(S, 256)
    NB = S // tm

    def body(q_ref, k_ref, f_ref, fb_ref, qw_ref, kw_ref, dQ_ref, dK_ref, dV_ref,
             dq_ref, dk_ref, dv_ref, df_ref, dfb_ref, dqw_ref, dkw_ref, carry):
        @pl.when(pl.program_id(0) == 0)
        def _():
            carry[...] = jnp.zeros_like(carry)
            dfb_ref[...] = jnp.zeros_like(dfb_ref)
            dqw_ref[...] = jnp.zeros_like(dqw_ref)
            dkw_ref[...] = jnp.zeros_like(dkw_ref)

        q, k = q_ref[...], k_ref[...]
        lane128 = lax.broadcasted_iota(jnp.int32, (tm, 128), 1)
        dcum = jnp.zeros((tm, 128), f32)
        dqs, dks, dvs = [], [], []
        dqw = jnp.zeros((1, FOX_D), f32)
        dkw = jnp.zeros((1, FOX_D), f32)
        for h in range(FOX_H):
            sl = slice(h * FOX_D, (h + 1) * FOX_D)
            dQh = dQ_ref[:, h * 128:(h + 1) * 128]
            dKh = dK_ref[:, h * 128:(h + 1) * 128]
            _, vq = jax.vjp(lambda a, w: _rms_w(a, w) * FOX_D ** -0.5, q[:, sl], qw_ref[...])
            dqh, dw1 = vq(dQh[:, 0:FOX_D])
            _, vk = jax.vjp(_rms_w, k[:, sl], kw_ref[...])
            dkh, dw2 = vk(dKh[:, 0:FOX_D])
            dqs.append(dqh)
            dks.append(dkh)
            dvs.append(dV_ref[:, h * 128:h * 128 + FOX_D])
            dqw = dqw + dw1
            dkw = dkw + dw2
            dcum = dcum + jnp.where(lane128 == h, dQh[:, FOX_D:FOX_D + 1] - dKh[:, FOX_D + 3:FOX_D + 4], 0.0)
        dq_ref[...] = jnp.concatenate(dqs, axis=1)
        dk_ref[...] = jnp.concatenate(dks, axis=1)
        dv_ref[...] = jnp.concatenate(dvs, axis=1)
        ii = lax.broadcasted_iota(jnp.int32, (tm, tm), 0)
        jj = lax.broadcasted_iota(jnp.int32, (tm, tm), 1)
        dlf = _hdot((ii <= jj).astype(f32), dcum) + carry[...]
        carry[...] += jnp.sum(dcum, axis=0, keepdims=True)
        df = dlf * jax.nn.sigmoid(-(f_ref[...] + fb_ref[...]))
        df_ref[...] = df
        dfb_ref[...] += jnp.sum(df, axis=0, keepdims=True)
        dqw_ref[...] += dqw
        dkw_ref[...] += dkw

    rv = lambda i: NB - 1 - i

    def colblk(c):
        return pl.BlockSpec((tm, FOX_W), lambda i: (rv(i), c))

    pad = pl.BlockSpec((tm, FOX_PW), lambda i: (rv(i), 0))
    cmp_ = pl.BlockSpec((tm, FOX_W), lambda i: (rv(i), 0))
    fblk = pl.BlockSpec((tm, 128), lambda i: (rv(i), 0))
    v128 = pl.BlockSpec((1, 128), lambda i: (0, 0))
    v64 = pl.BlockSpec((1, FOX_D), lambda i: (0, 0))
    return pl.pallas_call(
        body, name=name, grid=(NB,),
        in_specs=[colblk(0), colblk(1), pl.BlockSpec((tm, 128), lambda i: (rv(i), _FOX_F_BLK)), v128, v64, v64, pad, pad, pad],
        out_specs=[cmp_, cmp_, cmp_, fblk, v128, v64, v64],
        out_shape=[SDS((S, FOX_W), f32)] * 3 + [SDS((S, 128), f32), SDS((1, 128), f32), SDS((1, FOX_D), f32), SDS((1, FOX_D), f32)],
        scratch_shapes=[pltpu.VMEM((1, 128), f32)],
        compiler_params=_cp(("arbitrary",)),
    )(proj, proj, proj, f_bias, qn_w, kn_w, dQ, dK, dV)


FOX_HB = 2


def _diag_mask(t):
    return lax.broadcasted_iota(jnp.int32, (t, t), 1) <= lax.broadcasted_iota(jnp.int32, (t, t), 0)


def fox_attn_fwd(Q, K, V, name):
    S = Q.shape[0]
    t = _rows(S, 512)

    HB = FOX_HB
    HS = [slice(h * 128, (h + 1) * 128) for h in range(HB)]

    def body(q_ref, k_ref, v_ref, o_ref, m_sc, acc_sc):
        i = pl.program_id(1)
        qs = [q_ref[:, sl] for sl in HS]
        m_sc[...] = jnp.full_like(m_sc, NEG)
        acc_sc[...] = jnp.zeros_like(acc_sc)

        def tile(j, diag):
            j0 = pl.multiple_of(j * t, t)
            ss = [_nt(qs[h], k_ref[pl.ds(j0, t), HS[h]]) for h in range(HB)]
            if diag:
                ss = [jnp.where(_diag_mask(t), s, NEG) for s in ss]
            ms = [m_sc[h] for h in range(HB)]
            m_new = [jnp.maximum(ms[h], jnp.max(ss[h], axis=1, keepdims=True)) for h in range(HB)]
            ps = [jnp.exp(ss[h] - m_new[h]) for h in range(HB)]
            pv = [_nn(ps[h], v_ref[pl.ds(j0, t), HS[h]]) for h in range(HB)]
            for h in range(HB):
                acc_sc[h] = acc_sc[h] * jnp.exp(ms[h] - m_new[h]) + pv[h]
                m_sc[h] = m_new[h]

        def off_diag(j, c):
            tile(j, False)
            return c

        lax.fori_loop(0, i, off_diag, 0)
        tile(i, True)
        lane = lax.broadcasted_iota(jnp.int32, (t, 128), 1)
        for h in range(HB):
            acc = acc_sc[h]
            l = acc[:, FOX_D:FOX_D + 1]
            o_ref[:, HS[h]] = jnp.where(lane == FOX_D, m_sc[h] + jnp.log(l), acc / l)

    blk = pl.BlockSpec((t, HB * 128), lambda h, i: (i, h))
    seq = pl.BlockSpec((S, HB * 128), lambda h, i: (0, h))
    return pl.pallas_call(
        body, name=name, grid=(FOX_H // HB, S // t),
        in_specs=[blk, seq, seq], out_specs=blk, out_shape=SDS((S, FOX_PW), f32),
        scratch_shapes=[pltpu.VMEM((HB, t, 1), f32), pltpu.VMEM((HB, t, 128), f32)],
        compiler_params=_cp(("arbitrary", "arbitrary")),
    )(Q, K, V)


def fox_attn_bwd(Q, K, V, dO, O, name):
    S = Q.shape[0]
    t = _rows(S, 512)
    nq = S // t

    HB = FOX_HB
    HS = [slice(h * 128, (h + 1) * 128) for h in range(HB)]

    def body(k_ref, v_ref, q_ref, do_ref, o_ref, dq_ref, dk_ref, dv_ref):
        j = pl.program_id(1)

        @pl.when(j == 0)
        def _():
            dq_ref[...] = jnp.zeros_like(dq_ref)

        dk_ref[...] = jnp.zeros_like(dk_ref)
        dv_ref[...] = jnp.zeros_like(dv_ref)
        ks = [k_ref[:, sl] for sl in HS]
        vs = [v_ref[:, sl] for sl in HS]

        def tile(i, diag):
            i0 = pl.multiple_of(i * t, t)
            R = range(HB)
            qs = [q_ref[pl.ds(i0, t), HS[h]] for h in R]
            dos = [do_ref[pl.ds(i0, t), HS[h]] for h in R]
            ss = [_nt(qs[h], ks[h]) - o_ref[pl.ds(i0, t), h * 128 + FOX_D:h * 128 + FOX_D + 1] for h in R]
            if diag:
                ss = [jnp.where(_diag_mask(t), s, NEG) for s in ss]
            ps = [jnp.exp(s) for s in ss]
            dps = [_nt(dos[h], vs[h]) for h in R]
            dvs = [_tn(ps[h], dos[h]) for h in R]
            dss = [(ps[h] * dps[h]).astype(bf16) for h in R]
            dks = [_tn(dss[h], qs[h]) for h in R]
            dqs = [_nn(dss[h], ks[h]) for h in R]
            for h in R:
                dv_ref[:, HS[h]] += dvs[h]
                dk_ref[:, HS[h]] += dks[h]
                dq_ref[pl.ds(i0, t), HS[h]] += dqs[h]

        tile(j, True)

        def off_diag(i, c):
            tile(i, False)
            return c

        lax.fori_loop(j + 1, nq, off_diag, 0)

    blk = pl.BlockSpec((t, HB * 128), lambda h, j: (j, h))
    seq = pl.BlockSpec((S, HB * 128), lambda h, j: (0, h))
    return pl.pallas_call(
        body, name=name, grid=(FOX_H // HB, nq),
        in_specs=[blk, blk, seq, seq, seq], out_specs=[seq, blk, blk],
        out_shape=[SDS((S, FOX_PW), f32)] * 3,
        compiler_params=_cp(("arbitrary", "arbitrary")),
    )(K, V, Q, dO, O)


def fox_out_fwd(O, proj, W, x, gate, name):
    S, D = x.shape
    tm = _rows(S, 256)

    def body(o_ref, z_ref, w_ref, x_ref, g_ref, xn_ref, y_ref, og_ref):
        z = z_ref[...]
        og = jnp.concatenate([o_ref[:, h * 128:h * 128 + FOX_D] * _silu(z[:, h * FOX_D:(h + 1) * FOX_D]) for h in range(FOX_H)],
                             axis=1).astype(bf16)
        y = jnp.dot(og, w_ref[...], preferred_element_type=f32)
        og_ref[...] = og
        y_ref[...] = y
        xn_ref[...] = x_ref[...] + g_ref[...] * y

    row = pl.BlockSpec((tm, D), lambda i: (i, 0))
    cmp_ = pl.BlockSpec((tm, FOX_W), lambda i: (i, 0))
    return pl.pallas_call(
        body, name=name, grid=(S // tm,),
        in_specs=[pl.BlockSpec((tm, FOX_PW), lambda i: (i, 0)), pl.BlockSpec((tm, FOX_W), lambda i: (i, 3)),
                  pl.BlockSpec((FOX_W, D), lambda i: (0, 0)), row, pl.BlockSpec((1, D), lambda i: (0, 0))],
        out_specs=[row, row, cmp_],
        out_shape=[SDS((S, D), f32), SDS((S, D), f32), SDS((S, FOX_W), bf16)],
        compiler_params=_cp(("arbitrary",)),
    )(O, proj, W, x, gate)


def fox_out_bwd(dxn, y, gate, O, proj, W, name):
    S, D = dxn.shape
    tm = _rows(S, 256)

    def body(dx_ref, y_ref, g_ref, o_ref, z_ref, w_ref, dy_ref, dg_ref, dO_ref, dz_ref):
        @pl.when(pl.program_id(0) == 0)
        def _():
            dg_ref[...] = jnp.zeros_like(dg_ref)

        dx = dx_ref[...]
        dy = dx * g_ref[...]
        dy_ref[...] = dy
        dg_ref[...] += jnp.sum(dx * y_ref[...], axis=0, keepdims=True)
        dog = _nt(dy, w_ref[...])
        z = z_ref[...]
        lane = lax.broadcasted_iota(jnp.int32, (tm, FOX_D), 1)
        dzs = []
        for h in range(FOX_H):
            sl = slice(h * FOX_D, (h + 1) * FOX_D)
            zh = z[:, sl]
            sg = jax.nn.sigmoid(zh)
            oh = o_ref[:, h * 128:h * 128 + FOX_D]
            doh = dog[:, sl] * (zh * sg)
            delta = jnp.sum(doh * oh, axis=1, keepdims=True)
            dO_ref[:, h * 128:(h + 1) * 128] = jnp.concatenate([doh, jnp.where(lane == 0, -delta, 0.0)], axis=1).astype(bf16)
            dzs.append(dog[:, sl] * oh * (sg * (1.0 + zh * (1.0 - sg))))
        dz_ref[...] = jnp.concatenate(dzs, axis=1)

    row = pl.BlockSpec((tm, D), lambda i: (i, 0))
    vecd = pl.BlockSpec((1, D), lambda i: (0, 0))
    pad = pl.BlockSpec((tm, FOX_PW), lambda i: (i, 0))
    return pl.pallas_call(
        body, name=name, grid=(S // tm,),
        in_specs=[row, row, vecd, pad, pl.BlockSpec((tm, FOX_W), lambda i: (i, 3)), pl.BlockSpec((FOX_W, D), lambda i: (0, 0))],
        out_specs=[row, vecd, pad, pl.BlockSpec((tm, FOX_W), lambda i: (i, 0))],
        out_shape=[SDS((S, D), f32), SDS((1, D), f32), SDS((S, FOX_PW), bf16), SDS((S, FOX_W), f32)],
        compiler_params=_cp(("arbitrary",)),
    )(dxn, y, gate, O, proj, W)


def final_loss(x, fw, target, name):
    S, D = x.shape
    tm = _rows(S, 512)

    def body(x_ref, w_ref, t_ref, l_ref, dx_ref, dw_ref):
        @pl.when(pl.program_id(0) == 0)
        def _():
            l_ref[...] = jnp.zeros_like(l_ref)
            dw_ref[...] = jnp.zeros_like(dw_ref)

        out, vjp = jax.vjp(_rms_w, x_ref[...], w_ref[...])
        err = out - t_ref[...]
        l_ref[...] += 0.5 * jnp.sum(jnp.sum(err * err, axis=1, keepdims=True) * (1.0 / D), axis=0, keepdims=True)
        dx, dw = vjp(err * (1.0 / D))
        dx_ref[...] = dx
        dw_ref[...] += dw

    row = pl.BlockSpec((tm, D), lambda i: (i, 0))
    vec = pl.BlockSpec((1, D), lambda i: (0, 0))
    return pl.pallas_call(
        body, name=name, grid=(S // tm,),
        in_specs=[row, vec, row], out_specs=[pl.BlockSpec((1, 128), lambda i: (0, 0)), row, vec],
        out_shape=[SDS((1, 128), f32), SDS((S, D), f32), SDS((1, D), f32)],
        compiler_params=_cp(("arbitrary",)),
    )(x, fw, target)


def ada_fwd(c_all, ada_w, name):
    L, D, n = ada_w.shape

    def body(c_ref, w_ref, o_ref):
        cond = jnp.concatenate([_silu(c_ref[...]), jnp.zeros((8, D), f32)], axis=0)
        o_ref[0] = _nn(cond, w_ref[0])[0:8]

    return pl.pallas_call(
        body, name=name, grid=(L,),
        in_specs=[pl.BlockSpec((NDEV, D), lambda l: (0, 0)), pl.BlockSpec((1, D, n), lambda l: (l, 0, 0))],
        out_specs=pl.BlockSpec((1, NDEV, n), lambda l: (l, 0, 0)),
        out_shape=SDS((L, NDEV, n), f32),
        compiler_params=_cp(("arbitrary",)),
    )(c_all, ada_w)


def ada_grad(c_all, dmod, name):
    L, _, n = dmod.shape
    D = c_all.shape[1]

    def body(c_ref, d_ref, o_ref):
        cond = jnp.concatenate([_silu(c_ref[...]), jnp.zeros((8, D), f32)], axis=0)
        dm = jnp.concatenate([d_ref[0], jnp.zeros((8, n), f32)], axis=0)
        o_ref[0] = _tn(cond, dm)

    return pl.pallas_call(
        body, name=name, grid=(L,),
        in_specs=[pl.BlockSpec((NDEV, D), lambda l: (0, 0)), pl.BlockSpec((1, NDEV, n), lambda l: (l, 0, 0))],
        out_specs=pl.BlockSpec((1, D, n), lambda l: (l, 0, 0)),
        out_shape=SDS((L, D, n), f32),
        compiler_params=_cp(("arbitrary",)),
    )(c_all, dmod)


def reduce_adam(parts, w, m, v, tr, name):
    n, R, C = parts.shape
    c1 = 1.0 / (1.0 - ADAM_B1 ** ADAM_STEP)
    c2 = 1.0 / (1.0 - ADAM_B2 ** ADAM_STEP)

    def body(p_ref, w_ref, m_ref, v_ref, g_ref, d_ref, nm_ref, nv_ref):
        g = p_ref[0].astype(f32)
        for s in range(1, n):
            g = g + p_ref[s].astype(f32)
        nm = ADAM_B1 * m_ref[...] + (1.0 - ADAM_B1) * g
        nv = ADAM_B2 * v_ref[...] + (1.0 - ADAM_B2) * (g * g)
        g_ref[...] = g
        nm_ref[...] = nm
        nv_ref[...] = nv
        d_ref[...] = -ADAM_LR * ((nm * c1) / (jnp.sqrt(nv * c2) + ADAM_EPS) + ADAM_WD * w_ref[...])

    blk = pl.BlockSpec((tr, C), lambda i: (i, 0))
    return pl.pallas_call(
        body, name=name, grid=(R // tr,),
        in_specs=[pl.BlockSpec((n, tr, C), lambda i: (0, i, 0)), blk, blk, blk],
        out_specs=[blk] * 4, out_shape=[SDS((R, C), f32)] * 4,
        compiler_params=_cp(("arbitrary",)),
    )(parts, w, m, v)


def _my_pos():
    return lax.axis_index("x"), lax.axis_index("y"), lax.axis_index("c")


def all_gather(xs, name):
    n = len(xs)

    def body(*refs):
        x_refs, out_refs = refs[:n], refs[n:2 * n]
        send_sems, recv_sems, local_sems = refs[2 * n:]
        x_, y_, c_ = _my_pos()
        me, sibling = (x_, y_, c_), (x_, y_, 1 - c_)
        chips = [(1 - x_, y_), (x_, 1 - y_), (1 - x_, 1 - y_)]

        def rows(a, px, py, pc):
            return out_refs[a].at[4 * px + 2 * py + pc]

        def copy(a, k, block, to, own=False):
            return pltpu.make_async_remote_copy(
                src_ref=x_refs[a] if own else rows(a, *block), dst_ref=rows(a, *block),
                send_sem=send_sems.at[k, a], recv_sem=recv_sems.at[k, a], device_id=to, device_id_type=pl.DeviceIdType.MESH)

        mine = [pltpu.make_async_copy(x_refs[a], rows(a, *me), local_sems.at[a]) for a in range(n)]
        for cp in mine:
            cp.start()
        first = []
        for a in range(n):
            first.append(copy(a, 0, me, sibling, own=True))
            first += [copy(a, 1 + j, me, (*chip, c_), own=True) for j, chip in enumerate(chips)]
        for cp in first:
            cp.start()
        passed = []
        for j, chip in enumerate(chips):
            for a in range(n):
                copy(a, 1 + j, (*chip, c_), me).wait_recv()
                cp = copy(a, 4 + j, (*chip, c_), sibling)
                cp.start()
                passed.append(cp)
        for a in range(n):
            copy(a, 0, sibling, me).wait_recv()
            for j, chip in enumerate(chips):
                copy(a, 4 + j, (*chip, 1 - c_), me).wait_recv()
        for cp in first + passed:
            cp.wait_send()
        for cp in mine:
            cp.wait()

    any_ = pl.BlockSpec(memory_space=pl.ANY)
    return pl.pallas_call(
        body, name=name, out_shape=[SDS((NDEV,) + x.shape, x.dtype) for x in xs],
        in_specs=[any_] * n, out_specs=[any_] * n,
        scratch_shapes=[pltpu.SemaphoreType.DMA((7, n)), pltpu.SemaphoreType.DMA((7, n)), pltpu.SemaphoreType.DMA((n,))],
    )(*xs)


def all_to_all(xs, name):
    n = len(xs)

    def body(*refs):
        x_refs, out_refs = refs[:n], refs[n:2 * n]
        send_sems, recv_sems, local_sems = refs[2 * n:]
        x_, y_, c_ = _my_pos()
        me = 4 * x_ + 2 * y_ + c_
        local = [pltpu.make_async_copy(x_refs[a].at[me], out_refs[a].at[me], local_sems.at[a]) for a in range(n)]
        for cp in local:
            cp.start()
        copies = []
        for rel in range(1, NDEV):
            px = (x_ + ((rel >> 2) & 1)) % 2
            py = (y_ + ((rel >> 1) & 1)) % 2
            pc = (c_ + (rel & 1)) % 2
            for a in range(n):
                cp = pltpu.make_async_remote_copy(
                    src_ref=x_refs[a].at[4 * px + 2 * py + pc], dst_ref=out_refs[a].at[me],
                    send_sem=send_sems.at[rel - 1, a], recv_sem=recv_sems.at[rel - 1, a],
                    device_id=(px, py, pc), device_id_type=pl.DeviceIdType.MESH)
                cp.start()
                copies.append(cp)
        for cp in copies:
            cp.wait()
        for cp in local:
            cp.wait()

    any_ = pl.BlockSpec(memory_space=pl.ANY)
    return pl.pallas_call(
        body, name=name, out_shape=[SDS(x.shape, x.dtype) for x in xs],
        in_specs=[any_] * n, out_specs=[any_] * n,
        scratch_shapes=[pltpu.SemaphoreType.DMA((7, n)), pltpu.SemaphoreType.DMA((7, n)), pltpu.SemaphoreType.DMA((n,))],
    )(*xs)


GDN_COLS = ((0, GDN_CONV + GDN_V_W, 0), (GDN_CONV + GDN_V_W, GDN_CONV + GDN_V_W + 16, GDN_CONV + GDN_V_W),
            (GDN_CONV + GDN_V_W + 16, GDN_IN, GDN_CONV + GDN_V_W + 128))
FOX_COLS = ((0, FOX_IN, 0),)


def _col_pieces(d, per, cols):
    lo, hi = per * d, per * (d + 1)
    out = []
    for a, b, dst in cols:
        s, e = max(lo, a), min(hi, b)
        if s < e:
            out.append((s - lo, e - s, dst + s - a))
    return out


def cols_from_blocks(g, cols, n_out, name):
    _, L, R, C = g.shape
    tr = min(256, R)

    def body(g_ref, o_ref):
        o_ref[...] = jnp.zeros_like(o_ref)
        for d in range(NDEV):
            for off, ln, dst in _col_pieces(d, C, cols):
                o_ref[0, :, dst:dst + ln] = g_ref[d, 0, :, off:off + ln]

    return pl.pallas_call(
        body, name=name, grid=(L, R // tr),
        in_specs=[pl.BlockSpec((NDEV, 1, tr, C), lambda l, i: (0, l, i, 0))],
        out_specs=pl.BlockSpec((1, tr, n_out), lambda l, i: (l, i, 0)),
        out_shape=SDS((L, R, n_out), g.dtype),
        compiler_params=_cp(("arbitrary", "arbitrary")),
    )(g)


def blocks_from_cols(dw, C, cols, name):
    R, n_in = dw.shape
    tr = min(256, R)

    def body(x_ref, o_ref):
        for d in range(NDEV):
            for off, ln, src in _col_pieces(d, C, cols):
                o_ref[d, :, off:off + ln] = x_ref[:, src:src + ln].astype(bf16)

    return pl.pallas_call(
        body, name=name, grid=(R // tr,),
        in_specs=[pl.BlockSpec((tr, n_in), lambda i: (i, 0))],
        out_specs=pl.BlockSpec((NDEV, tr, C), lambda i: (0, i, 0)),
        out_shape=SDS((NDEV, R, C), bf16),
        compiler_params=_cp(("arbitrary",)),
    )(dw)


BIG = ("a_w_in", "a_conv_w", "a_w_out", "b_w_in", "b_w_out")
SMALL = ("norm_w", "ada_b", "a_A_log", "a_dt_bias", "a_norm_w", "b_f_bias", "b_qn_w", "b_kn_w", "final_norm_w")


def _pack_small(arrs):
    rows = []
    for a in arrs:
        fl = a.reshape(-1)
        pad = (-fl.shape[0]) % 128
        if pad:
            fl = jnp.concatenate([fl, jnp.zeros((pad,), fl.dtype)])
        rows.append(fl)
    flat = jnp.concatenate(rows)
    pad = (-flat.shape[0]) % (8 * 128)
    if pad:
        flat = jnp.concatenate([flat, jnp.zeros((pad,), flat.dtype)])
    return flat.reshape(-1, 128)


def _unpack(packed, shapes, align):
    flat = packed.reshape(-1)
    out, off = [], 0
    for shp in shapes:
        n = 1
        for d in shp:
            n *= d
        out.append(flat[off:off + n].reshape(shp))
        off += n + ((-n) % align)
    return out


def _full_from_gathered(g, shard_shape, axis):
    g = jnp.moveaxis(g, 0, axis)
    shp = list(shard_shape)
    shp[axis] *= NDEV
    return g.reshape(shp)


def _blocks_from_full(full, shard_shape, axis):
    shp = list(shard_shape)
    full = full.reshape(shp[:axis] + [NDEV, shp[axis]] + shp[axis + 1:])
    return jnp.moveaxis(full, axis, 0)


def _pad_lanes(v, n=128):
    v = v.reshape(1, -1)
    return jnp.concatenate([v, jnp.zeros((1, n - v.shape[1]), v.dtype)], axis=1)


def gdn_layer_fwd(x, mod, nw, W_in, conv_w, A_log, dt_bias, a_nw, W_out, tag):
    shift, scale, gate = mod
    proj, h = inproj_fwd(x, nw, scale, shift, W_in, GDN_TN, f"{tag}_inproj")
    qkvc = gdn_prep_fwd(proj, conv_w, f"{tag}_prep")
    gc, beta = gdn_gates_fwd(proj, A_log, dt_bias, f"{tag}_gates")
    u, w, attn, Ts = gdn_intra_fwd(qkvc, gc, beta, f"{tag}_intra")
    o, states = gdn_scan_fwd(qkvc, gc, u, w, attn, f"{tag}_scan")
    x_new, y, og = gdn_out_fwd(o, proj, a_nw, W_out, x, gate, f"{tag}_out")
    return x_new, (x, proj, h, qkvc, gc, beta, o, states, Ts, y, og, u, w, attn)


def gdn_layer_bwd(dxn, saved, mod, nw, W_in, conv_w, A_log, dt_bias, a_nw, W_out, tag):
    shift, scale, gate = mod
    x, proj, h, qkvc, gc, beta, o, states, Ts, y, og, u, w, attn = saved
    dy, dgate, do, dz, da_nw = gdn_out_bwd(dxn, y, gate, o, proj, a_nw, W_out, f"{tag}_out_bwd")
    dW_out = matmul_tn(og, dy, 512, f"{tag}_dwout")
    dq_s, dk_s, dgc_s, du, dw, dattn = gdn_scan_bwd(qkvc, gc, u, w, attn, states, do, f"{tag}_scan_bwd")
    dqkvc, dgc, dbeta = gdn_intra_bwd(qkvc, gc, beta, Ts, du, dw, dattn, dq_s, dk_s, dgc_s, f"{tag}_intra_bwd")
    db, da, dA_log, ddt = gdn_gates_bwd(proj, A_log, dt_bias, dgc, dbeta, f"{tag}_gates_bwd")
    dqkv, dconv_w = gdn_prep_bwd(proj, conv_w, dqkvc, f"{tag}_prep_bwd")
    dproj = jnp.concatenate([dqkv, dz, db, da], axis=1)
    dW_in = matmul_tn(h, dproj, GDN_TN, f"{tag}_dwin")
    dx, dnw, dscale, dshift = inproj_bwd_x(x, nw, scale, shift, W_in, dproj, dxn, GDN_TN, f"{tag}_inproj_bwd")
    grads = dict(norm_w=dnw, W_in=dW_in, conv_w=dconv_w, A_log=dA_log[:, :16], dt_bias=ddt[:, :16], a_nw=da_nw, W_out=dW_out,
                 dmod=jnp.concatenate([dshift, dscale, dgate], axis=1))
    return dx, grads


def fox_layer_fwd(x, mod, nw, W_in, f_bias, qn_w, kn_w, W_out, tag):
    shift, scale, gate = mod
    proj, h = inproj_fwd(x, nw, scale, shift, W_in, FOX_TN, f"{tag}_inproj")
    Q, K, V = fox_prep_fwd(proj, f_bias, qn_w, kn_w, f"{tag}_prep")
    O = fox_attn_fwd(Q, K, V, f"{tag}_attn")
    x_new, y, og = fox_out_fwd(O, proj, W_out, x, gate, f"{tag}_out")
    return x_new, (x, proj, h, Q, K, V, O, y, og)


def fox_layer_bwd(dxn, saved, mod, nw, W_in, f_bias, qn_w, kn_w, W_out, tag):
    shift, scale, gate = mod
    x, proj, h, Q, K, V, O, y, og = saved
    dy, dgate, dO, dz = fox_out_bwd(dxn, y, gate, O, proj, W_out, f"{tag}_out_bwd")
    dW_out = matmul_tn(og, dy, 512, f"{tag}_dwout")
    dQ, dK, dV = fox_attn_bwd(Q, K, V, dO, O, f"{tag}_attn_bwd")
    dq, dk, dv, df, dfb, dqw, dkw = fox_prep_bwd(proj, f_bias, qn_w, kn_w, dQ, dK, dV, f"{tag}_prep_bwd")
    dproj = jnp.concatenate([dq, dk, dv, dz, df], axis=1)
    dW_in = matmul_tn(h, dproj, FOX_TN, f"{tag}_dwin")
    dx, dnw, dscale, dshift = inproj_bwd_x(x, nw, scale, shift, W_in, dproj, dxn, FOX_TN, f"{tag}_inproj_bwd")
    grads = dict(norm_w=dnw, W_in=dW_in, f_bias=dfb[:, :16], qn_w=dqw, kn_w=dkw, W_out=dW_out,
                 dmod=jnp.concatenate([dshift, dscale, dgate], axis=1))
    return dx, grads


def device_step(x, mod_all, norm_w, full, small, final_norm_w, target):
    D = x.shape[1]
    mods = [(mod_all[i:i + 1, 0:D], mod_all[i:i + 1, D:2 * D], mod_all[i:i + 1, 2 * D:3 * D]) for i in range(4)]

    def a_args(j):
        return (full["a_w_in"][j], full["a_conv_w"][j], _pad_lanes(small["a_A_log"][j]), _pad_lanes(small["a_dt_bias"][j]),
                small["a_norm_w"][j:j + 1], full["a_w_out"][j])

    def b_args(j):
        return (full["b_w_in"][j], _pad_lanes(small["b_f_bias"][j]), small["b_qn_w"][j:j + 1], small["b_kn_w"][j:j + 1],
                full["b_w_out"][j])

    saved = []
    for i in range(4):
        j = i // 2
        if i % 2 == 0:
            x, sv = gdn_layer_fwd(x, mods[i], norm_w[i:i + 1], *a_args(j), tag=f"L{i}")
        else:
            x, sv = fox_layer_fwd(x, mods[i], norm_w[i:i + 1], *b_args(j), tag=f"L{i}")
        saved.append(sv)
    loss, dx, dfw = final_loss(x, final_norm_w.reshape(1, D), target, "final_loss")
    lg = [None] * 4
    for i in reversed(range(4)):
        j = i // 2
        if i % 2 == 0:
            dx, lg[i] = gdn_layer_bwd(dx, saved[i], mods[i], norm_w[i:i + 1], *a_args(j), tag=f"L{i}")
        else:
            dx, lg[i] = fox_layer_bwd(dx, saved[i], mods[i], norm_w[i:i + 1], *b_args(j), tag=f"L{i}")
    g = dict(
        norm_w=jnp.concatenate([lg[i]["norm_w"] for i in range(4)], axis=0),
        dmod=jnp.concatenate([lg[i]["dmod"] for i in range(4)], axis=0),
        a_w_in=[lg[i]["W_in"] for i in (0, 2)],
        a_conv_w=jnp.stack([lg[i]["conv_w"] for i in (0, 2)]),
        a_A_log=jnp.concatenate([lg[i]["A_log"] for i in (0, 2)], axis=0),
        a_dt_bias=jnp.concatenate([lg[i]["dt_bias"] for i in (0, 2)], axis=0),
        a_norm_w=jnp.concatenate([lg[i]["a_nw"] for i in (0, 2)], axis=0),
        a_w_out=jnp.stack([lg[i]["W_out"] for i in (0, 2)]),
        b_w_in=[lg[i]["W_in"] for i in (1, 3)],
        b_f_bias=jnp.concatenate([lg[i]["f_bias"] for i in (1, 3)], axis=0),
        b_qn_w=jnp.concatenate([lg[i]["qn_w"] for i in (1, 3)], axis=0),
        b_kn_w=jnp.concatenate([lg[i]["kn_w"] for i in (1, 3)], axis=0),
        b_w_out=jnp.stack([lg[i]["W_out"] for i in (1, 3)]),
        final_norm_w=dfw.reshape(-1),
    )
    return loss[0, 0], dx, g


def kernel(x, c, norm_w, ada_w, ada_b, a_w_in, a_conv_w, a_A_log, a_dt_bias, a_norm_w, a_w_out, b_w_in, b_f_bias, b_qn_w, b_kn_w, b_w_out, final_norm_w, loss_target, m_norm_w, m_ada_w, m_ada_b, m_a_w_in, m_a_conv_w, m_a_A_log, m_a_dt_bias, m_a_norm_w, m_a_w_out, m_b_w_in, m_b_f_bias, m_b_qn_w, m_b_kn_w, m_b_w_out, m_final_norm_w, v_norm_w, v_ada_w, v_ada_b, v_a_w_in, v_a_conv_w, v_a_A_log, v_a_dt_bias, v_a_norm_w, v_a_w_out, v_b_w_in, v_b_f_bias, v_b_qn_w, v_b_kn_w, v_b_w_out, v_final_norm_w):
    W = dict(norm_w=norm_w, ada_w=ada_w, ada_b=ada_b, a_w_in=a_w_in, a_conv_w=a_conv_w, a_A_log=a_A_log, a_dt_bias=a_dt_bias,
             a_norm_w=a_norm_w, a_w_out=a_w_out, b_w_in=b_w_in, b_f_bias=b_f_bias, b_qn_w=b_qn_w, b_kn_w=b_kn_w, b_w_out=b_w_out,
             final_norm_w=final_norm_w)
    M = dict(norm_w=m_norm_w, ada_w=m_ada_w, ada_b=m_ada_b, a_w_in=m_a_w_in, a_conv_w=m_a_conv_w, a_A_log=m_a_A_log,
             a_dt_bias=m_a_dt_bias, a_norm_w=m_a_norm_w, a_w_out=m_a_w_out, b_w_in=m_b_w_in, b_f_bias=m_b_f_bias, b_qn_w=m_b_qn_w,
             b_kn_w=m_b_kn_w, b_w_out=m_b_w_out, final_norm_w=m_final_norm_w)
    V = dict(norm_w=v_norm_w, ada_w=v_ada_w, ada_b=v_ada_b, a_w_in=v_a_w_in, a_conv_w=v_a_conv_w, a_A_log=v_a_A_log,
             a_dt_bias=v_a_dt_bias, a_norm_w=v_a_norm_w, a_w_out=v_a_w_out, b_w_in=v_b_w_in, b_f_bias=v_b_f_bias, b_qn_w=v_b_qn_w,
             b_kn_w=v_b_kn_w, b_w_out=v_b_w_out, final_norm_w=v_final_norm_w)
    S, D = x.shape[1], x.shape[2]
    me = 4 * lax.axis_index("x") + 2 * lax.axis_index("y") + lax.axis_index("c")
    small_shapes = [W[n].shape for n in SMALL]
    n_ain, n_bin = a_w_in.shape[2], b_w_in.shape[2]

    gath = all_gather([a_w_in.astype(bf16).reshape(-1, n_ain), a_w_out.astype(bf16).reshape(-1, D),
                       b_w_in.astype(bf16).reshape(-1, n_bin), b_w_out.astype(bf16).reshape(-1, D),
                       a_conv_w.reshape(8, -1), c.reshape(8, D // 8)], "gather_w")
    full = dict(
        a_w_in=cols_from_blocks(gath[0].reshape((NDEV,) + a_w_in.shape), GDN_COLS, GDN_IN_PAD, "a_w_in_cols"),
        b_w_in=cols_from_blocks(gath[2].reshape((NDEV,) + b_w_in.shape), FOX_COLS, FOX_IN_PAD, "b_w_in_cols"),
        a_w_out=_full_from_gathered(gath[1].reshape((NDEV,) + a_w_out.shape), a_w_out.shape, 1),
        b_w_out=_full_from_gathered(gath[3].reshape((NDEV,) + b_w_out.shape), b_w_out.shape, 1),
        a_conv_w=_full_from_gathered(gath[4].reshape((NDEV,) + a_conv_w.shape), a_conv_w.shape, 2))
    c_all = gath[5].reshape(NDEV, D)

    mod_part = ada_fwd(c_all, ada_w, "ada_fwd")
    n_ada = ada_w.shape[2]
    mod_g = all_gather([mod_part.reshape(4 * NDEV, n_ada)], "gather_mod")[0].reshape(NDEV, 4, NDEV, n_ada)
    mod_mine = lax.dynamic_index_in_dim(mod_g, me, axis=2, keepdims=False)
    mod_all = jnp.moveaxis(mod_mine, 0, 1).reshape(4, NDEV * n_ada) + ada_b

    loss, dx, g = device_step(x[0], mod_all, norm_w, full, W, final_norm_w, loss_target[0])
    loss = lax.psum(loss, MESH_AXES)

    g_small = dict(g, ada_b=g["dmod"])
    sp = _pack_small([g_small[n] for n in SMALL])
    sp_all = all_gather([sp], "gather_small")[0]
    sw, sm, sv = (_pack_small([T[n] for n in SMALL]) for T in (W, M, V))
    sg, sd, snm, snv = (_unpack(t, small_shapes, 128) for t in reduce_adam(sp_all, sw, sm, sv, sp.shape[0], "adam_small"))

    off_b = 0
    for n, shp in zip(SMALL, small_shapes):
        if n == "ada_b":
            break
        cnt = 1
        for d in shp:
            cnt *= d
        off_b += cnt + ((-cnt) % 128)
    dmod_all = sp_all.reshape(NDEV, -1)[:, off_b:off_b + 4 * 3 * D].reshape(NDEV, 4, 3 * D)
    dmod_cols = lax.dynamic_slice_in_dim(dmod_all, me * n_ada, n_ada, axis=2)
    g_ada = ada_grad(c_all, jnp.moveaxis(dmod_cols, 0, 1), "ada_grad")
    r_ada = reduce_adam(g_ada.reshape(1, 4 * D, n_ada), *(T["ada_w"].reshape(4 * D, n_ada) for T in (W, M, V)), 512, "adam_ada")
    ag, ad, anm, anv = (t.reshape(ada_w.shape) for t in r_ada)

    blocks = dict(
        a_w_in=jnp.stack([blocks_from_cols(g["a_w_in"][l], n_ain, GDN_COLS, f"a_w_in_blocks{l}") for l in range(2)], axis=1),
        b_w_in=jnp.stack([blocks_from_cols(g["b_w_in"][l], n_bin, FOX_COLS, f"b_w_in_blocks{l}") for l in range(2)], axis=1),
        a_w_out=_blocks_from_full(g["a_w_out"], a_w_out.shape, 1).astype(bf16),
        b_w_out=_blocks_from_full(g["b_w_out"], b_w_out.shape, 1).astype(bf16),
        a_conv_w=_blocks_from_full(g["a_conv_w"], a_conv_w.shape, 2))
    views = {n: (NDEV, -1, W[n].shape[-1]) for n in BIG}
    recv = all_to_all([blocks[n].reshape(views[n]) for n in BIG], "scatter_grads")
    big = {}
    for n, r in zip(BIG, recv):
        res = reduce_adam(r, *(T[n].reshape(r.shape[1:]) for T in (W, M, V)), min(256, r.shape[1]), f"adam_{n}")
        big[n] = [t.reshape(W[n].shape) for t in res]

    outs = {}
    for idx, (k, sm_l, ada_t) in enumerate((("grad", sg, ag), ("delta", sd, ad), ("new_m", snm, anm), ("new_v", snv, anv))):
        d = dict(zip(SMALL, sm_l))
        d.update({n: big[n][idx] for n in BIG})
        d["ada_w"] = ada_t
        outs[k] = d
    order = ("norm_w", "ada_w", "ada_b", "a_w_in", "a_conv_w", "a_A_log", "a_dt_bias", "a_norm_w", "a_w_out", "b_w_in", "b_f_bias",
             "b_qn_w", "b_kn_w", "b_w_out", "final_norm_w")
    return (loss, dx[None], *[outs["grad"][n] for n in order], *[outs["delta"][n] for n in order],
            *[outs["new_m"][n] for n in order], *[outs["new_v"][n] for n in order])
```

```python
import functools

import jax
import jax.numpy as jnp
from jax import lax
from jax.experimental import pallas as pl
from jax.experimental.pallas import tpu as pltpu

f32 = jnp.float32
bf16 = jnp.bfloat16
SDS = jax.ShapeDtypeStruct

EPS = 1e-6
CHUNK = 64
HD = 128
GDN_QK_HEADS = 8
GDN_V_HEADS = 16
GDN_QK_W = GDN_QK_HEADS * HD
GDN_V_W = GDN_V_HEADS * HD
GDN_CONV = 2 * GDN_QK_W + GDN_V_W
GDN_IN = GDN_CONV + GDN_V_W + 2 * GDN_V_HEADS
GDN_IN_PAD = GDN_CONV + GDN_V_W + 256
GDN_TN = 640
FOX_H = 16
FOX_D = 64
FOX_W = FOX_H * FOX_D
FOX_IN = 4 * FOX_W + FOX_H
FOX_IN_PAD = 4 * FOX_W + 128
FOX_TN = 1408
FOX_PW = FOX_H * 128
NDEV = 8
MESH_AXES = ("x", "y", "c")
NEG = -1e30

ADAM_LR = 0.001
ADAM_B1 = 0.9
ADAM_B2 = 0.999
ADAM_EPS = 1e-08
ADAM_WD = 0.01
ADAM_STEP = 10

VMEM_LIMIT = 56 * 1024 * 1024


def _cp(sem=None):
    return pltpu.CompilerParams(dimension_semantics=sem, vmem_limit_bytes=VMEM_LIMIT)


def _bdot(a, b, dims):
    return lax.dot_general(a.astype(bf16), b.astype(bf16), (dims, ((), ())), preferred_element_type=f32)


def _nn(a, b):
    return _bdot(a, b, ((1,), (0,)))


def _nt(a, b):
    return _bdot(a, b, ((1,), (1,)))


def _tn(a, b):
    return _bdot(a, b, ((0,), (0,)))


def _hdot(a, b, dims=((1,), (0,))):
    return lax.dot_general(a, b, (dims, ((), ())), precision=lax.Precision.HIGHEST, preferred_element_type=f32)


def _split2(a):
    hi = a.astype(bf16)
    return hi, (a - hi.astype(f32)).astype(bf16)


def _dot3(a, b):
    (ah, al), (bh, bl) = a, b
    return (jnp.dot(ah, bh, preferred_element_type=f32) + jnp.dot(ah, bl, preferred_element_type=f32)
            + jnp.dot(al, bh, preferred_element_type=f32))


@jax.custom_vjp
def _mm(a, b):
    return _nn(a, b)


_mm.defvjp(lambda a, b: (_nn(a, b), (a, b)), lambda r, g: (_nt(g, r[1]), _tn(r[0], g)))


@jax.custom_vjp
def _mm_nt(a, b):
    return _nt(a, b)


_mm_nt.defvjp(lambda a, b: (_nt(a, b), (a, b)), lambda r, g: (_nn(g, r[1]), _tn(g, r[0])))


@jax.custom_vjp
def _mm_tn(a, b):
    return _tn(a, b)


_mm_tn.defvjp(lambda a, b: (_tn(a, b), (a, b)), lambda r, g: (_nt(r[1], g), _nn(r[0], g)))


def _silu(x):
    return x * jax.nn.sigmoid(x)


def _rms_mod(x, nw, scale, shift):
    r = lax.rsqrt(jnp.mean(x * x, axis=-1, keepdims=True) + EPS)
    return (x * r * nw) * (1.0 + scale) + shift


def _rows(S, want):
    return min(want, S)


def _my_pos():
    return lax.axis_index("x"), lax.axis_index("y"), lax.axis_index("c")


def _exchange_copies(kind, x_refs, out_refs, send_sems, recv_sems, local_sems):
    x_, y_, c_ = _my_pos()
    me = 4 * x_ + 2 * y_ + c_
    own = kind == "gather"
    cps = [pltpu.make_async_copy(x_refs[a] if own else x_refs[a].at[me], out_refs[a].at[me], local_sems.at[a])
           for a in range(len(x_refs))]
    for rel in range(1, NDEV):
        px = (x_ + ((rel >> 2) & 1)) % 2
        py = (y_ + ((rel >> 1) & 1)) % 2
        pc = (c_ + (rel & 1)) % 2
        for a in range(len(x_refs)):
            cps.append(pltpu.make_async_remote_copy(
                src_ref=x_refs[a] if own else x_refs[a].at[4 * px + 2 * py + pc], dst_ref=out_refs[a].at[me],
                send_sem=send_sems.at[rel - 1, a], recv_sem=recv_sems.at[rel - 1, a],
                device_id=(px, py, pc), device_id_type=pl.DeviceIdType.MESH))
    return cps


def _exchange_scratch(n):
    return [pltpu.SemaphoreType.DMA((NDEV - 1, n)), pltpu.SemaphoreType.DMA((NDEV - 1, n)), pltpu.SemaphoreType.DMA((n,))]


def _call(body, *, name, grid, in_specs, out_specs, out_shape, args, scratch=(), side=None):
    params = _cp(("arbitrary",) * len(grid))
    if side is None:
        return pl.pallas_call(body, name=name, grid=grid, in_specs=in_specs, out_specs=out_specs, out_shape=out_shape,
                              scratch_shapes=list(scratch), compiler_params=params)(*args)
    kind, xs = side
    n_in, n_out, n_scr, ns = len(in_specs), len(out_shape), len(scratch), len(xs)
    steps = 1
    for g in grid:
        steps *= g

    def wrapped(*refs):
        o0 = n_in + ns
        s0 = o0 + n_out + ns
        step = pl.program_id(0)
        for d in range(1, len(grid)):
            step = step * grid[d] + pl.program_id(d)

        def copies():
            return _exchange_copies(kind, refs[n_in:o0], refs[o0 + n_out:s0], *refs[s0 + n_scr:])

        @pl.when(step == 0)
        def _():
            for cp in copies():
                cp.start()

        body(*refs[:n_in], *refs[o0:o0 + n_out], *refs[s0:s0 + n_scr])

        @pl.when(step == steps - 1)
        def _():
            for cp in copies():
                cp.wait()

    any_ = pl.BlockSpec(memory_space=pl.ANY)
    side_shapes = [SDS((NDEV,) + x.shape if kind == "gather" else x.shape, x.dtype) for x in xs]
    outs = pl.pallas_call(wrapped, name=name, grid=grid, in_specs=list(in_specs) + [any_] * ns,
                          out_specs=list(out_specs) + [any_] * ns, out_shape=list(out_shape) + side_shapes,
                          scratch_shapes=list(scratch) + _exchange_scratch(ns), compiler_params=params)(*args, *xs)
    return outs[:n_out], outs[n_out:]


def inproj_fwd(x, nw, scale, shift, W, tn, name, side=None):
    S, D = x.shape
    N = W.shape[1]
    tm = _rows(S, 512)

    def body(x_ref, nw_ref, sc_ref, sh_ref, w_ref, proj_ref, h_ref):
        @pl.when(pl.program_id(1) == 0)
        def _():
            h_ref[...] = _rms_mod(x_ref[...], nw_ref[...], sc_ref[...], sh_ref[...]).astype(bf16)

        proj_ref[...] = jnp.dot(h_ref[...], w_ref[...], preferred_element_type=f32)

    vec = pl.BlockSpec((1, D), lambda i, j: (0, 0))
    return _call(
        body, name=name, grid=(S // tm, N // tn),
        in_specs=[pl.BlockSpec((tm, D), lambda i, j: (i, 0)), vec, vec, vec, pl.BlockSpec((D, tn), lambda i, j: (0, j))],
        out_specs=[pl.BlockSpec((tm, tn), lambda i, j: (i, j)), pl.BlockSpec((tm, D), lambda i, j: (i, 0))],
        out_shape=[SDS((S, N), f32), SDS((S, D), bf16)], args=(x, nw, scale, shift, W), side=side)


def inproj_bwd_x(x, nw, scale, shift, W, dproj, dx_res, tn, name, side=None):
    S, D = x.shape
    N = W.shape[1]
    tm = _rows(S, 512)
    nj = N // tn

    def body(x_ref, nw_ref, sc_ref, sh_ref, w_ref, dp_ref, dxr_ref, dx_ref, dnw_ref, dsc_ref, dsh_ref, acc):
        i, j = pl.program_id(0), pl.program_id(1)

        @pl.when(j == 0)
        def _():
            acc[...] = jnp.zeros_like(acc)

        @pl.when((i == 0) & (j == 0))
        def _():
            dnw_ref[...] = jnp.zeros_like(dnw_ref)
            dsc_ref[...] = jnp.zeros_like(dsc_ref)
            dsh_ref[...] = jnp.zeros_like(dsh_ref)

        acc[...] += _nt(dp_ref[...], w_ref[...])

        @pl.when(j == nj - 1)
        def _():
            _, vjp = jax.vjp(_rms_mod, x_ref[...], nw_ref[...], sc_ref[...], sh_ref[...])
            dx, dnw, dsc, dsh = vjp(acc[...])
            dx_ref[...] = dxr_ref[...] + dx
            dnw_ref[...] += dnw
            dsc_ref[...] += dsc
            dsh_ref[...] += dsh

    vec = pl.BlockSpec((1, D), lambda i, j: (0, 0))
    row = pl.BlockSpec((tm, D), lambda i, j: (i, 0))
    return _call(
        body, name=name, grid=(S // tm, nj),
        in_specs=[row, vec, vec, vec, pl.BlockSpec((D, tn), lambda i, j: (0, j)), pl.BlockSpec((tm, tn), lambda i, j: (i, j)), row],
        out_specs=[row, vec, vec, vec],
        out_shape=[SDS((S, D), f32), SDS((1, D), f32), SDS((1, D), f32), SDS((1, D), f32)],
        scratch=[pltpu.VMEM((tm, D), f32)], args=(x, nw, scale, shift, W, dproj, dx_res), side=side)


def matmul_tn(a, b, tn, name, side=None):
    S, K = a.shape
    N = b.shape[1]
    tm = _rows(S, 512)
    ni = S // tm

    def body(a_ref, b_ref, o_ref):
        @pl.when(pl.program_id(1) == 0)
        def _():
            o_ref[...] = jnp.zeros_like(o_ref)

        o_ref[...] += _tn(a_ref[...], b_ref[...])

    return _call(
        body, name=name, grid=(N // tn, ni),
        in_specs=[pl.BlockSpec((tm, K), lambda j, i: (i, 0)), pl.BlockSpec((tm, tn), lambda j, i: (i, j))],
        out_specs=[pl.BlockSpec((K, tn), lambda j, i: (0, j))],
        out_shape=[SDS((K, N), f32)], args=(a, b), side=side)


def _conv_taps(xs, w, n_out):
    taps = []
    for j in range(4):
        s = 3 - j
        sh = xs if s == 0 else pltpu.roll(xs, s, axis=0)
        taps.append(sh[8:8 + n_out])
    conv = taps[0] * w[0] + taps[1] * w[1] + taps[2] * w[2] + taps[3] * w[3]
    return taps, conv


def _act_norm(conv, mul_norm, mul_plain):
    s = _silu(conv)
    r = lax.rsqrt(jnp.sum(s * s, axis=-1, keepdims=True) + EPS)
    return s * (mul_norm * r + mul_plain)


def _gdn_prep_mults(j):
    is_q = j < GDN_QK_HEADS
    is_k = (j >= GDN_QK_HEADS) & (j < 2 * GDN_QK_HEADS)
    mul_norm = jnp.where(is_q, HD ** -0.5, jnp.where(is_k, 1.0, 0.0)).astype(f32)
    mul_plain = jnp.where(is_q | is_k, 0.0, 1.0).astype(f32)
    return mul_norm, mul_plain


def gdn_prep_fwd(proj, conv_w, name):
    S = proj.shape[0]
    R = _rows(S, 512)

    def body(x_ref, w_ref, o_ref):
        mul_norm, mul_plain = _gdn_prep_mults(pl.program_id(0))
        w = [w_ref[j:j + 1, :] for j in range(4)]

        def piece(r, c):
            t0 = pl.multiple_of(r * R, R)
            cur = x_ref[pl.ds(t0, R), :]
            prev = x_ref[pl.ds(pl.multiple_of(jnp.maximum(t0 - 8, 0), 8), 8), :]
            prev = jnp.where(r == 0, 0.0, prev)
            _, conv = _conv_taps(jnp.concatenate([prev, cur], axis=0), w, R)
            o_ref[pl.ds(t0, R), :] = _act_norm(conv, mul_norm, mul_plain)
            return c

        lax.fori_loop(0, S // R, piece, 0)

    return pl.pallas_call(
        body, name=name, grid=(GDN_CONV // 128,),
        in_specs=[pl.BlockSpec((S, 128), lambda j: (0, j)), pl.BlockSpec((4, 128), lambda j: (0, j))],
        out_specs=pl.BlockSpec((S, 128), lambda j: (0, j)),
        out_shape=SDS((S, GDN_CONV), f32),
        compiler_params=_cp(("arbitrary",)),
    )(proj, conv_w)


def gdn_prep_bwd(proj, conv_w, dqkvc, name):
    S = proj.shape[0]
    R = _rows(S, 512)
    NP = S // R

    def body(x_ref, w_ref, dn_ref, dx_ref, dw_ref):
        mul_norm, mul_plain = _gdn_prep_mults(pl.program_id(0))
        w = [w_ref[j:j + 1, :] for j in range(4)]

        def piece(r, dw):
            t0 = pl.multiple_of(r * R, R)
            cur = x_ref[pl.ds(t0, R), :]
            prev = x_ref[pl.ds(pl.multiple_of(jnp.maximum(t0 - 8, 0), 8), 8), :]
            prev = jnp.where(r == 0, 0.0, prev)
            nxt0 = pl.multiple_of(jnp.minimum(t0 + R, S - 8), 8)
            nxt = x_ref[pl.ds(nxt0, 8), :]
            dn_cur = dn_ref[pl.ds(t0, R), :]
            dn_nxt = jnp.where(r == NP - 1, 0.0, dn_ref[pl.ds(nxt0, 8), :])
            xs = jnp.concatenate([prev, cur, nxt], axis=0)
            taps, conv = _conv_taps(xs, w, R + 8)
            dn = jnp.concatenate([dn_cur, dn_nxt], axis=0)
            _, vjp = jax.vjp(lambda c: _act_norm(c, mul_norm, mul_plain), conv)
            dxc = vjp(dn)[0]
            n = R + 8
            dx = dxc[0:R] * w[3]
            for j in range(3):
                s = 3 - j
                dx = dx + pltpu.roll(dxc, n - s, axis=0)[0:R] * w[j]
            dx_ref[pl.ds(t0, R), :] = dx
            return tuple(dw[j] + jnp.sum(dxc[0:R] * taps[j][0:R], axis=0, keepdims=True) for j in range(4))

        dw = lax.fori_loop(0, NP, piece, tuple(jnp.zeros((1, 128), f32) for _ in range(4)))
        for j in range(4):
            dw_ref[j:j + 1, :] = dw[j]

    col = pl.BlockSpec((S, 128), lambda j: (0, j))
    wsp = pl.BlockSpec((4, 128), lambda j: (0, j))
    return pl.pallas_call(
        body, name=name, grid=(GDN_CONV // 128,),
        in_specs=[col, wsp, col], out_specs=[col, wsp],
        out_shape=[SDS((S, GDN_CONV), f32), SDS((4, GDN_CONV), f32)],
        compiler_params=_cp(("arbitrary",)),
    )(proj, conv_w, dqkvc)


def _chunk_tril(R):
    ii = lax.broadcasted_iota(jnp.int32, (R, R), 0)
    jj = lax.broadcasted_iota(jnp.int32, (R, R), 1)
    return ((ii // CHUNK == jj // CHUNK) & (ii >= jj)).astype(f32)


def _gdn_gates(b, a, A_log, dt_bias, tril):
    beta = jax.nn.sigmoid(b)
    g = -jnp.exp(A_log) * jax.nn.softplus(a + dt_bias)
    return _hdot(tril, g), beta


_GDN_B_BLK = (GDN_CONV + GDN_V_W) // 128
_GDN_A_BLK = _GDN_B_BLK + 1


def gdn_gates_fwd(proj, A_log, dt_bias, name):
    S = proj.shape[0]
    R = _rows(S, 512)

    def body(b_ref, a_ref, al_ref, dt_ref, gc_ref, be_ref):
        gc, be = _gdn_gates(b_ref[...], a_ref[...], al_ref[...], dt_ref[...], _chunk_tril(R))
        gc_ref[...] = gc
        be_ref[...] = be

    vec = pl.BlockSpec((1, 128), lambda i: (0, 0))
    blk = pl.BlockSpec((R, 128), lambda i: (i, 0))
    return pl.pallas_call(
        body, name=name, grid=(S // R,),
        in_specs=[pl.BlockSpec((R, 128), lambda i: (i, _GDN_B_BLK)), pl.BlockSpec((R, 128), lambda i: (i, _GDN_A_BLK)), vec, vec],
        out_specs=[blk, blk], out_shape=[SDS((S, 128), f32), SDS((S, 128), f32)],
        compiler_params=_cp(("arbitrary",)),
    )(proj, proj, A_log, dt_bias)


def gdn_gates_bwd(proj, A_log, dt_bias, dgc, dbeta, name):
    S = proj.shape[0]
    R = _rows(S, 512)

    def body(b_ref, a_ref, al_ref, dt_ref, dgc_ref, dbe_ref, db_ref, da_ref, dal_ref, ddt_ref):
        @pl.when(pl.program_id(0) == 0)
        def _():
            dal_ref[...] = jnp.zeros_like(dal_ref)
            ddt_ref[...] = jnp.zeros_like(ddt_ref)

        tril = _chunk_tril(R)
        _, vjp = jax.vjp(lambda b, a, al, dt: _gdn_gates(b, a, al, dt, tril), b_ref[...], a_ref[...], al_ref[...], dt_ref[...])
        db, da, dal, ddt = vjp((dgc_ref[...], dbe_ref[...]))
        db_ref[...] = db
        da_ref[...] = da
        dal_ref[...] += dal
        ddt_ref[...] += ddt

    vec = pl.BlockSpec((1, 128), lambda i: (0, 0))
    blk = pl.BlockSpec((R, 128), lambda i: (i, 0))
    return pl.pallas_call(
        body, name=name, grid=(S // R,),
        in_specs=[pl.BlockSpec((R, 128), lambda i: (i, _GDN_B_BLK)), pl.BlockSpec((R, 128), lambda i: (i, _GDN_A_BLK)), vec, vec, blk, blk],
        out_specs=[blk, blk, vec, vec],
        out_shape=[SDS((S, 128), f32), SDS((S, 128), f32), SDS((1, 128), f32), SDS((1, 128), f32)],
        compiler_params=_cp(("arbitrary",)),
    )(proj, proj, A_log, dt_bias, dgc, dbeta)


@jax.custom_vjp
def _inv_given(L, T):
    return T


def _inv_given_bwd(T, ct):
    dL = -_nt(_tn(T, ct), T)
    return dL, jnp.zeros_like(T)


_inv_given.defvjp(lambda L, T: (T, T), _inv_given_bwd)


REP = GDN_V_HEADS // GDN_QK_HEADS


def _gdn_intra_all(qs, ks, vs, gcols, bcols, Ts=None):
    H = len(vs)
    C = vs[0].shape[0]
    ii = lax.broadcasted_iota(jnp.int32, (C, C), 0)
    jj = lax.broadcasted_iota(jnp.int32, (C, C), 1)
    grows = [jnp.sum(jnp.where(ii == jj, g, 0.0), axis=0, keepdims=True) for g in gcols]
    decs = [jnp.exp(jnp.where(ii >= jj, gcols[h] - grows[h], NEG)) for h in range(H)]
    kbs = [ks[h // REP] * bcols[h] for h in range(H)]
    As = [_mm_nt(kbs[h], ks[h // REP]) for h in range(H)]
    Ls = [jnp.where(ii > jj, As[h] * decs[h], 0.0) for h in range(H)]
    if Ts is None:
        T = _neumann_inv_batched(Ls)
    else:
        T = [_inv_given(Ls[h], Ts[h]) for h in range(H)]
    us = [_mm(T[h], vs[h] * bcols[h]) for h in range(H)]
    ws = [_mm(T[h], kbs[h] * jnp.exp(gcols[h])) for h in range(H)]
    qk = [_mm_nt(qs[p], ks[p]) for p in range(H // REP)]
    return us, ws, [qk[h // REP] * decs[h] for h in range(H)], T


def _neumann_inv_batched(Ls):
    n, C = 4, Ls[0].shape[0]
    r0 = lax.broadcasted_iota(jnp.int32, (n * C, n * C), 0)
    c0 = lax.broadcasted_iota(jnp.int32, (n * C, n * C), 1)
    same = (r0 // C) == (c0 // C)

    def blockdiag(split):
        return tuple(jnp.where(same, jnp.concatenate([x] * n, axis=0), jnp.zeros((), bf16)) for x in split)

    Ms = [jnp.concatenate(Ls[b:b + n], axis=1) for b in range(0, len(Ls), n)]
    eye = (lax.broadcasted_iota(jnp.int32, (C, n * C), 0) == (lax.broadcasted_iota(jnp.int32, (C, n * C), 1) & (C - 1))).astype(f32)
    Ps = [eye - M for M in Ms]
    Ss = [_split2(M) for M in Ms]
    Bs = [blockdiag(S) for S in Ss]
    k = 1
    while 2 * k < C:
        Ss = [_split2(_dot3(S, B)) for S, B in zip(Ss, Bs)]
        Bs = [blockdiag(S) for S in Ss]
        Ps = [P + _dot3(_split2(P), B) for P, B in zip(Ps, Bs)]
        k *= 2
    return [P[:, h * C:(h + 1) * C] for P in Ps for h in range(n)]


def _gdn_scan_all(qs, ks, gcols, us, ws, attns, S0s):
    H = len(us)
    C = us[0].shape[0]
    last = lax.broadcasted_iota(jnp.int32, (C, 1), 0) == C - 1
    glast = [jnp.sum(jnp.where(last, g, 0.0), axis=0, keepdims=True) for g in gcols]
    wS = [_mm(ws[h], S0s[h]) for h in range(H)]
    qS = [_mm(qs[h // REP] * jnp.exp(gcols[h]), S0s[h]) for h in range(H)]
    vn = [us[h] - wS[h] for h in range(H)]
    av = [_mm(attns[h], vn[h]) for h in range(H)]
    kv = [_mm_tn(ks[h // REP] * jnp.exp(glast[h] - gcols[h]), vn[h]) for h in range(H)]
    return [qS[h] + av[h] for h in range(H)], [S0s[h] * jnp.exp(glast[h]) + kv[h] for h in range(H)]


def _head_cols(blk):
    lane = lax.broadcasted_iota(jnp.int32, blk.shape, 1)
    return [jnp.sum(jnp.where(lane == h, blk, 0.0), axis=1, keepdims=True) for h in range(GDN_V_HEADS)]


def _head_lanes(cols):
    lane = lax.broadcasted_iota(jnp.int32, (cols[0].shape[0], 128), 1)
    out = jnp.zeros((cols[0].shape[0], 128), f32)
    for h, c in enumerate(cols):
        out = out + jnp.where(lane == h, c, 0.0)
    return out


def _heads(ref, n):
    return [ref[:, h * HD:(h + 1) * HD] for h in range(n)]


def _gdn_specs(NC, rv=None):
    ix = (lambda n: n) if rv is None else rv
    qs = pl.BlockSpec((CHUNK, GDN_QK_W), lambda n: (ix(n), 0))
    ks = pl.BlockSpec((CHUNK, GDN_QK_W), lambda n: (ix(n), 1))
    vs = pl.BlockSpec((CHUNK, GDN_V_W), lambda n: (ix(n), 1))
    g1 = pl.BlockSpec((CHUNK, 128), lambda n: (ix(n), 0))
    wide = pl.BlockSpec((CHUNK, GDN_V_W), lambda n: (ix(n), 0))
    sq = pl.BlockSpec((1, GDN_V_HEADS, CHUNK, CHUNK), lambda n: (ix(n), 0, 0, 0))
    st = pl.BlockSpec((1, GDN_V_HEADS, HD, HD), lambda n: (ix(n), 0, 0, 0))
    return qs, ks, vs, g1, wide, sq, st


def gdn_intra_fwd(qkvc, gc, beta, name):
    S = qkvc.shape[0]
    NC = S // CHUNK

    def body(q_ref, k_ref, v_ref, gc_ref, be_ref, u_ref, w_ref, at_ref, T_ref):
        us, ws, attns, Ts = _gdn_intra_all(_heads(q_ref, GDN_QK_HEADS), _heads(k_ref, GDN_QK_HEADS), _heads(v_ref, GDN_V_HEADS),
                                           _head_cols(gc_ref[...]), _head_cols(be_ref[...]))
        for h in range(GDN_V_HEADS):
            u_ref[:, h * HD:(h + 1) * HD] = us[h]
            w_ref[:, h * HD:(h + 1) * HD] = ws[h]
            at_ref[0, h] = attns[h]
            T_ref[0, h] = Ts[h]

    qs, ks, vs, g1, wide, sq, _ = _gdn_specs(NC)
    return pl.pallas_call(
        body, name=name, grid=(NC,),
        in_specs=[qs, ks, vs, g1, g1], out_specs=[wide, wide, sq, sq],
        out_shape=[SDS((S, GDN_V_W), f32), SDS((S, GDN_V_W), f32),
                   SDS((NC, GDN_V_HEADS, CHUNK, CHUNK), f32), SDS((NC, GDN_V_HEADS, CHUNK, CHUNK), f32)],
        compiler_params=_cp(("arbitrary",)),
    )(qkvc, qkvc, qkvc, gc, beta)


def gdn_scan_fwd(qkvc, gc, u, w, attn, name):
    S = qkvc.shape[0]
    NC = S // CHUNK

    def body(q_ref, k_ref, gc_ref, u_ref, w_ref, at_ref, o_ref, st_ref, state):
        @pl.when(pl.program_id(0) == 0)
        def _():
            state[...] = jnp.zeros_like(state)

        S0s = [state[h] for h in range(GDN_V_HEADS)]
        os_, S1s = _gdn_scan_all(_heads(q_ref, GDN_QK_HEADS), _heads(k_ref, GDN_QK_HEADS), _head_cols(gc_ref[...]),
                                 _heads(u_ref, GDN_V_HEADS), _heads(w_ref, GDN_V_HEADS),
                                 [at_ref[0, h] for h in range(GDN_V_HEADS)], S0s)
        for h in range(GDN_V_HEADS):
            o_ref[:, h * HD:(h + 1) * HD] = os_[h]
            st_ref[0, h] = S0s[h]
            state[h] = S1s[h]

    qs, ks, _, g1, wide, sq, st = _gdn_specs(NC)
    return pl.pallas_call(
        body, name=name, grid=(NC,),
        in_specs=[qs, ks, g1, wide, wide, sq], out_specs=[wide, st],
        out_shape=[SDS((S, GDN_V_W), f32), SDS((NC, GDN_V_HEADS, HD, HD), f32)],
        scratch_shapes=[pltpu.VMEM((GDN_V_HEADS, HD, HD), f32)],
        compiler_params=_cp(("arbitrary",)),
    )(qkvc, qkvc, gc, u, w, attn)


def gdn_scan_bwd(qkvc, gc, u, w, attn, states, do, name):
    S = qkvc.shape[0]
    NC = S // CHUNK

    def body(q_ref, k_ref, gc_ref, u_ref, w_ref, at_ref, st_ref, do_ref,
             dq_ref, dk_ref, dgc_ref, du_ref, dw_ref, dat_ref, dstate):
        @pl.when(pl.program_id(0) == 0)
        def _():
            dstate[...] = jnp.zeros_like(dstate)

        VH = range(GDN_V_HEADS)
        _, vjp = jax.vjp(_gdn_scan_all, _heads(q_ref, GDN_QK_HEADS), _heads(k_ref, GDN_QK_HEADS), _head_cols(gc_ref[...]),
                         _heads(u_ref, GDN_V_HEADS), _heads(w_ref, GDN_V_HEADS), [at_ref[0, h] for h in VH],
                         [st_ref[0, h] for h in VH])
        dqs, dks, dgs, dus, dws, dats, dS0s = vjp((_heads(do_ref, GDN_V_HEADS), [dstate[h] for h in VH]))
        for p in range(GDN_QK_HEADS):
            dq_ref[:, p * HD:(p + 1) * HD] = dqs[p]
            dk_ref[:, p * HD:(p + 1) * HD] = dks[p]
        for h in VH:
            du_ref[:, h * HD:(h + 1) * HD] = dus[h]
            dw_ref[:, h * HD:(h + 1) * HD] = dws[h]
            dat_ref[0, h] = dats[h]
            dstate[h] = dS0s[h]
        dgc_ref[...] = _head_lanes(dgs)

    qs, ks, _, g1, wide, sq, st = _gdn_specs(NC, lambda n: NC - 1 - n)
    dqs = pl.BlockSpec((CHUNK, GDN_QK_W), lambda n: (NC - 1 - n, 0))
    return pl.pallas_call(
        body, name=name, grid=(NC,),
        in_specs=[qs, ks, g1, wide, wide, sq, st, wide],
        out_specs=[dqs, dqs, g1, wide, wide, sq],
        out_shape=[SDS((S, GDN_QK_W), f32), SDS((S, GDN_QK_W), f32), SDS((S, 128), f32), SDS((S, GDN_V_W), f32),
                   SDS((S, GDN_V_W), f32), SDS((NC, GDN_V_HEADS, CHUNK, CHUNK), f32)],
        scratch_shapes=[pltpu.VMEM((GDN_V_HEADS, HD, HD), f32)],
        compiler_params=_cp(("arbitrary",)),
    )(qkvc, qkvc, gc, u, w, attn, states, do)


def gdn_intra_bwd(qkvc, gc, beta, Ts, du, dw, dattn, dq_s, dk_s, dgc_s, name):
    S = qkvc.shape[0]
    NC = S // CHUNK

    def body(q_ref, k_ref, v_ref, gc_ref, be_ref, T_ref, du_ref, dw_ref, dat_ref, dqs_ref, dks_ref, dgs_ref,
             dq_ref, dk_ref, dv_ref, dgc_ref, dbe_ref):
        VH = range(GDN_V_HEADS)
        Ts = [T_ref[0, h] for h in VH]
        _, vjp = jax.vjp(lambda q_, k_, v_, g_, b_: _gdn_intra_all(q_, k_, v_, g_, b_, Ts)[:3],
                         _heads(q_ref, GDN_QK_HEADS), _heads(k_ref, GDN_QK_HEADS), _heads(v_ref, GDN_V_HEADS),
                         _head_cols(gc_ref[...]), _head_cols(be_ref[...]))
        dqs, dks, dvs, dgs, dbs = vjp((_heads(du_ref, GDN_V_HEADS), _heads(dw_ref, GDN_V_HEADS), [dat_ref[0, h] for h in VH]))
        for p in range(GDN_QK_HEADS):
            dq_ref[:, p * HD:(p + 1) * HD] = dqs[p] + dqs_ref[:, p * HD:(p + 1) * HD]
            dk_ref[:, p * HD:(p + 1) * HD] = dks[p] + dks_ref[:, p * HD:(p + 1) * HD]
        for h in VH:
            dv_ref[:, h * HD:(h + 1) * HD] = dvs[h]
        dgc_ref[...] = _head_lanes(dgs) + dgs_ref[...]
        dbe_ref[...] = _head_lanes(dbs)

    qs, ks, vs, g1, wide, sq, _ = _gdn_specs(NC)
    dqs = pl.BlockSpec((CHUNK, GDN_QK_W), lambda n: (n, 0))
    dq, dk, dv, dgc, dbe = pl.pallas_call(
        body, name=name, grid=(NC,),
        in_specs=[qs, ks, vs, g1, g1, sq, wide, wide, sq, dqs, dqs, g1],
        out_specs=[dqs, dqs, wide, g1, g1],
        out_shape=[SDS((S, GDN_QK_W), f32), SDS((S, GDN_QK_W), f32), SDS((S, GDN_V_W), f32), SDS((S, 128), f32), SDS((S, 128), f32)],
        compiler_params=_cp(("arbitrary",)),
    )(qkvc, qkvc, qkvc, gc, beta, Ts, du, dw, dattn, dq_s, dk_s, dgc_s)
    return jnp.concatenate([dq, dk, dv], axis=1), dgc, dbe


def _gated_norm(o, z, nw):
    parts = []
    for h in range(GDN_V_HEADS):
        oh = o[:, h * HD:(h + 1) * HD]
        r = lax.rsqrt(jnp.mean(oh * oh, axis=-1, keepdims=True) + EPS)
        parts.append((oh * r * nw) * _silu(z[:, h * HD:(h + 1) * HD]))
    return jnp.concatenate(parts, axis=1)


def gdn_out_fwd(o, proj, nw, W, x, gate, name):
    S, D = x.shape
    tm = _rows(S, 256)

    def body(o_ref, z_ref, nw_ref, w_ref, x_ref, g_ref, xn_ref, y_ref, og_ref):
        og = _gated_norm(o_ref[...], z_ref[...], nw_ref[...]).astype(bf16)
        y = jnp.dot(og, w_ref[...], preferred_element_type=f32)
        og_ref[...] = og
        y_ref[...] = y
        xn_ref[...] = x_ref[...] + g_ref[...] * y

    row = pl.BlockSpec((tm, D), lambda i: (i, 0))
    wide = pl.BlockSpec((tm, GDN_V_W), lambda i: (i, 0))
    return pl.pallas_call(
        body, name=name, grid=(S // tm,),
        in_specs=[wide, pl.BlockSpec((tm, GDN_V_W), lambda i: (i, 2)), pl.BlockSpec((1, HD), lambda i: (0, 0)),
                  pl.BlockSpec((GDN_V_W, D), lambda i: (0, 0)), row, pl.BlockSpec((1, D), lambda i: (0, 0))],
        out_specs=[row, row, wide],
        out_shape=[SDS((S, D), f32), SDS((S, D), f32), SDS((S, GDN_V_W), bf16)],
        compiler_params=_cp(("arbitrary",)),
    )(o, proj, nw, W, x, gate)


def gdn_out_bwd(dxn, y, gate, o, proj, nw, W, name):
    S, D = dxn.shape
    tm = _rows(S, 256)

    def body(dx_ref, y_ref, g_ref, o_ref, z_ref, nw_ref, w_ref, dy_ref, dg_ref, do_ref, dz_ref, dnw_ref):
        @pl.when(pl.program_id(0) == 0)
        def _():
            dg_ref[...] = jnp.zeros_like(dg_ref)
            dnw_ref[...] = jnp.zeros_like(dnw_ref)

        dx = dx_ref[...]
        dy = dx * g_ref[...]
        dy_ref[...] = dy
        dg_ref[...] += jnp.sum(dx * y_ref[...], axis=0, keepdims=True)
        dog = _nt(dy, w_ref[...])
        _, vjp = jax.vjp(_gated_norm, o_ref[...], z_ref[...], nw_ref[...])
        do, dz, dnw = vjp(dog)
        do_ref[...] = do
        dz_ref[...] = dz
        dnw_ref[...] += dnw

    row = pl.BlockSpec((tm, D), lambda i: (i, 0))
    wide = pl.BlockSpec((tm, GDN_V_W), lambda i: (i, 0))
    vecd = pl.BlockSpec((1, D), lambda i: (0, 0))
    vech = pl.BlockSpec((1, HD), lambda i: (0, 0))
    return pl.pallas_call(
        body, name=name, grid=(S // tm,),
        in_specs=[row, row, vecd, wide, pl.BlockSpec((tm, GDN_V_W), lambda i: (i, 2)), vech, pl.BlockSpec((GDN_V_W, D), lambda i: (0, 0))],
        out_specs=[row, vecd, wide, wide, vech],
        out_shape=[SDS((S, D), f32), SDS((1, D), f32), SDS((S, GDN_V_W), f32), SDS((S, GDN_V_W), f32), SDS((1, HD), f32)],
        compiler_params=_cp(("arbitrary",)),
    )(dxn, y, gate, o, proj, nw, W)


def _rms_w(x, w):
    return (x * lax.rsqrt(jnp.mean(x * x, axis=-1, keepdims=True) + EPS)) * w


def _split3(c):
    hi = c.astype(bf16).astype(f32)
    r1 = c - hi
    mid = r1.astype(bf16).astype(f32)
    lo = (r1 - mid).astype(bf16).astype(f32)
    return hi, mid, lo


_FOX_F_BLK = 4 * FOX_W // 128


def fox_prep_fwd(proj, f_bias, qn_w, kn_w, name):
    S = proj.shape[0]
    tm = _rows(S, 256)

    def body(q_ref, k_ref, v_ref, f_ref, fb_ref, qw_ref, kw_ref, Q_ref, K_ref, V_ref, carry):
        @pl.when(pl.program_id(0) == 0)
        def _():
            carry[...] = jnp.zeros_like(carry)

        ii = lax.broadcasted_iota(jnp.int32, (tm, tm), 0)
        jj = lax.broadcasted_iota(jnp.int32, (tm, tm), 1)
        lf = jax.nn.log_sigmoid(f_ref[...] + fb_ref[...])
        cum = _hdot((ii >= jj).astype(f32), lf) + carry[...]
        carry[...] = cum[tm - 1:tm, :]
        lane = lax.broadcasted_iota(jnp.int32, (tm, FOX_D), 1)
        q, k, v = q_ref[...], k_ref[...], v_ref[...]
        for h in range(FOX_H):
            sl = slice(h * FOX_D, (h + 1) * FOX_D)
            hi, mid, lo = _split3(cum[:, h:h + 1])
            qn = _rms_w(q[:, sl], qw_ref[...]) * FOX_D ** -0.5
            kn = _rms_w(k[:, sl], kw_ref[...])
            eq = jnp.where(lane == 0, hi, jnp.where(lane == 1, mid, jnp.where(lane == 2, lo, jnp.where(lane < 6, 1.0, 0.0))))
            ek = jnp.where(lane < 3, 1.0, jnp.where(lane == 3, -hi, jnp.where(lane == 4, -mid, jnp.where(lane == 5, -lo, 0.0))))
            ev = jnp.where(lane == 0, 1.0, 0.0)
            Q_ref[:, h * 128:(h + 1) * 128] = jnp.concatenate([qn, eq], axis=1).astype(bf16)
            K_ref[:, h * 128:(h + 1) * 128] = jnp.concatenate([kn, ek], axis=1).astype(bf16)
            V_ref[:, h * 128:(h + 1) * 128] = jnp.concatenate([v[:, sl], ev], axis=1).astype(bf16)

    def colblk(c):
        return pl.BlockSpec((tm, FOX_W), lambda i: (i, c))

    pad = pl.BlockSpec((tm, FOX_PW), lambda i: (i, 0))
    return pl.pallas_call(
        body, name=name, grid=(S // tm,),
        in_specs=[colblk(0), colblk(1), colblk(2), pl.BlockSpec((tm, 128), lambda i: (i, _FOX_F_BLK)),
                  pl.BlockSpec((1, 128), lambda i: (0, 0)), pl.BlockSpec((1, FOX_D), lambda i: (0, 0)), pl.BlockSpec((1, FOX_D), lambda i: (0, 0))],
        out_specs=[pad, pad, pad],
        out_shape=[SDS((S, FOX_PW), bf16)] * 3,
        scratch_shapes=[pltpu.VMEM((1, 128), f32)],
        compiler_params=_cp(("arbitrary",)),
    )(proj, proj, proj, proj, f_bias, qn_w, kn_w)


def fox_prep_bwd(proj, f_bias, qn_w, kn_w, dQ, dK, dV, name):
    S = proj.shape[0]
    tm = _rows(S, 256)
    NB = S // tm

    def body(q_ref, k_ref, f_ref, fb_ref, qw_ref, kw_ref, dQ_ref, dK_ref, dV_ref,
             dq_ref, dk_ref, dv_ref, df_ref, dfb_ref, dqw_ref, dkw_ref, carry):
        @pl.when(pl.program_id(0) == 0)
        def _():
            carry[...] = jnp.zeros_like(carry)
            dfb_ref[...] = jnp.zeros_like(dfb_ref)
            dqw_ref[...] = jnp.zeros_like(dqw_ref)
            dkw_ref[...] = jnp.zeros_like(dkw_ref)

        q, k = q_ref[...], k_ref[...]
        lane128 = lax.broadcasted_iota(jnp.int32, (tm, 128), 1)
        dcum = jnp.zeros((tm, 128), f32)
        dqs, dks, dvs = [], [], []
        dqw = jnp.zeros((1, FOX_D), f32)
        dkw = jnp.zeros((1, FOX_D), f32)
        for h in range(FOX_H):
            sl = slice(h * FOX_D, (h + 1) * FOX_D)
            dQh = dQ_ref[:, h * 128:(h + 1) * 128]
            dKh = dK_ref[:, h * 128:(h + 1) * 128]
            _, vq = jax.vjp(lambda a, w: _rms_w(a, w) * FOX_D ** -0.5, q[:, sl], qw_ref[...])
            dqh, dw1 = vq(dQh[:, 0:FOX_D])
            _, vk = jax.vjp(_rms_w, k[:, sl], kw_ref[...])
            dkh, dw2 = vk(dKh[:, 0:FOX_D])
            dqs.append(dqh)
            dks.append(dkh)
            dvs.append(dV_ref[:, h * 128:h * 128 + FOX_D])
            dqw = dqw + dw1
            dkw = dkw + dw2
            dcum = dcum + jnp.where(lane128 == h, dQh[:, FOX_D:FOX_D + 1] - dKh[:, FOX_D + 3:FOX_D + 4], 0.0)
        dq_ref[...] = jnp.concatenate(dqs, axis=1)
        dk_ref[...] = jnp.concatenate(dks, axis=1)
        dv_ref[...] = jnp.concatenate(dvs, axis=1)
        ii = lax.broadcasted_iota(jnp.int32, (tm, tm), 0)
        jj = lax.broadcasted_iota(jnp.int32, (tm, tm), 1)
        dlf = _hdot((ii <= jj).astype(f32), dcum) + carry[...]
        carry[...] += jnp.sum(dcum, axis=0, keepdims=True)
        df = dlf * jax.nn.sigmoid(-(f_ref[...] + fb_ref[...]))
        df_ref[...] = df
        dfb_ref[...] += jnp.sum(df, axis=0, keepdims=True)
        dqw_ref[...] += dqw
        dkw_ref[...] += dkw

    rv = lambda i: NB - 1 - i

    def colblk(c):
        return pl.BlockSpec((tm, FOX_W), lambda i: (rv(i), c))

    pad = pl.BlockSpec((tm, FOX_PW), lambda i: (rv(i), 0))
    cmp_ = pl.BlockSpec((tm, FOX_W), lambda i: (rv(i), 0))
    fblk = pl.BlockSpec((tm, 128), lambda i: (rv(i), 0))
    v128 = pl.BlockSpec((1, 128), lambda i: (0, 0))
    v64 = pl.BlockSpec((1, FOX_D), lambda i: (0, 0))
    return pl.pallas_call(
        body, name=name, grid=(NB,),
        in_specs=[colblk(0), colblk(1), pl.BlockSpec((tm, 128), lambda i: (rv(i), _FOX_F_BLK)), v128, v64, v64, pad, pad, pad],
        out_specs=[cmp_, cmp_, cmp_, fblk, v128, v64, v64],
        out_shape=[SDS((S, FOX_W), f32)] * 3 + [SDS((S, 128), f32), SDS((1, 128), f32), SDS((1, FOX_D), f32), SDS((1, FOX_D), f32)],
        scratch_shapes=[pltpu.VMEM((1, 128), f32)],
        compiler_params=_cp(("arbitrary",)),
    )(proj, proj, proj, f_bias, qn_w, kn_w, dQ, dK, dV)


FOX_HB = 2


def _diag_mask(t):
    return lax.broadcasted_iota(jnp.int32, (t, t), 1) <= lax.broadcasted_iota(jnp.int32, (t, t), 0)


def fox_attn_fwd(Q, K, V, name):
    S = Q.shape[0]
    t = _rows(S, 512)

    HB = FOX_HB
    HS = [slice(h * 128, (h + 1) * 128) for h in range(HB)]

    def body(q_ref, k_ref, v_ref, o_ref, m_sc, acc_sc):
        i = pl.program_id(1)
        qs = [q_ref[:, sl] for sl in HS]
        m_sc[...] = jnp.full_like(m_sc, NEG)
        acc_sc[...] = jnp.zeros_like(acc_sc)

        def tile(j, diag):
            j0 = pl.multiple_of(j * t, t)
            ss = [_nt(qs[h], k_ref[pl.ds(j0, t), HS[h]]) for h in range(HB)]
            if diag:
                ss = [jnp.where(_diag_mask(t), s, NEG) for s in ss]
            ms = [m_sc[h] for h in range(HB)]
            m_new = [jnp.maximum(ms[h], jnp.max(ss[h], axis=1, keepdims=True)) for h in range(HB)]
            ps = [jnp.exp(ss[h] - m_new[h]) for h in range(HB)]
            pv = [_nn(ps[h], v_ref[pl.ds(j0, t), HS[h]]) for h in range(HB)]
            for h in range(HB):
                acc_sc[h] = acc_sc[h] * jnp.exp(ms[h] - m_new[h]) + pv[h]
                m_sc[h] = m_new[h]

        def off_diag(j, c):
            tile(j, False)
            return c

        lax.fori_loop(0, i, off_diag, 0)
        tile(i, True)
        lane = lax.broadcasted_iota(jnp.int32, (t, 128), 1)
        for h in range(HB):
            acc = acc_sc[h]
            l = acc[:, FOX_D:FOX_D + 1]
            o_ref[:, HS[h]] = jnp.where(lane == FOX_D, m_sc[h] + jnp.log(l), acc / l)

    blk = pl.BlockSpec((t, HB * 128), lambda h, i: (i, h))
    seq = pl.BlockSpec((S, HB * 128), lambda h, i: (0, h))
    return pl.pallas_call(
        body, name=name, grid=(FOX_H // HB, S // t),
        in_specs=[blk, seq, seq], out_specs=blk, out_shape=SDS((S, FOX_PW), f32),
        scratch_shapes=[pltpu.VMEM((HB, t, 1), f32), pltpu.VMEM((HB, t, 128), f32)],
        compiler_params=_cp(("arbitrary", "arbitrary")),
    )(Q, K, V)


def fox_attn_bwd(Q, K, V, dO, O, name):
    S = Q.shape[0]
    t = _rows(S, 512)
    nq = S // t

    HB = FOX_HB
    HS = [slice(h * 128, (h + 1) * 128) for h in range(HB)]

    def body(k_ref, v_ref, q_ref, do_ref, o_ref, dq_ref, dk_ref, dv_ref):
        j = pl.program_id(1)

        @pl.when(j == 0)
        def _():
            dq_ref[...] = jnp.zeros_like(dq_ref)

        dk_ref[...] = jnp.zeros_like(dk_ref)
        dv_ref[...] = jnp.zeros_like(dv_ref)
        ks = [k_ref[:, sl] for sl in HS]
        vs = [v_ref[:, sl] for sl in HS]

        def tile(i, diag):
            i0 = pl.multiple_of(i * t, t)
            R = range(HB)
            qs = [q_ref[pl.ds(i0, t), HS[h]] for h in R]
            dos = [do_ref[pl.ds(i0, t), HS[h]] for h in R]
            ss = [_nt(qs[h], ks[h]) - o_ref[pl.ds(i0, t), h * 128 + FOX_D:h * 128 + FOX_D + 1] for h in R]
            if diag:
                ss = [jnp.where(_diag_mask(t), s, NEG) for s in ss]
            ps = [jnp.exp(s) for s in ss]
            dps = [_nt(dos[h], vs[h]) for h in R]
            dvs = [_tn(ps[h], dos[h]) for h in R]
            dss = [(ps[h] * dps[h]).astype(bf16) for h in R]
            dks = [_tn(dss[h], qs[h]) for h in R]
            dqs = [_nn(dss[h], ks[h]) for h in R]
            for h in R:
                dv_ref[:, HS[h]] += dvs[h]
                dk_ref[:, HS[h]] += dks[h]
                dq_ref[pl.ds(i0, t), HS[h]] += dqs[h]

        tile(j, True)

        def off_diag(i, c):
            tile(i, False)
            return c

        lax.fori_loop(j + 1, nq, off_diag, 0)

    blk = pl.BlockSpec((t, HB * 128), lambda h, j: (j, h))
    seq = pl.BlockSpec((S, HB * 128), lambda h, j: (0, h))
    return pl.pallas_call(
        body, name=name, grid=(FOX_H // HB, nq),
        in_specs=[blk, blk, seq, seq, seq], out_specs=[seq, blk, blk],
        out_shape=[SDS((S, FOX_PW), f32)] * 3,
        compiler_params=_cp(("arbitrary", "arbitrary")),
    )(K, V, Q, dO, O)


def fox_out_fwd(O, proj, W, x, gate, name):
    S, D = x.shape
    tm = _rows(S, 256)

    def body(o_ref, z_ref, w_ref, x_ref, g_ref, xn_ref, y_ref, og_ref):
        z = z_ref[...]
        og = jnp.concatenate([o_ref[:, h * 128:h * 128 + FOX_D] * _silu(z[:, h * FOX_D:(h + 1) * FOX_D]) for h in range(FOX_H)],
                             axis=1).astype(bf16)
        y = jnp.dot(og, w_ref[...], preferred_element_type=f32)
        og_ref[...] = og
        y_ref[...] = y
        xn_ref[...] = x_ref[...] + g_ref[...] * y

    row = pl.BlockSpec((tm, D), lambda i: (i, 0))
    cmp_ = pl.BlockSpec((tm, FOX_W), lambda i: (i, 0))
    return pl.pallas_call(
        body, name=name, grid=(S // tm,),
        in_specs=[pl.BlockSpec((tm, FOX_PW), lambda i: (i, 0)), pl.BlockSpec((tm, FOX_W), lambda i: (i, 3)),
                  pl.BlockSpec((FOX_W, D), lambda i: (0, 0)), row, pl.BlockSpec((1, D), lambda i: (0, 0))],
        out_specs=[row, row, cmp_],
        out_shape=[SDS((S, D), f32), SDS((S, D), f32), SDS((S, FOX_W), bf16)],
        compiler_params=_cp(("arbitrary",)),
    )(O, proj, W, x, gate)


def fox_out_bwd(dxn, y, gate, O, proj, W, name):
    S, D = dxn.shape
    tm = _rows(S, 256)

    def body(dx_ref, y_ref, g_ref, o_ref, z_ref, w_ref, dy_ref, dg_ref, dO_ref, dz_ref):
        @pl.when(pl.program_id(0) == 0)
        def _():
            dg_ref[...] = jnp.zeros_like(dg_ref)

        dx = dx_ref[...]
        dy = dx * g_ref[...]
        dy_ref[...] = dy
        dg_ref[...] += jnp.sum(dx * y_ref[...], axis=0, keepdims=True)
        dog = _nt(dy, w_ref[...])
        z = z_ref[...]
        lane = lax.broadcasted_iota(jnp.int32, (tm, FOX_D), 1)
        dzs = []
        for h in range(FOX_H):
            sl = slice(h * FOX_D, (h + 1) * FOX_D)
            zh = z[:, sl]
            sg = jax.nn.sigmoid(zh)
            oh = o_ref[:, h * 128:h * 128 + FOX_D]
            doh = dog[:, sl] * (zh * sg)
            delta = jnp.sum(doh * oh, axis=1, keepdims=True)
            dO_ref[:, h * 128:(h + 1) * 128] = jnp.concatenate([doh, jnp.where(lane == 0, -delta, 0.0)], axis=1).astype(bf16)
            dzs.append(dog[:, sl] * oh * (sg * (1.0 + zh * (1.0 - sg))))
        dz_ref[...] = jnp.concatenate(dzs, axis=1)

    row = pl.BlockSpec((tm, D), lambda i: (i, 0))
    vecd = pl.BlockSpec((1, D), lambda i: (0, 0))
    pad = pl.BlockSpec((tm, FOX_PW), lambda i: (i, 0))
    return pl.pallas_call(
        body, name=name, grid=(S // tm,),
        in_specs=[row, row, vecd, pad, pl.BlockSpec((tm, FOX_W), lambda i: (i, 3)), pl.BlockSpec((FOX_W, D), lambda i: (0, 0))],
        out_specs=[row, vecd, pad, pl.BlockSpec((tm, FOX_W), lambda i: (i, 0))],
        out_shape=[SDS((S, D), f32), SDS((1, D), f32), SDS((S, FOX_PW), bf16), SDS((S, FOX_W), f32)],
        compiler_params=_cp(("arbitrary",)),
    )(dxn, y, gate, O, proj, W)


def final_loss(x, fw, target, name):
    S, D = x.shape
    tm = _rows(S, 512)

    def body(x_ref, w_ref, t_ref, l_ref, dx_ref, dw_ref):
        @pl.when(pl.program_id(0) == 0)
        def _():
            l_ref[...] = jnp.zeros_like(l_ref)
            dw_ref[...] = jnp.zeros_like(dw_ref)

        out, vjp = jax.vjp(_rms_w, x_ref[...], w_ref[...])
        err = out - t_ref[...]
        l_ref[...] += 0.5 * jnp.sum(jnp.sum(err * err, axis=1, keepdims=True) * (1.0 / D), axis=0, keepdims=True)
        dx, dw = vjp(err * (1.0 / D))
        dx_ref[...] = dx
        dw_ref[...] += dw

    row = pl.BlockSpec((tm, D), lambda i: (i, 0))
    vec = pl.BlockSpec((1, D), lambda i: (0, 0))
    return pl.pallas_call(
        body, name=name, grid=(S // tm,),
        in_specs=[row, vec, row], out_specs=[pl.BlockSpec((1, 128), lambda i: (0, 0)), row, vec],
        out_shape=[SDS((1, 128), f32), SDS((S, D), f32), SDS((1, D), f32)],
        compiler_params=_cp(("arbitrary",)),
    )(x, fw, target)


def ada_fwd(c_all, ada_w, name):
    L, D, n = ada_w.shape

    def body(c_ref, w_ref, o_ref):
        cond = jnp.concatenate([_silu(c_ref[...]), jnp.zeros((8, D), f32)], axis=0)
        o_ref[0] = _nn(cond, w_ref[0])[0:8]

    return pl.pallas_call(
        body, name=name, grid=(L,),
        in_specs=[pl.BlockSpec((NDEV, D), lambda l: (0, 0)), pl.BlockSpec((1, D, n), lambda l: (l, 0, 0))],
        out_specs=pl.BlockSpec((1, NDEV, n), lambda l: (l, 0, 0)),
        out_shape=SDS((L, NDEV, n), f32),
        compiler_params=_cp(("arbitrary",)),
    )(c_all, ada_w)


def ada_grad(c_all, dmod, name):
    L, _, n = dmod.shape
    D = c_all.shape[1]

    def body(c_ref, d_ref, o_ref):
        cond = jnp.concatenate([_silu(c_ref[...]), jnp.zeros((8, D), f32)], axis=0)
        dm = jnp.concatenate([d_ref[0], jnp.zeros((8, n), f32)], axis=0)
        o_ref[0] = _tn(cond, dm)

    return pl.pallas_call(
        body, name=name, grid=(L,),
        in_specs=[pl.BlockSpec((NDEV, D), lambda l: (0, 0)), pl.BlockSpec((1, NDEV, n), lambda l: (l, 0, 0))],
        out_specs=pl.BlockSpec((1, D, n), lambda l: (l, 0, 0)),
        out_shape=SDS((L, D, n), f32),
        compiler_params=_cp(("arbitrary",)),
    )(c_all, dmod)


def reduce_adam(parts, w, m, v, tr, name):
    n, R, C = parts.shape
    c1 = 1.0 / (1.0 - ADAM_B1 ** ADAM_STEP)
    c2 = 1.0 / (1.0 - ADAM_B2 ** ADAM_STEP)

    def body(p_ref, w_ref, m_ref, v_ref, g_ref, d_ref, nm_ref, nv_ref):
        g = p_ref[0].astype(f32)
        for s in range(1, n):
            g = g + p_ref[s].astype(f32)
        nm = ADAM_B1 * m_ref[...] + (1.0 - ADAM_B1) * g
        nv = ADAM_B2 * v_ref[...] + (1.0 - ADAM_B2) * (g * g)
        g_ref[...] = g
        nm_ref[...] = nm
        nv_ref[...] = nv
        d_ref[...] = -ADAM_LR * ((nm * c1) / (jnp.sqrt(nv * c2) + ADAM_EPS) + ADAM_WD * w_ref[...])

    blk = pl.BlockSpec((tr, C), lambda i: (i, 0))
    return pl.pallas_call(
        body, name=name, grid=(R // tr,),
        in_specs=[pl.BlockSpec((n, tr, C), lambda i: (0, i, 0)), blk, blk, blk],
        out_specs=[blk] * 4, out_shape=[SDS((R, C), f32)] * 4,
        compiler_params=_cp(("arbitrary",)),
    )(parts, w, m, v)


def all_gather(xs, name):
    n = len(xs)

    def body(*refs):
        x_refs, out_refs = refs[:n], refs[n:2 * n]
        send_sems, recv_sems, local_sems = refs[2 * n:]
        x_, y_, c_ = _my_pos()
        me, sibling = (x_, y_, c_), (x_, y_, 1 - c_)
        chips = [(1 - x_, y_), (x_, 1 - y_), (1 - x_, 1 - y_)]

        def rows(a, px, py, pc):
            return out_refs[a].at[4 * px + 2 * py + pc]

        def copy(a, k, block, to, own=False):
            return pltpu.make_async_remote_copy(
                src_ref=x_refs[a] if own else rows(a, *block), dst_ref=rows(a, *block),
                send_sem=send_sems.at[k, a], recv_sem=recv_sems.at[k, a], device_id=to, device_id_type=pl.DeviceIdType.MESH)

        mine = [pltpu.make_async_copy(x_refs[a], rows(a, *me), local_sems.at[a]) for a in range(n)]
        for cp in mine:
            cp.start()
        first = []
        for a in range(n):
            first.append(copy(a, 0, me, sibling, own=True))
            first += [copy(a, 1 + j, me, (*chip, c_), own=True) for j, chip in enumerate(chips)]
        for cp in first:
            cp.start()
        passed = []
        for j, chip in enumerate(chips):
            for a in range(n):
                copy(a, 1 + j, (*chip, c_), me).wait_recv()
                cp = copy(a, 4 + j, (*chip, c_), sibling)
                cp.start()
                passed.append(cp)
        for a in range(n):
            copy(a, 0, sibling, me).wait_recv()
            for j, chip in enumerate(chips):
                copy(a, 4 + j, (*chip, 1 - c_), me).wait_recv()
        for cp in first + passed:
            cp.wait_send()
        for cp in mine:
            cp.wait()

    any_ = pl.BlockSpec(memory_space=pl.ANY)
    return pl.pallas_call(
        body, name=name, out_shape=[SDS((NDEV,) + x.shape, x.dtype) for x in xs],
        in_specs=[any_] * n, out_specs=[any_] * n,
        scratch_shapes=[pltpu.SemaphoreType.DMA((7, n)), pltpu.SemaphoreType.DMA((7, n)), pltpu.SemaphoreType.DMA((n,))],
    )(*xs)


def all_to_all(xs, name):
    n = len(xs)

    def body(*refs):
        cps = _exchange_copies("scatter", refs[:n], refs[n:2 * n], *refs[2 * n:])
        for cp in cps:
            cp.start()
        for cp in cps:
            cp.wait()

    any_ = pl.BlockSpec(memory_space=pl.ANY)
    return pl.pallas_call(
        body, name=name, out_shape=[SDS(x.shape, x.dtype) for x in xs],
        in_specs=[any_] * n, out_specs=[any_] * n, scratch_shapes=_exchange_scratch(n),
    )(*xs)


GDN_COLS = ((0, GDN_CONV + GDN_V_W, 0), (GDN_CONV + GDN_V_W, GDN_CONV + GDN_V_W + 16, GDN_CONV + GDN_V_W),
            (GDN_CONV + GDN_V_W + 16, GDN_IN, GDN_CONV + GDN_V_W + 128))
FOX_COLS = ((0, FOX_IN, 0),)


def _col_pieces(d, per, cols):
    lo, hi = per * d, per * (d + 1)
    out = []
    for a, b, dst in cols:
        s, e = max(lo, a), min(hi, b)
        if s < e:
            out.append((s - lo, e - s, dst + s - a))
    return out


def cols_from_blocks(g, cols, n_out, name):
    _, L, R, C = g.shape
    tr = min(256, R)

    def body(g_ref, o_ref):
        o_ref[...] = jnp.zeros_like(o_ref)
        for d in range(NDEV):
            for off, ln, dst in _col_pieces(d, C, cols):
                o_ref[0, :, dst:dst + ln] = g_ref[d, 0, :, off:off + ln]

    return pl.pallas_call(
        body, name=name, grid=(L, R // tr),
        in_specs=[pl.BlockSpec((NDEV, 1, tr, C), lambda l, i: (0, l, i, 0))],
        out_specs=pl.BlockSpec((1, tr, n_out), lambda l, i: (l, i, 0)),
        out_shape=SDS((L, R, n_out), g.dtype),
        compiler_params=_cp(("arbitrary", "arbitrary")),
    )(g)


def blocks_from_cols(dw, C, cols, name):
    R, n_in = dw.shape
    tr = min(256, R)

    def body(x_ref, o_ref):
        for d in range(NDEV):
            for off, ln, src in _col_pieces(d, C, cols):
                o_ref[d, :, off:off + ln] = x_ref[:, src:src + ln].astype(bf16)

    return pl.pallas_call(
        body, name=name, grid=(R // tr,),
        in_specs=[pl.BlockSpec((tr, n_in), lambda i: (i, 0))],
        out_specs=pl.BlockSpec((NDEV, tr, C), lambda i: (0, i, 0)),
        out_shape=SDS((NDEV, R, C), bf16),
        compiler_params=_cp(("arbitrary",)),
    )(dw)


BIG = ("a_w_in", "a_conv_w", "a_w_out", "b_w_in", "b_w_out")
SMALL = ("norm_w", "ada_b", "a_A_log", "a_dt_bias", "a_norm_w", "b_f_bias", "b_qn_w", "b_kn_w", "final_norm_w")


def _pack_small(arrs):
    rows = []
    for a in arrs:
        fl = a.reshape(-1)
        pad = (-fl.shape[0]) % 128
        if pad:
            fl = jnp.concatenate([fl, jnp.zeros((pad,), fl.dtype)])
        rows.append(fl)
    flat = jnp.concatenate(rows)
    pad = (-flat.shape[0]) % (8 * 128)
    if pad:
        flat = jnp.concatenate([flat, jnp.zeros((pad,), flat.dtype)])
    return flat.reshape(-1, 128)


def _unpack(packed, shapes, align):
    flat = packed.reshape(-1)
    out, off = [], 0
    for shp in shapes:
        n = 1
        for d in shp:
            n *= d
        out.append(flat[off:off + n].reshape(shp))
        off += n + ((-n) % align)
    return out


def _full_from_gathered(g, shard_shape, axis):
    g = jnp.moveaxis(g, 0, axis)
    shp = list(shard_shape)
    shp[axis] *= NDEV
    return g.reshape(shp)


def _pad_lanes(v, n=128):
    v = v.reshape(1, -1)
    return jnp.concatenate([v, jnp.zeros((1, n - v.shape[1]), v.dtype)], axis=1)


def _with_side(res, side):
    return res if side is not None else (res, None)


def gdn_layer_fwd(x, mod, nw, weights, tag, side=None):
    W_in, conv_w, A_log, dt_bias, a_nw, W_out = weights
    shift, scale, gate = mod
    (proj, h), got = _with_side(inproj_fwd(x, nw, scale, shift, W_in, GDN_TN, f"{tag}_inproj", side), side)
    qkvc = gdn_prep_fwd(proj, conv_w, f"{tag}_prep")
    gc, beta = gdn_gates_fwd(proj, A_log, dt_bias, f"{tag}_gates")
    u, w, attn, Ts = gdn_intra_fwd(qkvc, gc, beta, f"{tag}_intra")
    o, states = gdn_scan_fwd(qkvc, gc, u, w, attn, f"{tag}_scan")
    x_new, y, og = gdn_out_fwd(o, proj, a_nw, W_out, x, gate, f"{tag}_out")
    return x_new, (x, proj, h, qkvc, gc, beta, o, states, Ts, y, og, u, w, attn), got


def gdn_layer_bwd(dxn, saved, mod, nw, weights, tag, side_dwin=None, side_ibwd=None):
    W_in, conv_w, A_log, dt_bias, a_nw, W_out = weights
    shift, scale, gate = mod
    x, proj, h, qkvc, gc, beta, o, states, Ts, y, og, u, w, attn = saved
    dy, dgate, do, dz, da_nw = gdn_out_bwd(dxn, y, gate, o, proj, a_nw, W_out, f"{tag}_out_bwd")
    dW_out, = matmul_tn(og, dy, 512, f"{tag}_dwout")
    dq_s, dk_s, dgc_s, du, dw, dattn = gdn_scan_bwd(qkvc, gc, u, w, attn, states, do, f"{tag}_scan_bwd")
    dqkvc, dgc, dbeta = gdn_intra_bwd(qkvc, gc, beta, Ts, du, dw, dattn, dq_s, dk_s, dgc_s, f"{tag}_intra_bwd")
    db, da, dA_log, ddt = gdn_gates_bwd(proj, A_log, dt_bias, dgc, dbeta, f"{tag}_gates_bwd")
    dqkv, dconv_w = gdn_prep_bwd(proj, conv_w, dqkvc, f"{tag}_prep_bwd")
    dproj = jnp.concatenate([dqkv, dz, db, da], axis=1)
    (dW_in,), got1 = _with_side(matmul_tn(h, dproj, GDN_TN, f"{tag}_dwin", side_dwin), side_dwin)
    side2 = side_ibwd(dW_out, dW_in, dconv_w) if side_ibwd is not None else None
    (dx, dnw, dscale, dshift), got2 = _with_side(
        inproj_bwd_x(x, nw, scale, shift, W_in, dproj, dxn, GDN_TN, f"{tag}_inproj_bwd", side2), side2)
    grads = dict(norm_w=dnw, W_in=dW_in, conv_w=dconv_w, A_log=dA_log[:, :16], dt_bias=ddt[:, :16], a_nw=da_nw, W_out=dW_out,
                 dmod=jnp.concatenate([dshift, dscale, dgate], axis=1))
    return dx, grads, got1, got2


def fox_layer_fwd(x, mod, nw, weights, tag, side=None):
    W_in, f_bias, qn_w, kn_w, W_out = weights
    shift, scale, gate = mod
    (proj, h), got = _with_side(inproj_fwd(x, nw, scale, shift, W_in, FOX_TN, f"{tag}_inproj", side), side)
    Q, K, V = fox_prep_fwd(proj, f_bias, qn_w, kn_w, f"{tag}_prep")
    O = fox_attn_fwd(Q, K, V, f"{tag}_attn")
    x_new, y, og = fox_out_fwd(O, proj, W_out, x, gate, f"{tag}_out")
    return x_new, (x, proj, h, Q, K, V, O, y, og), got


def fox_layer_bwd(dxn, saved, mod, nw, weights, tag, side_dwin=None):
    W_in, f_bias, qn_w, kn_w, W_out = weights
    shift, scale, gate = mod
    x, proj, h, Q, K, V, O, y, og = saved
    dy, dgate, dO, dz = fox_out_bwd(dxn, y, gate, O, proj, W_out, f"{tag}_out_bwd")
    dW_out, = matmul_tn(og, dy, 512, f"{tag}_dwout")
    dQ, dK, dV = fox_attn_bwd(Q, K, V, dO, O, f"{tag}_attn_bwd")
    dq, dk, dv, df, dfb, dqw, dkw = fox_prep_bwd(proj, f_bias, qn_w, kn_w, dQ, dK, dV, f"{tag}_prep_bwd")
    dproj = jnp.concatenate([dq, dk, dv, dz, df], axis=1)
    (dW_in,), got1 = _with_side(matmul_tn(h, dproj, FOX_TN, f"{tag}_dwin", side_dwin), side_dwin)
    dx, dnw, dscale, dshift = inproj_bwd_x(x, nw, scale, shift, W_in, dproj, dxn, FOX_TN, f"{tag}_inproj_bwd")
    grads = dict(norm_w=dnw, W_in=dW_in, f_bias=dfb[:, :16], qn_w=dqw, kn_w=dkw, W_out=dW_out,
                 dmod=jnp.concatenate([dshift, dscale, dgate], axis=1))
    return dx, grads, got1


class LocalPlan:
    def __init__(self, full):
        self.full = full

    def layer_weights(self, i):
        j, f = i // 2, self.full
        return (f["a_w_in"][j], f["a_w_out"][j], f["a_conv_w"][j]) if i % 2 == 0 else (f["b_w_in"][j], f["b_w_out"][j])

    def fwd_side(self, i):
        return None

    def fwd_got(self, i, got):
        pass

    def dwin_side(self, i):
        return None

    def ibwd_side(self, i):
        return None

    def bwd_got(self, i, grads, got1, got2):
        pass


def device_step(x, mod_all, norm_w, small, final_norm_w, target, plan):
    D = x.shape[1]
    mods = [(mod_all[i:i + 1, 0:D], mod_all[i:i + 1, D:2 * D], mod_all[i:i + 1, 2 * D:3 * D]) for i in range(4)]

    def weights(i):
        j = i // 2
        if i % 2 == 0:
            W_in, W_out, conv_w = plan.layer_weights(i)
            return (W_in, conv_w, _pad_lanes(small["a_A_log"][j]), _pad_lanes(small["a_dt_bias"][j]), small["a_norm_w"][j:j + 1], W_out)
        W_in, W_out = plan.layer_weights(i)
        return (W_in, _pad_lanes(small["b_f_bias"][j]), small["b_qn_w"][j:j + 1], small["b_kn_w"][j:j + 1], W_out)

    saved, wts = [], []
    for i in range(4):
        wts.append(weights(i))
        fwd = gdn_layer_fwd if i % 2 == 0 else fox_layer_fwd
        x, sv, got = fwd(x, mods[i], norm_w[i:i + 1], wts[i], f"L{i}", plan.fwd_side(i))
        plan.fwd_got(i, got)
        saved.append(sv)
    loss, dx, dfw = final_loss(x, final_norm_w.reshape(1, D), target, "final_loss")
    lg = [None] * 4
    for i in reversed(range(4)):
        if i % 2 == 0:
            dx, lg[i], got1, got2 = gdn_layer_bwd(dx, saved[i], mods[i], norm_w[i:i + 1], wts[i], f"L{i}", plan.dwin_side(i), plan.ibwd_side(i))
        else:
            dx, lg[i], got1 = fox_layer_bwd(dx, saved[i], mods[i], norm_w[i:i + 1], wts[i], f"L{i}", plan.dwin_side(i))
            got2 = None
        plan.bwd_got(i, lg[i], got1, got2)
    g = dict(
        norm_w=jnp.concatenate([lg[i]["norm_w"] for i in range(4)], axis=0),
        dmod=jnp.concatenate([lg[i]["dmod"] for i in range(4)], axis=0),
        a_w_in=[lg[i]["W_in"] for i in (0, 2)],
        a_conv_w=jnp.stack([lg[i]["conv_w"] for i in (0, 2)]),
        a_A_log=jnp.concatenate([lg[i]["A_log"] for i in (0, 2)], axis=0),
        a_dt_bias=jnp.concatenate([lg[i]["dt_bias"] for i in (0, 2)], axis=0),
        a_norm_w=jnp.concatenate([lg[i]["a_nw"] for i in (0, 2)], axis=0),
        a_w_out=[lg[i]["W_out"] for i in (0, 2)],
        b_w_in=[lg[i]["W_in"] for i in (1, 3)],
        b_f_bias=jnp.concatenate([lg[i]["f_bias"] for i in (1, 3)], axis=0),
        b_qn_w=jnp.concatenate([lg[i]["qn_w"] for i in (1, 3)], axis=0),
        b_kn_w=jnp.concatenate([lg[i]["kn_w"] for i in (1, 3)], axis=0),
        b_w_out=[lg[i]["W_out"] for i in (1, 3)],
        final_norm_w=dfw.reshape(-1),
    )
    return loss[0, 0], dx, g


class MeshPlan:
    def __init__(self, shards, w0, conv_full):
        self.shards = shards
        self.w = {0: w0}
        self.conv = conv_full
        self.recv = {}
        self.pending = {}

    def layer_weights(self, i):
        return self.w[i]

    def _shard2d(self, name, j):
        s = self.shards[name][j]
        return s.reshape(-1, s.shape[-1])

    def fwd_side(self, i):
        if i == 3:
            return None
        names = ("b_w_in", "b_w_out") if i % 2 == 0 else ("a_w_in", "a_w_out")
        return ("gather", [self._shard2d(n, (i + 1) // 2) for n in names])

    def fwd_got(self, i, got):
        if got is None:
            return
        j = (i + 1) // 2
        g_in, g_out = got
        D = g_out.shape[-1]
        if i % 2 == 0:
            W_in = cols_from_blocks(g_in[:, None], FOX_COLS, FOX_IN_PAD, f"b_w_in_cols{j}")[0]
            self.w[i + 1] = (W_in, g_out.reshape(-1, D))
        else:
            W_in = cols_from_blocks(g_in[:, None], GDN_COLS, GDN_IN_PAD, f"a_w_in_cols{j}")[0]
            self.w[i + 1] = (W_in, g_out.reshape(-1, D), self.conv[j])

    @staticmethod
    def _out_blocks(dW_out):
        return dW_out.astype(bf16).reshape(NDEV, -1, dW_out.shape[-1])

    def dwin_side(self, i):
        p = self.pending.pop(("dwin", i), None)
        if p is None:
            return None
        self._names = p[0]
        return ("scatter", p[1])

    def ibwd_side(self, i):
        if i == 2:
            def side(dW_out, dW_in, dconv_w):
                self._names2 = [("a_w_in", 1)]
                return ("scatter", [blocks_from_cols(dW_in, self.shards["a_w_in"].shape[-1], GDN_COLS, "a_w_in_blocks1")])
            return side
        if i == 0:
            def side(dW_out, dW_in, dconv_w):
                self._names2 = [("a_w_out", 0), ("a_conv_w", None)]
                conv = jnp.stack([dconv_w, self._dconv1])
                n = conv.shape[-1] // NDEV
                conv = jnp.moveaxis(conv.reshape(2, 4, NDEV, n), 2, 0).reshape(NDEV, 8, n)
                return ("scatter", [self._out_blocks(dW_out), conv])
            return side
        return None

    def bwd_got(self, i, grads, got1, got2):
        if got1 is not None:
            self.recv.update(zip(self._names, got1))
        if got2 is not None:
            self.recv.update(zip(self._names2, got2))
        if i == 3:
            blk = blocks_from_cols(grads["W_in"], self.shards["b_w_in"].shape[-1], FOX_COLS, "b_w_in_blocks1")
            self.pending[("dwin", 2)] = ([("b_w_in", 1), ("b_w_out", 1)], [blk, self._out_blocks(grads["W_out"])])
        elif i == 2:
            self.pending[("dwin", 1)] = ([("a_w_out", 1)], [self._out_blocks(grads["W_out"])])
            self._dconv1 = grads["conv_w"]
        elif i == 1:
            blk = blocks_from_cols(grads["W_in"], self.shards["b_w_in"].shape[-1], FOX_COLS, "b_w_in_blocks0")
            self.pending[("dwin", 0)] = ([("b_w_in", 0), ("b_w_out", 0)], [blk, self._out_blocks(grads["W_out"])])
        else:
            blk = blocks_from_cols(grads["W_in"], self.shards["a_w_in"].shape[-1], GDN_COLS, "a_w_in_blocks0")
            self.recv[("a_w_in", 0)], = all_to_all([blk], "scatter_last")


def kernel(x, c, norm_w, ada_w, ada_b, a_w_in, a_conv_w, a_A_log, a_dt_bias, a_norm_w, a_w_out, b_w_in, b_f_bias, b_qn_w, b_kn_w, b_w_out, final_norm_w, loss_target, m_norm_w, m_ada_w, m_ada_b, m_a_w_in, m_a_conv_w, m_a_A_log, m_a_dt_bias, m_a_norm_w, m_a_w_out, m_b_w_in, m_b_f_bias, m_b_qn_w, m_b_kn_w, m_b_w_out, m_final_norm_w, v_norm_w, v_ada_w, v_ada_b, v_a_w_in, v_a_conv_w, v_a_A_log, v_a_dt_bias, v_a_norm_w, v_a_w_out, v_b_w_in, v_b_f_bias, v_b_qn_w, v_b_kn_w, v_b_w_out, v_final_norm_w):
    W = dict(norm_w=norm_w, ada_w=ada_w, ada_b=ada_b, a_w_in=a_w_in, a_conv_w=a_conv_w, a_A_log=a_A_log, a_dt_bias=a_dt_bias,
             a_norm_w=a_norm_w, a_w_out=a_w_out, b_w_in=b_w_in, b_f_bias=b_f_bias, b_qn_w=b_qn_w, b_kn_w=b_kn_w, b_w_out=b_w_out,
             final_norm_w=final_norm_w)
    M = dict(norm_w=m_norm_w, ada_w=m_ada_w, ada_b=m_ada_b, a_w_in=m_a_w_in, a_conv_w=m_a_conv_w, a_A_log=m_a_A_log,
             a_dt_bias=m_a_dt_bias, a_norm_w=m_a_norm_w, a_w_out=m_a_w_out, b_w_in=m_b_w_in, b_f_bias=m_b_f_bias, b_qn_w=m_b_qn_w,
             b_kn_w=m_b_kn_w, b_w_out=m_b_w_out, final_norm_w=m_final_norm_w)
    V = dict(norm_w=v_norm_w, ada_w=v_ada_w, ada_b=v_ada_b, a_w_in=v_a_w_in, a_conv_w=v_a_conv_w, a_A_log=v_a_A_log,
             a_dt_bias=v_a_dt_bias, a_norm_w=v_a_norm_w, a_w_out=v_a_w_out, b_w_in=v_b_w_in, b_f_bias=v_b_f_bias, b_qn_w=v_b_qn_w,
             b_kn_w=v_b_kn_w, b_w_out=v_b_w_out, final_norm_w=v_final_norm_w)
    S, D = x.shape[1], x.shape[2]
    me = 4 * lax.axis_index("x") + 2 * lax.axis_index("y") + lax.axis_index("c")
    small_shapes = [W[n].shape for n in SMALL]

    shards = {n: W[n].astype(bf16) for n in ("a_w_in", "a_w_out", "b_w_in", "b_w_out")}
    gath = all_gather([shards["a_w_in"][0], shards["a_w_out"][0], a_conv_w.reshape(8, -1), c.reshape(8, D // 8)], "gather_w0")
    conv_full = _full_from_gathered(gath[2].reshape((NDEV,) + a_conv_w.shape), a_conv_w.shape, 2)
    w0 = (cols_from_blocks(gath[0][:, None], GDN_COLS, GDN_IN_PAD, "a_w_in_cols0")[0], gath[1].reshape(-1, D), conv_full[0])
    plan = MeshPlan(shards, w0, conv_full)
    c_all = gath[3].reshape(NDEV, D)

    mod_part = ada_fwd(c_all, ada_w, "ada_fwd")
    n_ada = ada_w.shape[2]
    mod_g = all_gather([mod_part.reshape(4 * NDEV, n_ada)], "gather_mod")[0].reshape(NDEV, 4, NDEV, n_ada)
    mod_mine = lax.dynamic_index_in_dim(mod_g, me, axis=2, keepdims=False)
    mod_all = jnp.moveaxis(mod_mine, 0, 1).reshape(4, NDEV * n_ada) + ada_b

    loss, dx, g = device_step(x[0], mod_all, norm_w, W, final_norm_w, loss_target[0], plan)
    loss = lax.psum(loss, MESH_AXES)

    g_small = dict(g, ada_b=g["dmod"])
    sp = _pack_small([g_small[n] for n in SMALL])
    sp_all = all_gather([sp], "gather_small")[0]
    sw, sm, sv = (_pack_small([T[n] for n in SMALL]) for T in (W, M, V))
    sg, sd, snm, snv = (_unpack(t, small_shapes, 128) for t in reduce_adam(sp_all, sw, sm, sv, sp.shape[0], "adam_small"))

    off_b = 0
    for n, shp in zip(SMALL, small_shapes):
        if n == "ada_b":
            break
        cnt = 1
        for d in shp:
            cnt *= d
        off_b += cnt + ((-cnt) % 128)
    dmod_all = sp_all.reshape(NDEV, -1)[:, off_b:off_b + 4 * 3 * D].reshape(NDEV, 4, 3 * D)
    dmod_cols = lax.dynamic_slice_in_dim(dmod_all, me * n_ada, n_ada, axis=2)
    g_ada = ada_grad(c_all, jnp.moveaxis(dmod_cols, 0, 1), "ada_grad")
    r_ada = reduce_adam(g_ada.reshape(1, 4 * D, n_ada), *(T["ada_w"].reshape(4 * D, n_ada) for T in (W, M, V)), 512, "adam_ada")
    ag, ad, anm, anv = (t.reshape(ada_w.shape) for t in r_ada)

    big = {}
    for n in BIG:
        C = W[n].shape[-1]
        parts = plan.recv[(n, None)] if n == "a_conv_w" else jnp.stack([plan.recv[(n, 0)], plan.recv[(n, 1)]], axis=1).reshape(NDEV, -1, C)
        res = reduce_adam(parts, *(T[n].reshape(parts.shape[1:]) for T in (W, M, V)), min(256, parts.shape[1]), f"adam_{n}")
        big[n] = [t.reshape(W[n].shape) for t in res]

    outs = {}
    for idx, (k, sm_l, ada_t) in enumerate((("grad", sg, ag), ("delta", sd, ad), ("new_m", snm, anm), ("new_v", snv, anv))):
        d = dict(zip(SMALL, sm_l))
        d.update({n: big[n][idx] for n in BIG})
        d["ada_w"] = ada_t
        outs[k] = d
    order = ("norm_w", "ada_w", "ada_b", "a_w_in", "a_conv_w", "a_A_log", "a_dt_bias", "a_norm_w", "a_w_out", "b_w_in", "b_f_bias",
             "b_qn_w", "b_kn_w", "b_w_out", "final_norm_w")
    return (loss, dx[None], *[outs["grad"][n] for n in order], *[outs["delta"][n] for n in order],
            *[outs["new_m"][n] for n in order], *[outs["new_v"][n] for n in order])
```

```python
import functools

import jax
import jax.numpy as jnp
from jax import lax
from jax.experimental import pallas as pl
from jax.experimental.pallas import tpu as pltpu

f32 = jnp.float32
bf16 = jnp.bfloat16
SDS = jax.ShapeDtypeStruct

EPS = 1e-6
CHUNK = 64
HD = 128
GDN_QK_HEADS = 8
GDN_V_HEADS = 16
GDN_QK_W = GDN_QK_HEADS * HD
GDN_V_W = GDN_V_HEADS * HD
GDN_CONV = 2 * GDN_QK_W + GDN_V_W
GDN_IN = GDN_CONV + GDN_V_W + 2 * GDN_V_HEADS
GDN_IN_PAD = GDN_CONV + GDN_V_W + 256
GDN_TN = 640
FOX_H = 16
FOX_D = 64
FOX_W = FOX_H * FOX_D
FOX_IN = 4 * FOX_W + FOX_H
FOX_IN_PAD = 4 * FOX_W + 128
FOX_TN = 1408
FOX_PW = FOX_H * 128
NDEV = 8
MESH_AXES = ("x", "y", "c")
NEG = -1e30

ADAM_LR = 0.001
ADAM_B1 = 0.9
ADAM_B2 = 0.999
ADAM_EPS = 1e-08
ADAM_WD = 0.01
ADAM_STEP = 10

VMEM_LIMIT = 56 * 1024 * 1024


def _cp(sem=None):
    return pltpu.CompilerParams(dimension_semantics=sem, vmem_limit_bytes=VMEM_LIMIT)


def _bdot(a, b, dims):
    return lax.dot_general(a.astype(bf16), b.astype(bf16), (dims, ((), ())), preferred_element_type=f32)


def _nn(a, b):
    return _bdot(a, b, ((1,), (0,)))


def _nt(a, b):
    return _bdot(a, b, ((1,), (1,)))


def _tn(a, b):
    return _bdot(a, b, ((0,), (0,)))


def _hdot(a, b, dims=((1,), (0,))):
    return lax.dot_general(a, b, (dims, ((), ())), precision=lax.Precision.HIGHEST, preferred_element_type=f32)


def _split2(a):
    hi = a.astype(bf16)
    return hi, (a - hi.astype(f32)).astype(bf16)


def _dot3(a, b):
    (ah, al), (bh, bl) = a, b
    return (jnp.dot(ah, bh, preferred_element_type=f32) + jnp.dot(ah, bl, preferred_element_type=f32)
            + jnp.dot(al, bh, preferred_element_type=f32))


@jax.custom_vjp
def _mm(a, b):
    return _nn(a, b)


_mm.defvjp(lambda a, b: (_nn(a, b), (a, b)), lambda r, g: (_nt(g, r[1]), _tn(r[0], g)))


@jax.custom_vjp
def _mm_nt(a, b):
    return _nt(a, b)


_mm_nt.defvjp(lambda a, b: (_nt(a, b), (a, b)), lambda r, g: (_nn(g, r[1]), _tn(g, r[0])))


@jax.custom_vjp
def _mm_tn(a, b):
    return _tn(a, b)


_mm_tn.defvjp(lambda a, b: (_tn(a, b), (a, b)), lambda r, g: (_nt(r[1], g), _nn(r[0], g)))


def _silu(x):
    return x * jax.nn.sigmoid(x)


def _rms_mod(x, nw, scale, shift):
    r = lax.rsqrt(jnp.mean(x * x, axis=-1, keepdims=True) + EPS)
    return (x * r * nw) * (1.0 + scale) + shift


def _rows(S, want):
    return min(want, S)


def _my_pos():
    return lax.axis_index("x"), lax.axis_index("y"), lax.axis_index("c")


def _exchange_copies(kind, x_refs, out_refs, send_sems, recv_sems, local_sems):
    x_, y_, c_ = _my_pos()
    me = 4 * x_ + 2 * y_ + c_
    own = kind == "gather"
    cps = [pltpu.make_async_copy(x_refs[a] if own else x_refs[a].at[me], out_refs[a].at[me], local_sems.at[a])
           for a in range(len(x_refs))]
    for rel in range(1, NDEV):
        px = (x_ + ((rel >> 2) & 1)) % 2
        py = (y_ + ((rel >> 1) & 1)) % 2
        pc = (c_ + (rel & 1)) % 2
        for a in range(len(x_refs)):
            cps.append(pltpu.make_async_remote_copy(
                src_ref=x_refs[a] if own else x_refs[a].at[4 * px + 2 * py + pc], dst_ref=out_refs[a].at[me],
                send_sem=send_sems.at[rel - 1, a], recv_sem=recv_sems.at[rel - 1, a],
                device_id=(px, py, pc), device_id_type=pl.DeviceIdType.MESH))
    return cps


def _exchange_scratch(n):
    return [pltpu.SemaphoreType.DMA((NDEV - 1, n)), pltpu.SemaphoreType.DMA((NDEV - 1, n)), pltpu.SemaphoreType.DMA((n,))]


def _call(body, *, name, grid, in_specs, out_specs, out_shape, args, scratch=(), side=None):
    params = _cp(("arbitrary",) * len(grid))
    if side is None:
        return pl.pallas_call(body, name=name, grid=grid, in_specs=in_specs, out_specs=out_specs, out_shape=out_shape,
                              scratch_shapes=list(scratch), compiler_params=params)(*args)
    kind, xs = side
    n_in, n_out, n_scr, ns = len(in_specs), len(out_shape), len(scratch), len(xs)
    steps = 1
    for g in grid:
        steps *= g

    def wrapped(*refs):
        o0 = n_in + ns
        s0 = o0 + n_out + ns
        step = pl.program_id(0)
        for d in range(1, len(grid)):
            step = step * grid[d] + pl.program_id(d)

        def copies():
            return _exchange_copies(kind, refs[n_in:o0], refs[o0 + n_out:s0], *refs[s0 + n_scr:])

        @pl.when(step == 0)
        def _():
            for cp in copies():
                cp.start()

        body(*refs[:n_in], *refs[o0:o0 + n_out], *refs[s0:s0 + n_scr])

        @pl.when(step == steps - 1)
        def _():
            for cp in copies():
                cp.wait()

    any_ = pl.BlockSpec(memory_space=pl.ANY)
    side_shapes = [SDS((NDEV,) + x.shape if kind == "gather" else x.shape, x.dtype) for x in xs]
    outs = pl.pallas_call(wrapped, name=name, grid=grid, in_specs=list(in_specs) + [any_] * ns,
                          out_specs=list(out_specs) + [any_] * ns, out_shape=list(out_shape) + side_shapes,
                          scratch_shapes=list(scratch) + _exchange_scratch(ns), compiler_params=params)(*args, *xs)
    return outs[:n_out], outs[n_out:]


def inproj_fwd(x, nw, scale, shift, W, tn, name, side=None):
    S, D = x.shape
    N = W.shape[1]
    tm = _rows(S, 512)

    def body(x_ref, nw_ref, sc_ref, sh_ref, w_ref, proj_ref, h_ref):
        @pl.when(pl.program_id(1) == 0)
        def _():
            h_ref[...] = _rms_mod(x_ref[...], nw_ref[...], sc_ref[...], sh_ref[...]).astype(bf16)

        proj_ref[...] = jnp.dot(h_ref[...], w_ref[...], preferred_element_type=f32)

    vec = pl.BlockSpec((1, D), lambda i, j: (0, 0))
    return _call(
        body, name=name, grid=(S // tm, N // tn),
        in_specs=[pl.BlockSpec((tm, D), lambda i, j: (i, 0)), vec, vec, vec, pl.BlockSpec((D, tn), lambda i, j: (0, j))],
        out_specs=[pl.BlockSpec((tm, tn), lambda i, j: (i, j)), pl.BlockSpec((tm, D), lambda i, j: (i, 0))],
        out_shape=[SDS((S, N), f32), SDS((S, D), bf16)], args=(x, nw, scale, shift, W), side=side)


def inproj_bwd_x(x, nw, scale, shift, W, dproj, dx_res, tn, name, side=None):
    S, D = x.shape
    N = W.shape[1]
    tm = _rows(S, 512)
    nj = N // tn

    def body(x_ref, nw_ref, sc_ref, sh_ref, w_ref, dp_ref, dxr_ref, dx_ref, dnw_ref, dsc_ref, dsh_ref, acc):
        i, j = pl.program_id(0), pl.program_id(1)

        @pl.when(j == 0)
        def _():
            acc[...] = jnp.zeros_like(acc)

        @pl.when((i == 0) & (j == 0))
        def _():
            dnw_ref[...] = jnp.zeros_like(dnw_ref)
            dsc_ref[...] = jnp.zeros_like(dsc_ref)
            dsh_ref[...] = jnp.zeros_like(dsh_ref)

        acc[...] += _nt(dp_ref[...], w_ref[...])

        @pl.when(j == nj - 1)
        def _():
            _, vjp = jax.vjp(_rms_mod, x_ref[...], nw_ref[...], sc_ref[...], sh_ref[...])
            dx, dnw, dsc, dsh = vjp(acc[...])
            dx_ref[...] = dxr_ref[...] + dx
            dnw_ref[...] += dnw
            dsc_ref[...] += dsc
            dsh_ref[...] += dsh

    vec = pl.BlockSpec((1, D), lambda i, j: (0, 0))
    row = pl.BlockSpec((tm, D), lambda i, j: (i, 0))
    return _call(
        body, name=name, grid=(S // tm, nj),
        in_specs=[row, vec, vec, vec, pl.BlockSpec((D, tn), lambda i, j: (0, j)), pl.BlockSpec((tm, tn), lambda i, j: (i, j)), row],
        out_specs=[row, vec, vec, vec],
        out_shape=[SDS((S, D), f32), SDS((1, D), f32), SDS((1, D), f32), SDS((1, D), f32)],
        scratch=[pltpu.VMEM((tm, D), f32)], args=(x, nw, scale, shift, W, dproj, dx_res), side=side)


def matmul_tn(a, b, tn, name, side=None):
    S, K = a.shape
    N = b.shape[1]
    tm = _rows(S, 512)
    ni = S // tm

    def body(a_ref, b_ref, o_ref):
        @pl.when(pl.program_id(1) == 0)
        def _():
            o_ref[...] = jnp.zeros_like(o_ref)

        o_ref[...] += _tn(a_ref[...], b_ref[...])

    return _call(
        body, name=name, grid=(N // tn, ni),
        in_specs=[pl.BlockSpec((tm, K), lambda j, i: (i, 0)), pl.BlockSpec((tm, tn), lambda j, i: (i, j))],
        out_specs=[pl.BlockSpec((K, tn), lambda j, i: (0, j))],
        out_shape=[SDS((K, N), f32)], args=(a, b), side=side)


def _conv_taps(xs, w, n_out):
    taps = []
    for j in range(4):
        s = 3 - j
        sh = xs if s == 0 else pltpu.roll(xs, s, axis=0)
        taps.append(sh[8:8 + n_out])
    conv = taps[0] * w[0] + taps[1] * w[1] + taps[2] * w[2] + taps[3] * w[3]
    return taps, conv


def _act_norm(conv, mul_norm, mul_plain):
    s = _silu(conv)
    r = lax.rsqrt(jnp.sum(s * s, axis=-1, keepdims=True) + EPS)
    return s * (mul_norm * r + mul_plain)


def _gdn_prep_mults(j):
    is_q = j < GDN_QK_HEADS
    is_k = (j >= GDN_QK_HEADS) & (j < 2 * GDN_QK_HEADS)
    mul_norm = jnp.where(is_q, HD ** -0.5, jnp.where(is_k, 1.0, 0.0)).astype(f32)
    mul_plain = jnp.where(is_q | is_k, 0.0, 1.0).astype(f32)
    return mul_norm, mul_plain


def gdn_prep_fwd(proj, conv_w, name):
    S = proj.shape[0]
    R = _rows(S, 512)

    def body(x_ref, w_ref, o_ref):
        mul_norm, mul_plain = _gdn_prep_mults(pl.program_id(0))
        w = [w_ref[j:j + 1, :] for j in range(4)]

        def piece(r, c):
            t0 = pl.multiple_of(r * R, R)
            cur = x_ref[pl.ds(t0, R), :]
            prev = x_ref[pl.ds(pl.multiple_of(jnp.maximum(t0 - 8, 0), 8), 8), :]
            prev = jnp.where(r == 0, 0.0, prev)
            _, conv = _conv_taps(jnp.concatenate([prev, cur], axis=0), w, R)
            o_ref[pl.ds(t0, R), :] = _act_norm(conv, mul_norm, mul_plain)
            return c

        lax.fori_loop(0, S // R, piece, 0)

    return pl.pallas_call(
        body, name=name, grid=(GDN_CONV // 128,),
        in_specs=[pl.BlockSpec((S, 128), lambda j: (0, j)), pl.BlockSpec((4, 128), lambda j: (0, j))],
        out_specs=pl.BlockSpec((S, 128), lambda j: (0, j)),
        out_shape=SDS((S, GDN_CONV), f32),
        compiler_params=_cp(("arbitrary",)),
    )(proj, conv_w)


def gdn_prep_bwd(proj, conv_w, dqkvc, name):
    S = proj.shape[0]
    R = _rows(S, 512)
    NP = S // R

    def body(x_ref, w_ref, dn_ref, dx_ref, dw_ref):
        mul_norm, mul_plain = _gdn_prep_mults(pl.program_id(0))
        w = [w_ref[j:j + 1, :] for j in range(4)]

        def piece(r, dw):
            t0 = pl.multiple_of(r * R, R)
            cur = x_ref[pl.ds(t0, R), :]
            prev = x_ref[pl.ds(pl.multiple_of(jnp.maximum(t0 - 8, 0), 8), 8), :]
            prev = jnp.where(r == 0, 0.0, prev)
            nxt0 = pl.multiple_of(jnp.minimum(t0 + R, S - 8), 8)
            nxt = x_ref[pl.ds(nxt0, 8), :]
            dn_cur = dn_ref[pl.ds(t0, R), :]
            dn_nxt = jnp.where(r == NP - 1, 0.0, dn_ref[pl.ds(nxt0, 8), :])
            xs = jnp.concatenate([prev, cur, nxt], axis=0)
            taps, conv = _conv_taps(xs, w, R + 8)
            dn = jnp.concatenate([dn_cur, dn_nxt], axis=0)
            _, vjp = jax.vjp(lambda c: _act_norm(c, mul_norm, mul_plain), conv)
            dxc = vjp(dn)[0]
            n = R + 8
            dx = dxc[0:R] * w[3]
            for j in range(3):
                s = 3 - j
                dx = dx + pltpu.roll(dxc, n - s, axis=0)[0:R] * w[j]
            dx_ref[pl.ds(t0, R), :] = dx
            return tuple(dw[j] + jnp.sum(dxc[0:R] * taps[j][0:R], axis=0, keepdims=True) for j in range(4))

        dw = lax.fori_loop(0, NP, piece, tuple(jnp.zeros((1, 128), f32) for _ in range(4)))
        for j in range(4):
            dw_ref[j:j + 1, :] = dw[j]

    col = pl.BlockSpec((S, 128), lambda j: (0, j))
    wsp = pl.BlockSpec((4, 128), lambda j: (0, j))
    return pl.pallas_call(
        body, name=name, grid=(GDN_CONV // 128,),
        in_specs=[col, wsp, col], out_specs=[col, wsp],
        out_shape=[SDS((S, GDN_CONV), f32), SDS((4, GDN_CONV), f32)],
        compiler_params=_cp(("arbitrary",)),
    )(proj, conv_w, dqkvc)


def _chunk_tril(R):
    ii = lax.broadcasted_iota(jnp.int32, (R, R), 0)
    jj = lax.broadcasted_iota(jnp.int32, (R, R), 1)
    return ((ii // CHUNK == jj // CHUNK) & (ii >= jj)).astype(f32)


def _gdn_gates(b, a, A_log, dt_bias, tril):
    beta = jax.nn.sigmoid(b)
    g = -jnp.exp(A_log) * jax.nn.softplus(a + dt_bias)
    return _hdot(tril, g), beta


_GDN_B_BLK = (GDN_CONV + GDN_V_W) // 128
_GDN_A_BLK = _GDN_B_BLK + 1


def gdn_gates_fwd(proj, A_log, dt_bias, name):
    S = proj.shape[0]
    R = _rows(S, 512)

    def body(b_ref, a_ref, al_ref, dt_ref, gc_ref, be_ref):
        gc, be = _gdn_gates(b_ref[...], a_ref[...], al_ref[...], dt_ref[...], _chunk_tril(R))
        gc_ref[...] = gc
        be_ref[...] = be

    vec = pl.BlockSpec((1, 128), lambda i: (0, 0))
    blk = pl.BlockSpec((R, 128), lambda i: (i, 0))
    return pl.pallas_call(
        body, name=name, grid=(S // R,),
        in_specs=[pl.BlockSpec((R, 128), lambda i: (i, _GDN_B_BLK)), pl.BlockSpec((R, 128), lambda i: (i, _GDN_A_BLK)), vec, vec],
        out_specs=[blk, blk], out_shape=[SDS((S, 128), f32), SDS((S, 128), f32)],
        compiler_params=_cp(("arbitrary",)),
    )(proj, proj, A_log, dt_bias)


def gdn_gates_bwd(proj, A_log, dt_bias, dgc, dbeta, name):
    S = proj.shape[0]
    R = _rows(S, 512)

    def body(b_ref, a_ref, al_ref, dt_ref, dgc_ref, dbe_ref, db_ref, da_ref, dal_ref, ddt_ref):
        @pl.when(pl.program_id(0) == 0)
        def _():
            dal_ref[...] = jnp.zeros_like(dal_ref)
            ddt_ref[...] = jnp.zeros_like(ddt_ref)

        tril = _chunk_tril(R)
        _, vjp = jax.vjp(lambda b, a, al, dt: _gdn_gates(b, a, al, dt, tril), b_ref[...], a_ref[...], al_ref[...], dt_ref[...])
        db, da, dal, ddt = vjp((dgc_ref[...], dbe_ref[...]))
        db_ref[...] = db
        da_ref[...] = da
        dal_ref[...] += dal
        ddt_ref[...] += ddt

    vec = pl.BlockSpec((1, 128), lambda i: (0, 0))
    blk = pl.BlockSpec((R, 128), lambda i: (i, 0))
    return pl.pallas_call(
        body, name=name, grid=(S // R,),
        in_specs=[pl.BlockSpec((R, 128), lambda i: (i, _GDN_B_BLK)), pl.BlockSpec((R, 128), lambda i: (i, _GDN_A_BLK)), vec, vec, blk, blk],
        out_specs=[blk, blk, vec, vec],
        out_shape=[SDS((S, 128), f32), SDS((S, 128), f32), SDS((1, 128), f32), SDS((1, 128), f32)],
        compiler_params=_cp(("arbitrary",)),
    )(proj, proj, A_log, dt_bias, dgc, dbeta)


@jax.custom_vjp
def _inv_given(L, T):
    return T


def _inv_given_bwd(T, ct):
    dL = -_nt(_tn(T, ct), T)
    return dL, jnp.zeros_like(T)


_inv_given.defvjp(lambda L, T: (T, T), _inv_given_bwd)


REP = GDN_V_HEADS // GDN_QK_HEADS


def _gdn_intra_all(qs, ks, vs, gcols, bcols, Ts=None):
    H = len(vs)
    C = vs[0].shape[0]
    ii = lax.broadcasted_iota(jnp.int32, (C, C), 0)
    jj = lax.broadcasted_iota(jnp.int32, (C, C), 1)
    grows = [jnp.sum(jnp.where(ii == jj, g, 0.0), axis=0, keepdims=True) for g in gcols]
    decs = [jnp.exp(jnp.where(ii >= jj, gcols[h] - grows[h], NEG)) for h in range(H)]
    kbs = [ks[h // REP] * bcols[h] for h in range(H)]
    As = [_mm_nt(kbs[h], ks[h // REP]) for h in range(H)]
    Ls = [jnp.where(ii > jj, As[h] * decs[h], 0.0) for h in range(H)]
    if Ts is None:
        T = _neumann_inv_batched(Ls)
    else:
        T = [_inv_given(Ls[h], Ts[h]) for h in range(H)]
    us = [_mm(T[h], vs[h] * bcols[h]) for h in range(H)]
    ws = [_mm(T[h], kbs[h] * jnp.exp(gcols[h])) for h in range(H)]
    qk = [_mm_nt(qs[p], ks[p]) for p in range(H // REP)]
    return us, ws, [qk[h // REP] * decs[h] for h in range(H)], T


def _neumann_inv_batched(Ls):
    n, C = 4, Ls[0].shape[0]
    r0 = lax.broadcasted_iota(jnp.int32, (n * C, n * C), 0)
    c0 = lax.broadcasted_iota(jnp.int32, (n * C, n * C), 1)
    same = (r0 // C) == (c0 // C)

    def blockdiag(split):
        return tuple(jnp.where(same, jnp.concatenate([x] * n, axis=0), jnp.zeros((), bf16)) for x in split)

    Ms = [jnp.concatenate(Ls[b:b + n], axis=1) for b in range(0, len(Ls), n)]
    eye = (lax.broadcasted_iota(jnp.int32, (C, n * C), 0) == (lax.broadcasted_iota(jnp.int32, (C, n * C), 1) & (C - 1))).astype(f32)
    Ps = [eye - M for M in Ms]
    Ss = [_split2(M) for M in Ms]
    Bs = [blockdiag(S) for S in Ss]
    k = 1
    while 2 * k < C:
        Ss = [_split2(_dot3(S, B)) for S, B in zip(Ss, Bs)]
        Bs = [blockdiag(S) for S in Ss]
        Ps = [P + _dot3(_split2(P), B) for P, B in zip(Ps, Bs)]
        k *= 2
    return [P[:, h * C:(h + 1) * C] for P in Ps for h in range(n)]


def _gdn_scan_all(qs, ks, gcols, us, ws, attns, S0s):
    H = len(us)
    C = us[0].shape[0]
    last = lax.broadcasted_iota(jnp.int32, (C, 1), 0) == C - 1
    glast = [jnp.sum(jnp.where(last, g, 0.0), axis=0, keepdims=True) for g in gcols]
    wS = [_mm(ws[h], S0s[h]) for h in range(H)]
    qS = [_mm(qs[h // REP] * jnp.exp(gcols[h]), S0s[h]) for h in range(H)]
    vn = [us[h] - wS[h] for h in range(H)]
    av = [_mm(attns[h], vn[h]) for h in range(H)]
    kv = [_mm_tn(ks[h // REP] * jnp.exp(glast[h] - gcols[h]), vn[h]) for h in range(H)]
    return [qS[h] + av[h] for h in range(H)], [S0s[h] * jnp.exp(glast[h]) + kv[h] for h in range(H)]


def _head_cols(blk):
    lane = lax.broadcasted_iota(jnp.int32, blk.shape, 1)
    return [jnp.sum(jnp.where(lane == h, blk, 0.0), axis=1, keepdims=True) for h in range(GDN_V_HEADS)]


def _head_lanes(cols):
    lane = lax.broadcasted_iota(jnp.int32, (cols[0].shape[0], 128), 1)
    out = jnp.zeros((cols[0].shape[0], 128), f32)
    for h, c in enumerate(cols):
        out = out + jnp.where(lane == h, c, 0.0)
    return out


def _heads(ref, n):
    return [ref[:, h * HD:(h + 1) * HD] for h in range(n)]


def _gdn_specs(NC, rv=None):
    ix = (lambda n: n) if rv is None else rv
    qs = pl.BlockSpec((CHUNK, GDN_QK_W), lambda n: (ix(n), 0))
    ks = pl.BlockSpec((CHUNK, GDN_QK_W), lambda n: (ix(n), 1))
    vs = pl.BlockSpec((CHUNK, GDN_V_W), lambda n: (ix(n), 1))
    g1 = pl.BlockSpec((CHUNK, 128), lambda n: (ix(n), 0))
    wide = pl.BlockSpec((CHUNK, GDN_V_W), lambda n: (ix(n), 0))
    sq = pl.BlockSpec((1, GDN_V_HEADS, CHUNK, CHUNK), lambda n: (ix(n), 0, 0, 0))
    st = pl.BlockSpec((1, GDN_V_HEADS, HD, HD), lambda n: (ix(n), 0, 0, 0))
    return qs, ks, vs, g1, wide, sq, st


def gdn_intra_fwd(qkvc, gc, beta, name, side=None):
    S = qkvc.shape[0]
    NC = S // CHUNK

    def body(q_ref, k_ref, v_ref, gc_ref, be_ref, u_ref, w_ref, at_ref, T_ref):
        us, ws, attns, Ts = _gdn_intra_all(_heads(q_ref, GDN_QK_HEADS), _heads(k_ref, GDN_QK_HEADS), _heads(v_ref, GDN_V_HEADS),
                                           _head_cols(gc_ref[...]), _head_cols(be_ref[...]))
        for h in range(GDN_V_HEADS):
            u_ref[:, h * HD:(h + 1) * HD] = us[h]
            w_ref[:, h * HD:(h + 1) * HD] = ws[h]
            at_ref[0, h] = attns[h]
            T_ref[0, h] = Ts[h]

    qs, ks, vs, g1, wide, sq, _ = _gdn_specs(NC)
    return _call(
        body, name=name, grid=(NC,),
        in_specs=[qs, ks, vs, g1, g1], out_specs=[wide, wide, sq, sq],
        out_shape=[SDS((S, GDN_V_W), f32), SDS((S, GDN_V_W), f32),
                   SDS((NC, GDN_V_HEADS, CHUNK, CHUNK), f32), SDS((NC, GDN_V_HEADS, CHUNK, CHUNK), f32)],
        args=(qkvc, qkvc, qkvc, gc, beta), side=side)


def gdn_scan_fwd(qkvc, gc, u, w, attn, name):
    S = qkvc.shape[0]
    NC = S // CHUNK

    def body(q_ref, k_ref, gc_ref, u_ref, w_ref, at_ref, o_ref, st_ref, state):
        @pl.when(pl.program_id(0) == 0)
        def _():
            state[...] = jnp.zeros_like(state)

        S0s = [state[h] for h in range(GDN_V_HEADS)]
        os_, S1s = _gdn_scan_all(_heads(q_ref, GDN_QK_HEADS), _heads(k_ref, GDN_QK_HEADS), _head_cols(gc_ref[...]),
                                 _heads(u_ref, GDN_V_HEADS), _heads(w_ref, GDN_V_HEADS),
                                 [at_ref[0, h] for h in range(GDN_V_HEADS)], S0s)
        for h in range(GDN_V_HEADS):
            o_ref[:, h * HD:(h + 1) * HD] = os_[h]
            st_ref[0, h] = S0s[h]
            state[h] = S1s[h]

    qs, ks, _, g1, wide, sq, st = _gdn_specs(NC)
    return pl.pallas_call(
        body, name=name, grid=(NC,),
        in_specs=[qs, ks, g1, wide, wide, sq], out_specs=[wide, st],
        out_shape=[SDS((S, GDN_V_W), f32), SDS((NC, GDN_V_HEADS, HD, HD), f32)],
        scratch_shapes=[pltpu.VMEM((GDN_V_HEADS, HD, HD), f32)],
        compiler_params=_cp(("arbitrary",)),
    )(qkvc, qkvc, gc, u, w, attn)


def gdn_scan_bwd(qkvc, gc, u, w, attn, states, do, name, side=None):
    S = qkvc.shape[0]
    NC = S // CHUNK

    def body(q_ref, k_ref, gc_ref, u_ref, w_ref, at_ref, st_ref, do_ref,
             dq_ref, dk_ref, dgc_ref, du_ref, dw_ref, dat_ref, dstate):
        @pl.when(pl.program_id(0) == 0)
        def _():
            dstate[...] = jnp.zeros_like(dstate)

        VH = range(GDN_V_HEADS)
        _, vjp = jax.vjp(_gdn_scan_all, _heads(q_ref, GDN_QK_HEADS), _heads(k_ref, GDN_QK_HEADS), _head_cols(gc_ref[...]),
                         _heads(u_ref, GDN_V_HEADS), _heads(w_ref, GDN_V_HEADS), [at_ref[0, h] for h in VH],
                         [st_ref[0, h] for h in VH])
        dqs, dks, dgs, dus, dws, dats, dS0s = vjp((_heads(do_ref, GDN_V_HEADS), [dstate[h] for h in VH]))
        for p in range(GDN_QK_HEADS):
            dq_ref[:, p * HD:(p + 1) * HD] = dqs[p]
            dk_ref[:, p * HD:(p + 1) * HD] = dks[p]
        for h in VH:
            du_ref[:, h * HD:(h + 1) * HD] = dus[h]
            dw_ref[:, h * HD:(h + 1) * HD] = dws[h]
            dat_ref[0, h] = dats[h]
            dstate[h] = dS0s[h]
        dgc_ref[...] = _head_lanes(dgs)

    qs, ks, _, g1, wide, sq, st = _gdn_specs(NC, lambda n: NC - 1 - n)
    dqs = pl.BlockSpec((CHUNK, GDN_QK_W), lambda n: (NC - 1 - n, 0))
    return _call(
        body, name=name, grid=(NC,),
        in_specs=[qs, ks, g1, wide, wide, sq, st, wide],
        out_specs=[dqs, dqs, g1, wide, wide, sq],
        out_shape=[SDS((S, GDN_QK_W), f32), SDS((S, GDN_QK_W), f32), SDS((S, 128), f32), SDS((S, GDN_V_W), f32),
                   SDS((S, GDN_V_W), f32), SDS((NC, GDN_V_HEADS, CHUNK, CHUNK), f32)],
        scratch=[pltpu.VMEM((GDN_V_HEADS, HD, HD), f32)], args=(qkvc, qkvc, gc, u, w, attn, states, do), side=side)


def gdn_intra_bwd(qkvc, gc, beta, Ts, du, dw, dattn, dq_s, dk_s, dgc_s, name):
    S = qkvc.shape[0]
    NC = S // CHUNK

    def body(q_ref, k_ref, v_ref, gc_ref, be_ref, T_ref, du_ref, dw_ref, dat_ref, dqs_ref, dks_ref, dgs_ref,
             dq_ref, dk_ref, dv_ref, dgc_ref, dbe_ref):
        VH = range(GDN_V_HEADS)
        Ts = [T_ref[0, h] for h in VH]
        _, vjp = jax.vjp(lambda q_, k_, v_, g_, b_: _gdn_intra_all(q_, k_, v_, g_, b_, Ts)[:3],
                         _heads(q_ref, GDN_QK_HEADS), _heads(k_ref, GDN_QK_HEADS), _heads(v_ref, GDN_V_HEADS),
                         _head_cols(gc_ref[...]), _head_cols(be_ref[...]))
        dqs, dks, dvs, dgs, dbs = vjp((_heads(du_ref, GDN_V_HEADS), _heads(dw_ref, GDN_V_HEADS), [dat_ref[0, h] for h in VH]))
        for p in range(GDN_QK_HEADS):
            dq_ref[:, p * HD:(p + 1) * HD] = dqs[p] + dqs_ref[:, p * HD:(p + 1) * HD]
            dk_ref[:, p * HD:(p + 1) * HD] = dks[p] + dks_ref[:, p * HD:(p + 1) * HD]
        for h in VH:
            dv_ref[:, h * HD:(h + 1) * HD] = dvs[h]
        dgc_ref[...] = _head_lanes(dgs) + dgs_ref[...]
        dbe_ref[...] = _head_lanes(dbs)

    qs, ks, vs, g1, wide, sq, _ = _gdn_specs(NC)
    dqs = pl.BlockSpec((CHUNK, GDN_QK_W), lambda n: (n, 0))
    dq, dk, dv, dgc, dbe = pl.pallas_call(
        body, name=name, grid=(NC,),
        in_specs=[qs, ks, vs, g1, g1, sq, wide, wide, sq, dqs, dqs, g1],
        out_specs=[dqs, dqs, wide, g1, g1],
        out_shape=[SDS((S, GDN_QK_W), f32), SDS((S, GDN_QK_W), f32), SDS((S, GDN_V_W), f32), SDS((S, 128), f32), SDS((S, 128), f32)],
        compiler_params=_cp(("arbitrary",)),
    )(qkvc, qkvc, qkvc, gc, beta, Ts, du, dw, dattn, dq_s, dk_s, dgc_s)
    return jnp.concatenate([dq, dk, dv], axis=1), dgc, dbe


def _gated_norm(o, z, nw):
    parts = []
    for h in range(GDN_V_HEADS):
        oh = o[:, h * HD:(h + 1) * HD]
        r = lax.rsqrt(jnp.mean(oh * oh, axis=-1, keepdims=True) + EPS)
        parts.append((oh * r * nw) * _silu(z[:, h * HD:(h + 1) * HD]))
    return jnp.concatenate(parts, axis=1)


def gdn_out_fwd(o, proj, nw, W, x, gate, name):
    S, D = x.shape
    tm = _rows(S, 256)

    def body(o_ref, z_ref, nw_ref, w_ref, x_ref, g_ref, xn_ref, y_ref, og_ref):
        og = _gated_norm(o_ref[...], z_ref[...], nw_ref[...]).astype(bf16)
        y = jnp.dot(og, w_ref[...], preferred_element_type=f32)
        og_ref[...] = og
        y_ref[...] = y
        xn_ref[...] = x_ref[...] + g_ref[...] * y

    row = pl.BlockSpec((tm, D), lambda i: (i, 0))
    wide = pl.BlockSpec((tm, GDN_V_W), lambda i: (i, 0))
    return pl.pallas_call(
        body, name=name, grid=(S // tm,),
        in_specs=[wide, pl.BlockSpec((tm, GDN_V_W), lambda i: (i, 2)), pl.BlockSpec((1, HD), lambda i: (0, 0)),
                  pl.BlockSpec((GDN_V_W, D), lambda i: (0, 0)), row, pl.BlockSpec((1, D), lambda i: (0, 0))],
        out_specs=[row, row, wide],
        out_shape=[SDS((S, D), f32), SDS((S, D), f32), SDS((S, GDN_V_W), bf16)],
        compiler_params=_cp(("arbitrary",)),
    )(o, proj, nw, W, x, gate)


def gdn_out_bwd(dxn, y, gate, o, proj, nw, W, name):
    S, D = dxn.shape
    tm = _rows(S, 256)

    def body(dx_ref, y_ref, g_ref, o_ref, z_ref, nw_ref, w_ref, dy_ref, dg_ref, do_ref, dz_ref, dnw_ref):
        @pl.when(pl.program_id(0) == 0)
        def _():
            dg_ref[...] = jnp.zeros_like(dg_ref)
            dnw_ref[...] = jnp.zeros_like(dnw_ref)

        dx = dx_ref[...]
        dy = dx * g_ref[...]
        dy_ref[...] = dy
        dg_ref[...] += jnp.sum(dx * y_ref[...], axis=0, keepdims=True)
        dog = _nt(dy, w_ref[...])
        _, vjp = jax.vjp(_gated_norm, o_ref[...], z_ref[...], nw_ref[...])
        do, dz, dnw = vjp(dog)
        do_ref[...] = do
        dz_ref[...] = dz
        dnw_ref[...] += dnw

    row = pl.BlockSpec((tm, D), lambda i: (i, 0))
    wide = pl.BlockSpec((tm, GDN_V_W), lambda i: (i, 0))
    vecd = pl.BlockSpec((1, D), lambda i: (0, 0))
    vech = pl.BlockSpec((1, HD), lambda i: (0, 0))
    return pl.pallas_call(
        body, name=name, grid=(S // tm,),
        in_specs=[row, row, vecd, wide, pl.BlockSpec((tm, GDN_V_W), lambda i: (i, 2)), vech, pl.BlockSpec((GDN_V_W, D), lambda i: (0, 0))],
        out_specs=[row, vecd, wide, wide, vech],
        out_shape=[SDS((S, D), f32), SDS((1, D), f32), SDS((S, GDN_V_W), f32), SDS((S, GDN_V_W), f32), SDS((1, HD), f32)],
        compiler_params=_cp(("arbitrary",)),
    )(dxn, y, gate, o, proj, nw, W)


def _rms_w(x, w):
    return (x * lax.rsqrt(jnp.mean(x * x, axis=-1, keepdims=True) + EPS)) * w


def _split3(c):
    hi = c.astype(bf16).astype(f32)
    r1 = c - hi
    mid = r1.astype(bf16).astype(f32)
    lo = (r1 - mid).astype(bf16).astype(f32)
    return hi, mid, lo


_FOX_F_BLK = 4 * FOX_W // 128


def fox_prep_fwd(proj, f_bias, qn_w, kn_w, name):
    S = proj.shape[0]
    tm = _rows(S, 256)

    def body(q_ref, k_ref, v_ref, f_ref, fb_ref, qw_ref, kw_ref, Q_ref, K_ref, V_ref, carry):
        @pl.when(pl.program_id(0) == 0)
        def _():
            carry[...] = jnp.zeros_like(carry)

        ii = lax.broadcasted_iota(jnp.int32, (tm, tm), 0)
        jj = lax.broadcasted_iota(jnp.int32, (tm, tm), 1)
        lf = jax.nn.log_sigmoid(f_ref[...] + fb_ref[...])
        cum = _hdot((ii >= jj).astype(f32), lf) + carry[...]
        carry[...] = cum[tm - 1:tm, :]
        lane = lax.broadcasted_iota(jnp.int32, (tm, FOX_D), 1)
        q, k, v = q_ref[...], k_ref[...], v_ref[...]
        for h in range(FOX_H):
            sl = slice(h * FOX_D, (h + 1) * FOX_D)
            hi, mid, lo = _split3(cum[:, h:h + 1])
            qn = _rms_w(q[:, sl], qw_ref[...]) * FOX_D ** -0.5
            kn = _rms_w(k[:, sl], kw_ref[...])
            eq = jnp.where(lane == 0, hi, jnp.where(lane == 1, mid, jnp.where(lane == 2, lo, jnp.where(lane < 6, 1.0, 0.0))))
            ek = jnp.where(lane < 3, 1.0, jnp.where(lane == 3, -hi, jnp.where(lane == 4, -mid, jnp.where(lane == 5, -lo, 0.0))))
            ev = jnp.where(lane == 0, 1.0, 0.0)
            Q_ref[:, h * 128:(h + 1) * 128] = jnp.concatenate([qn, eq], axis=1).astype(bf16)
            K_ref[:, h * 128:(h + 1) * 128] = jnp.concatenate([kn, ek], axis=1).astype(bf16)
            V_ref[:, h * 128:(h + 1) * 128] = jnp.concatenate([v[:, sl], ev], axis=1).astype(bf16)

    def colblk(c):
        return pl.BlockSpec((tm, FOX_W), lambda i: (i, c))

    pad = pl.BlockSpec((tm, FOX_PW), lambda i: (i, 0))
    return pl.pallas_call(
        body, name=name, grid=(S // tm,),
        in_specs=[colblk(0), colblk(1), colblk(2), pl.BlockSpec((tm, 128), lambda i: (i, _FOX_F_BLK)),
                  pl.BlockSpec((1, 128), lambda i: (0, 0)), pl.BlockSpec((1, FOX_D), lambda i: (0, 0)), pl.BlockSpec((1, FOX_D), lambda i: (0, 0))],
        out_specs=[pad, pad, pad],
        out_shape=[SDS((S, FOX_PW), bf16)] * 3,
        scratch_shapes=[pltpu.VMEM((1, 128), f32)],
        compiler_params=_cp(("arbitrary",)),
    )(proj, proj, proj, proj, f_bias, qn_w, kn_w)


def fox_prep_bwd(proj, f_bias, qn_w, kn_w, dQ, dK, dV, name):
    S = proj.shape[0]
    tm = _rows(S, 256)
    NB = S // tm

    def body(q_ref, k_ref, f_ref, fb_ref, qw_ref, kw_ref, dQ_ref, dK_ref, dV_ref,
             dq_ref, dk_ref, dv_ref, df_ref, dfb_ref, dqw_ref, dkw_ref, carry):
        @pl.when(pl.program_id(0) == 0)
        def _():
            carry[...] = jnp.zeros_like(carry)
            dfb_ref[...] = jnp.zeros_like(dfb_ref)
            dqw_ref[...] = jnp.zeros_like(dqw_ref)
            dkw_ref[...] = jnp.zeros_like(dkw_ref)

        q, k = q_ref[...], k_ref[...]
        lane128 = lax.broadcasted_iota(jnp.int32, (tm, 128), 1)
        dcum = jnp.zeros((tm, 128), f32)
        dqs, dks, dvs = [], [], []
        dqw = jnp.zeros((1, FOX_D), f32)
        dkw = jnp.zeros((1, FOX_D), f32)
        for h in range(FOX_H):
            sl = slice(h * FOX_D, (h + 1) * FOX_D)
            dQh = dQ_ref[:, h * 128:(h + 1) * 128]
            dKh = dK_ref[:, h * 128:(h + 1) * 128]
            _, vq = jax.vjp(lambda a, w: _rms_w(a, w) * FOX_D ** -0.5, q[:, sl], qw_ref[...])
            dqh, dw1 = vq(dQh[:, 0:FOX_D])
            _, vk = jax.vjp(_rms_w, k[:, sl], kw_ref[...])
            dkh, dw2 = vk(dKh[:, 0:FOX_D])
            dqs.append(dqh)
            dks.append(dkh)
            dvs.append(dV_ref[:, h * 128:h * 128 + FOX_D])
            dqw = dqw + dw1
            dkw = dkw + dw2
            dcum = dcum + jnp.where(lane128 == h, dQh[:, FOX_D:FOX_D + 1] - dKh[:, FOX_D + 3:FOX_D + 4], 0.0)
        dq_ref[...] = jnp.concatenate(dqs, axis=1)
        dk_ref[...] = jnp.concatenate(dks, axis=1)
        dv_ref[...] = jnp.concatenate(dvs, axis=1)
        ii = lax.broadcasted_iota(jnp.int32, (tm, tm), 0)
        jj = lax.broadcasted_iota(jnp.int32, (tm, tm), 1)
        dlf = _hdot((ii <= jj).astype(f32), dcum) + carry[...]
        carry[...] += jnp.sum(dcum, axis=0, keepdims=True)
        df = dlf * jax.nn.sigmoid(-(f_ref[...] + fb_ref[...]))
        df_ref[...] = df
        dfb_ref[...] += jnp.sum(df, axis=0, keepdims=True)
        dqw_ref[...] += dqw
        dkw_ref[...] += dkw

    rv = lambda i: NB - 1 - i

    def colblk(c):
        return pl.BlockSpec((tm, FOX_W), lambda i: (rv(i), c))

    pad = pl.BlockSpec((tm, FOX_PW), lambda i: (rv(i), 0))
    cmp_ = pl.BlockSpec((tm, FOX_W), lambda i: (rv(i), 0))
    fblk = pl.BlockSpec((tm, 128), lambda i: (rv(i), 0))
    v128 = pl.BlockSpec((1, 128), lambda i: (0, 0))
    v64 = pl.BlockSpec((1, FOX_D), lambda i: (0, 0))
    return pl.pallas_call(
        body, name=name, grid=(NB,),
        in_specs=[colblk(0), colblk(1), pl.BlockSpec((tm, 128), lambda i: (rv(i), _FOX_F_BLK)), v128, v64, v64, pad, pad, pad],
        out_specs=[cmp_, cmp_, cmp_, fblk, v128, v64, v64],
        out_shape=[SDS((S, FOX_W), f32)] * 3 + [SDS((S, 128), f32), SDS((1, 128), f32), SDS((1, FOX_D), f32), SDS((1, FOX_D), f32)],
        scratch_shapes=[pltpu.VMEM((1, 128), f32)],
        compiler_params=_cp(("arbitrary",)),
    )(proj, proj, proj, f_bias, qn_w, kn_w, dQ, dK, dV)


FOX_HB = 2


def _diag_mask(t):
    return lax.broadcasted_iota(jnp.int32, (t, t), 1) <= lax.broadcasted_iota(jnp.int32, (t, t), 0)


def fox_attn_fwd(Q, K, V, name):
    S = Q.shape[0]
    t = _rows(S, 512)

    HB = FOX_HB
    HS = [slice(h * 128, (h + 1) * 128) for h in range(HB)]

    def body(q_ref, k_ref, v_ref, o_ref, m_sc, acc_sc, s_sc):
        i = pl.program_id(1)
        qs = [q_ref[:, sl] for sl in HS]
        m_sc[...] = jnp.full_like(m_sc, NEG)
        acc_sc[...] = jnp.zeros_like(acc_sc)

        def scores(j):
            j0 = pl.multiple_of(j * t, t)
            return [_nt(qs[h], k_ref[pl.ds(j0, t), HS[h]]) for h in range(HB)]

        def tile(j, diag):
            j0 = pl.multiple_of(j * t, t)
            ss = [s_sc[h] for h in range(HB)]
            if diag:
                ss = [jnp.where(_diag_mask(t), s, NEG) for s in ss]
            else:
                nxt = scores(j + 1)
            ms = [m_sc[h] for h in range(HB)]
            m_new = [jnp.maximum(ms[h], jnp.max(ss[h], axis=1, keepdims=True)) for h in range(HB)]
            ps = [jnp.exp(ss[h] - m_new[h]) for h in range(HB)]
            pv = [_nn(ps[h], v_ref[pl.ds(j0, t), HS[h]]) for h in range(HB)]
            for h in range(HB):
                acc_sc[h] = acc_sc[h] * jnp.exp(ms[h] - m_new[h]) + pv[h]
                m_sc[h] = m_new[h]
                if not diag:
                    s_sc[h] = nxt[h]

        def off_diag(j, c):
            tile(j, False)
            return c

        first = scores(0)
        for h in range(HB):
            s_sc[h] = first[h]
        lax.fori_loop(0, i, off_diag, 0)
        tile(i, True)
        lane = lax.broadcasted_iota(jnp.int32, (t, 128), 1)
        for h in range(HB):
            acc = acc_sc[h]
            l = acc[:, FOX_D:FOX_D + 1]
            o_ref[:, HS[h]] = jnp.where(lane == FOX_D, m_sc[h] + jnp.log(l), acc / l)

    blk = pl.BlockSpec((t, HB * 128), lambda h, i: (i, h))
    seq = pl.BlockSpec((S, HB * 128), lambda h, i: (0, h))
    return pl.pallas_call(
        body, name=name, grid=(FOX_H // HB, S // t),
        in_specs=[blk, seq, seq], out_specs=blk, out_shape=SDS((S, FOX_PW), f32),
        scratch_shapes=[pltpu.VMEM((HB, t, 1), f32), pltpu.VMEM((HB, t, 128), f32), pltpu.VMEM((HB, t, t), f32)],
        compiler_params=_cp(("arbitrary", "arbitrary")),
    )(Q, K, V)


def fox_attn_bwd(Q, K, V, dO, O, name):
    S = Q.shape[0]
    t = _rows(S, 512)
    nq = S // t

    HB = FOX_HB
    HS = [slice(h * 128, (h + 1) * 128) for h in range(HB)]

    def body(k_ref, v_ref, q_ref, do_ref, o_ref, dq_ref, dk_ref, dv_ref):
        j = pl.program_id(1)

        @pl.when(j == 0)
        def _():
            dq_ref[...] = jnp.zeros_like(dq_ref)

        dk_ref[...] = jnp.zeros_like(dk_ref)
        dv_ref[...] = jnp.zeros_like(dv_ref)
        ks = [k_ref[:, sl] for sl in HS]
        vs = [v_ref[:, sl] for sl in HS]

        def tile(i, diag):
            i0 = pl.multiple_of(i * t, t)
            R = range(HB)
            qs = [q_ref[pl.ds(i0, t), HS[h]] for h in R]
            dos = [do_ref[pl.ds(i0, t), HS[h]] for h in R]
            ss = [_nt(qs[h], ks[h]) - o_ref[pl.ds(i0, t), h * 128 + FOX_D:h * 128 + FOX_D + 1] for h in R]
            if diag:
                ss = [jnp.where(_diag_mask(t), s, NEG) for s in ss]
            ps = [jnp.exp(s) for s in ss]
            dps = [_nt(dos[h], vs[h]) for h in R]
            dvs = [_tn(ps[h], dos[h]) for h in R]
            dss = [(ps[h] * dps[h]).astype(bf16) for h in R]
            dks = [_tn(dss[h], qs[h]) for h in R]
            dqs = [_nn(dss[h], ks[h]) for h in R]
            for h in R:
                dv_ref[:, HS[h]] += dvs[h]
                dk_ref[:, HS[h]] += dks[h]
                dq_ref[pl.ds(i0, t), HS[h]] += dqs[h]

        tile(j, True)

        def off_diag(i, c):
            tile(i, False)
            return c

        lax.fori_loop(j + 1, nq, off_diag, 0)

    blk = pl.BlockSpec((t, HB * 128), lambda h, j: (j, h))
    seq = pl.BlockSpec((S, HB * 128), lambda h, j: (0, h))
    return pl.pallas_call(
        body, name=name, grid=(FOX_H // HB, nq),
        in_specs=[blk, blk, seq, seq, seq], out_specs=[seq, blk, blk],
        out_shape=[SDS((S, FOX_PW), f32)] * 3,
        compiler_params=_cp(("arbitrary", "arbitrary")),
    )(K, V, Q, dO, O)


def fox_out_fwd(O, proj, W, x, gate, name):
    S, D = x.shape
    tm = _rows(S, 256)

    def body(o_ref, z_ref, w_ref, x_ref, g_ref, xn_ref, y_ref, og_ref):
        z = z_ref[...]
        og = jnp.concatenate([o_ref[:, h * 128:h * 128 + FOX_D] * _silu(z[:, h * FOX_D:(h + 1) * FOX_D]) for h in range(FOX_H)],
                             axis=1).astype(bf16)
        y = jnp.dot(og, w_ref[...], preferred_element_type=f32)
        og_ref[...] = og
        y_ref[...] = y
        xn_ref[...] = x_ref[...] + g_ref[...] * y

    row = pl.BlockSpec((tm, D), lambda i: (i, 0))
    cmp_ = pl.BlockSpec((tm, FOX_W), lambda i: (i, 0))
    return pl.pallas_call(
        body, name=name, grid=(S // tm,),
        in_specs=[pl.BlockSpec((tm, FOX_PW), lambda i: (i, 0)), pl.BlockSpec((tm, FOX_W), lambda i: (i, 3)),
                  pl.BlockSpec((FOX_W, D), lambda i: (0, 0)), row, pl.BlockSpec((1, D), lambda i: (0, 0))],
        out_specs=[row, row, cmp_],
        out_shape=[SDS((S, D), f32), SDS((S, D), f32), SDS((S, FOX_W), bf16)],
        compiler_params=_cp(("arbitrary",)),
    )(O, proj, W, x, gate)


def fox_out_bwd(dxn, y, gate, O, proj, W, name):
    S, D = dxn.shape
    tm = _rows(S, 256)

    def body(dx_ref, y_ref, g_ref, o_ref, z_ref, w_ref, dy_ref, dg_ref, dO_ref, dz_ref):
        @pl.when(pl.program_id(0) == 0)
        def _():
            dg_ref[...] = jnp.zeros_like(dg_ref)

        dx = dx_ref[...]
        dy = dx * g_ref[...]
        dy_ref[...] = dy
        dg_ref[...] += jnp.sum(dx * y_ref[...], axis=0, keepdims=True)
        dog = _nt(dy, w_ref[...])
        z = z_ref[...]
        lane = lax.broadcasted_iota(jnp.int32, (tm, FOX_D), 1)
        dzs = []
        for h in range(FOX_H):
            sl = slice(h * FOX_D, (h + 1) * FOX_D)
            zh = z[:, sl]
            sg = jax.nn.sigmoid(zh)
            oh = o_ref[:, h * 128:h * 128 + FOX_D]
            doh = dog[:, sl] * (zh * sg)
            delta = jnp.sum(doh * oh, axis=1, keepdims=True)
            dO_ref[:, h * 128:(h + 1) * 128] = jnp.concatenate([doh, jnp.where(lane == 0, -delta, 0.0)], axis=1).astype(bf16)
            dzs.append(dog[:, sl] * oh * (sg * (1.0 + zh * (1.0 - sg))))
        dz_ref[...] = jnp.concatenate(dzs, axis=1)

    row = pl.BlockSpec((tm, D), lambda i: (i, 0))
    vecd = pl.BlockSpec((1, D), lambda i: (0, 0))
    pad = pl.BlockSpec((tm, FOX_PW), lambda i: (i, 0))
    return pl.pallas_call(
        body, name=name, grid=(S // tm,),
        in_specs=[row, row, vecd, pad, pl.BlockSpec((tm, FOX_W), lambda i: (i, 3)), pl.BlockSpec((FOX_W, D), lambda i: (0, 0))],
        out_specs=[row, vecd, pad, pl.BlockSpec((tm, FOX_W), lambda i: (i, 0))],
        out_shape=[SDS((S, D), f32), SDS((1, D), f32), SDS((S, FOX_PW), bf16), SDS((S, FOX_W), f32)],
        compiler_params=_cp(("arbitrary",)),
    )(dxn, y, gate, O, proj, W)


def final_loss(x, fw, target, name):
    S, D = x.shape
    tm = _rows(S, 512)

    def body(x_ref, w_ref, t_ref, l_ref, dx_ref, dw_ref):
        @pl.when(pl.program_id(0) == 0)
        def _():
            l_ref[...] = jnp.zeros_like(l_ref)
            dw_ref[...] = jnp.zeros_like(dw_ref)

        out, vjp = jax.vjp(_rms_w, x_ref[...], w_ref[...])
        err = out - t_ref[...]
        l_ref[...] += 0.5 * jnp.sum(jnp.sum(err * err, axis=1, keepdims=True) * (1.0 / D), axis=0, keepdims=True)
        dx, dw = vjp(err * (1.0 / D))
        dx_ref[...] = dx
        dw_ref[...] += dw

    row = pl.BlockSpec((tm, D), lambda i: (i, 0))
    vec = pl.BlockSpec((1, D), lambda i: (0, 0))
    return pl.pallas_call(
        body, name=name, grid=(S // tm,),
        in_specs=[row, vec, row], out_specs=[pl.BlockSpec((1, 128), lambda i: (0, 0)), row, vec],
        out_shape=[SDS((1, 128), f32), SDS((S, D), f32), SDS((1, D), f32)],
        compiler_params=_cp(("arbitrary",)),
    )(x, fw, target)


def ada_fwd(c_all, ada_w, name):
    L, D, n = ada_w.shape

    def body(c_ref, w_ref, o_ref):
        cond = jnp.concatenate([_silu(c_ref[...]), jnp.zeros((8, D), f32)], axis=0)
        o_ref[0] = _nn(cond, w_ref[0])[0:8]

    return pl.pallas_call(
        body, name=name, grid=(L,),
        in_specs=[pl.BlockSpec((NDEV, D), lambda l: (0, 0)), pl.BlockSpec((1, D, n), lambda l: (l, 0, 0))],
        out_specs=pl.BlockSpec((1, NDEV, n), lambda l: (l, 0, 0)),
        out_shape=SDS((L, NDEV, n), f32),
        compiler_params=_cp(("arbitrary",)),
    )(c_all, ada_w)


def ada_grad(c_all, dmod, name):
    L, _, n = dmod.shape
    D = c_all.shape[1]

    def body(c_ref, d_ref, o_ref):
        cond = jnp.concatenate([_silu(c_ref[...]), jnp.zeros((8, D), f32)], axis=0)
        dm = jnp.concatenate([d_ref[0], jnp.zeros((8, n), f32)], axis=0)
        o_ref[0] = _tn(cond, dm)

    return pl.pallas_call(
        body, name=name, grid=(L,),
        in_specs=[pl.BlockSpec((NDEV, D), lambda l: (0, 0)), pl.BlockSpec((1, NDEV, n), lambda l: (l, 0, 0))],
        out_specs=pl.BlockSpec((1, D, n), lambda l: (l, 0, 0)),
        out_shape=SDS((L, D, n), f32),
        compiler_params=_cp(("arbitrary",)),
    )(c_all, dmod)


def reduce_adam(parts, w, m, v, tr, name):
    n, R, C = parts.shape
    c1 = 1.0 / (1.0 - ADAM_B1 ** ADAM_STEP)
    c2 = 1.0 / (1.0 - ADAM_B2 ** ADAM_STEP)

    def body(p_ref, w_ref, m_ref, v_ref, g_ref, d_ref, nm_ref, nv_ref):
        g = p_ref[0].astype(f32)
        for s in range(1, n):
            g = g + p_ref[s].astype(f32)
        nm = ADAM_B1 * m_ref[...] + (1.0 - ADAM_B1) * g
        nv = ADAM_B2 * v_ref[...] + (1.0 - ADAM_B2) * (g * g)
        g_ref[...] = g
        nm_ref[...] = nm
        nv_ref[...] = nv
        d_ref[...] = -ADAM_LR * ((nm * c1) / (jnp.sqrt(nv * c2) + ADAM_EPS) + ADAM_WD * w_ref[...])

    blk = pl.BlockSpec((tr, C), lambda i: (i, 0))
    return pl.pallas_call(
        body, name=name, grid=(R // tr,),
        in_specs=[pl.BlockSpec((n, tr, C), lambda i: (0, i, 0)), blk, blk, blk],
        out_specs=[blk] * 4, out_shape=[SDS((R, C), f32)] * 4,
        compiler_params=_cp(("arbitrary",)),
    )(parts, w, m, v)


def all_gather(xs, name):
    n = len(xs)

    def body(*refs):
        x_refs, out_refs = refs[:n], refs[n:2 * n]
        send_sems, recv_sems, local_sems = refs[2 * n:]
        x_, y_, c_ = _my_pos()
        me, sibling = (x_, y_, c_), (x_, y_, 1 - c_)
        chips = [(1 - x_, y_), (x_, 1 - y_), (1 - x_, 1 - y_)]

        def rows(a, px, py, pc):
            return out_refs[a].at[4 * px + 2 * py + pc]

        def copy(a, k, block, to, own=False):
            return pltpu.make_async_remote_copy(
                src_ref=x_refs[a] if own else rows(a, *block), dst_ref=rows(a, *block),
                send_sem=send_sems.at[k, a], recv_sem=recv_sems.at[k, a], device_id=to, device_id_type=pl.DeviceIdType.MESH)

        mine = [pltpu.make_async_copy(x_refs[a], rows(a, *me), local_sems.at[a]) for a in range(n)]
        for cp in mine:
            cp.start()
        first = []
        for a in range(n):
            first.append(copy(a, 0, me, sibling, own=True))
            first += [copy(a, 1 + j, me, (*chip, c_), own=True) for j, chip in enumerate(chips)]
        for cp in first:
            cp.start()
        passed = []
        for j, chip in enumerate(chips):
            for a in range(n):
                copy(a, 1 + j, (*chip, c_), me).wait_recv()
                cp = copy(a, 4 + j, (*chip, c_), sibling)
                cp.start()
                passed.append(cp)
        for a in range(n):
            copy(a, 0, sibling, me).wait_recv()
            for j, chip in enumerate(chips):
                copy(a, 4 + j, (*chip, 1 - c_), me).wait_recv()
        for cp in first + passed:
            cp.wait_send()
        for cp in mine:
            cp.wait()

    any_ = pl.BlockSpec(memory_space=pl.ANY)
    return pl.pallas_call(
        body, name=name, out_shape=[SDS((NDEV,) + x.shape, x.dtype) for x in xs],
        in_specs=[any_] * n, out_specs=[any_] * n,
        scratch_shapes=[pltpu.SemaphoreType.DMA((7, n)), pltpu.SemaphoreType.DMA((7, n)), pltpu.SemaphoreType.DMA((n,))],
    )(*xs)


GDN_COLS = ((0, GDN_CONV + GDN_V_W, 0), (GDN_CONV + GDN_V_W, GDN_CONV + GDN_V_W + 16, GDN_CONV + GDN_V_W),
            (GDN_CONV + GDN_V_W + 16, GDN_IN, GDN_CONV + GDN_V_W + 128))
FOX_COLS = ((0, FOX_IN, 0),)


def _col_pieces(d, per, cols):
    lo, hi = per * d, per * (d + 1)
    out = []
    for a, b, dst in cols:
        s, e = max(lo, a), min(hi, b)
        if s < e:
            out.append((s - lo, e - s, dst + s - a))
    return out


def cols_from_blocks(g, cols, n_out, name):
    _, L, R, C = g.shape
    tr = min(256, R)

    def body(g_ref, o_ref):
        o_ref[...] = jnp.zeros_like(o_ref)
        for d in range(NDEV):
            for off, ln, dst in _col_pieces(d, C, cols):
                o_ref[0, :, dst:dst + ln] = g_ref[d, 0, :, off:off + ln]

    return pl.pallas_call(
        body, name=name, grid=(L, R // tr),
        in_specs=[pl.BlockSpec((NDEV, 1, tr, C), lambda l, i: (0, l, i, 0))],
        out_specs=pl.BlockSpec((1, tr, n_out), lambda l, i: (l, i, 0)),
        out_shape=SDS((L, R, n_out), g.dtype),
        compiler_params=_cp(("arbitrary", "arbitrary")),
    )(g)


def blocks_from_cols(dw, C, cols, name):
    R, n_in = dw.shape
    tr = min(256, R)

    def body(x_ref, o_ref):
        for d in range(NDEV):
            for off, ln, src in _col_pieces(d, C, cols):
                o_ref[d, :, off:off + ln] = x_ref[:, src:src + ln].astype(bf16)

    return pl.pallas_call(
        body, name=name, grid=(R // tr,),
        in_specs=[pl.BlockSpec((tr, n_in), lambda i: (i, 0))],
        out_specs=pl.BlockSpec((NDEV, tr, C), lambda i: (0, i, 0)),
        out_shape=SDS((NDEV, R, C), bf16),
        compiler_params=_cp(("arbitrary",)),
    )(dw)


BIG = ("a_w_in", "a_conv_w", "a_w_out", "b_w_in", "b_w_out")
SMALL = ("norm_w", "ada_b", "a_A_log", "a_dt_bias", "a_norm_w", "b_f_bias", "b_qn_w", "b_kn_w", "final_norm_w")


def _pack_small(arrs):
    rows = []
    for a in arrs:
        fl = a.reshape(-1)
        pad = (-fl.shape[0]) % 128
        if pad:
            fl = jnp.concatenate([fl, jnp.zeros((pad,), fl.dtype)])
        rows.append(fl)
    flat = jnp.concatenate(rows)
    pad = (-flat.shape[0]) % (8 * 128)
    if pad:
        flat = jnp.concatenate([flat, jnp.zeros((pad,), flat.dtype)])
    return flat.reshape(-1, 128)


def _unpack(packed, shapes, align):
    flat = packed.reshape(-1)
    out, off = [], 0
    for shp in shapes:
        n = 1
        for d in shp:
            n *= d
        out.append(flat[off:off + n].reshape(shp))
        off += n + ((-n) % align)
    return out


def _full_from_gathered(g, shard_shape, axis):
    g = jnp.moveaxis(g, 0, axis)
    shp = list(shard_shape)
    shp[axis] *= NDEV
    return g.reshape(shp)


def _pad_lanes(v, n=128):
    v = v.reshape(1, -1)
    return jnp.concatenate([v, jnp.zeros((1, n - v.shape[1]), v.dtype)], axis=1)


def _with_side(res, side):
    return res if side is not None else (res, None)


def gdn_layer_fwd(x, mod, nw, weights, tag, side=None, side_intra=None):
    W_in, conv_w, A_log, dt_bias, a_nw, W_out = weights
    shift, scale, gate = mod
    (proj, h), got = _with_side(inproj_fwd(x, nw, scale, shift, W_in, GDN_TN, f"{tag}_inproj", side), side)
    qkvc = gdn_prep_fwd(proj, conv_w, f"{tag}_prep")
    gc, beta = gdn_gates_fwd(proj, A_log, dt_bias, f"{tag}_gates")
    (u, w, attn, Ts), got_i = _with_side(gdn_intra_fwd(qkvc, gc, beta, f"{tag}_intra", side_intra), side_intra)
    o, states = gdn_scan_fwd(qkvc, gc, u, w, attn, f"{tag}_scan")
    x_new, y, og = gdn_out_fwd(o, proj, a_nw, W_out, x, gate, f"{tag}_out")
    return x_new, (x, proj, h, qkvc, gc, beta, o, states, Ts, y, og, u, w, attn), (got, got_i)


def gdn_layer_bwd(dxn, saved, mod, nw, weights, tag, side_dwin=None, side_ibwd=None, side_sbwd=None):
    W_in, conv_w, A_log, dt_bias, a_nw, W_out = weights
    shift, scale, gate = mod
    x, proj, h, qkvc, gc, beta, o, states, Ts, y, og, u, w, attn = saved
    dy, dgate, do, dz, da_nw = gdn_out_bwd(dxn, y, gate, o, proj, a_nw, W_out, f"{tag}_out_bwd")
    dW_out, = matmul_tn(og, dy, 512, f"{tag}_dwout")
    side0 = side_sbwd(dW_out) if side_sbwd is not None else None
    (dq_s, dk_s, dgc_s, du, dw, dattn), got0 = _with_side(
        gdn_scan_bwd(qkvc, gc, u, w, attn, states, do, f"{tag}_scan_bwd", side0), side0)
    dqkvc, dgc, dbeta = gdn_intra_bwd(qkvc, gc, beta, Ts, du, dw, dattn, dq_s, dk_s, dgc_s, f"{tag}_intra_bwd")
    db, da, dA_log, ddt = gdn_gates_bwd(proj, A_log, dt_bias, dgc, dbeta, f"{tag}_gates_bwd")
    dqkv, dconv_w = gdn_prep_bwd(proj, conv_w, dqkvc, f"{tag}_prep_bwd")
    dproj = jnp.concatenate([dqkv, dz, db, da], axis=1)
    (dW_in,), got1 = _with_side(matmul_tn(h, dproj, GDN_TN, f"{tag}_dwin", side_dwin), side_dwin)
    side2 = side_ibwd(dW_in, dconv_w) if side_ibwd is not None else None
    (dx, dnw, dscale, dshift), got2 = _with_side(
        inproj_bwd_x(x, nw, scale, shift, W_in, dproj, dxn, GDN_TN, f"{tag}_inproj_bwd", side2), side2)
    grads = dict(norm_w=dnw, W_in=dW_in, conv_w=dconv_w, A_log=dA_log[:, :16], dt_bias=ddt[:, :16], a_nw=da_nw, W_out=dW_out,
                 dmod=jnp.concatenate([dshift, dscale, dgate], axis=1))
    return dx, grads, (got0, got1, got2)


def fox_layer_fwd(x, mod, nw, weights, tag, side=None):
    W_in, f_bias, qn_w, kn_w, W_out = weights
    shift, scale, gate = mod
    (proj, h), got = _with_side(inproj_fwd(x, nw, scale, shift, W_in, FOX_TN, f"{tag}_inproj", side), side)
    Q, K, V = fox_prep_fwd(proj, f_bias, qn_w, kn_w, f"{tag}_prep")
    O = fox_attn_fwd(Q, K, V, f"{tag}_attn")
    x_new, y, og = fox_out_fwd(O, proj, W_out, x, gate, f"{tag}_out")
    return x_new, (x, proj, h, Q, K, V, O, y, og), (got, None)


def fox_layer_bwd(dxn, saved, mod, nw, weights, tag, side_dwin=None):
    W_in, f_bias, qn_w, kn_w, W_out = weights
    shift, scale, gate = mod
    x, proj, h, Q, K, V, O, y, og = saved
    dy, dgate, dO, dz = fox_out_bwd(dxn, y, gate, O, proj, W_out, f"{tag}_out_bwd")
    dW_out, = matmul_tn(og, dy, 512, f"{tag}_dwout")
    dQ, dK, dV = fox_attn_bwd(Q, K, V, dO, O, f"{tag}_attn_bwd")
    dq, dk, dv, df, dfb, dqw, dkw = fox_prep_bwd(proj, f_bias, qn_w, kn_w, dQ, dK, dV, f"{tag}_prep_bwd")
    dproj = jnp.concatenate([dq, dk, dv, dz, df], axis=1)
    (dW_in,), got1 = _with_side(matmul_tn(h, dproj, FOX_TN, f"{tag}_dwin", side_dwin), side_dwin)
    dx, dnw, dscale, dshift = inproj_bwd_x(x, nw, scale, shift, W_in, dproj, dxn, FOX_TN, f"{tag}_inproj_bwd")
    grads = dict(norm_w=dnw, W_in=dW_in, f_bias=dfb[:, :16], qn_w=dqw, kn_w=dkw, W_out=dW_out,
                 dmod=jnp.concatenate([dshift, dscale, dgate], axis=1))
    return dx, grads, (None, got1, None)


class LocalPlan:
    def __init__(self, full):
        self.full = full

    def layer_weights(self, i):
        j, f = i // 2, self.full
        return (f["a_w_in"][j], f["a_w_out"][j], f["a_conv_w"][j]) if i % 2 == 0 else (f["b_w_in"][j], f["b_w_out"][j])

    def fwd_sides(self, i):
        return None, None

    def fwd_got(self, i, gots):
        pass

    def bwd_sides(self, i):
        return None, None, None

    def bwd_got(self, i, grads, gots):
        pass


def device_step(x, mod_all, norm_w, small, final_norm_w, target, plan):
    D = x.shape[1]
    mods = [(mod_all[i:i + 1, 0:D], mod_all[i:i + 1, D:2 * D], mod_all[i:i + 1, 2 * D:3 * D]) for i in range(4)]

    def weights(i):
        j = i // 2
        if i % 2 == 0:
            W_in, W_out, conv_w = plan.layer_weights(i)
            return (W_in, conv_w, _pad_lanes(small["a_A_log"][j]), _pad_lanes(small["a_dt_bias"][j]), small["a_norm_w"][j:j + 1], W_out)
        W_in, W_out = plan.layer_weights(i)
        return (W_in, _pad_lanes(small["b_f_bias"][j]), small["b_qn_w"][j:j + 1], small["b_kn_w"][j:j + 1], W_out)

    saved, wts = [], []
    for i in range(4):
        wts.append(weights(i))
        side, side_intra = plan.fwd_sides(i)
        if i % 2 == 0:
            x, sv, gots = gdn_layer_fwd(x, mods[i], norm_w[i:i + 1], wts[i], f"L{i}", side, side_intra)
        else:
            x, sv, gots = fox_layer_fwd(x, mods[i], norm_w[i:i + 1], wts[i], f"L{i}", side)
        plan.fwd_got(i, gots)
        saved.append(sv)
    loss, dx, dfw = final_loss(x, final_norm_w.reshape(1, D), target, "final_loss")
    lg = [None] * 4
    for i in reversed(range(4)):
        side_sbwd, side_dwin, side_ibwd = plan.bwd_sides(i)
        if i % 2 == 0:
            dx, lg[i], gots = gdn_layer_bwd(dx, saved[i], mods[i], norm_w[i:i + 1], wts[i], f"L{i}", side_dwin, side_ibwd, side_sbwd)
        else:
            dx, lg[i], gots = fox_layer_bwd(dx, saved[i], mods[i], norm_w[i:i + 1], wts[i], f"L{i}", side_dwin)
        plan.bwd_got(i, lg[i], gots)
    g = dict(
        norm_w=jnp.concatenate([lg[i]["norm_w"] for i in range(4)], axis=0),
        dmod=jnp.concatenate([lg[i]["dmod"] for i in range(4)], axis=0),
        a_w_in=[lg[i]["W_in"] for i in (0, 2)],
        a_conv_w=jnp.stack([lg[i]["conv_w"] for i in (0, 2)]),
        a_A_log=jnp.concatenate([lg[i]["A_log"] for i in (0, 2)], axis=0),
        a_dt_bias=jnp.concatenate([lg[i]["dt_bias"] for i in (0, 2)], axis=0),
        a_norm_w=jnp.concatenate([lg[i]["a_nw"] for i in (0, 2)], axis=0),
        a_w_out=[lg[i]["W_out"] for i in (0, 2)],
        b_w_in=[lg[i]["W_in"] for i in (1, 3)],
        b_f_bias=jnp.concatenate([lg[i]["f_bias"] for i in (1, 3)], axis=0),
        b_qn_w=jnp.concatenate([lg[i]["qn_w"] for i in (1, 3)], axis=0),
        b_kn_w=jnp.concatenate([lg[i]["kn_w"] for i in (1, 3)], axis=0),
        b_w_out=[lg[i]["W_out"] for i in (1, 3)],
        final_norm_w=dfw.reshape(-1),
    )
    return loss[0, 0], dx, g


class MeshPlan:
    def __init__(self, shards, w0, conv_full):
        self.shards = shards
        self.w = {0: w0}
        self.conv = conv_full
        self.recv = {}
        self.pending = {}
        self.names = [None, None, None]

    def layer_weights(self, i):
        return self.w[i]

    def _gather_side(self, names, j):
        out = []
        for n in names:
            sh = self.shards[n][j]
            out.append(sh.reshape(-1, sh.shape[-1]))
        return ("gather", out)

    def fwd_sides(self, i):
        if i == 0:
            return self._gather_side(("b_w_in", "b_w_out"), 0), self._gather_side(("a_w_in", "a_w_out"), 1)
        if i == 2:
            return self._gather_side(("b_w_in", "b_w_out"), 1), None
        return None, None

    def _set_weights(self, layer, got):
        g_in, g_out = got
        D = g_out.shape[-1]
        j = layer // 2
        if layer % 2 == 1:
            W_in = cols_from_blocks(g_in[:, None], FOX_COLS, FOX_IN_PAD, f"b_w_in_cols{j}")[0]
            self.w[layer] = (W_in, g_out.reshape(-1, D))
        else:
            W_in = cols_from_blocks(g_in[:, None], GDN_COLS, GDN_IN_PAD, f"a_w_in_cols{j}")[0]
            self.w[layer] = (W_in, g_out.reshape(-1, D), self.conv[j])

    def fwd_got(self, i, gots):
        got, got_intra = gots
        if got is not None:
            self._set_weights(i + 1, got)
        if got_intra is not None:
            self._set_weights(i + 2, got_intra)

    @staticmethod
    def _out_blocks(dW_out):
        return dW_out.astype(bf16).reshape(NDEV, -1, dW_out.shape[-1])

    def _in_blocks(self, name, j, dW_in):
        cols = GDN_COLS if name == "a_w_in" else FOX_COLS
        return blocks_from_cols(dW_in, self.shards[name].shape[-1], cols, f"{name}_blocks{j}")

    def bwd_sides(self, i):
        self.names = [None, None, None]
        side_dwin = None
        if i in self.pending:
            self.names[1], arrs = self.pending.pop(i)
            side_dwin = ("scatter", arrs)
        if i % 2 == 1:
            return None, side_dwin, None
        j = i // 2

        def side_ibwd(dW_in, dconv_w):
            self.names[2], arrs = [("a_w_in", j)], [self._in_blocks("a_w_in", j, dW_in)]
            if j == 0:
                conv = jnp.stack([dconv_w, self._dconv1])
                n = conv.shape[-1] // NDEV
                self.names[2].append(("a_conv_w", None))
                arrs.append(jnp.moveaxis(conv.reshape(2, 4, NDEV, n), 2, 0).reshape(NDEV, 8, n))
            return ("scatter", arrs)

        def side_sbwd(dW_out):
            self.names[0] = [("a_w_out", 0)]
            return ("scatter", [self._out_blocks(dW_out)])

        return (side_sbwd if j == 0 else None), side_dwin, side_ibwd

    def bwd_got(self, i, grads, gots):
        for names, got in zip(self.names, gots):
            if got is not None:
                self.recv.update(zip(names, got))
        if i % 2 == 1:
            j = i // 2
            self.pending[i - 1] = ([("b_w_in", j), ("b_w_out", j)], [self._in_blocks("b_w_in", j, grads["W_in"]), self._out_blocks(grads["W_out"])])
        elif i == 2:
            self.pending[1] = ([("a_w_out", 1)], [self._out_blocks(grads["W_out"])])
            self._dconv1 = grads["conv_w"]


def kernel(x, c, norm_w, ada_w, ada_b, a_w_in, a_conv_w, a_A_log, a_dt_bias, a_norm_w, a_w_out, b_w_in, b_f_bias, b_qn_w, b_kn_w, b_w_out, final_norm_w, loss_target, m_norm_w, m_ada_w, m_ada_b, m_a_w_in, m_a_conv_w, m_a_A_log, m_a_dt_bias, m_a_norm_w, m_a_w_out, m_b_w_in, m_b_f_bias, m_b_qn_w, m_b_kn_w, m_b_w_out, m_final_norm_w, v_norm_w, v_ada_w, v_ada_b, v_a_w_in, v_a_conv_w, v_a_A_log, v_a_dt_bias, v_a_norm_w, v_a_w_out, v_b_w_in, v_b_f_bias, v_b_qn_w, v_b_kn_w, v_b_w_out, v_final_norm_w):
    W = dict(norm_w=norm_w, ada_w=ada_w, ada_b=ada_b, a_w_in=a_w_in, a_conv_w=a_conv_w, a_A_log=a_A_log, a_dt_bias=a_dt_bias,
             a_norm_w=a_norm_w, a_w_out=a_w_out, b_w_in=b_w_in, b_f_bias=b_f_bias, b_qn_w=b_qn_w, b_kn_w=b_kn_w, b_w_out=b_w_out,
             final_norm_w=final_norm_w)
    M = dict(norm_w=m_norm_w, ada_w=m_ada_w, ada_b=m_ada_b, a_w_in=m_a_w_in, a_conv_w=m_a_conv_w, a_A_log=m_a_A_log,
             a_dt_bias=m_a_dt_bias, a_norm_w=m_a_norm_w, a_w_out=m_a_w_out, b_w_in=m_b_w_in, b_f_bias=m_b_f_bias, b_qn_w=m_b_qn_w,
             b_kn_w=m_b_kn_w, b_w_out=m_b_w_out, final_norm_w=m_final_norm_w)
    V = dict(norm_w=v_norm_w, ada_w=v_ada_w, ada_b=v_ada_b, a_w_in=v_a_w_in, a_conv_w=v_a_conv_w, a_A_log=v_a_A_log,
             a_dt_bias=v_a_dt_bias, a_norm_w=v_a_norm_w, a_w_out=v_a_w_out, b_w_in=v_b_w_in, b_f_bias=v_b_f_bias, b_qn_w=v_b_qn_w,
             b_kn_w=v_b_kn_w, b_w_out=v_b_w_out, final_norm_w=v_final_norm_w)
    S, D = x.shape[1], x.shape[2]
    me = 4 * lax.axis_index("x") + 2 * lax.axis_index("y") + lax.axis_index("c")
    small_shapes = [W[n].shape for n in SMALL]

    shards = {n: W[n].astype(bf16) for n in ("a_w_in", "a_w_out", "b_w_in", "b_w_out")}
    gath = all_gather([shards["a_w_in"][0], shards["a_w_out"][0], a_conv_w.reshape(8, -1), c.reshape(8, D // 8)], "gather_w0")
    conv_full = _full_from_gathered(gath[2].reshape((NDEV,) + a_conv_w.shape), a_conv_w.shape, 2)
    w0 = (cols_from_blocks(gath[0][:, None], GDN_COLS, GDN_IN_PAD, "a_w_in_cols0")[0], gath[1].reshape(-1, D), conv_full[0])
    plan = MeshPlan(shards, w0, conv_full)
    c_all = gath[3].reshape(NDEV, D)

    mod_part = ada_fwd(c_all, ada_w, "ada_fwd")
    n_ada = ada_w.shape[2]
    mod_g = all_gather([mod_part.reshape(4 * NDEV, n_ada)], "gather_mod")[0].reshape(NDEV, 4, NDEV, n_ada)
    mod_mine = lax.dynamic_index_in_dim(mod_g, me, axis=2, keepdims=False)
    mod_all = jnp.moveaxis(mod_mine, 0, 1).reshape(4, NDEV * n_ada) + ada_b

    loss, dx, g = device_step(x[0], mod_all, norm_w, W, final_norm_w, loss_target[0], plan)
    loss = lax.psum(loss, MESH_AXES)

    g_small = dict(g, ada_b=g["dmod"])
    sp = _pack_small([g_small[n] for n in SMALL])
    sp_all = all_gather([sp], "gather_small")[0]
    sw, sm, sv = (_pack_small([T[n] for n in SMALL]) for T in (W, M, V))
    sg, sd, snm, snv = (_unpack(t, small_shapes, 128) for t in reduce_adam(sp_all, sw, sm, sv, sp.shape[0], "adam_small"))

    off_b = 0
    for n, shp in zip(SMALL, small_shapes):
        if n == "ada_b":
            break
        cnt = 1
        for d in shp:
            cnt *= d
        off_b += cnt + ((-cnt) % 128)
    dmod_all = sp_all.reshape(NDEV, -1)[:, off_b:off_b + 4 * 3 * D].reshape(NDEV, 4, 3 * D)
    dmod_cols = lax.dynamic_slice_in_dim(dmod_all, me * n_ada, n_ada, axis=2)
    g_ada = ada_grad(c_all, jnp.moveaxis(dmod_cols, 0, 1), "ada_grad")
    r_ada = reduce_adam(g_ada.reshape(1, 4 * D, n_ada), *(T["ada_w"].reshape(4 * D, n_ada) for T in (W, M, V)), 512, "adam_ada")
    ag, ad, anm, anv = (t.reshape(ada_w.shape) for t in r_ada)

    big = {}
    for n in BIG:
        C = W[n].shape[-1]
        parts = plan.recv[(n, None)] if n == "a_conv_w" else jnp.stack([plan.recv[(n, 0)], plan.recv[(n, 1)]], axis=1).reshape(NDEV, -1, C)
        res = reduce_adam(parts, *(T[n].reshape(parts.shape[1:]) for T in (W, M, V)), min(256, parts.shape[1]), f"adam_{n}")
        big[n] = [t.reshape(W[n].shape) for t in res]

    outs = {}
    for idx, (k, sm_l, ada_t) in enumerate((("grad", sg, ag), ("delta", sd, ad), ("new_m", snm, anm), ("new_v", snv, anv))):
        d = dict(zip(SMALL, sm_l))
        d.update({n: big[n][idx] for n in BIG})
        d["ada_w"] = ada_t
        outs[k] = d
    order = ("norm_w", "ada_w", "ada_b", "a_w_in", "a_conv_w", "a_A_log", "a_dt_bias", "a_norm_w", "a_w_out", "b_w_in", "b_f_bias",
             "b_qn_w", "b_kn_w", "b_w_out", "final_norm_w")
    return (loss, dx[None], *[outs["grad"][n] for n in order], *[outs["delta"][n] for n in order],
            *[outs["new_m"][n] for n in order], *[outs["new_v"][n] for n in order])
```

```python
import functools

import jax
import jax.numpy as jnp
from jax import lax
from jax.experimental import pallas as pl
from jax.experimental.pallas import tpu as pltpu

f32 = jnp.float32
bf16 = jnp.bfloat16
SDS = jax.ShapeDtypeStruct

EPS = 1e-6
CHUNK = 64
HD = 128
GDN_QK_HEADS = 8
GDN_V_HEADS = 16
GDN_QK_W = GDN_QK_HEADS * HD
GDN_V_W = GDN_V_HEADS * HD
GDN_CONV = 2 * GDN_QK_W + GDN_V_W
GDN_IN = GDN_CONV + GDN_V_W + 2 * GDN_V_HEADS
GDN_IN_PAD = GDN_CONV + GDN_V_W + 256
GDN_TN = 640
FOX_H = 16
FOX_D = 64
FOX_W = FOX_H * FOX_D
FOX_IN = 4 * FOX_W + FOX_H
FOX_IN_PAD = 4 * FOX_W + 128
FOX_TN = 1408
FOX_PW = FOX_H * 128
NDEV = 8
MESH_AXES = ("x", "y", "c")
NEG = -1e30

ADAM_LR = 0.001
ADAM_B1 = 0.9
ADAM_B2 = 0.999
ADAM_EPS = 1e-08
ADAM_WD = 0.01
ADAM_STEP = 10

VMEM_LIMIT = 56 * 1024 * 1024


def _cp(sem=None):
    return pltpu.CompilerParams(dimension_semantics=sem, vmem_limit_bytes=VMEM_LIMIT)


def _bdot(a, b, dims):
    return lax.dot_general(a.astype(bf16), b.astype(bf16), (dims, ((), ())), preferred_element_type=f32)


def _nn(a, b):
    return _bdot(a, b, ((1,), (0,)))


def _nt(a, b):
    return _bdot(a, b, ((1,), (1,)))


def _tn(a, b):
    return _bdot(a, b, ((0,), (0,)))


def _hdot(a, b, dims=((1,), (0,))):
    return lax.dot_general(a, b, (dims, ((), ())), precision=lax.Precision.HIGHEST, preferred_element_type=f32)


def _split2(a):
    hi = a.astype(bf16)
    return hi, (a - hi.astype(f32)).astype(bf16)


def _dot3(a, b):
    (ah, al), (bh, bl) = a, b
    return (jnp.dot(ah, bh, preferred_element_type=f32) + jnp.dot(ah, bl, preferred_element_type=f32)
            + jnp.dot(al, bh, preferred_element_type=f32))


@jax.custom_vjp
def _mm(a, b):
    return _nn(a, b)


_mm.defvjp(lambda a, b: (_nn(a, b), (a, b)), lambda r, g: (_nt(g, r[1]), _tn(r[0], g)))


@jax.custom_vjp
def _mm_nt(a, b):
    return _nt(a, b)


_mm_nt.defvjp(lambda a, b: (_nt(a, b), (a, b)), lambda r, g: (_nn(g, r[1]), _tn(g, r[0])))


@jax.custom_vjp
def _mm_tn(a, b):
    return _tn(a, b)


_mm_tn.defvjp(lambda a, b: (_tn(a, b), (a, b)), lambda r, g: (_nt(r[1], g), _nn(r[0], g)))


def _silu(x):
    return x * jax.nn.sigmoid(x)


def _rms_mod(x, nw, scale, shift):
    r = lax.rsqrt(jnp.mean(x * x, axis=-1, keepdims=True) + EPS)
    return (x * r * nw) * (1.0 + scale) + shift


def _rows(S, want):
    return min(want, S)


def _my_pos():
    return lax.axis_index("x"), lax.axis_index("y"), lax.axis_index("c")


def _exchange_copies(kind, x_refs, out_refs, send_sems, recv_sems, local_sems):
    x_, y_, c_ = _my_pos()
    me = 4 * x_ + 2 * y_ + c_
    own = kind == "gather"
    cps = [pltpu.make_async_copy(x_refs[a] if own else x_refs[a].at[me], out_refs[a].at[me], local_sems.at[a])
           for a in range(len(x_refs))]
    for rel in range(1, NDEV):
        px = (x_ + ((rel >> 2) & 1)) % 2
        py = (y_ + ((rel >> 1) & 1)) % 2
        pc = (c_ + (rel & 1)) % 2
        for a in range(len(x_refs)):
            cps.append(pltpu.make_async_remote_copy(
                src_ref=x_refs[a] if own else x_refs[a].at[4 * px + 2 * py + pc], dst_ref=out_refs[a].at[me],
                send_sem=send_sems.at[rel - 1, a], recv_sem=recv_sems.at[rel - 1, a],
                device_id=(px, py, pc), device_id_type=pl.DeviceIdType.MESH))
    return cps


def _exchange_scratch(n):
    return [pltpu.SemaphoreType.DMA((NDEV - 1, n)), pltpu.SemaphoreType.DMA((NDEV - 1, n)), pltpu.SemaphoreType.DMA((n,))]


def _call(body, *, name, grid, in_specs, out_specs, out_shape, args, scratch=(), side=None):
    params = _cp(("arbitrary",) * len(grid))
    if side is None:
        return pl.pallas_call(body, name=name, grid=grid, in_specs=in_specs, out_specs=out_specs, out_shape=out_shape,
                              scratch_shapes=list(scratch), compiler_params=params)(*args)
    kind, xs = side
    n_in, n_out, n_scr, ns = len(in_specs), len(out_shape), len(scratch), len(xs)
    steps = 1
    for g in grid:
        steps *= g

    def wrapped(*refs):
        o0 = n_in + ns
        s0 = o0 + n_out + ns
        step = pl.program_id(0)
        for d in range(1, len(grid)):
            step = step * grid[d] + pl.program_id(d)

        def copies():
            return _exchange_copies(kind, refs[n_in:o0], refs[o0 + n_out:s0], *refs[s0 + n_scr:])

        @pl.when(step == 0)
        def _():
            for cp in copies():
                cp.start()

        body(*refs[:n_in], *refs[o0:o0 + n_out], *refs[s0:s0 + n_scr])

        @pl.when(step == steps - 1)
        def _():
            for cp in copies():
                cp.wait()

    any_ = pl.BlockSpec(memory_space=pl.ANY)
    side_shapes = [SDS((NDEV,) + x.shape if kind == "gather" else x.shape, x.dtype) for x in xs]
    outs = pl.pallas_call(wrapped, name=name, grid=grid, in_specs=list(in_specs) + [any_] * ns,
                          out_specs=list(out_specs) + [any_] * ns, out_shape=list(out_shape) + side_shapes,
                          scratch_shapes=list(scratch) + _exchange_scratch(ns), compiler_params=params)(*args, *xs)
    return outs[:n_out], outs[n_out:]


def inproj_fwd(x, nw, scale, shift, W, tn, name, side=None):
    S, D = x.shape
    N = W.shape[1]
    tm = _rows(S, 1024)

    def body(x_ref, nw_ref, sc_ref, sh_ref, w_ref, proj_ref, h_ref):
        @pl.when(pl.program_id(1) == 0)
        def _():
            h_ref[...] = _rms_mod(x_ref[...], nw_ref[...], sc_ref[...], sh_ref[...]).astype(bf16)

        proj_ref[...] = jnp.dot(h_ref[...], w_ref[...], preferred_element_type=f32)

    vec = pl.BlockSpec((1, D), lambda i, j: (0, 0))
    return _call(
        body, name=name, grid=(S // tm, N // tn),
        in_specs=[pl.BlockSpec((tm, D), lambda i, j: (i, 0)), vec, vec, vec, pl.BlockSpec((D, tn), lambda i, j: (0, j))],
        out_specs=[pl.BlockSpec((tm, tn), lambda i, j: (i, j)), pl.BlockSpec((tm, D), lambda i, j: (i, 0))],
        out_shape=[SDS((S, N), f32), SDS((S, D), bf16)], args=(x, nw, scale, shift, W), side=side)


def inproj_bwd_x(x, nw, scale, shift, W, dproj, dx_res, tn, name, side=None):
    S, D = x.shape
    N = W.shape[1]
    tm = _rows(S, 1024)
    nj = N // tn

    def body(x_ref, nw_ref, sc_ref, sh_ref, w_ref, dp_ref, dxr_ref, dx_ref, dnw_ref, dsc_ref, dsh_ref, acc):
        i, j = pl.program_id(0), pl.program_id(1)

        @pl.when(j == 0)
        def _():
            acc[...] = jnp.zeros_like(acc)

        @pl.when((i == 0) & (j == 0))
        def _():
            dnw_ref[...] = jnp.zeros_like(dnw_ref)
            dsc_ref[...] = jnp.zeros_like(dsc_ref)
            dsh_ref[...] = jnp.zeros_like(dsh_ref)

        acc[...] += _nt(dp_ref[...], w_ref[...])

        @pl.when(j == nj - 1)
        def _():
            _, vjp = jax.vjp(_rms_mod, x_ref[...], nw_ref[...], sc_ref[...], sh_ref[...])
            dx, dnw, dsc, dsh = vjp(acc[...])
            dx_ref[...] = dxr_ref[...] + dx
            dnw_ref[...] += dnw
            dsc_ref[...] += dsc
            dsh_ref[...] += dsh

    vec = pl.BlockSpec((1, D), lambda i, j: (0, 0))
    row = pl.BlockSpec((tm, D), lambda i, j: (i, 0))
    return _call(
        body, name=name, grid=(S // tm, nj),
        in_specs=[row, vec, vec, vec, pl.BlockSpec((D, tn), lambda i, j: (0, j)), pl.BlockSpec((tm, tn), lambda i, j: (i, j)), row],
        out_specs=[row, vec, vec, vec],
        out_shape=[SDS((S, D), f32), SDS((1, D), f32), SDS((1, D), f32), SDS((1, D), f32)],
        scratch=[pltpu.VMEM((tm, D), f32)], args=(x, nw, scale, shift, W, dproj, dx_res), side=side)


def matmul_tn(a, b, tn, name, side=None):
    S, K = a.shape
    N = b.shape[1]
    tm = _rows(S, 1024)
    ni = S // tm

    def body(a_ref, b_ref, o_ref):
        @pl.when(pl.program_id(1) == 0)
        def _():
            o_ref[...] = jnp.zeros_like(o_ref)

        o_ref[...] += _tn(a_ref[...], b_ref[...])

    return _call(
        body, name=name, grid=(N // tn, ni),
        in_specs=[pl.BlockSpec((tm, K), lambda j, i: (i, 0)), pl.BlockSpec((tm, tn), lambda j, i: (i, j))],
        out_specs=[pl.BlockSpec((K, tn), lambda j, i: (0, j))],
        out_shape=[SDS((K, N), f32)], args=(a, b), side=side)


def _conv_taps(xs, w, n_out):
    taps = []
    for j in range(4):
        s = 3 - j
        sh = xs if s == 0 else pltpu.roll(xs, s, axis=0)
        taps.append(sh[8:8 + n_out])
    conv = taps[0] * w[0] + taps[1] * w[1] + taps[2] * w[2] + taps[3] * w[3]
    return taps, conv


def _act_norm(conv, mul_norm, mul_plain):
    s = _silu(conv)
    r = lax.rsqrt(jnp.sum(s * s, axis=-1, keepdims=True) + EPS)
    return s * (mul_norm * r + mul_plain)


def _gdn_prep_mults(j):
    is_q = j < GDN_QK_HEADS
    is_k = (j >= GDN_QK_HEADS) & (j < 2 * GDN_QK_HEADS)
    mul_norm = jnp.where(is_q, HD ** -0.5, jnp.where(is_k, 1.0, 0.0)).astype(f32)
    mul_plain = jnp.where(is_q | is_k, 0.0, 1.0).astype(f32)
    return mul_norm, mul_plain


def gdn_prep_fwd(proj, conv_w, name):
    S = proj.shape[0]
    R = _rows(S, 512)

    def body(x_ref, w_ref, o_ref):
        mul_norm, mul_plain = _gdn_prep_mults(pl.program_id(0))
        w = [w_ref[j:j + 1, :] for j in range(4)]

        def piece(r, c):
            t0 = pl.multiple_of(r * R, R)
            cur = x_ref[pl.ds(t0, R), :]
            prev = x_ref[pl.ds(pl.multiple_of(jnp.maximum(t0 - 8, 0), 8), 8), :]
            prev = jnp.where(r == 0, 0.0, prev)
            _, conv = _conv_taps(jnp.concatenate([prev, cur], axis=0), w, R)
            o_ref[pl.ds(t0, R), :] = _act_norm(conv, mul_norm, mul_plain)
            return c

        lax.fori_loop(0, S // R, piece, 0)

    return pl.pallas_call(
        body, name=name, grid=(GDN_CONV // 128,),
        in_specs=[pl.BlockSpec((S, 128), lambda j: (0, j)), pl.BlockSpec((4, 128), lambda j: (0, j))],
        out_specs=pl.BlockSpec((S, 128), lambda j: (0, j)),
        out_shape=SDS((S, GDN_CONV), f32),
        compiler_params=_cp(("arbitrary",)),
    )(proj, conv_w)


def gdn_prep_bwd(proj, conv_w, dqkvc, dproj, name):
    S = proj.shape[0]
    R = _rows(S, 512)
    NP = S // R

    def body(x_ref, w_ref, dn_ref, _, dx_ref, dw_ref):
        mul_norm, mul_plain = _gdn_prep_mults(pl.program_id(0))
        w = [w_ref[j:j + 1, :] for j in range(4)]

        def piece(r, dw):
            t0 = pl.multiple_of(r * R, R)
            cur = x_ref[pl.ds(t0, R), :]
            prev = x_ref[pl.ds(pl.multiple_of(jnp.maximum(t0 - 8, 0), 8), 8), :]
            prev = jnp.where(r == 0, 0.0, prev)
            nxt0 = pl.multiple_of(jnp.minimum(t0 + R, S - 8), 8)
            nxt = x_ref[pl.ds(nxt0, 8), :]
            dn_cur = dn_ref[pl.ds(t0, R), :]
            dn_nxt = jnp.where(r == NP - 1, 0.0, dn_ref[pl.ds(nxt0, 8), :])
            xs = jnp.concatenate([prev, cur, nxt], axis=0)
            taps, conv = _conv_taps(xs, w, R + 8)
            dn = jnp.concatenate([dn_cur, dn_nxt], axis=0)
            _, vjp = jax.vjp(lambda c: _act_norm(c, mul_norm, mul_plain), conv)
            dxc = vjp(dn)[0]
            n = R + 8
            dx = dxc[0:R] * w[3]
            for j in range(3):
                s = 3 - j
                dx = dx + pltpu.roll(dxc, n - s, axis=0)[0:R] * w[j]
            dx_ref[pl.ds(t0, R), :] = dx
            return tuple(dw[j] + jnp.sum(dxc[0:R] * taps[j][0:R], axis=0, keepdims=True) for j in range(4))

        dw = lax.fori_loop(0, NP, piece, tuple(jnp.zeros((1, 128), f32) for _ in range(4)))
        for j in range(4):
            dw_ref[j:j + 1, :] = dw[j]

    col = pl.BlockSpec((S, 128), lambda j: (0, j))
    wsp = pl.BlockSpec((4, 128), lambda j: (0, j))
    return pl.pallas_call(
        body, name=name, grid=(GDN_CONV // 128,),
        in_specs=[col, wsp, col, pl.BlockSpec(memory_space=pl.ANY)], out_specs=[col, wsp],
        out_shape=[SDS(dproj.shape, f32), SDS((4, GDN_CONV), f32)],
        input_output_aliases={3: 0},
        compiler_params=_cp(("arbitrary",)),
    )(proj, conv_w, dqkvc, dproj)


def _chunk_tril(R):
    ii = lax.broadcasted_iota(jnp.int32, (R, R), 0)
    jj = lax.broadcasted_iota(jnp.int32, (R, R), 1)
    return ((ii // CHUNK == jj // CHUNK) & (ii >= jj)).astype(f32)


def _gdn_gates(b, a, A_log, dt_bias, tril):
    beta = jax.nn.sigmoid(b)
    g = -jnp.exp(A_log) * jax.nn.softplus(a + dt_bias)
    return _hdot(tril, g), beta


_GDN_B_BLK = (GDN_CONV + GDN_V_W) // 128
_GDN_A_BLK = _GDN_B_BLK + 1


def gdn_gates_fwd(proj, A_log, dt_bias, name):
    S = proj.shape[0]
    R = _rows(S, 512)

    def body(b_ref, a_ref, al_ref, dt_ref, gc_ref, be_ref):
        gc, be = _gdn_gates(b_ref[...], a_ref[...], al_ref[...], dt_ref[...], _chunk_tril(R))
        gc_ref[...] = gc
        be_ref[...] = be

    vec = pl.BlockSpec((1, 128), lambda i: (0, 0))
    blk = pl.BlockSpec((R, 128), lambda i: (i, 0))
    return pl.pallas_call(
        body, name=name, grid=(S // R,),
        in_specs=[pl.BlockSpec((R, 128), lambda i: (i, _GDN_B_BLK)), pl.BlockSpec((R, 128), lambda i: (i, _GDN_A_BLK)), vec, vec],
        out_specs=[blk, blk], out_shape=[SDS((S, 128), f32), SDS((S, 128), f32)],
        compiler_params=_cp(("arbitrary",)),
    )(proj, proj, A_log, dt_bias)


def gdn_gates_bwd(proj, A_log, dt_bias, dgc, dbeta, dproj, name):
    S = proj.shape[0]
    R = _rows(S, 512)

    def body(b_ref, a_ref, al_ref, dt_ref, dgc_ref, dbe_ref, _, dp_ref, dal_ref, ddt_ref):
        @pl.when(pl.program_id(0) == 0)
        def _():
            dal_ref[...] = jnp.zeros_like(dal_ref)
            ddt_ref[...] = jnp.zeros_like(ddt_ref)

        tril = _chunk_tril(R)
        _, vjp = jax.vjp(lambda b, a, al, dt: _gdn_gates(b, a, al, dt, tril), b_ref[...], a_ref[...], al_ref[...], dt_ref[...])
        db, da, dal, ddt = vjp((dgc_ref[...], dbe_ref[...]))
        dp_ref[:, 0:128] = db
        dp_ref[:, 128:256] = da
        dal_ref[...] += dal
        ddt_ref[...] += ddt

    vec = pl.BlockSpec((1, 128), lambda i: (0, 0))
    blk = pl.BlockSpec((R, 128), lambda i: (i, 0))
    return pl.pallas_call(
        body, name=name, grid=(S // R,),
        in_specs=[pl.BlockSpec((R, 128), lambda i: (i, _GDN_B_BLK)), pl.BlockSpec((R, 128), lambda i: (i, _GDN_A_BLK)), vec, vec, blk, blk,
                  pl.BlockSpec(memory_space=pl.ANY)],
        out_specs=[pl.BlockSpec((R, 256), lambda i: (i, _GDN_B_BLK // 2)), vec, vec],
        out_shape=[SDS(dproj.shape, f32), SDS((1, 128), f32), SDS((1, 128), f32)],
        input_output_aliases={6: 0},
        compiler_params=_cp(("arbitrary",)),
    )(proj, proj, A_log, dt_bias, dgc, dbeta, dproj)


@jax.custom_vjp
def _inv_given(L, T):
    return T


def _inv_given_bwd(T, ct):
    dL = -_nt(_tn(T, ct), T)
    return dL, jnp.zeros_like(T)


_inv_given.defvjp(lambda L, T: (T, T), _inv_given_bwd)


REP = GDN_V_HEADS // GDN_QK_HEADS


def _gdn_intra_all(qs, ks, vs, gcols, bcols, Ts=None):
    H = len(vs)
    C = vs[0].shape[0]
    ii = lax.broadcasted_iota(jnp.int32, (C, C), 0)
    jj = lax.broadcasted_iota(jnp.int32, (C, C), 1)
    grows = [jnp.sum(jnp.where(ii == jj, g, 0.0), axis=0, keepdims=True) for g in gcols]
    decs = [jnp.exp(jnp.where(ii >= jj, gcols[h] - grows[h], NEG)) for h in range(H)]
    kbs = [ks[h // REP] * bcols[h] for h in range(H)]
    As = [_mm_nt(kbs[h], ks[h // REP]) for h in range(H)]
    Ls = [jnp.where(ii > jj, As[h] * decs[h], 0.0) for h in range(H)]
    if Ts is None:
        T = _neumann_inv_batched(Ls)
    else:
        T = [_inv_given(Ls[h], Ts[h]) for h in range(H)]
    us = [_mm(T[h], vs[h] * bcols[h]) for h in range(H)]
    ws = [_mm(T[h], kbs[h] * jnp.exp(gcols[h])) for h in range(H)]
    qk = [_mm_nt(qs[p], ks[p]) for p in range(H // REP)]
    return us, ws, [qk[h // REP] * decs[h] for h in range(H)], T


def _neumann_inv_batched(Ls):
    n, C = 4, Ls[0].shape[0]
    r0 = lax.broadcasted_iota(jnp.int32, (n * C, n * C), 0)
    c0 = lax.broadcasted_iota(jnp.int32, (n * C, n * C), 1)
    same = (r0 // C) == (c0 // C)

    def blockdiag(split):
        return tuple(jnp.where(same, jnp.concatenate([x] * n, axis=0), jnp.zeros((), bf16)) for x in split)

    Ms = [jnp.concatenate(Ls[b:b + n], axis=1) for b in range(0, len(Ls), n)]
    eye = (lax.broadcasted_iota(jnp.int32, (C, n * C), 0) == (lax.broadcasted_iota(jnp.int32, (C, n * C), 1) & (C - 1))).astype(f32)
    Ps = [eye - M for M in Ms]
    Ss = [_split2(M) for M in Ms]
    Bs = [blockdiag(S) for S in Ss]
    k = 1
    while 2 * k < C:
        Ss = [_split2(_dot3(S, B)) for S, B in zip(Ss, Bs)]
        Bs = [blockdiag(S) for S in Ss]
        Ps = [P + _dot3(_split2(P), B) for P, B in zip(Ps, Bs)]
        k *= 2
    return [P[:, h * C:(h + 1) * C] for P in Ps for h in range(n)]


def _gdn_scan_all(qs, ks, gcols, us, ws, attns, S0s):
    H = len(us)
    C = us[0].shape[0]
    last = lax.broadcasted_iota(jnp.int32, (C, 1), 0) == C - 1
    glast = [jnp.sum(jnp.where(last, g, 0.0), axis=0, keepdims=True) for g in gcols]
    wS = [_mm(ws[h], S0s[h]) for h in range(H)]
    qS = [_mm(qs[h // REP] * jnp.exp(gcols[h]), S0s[h]) for h in range(H)]
    vn = [us[h] - wS[h] for h in range(H)]
    av = [_mm(attns[h], vn[h]) for h in range(H)]
    kv = [_mm_tn(ks[h // REP] * jnp.exp(glast[h] - gcols[h]), vn[h]) for h in range(H)]
    return [qS[h] + av[h] for h in range(H)], [S0s[h] * jnp.exp(glast[h]) + kv[h] for h in range(H)]


def _head_cols(blk):
    lane = lax.broadcasted_iota(jnp.int32, blk.shape, 1)
    return [jnp.sum(jnp.where(lane == h, blk, 0.0), axis=1, keepdims=True) for h in range(GDN_V_HEADS)]


def _head_lanes(cols):
    lane = lax.broadcasted_iota(jnp.int32, (cols[0].shape[0], 128), 1)
    out = jnp.zeros((cols[0].shape[0], 128), f32)
    for h, c in enumerate(cols):
        out = out + jnp.where(lane == h, c, 0.0)
    return out


def _heads(ref, n):
    return [ref[:, h * HD:(h + 1) * HD] for h in range(n)]


def _gdn_specs(NC, rv=None):
    ix = (lambda n: n) if rv is None else rv
    qs = pl.BlockSpec((CHUNK, GDN_QK_W), lambda n: (ix(n), 0))
    ks = pl.BlockSpec((CHUNK, GDN_QK_W), lambda n: (ix(n), 1))
    vs = pl.BlockSpec((CHUNK, GDN_V_W), lambda n: (ix(n), 1))
    g1 = pl.BlockSpec((CHUNK, 128), lambda n: (ix(n), 0))
    wide = pl.BlockSpec((CHUNK, GDN_V_W), lambda n: (ix(n), 0))
    sq = pl.BlockSpec((1, GDN_V_HEADS, CHUNK, CHUNK), lambda n: (ix(n), 0, 0, 0))
    st = pl.BlockSpec((1, GDN_V_HEADS, HD, HD), lambda n: (ix(n), 0, 0, 0))
    return qs, ks, vs, g1, wide, sq, st


def gdn_intra_fwd(qkvc, gc, beta, name, side=None):
    S = qkvc.shape[0]
    NC = S // CHUNK

    def body(q_ref, k_ref, v_ref, gc_ref, be_ref, u_ref, w_ref, at_ref, T_ref):
        us, ws, attns, Ts = _gdn_intra_all(_heads(q_ref, GDN_QK_HEADS), _heads(k_ref, GDN_QK_HEADS), _heads(v_ref, GDN_V_HEADS),
                                           _head_cols(gc_ref[...]), _head_cols(be_ref[...]))
        for h in range(GDN_V_HEADS):
            u_ref[:, h * HD:(h + 1) * HD] = us[h]
            w_ref[:, h * HD:(h + 1) * HD] = ws[h]
            at_ref[0, h] = attns[h]
            T_ref[0, h] = Ts[h]

    qs, ks, vs, g1, wide, sq, _ = _gdn_specs(NC)
    return _call(
        body, name=name, grid=(NC,),
        in_specs=[qs, ks, vs, g1, g1], out_specs=[wide, wide, sq, sq],
        out_shape=[SDS((S, GDN_V_W), f32), SDS((S, GDN_V_W), f32),
                   SDS((NC, GDN_V_HEADS, CHUNK, CHUNK), f32), SDS((NC, GDN_V_HEADS, CHUNK, CHUNK), f32)],
        args=(qkvc, qkvc, qkvc, gc, beta), side=side)


def gdn_scan_fwd(qkvc, gc, u, w, attn, name):
    S = qkvc.shape[0]
    NC = S // CHUNK

    def body(q_ref, k_ref, gc_ref, u_ref, w_ref, at_ref, o_ref, st_ref, state):
        @pl.when(pl.program_id(0) == 0)
        def _():
            state[...] = jnp.zeros_like(state)

        S0s = [state[h] for h in range(GDN_V_HEADS)]
        os_, S1s = _gdn_scan_all(_heads(q_ref, GDN_QK_HEADS), _heads(k_ref, GDN_QK_HEADS), _head_cols(gc_ref[...]),
                                 _heads(u_ref, GDN_V_HEADS), _heads(w_ref, GDN_V_HEADS),
                                 [at_ref[0, h] for h in range(GDN_V_HEADS)], S0s)
        for h in range(GDN_V_HEADS):
            o_ref[:, h * HD:(h + 1) * HD] = os_[h]
            st_ref[0, h] = S0s[h]
            state[h] = S1s[h]

    qs, ks, _, g1, wide, sq, st = _gdn_specs(NC)
    return pl.pallas_call(
        body, name=name, grid=(NC,),
        in_specs=[qs, ks, g1, wide, wide, sq], out_specs=[wide, st],
        out_shape=[SDS((S, GDN_V_W), f32), SDS((NC, GDN_V_HEADS, HD, HD), f32)],
        scratch_shapes=[pltpu.VMEM((GDN_V_HEADS, HD, HD), f32)],
        compiler_params=_cp(("arbitrary",)),
    )(qkvc, qkvc, gc, u, w, attn)


def gdn_scan_bwd(qkvc, gc, u, w, attn, states, do, name, side=None):
    S = qkvc.shape[0]
    NC = S // CHUNK

    def body(q_ref, k_ref, gc_ref, u_ref, w_ref, at_ref, st_ref, do_ref,
             dq_ref, dk_ref, dgc_ref, du_ref, dw_ref, dat_ref, dstate):
        @pl.when(pl.program_id(0) == 0)
        def _():
            dstate[...] = jnp.zeros_like(dstate)

        VH = range(GDN_V_HEADS)
        _, vjp = jax.vjp(_gdn_scan_all, _heads(q_ref, GDN_QK_HEADS), _heads(k_ref, GDN_QK_HEADS), _head_cols(gc_ref[...]),
                         _heads(u_ref, GDN_V_HEADS), _heads(w_ref, GDN_V_HEADS), [at_ref[0, h] for h in VH],
                         [st_ref[0, h] for h in VH])
        dqs, dks, dgs, dus, dws, dats, dS0s = vjp((_heads(do_ref, GDN_V_HEADS), [dstate[h] for h in VH]))
        for p in range(GDN_QK_HEADS):
            dq_ref[:, p * HD:(p + 1) * HD] = dqs[p]
            dk_ref[:, p * HD:(p + 1) * HD] = dks[p]
        for h in VH:
            du_ref[:, h * HD:(h + 1) * HD] = dus[h]
            dw_ref[:, h * HD:(h + 1) * HD] = dws[h]
            dat_ref[0, h] = dats[h]
            dstate[h] = dS0s[h]
        dgc_ref[...] = _head_lanes(dgs)

    qs, ks, _, g1, wide, sq, st = _gdn_specs(NC, lambda n: NC - 1 - n)
    dqs = pl.BlockSpec((CHUNK, GDN_QK_W), lambda n: (NC - 1 - n, 0))
    return _call(
        body, name=name, grid=(NC,),
        in_specs=[qs, ks, g1, wide, wide, sq, st, wide],
        out_specs=[dqs, dqs, g1, wide, wide, sq],
        out_shape=[SDS((S, GDN_QK_W), f32), SDS((S, GDN_QK_W), f32), SDS((S, 128), f32), SDS((S, GDN_V_W), f32),
                   SDS((S, GDN_V_W), f32), SDS((NC, GDN_V_HEADS, CHUNK, CHUNK), f32)],
        scratch=[pltpu.VMEM((GDN_V_HEADS, HD, HD), f32)], args=(qkvc, qkvc, gc, u, w, attn, states, do), side=side)


def gdn_intra_bwd(qkvc, gc, beta, Ts, du, dw, dattn, dq_s, dk_s, dgc_s, name):
    S = qkvc.shape[0]
    NC = S // CHUNK

    def body(q_ref, k_ref, v_ref, gc_ref, be_ref, T_ref, du_ref, dw_ref, dat_ref, dqs_ref, dks_ref, dgs_ref,
             dqkv_ref, dgc_ref, dbe_ref):
        VH = range(GDN_V_HEADS)
        Ts = [T_ref[0, h] for h in VH]
        _, vjp = jax.vjp(lambda q_, k_, v_, g_, b_: _gdn_intra_all(q_, k_, v_, g_, b_, Ts)[:3],
                         _heads(q_ref, GDN_QK_HEADS), _heads(k_ref, GDN_QK_HEADS), _heads(v_ref, GDN_V_HEADS),
                         _head_cols(gc_ref[...]), _head_cols(be_ref[...]))
        dqs, dks, dvs, dgs, dbs = vjp((_heads(du_ref, GDN_V_HEADS), _heads(dw_ref, GDN_V_HEADS), [dat_ref[0, h] for h in VH]))
        for p in range(GDN_QK_HEADS):
            dqkv_ref[:, p * HD:(p + 1) * HD] = dqs[p] + dqs_ref[:, p * HD:(p + 1) * HD]
            dqkv_ref[:, GDN_QK_W + p * HD:GDN_QK_W + (p + 1) * HD] = dks[p] + dks_ref[:, p * HD:(p + 1) * HD]
        for h in VH:
            dqkv_ref[:, 2 * GDN_QK_W + h * HD:2 * GDN_QK_W + (h + 1) * HD] = dvs[h]
        dgc_ref[...] = _head_lanes(dgs) + dgs_ref[...]
        dbe_ref[...] = _head_lanes(dbs)

    qs, ks, vs, g1, wide, sq, _ = _gdn_specs(NC)
    dqs = pl.BlockSpec((CHUNK, GDN_QK_W), lambda n: (n, 0))
    return pl.pallas_call(
        body, name=name, grid=(NC,),
        in_specs=[qs, ks, vs, g1, g1, sq, wide, wide, sq, dqs, dqs, g1],
        out_specs=[pl.BlockSpec((CHUNK, GDN_CONV), lambda n: (n, 0)), g1, g1],
        out_shape=[SDS((S, GDN_CONV), f32), SDS((S, 128), f32), SDS((S, 128), f32)],
        compiler_params=_cp(("arbitrary",)),
    )(qkvc, qkvc, qkvc, gc, beta, Ts, du, dw, dattn, dq_s, dk_s, dgc_s)


def _gated_norm(o, z, nw):
    parts = []
    for h in range(GDN_V_HEADS):
        oh = o[:, h * HD:(h + 1) * HD]
        r = lax.rsqrt(jnp.mean(oh * oh, axis=-1, keepdims=True) + EPS)
        parts.append((oh * r * nw) * _silu(z[:, h * HD:(h + 1) * HD]))
    return jnp.concatenate(parts, axis=1)


def gdn_out_fwd(o, proj, nw, W, x, gate, name):
    S, D = x.shape
    tm = _rows(S, 256)

    def body(o_ref, z_ref, nw_ref, w_ref, x_ref, g_ref, xn_ref, y_ref, og_ref):
        og = _gated_norm(o_ref[...], z_ref[...], nw_ref[...]).astype(bf16)
        y = jnp.dot(og, w_ref[...], preferred_element_type=f32)
        og_ref[...] = og
        y_ref[...] = y
        xn_ref[...] = x_ref[...] + g_ref[...] * y

    row = pl.BlockSpec((tm, D), lambda i: (i, 0))
    wide = pl.BlockSpec((tm, GDN_V_W), lambda i: (i, 0))
    return pl.pallas_call(
        body, name=name, grid=(S // tm,),
        in_specs=[wide, pl.BlockSpec((tm, GDN_V_W), lambda i: (i, 2)), pl.BlockSpec((1, HD), lambda i: (0, 0)),
                  pl.BlockSpec((GDN_V_W, D), lambda i: (0, 0)), row, pl.BlockSpec((1, D), lambda i: (0, 0))],
        out_specs=[row, row, wide],
        out_shape=[SDS((S, D), f32), SDS((S, D), f32), SDS((S, GDN_V_W), bf16)],
        compiler_params=_cp(("arbitrary",)),
    )(o, proj, nw, W, x, gate)


def gdn_out_bwd(dxn, y, gate, o, proj, nw, W, name):
    S, D = dxn.shape
    tm = _rows(S, 256)

    def body(dx_ref, y_ref, g_ref, o_ref, z_ref, nw_ref, w_ref, dy_ref, dg_ref, do_ref, dz_ref, dnw_ref):
        @pl.when(pl.program_id(0) == 0)
        def _():
            dg_ref[...] = jnp.zeros_like(dg_ref)
            dnw_ref[...] = jnp.zeros_like(dnw_ref)

        dx = dx_ref[...]
        dy = dx * g_ref[...]
        dy_ref[...] = dy
        dg_ref[...] += jnp.sum(dx * y_ref[...], axis=0, keepdims=True)
        dog = _nt(dy, w_ref[...])
        _, vjp = jax.vjp(_gated_norm, o_ref[...], z_ref[...], nw_ref[...])
        do, dz, dnw = vjp(dog)
        do_ref[...] = do
        dz_ref[...] = dz
        dnw_ref[...] += dnw

    row = pl.BlockSpec((tm, D), lambda i: (i, 0))
    wide = pl.BlockSpec((tm, GDN_V_W), lambda i: (i, 0))
    vecd = pl.BlockSpec((1, D), lambda i: (0, 0))
    vech = pl.BlockSpec((1, HD), lambda i: (0, 0))
    return pl.pallas_call(
        body, name=name, grid=(S // tm,),
        in_specs=[row, row, vecd, wide, pl.BlockSpec((tm, GDN_V_W), lambda i: (i, 2)), vech, pl.BlockSpec((GDN_V_W, D), lambda i: (0, 0))],
        out_specs=[row, vecd, wide, pl.BlockSpec((tm, GDN_V_W), lambda i: (i, 2)), vech],
        out_shape=[SDS((S, D), f32), SDS((1, D), f32), SDS((S, GDN_V_W), f32), SDS((S, GDN_IN_PAD), f32), SDS((1, HD), f32)],
        compiler_params=_cp(("arbitrary",)),
    )(dxn, y, gate, o, proj, nw, W)


def _rms_w(x, w):
    return (x * lax.rsqrt(jnp.mean(x * x, axis=-1, keepdims=True) + EPS)) * w


def _split3(c):
    hi = c.astype(bf16).astype(f32)
    r1 = c - hi
    mid = r1.astype(bf16).astype(f32)
    lo = (r1 - mid).astype(bf16).astype(f32)
    return hi, mid, lo


_FOX_F_BLK = 4 * FOX_W // 128


def fox_prep_fwd(proj, f_bias, qn_w, kn_w, name):
    S = proj.shape[0]
    tm = _rows(S, 256)

    def body(q_ref, k_ref, v_ref, f_ref, fb_ref, qw_ref, kw_ref, Q_ref, K_ref, V_ref, carry):
        @pl.when(pl.program_id(0) == 0)
        def _():
            carry[...] = jnp.zeros_like(carry)

        ii = lax.broadcasted_iota(jnp.int32, (tm, tm), 0)
        jj = lax.broadcasted_iota(jnp.int32, (tm, tm), 1)
        lf = jax.nn.log_sigmoid(f_ref[...] + fb_ref[...])
        cum = _hdot((ii >= jj).astype(f32), lf) + carry[...]
        carry[...] = cum[tm - 1:tm, :]
        lane = lax.broadcasted_iota(jnp.int32, (tm, FOX_D), 1)
        q, k, v = q_ref[...], k_ref[...], v_ref[...]
        for h in range(FOX_H):
            sl = slice(h * FOX_D, (h + 1) * FOX_D)
            hi, mid, lo = _split3(cum[:, h:h + 1])
            qn = _rms_w(q[:, sl], qw_ref[...]) * FOX_D ** -0.5
            kn = _rms_w(k[:, sl], kw_ref[...])
            eq = jnp.where(lane == 0, hi, jnp.where(lane == 1, mid, jnp.where(lane == 2, lo, jnp.where(lane < 6, 1.0, 0.0))))
            ek = jnp.where(lane < 3, 1.0, jnp.where(lane == 3, -hi, jnp.where(lane == 4, -mid, jnp.where(lane == 5, -lo, 0.0))))
            ev = jnp.where(lane == 0, 1.0, 0.0)
            Q_ref[:, h * 128:(h + 1) * 128] = jnp.concatenate([qn, eq], axis=1).astype(bf16)
            K_ref[:, h * 128:(h + 1) * 128] = jnp.concatenate([kn, ek], axis=1).astype(bf16)
            V_ref[:, h * 128:(h + 1) * 128] = jnp.concatenate([v[:, sl], ev], axis=1).astype(bf16)

    def colblk(c):
        return pl.BlockSpec((tm, FOX_W), lambda i: (i, c))

    pad = pl.BlockSpec((tm, FOX_PW), lambda i: (i, 0))
    return pl.pallas_call(
        body, name=name, grid=(S // tm,),
        in_specs=[colblk(0), colblk(1), colblk(2), pl.BlockSpec((tm, 128), lambda i: (i, _FOX_F_BLK)),
                  pl.BlockSpec((1, 128), lambda i: (0, 0)), pl.BlockSpec((1, FOX_D), lambda i: (0, 0)), pl.BlockSpec((1, FOX_D), lambda i: (0, 0))],
        out_specs=[pad, pad, pad],
        out_shape=[SDS((S, FOX_PW), bf16)] * 3,
        scratch_shapes=[pltpu.VMEM((1, 128), f32)],
        compiler_params=_cp(("arbitrary",)),
    )(proj, proj, proj, proj, f_bias, qn_w, kn_w)


def fox_prep_bwd(proj, f_bias, qn_w, kn_w, dQ, dK, dV, dz, name):
    S = proj.shape[0]
    tm = _rows(S, 256)
    NB = S // tm

    def body(q_ref, k_ref, f_ref, fb_ref, qw_ref, kw_ref, dQ_ref, dK_ref, dV_ref, dz_ref,
             dp_ref, dfb_ref, dqw_ref, dkw_ref, carry):
        @pl.when(pl.program_id(0) == 0)
        def _():
            carry[...] = jnp.zeros_like(carry)
            dfb_ref[...] = jnp.zeros_like(dfb_ref)
            dqw_ref[...] = jnp.zeros_like(dqw_ref)
            dkw_ref[...] = jnp.zeros_like(dkw_ref)

        q, k = q_ref[...], k_ref[...]
        lane128 = lax.broadcasted_iota(jnp.int32, (tm, 128), 1)
        dcum = jnp.zeros((tm, 128), f32)
        dqs, dks, dvs = [], [], []
        dqw = jnp.zeros((1, FOX_D), f32)
        dkw = jnp.zeros((1, FOX_D), f32)
        for h in range(FOX_H):
            sl = slice(h * FOX_D, (h + 1) * FOX_D)
            dQh = dQ_ref[:, h * 128:(h + 1) * 128]
            dKh = dK_ref[:, h * 128:(h + 1) * 128]
            _, vq = jax.vjp(lambda a, w: _rms_w(a, w) * FOX_D ** -0.5, q[:, sl], qw_ref[...])
            dqh, dw1 = vq(dQh[:, 0:FOX_D])
            _, vk = jax.vjp(_rms_w, k[:, sl], kw_ref[...])
            dkh, dw2 = vk(dKh[:, 0:FOX_D])
            dqs.append(dqh)
            dks.append(dkh)
            dvs.append(dV_ref[:, h * 128:h * 128 + FOX_D])
            dqw = dqw + dw1
            dkw = dkw + dw2
            dcum = dcum + jnp.where(lane128 == h, dQh[:, FOX_D:FOX_D + 1] - dKh[:, FOX_D + 3:FOX_D + 4], 0.0)
        dp_ref[:, 0:FOX_W] = jnp.concatenate(dqs, axis=1)
        dp_ref[:, FOX_W:2 * FOX_W] = jnp.concatenate(dks, axis=1)
        dp_ref[:, 2 * FOX_W:3 * FOX_W] = jnp.concatenate(dvs, axis=1)
        dp_ref[:, 3 * FOX_W:4 * FOX_W] = dz_ref[...]
        ii = lax.broadcasted_iota(jnp.int32, (tm, tm), 0)
        jj = lax.broadcasted_iota(jnp.int32, (tm, tm), 1)
        dlf = _hdot((ii <= jj).astype(f32), dcum) + carry[...]
        carry[...] += jnp.sum(dcum, axis=0, keepdims=True)
        df = dlf * jax.nn.sigmoid(-(f_ref[...] + fb_ref[...]))
        dp_ref[:, 4 * FOX_W:FOX_IN_PAD] = df
        dfb_ref[...] += jnp.sum(df, axis=0, keepdims=True)
        dqw_ref[...] += dqw
        dkw_ref[...] += dkw

    rv = lambda i: NB - 1 - i

    def colblk(c):
        return pl.BlockSpec((tm, FOX_W), lambda i: (rv(i), c))

    pad = pl.BlockSpec((tm, FOX_PW), lambda i: (rv(i), 0))
    cmp_ = pl.BlockSpec((tm, FOX_W), lambda i: (rv(i), 0))
    v128 = pl.BlockSpec((1, 128), lambda i: (0, 0))
    v64 = pl.BlockSpec((1, FOX_D), lambda i: (0, 0))
    return pl.pallas_call(
        body, name=name, grid=(NB,),
        in_specs=[colblk(0), colblk(1), pl.BlockSpec((tm, 128), lambda i: (rv(i), _FOX_F_BLK)), v128, v64, v64, pad, pad, pad, cmp_],
        out_specs=[pl.BlockSpec((tm, FOX_IN_PAD), lambda i: (rv(i), 0)), v128, v64, v64],
        out_shape=[SDS((S, FOX_IN_PAD), f32), SDS((1, 128), f32), SDS((1, FOX_D), f32), SDS((1, FOX_D), f32)],
        scratch_shapes=[pltpu.VMEM((1, 128), f32)],
        compiler_params=_cp(("arbitrary",)),
    )(proj, proj, proj, f_bias, qn_w, kn_w, dQ, dK, dV, dz)


FOX_HB = 2


def _diag_mask(t):
    return lax.broadcasted_iota(jnp.int32, (t, t), 1) <= lax.broadcasted_iota(jnp.int32, (t, t), 0)


def fox_attn_fwd(Q, K, V, name):
    S = Q.shape[0]
    t = _rows(S, 512)

    HB = FOX_HB
    HS = [slice(h * 128, (h + 1) * 128) for h in range(HB)]

    def body(q_ref, k_ref, v_ref, o_ref, m_sc, acc_sc, s_sc):
        i = pl.program_id(1)
        qs = [q_ref[:, sl] for sl in HS]
        m_sc[...] = jnp.full_like(m_sc, NEG)
        acc_sc[...] = jnp.zeros_like(acc_sc)

        def scores(j):
            j0 = pl.multiple_of(j * t, t)
            return [_nt(qs[h], k_ref[pl.ds(j0, t), HS[h]]) for h in range(HB)]

        def tile(j, diag):
            j0 = pl.multiple_of(j * t, t)
            ss = [s_sc[h] for h in range(HB)]
            if diag:
                ss = [jnp.where(_diag_mask(t), s, NEG) for s in ss]
            else:
                nxt = scores(j + 1)
            ms = [m_sc[h] for h in range(HB)]
            m_new = [jnp.maximum(ms[h], jnp.max(ss[h], axis=1, keepdims=True)) for h in range(HB)]
            ps = [jnp.exp(ss[h] - m_new[h]) for h in range(HB)]
            pv = [_nn(ps[h], v_ref[pl.ds(j0, t), HS[h]]) for h in range(HB)]
            for h in range(HB):
                acc_sc[h] = acc_sc[h] * jnp.exp(ms[h] - m_new[h]) + pv[h]
                m_sc[h] = m_new[h]
                if not diag:
                    s_sc[h] = nxt[h]

        def off_diag(j, c):
            tile(j, False)
            return c

        first = scores(0)
        for h in range(HB):
            s_sc[h] = first[h]
        lax.fori_loop(0, i, off_diag, 0)
        tile(i, True)
        lane = lax.broadcasted_iota(jnp.int32, (t, 128), 1)
        for h in range(HB):
            acc = acc_sc[h]
            l = acc[:, FOX_D:FOX_D + 1]
            o_ref[:, HS[h]] = jnp.where(lane == FOX_D, m_sc[h] + jnp.log(l), acc / l)

    blk = pl.BlockSpec((t, HB * 128), lambda h, i: (i, h))
    seq = pl.BlockSpec((S, HB * 128), lambda h, i: (0, h))
    return pl.pallas_call(
        body, name=name, grid=(FOX_H // HB, S // t),
        in_specs=[blk, seq, seq], out_specs=blk, out_shape=SDS((S, FOX_PW), f32),
        scratch_shapes=[pltpu.VMEM((HB, t, 1), f32), pltpu.VMEM((HB, t, 128), f32), pltpu.VMEM((HB, t, t), f32)],
        compiler_params=_cp(("arbitrary", "arbitrary")),
    )(Q, K, V)


def fox_attn_bwd(Q, K, V, dO, O, name):
    S = Q.shape[0]
    t = _rows(S, 512)
    nq = S // t

    HB = FOX_HB
    HS = [slice(h * 128, (h + 1) * 128) for h in range(HB)]

    def body(k_ref, v_ref, q_ref, do_ref, o_ref, dq_ref, dk_ref, dv_ref):
        j = pl.program_id(1)

        @pl.when(j == 0)
        def _():
            dq_ref[...] = jnp.zeros_like(dq_ref)

        dk_ref[...] = jnp.zeros_like(dk_ref)
        dv_ref[...] = jnp.zeros_like(dv_ref)
        ks = [k_ref[:, sl] for sl in HS]
        vs = [v_ref[:, sl] for sl in HS]

        def tile(i, diag):
            i0 = pl.multiple_of(i * t, t)
            R = range(HB)
            qs = [q_ref[pl.ds(i0, t), HS[h]] for h in R]
            dos = [do_ref[pl.ds(i0, t), HS[h]] for h in R]
            ss = [_nt(qs[h], ks[h]) - o_ref[pl.ds(i0, t), h * 128 + FOX_D:h * 128 + FOX_D + 1] for h in R]
            if diag:
                ss = [jnp.where(_diag_mask(t), s, NEG) for s in ss]
            ps = [jnp.exp(s) for s in ss]
            dps = [_nt(dos[h], vs[h]) for h in R]
            dvs = [_tn(ps[h], dos[h]) for h in R]
            dss = [(ps[h] * dps[h]).astype(bf16) for h in R]
            dks = [_tn(dss[h], qs[h]) for h in R]
            dqs = [_nn(dss[h], ks[h]) for h in R]
            for h in R:
                dv_ref[:, HS[h]] += dvs[h]
                dk_ref[:, HS[h]] += dks[h]
                dq_ref[pl.ds(i0, t), HS[h]] += dqs[h]

        tile(j, True)

        def off_diag(i, c):
            tile(i, False)
            return c

        lax.fori_loop(j + 1, nq, off_diag, 0)

    blk = pl.BlockSpec((t, HB * 128), lambda h, j: (j, h))
    seq = pl.BlockSpec((S, HB * 128), lambda h, j: (0, h))
    return pl.pallas_call(
        body, name=name, grid=(FOX_H // HB, nq),
        in_specs=[blk, blk, seq, seq, seq], out_specs=[seq, blk, blk],
        out_shape=[SDS((S, FOX_PW), f32)] * 3,
        compiler_params=_cp(("arbitrary", "arbitrary")),
    )(K, V, Q, dO, O)


def fox_out_fwd(O, proj, W, x, gate, name):
    S, D = x.shape
    tm = _rows(S, 256)

    def body(o_ref, z_ref, w_ref, x_ref, g_ref, xn_ref, y_ref, og_ref):
        z = z_ref[...]
        og = jnp.concatenate([o_ref[:, h * 128:h * 128 + FOX_D] * _silu(z[:, h * FOX_D:(h + 1) * FOX_D]) for h in range(FOX_H)],
                             axis=1).astype(bf16)
        y = jnp.dot(og, w_ref[...], preferred_element_type=f32)
        og_ref[...] = og
        y_ref[...] = y
        xn_ref[...] = x_ref[...] + g_ref[...] * y

    row = pl.BlockSpec((tm, D), lambda i: (i, 0))
    cmp_ = pl.BlockSpec((tm, FOX_W), lambda i: (i, 0))
    return pl.pallas_call(
        body, name=name, grid=(S // tm,),
        in_specs=[pl.BlockSpec((tm, FOX_PW), lambda i: (i, 0)), pl.BlockSpec((tm, FOX_W), lambda i: (i, 3)),
                  pl.BlockSpec((FOX_W, D), lambda i: (0, 0)), row, pl.BlockSpec((1, D), lambda i: (0, 0))],
        out_specs=[row, row, cmp_],
        out_shape=[SDS((S, D), f32), SDS((S, D), f32), SDS((S, FOX_W), bf16)],
        compiler_params=_cp(("arbitrary",)),
    )(O, proj, W, x, gate)


def fox_out_bwd(dxn, y, gate, O, proj, W, name):
    S, D = dxn.shape
    tm = _rows(S, 256)

    def body(dx_ref, y_ref, g_ref, o_ref, z_ref, w_ref, dy_ref, dg_ref, dO_ref, dz_ref):
        @pl.when(pl.program_id(0) == 0)
        def _():
            dg_ref[...] = jnp.zeros_like(dg_ref)

        dx = dx_ref[...]
        dy = dx * g_ref[...]
        dy_ref[...] = dy
        dg_ref[...] += jnp.sum(dx * y_ref[...], axis=0, keepdims=True)
        dog = _nt(dy, w_ref[...])
        z = z_ref[...]
        lane = lax.broadcasted_iota(jnp.int32, (tm, FOX_D), 1)
        dzs = []
        for h in range(FOX_H):
            sl = slice(h * FOX_D, (h + 1) * FOX_D)
            zh = z[:, sl]
            sg = jax.nn.sigmoid(zh)
            oh = o_ref[:, h * 128:h * 128 + FOX_D]
            doh = dog[:, sl] * (zh * sg)
            delta = jnp.sum(doh * oh, axis=1, keepdims=True)
            dO_ref[:, h * 128:(h + 1) * 128] = jnp.concatenate([doh, jnp.where(lane == 0, -delta, 0.0)], axis=1).astype(bf16)
            dzs.append(dog[:, sl] * oh * (sg * (1.0 + zh * (1.0 - sg))))
        dz_ref[...] = jnp.concatenate(dzs, axis=1)

    row = pl.BlockSpec((tm, D), lambda i: (i, 0))
    vecd = pl.BlockSpec((1, D), lambda i: (0, 0))
    pad = pl.BlockSpec((tm, FOX_PW), lambda i: (i, 0))
    return pl.pallas_call(
        body, name=name, grid=(S // tm,),
        in_specs=[row, row, vecd, pad, pl.BlockSpec((tm, FOX_W), lambda i: (i, 3)), pl.BlockSpec((FOX_W, D), lambda i: (0, 0))],
        out_specs=[row, vecd, pad, pl.BlockSpec((tm, FOX_W), lambda i: (i, 0))],
        out_shape=[SDS((S, D), f32), SDS((1, D), f32), SDS((S, FOX_PW), bf16), SDS((S, FOX_W), f32)],
        compiler_params=_cp(("arbitrary",)),
    )(dxn, y, gate, O, proj, W)


def final_loss(x, fw, target, name):
    S, D = x.shape
    tm = _rows(S, 512)

    def body(x_ref, w_ref, t_ref, l_ref, dx_ref, dw_ref):
        @pl.when(pl.program_id(0) == 0)
        def _():
            l_ref[...] = jnp.zeros_like(l_ref)
            dw_ref[...] = jnp.zeros_like(dw_ref)

        out, vjp = jax.vjp(_rms_w, x_ref[...], w_ref[...])
        err = out - t_ref[...]
        l_ref[...] += 0.5 * jnp.sum(jnp.sum(err * err, axis=1, keepdims=True) * (1.0 / D), axis=0, keepdims=True)
        dx, dw = vjp(err * (1.0 / D))
        dx_ref[...] = dx
        dw_ref[...] += dw

    row = pl.BlockSpec((tm, D), lambda i: (i, 0))
    vec = pl.BlockSpec((1, D), lambda i: (0, 0))
    return pl.pallas_call(
        body, name=name, grid=(S // tm,),
        in_specs=[row, vec, row], out_specs=[pl.BlockSpec((1, 128), lambda i: (0, 0)), row, vec],
        out_shape=[SDS((1, 128), f32), SDS((S, D), f32), SDS((1, D), f32)],
        compiler_params=_cp(("arbitrary",)),
    )(x, fw, target)


def ada_fwd(c_all, ada_w, name):
    L, D, n = ada_w.shape

    def body(c_ref, w_ref, o_ref):
        cond = jnp.concatenate([_silu(c_ref[...]), jnp.zeros((8, D), f32)], axis=0)
        o_ref[0] = _nn(cond, w_ref[0])[0:8]

    return pl.pallas_call(
        body, name=name, grid=(L,),
        in_specs=[pl.BlockSpec((NDEV, D), lambda l: (0, 0)), pl.BlockSpec((1, D, n), lambda l: (l, 0, 0))],
        out_specs=pl.BlockSpec((1, NDEV, n), lambda l: (l, 0, 0)),
        out_shape=SDS((L, NDEV, n), f32),
        compiler_params=_cp(("arbitrary",)),
    )(c_all, ada_w)


def ada_grad(c_all, dmod, name):
    L, _, n = dmod.shape
    D = c_all.shape[1]

    def body(c_ref, d_ref, o_ref):
        cond = jnp.concatenate([_silu(c_ref[...]), jnp.zeros((8, D), f32)], axis=0)
        dm = jnp.concatenate([d_ref[0], jnp.zeros((8, n), f32)], axis=0)
        o_ref[0] = _tn(cond, dm)

    return pl.pallas_call(
        body, name=name, grid=(L,),
        in_specs=[pl.BlockSpec((NDEV, D), lambda l: (0, 0)), pl.BlockSpec((1, NDEV, n), lambda l: (l, 0, 0))],
        out_specs=pl.BlockSpec((1, D, n), lambda l: (l, 0, 0)),
        out_shape=SDS((L, D, n), f32),
        compiler_params=_cp(("arbitrary",)),
    )(c_all, dmod)


def reduce_adam(parts, w, m, v, tr, name):
    n, R, C = parts.shape
    c1 = 1.0 / (1.0 - ADAM_B1 ** ADAM_STEP)
    c2 = 1.0 / (1.0 - ADAM_B2 ** ADAM_STEP)

    def body(p_ref, w_ref, m_ref, v_ref, g_ref, d_ref, nm_ref, nv_ref):
        g = p_ref[0].astype(f32)
        for s in range(1, n):
            g = g + p_ref[s].astype(f32)
        nm = ADAM_B1 * m_ref[...] + (1.0 - ADAM_B1) * g
        nv = ADAM_B2 * v_ref[...] + (1.0 - ADAM_B2) * (g * g)
        g_ref[...] = g
        nm_ref[...] = nm
        nv_ref[...] = nv
        d_ref[...] = -ADAM_LR * ((nm * c1) / (jnp.sqrt(nv * c2) + ADAM_EPS) + ADAM_WD * w_ref[...])

    blk = pl.BlockSpec((tr, C), lambda i: (i, 0))
    return pl.pallas_call(
        body, name=name, grid=(R // tr,),
        in_specs=[pl.BlockSpec((n, tr, C), lambda i: (0, i, 0)), blk, blk, blk],
        out_specs=[blk] * 4, out_shape=[SDS((R, C), f32)] * 4,
        compiler_params=_cp(("arbitrary",)),
    )(parts, w, m, v)


def all_gather(xs, name):
    n = len(xs)

    def body(*refs):
        x_refs, out_refs = refs[:n], refs[n:2 * n]
        send_sems, recv_sems, local_sems = refs[2 * n:]
        x_, y_, c_ = _my_pos()
        me, sibling = (x_, y_, c_), (x_, y_, 1 - c_)
        chips = [(1 - x_, y_), (x_, 1 - y_), (1 - x_, 1 - y_)]

        def rows(a, px, py, pc):
            return out_refs[a].at[4 * px + 2 * py + pc]

        def copy(a, k, block, to, own=False):
            return pltpu.make_async_remote_copy(
                src_ref=x_refs[a] if own else rows(a, *block), dst_ref=rows(a, *block),
                send_sem=send_sems.at[k, a], recv_sem=recv_sems.at[k, a], device_id=to, device_id_type=pl.DeviceIdType.MESH)

        mine = [pltpu.make_async_copy(x_refs[a], rows(a, *me), local_sems.at[a]) for a in range(n)]
        for cp in mine:
            cp.start()
        first = []
        for a in range(n):
            first.append(copy(a, 0, me, sibling, own=True))
            first += [copy(a, 1 + j, me, (*chip, c_), own=True) for j, chip in enumerate(chips)]
        for cp in first:
            cp.start()
        passed = []
        for j, chip in enumerate(chips):
            for a in range(n):
                copy(a, 1 + j, (*chip, c_), me).wait_recv()
                cp = copy(a, 4 + j, (*chip, c_), sibling)
                cp.start()
                passed.append(cp)
        for a in range(n):
            copy(a, 0, sibling, me).wait_recv()
            for j, chip in enumerate(chips):
                copy(a, 4 + j, (*chip, 1 - c_), me).wait_recv()
        for cp in first + passed:
            cp.wait_send()
        for cp in mine:
            cp.wait()

    any_ = pl.BlockSpec(memory_space=pl.ANY)
    return pl.pallas_call(
        body, name=name, out_shape=[SDS((NDEV,) + x.shape, x.dtype) for x in xs],
        in_specs=[any_] * n, out_specs=[any_] * n,
        scratch_shapes=[pltpu.SemaphoreType.DMA((7, n)), pltpu.SemaphoreType.DMA((7, n)), pltpu.SemaphoreType.DMA((n,))],
    )(*xs)


GDN_COLS = ((0, GDN_CONV + GDN_V_W, 0), (GDN_CONV + GDN_V_W, GDN_CONV + GDN_V_W + 16, GDN_CONV + GDN_V_W),
            (GDN_CONV + GDN_V_W + 16, GDN_IN, GDN_CONV + GDN_V_W + 128))
FOX_COLS = ((0, FOX_IN, 0),)


def _col_pieces(d, per, cols):
    lo, hi = per * d, per * (d + 1)
    out = []
    for a, b, dst in cols:
        s, e = max(lo, a), min(hi, b)
        if s < e:
            out.append((s - lo, e - s, dst + s - a))
    return out


def cols_from_blocks(g, cols, n_out, name):
    _, L, R, C = g.shape
    tr = min(256, R)

    def body(g_ref, o_ref):
        o_ref[...] = jnp.zeros_like(o_ref)
        for d in range(NDEV):
            for off, ln, dst in _col_pieces(d, C, cols):
                o_ref[0, :, dst:dst + ln] = g_ref[d, 0, :, off:off + ln]

    return pl.pallas_call(
        body, name=name, grid=(L, R // tr),
        in_specs=[pl.BlockSpec((NDEV, 1, tr, C), lambda l, i: (0, l, i, 0))],
        out_specs=pl.BlockSpec((1, tr, n_out), lambda l, i: (l, i, 0)),
        out_shape=SDS((L, R, n_out), g.dtype),
        compiler_params=_cp(("arbitrary", "arbitrary")),
    )(g)


def blocks_from_cols(dw, C, cols, name):
    R, n_in = dw.shape
    tr = min(256, R)

    def body(x_ref, o_ref):
        for d in range(NDEV):
            for off, ln, src in _col_pieces(d, C, cols):
                o_ref[d, :, off:off + ln] = x_ref[:, src:src + ln].astype(bf16)

    return pl.pallas_call(
        body, name=name, grid=(R // tr,),
        in_specs=[pl.BlockSpec((tr, n_in), lambda i: (i, 0))],
        out_specs=pl.BlockSpec((NDEV, tr, C), lambda i: (0, i, 0)),
        out_shape=SDS((NDEV, R, C), bf16),
        compiler_params=_cp(("arbitrary",)),
    )(dw)


BIG = ("a_w_in", "a_conv_w", "a_w_out", "b_w_in", "b_w_out")
SMALL = ("norm_w", "ada_b", "a_A_log", "a_dt_bias", "a_norm_w", "b_f_bias", "b_qn_w", "b_kn_w", "final_norm_w")


def _pack_small(arrs):
    rows = []
    for a in arrs:
        fl = a.reshape(-1)
        pad = (-fl.shape[0]) % 128
        if pad:
            fl = jnp.concatenate([fl, jnp.zeros((pad,), fl.dtype)])
        rows.append(fl)
    flat = jnp.concatenate(rows)
    pad = (-flat.shape[0]) % (8 * 128)
    if pad:
        flat = jnp.concatenate([flat, jnp.zeros((pad,), flat.dtype)])
    return flat.reshape(-1, 128)


def _unpack(packed, shapes, align):
    flat = packed.reshape(-1)
    out, off = [], 0
    for shp in shapes:
        n = 1
        for d in shp:
            n *= d
        out.append(flat[off:off + n].reshape(shp))
        off += n + ((-n) % align)
    return out


def _full_from_gathered(g, shard_shape, axis):
    g = jnp.moveaxis(g, 0, axis)
    shp = list(shard_shape)
    shp[axis] *= NDEV
    return g.reshape(shp)


def _pad_lanes(v, n=128):
    v = v.reshape(1, -1)
    return jnp.concatenate([v, jnp.zeros((1, n - v.shape[1]), v.dtype)], axis=1)


def _with_side(res, side):
    return res if side is not None else (res, None)


def gdn_layer_fwd(x, mod, nw, weights, tag, side=None, side_intra=None):
    W_in, conv_w, A_log, dt_bias, a_nw, W_out = weights
    shift, scale, gate = mod
    (proj, h), got = _with_side(inproj_fwd(x, nw, scale, shift, W_in, GDN_TN, f"{tag}_inproj", side), side)
    qkvc = gdn_prep_fwd(proj, conv_w, f"{tag}_prep")
    gc, beta = gdn_gates_fwd(proj, A_log, dt_bias, f"{tag}_gates")
    (u, w, attn, Ts), got_i = _with_side(gdn_intra_fwd(qkvc, gc, beta, f"{tag}_intra", side_intra), side_intra)
    o, states = gdn_scan_fwd(qkvc, gc, u, w, attn, f"{tag}_scan")
    x_new, y, og = gdn_out_fwd(o, proj, a_nw, W_out, x, gate, f"{tag}_out")
    return x_new, (x, proj, h, qkvc, gc, beta, o, states, Ts, y, og, u, w, attn), (got, got_i)


def gdn_layer_bwd(dxn, saved, mod, nw, weights, tag, side_dwin=None, side_ibwd=None, side_sbwd=None):
    W_in, conv_w, A_log, dt_bias, a_nw, W_out = weights
    shift, scale, gate = mod
    x, proj, h, qkvc, gc, beta, o, states, Ts, y, og, u, w, attn = saved
    dy, dgate, do, dproj, da_nw = gdn_out_bwd(dxn, y, gate, o, proj, a_nw, W_out, f"{tag}_out_bwd")
    dW_out, = matmul_tn(og, dy, 512, f"{tag}_dwout")
    side0 = side_sbwd(dW_out) if side_sbwd is not None else None
    (dq_s, dk_s, dgc_s, du, dw, dattn), got0 = _with_side(
        gdn_scan_bwd(qkvc, gc, u, w, attn, states, do, f"{tag}_scan_bwd", side0), side0)
    dqkvc, dgc, dbeta = gdn_intra_bwd(qkvc, gc, beta, Ts, du, dw, dattn, dq_s, dk_s, dgc_s, f"{tag}_intra_bwd")
    dproj, dA_log, ddt = gdn_gates_bwd(proj, A_log, dt_bias, dgc, dbeta, dproj, f"{tag}_gates_bwd")
    dproj, dconv_w = gdn_prep_bwd(proj, conv_w, dqkvc, dproj, f"{tag}_prep_bwd")
    (dW_in,), got1 = _with_side(matmul_tn(h, dproj, GDN_TN, f"{tag}_dwin", side_dwin), side_dwin)
    side2 = side_ibwd(dW_in, dconv_w) if side_ibwd is not None else None
    (dx, dnw, dscale, dshift), got2 = _with_side(
        inproj_bwd_x(x, nw, scale, shift, W_in, dproj, dxn, GDN_TN, f"{tag}_inproj_bwd", side2), side2)
    grads = dict(norm_w=dnw, W_in=dW_in, conv_w=dconv_w, A_log=dA_log[:, :16], dt_bias=ddt[:, :16], a_nw=da_nw, W_out=dW_out,
                 dmod=jnp.concatenate([dshift, dscale, dgate], axis=1))
    return dx, grads, (got0, got1, got2)


def fox_layer_fwd(x, mod, nw, weights, tag, side=None):
    W_in, f_bias, qn_w, kn_w, W_out = weights
    shift, scale, gate = mod
    (proj, h), got = _with_side(inproj_fwd(x, nw, scale, shift, W_in, FOX_TN, f"{tag}_inproj", side), side)
    Q, K, V = fox_prep_fwd(proj, f_bias, qn_w, kn_w, f"{tag}_prep")
    O = fox_attn_fwd(Q, K, V, f"{tag}_attn")
    x_new, y, og = fox_out_fwd(O, proj, W_out, x, gate, f"{tag}_out")
    return x_new, (x, proj, h, Q, K, V, O, y, og), (got, None)


def fox_layer_bwd(dxn, saved, mod, nw, weights, tag, side_dwin=None):
    W_in, f_bias, qn_w, kn_w, W_out = weights
    shift, scale, gate = mod
    x, proj, h, Q, K, V, O, y, og = saved
    dy, dgate, dO, dz = fox_out_bwd(dxn, y, gate, O, proj, W_out, f"{tag}_out_bwd")
    dW_out, = matmul_tn(og, dy, 512, f"{tag}_dwout")
    dQ, dK, dV = fox_attn_bwd(Q, K, V, dO, O, f"{tag}_attn_bwd")
    dproj, dfb, dqw, dkw = fox_prep_bwd(proj, f_bias, qn_w, kn_w, dQ, dK, dV, dz, f"{tag}_prep_bwd")
    (dW_in,), got1 = _with_side(matmul_tn(h, dproj, FOX_TN, f"{tag}_dwin", side_dwin), side_dwin)
    dx, dnw, dscale, dshift = inproj_bwd_x(x, nw, scale, shift, W_in, dproj, dxn, FOX_TN, f"{tag}_inproj_bwd")
    grads = dict(norm_w=dnw, W_in=dW_in, f_bias=dfb[:, :16], qn_w=dqw, kn_w=dkw, W_out=dW_out,
                 dmod=jnp.concatenate([dshift, dscale, dgate], axis=1))
    return dx, grads, (None, got1, None)


class LocalPlan:
    def __init__(self, full):
        self.full = full

    def layer_weights(self, i):
        j, f = i // 2, self.full
        return (f["a_w_in"][j], f["a_w_out"][j], f["a_conv_w"][j]) if i % 2 == 0 else (f["b_w_in"][j], f["b_w_out"][j])

    def fwd_sides(self, i):
        return None, None

    def fwd_got(self, i, gots):
        pass

    def bwd_sides(self, i):
        return None, None, None

    def bwd_got(self, i, grads, gots):
        pass


def device_step(x, mod_all, norm_w, small, final_norm_w, target, plan):
    D = x.shape[1]
    mods = [(mod_all[i:i + 1, 0:D], mod_all[i:i + 1, D:2 * D], mod_all[i:i + 1, 2 * D:3 * D]) for i in range(4)]

    def weights(i):
        j = i // 2
        if i % 2 == 0:
            W_in, W_out, conv_w = plan.layer_weights(i)
            return (W_in, conv_w, _pad_lanes(small["a_A_log"][j]), _pad_lanes(small["a_dt_bias"][j]), small["a_norm_w"][j:j + 1], W_out)
        W_in, W_out = plan.layer_weights(i)
        return (W_in, _pad_lanes(small["b_f_bias"][j]), small["b_qn_w"][j:j + 1], small["b_kn_w"][j:j + 1], W_out)

    saved, wts = [], []
    for i in range(4):
        wts.append(weights(i))
        side, side_intra = plan.fwd_sides(i)
        if i % 2 == 0:
            x, sv, gots = gdn_layer_fwd(x, mods[i], norm_w[i:i + 1], wts[i], f"L{i}", side, side_intra)
        else:
            x, sv, gots = fox_layer_fwd(x, mods[i], norm_w[i:i + 1], wts[i], f"L{i}", side)
        plan.fwd_got(i, gots)
        saved.append(sv)
    loss, dx, dfw = final_loss(x, final_norm_w.reshape(1, D), target, "final_loss")
    lg = [None] * 4
    for i in reversed(range(4)):
        side_sbwd, side_dwin, side_ibwd = plan.bwd_sides(i)
        if i % 2 == 0:
            dx, lg[i], gots = gdn_layer_bwd(dx, saved[i], mods[i], norm_w[i:i + 1], wts[i], f"L{i}", side_dwin, side_ibwd, side_sbwd)
        else:
            dx, lg[i], gots = fox_layer_bwd(dx, saved[i], mods[i], norm_w[i:i + 1], wts[i], f"L{i}", side_dwin)
        plan.bwd_got(i, lg[i], gots)
    g = dict(
        norm_w=jnp.concatenate([lg[i]["norm_w"] for i in range(4)], axis=0),
        dmod=jnp.concatenate([lg[i]["dmod"] for i in range(4)], axis=0),
        a_w_in=[lg[i]["W_in"] for i in (0, 2)],
        a_conv_w=jnp.stack([lg[i]["conv_w"] for i in (0, 2)]),
        a_A_log=jnp.concatenate([lg[i]["A_log"] for i in (0, 2)], axis=0),
        a_dt_bias=jnp.concatenate([lg[i]["dt_bias"] for i in (0, 2)], axis=0),
        a_norm_w=jnp.concatenate([lg[i]["a_nw"] for i in (0, 2)], axis=0),
        a_w_out=[lg[i]["W_out"] for i in (0, 2)],
        b_w_in=[lg[i]["W_in"] for i in (1, 3)],
        b_f_bias=jnp.concatenate([lg[i]["f_bias"] for i in (1, 3)], axis=0),
        b_qn_w=jnp.concatenate([lg[i]["qn_w"] for i in (1, 3)], axis=0),
        b_kn_w=jnp.concatenate([lg[i]["kn_w"] for i in (1, 3)], axis=0),
        b_w_out=[lg[i]["W_out"] for i in (1, 3)],
        final_norm_w=dfw.reshape(-1),
    )
    return loss[0, 0], dx, g


class MeshPlan:
    def __init__(self, shards, w0, conv_full):
        self.shards = shards
        self.w = {0: w0}
        self.conv = conv_full
        self.recv = {}
        self.pending = {}
        self.names = [None, None, None]

    def layer_weights(self, i):
        return self.w[i]

    def _gather_side(self, names, j):
        out = []
        for n in names:
            sh = self.shards[n][j]
            out.append(sh.reshape(-1, sh.shape[-1]))
        return ("gather", out)

    def fwd_sides(self, i):
        if i == 0:
            return self._gather_side(("b_w_in", "b_w_out"), 0), self._gather_side(("a_w_in", "a_w_out"), 1)
        if i == 2:
            return self._gather_side(("b_w_in", "b_w_out"), 1), None
        return None, None

    def _set_weights(self, layer, got):
        g_in, g_out = got
        D = g_out.shape[-1]
        j = layer // 2
        if layer % 2 == 1:
            W_in = cols_from_blocks(g_in[:, None], FOX_COLS, FOX_IN_PAD, f"b_w_in_cols{j}")[0]
            self.w[layer] = (W_in, g_out.reshape(-1, D))
        else:
            W_in = cols_from_blocks(g_in[:, None], GDN_COLS, GDN_IN_PAD, f"a_w_in_cols{j}")[0]
            self.w[layer] = (W_in, g_out.reshape(-1, D), self.conv[j])

    def fwd_got(self, i, gots):
        got, got_intra = gots
        if got is not None:
            self._set_weights(i + 1, got)
        if got_intra is not None:
            self._set_weights(i + 2, got_intra)

    @staticmethod
    def _out_blocks(dW_out):
        return dW_out.astype(bf16).reshape(NDEV, -1, dW_out.shape[-1])

    def _in_blocks(self, name, j, dW_in):
        cols = GDN_COLS if name == "a_w_in" else FOX_COLS
        return blocks_from_cols(dW_in, self.shards[name].shape[-1], cols, f"{name}_blocks{j}")

    def bwd_sides(self, i):
        self.names = [None, None, None]
        side_dwin = None
        if i in self.pending:
            self.names[1], arrs = self.pending.pop(i)
            side_dwin = ("scatter", arrs)
        if i % 2 == 1:
            return None, side_dwin, None
        j = i // 2

        def side_ibwd(dW_in, dconv_w):
            self.names[2], arrs = [("a_w_in", j)], [self._in_blocks("a_w_in", j, dW_in)]
            if j == 0:
                conv = jnp.stack([dconv_w, self._dconv1])
                n = conv.shape[-1] // NDEV
                self.names[2].append(("a_conv_w", None))
                arrs.append(jnp.moveaxis(conv.reshape(2, 4, NDEV, n), 2, 0).reshape(NDEV, 8, n))
            return ("scatter", arrs)

        def side_sbwd(dW_out):
            self.names[0] = [("a_w_out", 0)]
            return ("scatter", [self._out_blocks(dW_out)])

        return (side_sbwd if j == 0 else None), side_dwin, side_ibwd

    def bwd_got(self, i, grads, gots):
        for names, got in zip(self.names, gots):
            if got is not None:
                self.recv.update(zip(names, got))
        if i % 2 == 1:
            j = i // 2
            self.pending[i - 1] = ([("b_w_in", j), ("b_w_out", j)], [self._in_blocks("b_w_in", j, grads["W_in"]), self._out_blocks(grads["W_out"])])
        elif i == 2:
            self.pending[1] = ([("a_w_out", 1)], [self._out_blocks(grads["W_out"])])
            self._dconv1 = grads["conv_w"]


def kernel(x, c, norm_w, ada_w, ada_b, a_w_in, a_conv_w, a_A_log, a_dt_bias, a_norm_w, a_w_out, b_w_in, b_f_bias, b_qn_w, b_kn_w, b_w_out, final_norm_w, loss_target, m_norm_w, m_ada_w, m_ada_b, m_a_w_in, m_a_conv_w, m_a_A_log, m_a_dt_bias, m_a_norm_w, m_a_w_out, m_b_w_in, m_b_f_bias, m_b_qn_w, m_b_kn_w, m_b_w_out, m_final_norm_w, v_norm_w, v_ada_w, v_ada_b, v_a_w_in, v_a_conv_w, v_a_A_log, v_a_dt_bias, v_a_norm_w, v_a_w_out, v_b_w_in, v_b_f_bias, v_b_qn_w, v_b_kn_w, v_b_w_out, v_final_norm_w):
    W = dict(norm_w=norm_w, ada_w=ada_w, ada_b=ada_b, a_w_in=a_w_in, a_conv_w=a_conv_w, a_A_log=a_A_log, a_dt_bias=a_dt_bias,
             a_norm_w=a_norm_w, a_w_out=a_w_out, b_w_in=b_w_in, b_f_bias=b_f_bias, b_qn_w=b_qn_w, b_kn_w=b_kn_w, b_w_out=b_w_out,
             final_norm_w=final_norm_w)
    M = dict(norm_w=m_norm_w, ada_w=m_ada_w, ada_b=m_ada_b, a_w_in=m_a_w_in, a_conv_w=m_a_conv_w, a_A_log=m_a_A_log,
             a_dt_bias=m_a_dt_bias, a_norm_w=m_a_norm_w, a_w_out=m_a_w_out, b_w_in=m_b_w_in, b_f_bias=m_b_f_bias, b_qn_w=m_b_qn_w,
             b_kn_w=m_b_kn_w, b_w_out=m_b_w_out, final_norm_w=m_final_norm_w)
    V = dict(norm_w=v_norm_w, ada_w=v_ada_w, ada_b=v_ada_b, a_w_in=v_a_w_in, a_conv_w=v_a_conv_w, a_A_log=v_a_A_log,
             a_dt_bias=v_a_dt_bias, a_norm_w=v_a_norm_w, a_w_out=v_a_w_out, b_w_in=v_b_w_in, b_f_bias=v_b_f_bias, b_qn_w=v_b_qn_w,
             b_kn_w=v_b_kn_w, b_w_out=v_b_w_out, final_norm_w=v_final_norm_w)
    S, D = x.shape[1], x.shape[2]
    me = 4 * lax.axis_index("x") + 2 * lax.axis_index("y") + lax.axis_index("c")
    small_shapes = [W[n].shape for n in SMALL]

    shards = {n: W[n].astype(bf16) for n in ("a_w_in", "a_w_out", "b_w_in", "b_w_out")}
    gath = all_gather([shards["a_w_in"][0], shards["a_w_out"][0], a_conv_w.reshape(8, -1), c.reshape(8, D // 8)], "gather_w0")
    conv_full = _full_from_gathered(gath[2].reshape((NDEV,) + a_conv_w.shape), a_conv_w.shape, 2)
    w0 = (cols_from_blocks(gath[0][:, None], GDN_COLS, GDN_IN_PAD, "a_w_in_cols0")[0], gath[1].reshape(-1, D), conv_full[0])
    plan = MeshPlan(shards, w0, conv_full)
    c_all = gath[3].reshape(NDEV, D)

    mod_part = ada_fwd(c_all, ada_w, "ada_fwd")
    n_ada = ada_w.shape[2]
    mod_g = all_gather([mod_part.reshape(4 * NDEV, n_ada)], "gather_mod")[0].reshape(NDEV, 4, NDEV, n_ada)
    mod_mine = lax.dynamic_index_in_dim(mod_g, me, axis=2, keepdims=False)
    mod_all = jnp.moveaxis(mod_mine, 0, 1).reshape(4, NDEV * n_ada) + ada_b

    loss, dx, g = device_step(x[0], mod_all, norm_w, W, final_norm_w, loss_target[0], plan)
    loss = lax.psum(loss, MESH_AXES)

    g_small = dict(g, ada_b=g["dmod"])
    sp = _pack_small([g_small[n] for n in SMALL])
    sp_all = all_gather([sp], "gather_small")[0]
    sw, sm, sv = (_pack_small([T[n] for n in SMALL]) for T in (W, M, V))
    sg, sd, snm, snv = (_unpack(t, small_shapes, 128) for t in reduce_adam(sp_all, sw, sm, sv, sp.shape[0], "adam_small"))

    off_b = 0
    for n, shp in zip(SMALL, small_shapes):
        if n == "ada_b":
            break
        cnt = 1
        for d in shp:
            cnt *= d
        off_b += cnt + ((-cnt) % 128)
    dmod_all = sp_all.reshape(NDEV, -1)[:, off_b:off_b + 4 * 3 * D].reshape(NDEV, 4, 3 * D)
    dmod_cols = lax.dynamic_slice_in_dim(dmod_all, me * n_ada, n_ada, axis=2)
    g_ada = ada_grad(c_all, jnp.moveaxis(dmod_cols, 0, 1), "ada_grad")
    r_ada = reduce_adam(g_ada.reshape(1, 4 * D, n_ada), *(T["ada_w"].reshape(4 * D, n_ada) for T in (W, M, V)), 512, "adam_ada")
    ag, ad, anm, anv = (t.reshape(ada_w.shape) for t in r_ada)

    big = {}
    for n in BIG:
        C = W[n].shape[-1]
        parts = plan.recv[(n, None)] if n == "a_conv_w" else jnp.stack([plan.recv[(n, 0)], plan.recv[(n, 1)]], axis=1).reshape(NDEV, -1, C)
        res = reduce_adam(parts, *(T[n].reshape(parts.shape[1:]) for T in (W, M, V)), min(256, parts.shape[1]), f"adam_{n}")
        big[n] = [t.reshape(W[n].shape) for t in res]

    outs = {}
    for idx, (k, sm_l, ada_t) in enumerate((("grad", sg, ag), ("delta", sd, ad), ("new_m", snm, anm), ("new_v", snv, anv))):
        d = dict(zip(SMALL, sm_l))
        d.update({n: big[n][idx] for n in BIG})
        d["ada_w"] = ada_t
        outs[k] = d
    order = ("norm_w", "ada_w", "ada_b", "a_w_in", "a_conv_w", "a_A_log", "a_dt_bias", "a_norm_w", "a_w_out", "b_w_in", "b_f_bias",
             "b_qn_w", "b_kn_w", "b_w_out", "final_norm_w")
    return (loss, dx[None], *[outs["grad"][n] for n in order], *[outs["delta"][n] for n in order],
            *[outs["new_m"][n] for n in order], *[outs["new_v"][n] for n in order])
```

```python
import functools

import jax
import jax.numpy as jnp
from jax import lax
from jax.experimental import pallas as pl
from jax.experimental.pallas import tpu as pltpu

f32 = jnp.float32
bf16 = jnp.bfloat16
SDS = jax.ShapeDtypeStruct

EPS = 1e-6
CHUNK = 64
HD = 128
GDN_QK_HEADS = 8
GDN_V_HEADS = 16
GDN_QK_W = GDN_QK_HEADS * HD
GDN_V_W = GDN_V_HEADS * HD
GDN_CONV = 2 * GDN_QK_W + GDN_V_W
GDN_IN = GDN_CONV + GDN_V_W + 2 * GDN_V_HEADS
GDN_IN_PAD = GDN_CONV + GDN_V_W + 256
GDN_TN = 640
FOX_H = 16
FOX_D = 64
FOX_W = FOX_H * FOX_D
FOX_IN = 4 * FOX_W + FOX_H
FOX_IN_PAD = 4 * FOX_W + 128
FOX_TN = 1408
FOX_PW = FOX_H * 128
NDEV = 8
MESH_AXES = ("x", "y", "c")
NEG = -1e30

ADAM_LR = 0.001
ADAM_B1 = 0.9
ADAM_B2 = 0.999
ADAM_EPS = 1e-08
ADAM_WD = 0.01
ADAM_STEP = 10

VMEM_LIMIT = 56 * 1024 * 1024


def _cp(sem=None):
    return pltpu.CompilerParams(dimension_semantics=sem, vmem_limit_bytes=VMEM_LIMIT)


def _bdot(a, b, dims):
    return lax.dot_general(a.astype(bf16), b.astype(bf16), (dims, ((), ())), preferred_element_type=f32)


def _nn(a, b):
    return _bdot(a, b, ((1,), (0,)))


def _nt(a, b):
    return _bdot(a, b, ((1,), (1,)))


def _tn(a, b):
    return _bdot(a, b, ((0,), (0,)))


def _hdot(a, b, dims=((1,), (0,))):
    return lax.dot_general(a, b, (dims, ((), ())), precision=lax.Precision.HIGHEST, preferred_element_type=f32)


def _split2(a):
    hi = a.astype(bf16)
    return hi, (a - hi.astype(f32)).astype(bf16)


def _dot3(a, b):
    (ah, al), (bh, bl) = a, b
    return (jnp.dot(ah, bh, preferred_element_type=f32) + jnp.dot(ah, bl, preferred_element_type=f32)
            + jnp.dot(al, bh, preferred_element_type=f32))


@jax.custom_vjp
def _mm(a, b):
    return _nn(a, b)


_mm.defvjp(lambda a, b: (_nn(a, b), (a, b)), lambda r, g: (_nt(g, r[1]), _tn(r[0], g)))


@jax.custom_vjp
def _mm_nt(a, b):
    return _nt(a, b)


_mm_nt.defvjp(lambda a, b: (_nt(a, b), (a, b)), lambda r, g: (_nn(g, r[1]), _tn(g, r[0])))


@jax.custom_vjp
def _mm_tn(a, b):
    return _tn(a, b)


_mm_tn.defvjp(lambda a, b: (_tn(a, b), (a, b)), lambda r, g: (_nt(r[1], g), _nn(r[0], g)))


def _silu(x):
    return x * jax.nn.sigmoid(x)


def _rms_mod(x, nw, scale, shift):
    r = lax.rsqrt(jnp.mean(x * x, axis=-1, keepdims=True) + EPS)
    return (x * r * nw) * (1.0 + scale) + shift


def _rows(S, want):
    return min(want, S)


def _my_pos():
    return lax.axis_index("x"), lax.axis_index("y"), lax.axis_index("c")


def _exchange_copies(kind, x_refs, out_refs, send_sems, recv_sems, local_sems):
    x_, y_, c_ = _my_pos()
    me = 4 * x_ + 2 * y_ + c_
    own = kind == "gather"
    cps = [pltpu.make_async_copy(x_refs[a] if own else x_refs[a].at[me], out_refs[a].at[me], local_sems.at[a])
           for a in range(len(x_refs))]
    for rel in range(1, NDEV):
        px = (x_ + ((rel >> 2) & 1)) % 2
        py = (y_ + ((rel >> 1) & 1)) % 2
        pc = (c_ + (rel & 1)) % 2
        for a in range(len(x_refs)):
            cps.append(pltpu.make_async_remote_copy(
                src_ref=x_refs[a] if own else x_refs[a].at[4 * px + 2 * py + pc], dst_ref=out_refs[a].at[me],
                send_sem=send_sems.at[rel - 1, a], recv_sem=recv_sems.at[rel - 1, a],
                device_id=(px, py, pc), device_id_type=pl.DeviceIdType.MESH))
    return cps


def _exchange_scratch(n):
    return [pltpu.SemaphoreType.DMA((NDEV - 1, n)), pltpu.SemaphoreType.DMA((NDEV - 1, n)), pltpu.SemaphoreType.DMA((n,))]


def _call(body, *, name, grid, in_specs, out_specs, out_shape, args, scratch=(), side=None):
    params = _cp(("arbitrary",) * len(grid))
    if side is None:
        return pl.pallas_call(body, name=name, grid=grid, in_specs=in_specs, out_specs=out_specs, out_shape=out_shape,
                              scratch_shapes=list(scratch), compiler_params=params)(*args)
    kind, xs = side
    n_in, n_out, n_scr, ns = len(in_specs), len(out_shape), len(scratch), len(xs)
    steps = 1
    for g in grid:
        steps *= g

    def wrapped(*refs):
        o0 = n_in + ns
        s0 = o0 + n_out + ns
        step = pl.program_id(0)
        for d in range(1, len(grid)):
            step = step * grid[d] + pl.program_id(d)

        def copies():
            return _exchange_copies(kind, refs[n_in:o0], refs[o0 + n_out:s0], *refs[s0 + n_scr:])

        @pl.when(step == 0)
        def _():
            for cp in copies():
                cp.start()

        body(*refs[:n_in], *refs[o0:o0 + n_out], *refs[s0:s0 + n_scr])

        @pl.when(step == steps - 1)
        def _():
            for cp in copies():
                cp.wait()

    any_ = pl.BlockSpec(memory_space=pl.ANY)
    side_shapes = [SDS((NDEV,) + x.shape if kind == "gather" else x.shape, x.dtype) for x in xs]
    outs = pl.pallas_call(wrapped, name=name, grid=grid, in_specs=list(in_specs) + [any_] * ns,
                          out_specs=list(out_specs) + [any_] * ns, out_shape=list(out_shape) + side_shapes,
                          scratch_shapes=list(scratch) + _exchange_scratch(ns), compiler_params=params)(*args, *xs)
    return outs[:n_out], outs[n_out:]


def inproj_fwd(x, nw, scale, shift, W, tn, name, side=None):
    S, D = x.shape
    N = W.shape[1]
    tm = _rows(S, 1024)

    def body(x_ref, nw_ref, sc_ref, sh_ref, w_ref, proj_ref, h_ref):
        @pl.when(pl.program_id(1) == 0)
        def _():
            h_ref[...] = _rms_mod(x_ref[...], nw_ref[...], sc_ref[...], sh_ref[...]).astype(bf16)

        proj_ref[...] = jnp.dot(h_ref[...], w_ref[...], preferred_element_type=f32)

    vec = pl.BlockSpec((1, D), lambda i, j: (0, 0))
    return _call(
        body, name=name, grid=(S // tm, N // tn),
        in_specs=[pl.BlockSpec((tm, D), lambda i, j: (i, 0)), vec, vec, vec, pl.BlockSpec((D, tn), lambda i, j: (0, j))],
        out_specs=[pl.BlockSpec((tm, tn), lambda i, j: (i, j)), pl.BlockSpec((tm, D), lambda i, j: (i, 0))],
        out_shape=[SDS((S, N), f32), SDS((S, D), bf16)], args=(x, nw, scale, shift, W), side=side)


def inproj_bwd_x(x, nw, scale, shift, W, dproj, dx_res, tn, name, side=None):
    S, D = x.shape
    N = W.shape[1]
    tm = _rows(S, 1024)
    nj = N // tn

    def body(x_ref, nw_ref, sc_ref, sh_ref, w_ref, dp_ref, dxr_ref, dx_ref, dnw_ref, dsc_ref, dsh_ref, acc):
        i, j = pl.program_id(0), pl.program_id(1)

        @pl.when(j == 0)
        def _():
            acc[...] = jnp.zeros_like(acc)

        @pl.when((i == 0) & (j == 0))
        def _():
            dnw_ref[...] = jnp.zeros_like(dnw_ref)
            dsc_ref[...] = jnp.zeros_like(dsc_ref)
            dsh_ref[...] = jnp.zeros_like(dsh_ref)

        acc[...] += _nt(dp_ref[...], w_ref[...])

        @pl.when(j == nj - 1)
        def _():
            _, vjp = jax.vjp(_rms_mod, x_ref[...], nw_ref[...], sc_ref[...], sh_ref[...])
            dx, dnw, dsc, dsh = vjp(acc[...])
            dx_ref[...] = dxr_ref[...] + dx
            dnw_ref[...] += dnw
            dsc_ref[...] += dsc
            dsh_ref[...] += dsh

    vec = pl.BlockSpec((1, D), lambda i, j: (0, 0))
    row = pl.BlockSpec((tm, D), lambda i, j: (i, 0))
    return _call(
        body, name=name, grid=(S // tm, nj),
        in_specs=[row, vec, vec, vec, pl.BlockSpec((D, tn), lambda i, j: (0, j)), pl.BlockSpec((tm, tn), lambda i, j: (i, j)), row],
        out_specs=[row, vec, vec, vec],
        out_shape=[SDS((S, D), f32), SDS((1, D), f32), SDS((1, D), f32), SDS((1, D), f32)],
        scratch=[pltpu.VMEM((tm, D), f32)], args=(x, nw, scale, shift, W, dproj, dx_res), side=side)


def matmul_tn(a, b, tn, name, side=None):
    S, K = a.shape
    N = b.shape[1]
    tm = _rows(S, 1024)
    ni = S // tm

    def body(a_ref, b_ref, o_ref):
        @pl.when(pl.program_id(1) == 0)
        def _():
            o_ref[...] = jnp.zeros_like(o_ref)

        o_ref[...] += _tn(a_ref[...], b_ref[...])

    return _call(
        body, name=name, grid=(N // tn, ni),
        in_specs=[pl.BlockSpec((tm, K), lambda j, i: (i, 0)), pl.BlockSpec((tm, tn), lambda j, i: (i, j))],
        out_specs=[pl.BlockSpec((K, tn), lambda j, i: (0, j))],
        out_shape=[SDS((K, N), f32)], args=(a, b), side=side)


def _conv_taps(xs, w, n_out):
    taps = []
    for j in range(4):
        s = 3 - j
        sh = xs if s == 0 else pltpu.roll(xs, s, axis=0)
        taps.append(sh[8:8 + n_out])
    conv = taps[0] * w[0] + taps[1] * w[1] + taps[2] * w[2] + taps[3] * w[3]
    return taps, conv


def _act_norm(conv, mul_norm, mul_plain):
    s = _silu(conv)
    r = lax.rsqrt(jnp.sum(s * s, axis=-1, keepdims=True) + EPS)
    return s * (mul_norm * r + mul_plain)


def _gdn_prep_mults(j):
    is_q = j < GDN_QK_HEADS
    is_k = (j >= GDN_QK_HEADS) & (j < 2 * GDN_QK_HEADS)
    mul_norm = jnp.where(is_q, HD ** -0.5, jnp.where(is_k, 1.0, 0.0)).astype(f32)
    mul_plain = jnp.where(is_q | is_k, 0.0, 1.0).astype(f32)
    return mul_norm, mul_plain


def gdn_prep_fwd(proj, conv_w, name):
    S = proj.shape[0]
    R = _rows(S, 512)

    def body(x_ref, w_ref, o_ref):
        mul_norm, mul_plain = _gdn_prep_mults(pl.program_id(0))
        w = [w_ref[j:j + 1, :] for j in range(4)]

        def piece(r, c):
            t0 = pl.multiple_of(r * R, R)
            cur = x_ref[pl.ds(t0, R), :]
            prev = x_ref[pl.ds(pl.multiple_of(jnp.maximum(t0 - 8, 0), 8), 8), :]
            prev = jnp.where(r == 0, 0.0, prev)
            _, conv = _conv_taps(jnp.concatenate([prev, cur], axis=0), w, R)
            o_ref[pl.ds(t0, R), :] = _act_norm(conv, mul_norm, mul_plain)
            return c

        lax.fori_loop(0, S // R, piece, 0)

    return pl.pallas_call(
        body, name=name, grid=(GDN_CONV // 128,),
        in_specs=[pl.BlockSpec((S, 128), lambda j: (0, j)), pl.BlockSpec((4, 128), lambda j: (0, j))],
        out_specs=pl.BlockSpec((S, 128), lambda j: (0, j)),
        out_shape=SDS((S, GDN_CONV), f32),
        compiler_params=_cp(("arbitrary",)),
    )(proj, conv_w)


def gdn_prep_bwd(proj, conv_w, dqkvc, dproj, name):
    S = proj.shape[0]
    R = _rows(S, 512)
    NP = S // R

    def body(x_ref, w_ref, dn_ref, _, dx_ref, dw_ref):
        mul_norm, mul_plain = _gdn_prep_mults(pl.program_id(0))
        w = [w_ref[j:j + 1, :] for j in range(4)]

        def piece(r, dw):
            t0 = pl.multiple_of(r * R, R)
            cur = x_ref[pl.ds(t0, R), :]
            prev = x_ref[pl.ds(pl.multiple_of(jnp.maximum(t0 - 8, 0), 8), 8), :]
            prev = jnp.where(r == 0, 0.0, prev)
            nxt0 = pl.multiple_of(jnp.minimum(t0 + R, S - 8), 8)
            nxt = x_ref[pl.ds(nxt0, 8), :]
            dn_cur = dn_ref[pl.ds(t0, R), :]
            dn_nxt = jnp.where(r == NP - 1, 0.0, dn_ref[pl.ds(nxt0, 8), :])
            xs = jnp.concatenate([prev, cur, nxt], axis=0)
            taps, conv = _conv_taps(xs, w, R + 8)
            dn = jnp.concatenate([dn_cur, dn_nxt], axis=0)
            _, vjp = jax.vjp(lambda c: _act_norm(c, mul_norm, mul_plain), conv)
            dxc = vjp(dn)[0]
            n = R + 8
            dx = dxc[0:R] * w[3]
            for j in range(3):
                s = 3 - j
                dx = dx + pltpu.roll(dxc, n - s, axis=0)[0:R] * w[j]
            dx_ref[pl.ds(t0, R), :] = dx
            return tuple(dw[j] + jnp.sum(dxc[0:R] * taps[j][0:R], axis=0, keepdims=True) for j in range(4))

        dw = lax.fori_loop(0, NP, piece, tuple(jnp.zeros((1, 128), f32) for _ in range(4)))
        for j in range(4):
            dw_ref[j:j + 1, :] = dw[j]

    col = pl.BlockSpec((S, 128), lambda j: (0, j))
    wsp = pl.BlockSpec((4, 128), lambda j: (0, j))
    return pl.pallas_call(
        body, name=name, grid=(GDN_CONV // 128,),
        in_specs=[col, wsp, col, pl.BlockSpec(memory_space=pl.ANY)], out_specs=[col, wsp],
        out_shape=[SDS(dproj.shape, f32), SDS((4, GDN_CONV), f32)],
        input_output_aliases={3: 0},
        compiler_params=_cp(("arbitrary",)),
    )(proj, conv_w, dqkvc, dproj)


def _chunk_tril(R):
    ii = lax.broadcasted_iota(jnp.int32, (R, R), 0)
    jj = lax.broadcasted_iota(jnp.int32, (R, R), 1)
    return ((ii // CHUNK == jj // CHUNK) & (ii >= jj)).astype(f32)


def _gdn_gates(b, a, A_log, dt_bias, tril):
    beta = jax.nn.sigmoid(b)
    g = -jnp.exp(A_log) * jax.nn.softplus(a + dt_bias)
    return _hdot(tril, g), beta


_GDN_B_BLK = (GDN_CONV + GDN_V_W) // 128
_GDN_A_BLK = _GDN_B_BLK + 1


def gdn_gates_fwd(proj, A_log, dt_bias, name):
    S = proj.shape[0]
    R = _rows(S, 512)

    def body(b_ref, a_ref, al_ref, dt_ref, gc_ref, be_ref):
        gc, be = _gdn_gates(b_ref[...], a_ref[...], al_ref[...], dt_ref[...], _chunk_tril(R))
        gc_ref[...] = gc
        be_ref[...] = be

    vec = pl.BlockSpec((1, 128), lambda i: (0, 0))
    blk = pl.BlockSpec((R, 128), lambda i: (i, 0))
    return pl.pallas_call(
        body, name=name, grid=(S // R,),
        in_specs=[pl.BlockSpec((R, 128), lambda i: (i, _GDN_B_BLK)), pl.BlockSpec((R, 128), lambda i: (i, _GDN_A_BLK)), vec, vec],
        out_specs=[blk, blk], out_shape=[SDS((S, 128), f32), SDS((S, 128), f32)],
        compiler_params=_cp(("arbitrary",)),
    )(proj, proj, A_log, dt_bias)


def gdn_gates_bwd(proj, A_log, dt_bias, dgc, dbeta, dproj, name):
    S = proj.shape[0]
    R = _rows(S, 512)

    def body(b_ref, a_ref, al_ref, dt_ref, dgc_ref, dbe_ref, _, dp_ref, dal_ref, ddt_ref):
        @pl.when(pl.program_id(0) == 0)
        def _():
            dal_ref[...] = jnp.zeros_like(dal_ref)
            ddt_ref[...] = jnp.zeros_like(ddt_ref)

        tril = _chunk_tril(R)
        _, vjp = jax.vjp(lambda b, a, al, dt: _gdn_gates(b, a, al, dt, tril), b_ref[...], a_ref[...], al_ref[...], dt_ref[...])
        db, da, dal, ddt = vjp((dgc_ref[...], dbe_ref[...]))
        dp_ref[:, 0:128] = db
        dp_ref[:, 128:256] = da
        dal_ref[...] += dal
        ddt_ref[...] += ddt

    vec = pl.BlockSpec((1, 128), lambda i: (0, 0))
    blk = pl.BlockSpec((R, 128), lambda i: (i, 0))
    return pl.pallas_call(
        body, name=name, grid=(S // R,),
        in_specs=[pl.BlockSpec((R, 128), lambda i: (i, _GDN_B_BLK)), pl.BlockSpec((R, 128), lambda i: (i, _GDN_A_BLK)), vec, vec, blk, blk,
                  pl.BlockSpec(memory_space=pl.ANY)],
        out_specs=[pl.BlockSpec((R, 256), lambda i: (i, _GDN_B_BLK // 2)), vec, vec],
        out_shape=[SDS(dproj.shape, f32), SDS((1, 128), f32), SDS((1, 128), f32)],
        input_output_aliases={6: 0},
        compiler_params=_cp(("arbitrary",)),
    )(proj, proj, A_log, dt_bias, dgc, dbeta, dproj)


@jax.custom_vjp
def _inv_given(L, T):
    return T


def _inv_given_bwd(T, ct):
    dL = -_nt(_tn(T, ct), T)
    return dL, jnp.zeros_like(T)


_inv_given.defvjp(lambda L, T: (T, T), _inv_given_bwd)


REP = GDN_V_HEADS // GDN_QK_HEADS


def _gdn_intra_all(qs, ks, vs, gcols, bcols, Ts=None):
    H = len(vs)
    C = vs[0].shape[0]
    ii = lax.broadcasted_iota(jnp.int32, (C, C), 0)
    jj = lax.broadcasted_iota(jnp.int32, (C, C), 1)
    grows = [jnp.sum(jnp.where(ii == jj, g, 0.0), axis=0, keepdims=True) for g in gcols]
    decs = [jnp.exp(jnp.where(ii >= jj, gcols[h] - grows[h], NEG)) for h in range(H)]
    kbs = [ks[h // REP] * bcols[h] for h in range(H)]
    As = [_mm_nt(kbs[h], ks[h // REP]) for h in range(H)]
    Ls = [jnp.where(ii > jj, As[h] * decs[h], 0.0) for h in range(H)]
    if Ts is None:
        T = _neumann_inv_batched(Ls)
    else:
        T = [_inv_given(Ls[h], Ts[h]) for h in range(H)]
    us = [_mm(T[h], vs[h] * bcols[h]) for h in range(H)]
    ws = [_mm(T[h], kbs[h] * jnp.exp(gcols[h])) for h in range(H)]
    qk = [_mm_nt(qs[p], ks[p]) for p in range(H // REP)]
    return us, ws, [qk[h // REP] * decs[h] for h in range(H)], T


def _neumann_inv_batched(Ls):
    n, C = 4, Ls[0].shape[0]
    r0 = lax.broadcasted_iota(jnp.int32, (n * C, n * C), 0)
    c0 = lax.broadcasted_iota(jnp.int32, (n * C, n * C), 1)
    same = (r0 // C) == (c0 // C)

    def blockdiag(split):
        return tuple(jnp.where(same, jnp.concatenate([x] * n, axis=0), jnp.zeros((), bf16)) for x in split)

    Ms = [jnp.concatenate(Ls[b:b + n], axis=1) for b in range(0, len(Ls), n)]
    eye = (lax.broadcasted_iota(jnp.int32, (C, n * C), 0) == (lax.broadcasted_iota(jnp.int32, (C, n * C), 1) & (C - 1))).astype(f32)
    Ps = [eye - M for M in Ms]
    Ss = [_split2(M) for M in Ms]
    Bs = [blockdiag(S) for S in Ss]
    k = 1
    while 2 * k < C:
        Ss = [_split2(_dot3(S, B)) for S, B in zip(Ss, Bs)]
        Bs = [blockdiag(S) for S in Ss]
        Ps = [P + _dot3(_split2(P), B) for P, B in zip(Ps, Bs)]
        k *= 2
    return [P[:, h * C:(h + 1) * C] for P in Ps for h in range(n)]


def _gdn_scan_all(qs, ks, gcols, us, ws, attns, S0s):
    H = len(us)
    C = us[0].shape[0]
    last = lax.broadcasted_iota(jnp.int32, (C, 1), 0) == C - 1
    glast = [jnp.sum(jnp.where(last, g, 0.0), axis=0, keepdims=True) for g in gcols]
    wS = [_mm(ws[h], S0s[h]) for h in range(H)]
    qS = [_mm(qs[h // REP] * jnp.exp(gcols[h]), S0s[h]) for h in range(H)]
    vn = [us[h] - wS[h] for h in range(H)]
    av = [_mm(attns[h], vn[h]) for h in range(H)]
    kv = [_mm_tn(ks[h // REP] * jnp.exp(glast[h] - gcols[h]), vn[h]) for h in range(H)]
    return [qS[h] + av[h] for h in range(H)], [S0s[h] * jnp.exp(glast[h]) + kv[h] for h in range(H)]


def _head_cols(blk):
    lane = lax.broadcasted_iota(jnp.int32, blk.shape, 1)
    return [jnp.sum(jnp.where(lane == h, blk, 0.0), axis=1, keepdims=True) for h in range(GDN_V_HEADS)]


def _head_lanes(cols):
    lane = lax.broadcasted_iota(jnp.int32, (cols[0].shape[0], 128), 1)
    out = jnp.zeros((cols[0].shape[0], 128), f32)
    for h, c in enumerate(cols):
        out = out + jnp.where(lane == h, c, 0.0)
    return out


def _heads(ref, n):
    return [ref[:, h * HD:(h + 1) * HD] for h in range(n)]


def _gdn_specs(NC, rv=None):
    ix = (lambda n: n) if rv is None else rv
    qs = pl.BlockSpec((CHUNK, GDN_QK_W), lambda n: (ix(n), 0))
    ks = pl.BlockSpec((CHUNK, GDN_QK_W), lambda n: (ix(n), 1))
    vs = pl.BlockSpec((CHUNK, GDN_V_W), lambda n: (ix(n), 1))
    g1 = pl.BlockSpec((CHUNK, 128), lambda n: (ix(n), 0))
    wide = pl.BlockSpec((CHUNK, GDN_V_W), lambda n: (ix(n), 0))
    sq = pl.BlockSpec((1, GDN_V_HEADS, CHUNK, CHUNK), lambda n: (ix(n), 0, 0, 0))
    st = pl.BlockSpec((1, GDN_V_HEADS, HD, HD), lambda n: (ix(n), 0, 0, 0))
    return qs, ks, vs, g1, wide, sq, st


def gdn_intra_fwd(qkvc, gc, beta, name, side=None):
    S = qkvc.shape[0]
    NC = S // CHUNK

    def body(q_ref, k_ref, v_ref, gc_ref, be_ref, u_ref, w_ref, at_ref, T_ref):
        us, ws, attns, Ts = _gdn_intra_all(_heads(q_ref, GDN_QK_HEADS), _heads(k_ref, GDN_QK_HEADS), _heads(v_ref, GDN_V_HEADS),
                                           _head_cols(gc_ref[...]), _head_cols(be_ref[...]))
        for h in range(GDN_V_HEADS):
            u_ref[:, h * HD:(h + 1) * HD] = us[h]
            w_ref[:, h * HD:(h + 1) * HD] = ws[h]
            at_ref[0, h] = attns[h]
            T_ref[0, h] = Ts[h]

    qs, ks, vs, g1, wide, sq, _ = _gdn_specs(NC)
    return _call(
        body, name=name, grid=(NC,),
        in_specs=[qs, ks, vs, g1, g1], out_specs=[wide, wide, sq, sq],
        out_shape=[SDS((S, GDN_V_W), f32), SDS((S, GDN_V_W), f32),
                   SDS((NC, GDN_V_HEADS, CHUNK, CHUNK), f32), SDS((NC, GDN_V_HEADS, CHUNK, CHUNK), f32)],
        args=(qkvc, qkvc, qkvc, gc, beta), side=side)


def gdn_scan_fwd(qkvc, gc, u, w, attn, name):
    S = qkvc.shape[0]
    NC = S // CHUNK

    def body(q_ref, k_ref, gc_ref, u_ref, w_ref, at_ref, o_ref, st_ref, state):
        @pl.when(pl.program_id(0) == 0)
        def _():
            state[...] = jnp.zeros_like(state)

        S0s = [state[h] for h in range(GDN_V_HEADS)]
        os_, S1s = _gdn_scan_all(_heads(q_ref, GDN_QK_HEADS), _heads(k_ref, GDN_QK_HEADS), _head_cols(gc_ref[...]),
                                 _heads(u_ref, GDN_V_HEADS), _heads(w_ref, GDN_V_HEADS),
                                 [at_ref[0, h] for h in range(GDN_V_HEADS)], S0s)
        for h in range(GDN_V_HEADS):
            o_ref[:, h * HD:(h + 1) * HD] = os_[h]
            st_ref[0, h] = S0s[h]
            state[h] = S1s[h]

    qs, ks, _, g1, wide, sq, st = _gdn_specs(NC)
    return pl.pallas_call(
        body, name=name, grid=(NC,),
        in_specs=[qs, ks, g1, wide, wide, sq], out_specs=[wide, st],
        out_shape=[SDS((S, GDN_V_W), f32), SDS((NC, GDN_V_HEADS, HD, HD), f32)],
        scratch_shapes=[pltpu.VMEM((GDN_V_HEADS, HD, HD), f32)],
        compiler_params=_cp(("arbitrary",)),
    )(qkvc, qkvc, gc, u, w, attn)


def gdn_scan_bwd(qkvc, gc, u, w, attn, states, do, name, side=None):
    S = qkvc.shape[0]
    NC = S // CHUNK

    def body(q_ref, k_ref, gc_ref, u_ref, w_ref, at_ref, st_ref, do_ref,
             dq_ref, dk_ref, dgc_ref, du_ref, dw_ref, dat_ref, dstate):
        @pl.when(pl.program_id(0) == 0)
        def _():
            dstate[...] = jnp.zeros_like(dstate)

        VH = range(GDN_V_HEADS)
        _, vjp = jax.vjp(_gdn_scan_all, _heads(q_ref, GDN_QK_HEADS), _heads(k_ref, GDN_QK_HEADS), _head_cols(gc_ref[...]),
                         _heads(u_ref, GDN_V_HEADS), _heads(w_ref, GDN_V_HEADS), [at_ref[0, h] for h in VH],
                         [st_ref[0, h] for h in VH])
        dqs, dks, dgs, dus, dws, dats, dS0s = vjp((_heads(do_ref, GDN_V_HEADS), [dstate[h] for h in VH]))
        for p in range(GDN_QK_HEADS):
            dq_ref[:, p * HD:(p + 1) * HD] = dqs[p]
            dk_ref[:, p * HD:(p + 1) * HD] = dks[p]
        for h in VH:
            du_ref[:, h * HD:(h + 1) * HD] = dus[h]
            dw_ref[:, h * HD:(h + 1) * HD] = dws[h]
            dat_ref[0, h] = dats[h]
            dstate[h] = dS0s[h]
        dgc_ref[...] = _head_lanes(dgs)

    qs, ks, _, g1, wide, sq, st = _gdn_specs(NC, lambda n: NC - 1 - n)
    dqs = pl.BlockSpec((CHUNK, GDN_QK_W), lambda n: (NC - 1 - n, 0))
    return _call(
        body, name=name, grid=(NC,),
        in_specs=[qs, ks, g1, wide, wide, sq, st, wide],
        out_specs=[dqs, dqs, g1, wide, wide, sq],
        out_shape=[SDS((S, GDN_QK_W), f32), SDS((S, GDN_QK_W), f32), SDS((S, 128), f32), SDS((S, GDN_V_W), f32),
                   SDS((S, GDN_V_W), f32), SDS((NC, GDN_V_HEADS, CHUNK, CHUNK), f32)],
        scratch=[pltpu.VMEM((GDN_V_HEADS, HD, HD), f32)], args=(qkvc, qkvc, gc, u, w, attn, states, do), side=side)


def gdn_intra_bwd(qkvc, gc, beta, Ts, du, dw, dattn, dq_s, dk_s, dgc_s, name, side=None):
    S = qkvc.shape[0]
    NC = S // CHUNK

    def body(q_ref, k_ref, v_ref, gc_ref, be_ref, T_ref, du_ref, dw_ref, dat_ref, dqs_ref, dks_ref, dgs_ref,
             dqkv_ref, dgc_ref, dbe_ref):
        VH = range(GDN_V_HEADS)
        Ts = [T_ref[0, h] for h in VH]
        _, vjp = jax.vjp(lambda q_, k_, v_, g_, b_: _gdn_intra_all(q_, k_, v_, g_, b_, Ts)[:3],
                         _heads(q_ref, GDN_QK_HEADS), _heads(k_ref, GDN_QK_HEADS), _heads(v_ref, GDN_V_HEADS),
                         _head_cols(gc_ref[...]), _head_cols(be_ref[...]))
        dqs, dks, dvs, dgs, dbs = vjp((_heads(du_ref, GDN_V_HEADS), _heads(dw_ref, GDN_V_HEADS), [dat_ref[0, h] for h in VH]))
        for p in range(GDN_QK_HEADS):
            dqkv_ref[:, p * HD:(p + 1) * HD] = dqs[p] + dqs_ref[:, p * HD:(p + 1) * HD]
            dqkv_ref[:, GDN_QK_W + p * HD:GDN_QK_W + (p + 1) * HD] = dks[p] + dks_ref[:, p * HD:(p + 1) * HD]
        for h in VH:
            dqkv_ref[:, 2 * GDN_QK_W + h * HD:2 * GDN_QK_W + (h + 1) * HD] = dvs[h]
        dgc_ref[...] = _head_lanes(dgs) + dgs_ref[...]
        dbe_ref[...] = _head_lanes(dbs)

    qs, ks, vs, g1, wide, sq, _ = _gdn_specs(NC)
    dqs = pl.BlockSpec((CHUNK, GDN_QK_W), lambda n: (n, 0))
    return _call(
        body, name=name, grid=(NC,),
        in_specs=[qs, ks, vs, g1, g1, sq, wide, wide, sq, dqs, dqs, g1],
        out_specs=[pl.BlockSpec((CHUNK, GDN_CONV), lambda n: (n, 0)), g1, g1],
        out_shape=[SDS((S, GDN_CONV), f32), SDS((S, 128), f32), SDS((S, 128), f32)],
        args=(qkvc, qkvc, qkvc, gc, beta, Ts, du, dw, dattn, dq_s, dk_s, dgc_s), side=side)


def _gated_norm(o, z, nw):
    parts = []
    for h in range(GDN_V_HEADS):
        oh = o[:, h * HD:(h + 1) * HD]
        r = lax.rsqrt(jnp.mean(oh * oh, axis=-1, keepdims=True) + EPS)
        parts.append((oh * r * nw) * _silu(z[:, h * HD:(h + 1) * HD]))
    return jnp.concatenate(parts, axis=1)


def gdn_out_fwd(o, proj, nw, W, x, gate, name):
    S, D = x.shape
    tm = _rows(S, 256)

    def body(o_ref, z_ref, nw_ref, w_ref, x_ref, g_ref, xn_ref, y_ref, og_ref):
        og = _gated_norm(o_ref[...], z_ref[...], nw_ref[...]).astype(bf16)
        y = jnp.dot(og, w_ref[...], preferred_element_type=f32)
        og_ref[...] = og
        y_ref[...] = y
        xn_ref[...] = x_ref[...] + g_ref[...] * y

    row = pl.BlockSpec((tm, D), lambda i: (i, 0))
    wide = pl.BlockSpec((tm, GDN_V_W), lambda i: (i, 0))
    return pl.pallas_call(
        body, name=name, grid=(S // tm,),
        in_specs=[wide, pl.BlockSpec((tm, GDN_V_W), lambda i: (i, 2)), pl.BlockSpec((1, HD), lambda i: (0, 0)),
                  pl.BlockSpec((GDN_V_W, D), lambda i: (0, 0)), row, pl.BlockSpec((1, D), lambda i: (0, 0))],
        out_specs=[row, row, wide],
        out_shape=[SDS((S, D), f32), SDS((S, D), f32), SDS((S, GDN_V_W), bf16)],
        compiler_params=_cp(("arbitrary",)),
    )(o, proj, nw, W, x, gate)


def gdn_out_bwd(dxn, y, gate, o, proj, nw, W, name):
    S, D = dxn.shape
    tm = _rows(S, 256)

    def body(dx_ref, y_ref, g_ref, o_ref, z_ref, nw_ref, w_ref, dy_ref, dg_ref, do_ref, dz_ref, dnw_ref):
        @pl.when(pl.program_id(0) == 0)
        def _():
            dg_ref[...] = jnp.zeros_like(dg_ref)
            dnw_ref[...] = jnp.zeros_like(dnw_ref)

        dx = dx_ref[...]
        dy = dx * g_ref[...]
        dy_ref[...] = dy
        dg_ref[...] += jnp.sum(dx * y_ref[...], axis=0, keepdims=True)
        dog = _nt(dy, w_ref[...])
        _, vjp = jax.vjp(_gated_norm, o_ref[...], z_ref[...], nw_ref[...])
        do, dz, dnw = vjp(dog)
        do_ref[...] = do
        dz_ref[...] = dz
        dnw_ref[...] += dnw

    row = pl.BlockSpec((tm, D), lambda i: (i, 0))
    wide = pl.BlockSpec((tm, GDN_V_W), lambda i: (i, 0))
    vecd = pl.BlockSpec((1, D), lambda i: (0, 0))
    vech = pl.BlockSpec((1, HD), lambda i: (0, 0))
    return pl.pallas_call(
        body, name=name, grid=(S // tm,),
        in_specs=[row, row, vecd, wide, pl.BlockSpec((tm, GDN_V_W), lambda i: (i, 2)), vech, pl.BlockSpec((GDN_V_W, D), lambda i: (0, 0))],
        out_specs=[row, vecd, wide, pl.BlockSpec((tm, GDN_V_W), lambda i: (i, 2)), vech],
        out_shape=[SDS((S, D), f32), SDS((1, D), f32), SDS((S, GDN_V_W), f32), SDS((S, GDN_IN_PAD), f32), SDS((1, HD), f32)],
        compiler_params=_cp(("arbitrary",)),
    )(dxn, y, gate, o, proj, nw, W)


def _rms_w(x, w):
    return (x * lax.rsqrt(jnp.mean(x * x, axis=-1, keepdims=True) + EPS)) * w


def _split3(c):
    hi = c.astype(bf16).astype(f32)
    r1 = c - hi
    mid = r1.astype(bf16).astype(f32)
    lo = (r1 - mid).astype(bf16).astype(f32)
    return hi, mid, lo


_FOX_F_BLK = 4 * FOX_W // 128


def fox_prep_fwd(proj, f_bias, qn_w, kn_w, name):
    S = proj.shape[0]
    tm = _rows(S, 256)

    def body(q_ref, k_ref, v_ref, f_ref, fb_ref, qw_ref, kw_ref, Q_ref, K_ref, V_ref, carry):
        @pl.when(pl.program_id(0) == 0)
        def _():
            carry[...] = jnp.zeros_like(carry)

        ii = lax.broadcasted_iota(jnp.int32, (tm, tm), 0)
        jj = lax.broadcasted_iota(jnp.int32, (tm, tm), 1)
        lf = jax.nn.log_sigmoid(f_ref[...] + fb_ref[...])
        cum = _hdot((ii >= jj).astype(f32), lf) + carry[...]
        carry[...] = cum[tm - 1:tm, :]
        lane = lax.broadcasted_iota(jnp.int32, (tm, FOX_D), 1)
        q, k, v = q_ref[...], k_ref[...], v_ref[...]
        for h in range(FOX_H):
            sl = slice(h * FOX_D, (h + 1) * FOX_D)
            hi, mid, lo = _split3(cum[:, h:h + 1])
            qn = _rms_w(q[:, sl], qw_ref[...]) * FOX_D ** -0.5
            kn = _rms_w(k[:, sl], kw_ref[...])
            eq = jnp.where(lane == 0, hi, jnp.where(lane == 1, mid, jnp.where(lane == 2, lo, jnp.where(lane < 6, 1.0, 0.0))))
            ek = jnp.where(lane < 3, 1.0, jnp.where(lane == 3, -hi, jnp.where(lane == 4, -mid, jnp.where(lane == 5, -lo, 0.0))))
            ev = jnp.where(lane == 0, 1.0, 0.0)
            Q_ref[:, h * 128:(h + 1) * 128] = jnp.concatenate([qn, eq], axis=1).astype(bf16)
            K_ref[:, h * 128:(h + 1) * 128] = jnp.concatenate([kn, ek], axis=1).astype(bf16)
            V_ref[:, h * 128:(h + 1) * 128] = jnp.concatenate([v[:, sl], ev], axis=1).astype(bf16)

    def colblk(c):
        return pl.BlockSpec((tm, FOX_W), lambda i: (i, c))

    pad = pl.BlockSpec((tm, FOX_PW), lambda i: (i, 0))
    return pl.pallas_call(
        body, name=name, grid=(S // tm,),
        in_specs=[colblk(0), colblk(1), colblk(2), pl.BlockSpec((tm, 128), lambda i: (i, _FOX_F_BLK)),
                  pl.BlockSpec((1, 128), lambda i: (0, 0)), pl.BlockSpec((1, FOX_D), lambda i: (0, 0)), pl.BlockSpec((1, FOX_D), lambda i: (0, 0))],
        out_specs=[pad, pad, pad],
        out_shape=[SDS((S, FOX_PW), bf16)] * 3,
        scratch_shapes=[pltpu.VMEM((1, 128), f32)],
        compiler_params=_cp(("arbitrary",)),
    )(proj, proj, proj, proj, f_bias, qn_w, kn_w)


def fox_prep_bwd(proj, f_bias, qn_w, kn_w, dQ, dK, dV, dz, name):
    S = proj.shape[0]
    tm = _rows(S, 256)
    NB = S // tm

    def body(q_ref, k_ref, f_ref, fb_ref, qw_ref, kw_ref, dQ_ref, dK_ref, dV_ref, dz_ref,
             dp_ref, dfb_ref, dqw_ref, dkw_ref, carry):
        @pl.when(pl.program_id(0) == 0)
        def _():
            carry[...] = jnp.zeros_like(carry)
            dfb_ref[...] = jnp.zeros_like(dfb_ref)
            dqw_ref[...] = jnp.zeros_like(dqw_ref)
            dkw_ref[...] = jnp.zeros_like(dkw_ref)

        q, k = q_ref[...], k_ref[...]
        lane128 = lax.broadcasted_iota(jnp.int32, (tm, 128), 1)
        dcum = jnp.zeros((tm, 128), f32)
        dqs, dks, dvs = [], [], []
        dqw = jnp.zeros((1, FOX_D), f32)
        dkw = jnp.zeros((1, FOX_D), f32)
        for h in range(FOX_H):
            sl = slice(h * FOX_D, (h + 1) * FOX_D)
            dQh = dQ_ref[:, h * 128:(h + 1) * 128]
            dKh = dK_ref[:, h * 128:(h + 1) * 128]
            _, vq = jax.vjp(lambda a, w: _rms_w(a, w) * FOX_D ** -0.5, q[:, sl], qw_ref[...])
            dqh, dw1 = vq(dQh[:, 0:FOX_D])
            _, vk = jax.vjp(_rms_w, k[:, sl], kw_ref[...])
            dkh, dw2 = vk(dKh[:, 0:FOX_D])
            dqs.append(dqh)
            dks.append(dkh)
            dvs.append(dV_ref[:, h * 128:h * 128 + FOX_D])
            dqw = dqw + dw1
            dkw = dkw + dw2
            dcum = dcum + jnp.where(lane128 == h, dQh[:, FOX_D:FOX_D + 1] - dKh[:, FOX_D + 3:FOX_D + 4], 0.0)
        dp_ref[:, 0:FOX_W] = jnp.concatenate(dqs, axis=1)
        dp_ref[:, FOX_W:2 * FOX_W] = jnp.concatenate(dks, axis=1)
        dp_ref[:, 2 * FOX_W:3 * FOX_W] = jnp.concatenate(dvs, axis=1)
        dp_ref[:, 3 * FOX_W:4 * FOX_W] = dz_ref[...]
        ii = lax.broadcasted_iota(jnp.int32, (tm, tm), 0)
        jj = lax.broadcasted_iota(jnp.int32, (tm, tm), 1)
        dlf = _hdot((ii <= jj).astype(f32), dcum) + carry[...]
        carry[...] += jnp.sum(dcum, axis=0, keepdims=True)
        df = dlf * jax.nn.sigmoid(-(f_ref[...] + fb_ref[...]))
        dp_ref[:, 4 * FOX_W:FOX_IN_PAD] = df
        dfb_ref[...] += jnp.sum(df, axis=0, keepdims=True)
        dqw_ref[...] += dqw
        dkw_ref[...] += dkw

    rv = lambda i: NB - 1 - i

    def colblk(c):
        return pl.BlockSpec((tm, FOX_W), lambda i: (rv(i), c))

    pad = pl.BlockSpec((tm, FOX_PW), lambda i: (rv(i), 0))
    cmp_ = pl.BlockSpec((tm, FOX_W), lambda i: (rv(i), 0))
    v128 = pl.BlockSpec((1, 128), lambda i: (0, 0))
    v64 = pl.BlockSpec((1, FOX_D), lambda i: (0, 0))
    return pl.pallas_call(
        body, name=name, grid=(NB,),
        in_specs=[colblk(0), colblk(1), pl.BlockSpec((tm, 128), lambda i: (rv(i), _FOX_F_BLK)), v128, v64, v64, pad, pad, pad, cmp_],
        out_specs=[pl.BlockSpec((tm, FOX_IN_PAD), lambda i: (rv(i), 0)), v128, v64, v64],
        out_shape=[SDS((S, FOX_IN_PAD), f32), SDS((1, 128), f32), SDS((1, FOX_D), f32), SDS((1, FOX_D), f32)],
        scratch_shapes=[pltpu.VMEM((1, 128), f32)],
        compiler_params=_cp(("arbitrary",)),
    )(proj, proj, proj, f_bias, qn_w, kn_w, dQ, dK, dV, dz)


FOX_HB = 2


def _diag_mask(t):
    return lax.broadcasted_iota(jnp.int32, (t, t), 1) <= lax.broadcasted_iota(jnp.int32, (t, t), 0)


def fox_attn_fwd(Q, K, V, name, side=None):
    S = Q.shape[0]
    t = _rows(S, 512)

    HB = FOX_HB
    HS = [slice(h * 128, (h + 1) * 128) for h in range(HB)]

    def body(q_ref, k_ref, v_ref, o_ref, m_sc, acc_sc, s_sc):
        i = pl.program_id(1)
        qs = [q_ref[:, sl] for sl in HS]
        m_sc[...] = jnp.full_like(m_sc, NEG)
        acc_sc[...] = jnp.zeros_like(acc_sc)

        def scores(j):
            j0 = pl.multiple_of(j * t, t)
            return [_nt(qs[h], k_ref[pl.ds(j0, t), HS[h]]) for h in range(HB)]

        def tile(j, diag):
            j0 = pl.multiple_of(j * t, t)
            ss = [s_sc[h] for h in range(HB)]
            if diag:
                ss = [jnp.where(_diag_mask(t), s, NEG) for s in ss]
            else:
                nxt = scores(j + 1)
            ms = [m_sc[h] for h in range(HB)]
            m_new = [jnp.maximum(ms[h], jnp.max(ss[h], axis=1, keepdims=True)) for h in range(HB)]
            ps = [jnp.exp(ss[h] - m_new[h]) for h in range(HB)]
            pv = [_nn(ps[h], v_ref[pl.ds(j0, t), HS[h]]) for h in range(HB)]
            for h in range(HB):
                acc_sc[h] = acc_sc[h] * jnp.exp(ms[h] - m_new[h]) + pv[h]
                m_sc[h] = m_new[h]
                if not diag:
                    s_sc[h] = nxt[h]

        def off_diag(j, c):
            tile(j, False)
            return c

        first = scores(0)
        for h in range(HB):
            s_sc[h] = first[h]
        lax.fori_loop(0, i, off_diag, 0)
        tile(i, True)
        lane = lax.broadcasted_iota(jnp.int32, (t, 128), 1)
        for h in range(HB):
            acc = acc_sc[h]
            l = acc[:, FOX_D:FOX_D + 1]
            o_ref[:, HS[h]] = jnp.where(lane == FOX_D, m_sc[h] + jnp.log(l), acc / l)

    blk = pl.BlockSpec((t, HB * 128), lambda h, i: (i, h))
    seq = pl.BlockSpec((S, HB * 128), lambda h, i: (0, h))
    return _call(
        body, name=name, grid=(FOX_H // HB, S // t),
        in_specs=[blk, seq, seq], out_specs=[blk], out_shape=[SDS((S, FOX_PW), f32)],
        scratch=[pltpu.VMEM((HB, t, 1), f32), pltpu.VMEM((HB, t, 128), f32), pltpu.VMEM((HB, t, t), f32)],
        args=(Q, K, V), side=side)


def fox_attn_bwd(Q, K, V, dO, O, name, side=None):
    S = Q.shape[0]
    t = _rows(S, 512)
    nq = S // t

    HB = FOX_HB
    HS = [slice(h * 128, (h + 1) * 128) for h in range(HB)]

    def body(k_ref, v_ref, q_ref, do_ref, o_ref, dq_ref, dk_ref, dv_ref):
        j = pl.program_id(1)

        @pl.when(j == 0)
        def _():
            dq_ref[...] = jnp.zeros_like(dq_ref)

        dk_ref[...] = jnp.zeros_like(dk_ref)
        dv_ref[...] = jnp.zeros_like(dv_ref)
        ks = [k_ref[:, sl] for sl in HS]
        vs = [v_ref[:, sl] for sl in HS]

        def tile(i, diag):
            i0 = pl.multiple_of(i * t, t)
            R = range(HB)
            qs = [q_ref[pl.ds(i0, t), HS[h]] for h in R]
            dos = [do_ref[pl.ds(i0, t), HS[h]] for h in R]
            ss = [_nt(qs[h], ks[h]) - o_ref[pl.ds(i0, t), h * 128 + FOX_D:h * 128 + FOX_D + 1] for h in R]
            if diag:
                ss = [jnp.where(_diag_mask(t), s, NEG) for s in ss]
            ps = [jnp.exp(s) for s in ss]
            dps = [_nt(dos[h], vs[h]) for h in R]
            dvs = [_tn(ps[h], dos[h]) for h in R]
            dss = [(ps[h] * dps[h]).astype(bf16) for h in R]
            dks = [_tn(dss[h], qs[h]) for h in R]
            dqs = [_nn(dss[h], ks[h]) for h in R]
            for h in R:
                dv_ref[:, HS[h]] += dvs[h]
                dk_ref[:, HS[h]] += dks[h]
                dq_ref[pl.ds(i0, t), HS[h]] += dqs[h]

        tile(j, True)

        def off_diag(i, c):
            tile(i, False)
            return c

        lax.fori_loop(j + 1, nq, off_diag, 0)

    blk = pl.BlockSpec((t, HB * 128), lambda h, j: (j, h))
    seq = pl.BlockSpec((S, HB * 128), lambda h, j: (0, h))
    return _call(
        body, name=name, grid=(FOX_H // HB, nq),
        in_specs=[blk, blk, seq, seq, seq], out_specs=[seq, blk, blk],
        out_shape=[SDS((S, FOX_PW), f32)] * 3, args=(K, V, Q, dO, O), side=side)


def fox_out_fwd(O, proj, W, x, gate, name):
    S, D = x.shape
    tm = _rows(S, 256)

    def body(o_ref, z_ref, w_ref, x_ref, g_ref, xn_ref, y_ref, og_ref):
        z = z_ref[...]
        og = jnp.concatenate([o_ref[:, h * 128:h * 128 + FOX_D] * _silu(z[:, h * FOX_D:(h + 1) * FOX_D]) for h in range(FOX_H)],
                             axis=1).astype(bf16)
        y = jnp.dot(og, w_ref[...], preferred_element_type=f32)
        og_ref[...] = og
        y_ref[...] = y
        xn_ref[...] = x_ref[...] + g_ref[...] * y

    row = pl.BlockSpec((tm, D), lambda i: (i, 0))
    cmp_ = pl.BlockSpec((tm, FOX_W), lambda i: (i, 0))
    return pl.pallas_call(
        body, name=name, grid=(S // tm,),
        in_specs=[pl.BlockSpec((tm, FOX_PW), lambda i: (i, 0)), pl.BlockSpec((tm, FOX_W), lambda i: (i, 3)),
                  pl.BlockSpec((FOX_W, D), lambda i: (0, 0)), row, pl.BlockSpec((1, D), lambda i: (0, 0))],
        out_specs=[row, row, cmp_],
        out_shape=[SDS((S, D), f32), SDS((S, D), f32), SDS((S, FOX_W), bf16)],
        compiler_params=_cp(("arbitrary",)),
    )(O, proj, W, x, gate)


def fox_out_bwd(dxn, y, gate, O, proj, W, name):
    S, D = dxn.shape
    tm = _rows(S, 256)

    def body(dx_ref, y_ref, g_ref, o_ref, z_ref, w_ref, dy_ref, dg_ref, dO_ref, dz_ref):
        @pl.when(pl.program_id(0) == 0)
        def _():
            dg_ref[...] = jnp.zeros_like(dg_ref)

        dx = dx_ref[...]
        dy = dx * g_ref[...]
        dy_ref[...] = dy
        dg_ref[...] += jnp.sum(dx * y_ref[...], axis=0, keepdims=True)
        dog = _nt(dy, w_ref[...])
        z = z_ref[...]
        lane = lax.broadcasted_iota(jnp.int32, (tm, FOX_D), 1)
        dzs = []
        for h in range(FOX_H):
            sl = slice(h * FOX_D, (h + 1) * FOX_D)
            zh = z[:, sl]
            sg = jax.nn.sigmoid(zh)
            oh = o_ref[:, h * 128:h * 128 + FOX_D]
            doh = dog[:, sl] * (zh * sg)
            delta = jnp.sum(doh * oh, axis=1, keepdims=True)
            dO_ref[:, h * 128:(h + 1) * 128] = jnp.concatenate([doh, jnp.where(lane == 0, -delta, 0.0)], axis=1).astype(bf16)
            dzs.append(dog[:, sl] * oh * (sg * (1.0 + zh * (1.0 - sg))))
        dz_ref[...] = jnp.concatenate(dzs, axis=1)

    row = pl.BlockSpec((tm, D), lambda i: (i, 0))
    vecd = pl.BlockSpec((1, D), lambda i: (0, 0))
    pad = pl.BlockSpec((tm, FOX_PW), lambda i: (i, 0))
    return pl.pallas_call(
        body, name=name, grid=(S // tm,),
        in_specs=[row, row, vecd, pad, pl.BlockSpec((tm, FOX_W), lambda i: (i, 3)), pl.BlockSpec((FOX_W, D), lambda i: (0, 0))],
        out_specs=[row, vecd, pad, pl.BlockSpec((tm, FOX_W), lambda i: (i, 0))],
        out_shape=[SDS((S, D), f32), SDS((1, D), f32), SDS((S, FOX_PW), bf16), SDS((S, FOX_W), f32)],
        compiler_params=_cp(("arbitrary",)),
    )(dxn, y, gate, O, proj, W)


def final_loss(x, fw, target, name):
    S, D = x.shape
    tm = _rows(S, 512)

    def body(x_ref, w_ref, t_ref, l_ref, dx_ref, dw_ref):
        @pl.when(pl.program_id(0) == 0)
        def _():
            l_ref[...] = jnp.zeros_like(l_ref)
            dw_ref[...] = jnp.zeros_like(dw_ref)

        out, vjp = jax.vjp(_rms_w, x_ref[...], w_ref[...])
        err = out - t_ref[...]
        l_ref[...] += 0.5 * jnp.sum(jnp.sum(err * err, axis=1, keepdims=True) * (1.0 / D), axis=0, keepdims=True)
        dx, dw = vjp(err * (1.0 / D))
        dx_ref[...] = dx
        dw_ref[...] += dw

    row = pl.BlockSpec((tm, D), lambda i: (i, 0))
    vec = pl.BlockSpec((1, D), lambda i: (0, 0))
    return pl.pallas_call(
        body, name=name, grid=(S // tm,),
        in_specs=[row, vec, row], out_specs=[pl.BlockSpec((1, 128), lambda i: (0, 0)), row, vec],
        out_shape=[SDS((1, 128), f32), SDS((S, D), f32), SDS((1, D), f32)],
        compiler_params=_cp(("arbitrary",)),
    )(x, fw, target)


def ada_fwd(c_all, ada_w, name):
    L, D, n = ada_w.shape

    def body(c_ref, w_ref, o_ref):
        cond = jnp.concatenate([_silu(c_ref[...]), jnp.zeros((8, D), f32)], axis=0)
        o_ref[0] = _nn(cond, w_ref[0])[0:8]

    return pl.pallas_call(
        body, name=name, grid=(L,),
        in_specs=[pl.BlockSpec((NDEV, D), lambda l: (0, 0)), pl.BlockSpec((1, D, n), lambda l: (l, 0, 0))],
        out_specs=pl.BlockSpec((1, NDEV, n), lambda l: (l, 0, 0)),
        out_shape=SDS((L, NDEV, n), f32),
        compiler_params=_cp(("arbitrary",)),
    )(c_all, ada_w)


def ada_grad(c_all, dmod, name):
    L, _, n = dmod.shape
    D = c_all.shape[1]

    def body(c_ref, d_ref, o_ref):
        cond = jnp.concatenate([_silu(c_ref[...]), jnp.zeros((8, D), f32)], axis=0)
        dm = jnp.concatenate([d_ref[0], jnp.zeros((8, n), f32)], axis=0)
        o_ref[0] = _tn(cond, dm)

    return pl.pallas_call(
        body, name=name, grid=(L,),
        in_specs=[pl.BlockSpec((NDEV, D), lambda l: (0, 0)), pl.BlockSpec((1, NDEV, n), lambda l: (l, 0, 0))],
        out_specs=pl.BlockSpec((1, D, n), lambda l: (l, 0, 0)),
        out_shape=SDS((L, D, n), f32),
        compiler_params=_cp(("arbitrary",)),
    )(c_all, dmod)


def reduce_adam(parts, w, m, v, tr, name):
    n, R, C = parts.shape
    c1 = 1.0 / (1.0 - ADAM_B1 ** ADAM_STEP)
    c2 = 1.0 / (1.0 - ADAM_B2 ** ADAM_STEP)

    def body(p_ref, w_ref, m_ref, v_ref, g_ref, d_ref, nm_ref, nv_ref):
        g = p_ref[0].astype(f32)
        for s in range(1, n):
            g = g + p_ref[s].astype(f32)
        nm = ADAM_B1 * m_ref[...] + (1.0 - ADAM_B1) * g
        nv = ADAM_B2 * v_ref[...] + (1.0 - ADAM_B2) * (g * g)
        g_ref[...] = g
        nm_ref[...] = nm
        nv_ref[...] = nv
        d_ref[...] = -ADAM_LR * ((nm * c1) / (jnp.sqrt(nv * c2) + ADAM_EPS) + ADAM_WD * w_ref[...])

    blk = pl.BlockSpec((tr, C), lambda i: (i, 0))
    return pl.pallas_call(
        body, name=name, grid=(R // tr,),
        in_specs=[pl.BlockSpec((n, tr, C), lambda i: (0, i, 0)), blk, blk, blk],
        out_specs=[blk] * 4, out_shape=[SDS((R, C), f32)] * 4,
        compiler_params=_cp(("arbitrary",)),
    )(parts, w, m, v)


def all_gather(xs, name):
    n = len(xs)

    def body(*refs):
        x_refs, out_refs = refs[:n], refs[n:2 * n]
        send_sems, recv_sems, local_sems = refs[2 * n:]
        x_, y_, c_ = _my_pos()
        me, sibling = (x_, y_, c_), (x_, y_, 1 - c_)
        chips = [(1 - x_, y_), (x_, 1 - y_), (1 - x_, 1 - y_)]

        def rows(a, px, py, pc):
            return out_refs[a].at[4 * px + 2 * py + pc]

        def copy(a, k, block, to, own=False):
            return pltpu.make_async_remote_copy(
                src_ref=x_refs[a] if own else rows(a, *block), dst_ref=rows(a, *block),
                send_sem=send_sems.at[k, a], recv_sem=recv_sems.at[k, a], device_id=to, device_id_type=pl.DeviceIdType.MESH)

        mine = [pltpu.make_async_copy(x_refs[a], rows(a, *me), local_sems.at[a]) for a in range(n)]
        for cp in mine:
            cp.start()
        first = []
        for a in range(n):
            first.append(copy(a, 0, me, sibling, own=True))
            first += [copy(a, 1 + j, me, (*chip, c_), own=True) for j, chip in enumerate(chips)]
        for cp in first:
            cp.start()
        passed = []
        for j, chip in enumerate(chips):
            for a in range(n):
                copy(a, 1 + j, (*chip, c_), me).wait_recv()
                cp = copy(a, 4 + j, (*chip, c_), sibling)
                cp.start()
                passed.append(cp)
        for a in range(n):
            copy(a, 0, sibling, me).wait_recv()
            for j, chip in enumerate(chips):
                copy(a, 4 + j, (*chip, 1 - c_), me).wait_recv()
        for cp in first + passed:
            cp.wait_send()
        for cp in mine:
            cp.wait()

    any_ = pl.BlockSpec(memory_space=pl.ANY)
    return pl.pallas_call(
        body, name=name, out_shape=[SDS((NDEV,) + x.shape, x.dtype) for x in xs],
        in_specs=[any_] * n, out_specs=[any_] * n,
        scratch_shapes=[pltpu.SemaphoreType.DMA((7, n)), pltpu.SemaphoreType.DMA((7, n)), pltpu.SemaphoreType.DMA((n,))],
    )(*xs)


GDN_COLS = ((0, GDN_CONV + GDN_V_W, 0), (GDN_CONV + GDN_V_W, GDN_CONV + GDN_V_W + 16, GDN_CONV + GDN_V_W),
            (GDN_CONV + GDN_V_W + 16, GDN_IN, GDN_CONV + GDN_V_W + 128))
FOX_COLS = ((0, FOX_IN, 0),)


def _col_pieces(d, per, cols):
    lo, hi = per * d, per * (d + 1)
    out = []
    for a, b, dst in cols:
        s, e = max(lo, a), min(hi, b)
        if s < e:
            out.append((s - lo, e - s, dst + s - a))
    return out


def cols_from_blocks(g, cols, n_out, name):
    _, L, R, C = g.shape
    tr = min(256, R)

    def body(g_ref, o_ref):
        o_ref[...] = jnp.zeros_like(o_ref)
        for d in range(NDEV):
            for off, ln, dst in _col_pieces(d, C, cols):
                o_ref[0, :, dst:dst + ln] = g_ref[d, 0, :, off:off + ln]

    return pl.pallas_call(
        body, name=name, grid=(L, R // tr),
        in_specs=[pl.BlockSpec((NDEV, 1, tr, C), lambda l, i: (0, l, i, 0))],
        out_specs=pl.BlockSpec((1, tr, n_out), lambda l, i: (l, i, 0)),
        out_shape=SDS((L, R, n_out), g.dtype),
        compiler_params=_cp(("arbitrary", "arbitrary")),
    )(g)


def blocks_from_cols(dw, C, cols, name):
    R, n_in = dw.shape
    tr = min(256, R)

    def body(x_ref, o_ref):
        for d in range(NDEV):
            for off, ln, src in _col_pieces(d, C, cols):
                o_ref[d, :, off:off + ln] = x_ref[:, src:src + ln].astype(bf16)

    return pl.pallas_call(
        body, name=name, grid=(R // tr,),
        in_specs=[pl.BlockSpec((tr, n_in), lambda i: (i, 0))],
        out_specs=pl.BlockSpec((NDEV, tr, C), lambda i: (0, i, 0)),
        out_shape=SDS((NDEV, R, C), bf16),
        compiler_params=_cp(("arbitrary",)),
    )(dw)


BIG = ("a_w_in", "a_conv_w", "a_w_out", "b_w_in", "b_w_out")
SMALL = ("norm_w", "ada_b", "a_A_log", "a_dt_bias", "a_norm_w", "b_f_bias", "b_qn_w", "b_kn_w", "final_norm_w")


def _pack_small(arrs):
    rows = []
    for a in arrs:
        fl = a.reshape(-1)
        pad = (-fl.shape[0]) % 128
        if pad:
            fl = jnp.concatenate([fl, jnp.zeros((pad,), fl.dtype)])
        rows.append(fl)
    flat = jnp.concatenate(rows)
    pad = (-flat.shape[0]) % (8 * 128)
    if pad:
        flat = jnp.concatenate([flat, jnp.zeros((pad,), flat.dtype)])
    return flat.reshape(-1, 128)


def _unpack(packed, shapes, align):
    flat = packed.reshape(-1)
    out, off = [], 0
    for shp in shapes:
        n = 1
        for d in shp:
            n *= d
        out.append(flat[off:off + n].reshape(shp))
        off += n + ((-n) % align)
    return out


def _full_from_gathered(g, shard_shape, axis):
    g = jnp.moveaxis(g, 0, axis)
    shp = list(shard_shape)
    shp[axis] *= NDEV
    return g.reshape(shp)


def _pad_lanes(v, n=128):
    v = v.reshape(1, -1)
    return jnp.concatenate([v, jnp.zeros((1, n - v.shape[1]), v.dtype)], axis=1)


def _carried(fn, *args, side=None, **grads):
    if callable(side):
        side = side(**grads)
    res = fn(*args, side)
    return res if side is not None else (res, None)


def gdn_layer_fwd(x, mod, nw, weights, tag, sides):
    W_in, conv_w, A_log, dt_bias, a_nw, W_out = weights
    shift, scale, gate = mod
    got = {}
    (proj, h), got["inproj"] = _carried(inproj_fwd, x, nw, scale, shift, W_in, GDN_TN, f"{tag}_inproj", side=sides.get("inproj"))
    qkvc = gdn_prep_fwd(proj, conv_w, f"{tag}_prep")
    gc, beta = gdn_gates_fwd(proj, A_log, dt_bias, f"{tag}_gates")
    (u, w, attn, Ts), got["intra"] = _carried(gdn_intra_fwd, qkvc, gc, beta, f"{tag}_intra", side=sides.get("intra"))
    o, states = gdn_scan_fwd(qkvc, gc, u, w, attn, f"{tag}_scan")
    x_new, y, og = gdn_out_fwd(o, proj, a_nw, W_out, x, gate, f"{tag}_out")
    return x_new, (x, proj, h, qkvc, gc, beta, o, states, Ts, y, og, u, w, attn), got


def gdn_layer_bwd(dxn, saved, mod, nw, weights, tag, sides):
    W_in, conv_w, A_log, dt_bias, a_nw, W_out = weights
    shift, scale, gate = mod
    x, proj, h, qkvc, gc, beta, o, states, Ts, y, og, u, w, attn = saved
    got = {}
    dy, dgate, do, dproj, da_nw = gdn_out_bwd(dxn, y, gate, o, proj, a_nw, W_out, f"{tag}_out_bwd")
    dW_out, = matmul_tn(og, dy, 512, f"{tag}_dwout")
    (dq_s, dk_s, dgc_s, du, dw, dattn), got["sbwd"] = _carried(
        gdn_scan_bwd, qkvc, gc, u, w, attn, states, do, f"{tag}_scan_bwd", side=sides.get("sbwd"), dW_out=dW_out)
    (dqkvc, dgc, dbeta), got["intrab"] = _carried(
        gdn_intra_bwd, qkvc, gc, beta, Ts, du, dw, dattn, dq_s, dk_s, dgc_s, f"{tag}_intra_bwd", side=sides.get("intrab"), dW_out=dW_out)
    dproj, dA_log, ddt = gdn_gates_bwd(proj, A_log, dt_bias, dgc, dbeta, dproj, f"{tag}_gates_bwd")
    dproj, dconv_w = gdn_prep_bwd(proj, conv_w, dqkvc, dproj, f"{tag}_prep_bwd")
    (dW_in,), got["dwin"] = _carried(matmul_tn, h, dproj, GDN_TN, f"{tag}_dwin", side=sides.get("dwin"), dW_out=dW_out)
    (dx, dnw, dscale, dshift), got["ibwd"] = _carried(
        inproj_bwd_x, x, nw, scale, shift, W_in, dproj, dxn, GDN_TN, f"{tag}_inproj_bwd", side=sides.get("ibwd"),
        dW_out=dW_out, dW_in=dW_in, dconv_w=dconv_w)
    grads = dict(norm_w=dnw, W_in=dW_in, conv_w=dconv_w, A_log=dA_log[:, :16], dt_bias=ddt[:, :16], a_nw=da_nw, W_out=dW_out,
                 dmod=jnp.concatenate([dshift, dscale, dgate], axis=1))
    return dx, grads, got


def fox_layer_fwd(x, mod, nw, weights, tag, sides):
    W_in, f_bias, qn_w, kn_w, W_out = weights
    shift, scale, gate = mod
    got = {}
    (proj, h), got["inproj"] = _carried(inproj_fwd, x, nw, scale, shift, W_in, FOX_TN, f"{tag}_inproj", side=sides.get("inproj"))
    Q, K, V = fox_prep_fwd(proj, f_bias, qn_w, kn_w, f"{tag}_prep")
    (O,), got["attn"] = _carried(fox_attn_fwd, Q, K, V, f"{tag}_attn", side=sides.get("attn"))
    x_new, y, og = fox_out_fwd(O, proj, W_out, x, gate, f"{tag}_out")
    return x_new, (x, proj, h, Q, K, V, O, y, og), got


def fox_layer_bwd(dxn, saved, mod, nw, weights, tag, sides):
    W_in, f_bias, qn_w, kn_w, W_out = weights
    shift, scale, gate = mod
    x, proj, h, Q, K, V, O, y, og = saved
    got = {}
    dy, dgate, dO, dz = fox_out_bwd(dxn, y, gate, O, proj, W_out, f"{tag}_out_bwd")
    dW_out, = matmul_tn(og, dy, 512, f"{tag}_dwout")
    (dQ, dK, dV), got["abwd"] = _carried(fox_attn_bwd, Q, K, V, dO, O, f"{tag}_attn_bwd", side=sides.get("abwd"))
    dproj, dfb, dqw, dkw = fox_prep_bwd(proj, f_bias, qn_w, kn_w, dQ, dK, dV, dz, f"{tag}_prep_bwd")
    (dW_in,), got["dwin"] = _carried(matmul_tn, h, dproj, FOX_TN, f"{tag}_dwin", side=sides.get("dwin"))
    dx, dnw, dscale, dshift = inproj_bwd_x(x, nw, scale, shift, W_in, dproj, dxn, FOX_TN, f"{tag}_inproj_bwd")
    grads = dict(norm_w=dnw, W_in=dW_in, f_bias=dfb[:, :16], qn_w=dqw, kn_w=dkw, W_out=dW_out,
                 dmod=jnp.concatenate([dshift, dscale, dgate], axis=1))
    return dx, grads, got


class LocalPlan:
    def __init__(self, full):
        self.full = full

    def layer_weights(self, i):
        j, f = i // 2, self.full
        return (f["a_w_in"][j], f["a_w_out"][j], f["a_conv_w"][j]) if i % 2 == 0 else (f["b_w_in"][j], f["b_w_out"][j])

    def fwd_sides(self, i):
        return {}

    def fwd_got(self, i, got):
        pass

    def bwd_sides(self, i):
        return {}

    def bwd_got(self, i, grads, got):
        pass


def device_step(x, mod_all, norm_w, small, final_norm_w, target, plan):
    D = x.shape[1]
    mods = [(mod_all[i:i + 1, 0:D], mod_all[i:i + 1, D:2 * D], mod_all[i:i + 1, 2 * D:3 * D]) for i in range(4)]

    def weights(i):
        j = i // 2
        if i % 2 == 0:
            W_in, W_out, conv_w = plan.layer_weights(i)
            return (W_in, conv_w, _pad_lanes(small["a_A_log"][j]), _pad_lanes(small["a_dt_bias"][j]), small["a_norm_w"][j:j + 1], W_out)
        W_in, W_out = plan.layer_weights(i)
        return (W_in, _pad_lanes(small["b_f_bias"][j]), small["b_qn_w"][j:j + 1], small["b_kn_w"][j:j + 1], W_out)

    saved, wts = [], []
    for i in range(4):
        wts.append(weights(i))
        fwd = gdn_layer_fwd if i % 2 == 0 else fox_layer_fwd
        x, sv, got = fwd(x, mods[i], norm_w[i:i + 1], wts[i], f"L{i}", plan.fwd_sides(i))
        plan.fwd_got(i, got)
        saved.append(sv)
    loss, dx, dfw = final_loss(x, final_norm_w.reshape(1, D), target, "final_loss")
    lg = [None] * 4
    for i in reversed(range(4)):
        bwd = gdn_layer_bwd if i % 2 == 0 else fox_layer_bwd
        dx, lg[i], got = bwd(dx, saved[i], mods[i], norm_w[i:i + 1], wts[i], f"L{i}", plan.bwd_sides(i))
        plan.bwd_got(i, lg[i], got)
    g = dict(
        norm_w=jnp.concatenate([lg[i]["norm_w"] for i in range(4)], axis=0),
        dmod=jnp.concatenate([lg[i]["dmod"] for i in range(4)], axis=0),
        a_w_in=[lg[i]["W_in"] for i in (0, 2)],
        a_conv_w=jnp.stack([lg[i]["conv_w"] for i in (0, 2)]),
        a_A_log=jnp.concatenate([lg[i]["A_log"] for i in (0, 2)], axis=0),
        a_dt_bias=jnp.concatenate([lg[i]["dt_bias"] for i in (0, 2)], axis=0),
        a_norm_w=jnp.concatenate([lg[i]["a_nw"] for i in (0, 2)], axis=0),
        a_w_out=[lg[i]["W_out"] for i in (0, 2)],
        b_w_in=[lg[i]["W_in"] for i in (1, 3)],
        b_f_bias=jnp.concatenate([lg[i]["f_bias"] for i in (1, 3)], axis=0),
        b_qn_w=jnp.concatenate([lg[i]["qn_w"] for i in (1, 3)], axis=0),
        b_kn_w=jnp.concatenate([lg[i]["kn_w"] for i in (1, 3)], axis=0),
        b_w_out=[lg[i]["W_out"] for i in (1, 3)],
        final_norm_w=dfw.reshape(-1),
    )
    return loss[0, 0], dx, g


class MeshPlan:
    def __init__(self, shards, w0, conv_full):
        self.shards = shards
        self.w = {0: w0}
        self.conv = conv_full
        self.recv = {}
        self.pending = {}
        self.names = {}

    def layer_weights(self, i):
        return self.w[i]

    def _gather_side(self, layer):
        names = ("a_w_in", "a_w_out") if layer % 2 == 0 else ("b_w_in", "b_w_out")
        out = []
        for n in names:
            sh = self.shards[n][layer // 2]
            out.append(sh.reshape(-1, sh.shape[-1]))
        return ("gather", out)

    def fwd_sides(self, i):
        if i == 0:
            return {"inproj": self._gather_side(1), "intra": self._gather_side(2)}
        if i == 1:
            return {"attn": self._gather_side(3)}
        return {}

    def fwd_got(self, i, got):
        for key, layer in (("inproj", 1), ("intra", 2), ("attn", 3)):
            if got.get(key) is None:
                continue
            g_in, g_out = got[key]
            D = g_out.shape[-1]
            j = layer // 2
            if layer % 2 == 1:
                W_in = cols_from_blocks(g_in[:, None], FOX_COLS, FOX_IN_PAD, f"b_w_in_cols{j}")[0]
                self.w[layer] = (W_in, g_out.reshape(-1, D))
            else:
                W_in = cols_from_blocks(g_in[:, None], GDN_COLS, GDN_IN_PAD, f"a_w_in_cols{j}")[0]
                self.w[layer] = (W_in, g_out.reshape(-1, D), self.conv[j])

    @staticmethod
    def _out_blocks(dW_out):
        return dW_out.astype(bf16).reshape(NDEV, -1, dW_out.shape[-1])

    def _in_blocks(self, name, j, dW_in):
        cols = GDN_COLS if name == "a_w_in" else FOX_COLS
        return blocks_from_cols(dW_in, self.shards[name].shape[-1], cols, f"{name}_blocks{j}")

    def bwd_sides(self, i):
        self.names = {}
        sides = {}
        for (layer, key) in [k for k in self.pending if k[0] == i]:
            self.names[key], arrs = self.pending.pop((layer, key))
            sides[key] = ("scatter", arrs)
        if i == 0:
            def sbwd(dW_out):
                self.names["sbwd"] = [("a_w_out", 0)]
                return ("scatter", [self._out_blocks(dW_out)])

            def ibwd(dW_out, dW_in, dconv_w):
                conv = jnp.stack([dconv_w, self._dconv1])
                n = conv.shape[-1] // NDEV
                self.names["ibwd"] = [("a_w_in", 0), ("a_conv_w", None)]
                return ("scatter", [self._in_blocks("a_w_in", 0, dW_in),
                                    jnp.moveaxis(conv.reshape(2, 4, NDEV, n), 2, 0).reshape(NDEV, 8, n)])

            sides["sbwd"], sides["ibwd"] = sbwd, ibwd
        return sides

    def bwd_got(self, i, grads, got):
        for key, arrs in got.items():
            if arrs is not None:
                self.recv.update(zip(self.names[key], arrs))
        j = i // 2
        if i % 2 == 1:
            self.pending[(i - 1, "sbwd" if i == 3 else "intrab")] = (
                [("b_w_in", j), ("b_w_out", j)], [self._in_blocks("b_w_in", j, grads["W_in"]), self._out_blocks(grads["W_out"])])
        elif i == 2:
            self.pending[(1, "abwd")] = (
                [("a_w_in", 1), ("a_w_out", 1)], [self._in_blocks("a_w_in", 1, grads["W_in"]), self._out_blocks(grads["W_out"])])
            self._dconv1 = grads["conv_w"]


def kernel(x, c, norm_w, ada_w, ada_b, a_w_in, a_conv_w, a_A_log, a_dt_bias, a_norm_w, a_w_out, b_w_in, b_f_bias, b_qn_w, b_kn_w, b_w_out, final_norm_w, loss_target, m_norm_w, m_ada_w, m_ada_b, m_a_w_in, m_a_conv_w, m_a_A_log, m_a_dt_bias, m_a_norm_w, m_a_w_out, m_b_w_in, m_b_f_bias, m_b_qn_w, m_b_kn_w, m_b_w_out, m_final_norm_w, v_norm_w, v_ada_w, v_ada_b, v_a_w_in, v_a_conv_w, v_a_A_log, v_a_dt_bias, v_a_norm_w, v_a_w_out, v_b_w_in, v_b_f_bias, v_b_qn_w, v_b_kn_w, v_b_w_out, v_final_norm_w):
    W = dict(norm_w=norm_w, ada_w=ada_w, ada_b=ada_b, a_w_in=a_w_in, a_conv_w=a_conv_w, a_A_log=a_A_log, a_dt_bias=a_dt_bias,
             a_norm_w=a_norm_w, a_w_out=a_w_out, b_w_in=b_w_in, b_f_bias=b_f_bias, b_qn_w=b_qn_w, b_kn_w=b_kn_w, b_w_out=b_w_out,
             final_norm_w=final_norm_w)
    M = dict(norm_w=m_norm_w, ada_w=m_ada_w, ada_b=m_ada_b, a_w_in=m_a_w_in, a_conv_w=m_a_conv_w, a_A_log=m_a_A_log,
             a_dt_bias=m_a_dt_bias, a_norm_w=m_a_norm_w, a_w_out=m_a_w_out, b_w_in=m_b_w_in, b_f_bias=m_b_f_bias, b_qn_w=m_b_qn_w,
             b_kn_w=m_b_kn_w, b_w_out=m_b_w_out, final_norm_w=m_final_norm_w)
    V = dict(norm_w=v_norm_w, ada_w=v_ada_w, ada_b=v_ada_b, a_w_in=v_a_w_in, a_conv_w=v_a_conv_w, a_A_log=v_a_A_log,
             a_dt_bias=v_a_dt_bias, a_norm_w=v_a_norm_w, a_w_out=v_a_w_out, b_w_in=v_b_w_in, b_f_bias=v_b_f_bias, b_qn_w=v_b_qn_w,
             b_kn_w=v_b_kn_w, b_w_out=v_b_w_out, final_norm_w=v_final_norm_w)
    S, D = x.shape[1], x.shape[2]
    me = 4 * lax.axis_index("x") + 2 * lax.axis_index("y") + lax.axis_index("c")
    small_shapes = [W[n].shape for n in SMALL]

    shards = {n: W[n].astype(bf16) for n in ("a_w_in", "a_w_out", "b_w_in", "b_w_out")}
    gath = all_gather([shards["a_w_in"][0], shards["a_w_out"][0], a_conv_w.reshape(8, -1), c.reshape(8, D // 8)], "gather_w0")
    conv_full = _full_from_gathered(gath[2].reshape((NDEV,) + a_conv_w.shape), a_conv_w.shape, 2)
    w0 = (cols_from_blocks(gath[0][:, None], GDN_COLS, GDN_IN_PAD, "a_w_in_cols0")[0], gath[1].reshape(-1, D), conv_full[0])
    plan = MeshPlan(shards, w0, conv_full)
    c_all = gath[3].reshape(NDEV, D)

    mod_part = ada_fwd(c_all, ada_w, "ada_fwd")
    n_ada = ada_w.shape[2]
    mod_g = all_gather([mod_part.reshape(4 * NDEV, n_ada)], "gather_mod")[0].reshape(NDEV, 4, NDEV, n_ada)
    mod_mine = lax.dynamic_index_in_dim(mod_g, me, axis=2, keepdims=False)
    mod_all = jnp.moveaxis(mod_mine, 0, 1).reshape(4, NDEV * n_ada) + ada_b

    loss, dx, g = device_step(x[0], mod_all, norm_w, W, final_norm_w, loss_target[0], plan)
    loss = lax.psum(loss, MESH_AXES)

    g_small = dict(g, ada_b=g["dmod"])
    sp = _pack_small([g_small[n] for n in SMALL])
    sp_all = all_gather([sp], "gather_small")[0]
    sw, sm, sv = (_pack_small([T[n] for n in SMALL]) for T in (W, M, V))
    sg, sd, snm, snv = (_unpack(t, small_shapes, 128) for t in reduce_adam(sp_all, sw, sm, sv, sp.shape[0], "adam_small"))

    off_b = 0
    for n, shp in zip(SMALL, small_shapes):
        if n == "ada_b":
            break
        cnt = 1
        for d in shp:
            cnt *= d
        off_b += cnt + ((-cnt) % 128)
    dmod_all = sp_all.reshape(NDEV, -1)[:, off_b:off_b + 4 * 3 * D].reshape(NDEV, 4, 3 * D)
    dmod_cols = lax.dynamic_slice_in_dim(dmod_all, me * n_ada, n_ada, axis=2)
    g_ada = ada_grad(c_all, jnp.moveaxis(dmod_cols, 0, 1), "ada_grad")
    r_ada = reduce_adam(g_ada.reshape(1, 4 * D, n_ada), *(T["ada_w"].reshape(4 * D, n_ada) for T in (W, M, V)), 512, "adam_ada")
    ag, ad, anm, anv = (t.reshape(ada_w.shape) for t in r_ada)

    big = {}
    for n in BIG:
        C = W[n].shape[-1]
        parts = plan.recv[(n, None)] if n == "a_conv_w" else jnp.stack([plan.recv[(n, 0)], plan.recv[(n, 1)]], axis=1).reshape(NDEV, -1, C)
        res = reduce_adam(parts, *(T[n].reshape(parts.shape[1:]) for T in (W, M, V)), min(256, parts.shape[1]), f"adam_{n}")
        big[n] = [t.reshape(W[n].shape) for t in res]

    outs = {}
    for idx, (k, sm_l, ada_t) in enumerate((("grad", sg, ag), ("delta", sd, ad), ("new_m", snm, anm), ("new_v", snv, anv))):
        d = dict(zip(SMALL, sm_l))
        d.update({n: big[n][idx] for n in BIG})
        d["ada_w"] = ada_t
        outs[k] = d
    order = ("norm_w", "ada_w", "ada_b", "a_w_in", "a_conv_w", "a_A_log", "a_dt_bias", "a_norm_w", "a_w_out", "b_w_in", "b_f_bias",
             "b_qn_w", "b_kn_w", "b_w_out", "final_norm_w")
    return (loss, dx[None], *[outs["grad"][n] for n in order], *[outs["delta"][n] for n in order],
            *[outs["new_m"][n] for n in order], *[outs["new_v"][n] for n in order])
```

```python
import functools

import jax
import jax.numpy as jnp
from jax import lax
from jax.experimental import pallas as pl
from jax.experimental.pallas import tpu as pltpu

f32 = jnp.float32
bf16 = jnp.bfloat16
SDS = jax.ShapeDtypeStruct

EPS = 1e-6
CHUNK = 64
HD = 128
GDN_QK_HEADS = 8
GDN_V_HEADS = 16
GDN_QK_W = GDN_QK_HEADS * HD
GDN_V_W = GDN_V_HEADS * HD
GDN_CONV = 2 * GDN_QK_W + GDN_V_W
GDN_IN = GDN_CONV + GDN_V_W + 2 * GDN_V_HEADS
GDN_IN_PAD = GDN_CONV + GDN_V_W + 256
GDN_TN = 640
FOX_H = 16
FOX_D = 64
FOX_W = FOX_H * FOX_D
FOX_IN = 4 * FOX_W + FOX_H
FOX_IN_PAD = 4 * FOX_W + 128
FOX_TN = 1408
FOX_PW = FOX_H * 128
NDEV = 8
MESH_AXES = ("x", "y", "c")
NEG = -1e30

ADAM_LR = 0.001
ADAM_B1 = 0.9
ADAM_B2 = 0.999
ADAM_EPS = 1e-08
ADAM_WD = 0.01
ADAM_STEP = 10

VMEM_LIMIT = 56 * 1024 * 1024


def _cp(sem=None):
    return pltpu.CompilerParams(dimension_semantics=sem, vmem_limit_bytes=VMEM_LIMIT)


def _bdot(a, b, dims):
    return lax.dot_general(a.astype(bf16), b.astype(bf16), (dims, ((), ())), preferred_element_type=f32)


def _nn(a, b):
    return _bdot(a, b, ((1,), (0,)))


def _nt(a, b):
    return _bdot(a, b, ((1,), (1,)))


def _tn(a, b):
    return _bdot(a, b, ((0,), (0,)))


def _hdot(a, b, dims=((1,), (0,))):
    return lax.dot_general(a, b, (dims, ((), ())), precision=lax.Precision.HIGHEST, preferred_element_type=f32)


def _split2(a):
    hi = a.astype(bf16)
    return hi, (a - hi.astype(f32)).astype(bf16)


def _dot3(a, b):
    (ah, al), (bh, bl) = a, b
    return (jnp.dot(ah, bh, preferred_element_type=f32) + jnp.dot(ah, bl, preferred_element_type=f32)
            + jnp.dot(al, bh, preferred_element_type=f32))


@jax.custom_vjp
def _mm(a, b):
    return _nn(a, b)


_mm.defvjp(lambda a, b: (_nn(a, b), (a, b)), lambda r, g: (_nt(g, r[1]), _tn(r[0], g)))


@jax.custom_vjp
def _mm_nt(a, b):
    return _nt(a, b)


_mm_nt.defvjp(lambda a, b: (_nt(a, b), (a, b)), lambda r, g: (_nn(g, r[1]), _tn(g, r[0])))


@jax.custom_vjp
def _mm_tn(a, b):
    return _tn(a, b)


_mm_tn.defvjp(lambda a, b: (_tn(a, b), (a, b)), lambda r, g: (_nt(r[1], g), _nn(r[0], g)))


def _silu(x):
    return x * jax.nn.sigmoid(x)


def _rms_mod(x, nw, scale, shift):
    r = lax.rsqrt(jnp.mean(x * x, axis=-1, keepdims=True) + EPS)
    return (x * r * nw) * (1.0 + scale) + shift


def _rows(S, want):
    return min(want, S)


def _my_pos():
    return lax.axis_index("x"), lax.axis_index("y"), lax.axis_index("c")


def _exchange_copies(kind, x_refs, out_refs, send_sems, recv_sems, local_sems):
    x_, y_, c_ = _my_pos()
    me = 4 * x_ + 2 * y_ + c_
    own = kind == "gather"
    cps = [pltpu.make_async_copy(x_refs[a] if own else x_refs[a].at[me], out_refs[a].at[me], local_sems.at[a])
           for a in range(len(x_refs))]
    for rel in range(1, NDEV):
        px = (x_ + ((rel >> 2) & 1)) % 2
        py = (y_ + ((rel >> 1) & 1)) % 2
        pc = (c_ + (rel & 1)) % 2
        for a in range(len(x_refs)):
            cps.append(pltpu.make_async_remote_copy(
                src_ref=x_refs[a] if own else x_refs[a].at[4 * px + 2 * py + pc], dst_ref=out_refs[a].at[me],
                send_sem=send_sems.at[rel - 1, a], recv_sem=recv_sems.at[rel - 1, a],
                device_id=(px, py, pc), device_id_type=pl.DeviceIdType.MESH))
    return cps


def _exchange_scratch(n):
    return [pltpu.SemaphoreType.DMA((NDEV - 1, n)), pltpu.SemaphoreType.DMA((NDEV - 1, n)), pltpu.SemaphoreType.DMA((n,))]


def _call(body, *, name, grid, in_specs, out_specs, out_shape, args, scratch=(), side=None):
    params = _cp(("arbitrary",) * len(grid))
    if side is None:
        return pl.pallas_call(body, name=name, grid=grid, in_specs=in_specs, out_specs=out_specs, out_shape=out_shape,
                              scratch_shapes=list(scratch), compiler_params=params)(*args)
    kind, xs = side
    n_in, n_out, n_scr, ns = len(in_specs), len(out_shape), len(scratch), len(xs)
    steps = 1
    for g in grid:
        steps *= g

    def wrapped(*refs):
        o0 = n_in + ns
        s0 = o0 + n_out + ns
        step = pl.program_id(0)
        for d in range(1, len(grid)):
            step = step * grid[d] + pl.program_id(d)

        def copies():
            return _exchange_copies(kind, refs[n_in:o0], refs[o0 + n_out:s0], *refs[s0 + n_scr:])

        @pl.when(step == 0)
        def _():
            for cp in copies():
                cp.start()

        body(*refs[:n_in], *refs[o0:o0 + n_out], *refs[s0:s0 + n_scr])

        @pl.when(step == steps - 1)
        def _():
            for cp in copies():
                cp.wait()

    any_ = pl.BlockSpec(memory_space=pl.ANY)
    side_shapes = [SDS((NDEV,) + x.shape if kind == "gather" else x.shape, x.dtype) for x in xs]
    outs = pl.pallas_call(wrapped, name=name, grid=grid, in_specs=list(in_specs) + [any_] * ns,
                          out_specs=list(out_specs) + [any_] * ns, out_shape=list(out_shape) + side_shapes,
                          scratch_shapes=list(scratch) + _exchange_scratch(ns), compiler_params=params)(*args, *xs)
    return outs[:n_out], outs[n_out:]


def inproj_fwd(x, nw, scale, shift, W, tn, name, side=None):
    S, D = x.shape
    N = W.shape[1]
    tm = _rows(S, 1024)

    def body(x_ref, nw_ref, sc_ref, sh_ref, w_ref, proj_ref, h_ref):
        @pl.when(pl.program_id(1) == 0)
        def _():
            h_ref[...] = _rms_mod(x_ref[...], nw_ref[...], sc_ref[...], sh_ref[...]).astype(bf16)

        proj_ref[...] = jnp.dot(h_ref[...], w_ref[...], preferred_element_type=f32)

    vec = pl.BlockSpec((1, D), lambda i, j: (0, 0))
    return _call(
        body, name=name, grid=(S // tm, N // tn),
        in_specs=[pl.BlockSpec((tm, D), lambda i, j: (i, 0)), vec, vec, vec, pl.BlockSpec((D, tn), lambda i, j: (0, j))],
        out_specs=[pl.BlockSpec((tm, tn), lambda i, j: (i, j)), pl.BlockSpec((tm, D), lambda i, j: (i, 0))],
        out_shape=[SDS((S, N), f32), SDS((S, D), bf16)], args=(x, nw, scale, shift, W), side=side)


def inproj_bwd_x(x, nw, scale, shift, W, dproj, dx_res, tn, name, side=None):
    S, D = x.shape
    N = W.shape[1]
    tm = _rows(S, 1024)
    nj = N // tn

    def body(x_ref, nw_ref, sc_ref, sh_ref, w_ref, dp_ref, dxr_ref, dx_ref, dnw_ref, dsc_ref, dsh_ref, acc):
        i, j = pl.program_id(0), pl.program_id(1)

        @pl.when(j == 0)
        def _():
            acc[...] = jnp.zeros_like(acc)

        @pl.when((i == 0) & (j == 0))
        def _():
            dnw_ref[...] = jnp.zeros_like(dnw_ref)
            dsc_ref[...] = jnp.zeros_like(dsc_ref)
            dsh_ref[...] = jnp.zeros_like(dsh_ref)

        acc[...] += _nt(dp_ref[...], w_ref[...])

        @pl.when(j == nj - 1)
        def _():
            _, vjp = jax.vjp(_rms_mod, x_ref[...], nw_ref[...], sc_ref[...], sh_ref[...])
            dx, dnw, dsc, dsh = vjp(acc[...])
            dx_ref[...] = dxr_ref[...] + dx
            dnw_ref[...] += dnw
            dsc_ref[...] += dsc
            dsh_ref[...] += dsh

    vec = pl.BlockSpec((1, D), lambda i, j: (0, 0))
    row = pl.BlockSpec((tm, D), lambda i, j: (i, 0))
    return _call(
        body, name=name, grid=(S // tm, nj),
        in_specs=[row, vec, vec, vec, pl.BlockSpec((D, tn), lambda i, j: (0, j)), pl.BlockSpec((tm, tn), lambda i, j: (i, j)), row],
        out_specs=[row, vec, vec, vec],
        out_shape=[SDS((S, D), f32), SDS((1, D), f32), SDS((1, D), f32), SDS((1, D), f32)],
        scratch=[pltpu.VMEM((tm, D), f32)], args=(x, nw, scale, shift, W, dproj, dx_res), side=side)


def matmul_tn(a, b, tn, name, side=None):
    S, K = a.shape
    N = b.shape[1]
    tm = _rows(S, 1024)
    ni = S // tm

    def body(a_ref, b_ref, o_ref):
        @pl.when(pl.program_id(1) == 0)
        def _():
            o_ref[...] = jnp.zeros_like(o_ref)

        o_ref[...] += _tn(a_ref[...], b_ref[...])

    return _call(
        body, name=name, grid=(N // tn, ni),
        in_specs=[pl.BlockSpec((tm, K), lambda j, i: (i, 0)), pl.BlockSpec((tm, tn), lambda j, i: (i, j))],
        out_specs=[pl.BlockSpec((K, tn), lambda j, i: (0, j))],
        out_shape=[SDS((K, N), f32)], args=(a, b), side=side)


def _conv_taps(xs, w, n_out):
    taps = []
    for j in range(4):
        s = 3 - j
        sh = xs if s == 0 else pltpu.roll(xs, s, axis=0)
        taps.append(sh[8:8 + n_out])
    conv = taps[0] * w[0] + taps[1] * w[1] + taps[2] * w[2] + taps[3] * w[3]
    return taps, conv


def _act_norm(conv, mul):
    s = _silu(conv)
    return s * (mul * lax.rsqrt(jnp.sum(s * s, axis=-1, keepdims=True) + EPS))


def gdn_prep_fwd(proj, conv_w, name):
    S = proj.shape[0]
    R = _rows(S, 512)

    def body(x_ref, w_ref, o_ref):
        j = pl.program_id(0)
        w = [w_ref[t:t + 1, :] for t in range(4)]

        def sweep(act):
            def piece(r, c):
                t0 = pl.multiple_of(r * R, R)
                cur = x_ref[pl.ds(t0, R), :]
                prev = x_ref[pl.ds(pl.multiple_of(jnp.maximum(t0 - 8, 0), 8), 8), :]
                prev = jnp.where(r == 0, 0.0, prev)
                _, conv = _conv_taps(jnp.concatenate([prev, cur], axis=0), w, R)
                o_ref[pl.ds(t0, R), :] = act(conv)
                return c

            lax.fori_loop(0, S // R, piece, 0)

        @pl.when(j < 2 * GDN_QK_HEADS)
        def _():
            sweep(lambda c: _act_norm(c, jnp.where(j < GDN_QK_HEADS, HD ** -0.5, 1.0).astype(f32)))

        @pl.when(j >= 2 * GDN_QK_HEADS)
        def _():
            sweep(_silu)

    return pl.pallas_call(
        body, name=name, grid=(GDN_CONV // 128,),
        in_specs=[pl.BlockSpec((S, 128), lambda j: (0, j)), pl.BlockSpec((4, 128), lambda j: (0, j))],
        out_specs=pl.BlockSpec((S, 128), lambda j: (0, j)),
        out_shape=SDS((S, GDN_CONV), f32),
        compiler_params=_cp(("arbitrary",)),
    )(proj, conv_w)


def gdn_prep_bwd(proj, conv_w, dqkvc, dproj, name):
    S = proj.shape[0]
    R = _rows(S, 512)
    NP = S // R

    def body(x_ref, w_ref, dn_ref, _, dx_ref, dw_ref):
        jb = pl.program_id(0)
        w = [w_ref[j:j + 1, :] for j in range(4)]

        def piece(act, r, dw):
            t0 = pl.multiple_of(r * R, R)
            cur = x_ref[pl.ds(t0, R), :]
            prev = x_ref[pl.ds(pl.multiple_of(jnp.maximum(t0 - 8, 0), 8), 8), :]
            prev = jnp.where(r == 0, 0.0, prev)
            nxt0 = pl.multiple_of(jnp.minimum(t0 + R, S - 8), 8)
            nxt = x_ref[pl.ds(nxt0, 8), :]
            dn_cur = dn_ref[pl.ds(t0, R), :]
            dn_nxt = jnp.where(r == NP - 1, 0.0, dn_ref[pl.ds(nxt0, 8), :])
            xs = jnp.concatenate([prev, cur, nxt], axis=0)
            taps, conv = _conv_taps(xs, w, R + 8)
            dn = jnp.concatenate([dn_cur, dn_nxt], axis=0)
            _, vjp = jax.vjp(act, conv)
            dxc = vjp(dn)[0]
            n = R + 8
            dx = dxc[0:R] * w[3]
            for j in range(3):
                s = 3 - j
                dx = dx + pltpu.roll(dxc, n - s, axis=0)[0:R] * w[j]
            dx_ref[pl.ds(t0, R), :] = dx
            return tuple(dw[j] + jnp.sum(dxc[0:R] * taps[j][0:R], axis=0, keepdims=True) for j in range(4))

        def sweep(act):
            dw = lax.fori_loop(0, NP, functools.partial(piece, act), tuple(jnp.zeros((1, 128), f32) for _ in range(4)))
            for j in range(4):
                dw_ref[j:j + 1, :] = dw[j]

        @pl.when(jb < 2 * GDN_QK_HEADS)
        def _():
            sweep(lambda c: _act_norm(c, jnp.where(jb < GDN_QK_HEADS, HD ** -0.5, 1.0).astype(f32)))

        @pl.when(jb >= 2 * GDN_QK_HEADS)
        def _():
            sweep(_silu)

    col = pl.BlockSpec((S, 128), lambda j: (0, j))
    wsp = pl.BlockSpec((4, 128), lambda j: (0, j))
    return pl.pallas_call(
        body, name=name, grid=(GDN_CONV // 128,),
        in_specs=[col, wsp, col, pl.BlockSpec(memory_space=pl.ANY)], out_specs=[col, wsp],
        out_shape=[SDS(dproj.shape, f32), SDS((4, GDN_CONV), f32)],
        input_output_aliases={3: 0},
        compiler_params=_cp(("arbitrary",)),
    )(proj, conv_w, dqkvc, dproj)


def _chunk_tril(R):
    ii = lax.broadcasted_iota(jnp.int32, (R, R), 0)
    jj = lax.broadcasted_iota(jnp.int32, (R, R), 1)
    return ((ii // CHUNK == jj // CHUNK) & (ii >= jj)).astype(f32)


def _gdn_gates(b, a, A_log, dt_bias, tril):
    beta = jax.nn.sigmoid(b)
    g = -jnp.exp(A_log) * jax.nn.softplus(a + dt_bias)
    return _hdot(tril, g), beta


_GDN_B_BLK = (GDN_CONV + GDN_V_W) // 128
_GDN_A_BLK = _GDN_B_BLK + 1


def gdn_gates_fwd(proj, A_log, dt_bias, name):
    S = proj.shape[0]
    R = _rows(S, 512)

    def body(b_ref, a_ref, al_ref, dt_ref, gc_ref, be_ref):
        gc, be = _gdn_gates(b_ref[...], a_ref[...], al_ref[...], dt_ref[...], _chunk_tril(R))
        gc_ref[...] = gc
        be_ref[...] = be

    vec = pl.BlockSpec((1, 128), lambda i: (0, 0))
    blk = pl.BlockSpec((R, 128), lambda i: (i, 0))
    return pl.pallas_call(
        body, name=name, grid=(S // R,),
        in_specs=[pl.BlockSpec((R, 128), lambda i: (i, _GDN_B_BLK)), pl.BlockSpec((R, 128), lambda i: (i, _GDN_A_BLK)), vec, vec],
        out_specs=[blk, blk], out_shape=[SDS((S, 128), f32), SDS((S, 128), f32)],
        compiler_params=_cp(("arbitrary",)),
    )(proj, proj, A_log, dt_bias)


def gdn_gates_bwd(proj, A_log, dt_bias, dgc, dbeta, dproj, name):
    S = proj.shape[0]
    R = _rows(S, 512)

    def body(b_ref, a_ref, al_ref, dt_ref, dgc_ref, dbe_ref, _, dp_ref, dal_ref, ddt_ref):
        @pl.when(pl.program_id(0) == 0)
        def _():
            dal_ref[...] = jnp.zeros_like(dal_ref)
            ddt_ref[...] = jnp.zeros_like(ddt_ref)

        tril = _chunk_tril(R)
        _, vjp = jax.vjp(lambda b, a, al, dt: _gdn_gates(b, a, al, dt, tril), b_ref[...], a_ref[...], al_ref[...], dt_ref[...])
        db, da, dal, ddt = vjp((dgc_ref[...], dbe_ref[...]))
        dp_ref[:, 0:128] = db
        dp_ref[:, 128:256] = da
        dal_ref[...] += dal
        ddt_ref[...] += ddt

    vec = pl.BlockSpec((1, 128), lambda i: (0, 0))
    blk = pl.BlockSpec((R, 128), lambda i: (i, 0))
    return pl.pallas_call(
        body, name=name, grid=(S // R,),
        in_specs=[pl.BlockSpec((R, 128), lambda i: (i, _GDN_B_BLK)), pl.BlockSpec((R, 128), lambda i: (i, _GDN_A_BLK)), vec, vec, blk, blk,
                  pl.BlockSpec(memory_space=pl.ANY)],
        out_specs=[pl.BlockSpec((R, 256), lambda i: (i, _GDN_B_BLK // 2)), vec, vec],
        out_shape=[SDS(dproj.shape, f32), SDS((1, 128), f32), SDS((1, 128), f32)],
        input_output_aliases={6: 0},
        compiler_params=_cp(("arbitrary",)),
    )(proj, proj, A_log, dt_bias, dgc, dbeta, dproj)


@jax.custom_vjp
def _inv_given(L, T):
    return T


def _inv_given_bwd(T, ct):
    dL = -_nt(_tn(T, ct), T)
    return dL, jnp.zeros_like(T)


_inv_given.defvjp(lambda L, T: (T, T), _inv_given_bwd)


REP = GDN_V_HEADS // GDN_QK_HEADS


def _gdn_intra_all(qs, ks, vs, gcols, bcols, Ts=None):
    H = len(vs)
    C = vs[0].shape[0]
    ii = lax.broadcasted_iota(jnp.int32, (C, C), 0)
    jj = lax.broadcasted_iota(jnp.int32, (C, C), 1)
    grows = [jnp.sum(jnp.where(ii == jj, g, 0.0), axis=0, keepdims=True) for g in gcols]
    decs = [jnp.exp(jnp.where(ii >= jj, gcols[h] - grows[h], NEG)) for h in range(H)]
    kbs = [ks[h // REP] * bcols[h] for h in range(H)]
    As = [_mm_nt(kbs[h], ks[h // REP]) for h in range(H)]
    Ls = [jnp.where(ii > jj, As[h] * decs[h], 0.0) for h in range(H)]
    if Ts is None:
        T = _neumann_inv_batched(Ls)
    else:
        T = [_inv_given(Ls[h], Ts[h]) for h in range(H)]
    us = [_mm(T[h], vs[h] * bcols[h]) for h in range(H)]
    ws = [_mm(T[h], kbs[h] * jnp.exp(gcols[h])) for h in range(H)]
    qk = [_mm_nt(qs[p], ks[p]) for p in range(H // REP)]
    return us, ws, [qk[h // REP] * decs[h] for h in range(H)], T


def _neumann_inv_batched(Ls):
    n, C = 4, Ls[0].shape[0]
    r0 = lax.broadcasted_iota(jnp.int32, (n * C, n * C), 0)
    c0 = lax.broadcasted_iota(jnp.int32, (n * C, n * C), 1)
    same = (r0 // C) == (c0 // C)

    def blockdiag(split):
        return tuple(jnp.where(same, jnp.concatenate([x] * n, axis=0), jnp.zeros((), bf16)) for x in split)

    Ms = [jnp.concatenate(Ls[b:b + n], axis=1) for b in range(0, len(Ls), n)]
    eye = (lax.broadcasted_iota(jnp.int32, (C, n * C), 0) == (lax.broadcasted_iota(jnp.int32, (C, n * C), 1) & (C - 1))).astype(f32)
    Ps = [eye - M for M in Ms]
    Ss = [_split2(M) for M in Ms]
    Bs = [blockdiag(S) for S in Ss]
    k = 1
    while 2 * k < C:
        Ss = [_split2(_dot3(S, B)) for S, B in zip(Ss, Bs)]
        Bs = [blockdiag(S) for S in Ss]
        Ps = [P + _dot3(_split2(P), B) for P, B in zip(Ps, Bs)]
        k *= 2
    return [P[:, h * C:(h + 1) * C] for P in Ps for h in range(n)]


def _gdn_scan_all(qs, ks, gcols, us, ws, attns, S0s):
    H = len(us)
    C = us[0].shape[0]
    last = lax.broadcasted_iota(jnp.int32, (C, 1), 0) == C - 1
    glast = [jnp.sum(jnp.where(last, g, 0.0), axis=0, keepdims=True) for g in gcols]
    wS = [_mm(ws[h], S0s[h]) for h in range(H)]
    qS = [_mm(qs[h // REP] * jnp.exp(gcols[h]), S0s[h]) for h in range(H)]
    vn = [us[h] - wS[h] for h in range(H)]
    av = [_mm(attns[h], vn[h]) for h in range(H)]
    kv = [_mm_tn(ks[h // REP] * jnp.exp(glast[h] - gcols[h]), vn[h]) for h in range(H)]
    return [qS[h] + av[h] for h in range(H)], [S0s[h] * jnp.exp(glast[h]) + kv[h] for h in range(H)]


def _head_cols(blk):
    lane = lax.broadcasted_iota(jnp.int32, blk.shape, 1)
    return [jnp.sum(jnp.where(lane == h, blk, 0.0), axis=1, keepdims=True) for h in range(GDN_V_HEADS)]


def _head_lanes(cols):
    lane = lax.broadcasted_iota(jnp.int32, (cols[0].shape[0], 128), 1)
    out = jnp.zeros((cols[0].shape[0], 128), f32)
    for h, c in enumerate(cols):
        out = out + jnp.where(lane == h, c, 0.0)
    return out


def _heads(ref, n):
    return [ref[:, h * HD:(h + 1) * HD] for h in range(n)]


def _gdn_specs(NC, rv=None):
    ix = (lambda n: n) if rv is None else rv
    qs = pl.BlockSpec((CHUNK, GDN_QK_W), lambda n: (ix(n), 0))
    ks = pl.BlockSpec((CHUNK, GDN_QK_W), lambda n: (ix(n), 1))
    vs = pl.BlockSpec((CHUNK, GDN_V_W), lambda n: (ix(n), 1))
    g1 = pl.BlockSpec((CHUNK, 128), lambda n: (ix(n), 0))
    wide = pl.BlockSpec((CHUNK, GDN_V_W), lambda n: (ix(n), 0))
    sq = pl.BlockSpec((1, GDN_V_HEADS, CHUNK, CHUNK), lambda n: (ix(n), 0, 0, 0))
    st = pl.BlockSpec((1, GDN_V_HEADS, HD, HD), lambda n: (ix(n), 0, 0, 0))
    return qs, ks, vs, g1, wide, sq, st


def gdn_intra_fwd(qkvc, gc, beta, name, side=None):
    S = qkvc.shape[0]
    NC = S // CHUNK

    def body(q_ref, k_ref, v_ref, gc_ref, be_ref, u_ref, w_ref, at_ref, T_ref):
        us, ws, attns, Ts = _gdn_intra_all(_heads(q_ref, GDN_QK_HEADS), _heads(k_ref, GDN_QK_HEADS), _heads(v_ref, GDN_V_HEADS),
                                           _head_cols(gc_ref[...]), _head_cols(be_ref[...]))
        for h in range(GDN_V_HEADS):
            u_ref[:, h * HD:(h + 1) * HD] = us[h]
            w_ref[:, h * HD:(h + 1) * HD] = ws[h]
            at_ref[0, h] = attns[h]
            T_ref[0, h] = Ts[h]

    qs, ks, vs, g1, wide, sq, _ = _gdn_specs(NC)
    return _call(
        body, name=name, grid=(NC,),
        in_specs=[qs, ks, vs, g1, g1], out_specs=[wide, wide, sq, sq],
        out_shape=[SDS((S, GDN_V_W), f32), SDS((S, GDN_V_W), f32),
                   SDS((NC, GDN_V_HEADS, CHUNK, CHUNK), f32), SDS((NC, GDN_V_HEADS, CHUNK, CHUNK), f32)],
        args=(qkvc, qkvc, qkvc, gc, beta), side=side)


def gdn_scan_fwd(qkvc, gc, u, w, attn, name):
    S = qkvc.shape[0]
    NC = S // CHUNK

    def body(q_ref, k_ref, gc_ref, u_ref, w_ref, at_ref, o_ref, st_ref, state):
        @pl.when(pl.program_id(0) == 0)
        def _():
            state[...] = jnp.zeros_like(state)

        S0s = [state[h] for h in range(GDN_V_HEADS)]
        os_, S1s = _gdn_scan_all(_heads(q_ref, GDN_QK_HEADS), _heads(k_ref, GDN_QK_HEADS), _head_cols(gc_ref[...]),
                                 _heads(u_ref, GDN_V_HEADS), _heads(w_ref, GDN_V_HEADS),
                                 [at_ref[0, h] for h in range(GDN_V_HEADS)], S0s)
        for h in range(GDN_V_HEADS):
            o_ref[:, h * HD:(h + 1) * HD] = os_[h]
            st_ref[0, h] = S0s[h]
            state[h] = S1s[h]

    qs, ks, _, g1, wide, sq, st = _gdn_specs(NC)
    return pl.pallas_call(
        body, name=name, grid=(NC,),
        in_specs=[qs, ks, g1, wide, wide, sq], out_specs=[wide, st],
        out_shape=[SDS((S, GDN_V_W), f32), SDS((NC, GDN_V_HEADS, HD, HD), f32)],
        scratch_shapes=[pltpu.VMEM((GDN_V_HEADS, HD, HD), f32)],
        compiler_params=_cp(("arbitrary",)),
    )(qkvc, qkvc, gc, u, w, attn)


def gdn_scan_bwd(qkvc, gc, u, w, attn, states, do, name, side=None):
    S = qkvc.shape[0]
    NC = S // CHUNK

    def body(q_ref, k_ref, gc_ref, u_ref, w_ref, at_ref, st_ref, do_ref,
             dq_ref, dk_ref, dgc_ref, du_ref, dw_ref, dat_ref, dstate):
        @pl.when(pl.program_id(0) == 0)
        def _():
            dstate[...] = jnp.zeros_like(dstate)

        VH = range(GDN_V_HEADS)
        _, vjp = jax.vjp(_gdn_scan_all, _heads(q_ref, GDN_QK_HEADS), _heads(k_ref, GDN_QK_HEADS), _head_cols(gc_ref[...]),
                         _heads(u_ref, GDN_V_HEADS), _heads(w_ref, GDN_V_HEADS), [at_ref[0, h] for h in VH],
                         [st_ref[0, h] for h in VH])
        dqs, dks, dgs, dus, dws, dats, dS0s = vjp((_heads(do_ref, GDN_V_HEADS), [dstate[h] for h in VH]))
        for p in range(GDN_QK_HEADS):
            dq_ref[:, p * HD:(p + 1) * HD] = dqs[p]
            dk_ref[:, p * HD:(p + 1) * HD] = dks[p]
        for h in VH:
            du_ref[:, h * HD:(h + 1) * HD] = dus[h]
            dw_ref[:, h * HD:(h + 1) * HD] = dws[h]
            dat_ref[0, h] = dats[h]
            dstate[h] = dS0s[h]
        dgc_ref[...] = _head_lanes(dgs)

    qs, ks, _, g1, wide, sq, st = _gdn_specs(NC, lambda n: NC - 1 - n)
    dqs = pl.BlockSpec((CHUNK, GDN_QK_W), lambda n: (NC - 1 - n, 0))
    return _call(
        body, name=name, grid=(NC,),
        in_specs=[qs, ks, g1, wide, wide, sq, st, wide],
        out_specs=[dqs, dqs, g1, wide, wide, sq],
        out_shape=[SDS((S, GDN_QK_W), f32), SDS((S, GDN_QK_W), f32), SDS((S, 128), f32), SDS((S, GDN_V_W), f32),
                   SDS((S, GDN_V_W), f32), SDS((NC, GDN_V_HEADS, CHUNK, CHUNK), f32)],
        scratch=[pltpu.VMEM((GDN_V_HEADS, HD, HD), f32)], args=(qkvc, qkvc, gc, u, w, attn, states, do), side=side)


def gdn_intra_bwd(qkvc, gc, beta, Ts, du, dw, dattn, dq_s, dk_s, dgc_s, name, side=None):
    S = qkvc.shape[0]
    NC = S // CHUNK

    def body(q_ref, k_ref, v_ref, gc_ref, be_ref, T_ref, du_ref, dw_ref, dat_ref, dqs_ref, dks_ref, dgs_ref,
             dqkv_ref, dgc_ref, dbe_ref):
        VH = range(GDN_V_HEADS)
        Ts = [T_ref[0, h] for h in VH]
        _, vjp = jax.vjp(lambda q_, k_, v_, g_, b_: _gdn_intra_all(q_, k_, v_, g_, b_, Ts)[:3],
                         _heads(q_ref, GDN_QK_HEADS), _heads(k_ref, GDN_QK_HEADS), _heads(v_ref, GDN_V_HEADS),
                         _head_cols(gc_ref[...]), _head_cols(be_ref[...]))
        dqs, dks, dvs, dgs, dbs = vjp((_heads(du_ref, GDN_V_HEADS), _heads(dw_ref, GDN_V_HEADS), [dat_ref[0, h] for h in VH]))
        for p in range(GDN_QK_HEADS):
            dqkv_ref[:, p * HD:(p + 1) * HD] = dqs[p] + dqs_ref[:, p * HD:(p + 1) * HD]
            dqkv_ref[:, GDN_QK_W + p * HD:GDN_QK_W + (p + 1) * HD] = dks[p] + dks_ref[:, p * HD:(p + 1) * HD]
        for h in VH:
            dqkv_ref[:, 2 * GDN_QK_W + h * HD:2 * GDN_QK_W + (h + 1) * HD] = dvs[h]
        dgc_ref[...] = _head_lanes(dgs) + dgs_ref[...]
        dbe_ref[...] = _head_lanes(dbs)

    qs, ks, vs, g1, wide, sq, _ = _gdn_specs(NC)
    dqs = pl.BlockSpec((CHUNK, GDN_QK_W), lambda n: (n, 0))
    return _call(
        body, name=name, grid=(NC,),
        in_specs=[qs, ks, vs, g1, g1, sq, wide, wide, sq, dqs, dqs, g1],
        out_specs=[pl.BlockSpec((CHUNK, GDN_CONV), lambda n: (n, 0)), g1, g1],
        out_shape=[SDS((S, GDN_CONV), f32), SDS((S, 128), f32), SDS((S, 128), f32)],
        args=(qkvc, qkvc, qkvc, gc, beta, Ts, du, dw, dattn, dq_s, dk_s, dgc_s), side=side)


def _gated_norm(o, z, nw):
    parts = []
    for h in range(GDN_V_HEADS):
        oh = o[:, h * HD:(h + 1) * HD]
        r = lax.rsqrt(jnp.mean(oh * oh, axis=-1, keepdims=True) + EPS)
        parts.append((oh * r * nw) * _silu(z[:, h * HD:(h + 1) * HD]))
    return jnp.concatenate(parts, axis=1)


def gdn_out_fwd(o, proj, nw, W, x, gate, name):
    S, D = x.shape
    tm = _rows(S, 256)

    def body(o_ref, z_ref, nw_ref, w_ref, x_ref, g_ref, xn_ref, y_ref, og_ref):
        og = _gated_norm(o_ref[...], z_ref[...], nw_ref[...]).astype(bf16)
        y = jnp.dot(og, w_ref[...], preferred_element_type=f32)
        og_ref[...] = og
        y_ref[...] = y
        xn_ref[...] = x_ref[...] + g_ref[...] * y

    row = pl.BlockSpec((tm, D), lambda i: (i, 0))
    wide = pl.BlockSpec((tm, GDN_V_W), lambda i: (i, 0))
    return pl.pallas_call(
        body, name=name, grid=(S // tm,),
        in_specs=[wide, pl.BlockSpec((tm, GDN_V_W), lambda i: (i, 2)), pl.BlockSpec((1, HD), lambda i: (0, 0)),
                  pl.BlockSpec((GDN_V_W, D), lambda i: (0, 0)), row, pl.BlockSpec((1, D), lambda i: (0, 0))],
        out_specs=[row, row, wide],
        out_shape=[SDS((S, D), f32), SDS((S, D), f32), SDS((S, GDN_V_W), bf16)],
        compiler_params=_cp(("arbitrary",)),
    )(o, proj, nw, W, x, gate)


def gdn_out_bwd(dxn, y, gate, o, proj, nw, W, name):
    S, D = dxn.shape
    tm = _rows(S, 256)

    def body(dx_ref, y_ref, g_ref, o_ref, z_ref, nw_ref, w_ref, dy_ref, dg_ref, do_ref, dz_ref, dnw_ref):
        @pl.when(pl.program_id(0) == 0)
        def _():
            dg_ref[...] = jnp.zeros_like(dg_ref)
            dnw_ref[...] = jnp.zeros_like(dnw_ref)

        dx = dx_ref[...]
        dy = dx * g_ref[...]
        dy_ref[...] = dy
        dg_ref[...] += jnp.sum(dx * y_ref[...], axis=0, keepdims=True)
        dog = _nt(dy, w_ref[...])
        _, vjp = jax.vjp(_gated_norm, o_ref[...], z_ref[...], nw_ref[...])
        do, dz, dnw = vjp(dog)
        do_ref[...] = do
        dz_ref[...] = dz
        dnw_ref[...] += dnw

    row = pl.BlockSpec((tm, D), lambda i: (i, 0))
    wide = pl.BlockSpec((tm, GDN_V_W), lambda i: (i, 0))
    vecd = pl.BlockSpec((1, D), lambda i: (0, 0))
    vech = pl.BlockSpec((1, HD), lambda i: (0, 0))
    return pl.pallas_call(
        body, name=name, grid=(S // tm,),
        in_specs=[row, row, vecd, wide, pl.BlockSpec((tm, GDN_V_W), lambda i: (i, 2)), vech, pl.BlockSpec((GDN_V_W, D), lambda i: (0, 0))],
        out_specs=[row, vecd, wide, pl.BlockSpec((tm, GDN_V_W), lambda i: (i, 2)), vech],
        out_shape=[SDS((S, D), f32), SDS((1, D), f32), SDS((S, GDN_V_W), f32), SDS((S, GDN_IN_PAD), f32), SDS((1, HD), f32)],
        compiler_params=_cp(("arbitrary",)),
    )(dxn, y, gate, o, proj, nw, W)


def _rms_w(x, w):
    return (x * lax.rsqrt(jnp.mean(x * x, axis=-1, keepdims=True) + EPS)) * w


def _split3(c):
    hi = c.astype(bf16).astype(f32)
    r1 = c - hi
    mid = r1.astype(bf16).astype(f32)
    lo = (r1 - mid).astype(bf16).astype(f32)
    return hi, mid, lo


_FOX_F_BLK = 4 * FOX_W // 128


def fox_prep_fwd(proj, f_bias, qn_w, kn_w, name):
    S = proj.shape[0]
    tm = _rows(S, 256)

    def body(q_ref, k_ref, v_ref, f_ref, fb_ref, qw_ref, kw_ref, Q_ref, K_ref, V_ref, carry):
        @pl.when(pl.program_id(0) == 0)
        def _():
            carry[...] = jnp.zeros_like(carry)

        ii = lax.broadcasted_iota(jnp.int32, (tm, tm), 0)
        jj = lax.broadcasted_iota(jnp.int32, (tm, tm), 1)
        lf = jax.nn.log_sigmoid(f_ref[...] + fb_ref[...])
        cum = _hdot((ii >= jj).astype(f32), lf) + carry[...]
        carry[...] = cum[tm - 1:tm, :]
        lane = lax.broadcasted_iota(jnp.int32, (tm, 128), 1)
        lo = lane < FOX_D
        qw2 = jnp.concatenate([qw_ref[...], qw_ref[...]], axis=1) * FOX_D ** -0.5
        kw2 = jnp.concatenate([kw_ref[...], kw_ref[...]], axis=1)

        def norm_pair(x, w2):
            x2 = x * x
            s_all = jnp.sum(x2, axis=1, keepdims=True)
            s_lo = jnp.sum(jnp.where(lo, x2, 0.0), axis=1, keepdims=True)
            r = jnp.where(lo, lax.rsqrt(s_lo * (1.0 / FOX_D) + EPS), lax.rsqrt((s_all - s_lo) * (1.0 / FOX_D) + EPS))
            return x * r * w2

        for p in range(FOX_H // 2):
            ps = slice(p * 128, (p + 1) * 128)
            yq, yk, xv = norm_pair(q_ref[:, ps], qw2), norm_pair(k_ref[:, ps], kw2), v_ref[:, ps]
            for e in range(2):
                h = 2 * p + e
                hi, mid, lw = _split3(cum[:, h:h + 1])
                eq = jnp.where(lane == FOX_D, hi, jnp.where(lane == FOX_D + 1, mid, jnp.where(lane == FOX_D + 2, lw, jnp.where(lane < FOX_D + 6, 1.0, 0.0))))
                ek = jnp.where(lane < FOX_D + 3, 1.0, jnp.where(lane == FOX_D + 3, -hi, jnp.where(lane == FOX_D + 4, -mid, jnp.where(lane == FOX_D + 5, -lw, 0.0))))
                ev = jnp.where(lane == FOX_D, 1.0, 0.0)
                mv = (lambda a: a) if e == 0 else (lambda a: pltpu.roll(a, FOX_D, axis=1))
                Q_ref[:, h * 128:(h + 1) * 128] = jnp.where(lo, mv(yq), eq).astype(bf16)
                K_ref[:, h * 128:(h + 1) * 128] = jnp.where(lo, mv(yk), ek).astype(bf16)
                V_ref[:, h * 128:(h + 1) * 128] = jnp.where(lo, mv(xv), ev).astype(bf16)

    def colblk(c):
        return pl.BlockSpec((tm, FOX_W), lambda i: (i, c))

    pad = pl.BlockSpec((tm, FOX_PW), lambda i: (i, 0))
    return pl.pallas_call(
        body, name=name, grid=(S // tm,),
        in_specs=[colblk(0), colblk(1), colblk(2), pl.BlockSpec((tm, 128), lambda i: (i, _FOX_F_BLK)),
                  pl.BlockSpec((1, 128), lambda i: (0, 0)), pl.BlockSpec((1, FOX_D), lambda i: (0, 0)), pl.BlockSpec((1, FOX_D), lambda i: (0, 0))],
        out_specs=[pad, pad, pad],
        out_shape=[SDS((S, FOX_PW), bf16)] * 3,
        scratch_shapes=[pltpu.VMEM((1, 128), f32)],
        compiler_params=_cp(("arbitrary",)),
    )(proj, proj, proj, proj, f_bias, qn_w, kn_w)


def fox_prep_bwd(proj, f_bias, qn_w, kn_w, dQ, dK, dV, dz, name):
    S = proj.shape[0]
    tm = _rows(S, 256)
    NB = S // tm

    def body(q_ref, k_ref, f_ref, fb_ref, qw_ref, kw_ref, dQ_ref, dK_ref, dV_ref, dz_ref,
             dp_ref, dfb_ref, dqw_ref, dkw_ref, carry):
        @pl.when(pl.program_id(0) == 0)
        def _():
            carry[...] = jnp.zeros_like(carry)
            dfb_ref[...] = jnp.zeros_like(dfb_ref)
            dqw_ref[...] = jnp.zeros_like(dqw_ref)
            dkw_ref[...] = jnp.zeros_like(dkw_ref)

        lane = lax.broadcasted_iota(jnp.int32, (tm, 128), 1)
        lo = lane < FOX_D
        qw2 = jnp.concatenate([qw_ref[...], qw_ref[...]], axis=1) * FOX_D ** -0.5
        kw2 = jnp.concatenate([kw_ref[...], kw_ref[...]], axis=1)

        def pair(ref, p):
            return jnp.where(lo, ref[:, 2 * p * 128:(2 * p + 1) * 128], pltpu.roll(ref[:, (2 * p + 1) * 128:(2 * p + 2) * 128], FOX_D, axis=1))

        def norm_pair_bwd(x, w2, dy):
            x2 = x * x
            s_all = jnp.sum(x2, axis=1, keepdims=True)
            s_lo = jnp.sum(jnp.where(lo, x2, 0.0), axis=1, keepdims=True)
            r = jnp.where(lo, lax.rsqrt(s_lo * (1.0 / FOX_D) + EPS), lax.rsqrt((s_all - s_lo) * (1.0 / FOX_D) + EPS))
            t = dy * w2 * x
            t_all = jnp.sum(t, axis=1, keepdims=True)
            t_lo = jnp.sum(jnp.where(lo, t, 0.0), axis=1, keepdims=True)
            dx = r * (w2 * dy - x * (r * r) * (jnp.where(lo, t_lo, t_all - t_lo) * (1.0 / FOX_D)))
            return dx, jnp.sum(dy * x * r, axis=0, keepdims=True)

        dcum = jnp.zeros((tm, 128), f32)
        dqw2 = jnp.zeros((1, 128), f32)
        dkw2 = jnp.zeros((1, 128), f32)
        for p in range(FOX_H // 2):
            ps = slice(p * 128, (p + 1) * 128)
            dxq, dw1 = norm_pair_bwd(q_ref[:, ps], qw2, pair(dQ_ref, p))
            dxk, dw2 = norm_pair_bwd(k_ref[:, ps], kw2, pair(dK_ref, p))
            dp_ref[:, p * 128:(p + 1) * 128] = dxq
            dp_ref[:, FOX_W + p * 128:FOX_W + (p + 1) * 128] = dxk
            dp_ref[:, 2 * FOX_W + p * 128:2 * FOX_W + (p + 1) * 128] = pair(dV_ref, p)
            dqw2 = dqw2 + dw1
            dkw2 = dkw2 + dw2
            for e in range(2):
                h = 2 * p + e
                dcum = dcum + jnp.where(lane == h, dQ_ref[:, h * 128 + FOX_D:h * 128 + FOX_D + 1]
                                        - dK_ref[:, h * 128 + FOX_D + 3:h * 128 + FOX_D + 4], 0.0)
        dp_ref[:, 3 * FOX_W:4 * FOX_W] = dz_ref[...]
        ii = lax.broadcasted_iota(jnp.int32, (tm, tm), 0)
        jj = lax.broadcasted_iota(jnp.int32, (tm, tm), 1)
        dlf = _hdot((ii <= jj).astype(f32), dcum) + carry[...]
        carry[...] += jnp.sum(dcum, axis=0, keepdims=True)
        df = dlf * jax.nn.sigmoid(-(f_ref[...] + fb_ref[...]))
        dp_ref[:, 4 * FOX_W:FOX_IN_PAD] = df
        dfb_ref[...] += jnp.sum(df, axis=0, keepdims=True)
        dqw_ref[...] += (dqw2[:, :FOX_D] + dqw2[:, FOX_D:]) * FOX_D ** -0.5
        dkw_ref[...] += dkw2[:, :FOX_D] + dkw2[:, FOX_D:]

    rv = lambda i: NB - 1 - i

    def colblk(c):
        return pl.BlockSpec((tm, FOX_W), lambda i: (rv(i), c))

    pad = pl.BlockSpec((tm, FOX_PW), lambda i: (rv(i), 0))
    cmp_ = pl.BlockSpec((tm, FOX_W), lambda i: (rv(i), 0))
    v128 = pl.BlockSpec((1, 128), lambda i: (0, 0))
    v64 = pl.BlockSpec((1, FOX_D), lambda i: (0, 0))
    return pl.pallas_call(
        body, name=name, grid=(NB,),
        in_specs=[colblk(0), colblk(1), pl.BlockSpec((tm, 128), lambda i: (rv(i), _FOX_F_BLK)), v128, v64, v64, pad, pad, pad, cmp_],
        out_specs=[pl.BlockSpec((tm, FOX_IN_PAD), lambda i: (rv(i), 0)), v128, v64, v64],
        out_shape=[SDS((S, FOX_IN_PAD), f32), SDS((1, 128), f32), SDS((1, FOX_D), f32), SDS((1, FOX_D), f32)],
        scratch_shapes=[pltpu.VMEM((1, 128), f32)],
        compiler_params=_cp(("arbitrary",)),
    )(proj, proj, proj, f_bias, qn_w, kn_w, dQ, dK, dV, dz)


FOX_HB = 2


def _diag_mask(t):
    return lax.broadcasted_iota(jnp.int32, (t, t), 1) <= lax.broadcasted_iota(jnp.int32, (t, t), 0)


def fox_attn_fwd(Q, K, V, name, side=None):
    S = Q.shape[0]
    t = _rows(S, 512)

    HB = FOX_HB
    HS = [slice(h * 128, (h + 1) * 128) for h in range(HB)]

    def body(q_ref, k_ref, v_ref, o_ref, m_sc, acc_sc, s_sc):
        i = pl.program_id(1)
        qs = [q_ref[:, sl] for sl in HS]
        m_sc[...] = jnp.full_like(m_sc, NEG)
        acc_sc[...] = jnp.zeros_like(acc_sc)

        def scores(j):
            j0 = pl.multiple_of(j * t, t)
            return [_nt(qs[h], k_ref[pl.ds(j0, t), HS[h]]) for h in range(HB)]

        def tile(j, diag):
            j0 = pl.multiple_of(j * t, t)
            ss = [s_sc[h] for h in range(HB)]
            if diag:
                ss = [jnp.where(_diag_mask(t), s, NEG) for s in ss]
            else:
                nxt = scores(j + 1)
            ms = [m_sc[h] for h in range(HB)]
            m_new = [jnp.maximum(ms[h], jnp.max(ss[h], axis=1, keepdims=True)) for h in range(HB)]
            ps = [jnp.exp(ss[h] - m_new[h]) for h in range(HB)]
            pv = [_nn(ps[h], v_ref[pl.ds(j0, t), HS[h]]) for h in range(HB)]
            for h in range(HB):
                acc_sc[h] = acc_sc[h] * jnp.exp(ms[h] - m_new[h]) + pv[h]
                m_sc[h] = m_new[h]
                if not diag:
                    s_sc[h] = nxt[h]

        def off_diag(j, c):
            tile(j, False)
            return c

        first = scores(0)
        for h in range(HB):
            s_sc[h] = first[h]
        lax.fori_loop(0, i, off_diag, 0)
        tile(i, True)
        lane = lax.broadcasted_iota(jnp.int32, (t, 128), 1)
        for h in range(HB):
            acc = acc_sc[h]
            l = acc[:, FOX_D:FOX_D + 1]
            o_ref[:, HS[h]] = jnp.where(lane == FOX_D, m_sc[h] + jnp.log(l), acc / l)

    blk = pl.BlockSpec((t, HB * 128), lambda h, i: (i, h))
    seq = pl.BlockSpec((S, HB * 128), lambda h, i: (0, h))
    return _call(
        body, name=name, grid=(FOX_H // HB, S // t),
        in_specs=[blk, seq, seq], out_specs=[blk], out_shape=[SDS((S, FOX_PW), f32)],
        scratch=[pltpu.VMEM((HB, t, 1), f32), pltpu.VMEM((HB, t, 128), f32), pltpu.VMEM((HB, t, t), f32)],
        args=(Q, K, V), side=side)


def fox_attn_bwd(Q, K, V, dO, O, name, side=None):
    S = Q.shape[0]
    t = _rows(S, 512)
    nq = S // t

    HB = FOX_HB
    HS = [slice(h * 128, (h + 1) * 128) for h in range(HB)]

    def body(k_ref, v_ref, q_ref, do_ref, o_ref, dq_ref, dk_ref, dv_ref):
        j = pl.program_id(1)

        @pl.when(j == 0)
        def _():
            dq_ref[...] = jnp.zeros_like(dq_ref)

        dk_ref[...] = jnp.zeros_like(dk_ref)
        dv_ref[...] = jnp.zeros_like(dv_ref)
        ks = [k_ref[:, sl] for sl in HS]
        vs = [v_ref[:, sl] for sl in HS]

        def tile(i, diag):
            i0 = pl.multiple_of(i * t, t)
            R = range(HB)
            qs = [q_ref[pl.ds(i0, t), HS[h]] for h in R]
            dos = [do_ref[pl.ds(i0, t), HS[h]] for h in R]
            ss = [_nt(qs[h], ks[h]) - o_ref[pl.ds(i0, t), h * 128 + FOX_D:h * 128 + FOX_D + 1] for h in R]
            if diag:
                ss = [jnp.where(_diag_mask(t), s, NEG) for s in ss]
            ps = [jnp.exp(s) for s in ss]
            dps = [_nt(dos[h], vs[h]) for h in R]
            dvs = [_tn(ps[h], dos[h]) for h in R]
            dss = [(ps[h] * dps[h]).astype(bf16) for h in R]
            dks = [_tn(dss[h], qs[h]) for h in R]
            dqs = [_nn(dss[h], ks[h]) for h in R]
            for h in R:
                dv_ref[:, HS[h]] += dvs[h]
                dk_ref[:, HS[h]] += dks[h]
                dq_ref[pl.ds(i0, t), HS[h]] += dqs[h]

        tile(j, True)

        def off_diag(i, c):
            tile(i, False)
            return c

        lax.fori_loop(j + 1, nq, off_diag, 0)

    blk = pl.BlockSpec((t, HB * 128), lambda h, j: (j, h))
    seq = pl.BlockSpec((S, HB * 128), lambda h, j: (0, h))
    return _call(
        body, name=name, grid=(FOX_H // HB, nq),
        in_specs=[blk, blk, seq, seq, seq], out_specs=[seq, blk, blk],
        out_shape=[SDS((S, FOX_PW), f32)] * 3, args=(K, V, Q, dO, O), side=side)


def fox_out_fwd(O, proj, W, x, gate, name):
    S, D = x.shape
    tm = _rows(S, 256)

    def body(o_ref, z_ref, w_ref, x_ref, g_ref, xn_ref, y_ref, og_ref):
        z = z_ref[...]
        og = jnp.concatenate([o_ref[:, h * 128:h * 128 + FOX_D] * _silu(z[:, h * FOX_D:(h + 1) * FOX_D]) for h in range(FOX_H)],
                             axis=1).astype(bf16)
        y = jnp.dot(og, w_ref[...], preferred_element_type=f32)
        og_ref[...] = og
        y_ref[...] = y
        xn_ref[...] = x_ref[...] + g_ref[...] * y

    row = pl.BlockSpec((tm, D), lambda i: (i, 0))
    cmp_ = pl.BlockSpec((tm, FOX_W), lambda i: (i, 0))
    return pl.pallas_call(
        body, name=name, grid=(S // tm,),
        in_specs=[pl.BlockSpec((tm, FOX_PW), lambda i: (i, 0)), pl.BlockSpec((tm, FOX_W), lambda i: (i, 3)),
                  pl.BlockSpec((FOX_W, D), lambda i: (0, 0)), row, pl.BlockSpec((1, D), lambda i: (0, 0))],
        out_specs=[row, row, cmp_],
        out_shape=[SDS((S, D), f32), SDS((S, D), f32), SDS((S, FOX_W), bf16)],
        compiler_params=_cp(("arbitrary",)),
    )(O, proj, W, x, gate)


def fox_out_bwd(dxn, y, gate, O, proj, W, name):
    S, D = dxn.shape
    tm = _rows(S, 256)

    def body(dx_ref, y_ref, g_ref, o_ref, z_ref, w_ref, dy_ref, dg_ref, dO_ref, dz_ref):
        @pl.when(pl.program_id(0) == 0)
        def _():
            dg_ref[...] = jnp.zeros_like(dg_ref)

        dx = dx_ref[...]
        dy = dx * g_ref[...]
        dy_ref[...] = dy
        dg_ref[...] += jnp.sum(dx * y_ref[...], axis=0, keepdims=True)
        dog = _nt(dy, w_ref[...])
        z = z_ref[...]
        lane = lax.broadcasted_iota(jnp.int32, (tm, FOX_D), 1)
        dzs = []
        for h in range(FOX_H):
            sl = slice(h * FOX_D, (h + 1) * FOX_D)
            zh = z[:, sl]
            sg = jax.nn.sigmoid(zh)
            oh = o_ref[:, h * 128:h * 128 + FOX_D]
            doh = dog[:, sl] * (zh * sg)
            delta = jnp.sum(doh * oh, axis=1, keepdims=True)
            dO_ref[:, h * 128:(h + 1) * 128] = jnp.concatenate([doh, jnp.where(lane == 0, -delta, 0.0)], axis=1).astype(bf16)
            dzs.append(dog[:, sl] * oh * (sg * (1.0 + zh * (1.0 - sg))))
        dz_ref[...] = jnp.concatenate(dzs, axis=1)

    row = pl.BlockSpec((tm, D), lambda i: (i, 0))
    vecd = pl.BlockSpec((1, D), lambda i: (0, 0))
    pad = pl.BlockSpec((tm, FOX_PW), lambda i: (i, 0))
    return pl.pallas_call(
        body, name=name, grid=(S // tm,),
        in_specs=[row, row, vecd, pad, pl.BlockSpec((tm, FOX_W), lambda i: (i, 3)), pl.BlockSpec((FOX_W, D), lambda i: (0, 0))],
        out_specs=[row, vecd, pad, pl.BlockSpec((tm, FOX_W), lambda i: (i, 0))],
        out_shape=[SDS((S, D), f32), SDS((1, D), f32), SDS((S, FOX_PW), bf16), SDS((S, FOX_W), f32)],
        compiler_params=_cp(("arbitrary",)),
    )(dxn, y, gate, O, proj, W)


def final_loss(x, fw, target, name):
    S, D = x.shape
    tm = _rows(S, 512)

    def body(x_ref, w_ref, t_ref, l_ref, dx_ref, dw_ref):
        @pl.when(pl.program_id(0) == 0)
        def _():
            l_ref[...] = jnp.zeros_like(l_ref)
            dw_ref[...] = jnp.zeros_like(dw_ref)

        out, vjp = jax.vjp(_rms_w, x_ref[...], w_ref[...])
        err = out - t_ref[...]
        l_ref[...] += 0.5 * jnp.sum(jnp.sum(err * err, axis=1, keepdims=True) * (1.0 / D), axis=0, keepdims=True)
        dx, dw = vjp(err * (1.0 / D))
        dx_ref[...] = dx
        dw_ref[...] += dw

    row = pl.BlockSpec((tm, D), lambda i: (i, 0))
    vec = pl.BlockSpec((1, D), lambda i: (0, 0))
    return pl.pallas_call(
        body, name=name, grid=(S // tm,),
        in_specs=[row, vec, row], out_specs=[pl.BlockSpec((1, 128), lambda i: (0, 0)), row, vec],
        out_shape=[SDS((1, 128), f32), SDS((S, D), f32), SDS((1, D), f32)],
        compiler_params=_cp(("arbitrary",)),
    )(x, fw, target)


def ada_fwd(c_all, ada_w, name):
    L, D, n = ada_w.shape

    def body(c_ref, w_ref, o_ref):
        cond = jnp.concatenate([_silu(c_ref[...]), jnp.zeros((8, D), f32)], axis=0)
        o_ref[0] = _nn(cond, w_ref[0])[0:8]

    return pl.pallas_call(
        body, name=name, grid=(L,),
        in_specs=[pl.BlockSpec((NDEV, D), lambda l: (0, 0)), pl.BlockSpec((1, D, n), lambda l: (l, 0, 0))],
        out_specs=pl.BlockSpec((1, NDEV, n), lambda l: (l, 0, 0)),
        out_shape=SDS((L, NDEV, n), f32),
        compiler_params=_cp(("arbitrary",)),
    )(c_all, ada_w)


def ada_grad(c_all, dmod, name):
    L, _, n = dmod.shape
    D = c_all.shape[1]

    def body(c_ref, d_ref, o_ref):
        cond = jnp.concatenate([_silu(c_ref[...]), jnp.zeros((8, D), f32)], axis=0)
        dm = jnp.concatenate([d_ref[0], jnp.zeros((8, n), f32)], axis=0)
        o_ref[0] = _tn(cond, dm)

    return pl.pallas_call(
        body, name=name, grid=(L,),
        in_specs=[pl.BlockSpec((NDEV, D), lambda l: (0, 0)), pl.BlockSpec((1, NDEV, n), lambda l: (l, 0, 0))],
        out_specs=pl.BlockSpec((1, D, n), lambda l: (l, 0, 0)),
        out_shape=SDS((L, D, n), f32),
        compiler_params=_cp(("arbitrary",)),
    )(c_all, dmod)


def reduce_adam(parts, w, m, v, tr, name):
    n, R, C = parts.shape
    c1 = 1.0 / (1.0 - ADAM_B1 ** ADAM_STEP)
    c2 = 1.0 / (1.0 - ADAM_B2 ** ADAM_STEP)

    def body(p_ref, w_ref, m_ref, v_ref, g_ref, d_ref, nm_ref, nv_ref):
        g = p_ref[0].astype(f32)
        for s in range(1, n):
            g = g + p_ref[s].astype(f32)
        nm = ADAM_B1 * m_ref[...] + (1.0 - ADAM_B1) * g
        nv = ADAM_B2 * v_ref[...] + (1.0 - ADAM_B2) * (g * g)
        g_ref[...] = g
        nm_ref[...] = nm
        nv_ref[...] = nv
        d_ref[...] = -ADAM_LR * ((nm * c1) / (jnp.sqrt(nv * c2) + ADAM_EPS) + ADAM_WD * w_ref[...])

    blk = pl.BlockSpec((tr, C), lambda i: (i, 0))
    return pl.pallas_call(
        body, name=name, grid=(R // tr,),
        in_specs=[pl.BlockSpec((n, tr, C), lambda i: (0, i, 0)), blk, blk, blk],
        out_specs=[blk] * 4, out_shape=[SDS((R, C), f32)] * 4,
        compiler_params=_cp(("arbitrary",)),
    )(parts, w, m, v)


def all_gather(xs, name):
    n = len(xs)

    def body(*refs):
        x_refs, out_refs = refs[:n], refs[n:2 * n]
        send_sems, recv_sems, local_sems = refs[2 * n:]
        x_, y_, c_ = _my_pos()
        me, sibling = (x_, y_, c_), (x_, y_, 1 - c_)
        chips = [(1 - x_, y_), (x_, 1 - y_), (1 - x_, 1 - y_)]

        def rows(a, px, py, pc):
            return out_refs[a].at[4 * px + 2 * py + pc]

        def copy(a, k, block, to, own=False):
            return pltpu.make_async_remote_copy(
                src_ref=x_refs[a] if own else rows(a, *block), dst_ref=rows(a, *block),
                send_sem=send_sems.at[k, a], recv_sem=recv_sems.at[k, a], device_id=to, device_id_type=pl.DeviceIdType.MESH)

        mine = [pltpu.make_async_copy(x_refs[a], rows(a, *me), local_sems.at[a]) for a in range(n)]
        for cp in mine:
            cp.start()
        first = []
        for a in range(n):
            first.append(copy(a, 0, me, sibling, own=True))
            first += [copy(a, 1 + j, me, (*chip, c_), own=True) for j, chip in enumerate(chips)]
        for cp in first:
            cp.start()
        passed = []
        for j, chip in enumerate(chips):
            for a in range(n):
                copy(a, 1 + j, (*chip, c_), me).wait_recv()
                cp = copy(a, 4 + j, (*chip, c_), sibling)
                cp.start()
                passed.append(cp)
        for a in range(n):
            copy(a, 0, sibling, me).wait_recv()
            for j, chip in enumerate(chips):
                copy(a, 4 + j, (*chip, 1 - c_), me).wait_recv()
        for cp in first + passed:
            cp.wait_send()
        for cp in mine:
            cp.wait()

    any_ = pl.BlockSpec(memory_space=pl.ANY)
    return pl.pallas_call(
        body, name=name, out_shape=[SDS((NDEV,) + x.shape, x.dtype) for x in xs],
        in_specs=[any_] * n, out_specs=[any_] * n,
        scratch_shapes=[pltpu.SemaphoreType.DMA((7, n)), pltpu.SemaphoreType.DMA((7, n)), pltpu.SemaphoreType.DMA((n,))],
    )(*xs)


GDN_COLS = ((0, GDN_CONV + GDN_V_W, 0), (GDN_CONV + GDN_V_W, GDN_CONV + GDN_V_W + 16, GDN_CONV + GDN_V_W),
            (GDN_CONV + GDN_V_W + 16, GDN_IN, GDN_CONV + GDN_V_W + 128))
FOX_COLS = ((0, FOX_IN, 0),)


def _col_pieces(d, per, cols):
    lo, hi = per * d, per * (d + 1)
    out = []
    for a, b, dst in cols:
        s, e = max(lo, a), min(hi, b)
        if s < e:
            out.append((s - lo, e - s, dst + s - a))
    return out


def cols_from_blocks(g, cols, n_out, name):
    _, L, R, C = g.shape
    tr = min(256, R)

    def body(g_ref, o_ref):
        o_ref[...] = jnp.zeros_like(o_ref)
        for d in range(NDEV):
            for off, ln, dst in _col_pieces(d, C, cols):
                o_ref[0, :, dst:dst + ln] = g_ref[d, 0, :, off:off + ln]

    return pl.pallas_call(
        body, name=name, grid=(L, R // tr),
        in_specs=[pl.BlockSpec((NDEV, 1, tr, C), lambda l, i: (0, l, i, 0))],
        out_specs=pl.BlockSpec((1, tr, n_out), lambda l, i: (l, i, 0)),
        out_shape=SDS((L, R, n_out), g.dtype),
        compiler_params=_cp(("arbitrary", "arbitrary")),
    )(g)


def blocks_from_cols(dw, C, cols, name):
    R, n_in = dw.shape
    tr = min(256, R)

    def body(x_ref, o_ref):
        for d in range(NDEV):
            for off, ln, src in _col_pieces(d, C, cols):
                o_ref[d, :, off:off + ln] = x_ref[:, src:src + ln].astype(bf16)

    return pl.pallas_call(
        body, name=name, grid=(R // tr,),
        in_specs=[pl.BlockSpec((tr, n_in), lambda i: (i, 0))],
        out_specs=pl.BlockSpec((NDEV, tr, C), lambda i: (0, i, 0)),
        out_shape=SDS((NDEV, R, C), bf16),
        compiler_params=_cp(("arbitrary",)),
    )(dw)


BIG = ("a_w_in", "a_conv_w", "a_w_out", "b_w_in", "b_w_out")
SMALL = ("norm_w", "ada_b", "a_A_log", "a_dt_bias", "a_norm_w", "b_f_bias", "b_qn_w", "b_kn_w", "final_norm_w")


def _pack_small(arrs):
    rows = []
    for a in arrs:
        fl = a.reshape(-1)
        pad = (-fl.shape[0]) % 128
        if pad:
            fl = jnp.concatenate([fl, jnp.zeros((pad,), fl.dtype)])
        rows.append(fl)
    flat = jnp.concatenate(rows)
    pad = (-flat.shape[0]) % (8 * 128)
    if pad:
        flat = jnp.concatenate([flat, jnp.zeros((pad,), flat.dtype)])
    return flat.reshape(-1, 128)


def _unpack(packed, shapes, align):
    flat = packed.reshape(-1)
    out, off = [], 0
    for shp in shapes:
        n = 1
        for d in shp:
            n *= d
        out.append(flat[off:off + n].reshape(shp))
        off += n + ((-n) % align)
    return out


def _full_from_gathered(g, shard_shape, axis):
    g = jnp.moveaxis(g, 0, axis)
    shp = list(shard_shape)
    shp[axis] *= NDEV
    return g.reshape(shp)


def _pad_lanes(v, n=128):
    v = v.reshape(1, -1)
    return jnp.concatenate([v, jnp.zeros((1, n - v.shape[1]), v.dtype)], axis=1)


def _carried(fn, *args, side=None, **grads):
    if callable(side):
        side = side(**grads)
    res = fn(*args, side)
    return res if side is not None else (res, None)


def gdn_layer_fwd(x, mod, nw, weights, tag, sides):
    W_in, conv_w, A_log, dt_bias, a_nw, W_out = weights
    shift, scale, gate = mod
    got = {}
    (proj, h), got["inproj"] = _carried(inproj_fwd, x, nw, scale, shift, W_in, GDN_TN, f"{tag}_inproj", side=sides.get("inproj"))
    qkvc = gdn_prep_fwd(proj, conv_w, f"{tag}_prep")
    gc, beta = gdn_gates_fwd(proj, A_log, dt_bias, f"{tag}_gates")
    (u, w, attn, Ts), got["intra"] = _carried(gdn_intra_fwd, qkvc, gc, beta, f"{tag}_intra", side=sides.get("intra"))
    o, states = gdn_scan_fwd(qkvc, gc, u, w, attn, f"{tag}_scan")
    x_new, y, og = gdn_out_fwd(o, proj, a_nw, W_out, x, gate, f"{tag}_out")
    return x_new, (x, proj, h, qkvc, gc, beta, o, states, Ts, y, og, u, w, attn), got


def gdn_layer_bwd(dxn, saved, mod, nw, weights, tag, sides):
    W_in, conv_w, A_log, dt_bias, a_nw, W_out = weights
    shift, scale, gate = mod
    x, proj, h, qkvc, gc, beta, o, states, Ts, y, og, u, w, attn = saved
    got = {}
    dy, dgate, do, dproj, da_nw = gdn_out_bwd(dxn, y, gate, o, proj, a_nw, W_out, f"{tag}_out_bwd")
    dW_out, = matmul_tn(og, dy, 512, f"{tag}_dwout")
    (dq_s, dk_s, dgc_s, du, dw, dattn), got["sbwd"] = _carried(
        gdn_scan_bwd, qkvc, gc, u, w, attn, states, do, f"{tag}_scan_bwd", side=sides.get("sbwd"), dW_out=dW_out)
    (dqkvc, dgc, dbeta), got["intrab"] = _carried(
        gdn_intra_bwd, qkvc, gc, beta, Ts, du, dw, dattn, dq_s, dk_s, dgc_s, f"{tag}_intra_bwd", side=sides.get("intrab"), dW_out=dW_out)
    dproj, dA_log, ddt = gdn_gates_bwd(proj, A_log, dt_bias, dgc, dbeta, dproj, f"{tag}_gates_bwd")
    dproj, dconv_w = gdn_prep_bwd(proj, conv_w, dqkvc, dproj, f"{tag}_prep_bwd")
    (dW_in,), got["dwin"] = _carried(matmul_tn, h, dproj, GDN_TN, f"{tag}_dwin", side=sides.get("dwin"), dW_out=dW_out)
    (dx, dnw, dscale, dshift), got["ibwd"] = _carried(
        inproj_bwd_x, x, nw, scale, shift, W_in, dproj, dxn, GDN_TN, f"{tag}_inproj_bwd", side=sides.get("ibwd"),
        dW_out=dW_out, dW_in=dW_in, dconv_w=dconv_w)
    grads = dict(norm_w=dnw, W_in=dW_in, conv_w=dconv_w, A_log=dA_log[:, :16], dt_bias=ddt[:, :16], a_nw=da_nw, W_out=dW_out,
                 dmod=jnp.concatenate([dshift, dscale, dgate], axis=1))
    return dx, grads, got


def fox_layer_fwd(x, mod, nw, weights, tag, sides):
    W_in, f_bias, qn_w, kn_w, W_out = weights
    shift, scale, gate = mod
    got = {}
    (proj, h), got["inproj"] = _carried(inproj_fwd, x, nw, scale, shift, W_in, FOX_TN, f"{tag}_inproj", side=sides.get("inproj"))
    Q, K, V = fox_prep_fwd(proj, f_bias, qn_w, kn_w, f"{tag}_prep")
    (O,), got["attn"] = _carried(fox_attn_fwd, Q, K, V, f"{tag}_attn", side=sides.get("attn"))
    x_new, y, og = fox_out_fwd(O, proj, W_out, x, gate, f"{tag}_out")
    return x_new, (x, proj, h, Q, K, V, O, y, og), got


def fox_layer_bwd(dxn, saved, mod, nw, weights, tag, sides):
    W_in, f_bias, qn_w, kn_w, W_out = weights
    shift, scale, gate = mod
    x, proj, h, Q, K, V, O, y, og = saved
    got = {}
    dy, dgate, dO, dz = fox_out_bwd(dxn, y, gate, O, proj, W_out, f"{tag}_out_bwd")
    dW_out, = matmul_tn(og, dy, 512, f"{tag}_dwout")
    (dQ, dK, dV), got["abwd"] = _carried(fox_attn_bwd, Q, K, V, dO, O, f"{tag}_attn_bwd", side=sides.get("abwd"))
    dproj, dfb, dqw, dkw = fox_prep_bwd(proj, f_bias, qn_w, kn_w, dQ, dK, dV, dz, f"{tag}_prep_bwd")
    (dW_in,), got["dwin"] = _carried(matmul_tn, h, dproj, FOX_TN, f"{tag}_dwin", side=sides.get("dwin"))
    dx, dnw, dscale, dshift = inproj_bwd_x(x, nw, scale, shift, W_in, dproj, dxn, FOX_TN, f"{tag}_inproj_bwd")
    grads = dict(norm_w=dnw, W_in=dW_in, f_bias=dfb[:, :16], qn_w=dqw, kn_w=dkw, W_out=dW_out,
                 dmod=jnp.concatenate([dshift, dscale, dgate], axis=1))
    return dx, grads, got


class LocalPlan:
    def __init__(self, full):
        self.full = full

    def layer_weights(self, i):
        j, f = i // 2, self.full
        return (f["a_w_in"][j], f["a_w_out"][j], f["a_conv_w"][j]) if i % 2 == 0 else (f["b_w_in"][j], f["b_w_out"][j])

    def fwd_sides(self, i):
        return {}

    def fwd_got(self, i, got):
        pass

    def bwd_sides(self, i):
        return {}

    def bwd_got(self, i, grads, got):
        pass


def device_step(x, mod_all, norm_w, small, final_norm_w, target, plan):
    D = x.shape[1]
    mods = [(mod_all[i:i + 1, 0:D], mod_all[i:i + 1, D:2 * D], mod_all[i:i + 1, 2 * D:3 * D]) for i in range(4)]

    def weights(i):
        j = i // 2
        if i % 2 == 0:
            W_in, W_out, conv_w = plan.layer_weights(i)
            return (W_in, conv_w, _pad_lanes(small["a_A_log"][j]), _pad_lanes(small["a_dt_bias"][j]), small["a_norm_w"][j:j + 1], W_out)
        W_in, W_out = plan.layer_weights(i)
        return (W_in, _pad_lanes(small["b_f_bias"][j]), small["b_qn_w"][j:j + 1], small["b_kn_w"][j:j + 1], W_out)

    saved, wts = [], []
    for i in range(4):
        wts.append(weights(i))
        fwd = gdn_layer_fwd if i % 2 == 0 else fox_layer_fwd
        x, sv, got = fwd(x, mods[i], norm_w[i:i + 1], wts[i], f"L{i}", plan.fwd_sides(i))
        plan.fwd_got(i, got)
        saved.append(sv)
    loss, dx, dfw = final_loss(x, final_norm_w.reshape(1, D), target, "final_loss")
    lg = [None] * 4
    for i in reversed(range(4)):
        bwd = gdn_layer_bwd if i % 2 == 0 else fox_layer_bwd
        dx, lg[i], got = bwd(dx, saved[i], mods[i], norm_w[i:i + 1], wts[i], f"L{i}", plan.bwd_sides(i))
        plan.bwd_got(i, lg[i], got)
    g = dict(
        norm_w=jnp.concatenate([lg[i]["norm_w"] for i in range(4)], axis=0),
        dmod=jnp.concatenate([lg[i]["dmod"] for i in range(4)], axis=0),
        a_w_in=[lg[i]["W_in"] for i in (0, 2)],
        a_conv_w=jnp.stack([lg[i]["conv_w"] for i in (0, 2)]),
        a_A_log=jnp.concatenate([lg[i]["A_log"] for i in (0, 2)], axis=0),
        a_dt_bias=jnp.concatenate([lg[i]["dt_bias"] for i in (0, 2)], axis=0),
        a_norm_w=jnp.concatenate([lg[i]["a_nw"] for i in (0, 2)], axis=0),
        a_w_out=[lg[i]["W_out"] for i in (0, 2)],
        b_w_in=[lg[i]["W_in"] for i in (1, 3)],
        b_f_bias=jnp.concatenate([lg[i]["f_bias"] for i in (1, 3)], axis=0),
        b_qn_w=jnp.concatenate([lg[i]["qn_w"] for i in (1, 3)], axis=0),
        b_kn_w=jnp.concatenate([lg[i]["kn_w"] for i in (1, 3)], axis=0),
        b_w_out=[lg[i]["W_out"] for i in (1, 3)],
        final_norm_w=dfw.reshape(-1),
    )
    return loss[0, 0], dx, g


class MeshPlan:
    def __init__(self, shards, w0, conv_full):
        self.shards = shards
        self.w = {0: w0}
        self.conv = conv_full
        self.recv = {}
        self.pending = {}
        self.names = {}

    def layer_weights(self, i):
        return self.w[i]

    def _gather_side(self, layer):
        names = ("a_w_in", "a_w_out") if layer % 2 == 0 else ("b_w_in", "b_w_out")
        out = []
        for n in names:
            sh = self.shards[n][layer // 2]
            out.append(sh.reshape(-1, sh.shape[-1]))
        return ("gather", out)

    def fwd_sides(self, i):
        if i == 0:
            return {"inproj": self._gather_side(1), "intra": self._gather_side(2)}
        if i == 1:
            return {"attn": self._gather_side(3)}
        return {}

    def fwd_got(self, i, got):
        for key, layer in (("inproj", 1), ("intra", 2), ("attn", 3)):
            if got.get(key) is None:
                continue
            g_in, g_out = got[key]
            D = g_out.shape[-1]
            j = layer // 2
            if layer % 2 == 1:
                W_in = cols_from_blocks(g_in[:, None], FOX_COLS, FOX_IN_PAD, f"b_w_in_cols{j}")[0]
                self.w[layer] = (W_in, g_out.reshape(-1, D))
            else:
                W_in = cols_from_blocks(g_in[:, None], GDN_COLS, GDN_IN_PAD, f"a_w_in_cols{j}")[0]
                self.w[layer] = (W_in, g_out.reshape(-1, D), self.conv[j])

    @staticmethod
    def _out_blocks(dW_out):
        return dW_out.astype(bf16).reshape(NDEV, -1, dW_out.shape[-1])

    def _in_blocks(self, name, j, dW_in):
        cols = GDN_COLS if name == "a_w_in" else FOX_COLS
        return blocks_from_cols(dW_in, self.shards[name].shape[-1], cols, f"{name}_blocks{j}")

    def bwd_sides(self, i):
        self.names = {}
        sides = {}
        for (layer, key) in [k for k in self.pending if k[0] == i]:
            self.names[key], arrs = self.pending.pop((layer, key))
            sides[key] = ("scatter", arrs)
        if i == 0:
            def sbwd(dW_out):
                self.names["sbwd"] = [("a_w_out", 0)]
                return ("scatter", [self._out_blocks(dW_out)])

            def ibwd(dW_out, dW_in, dconv_w):
                conv = jnp.stack([dconv_w, self._dconv1])
                n = conv.shape[-1] // NDEV
                self.names["ibwd"] = [("a_w_in", 0), ("a_conv_w", None)]
                return ("scatter", [self._in_blocks("a_w_in", 0, dW_in),
                                    jnp.moveaxis(conv.reshape(2, 4, NDEV, n), 2, 0).reshape(NDEV, 8, n)])

            sides["sbwd"], sides["ibwd"] = sbwd, ibwd
        return sides

    def bwd_got(self, i, grads, got):
        for key, arrs in got.items():
            if arrs is not None:
                self.recv.update(zip(self.names[key], arrs))
        j = i // 2
        if i % 2 == 1:
            self.pending[(i - 1, "sbwd" if i == 3 else "intrab")] = (
                [("b_w_in", j), ("b_w_out", j)], [self._in_blocks("b_w_in", j, grads["W_in"]), self._out_blocks(grads["W_out"])])
        elif i == 2:
            self.pending[(1, "abwd")] = (
                [("a_w_in", 1), ("a_w_out", 1)], [self._in_blocks("a_w_in", 1, grads["W_in"]), self._out_blocks(grads["W_out"])])
            self._dconv1 = grads["conv_w"]


def kernel(x, c, norm_w, ada_w, ada_b, a_w_in, a_conv_w, a_A_log, a_dt_bias, a_norm_w, a_w_out, b_w_in, b_f_bias, b_qn_w, b_kn_w, b_w_out, final_norm_w, loss_target, m_norm_w, m_ada_w, m_ada_b, m_a_w_in, m_a_conv_w, m_a_A_log, m_a_dt_bias, m_a_norm_w, m_a_w_out, m_b_w_in, m_b_f_bias, m_b_qn_w, m_b_kn_w, m_b_w_out, m_final_norm_w, v_norm_w, v_ada_w, v_ada_b, v_a_w_in, v_a_conv_w, v_a_A_log, v_a_dt_bias, v_a_norm_w, v_a_w_out, v_b_w_in, v_b_f_bias, v_b_qn_w, v_b_kn_w, v_b_w_out, v_final_norm_w):
    W = dict(norm_w=norm_w, ada_w=ada_w, ada_b=ada_b, a_w_in=a_w_in, a_conv_w=a_conv_w, a_A_log=a_A_log, a_dt_bias=a_dt_bias,
             a_norm_w=a_norm_w, a_w_out=a_w_out, b_w_in=b_w_in, b_f_bias=b_f_bias, b_qn_w=b_qn_w, b_kn_w=b_kn_w, b_w_out=b_w_out,
             final_norm_w=final_norm_w)
    M = dict(norm_w=m_norm_w, ada_w=m_ada_w, ada_b=m_ada_b, a_w_in=m_a_w_in, a_conv_w=m_a_conv_w, a_A_log=m_a_A_log,
             a_dt_bias=m_a_dt_bias, a_norm_w=m_a_norm_w, a_w_out=m_a_w_out, b_w_in=m_b_w_in, b_f_bias=m_b_f_bias, b_qn_w=m_b_qn_w,
             b_kn_w=m_b_kn_w, b_w_out=m_b_w_out, final_norm_w=m_final_norm_w)
    V = dict(norm_w=v_norm_w, ada_w=v_ada_w, ada_b=v_ada_b, a_w_in=v_a_w_in, a_conv_w=v_a_conv_w, a_A_log=v_a_A_log,
             a_dt_bias=v_a_dt_bias, a_norm_w=v_a_norm_w, a_w_out=v_a_w_out, b_w_in=v_b_w_in, b_f_bias=v_b_f_bias, b_qn_w=v_b_qn_w,
             b_kn_w=v_b_kn_w, b_w_out=v_b_w_out, final_norm_w=v_final_norm_w)
    S, D = x.shape[1], x.shape[2]
    me = 4 * lax.axis_index("x") + 2 * lax.axis_index("y") + lax.axis_index("c")
    small_shapes = [W[n].shape for n in SMALL]

    shards = {n: W[n].astype(bf16) for n in ("a_w_in", "a_w_out", "b_w_in", "b_w_out")}
    gath = all_gather([shards["a_w_in"][0], shards["a_w_out"][0], a_conv_w.reshape(8, -1), c.reshape(8, D // 8)], "gather_w0")
    conv_full = _full_from_gathered(gath[2].reshape((NDEV,) + a_conv_w.shape), a_conv_w.shape, 2)
    w0 = (cols_from_blocks(gath[0][:, None], GDN_COLS, GDN_IN_PAD, "a_w_in_cols0")[0], gath[1].reshape(-1, D), conv_full[0])
    plan = MeshPlan(shards, w0, conv_full)
    c_all = gath[3].reshape(NDEV, D)

    mod_part = ada_fwd(c_all, ada_w, "ada_fwd")
    n_ada = ada_w.shape[2]
    mod_g = all_gather([mod_part.reshape(4 * NDEV, n_ada)], "gather_mod")[0].reshape(NDEV, 4, NDEV, n_ada)
    mod_mine = lax.dynamic_index_in_dim(mod_g, me, axis=2, keepdims=False)
    mod_all = jnp.moveaxis(mod_mine, 0, 1).reshape(4, NDEV * n_ada) + ada_b

    loss, dx, g = device_step(x[0], mod_all, norm_w, W, final_norm_w, loss_target[0], plan)
    loss = lax.psum(loss, MESH_AXES)

    g_small = dict(g, ada_b=g["dmod"])
    sp = _pack_small([g_small[n] for n in SMALL])
    sp_all = all_gather([sp], "gather_small")[0]
    sw, sm, sv = (_pack_small([T[n] for n in SMALL]) for T in (W, M, V))
    sg, sd, snm, snv = (_unpack(t, small_shapes, 128) for t in reduce_adam(sp_all, sw, sm, sv, sp.shape[0], "adam_small"))

    off_b = 0
    for n, shp in zip(SMALL, small_shapes):
        if n == "ada_b":
            break
        cnt = 1
        for d in shp:
            cnt *= d
        off_b += cnt + ((-cnt) % 128)
    dmod_all = sp_all.reshape(NDEV, -1)[:, off_b:off_b + 4 * 3 * D].reshape(NDEV, 4, 3 * D)
    dmod_cols = lax.dynamic_slice_in_dim(dmod_all, me * n_ada, n_ada, axis=2)
    g_ada = ada_grad(c_all, jnp.moveaxis(dmod_cols, 0, 1), "ada_grad")
    r_ada = reduce_adam(g_ada.reshape(1, 4 * D, n_ada), *(T["ada_w"].reshape(4 * D, n_ada) for T in (W, M, V)), 512, "adam_ada")
    ag, ad, anm, anv = (t.reshape(ada_w.shape) for t in r_ada)

    big = {}
    for n in BIG:
        C = W[n].shape[-1]
        parts = plan.recv[(n, None)] if n == "a_conv_w" else jnp.stack([plan.recv[(n, 0)], plan.recv[(n, 1)]], axis=1).reshape(NDEV, -1, C)
        res = reduce_adam(parts, *(T[n].reshape(parts.shape[1:]) for T in (W, M, V)), min(256, parts.shape[1]), f"adam_{n}")
        big[n] = [t.reshape(W[n].shape) for t in res]

    outs = {}
    for idx, (k, sm_l, ada_t) in enumerate((("grad", sg, ag), ("delta", sd, ad), ("new_m", snm, anm), ("new_v", snv, anv))):
        d = dict(zip(SMALL, sm_l))
        d.update({n: big[n][idx] for n in BIG})
        d["ada_w"] = ada_t
        outs[k] = d
    order = ("norm_w", "ada_w", "ada_b", "a_w_in", "a_conv_w", "a_A_log", "a_dt_bias", "a_norm_w", "a_w_out", "b_w_in", "b_f_bias",
             "b_qn_w", "b_kn_w", "b_w_out", "final_norm_w")
    return (loss, dx[None], *[outs["grad"][n] for n in order], *[outs["delta"][n] for n in order],
            *[outs["new_m"][n] for n in order], *[outs["new_v"][n] for n in order])
```

```python
import functools

import jax
import jax.numpy as jnp
from jax import lax
from jax.experimental import pallas as pl
from jax.experimental.pallas import tpu as pltpu

f32 = jnp.float32
bf16 = jnp.bfloat16
SDS = jax.ShapeDtypeStruct

EPS = 1e-6
CHUNK = 64
HD = 128
GDN_QK_HEADS = 8
GDN_V_HEADS = 16
GDN_QK_W = GDN_QK_HEADS * HD
GDN_V_W = GDN_V_HEADS * HD
GDN_CONV = 2 * GDN_QK_W + GDN_V_W
GDN_IN = GDN_CONV + GDN_V_W + 2 * GDN_V_HEADS
GDN_IN_PAD = GDN_CONV + GDN_V_W + 256
GDN_TN = 640
FOX_H = 16
FOX_D = 64
FOX_W = FOX_H * FOX_D
FOX_IN = 4 * FOX_W + FOX_H
FOX_IN_PAD = 4 * FOX_W + 128
FOX_TN = 1408
FOX_PW = FOX_H * 128
NDEV = 8
MESH_AXES = ("x", "y", "c")
NEG = -1e30

ADAM_LR = 0.001
ADAM_B1 = 0.9
ADAM_B2 = 0.999
ADAM_EPS = 1e-08
ADAM_WD = 0.01
ADAM_STEP = 10

VMEM_LIMIT = 56 * 1024 * 1024


def _cp(sem=None):
    return pltpu.CompilerParams(dimension_semantics=sem, vmem_limit_bytes=VMEM_LIMIT)


def _bdot(a, b, dims):
    return lax.dot_general(a.astype(bf16), b.astype(bf16), (dims, ((), ())), preferred_element_type=f32)


def _nn(a, b):
    return _bdot(a, b, ((1,), (0,)))


def _nt(a, b):
    return _bdot(a, b, ((1,), (1,)))


def _tn(a, b):
    return _bdot(a, b, ((0,), (0,)))


def _hdot(a, b, dims=((1,), (0,))):
    return lax.dot_general(a, b, (dims, ((), ())), precision=lax.Precision.HIGHEST, preferred_element_type=f32)


def _split2(a):
    hi = a.astype(bf16)
    return hi, (a - hi.astype(f32)).astype(bf16)


def _dot3(a, b):
    (ah, al), (bh, bl) = a, b
    return (jnp.dot(ah, bh, preferred_element_type=f32) + jnp.dot(ah, bl, preferred_element_type=f32)
            + jnp.dot(al, bh, preferred_element_type=f32))


@jax.custom_vjp
def _mm(a, b):
    return _nn(a, b)


_mm.defvjp(lambda a, b: (_nn(a, b), (a, b)), lambda r, g: (_nt(g, r[1]), _tn(r[0], g)))


@jax.custom_vjp
def _mm_nt(a, b):
    return _nt(a, b)


_mm_nt.defvjp(lambda a, b: (_nt(a, b), (a, b)), lambda r, g: (_nn(g, r[1]), _tn(g, r[0])))


@jax.custom_vjp
def _mm_tn(a, b):
    return _tn(a, b)


_mm_tn.defvjp(lambda a, b: (_tn(a, b), (a, b)), lambda r, g: (_nt(r[1], g), _nn(r[0], g)))


def _silu(x):
    return x * jax.nn.sigmoid(x)


def _rms_mod(x, nw, scale, shift):
    r = lax.rsqrt(jnp.mean(x * x, axis=-1, keepdims=True) + EPS)
    return (x * r * nw) * (1.0 + scale) + shift


def _rows(S, want):
    return min(want, S)


def _my_pos():
    return lax.axis_index("x"), lax.axis_index("y"), lax.axis_index("c")


def _exchange_copies(kind, x_refs, out_refs, send_sems, recv_sems, local_sems):
    x_, y_, c_ = _my_pos()
    me = 4 * x_ + 2 * y_ + c_
    own = kind == "gather"
    cps = [pltpu.make_async_copy(x_refs[a] if own else x_refs[a].at[me], out_refs[a].at[me], local_sems.at[a])
           for a in range(len(x_refs))]
    for rel in range(1, NDEV):
        px = (x_ + ((rel >> 2) & 1)) % 2
        py = (y_ + ((rel >> 1) & 1)) % 2
        pc = (c_ + (rel & 1)) % 2
        for a in range(len(x_refs)):
            cps.append(pltpu.make_async_remote_copy(
                src_ref=x_refs[a] if own else x_refs[a].at[4 * px + 2 * py + pc], dst_ref=out_refs[a].at[me],
                send_sem=send_sems.at[rel - 1, a], recv_sem=recv_sems.at[rel - 1, a],
                device_id=(px, py, pc), device_id_type=pl.DeviceIdType.MESH))
    return cps


def _exchange_scratch(n):
    return [pltpu.SemaphoreType.DMA((NDEV - 1, n)), pltpu.SemaphoreType.DMA((NDEV - 1, n)), pltpu.SemaphoreType.DMA((n,))]


def _call(body, *, name, grid, in_specs, out_specs, out_shape, args, scratch=(), side=None):
    params = _cp(("arbitrary",) * len(grid))
    if side is None:
        return pl.pallas_call(body, name=name, grid=grid, in_specs=in_specs, out_specs=out_specs, out_shape=out_shape,
                              scratch_shapes=list(scratch), compiler_params=params)(*args)
    kind, xs = side
    n_in, n_out, n_scr, ns = len(in_specs), len(out_shape), len(scratch), len(xs)
    steps = 1
    for g in grid:
        steps *= g

    def wrapped(*refs):
        o0 = n_in + ns
        s0 = o0 + n_out + ns
        step = pl.program_id(0)
        for d in range(1, len(grid)):
            step = step * grid[d] + pl.program_id(d)

        def copies():
            return _exchange_copies(kind, refs[n_in:o0], refs[o0 + n_out:s0], *refs[s0 + n_scr:])

        @pl.when(step == 0)
        def _():
            for cp in copies():
                cp.start()

        body(*refs[:n_in], *refs[o0:o0 + n_out], *refs[s0:s0 + n_scr])

        @pl.when(step == steps - 1)
        def _():
            for cp in copies():
                cp.wait()

    any_ = pl.BlockSpec(memory_space=pl.ANY)
    side_shapes = [SDS((NDEV,) + x.shape if kind == "gather" else x.shape, x.dtype) for x in xs]
    outs = pl.pallas_call(wrapped, name=name, grid=grid, in_specs=list(in_specs) + [any_] * ns,
                          out_specs=list(out_specs) + [any_] * ns, out_shape=list(out_shape) + side_shapes,
                          scratch_shapes=list(scratch) + _exchange_scratch(ns), compiler_params=params)(*args, *xs)
    return outs[:n_out], outs[n_out:]


def inproj_fwd(x, nw, scale, shift, W, tn, name, side=None):
    S, D = x.shape
    N = W.shape[1]
    tm = _rows(S, 1024)

    def body(x_ref, nw_ref, sc_ref, sh_ref, w_ref, proj_ref, h_ref):
        @pl.when(pl.program_id(1) == 0)
        def _():
            h_ref[...] = _rms_mod(x_ref[...], nw_ref[...], sc_ref[...], sh_ref[...]).astype(bf16)

        proj_ref[...] = jnp.dot(h_ref[...], w_ref[...], preferred_element_type=f32)

    vec = pl.BlockSpec((1, D), lambda i, j: (0, 0))
    return _call(
        body, name=name, grid=(S // tm, N // tn),
        in_specs=[pl.BlockSpec((tm, D), lambda i, j: (i, 0)), vec, vec, vec, pl.BlockSpec((D, tn), lambda i, j: (0, j))],
        out_specs=[pl.BlockSpec((tm, tn), lambda i, j: (i, j)), pl.BlockSpec((tm, D), lambda i, j: (i, 0))],
        out_shape=[SDS((S, N), f32), SDS((S, D), bf16)], args=(x, nw, scale, shift, W), side=side)


def inproj_bwd_x(x, nw, scale, shift, W, dproj, dx_res, tn, name, side=None):
    S, D = x.shape
    N = W.shape[1]
    tm = _rows(S, 1024)
    nj = N // tn

    def body(x_ref, nw_ref, sc_ref, sh_ref, w_ref, dp_ref, dxr_ref, dx_ref, dnw_ref, dsc_ref, dsh_ref, acc):
        i, j = pl.program_id(0), pl.program_id(1)

        @pl.when(j == 0)
        def _():
            acc[...] = jnp.zeros_like(acc)

        @pl.when((i == 0) & (j == 0))
        def _():
            dnw_ref[...] = jnp.zeros_like(dnw_ref)
            dsc_ref[...] = jnp.zeros_like(dsc_ref)
            dsh_ref[...] = jnp.zeros_like(dsh_ref)

        acc[...] += _nt(dp_ref[...], w_ref[...])

        @pl.when(j == nj - 1)
        def _():
            _, vjp = jax.vjp(_rms_mod, x_ref[...], nw_ref[...], sc_ref[...], sh_ref[...])
            dx, dnw, dsc, dsh = vjp(acc[...])
            dx_ref[...] = dxr_ref[...] + dx
            dnw_ref[...] += dnw
            dsc_ref[...] += dsc
            dsh_ref[...] += dsh

    vec = pl.BlockSpec((1, D), lambda i, j: (0, 0))
    row = pl.BlockSpec((tm, D), lambda i, j: (i, 0))
    return _call(
        body, name=name, grid=(S // tm, nj),
        in_specs=[row, vec, vec, vec, pl.BlockSpec((D, tn), lambda i, j: (0, j)), pl.BlockSpec((tm, tn), lambda i, j: (i, j)), row],
        out_specs=[row, vec, vec, vec],
        out_shape=[SDS((S, D), f32), SDS((1, D), f32), SDS((1, D), f32), SDS((1, D), f32)],
        scratch=[pltpu.VMEM((tm, D), f32)], args=(x, nw, scale, shift, W, dproj, dx_res), side=side)


def matmul_tn(a, b, tn, name, side=None):
    S, K = a.shape
    N = b.shape[1]
    tm = _rows(S, 1024)
    ni = S // tm

    def body(a_ref, b_ref, o_ref):
        @pl.when(pl.program_id(1) == 0)
        def _():
            o_ref[...] = jnp.zeros_like(o_ref)

        o_ref[...] += _tn(a_ref[...], b_ref[...])

    return _call(
        body, name=name, grid=(N // tn, ni),
        in_specs=[pl.BlockSpec((tm, K), lambda j, i: (i, 0)), pl.BlockSpec((tm, tn), lambda j, i: (i, j))],
        out_specs=[pl.BlockSpec((K, tn), lambda j, i: (0, j))],
        out_shape=[SDS((K, N), f32)], args=(a, b), side=side)


def _conv_taps(xs, w, n_out):
    taps = []
    for j in range(4):
        s = 3 - j
        sh = xs if s == 0 else pltpu.roll(xs, s, axis=0)
        taps.append(sh[8:8 + n_out])
    conv = taps[0] * w[0] + taps[1] * w[1] + taps[2] * w[2] + taps[3] * w[3]
    return taps, conv


def _act_norm(conv, mul):
    s = _silu(conv)
    return s * (mul * lax.rsqrt(jnp.sum(s * s, axis=-1, keepdims=True) + EPS))


def gdn_prep_fwd(proj, conv_w, name):
    S = proj.shape[0]
    R = _rows(S, 512)

    def body(x_ref, w_ref, o_ref):
        j = pl.program_id(0)
        w = [w_ref[t:t + 1, :] for t in range(4)]

        def sweep(act):
            def piece(r, c):
                t0 = pl.multiple_of(r * R, R)
                cur = x_ref[pl.ds(t0, R), :]
                prev = x_ref[pl.ds(pl.multiple_of(jnp.maximum(t0 - 8, 0), 8), 8), :]
                prev = jnp.where(r == 0, 0.0, prev)
                _, conv = _conv_taps(jnp.concatenate([prev, cur], axis=0), w, R)
                o_ref[pl.ds(t0, R), :] = act(conv)
                return c

            lax.fori_loop(0, S // R, piece, 0)

        @pl.when(j < 2 * GDN_QK_HEADS)
        def _():
            sweep(lambda c: _act_norm(c, jnp.where(j < GDN_QK_HEADS, HD ** -0.5, 1.0).astype(f32)))

        @pl.when(j >= 2 * GDN_QK_HEADS)
        def _():
            sweep(_silu)

    return pl.pallas_call(
        body, name=name, grid=(GDN_CONV // 128,),
        in_specs=[pl.BlockSpec((S, 128), lambda j: (0, j)), pl.BlockSpec((4, 128), lambda j: (0, j))],
        out_specs=pl.BlockSpec((S, 128), lambda j: (0, j)),
        out_shape=SDS((S, GDN_CONV), f32),
        compiler_params=_cp(("arbitrary",)),
    )(proj, conv_w)


def gdn_prep_bwd(proj, conv_w, dqkvc, dproj, name):
    S = proj.shape[0]
    R = _rows(S, 512)
    NP = S // R

    def body(x_ref, w_ref, dn_ref, _, dx_ref, dw_ref):
        jb = pl.program_id(0)
        w = [w_ref[j:j + 1, :] for j in range(4)]

        def piece(act, r, dw):
            t0 = pl.multiple_of(r * R, R)
            cur = x_ref[pl.ds(t0, R), :]
            prev = x_ref[pl.ds(pl.multiple_of(jnp.maximum(t0 - 8, 0), 8), 8), :]
            prev = jnp.where(r == 0, 0.0, prev)
            nxt0 = pl.multiple_of(jnp.minimum(t0 + R, S - 8), 8)
            nxt = x_ref[pl.ds(nxt0, 8), :]
            dn_cur = dn_ref[pl.ds(t0, R), :]
            dn_nxt = jnp.where(r == NP - 1, 0.0, dn_ref[pl.ds(nxt0, 8), :])
            xs = jnp.concatenate([prev, cur, nxt], axis=0)
            taps, conv = _conv_taps(xs, w, R + 8)
            dn = jnp.concatenate([dn_cur, dn_nxt], axis=0)
            _, vjp = jax.vjp(act, conv)
            dxc = vjp(dn)[0]
            n = R + 8
            dx = dxc[0:R] * w[3]
            for j in range(3):
                s = 3 - j
                dx = dx + pltpu.roll(dxc, n - s, axis=0)[0:R] * w[j]
            dx_ref[pl.ds(t0, R), :] = dx
            return tuple(dw[j] + jnp.sum(dxc[0:R] * taps[j][0:R], axis=0, keepdims=True) for j in range(4))

        def sweep(act):
            dw = lax.fori_loop(0, NP, functools.partial(piece, act), tuple(jnp.zeros((1, 128), f32) for _ in range(4)))
            for j in range(4):
                dw_ref[j:j + 1, :] = dw[j]

        @pl.when(jb < 2 * GDN_QK_HEADS)
        def _():
            sweep(lambda c: _act_norm(c, jnp.where(jb < GDN_QK_HEADS, HD ** -0.5, 1.0).astype(f32)))

        @pl.when(jb >= 2 * GDN_QK_HEADS)
        def _():
            sweep(_silu)

    col = pl.BlockSpec((S, 128), lambda j: (0, j))
    wsp = pl.BlockSpec((4, 128), lambda j: (0, j))
    return pl.pallas_call(
        body, name=name, grid=(GDN_CONV // 128,),
        in_specs=[col, wsp, col, pl.BlockSpec(memory_space=pl.ANY)], out_specs=[col, wsp],
        out_shape=[SDS(dproj.shape, f32), SDS((4, GDN_CONV), f32)],
        input_output_aliases={3: 0},
        compiler_params=_cp(("arbitrary",)),
    )(proj, conv_w, dqkvc, dproj)


def _chunk_tril(R):
    ii = lax.broadcasted_iota(jnp.int32, (R, R), 0)
    jj = lax.broadcasted_iota(jnp.int32, (R, R), 1)
    return ((ii // CHUNK == jj // CHUNK) & (ii >= jj)).astype(f32)


def _gdn_gates(b, a, A_log, dt_bias, tril):
    beta = jax.nn.sigmoid(b)
    g = -jnp.exp(A_log) * jax.nn.softplus(a + dt_bias)
    return _hdot(tril, g), beta


_GDN_B_BLK = (GDN_CONV + GDN_V_W) // 128
_GDN_A_BLK = _GDN_B_BLK + 1


def gdn_gates_fwd(proj, A_log, dt_bias, name):
    S = proj.shape[0]
    R = _rows(S, 512)

    def body(b_ref, a_ref, al_ref, dt_ref, gc_ref, be_ref):
        gc, be = _gdn_gates(b_ref[...], a_ref[...], al_ref[...], dt_ref[...], _chunk_tril(R))
        gc_ref[...] = gc
        be_ref[...] = be

    vec = pl.BlockSpec((1, 128), lambda i: (0, 0))
    blk = pl.BlockSpec((R, 128), lambda i: (i, 0))
    return pl.pallas_call(
        body, name=name, grid=(S // R,),
        in_specs=[pl.BlockSpec((R, 128), lambda i: (i, _GDN_B_BLK)), pl.BlockSpec((R, 128), lambda i: (i, _GDN_A_BLK)), vec, vec],
        out_specs=[blk, blk], out_shape=[SDS((S, 128), f32), SDS((S, 128), f32)],
        compiler_params=_cp(("arbitrary",)),
    )(proj, proj, A_log, dt_bias)


def gdn_gates_bwd(proj, A_log, dt_bias, dgc, dbeta, dproj, name):
    S = proj.shape[0]
    R = _rows(S, 512)

    def body(b_ref, a_ref, al_ref, dt_ref, dgc_ref, dbe_ref, _, dp_ref, dal_ref, ddt_ref):
        @pl.when(pl.program_id(0) == 0)
        def _():
            dal_ref[...] = jnp.zeros_like(dal_ref)
            ddt_ref[...] = jnp.zeros_like(ddt_ref)

        tril = _chunk_tril(R)
        _, vjp = jax.vjp(lambda b, a, al, dt: _gdn_gates(b, a, al, dt, tril), b_ref[...], a_ref[...], al_ref[...], dt_ref[...])
        db, da, dal, ddt = vjp((dgc_ref[...], dbe_ref[...]))
        dp_ref[:, 0:128] = db
        dp_ref[:, 128:256] = da
        dal_ref[...] += dal
        ddt_ref[...] += ddt

    vec = pl.BlockSpec((1, 128), lambda i: (0, 0))
    blk = pl.BlockSpec((R, 128), lambda i: (i, 0))
    return pl.pallas_call(
        body, name=name, grid=(S // R,),
        in_specs=[pl.BlockSpec((R, 128), lambda i: (i, _GDN_B_BLK)), pl.BlockSpec((R, 128), lambda i: (i, _GDN_A_BLK)), vec, vec, blk, blk,
                  pl.BlockSpec(memory_space=pl.ANY)],
        out_specs=[pl.BlockSpec((R, 256), lambda i: (i, _GDN_B_BLK // 2)), vec, vec],
        out_shape=[SDS(dproj.shape, f32), SDS((1, 128), f32), SDS((1, 128), f32)],
        input_output_aliases={6: 0},
        compiler_params=_cp(("arbitrary",)),
    )(proj, proj, A_log, dt_bias, dgc, dbeta, dproj)


@jax.custom_vjp
def _inv_given(L, T):
    return T


def _inv_given_bwd(T, ct):
    dL = -_nt(_tn(T, ct), T)
    return dL, jnp.zeros_like(T)


_inv_given.defvjp(lambda L, T: (T, T), _inv_given_bwd)


REP = GDN_V_HEADS // GDN_QK_HEADS


def _gdn_intra_all(qs, ks, vs, gcols, bcols, Ts=None):
    H = len(vs)
    C = vs[0].shape[0]
    ii = lax.broadcasted_iota(jnp.int32, (C, C), 0)
    jj = lax.broadcasted_iota(jnp.int32, (C, C), 1)
    grows = [jnp.sum(jnp.where(ii == jj, g, 0.0), axis=0, keepdims=True) for g in gcols]
    decs = [jnp.exp(jnp.where(ii >= jj, gcols[h] - grows[h], NEG)) for h in range(H)]
    kbs = [ks[h // REP] * bcols[h] for h in range(H)]
    As = [_mm_nt(kbs[h], ks[h // REP]) for h in range(H)]
    Ls = [jnp.where(ii > jj, As[h] * decs[h], 0.0) for h in range(H)]
    if Ts is None:
        T = _neumann_inv_batched(Ls)
    else:
        T = [_inv_given(Ls[h], Ts[h]) for h in range(H)]
    us = [_mm(T[h], vs[h] * bcols[h]) for h in range(H)]
    ws = [_mm(T[h], kbs[h] * jnp.exp(gcols[h])) for h in range(H)]
    qk = [_mm_nt(qs[p], ks[p]) for p in range(H // REP)]
    return us, ws, [qk[h // REP] * decs[h] for h in range(H)], T


def _neumann_inv_batched(Ls):
    n, C = 4, Ls[0].shape[0]
    r0 = lax.broadcasted_iota(jnp.int32, (n * C, n * C), 0)
    c0 = lax.broadcasted_iota(jnp.int32, (n * C, n * C), 1)
    same = (r0 // C) == (c0 // C)

    def blockdiag(split):
        return tuple(jnp.where(same, jnp.concatenate([x] * n, axis=0), jnp.zeros((), bf16)) for x in split)

    Ms = [jnp.concatenate(Ls[b:b + n], axis=1) for b in range(0, len(Ls), n)]
    eye = (lax.broadcasted_iota(jnp.int32, (C, n * C), 0) == (lax.broadcasted_iota(jnp.int32, (C, n * C), 1) & (C - 1))).astype(f32)
    Ps = [eye - M for M in Ms]
    Ss = [_split2(M) for M in Ms]
    Bs = [blockdiag(S) for S in Ss]
    k = 1
    while 2 * k < C:
        Ss = [_split2(_dot3(S, B)) for S, B in zip(Ss, Bs)]
        Bs = [blockdiag(S) for S in Ss]
        Ps = [P + _dot3(_split2(P), B) for P, B in zip(Ps, Bs)]
        k *= 2
    return [P[:, h * C:(h + 1) * C] for P in Ps for h in range(n)]


def _gdn_scan_all(qs, ks, gcols, us, ws, attns, S0s):
    H = len(us)
    C = us[0].shape[0]
    last = lax.broadcasted_iota(jnp.int32, (C, 1), 0) == C - 1
    glast = [jnp.sum(jnp.where(last, g, 0.0), axis=0, keepdims=True) for g in gcols]
    wS = [_mm(ws[h], S0s[h]) for h in range(H)]
    qS = [_mm(qs[h // REP] * jnp.exp(gcols[h]), S0s[h]) for h in range(H)]
    vn = [us[h] - wS[h] for h in range(H)]
    av = [_mm(attns[h], vn[h]) for h in range(H)]
    kv = [_mm_tn(ks[h // REP] * jnp.exp(glast[h] - gcols[h]), vn[h]) for h in range(H)]
    return [qS[h] + av[h] for h in range(H)], [S0s[h] * jnp.exp(glast[h]) + kv[h] for h in range(H)]


def _head_cols(blk):
    lane = lax.broadcasted_iota(jnp.int32, blk.shape, 1)
    return [jnp.sum(jnp.where(lane == h, blk, 0.0), axis=1, keepdims=True) for h in range(GDN_V_HEADS)]


def _head_lanes(cols):
    lane = lax.broadcasted_iota(jnp.int32, (cols[0].shape[0], 128), 1)
    out = jnp.zeros((cols[0].shape[0], 128), f32)
    for h, c in enumerate(cols):
        out = out + jnp.where(lane == h, c, 0.0)
    return out


def _heads(ref, n):
    return [ref[:, h * HD:(h + 1) * HD].astype(f32) for h in range(n)]


def _mats(ref):
    return [ref[0, h].astype(f32) for h in range(GDN_V_HEADS)]


def _gdn_specs(NC, rv=None):
    ix = (lambda n: n) if rv is None else rv
    qs = pl.BlockSpec((CHUNK, GDN_QK_W), lambda n: (ix(n), 0))
    ks = pl.BlockSpec((CHUNK, GDN_QK_W), lambda n: (ix(n), 1))
    vs = pl.BlockSpec((CHUNK, GDN_V_W), lambda n: (ix(n), 1))
    g1 = pl.BlockSpec((CHUNK, 128), lambda n: (ix(n), 0))
    wide = pl.BlockSpec((CHUNK, GDN_V_W), lambda n: (ix(n), 0))
    sq = pl.BlockSpec((1, GDN_V_HEADS, CHUNK, CHUNK), lambda n: (ix(n), 0, 0, 0))
    st = pl.BlockSpec((1, GDN_V_HEADS, HD, HD), lambda n: (ix(n), 0, 0, 0))
    return qs, ks, vs, g1, wide, sq, st


def gdn_intra_fwd(qkvc, gc, beta, name, side=None):
    S = qkvc.shape[0]
    NC = S // CHUNK

    def body(q_ref, k_ref, v_ref, gc_ref, be_ref, u_ref, w_ref, at_ref, T_ref):
        us, ws, attns, Ts = _gdn_intra_all(_heads(q_ref, GDN_QK_HEADS), _heads(k_ref, GDN_QK_HEADS), _heads(v_ref, GDN_V_HEADS),
                                           _head_cols(gc_ref[...]), _head_cols(be_ref[...]))
        for h in range(GDN_V_HEADS):
            u_ref[:, h * HD:(h + 1) * HD] = us[h]
            w_ref[:, h * HD:(h + 1) * HD] = ws[h].astype(bf16)
            at_ref[0, h] = attns[h].astype(bf16)
            T_ref[0, h] = Ts[h].astype(bf16)

    qs, ks, vs, g1, wide, sq, _ = _gdn_specs(NC)
    return _call(
        body, name=name, grid=(NC,),
        in_specs=[qs, ks, vs, g1, g1], out_specs=[wide, wide, sq, sq],
        out_shape=[SDS((S, GDN_V_W), f32), SDS((S, GDN_V_W), bf16),
                   SDS((NC, GDN_V_HEADS, CHUNK, CHUNK), bf16), SDS((NC, GDN_V_HEADS, CHUNK, CHUNK), bf16)],
        args=(qkvc, qkvc, qkvc, gc, beta), side=side)


def gdn_scan_fwd(qkvc, gc, u, w, attn, name):
    S = qkvc.shape[0]
    NC = S // CHUNK

    def body(q_ref, k_ref, gc_ref, u_ref, w_ref, at_ref, o_ref, st_ref, state):
        @pl.when(pl.program_id(0) == 0)
        def _():
            state[...] = jnp.zeros_like(state)

        S0s = [state[h] for h in range(GDN_V_HEADS)]
        os_, S1s = _gdn_scan_all(_heads(q_ref, GDN_QK_HEADS), _heads(k_ref, GDN_QK_HEADS), _head_cols(gc_ref[...]),
                                 _heads(u_ref, GDN_V_HEADS), _heads(w_ref, GDN_V_HEADS), _mats(at_ref), S0s)
        for h in range(GDN_V_HEADS):
            o_ref[:, h * HD:(h + 1) * HD] = os_[h]
            st_ref[0, h] = S0s[h].astype(bf16)
            state[h] = S1s[h]

    qs, ks, _, g1, wide, sq, st = _gdn_specs(NC)
    return pl.pallas_call(
        body, name=name, grid=(NC,),
        in_specs=[qs, ks, g1, wide, wide, sq], out_specs=[wide, st],
        out_shape=[SDS((S, GDN_V_W), f32), SDS((NC, GDN_V_HEADS, HD, HD), bf16)],
        scratch_shapes=[pltpu.VMEM((GDN_V_HEADS, HD, HD), f32)],
        compiler_params=_cp(("arbitrary",)),
    )(qkvc, qkvc, gc, u, w, attn)


def gdn_scan_bwd(qkvc, gc, u, w, attn, states, do, name, side=None):
    S = qkvc.shape[0]
    NC = S // CHUNK

    def body(q_ref, k_ref, gc_ref, u_ref, w_ref, at_ref, st_ref, do_ref,
             dq_ref, dk_ref, dgc_ref, du_ref, dw_ref, dat_ref, dstate):
        @pl.when(pl.program_id(0) == 0)
        def _():
            dstate[...] = jnp.zeros_like(dstate)

        VH = range(GDN_V_HEADS)
        _, vjp = jax.vjp(_gdn_scan_all, _heads(q_ref, GDN_QK_HEADS), _heads(k_ref, GDN_QK_HEADS), _head_cols(gc_ref[...]),
                         _heads(u_ref, GDN_V_HEADS), _heads(w_ref, GDN_V_HEADS), _mats(at_ref), _mats(st_ref))
        dqs, dks, dgs, dus, dws, dats, dS0s = vjp((_heads(do_ref, GDN_V_HEADS), [dstate[h] for h in VH]))
        for p in range(GDN_QK_HEADS):
            dq_ref[:, p * HD:(p + 1) * HD] = dqs[p]
            dk_ref[:, p * HD:(p + 1) * HD] = dks[p]
        for h in VH:
            du_ref[:, h * HD:(h + 1) * HD] = dus[h].astype(bf16)
            dw_ref[:, h * HD:(h + 1) * HD] = dws[h].astype(bf16)
            dat_ref[0, h] = dats[h].astype(bf16)
            dstate[h] = dS0s[h]
        dgc_ref[...] = _head_lanes(dgs)

    qs, ks, _, g1, wide, sq, st = _gdn_specs(NC, lambda n: NC - 1 - n)
    dqs = pl.BlockSpec((CHUNK, GDN_QK_W), lambda n: (NC - 1 - n, 0))
    return _call(
        body, name=name, grid=(NC,),
        in_specs=[qs, ks, g1, wide, wide, sq, st, wide],
        out_specs=[dqs, dqs, g1, wide, wide, sq],
        out_shape=[SDS((S, GDN_QK_W), f32), SDS((S, GDN_QK_W), f32), SDS((S, 128), f32), SDS((S, GDN_V_W), bf16),
                   SDS((S, GDN_V_W), bf16), SDS((NC, GDN_V_HEADS, CHUNK, CHUNK), bf16)],
        scratch=[pltpu.VMEM((GDN_V_HEADS, HD, HD), f32)], args=(qkvc, qkvc, gc, u, w, attn, states, do), side=side)


def gdn_intra_bwd(qkvc, gc, beta, Ts, du, dw, dattn, dq_s, dk_s, dgc_s, name, side=None):
    S = qkvc.shape[0]
    NC = S // CHUNK

    def body(q_ref, k_ref, v_ref, gc_ref, be_ref, T_ref, du_ref, dw_ref, dat_ref, dqs_ref, dks_ref, dgs_ref,
             dqkv_ref, dgc_ref, dbe_ref):
        VH = range(GDN_V_HEADS)
        Ts = _mats(T_ref)
        _, vjp = jax.vjp(lambda q_, k_, v_, g_, b_: _gdn_intra_all(q_, k_, v_, g_, b_, Ts)[:3],
                         _heads(q_ref, GDN_QK_HEADS), _heads(k_ref, GDN_QK_HEADS), _heads(v_ref, GDN_V_HEADS),
                         _head_cols(gc_ref[...]), _head_cols(be_ref[...]))
        dqs, dks, dvs, dgs, dbs = vjp((_heads(du_ref, GDN_V_HEADS), _heads(dw_ref, GDN_V_HEADS), _mats(dat_ref)))
        for p in range(GDN_QK_HEADS):
            dqkv_ref[:, p * HD:(p + 1) * HD] = dqs[p] + dqs_ref[:, p * HD:(p + 1) * HD]
            dqkv_ref[:, GDN_QK_W + p * HD:GDN_QK_W + (p + 1) * HD] = dks[p] + dks_ref[:, p * HD:(p + 1) * HD]
        for h in VH:
            dqkv_ref[:, 2 * GDN_QK_W + h * HD:2 * GDN_QK_W + (h + 1) * HD] = dvs[h]
        dgc_ref[...] = _head_lanes(dgs) + dgs_ref[...]
        dbe_ref[...] = _head_lanes(dbs)

    qs, ks, vs, g1, wide, sq, _ = _gdn_specs(NC)
    dqs = pl.BlockSpec((CHUNK, GDN_QK_W), lambda n: (n, 0))
    return _call(
        body, name=name, grid=(NC,),
        in_specs=[qs, ks, vs, g1, g1, sq, wide, wide, sq, dqs, dqs, g1],
        out_specs=[pl.BlockSpec((CHUNK, GDN_CONV), lambda n: (n, 0)), g1, g1],
        out_shape=[SDS((S, GDN_CONV), f32), SDS((S, 128), f32), SDS((S, 128), f32)],
        args=(qkvc, qkvc, qkvc, gc, beta, Ts, du, dw, dattn, dq_s, dk_s, dgc_s), side=side)


def _gated_norm(o, z, nw):
    parts = []
    for h in range(GDN_V_HEADS):
        oh = o[:, h * HD:(h + 1) * HD]
        r = lax.rsqrt(jnp.mean(oh * oh, axis=-1, keepdims=True) + EPS)
        parts.append((oh * r * nw) * _silu(z[:, h * HD:(h + 1) * HD]))
    return jnp.concatenate(parts, axis=1)


def gdn_out_fwd(o, proj, nw, W, x, gate, name):
    S, D = x.shape
    tm = _rows(S, 256)

    def body(o_ref, z_ref, nw_ref, w_ref, x_ref, g_ref, xn_ref, y_ref, og_ref):
        og = _gated_norm(o_ref[...], z_ref[...], nw_ref[...]).astype(bf16)
        y = jnp.dot(og, w_ref[...], preferred_element_type=f32)
        og_ref[...] = og
        y_ref[...] = y
        xn_ref[...] = x_ref[...] + g_ref[...] * y

    row = pl.BlockSpec((tm, D), lambda i: (i, 0))
    wide = pl.BlockSpec((tm, GDN_V_W), lambda i: (i, 0))
    return pl.pallas_call(
        body, name=name, grid=(S // tm,),
        in_specs=[wide, pl.BlockSpec((tm, GDN_V_W), lambda i: (i, 2)), pl.BlockSpec((1, HD), lambda i: (0, 0)),
                  pl.BlockSpec((GDN_V_W, D), lambda i: (0, 0)), row, pl.BlockSpec((1, D), lambda i: (0, 0))],
        out_specs=[row, row, wide],
        out_shape=[SDS((S, D), f32), SDS((S, D), f32), SDS((S, GDN_V_W), bf16)],
        compiler_params=_cp(("arbitrary",)),
    )(o, proj, nw, W, x, gate)


def gdn_out_bwd(dxn, y, gate, o, proj, nw, W, name):
    S, D = dxn.shape
    tm = _rows(S, 256)

    def body(dx_ref, y_ref, g_ref, o_ref, z_ref, nw_ref, w_ref, dy_ref, dg_ref, do_ref, dz_ref, dnw_ref):
        @pl.when(pl.program_id(0) == 0)
        def _():
            dg_ref[...] = jnp.zeros_like(dg_ref)
            dnw_ref[...] = jnp.zeros_like(dnw_ref)

        dx = dx_ref[...]
        dy = dx * g_ref[...]
        dy_ref[...] = dy
        dg_ref[...] += jnp.sum(dx * y_ref[...], axis=0, keepdims=True)
        dog = _nt(dy, w_ref[...])
        _, vjp = jax.vjp(_gated_norm, o_ref[...], z_ref[...], nw_ref[...])
        do, dz, dnw = vjp(dog)
        do_ref[...] = do
        dz_ref[...] = dz
        dnw_ref[...] += dnw

    row = pl.BlockSpec((tm, D), lambda i: (i, 0))
    wide = pl.BlockSpec((tm, GDN_V_W), lambda i: (i, 0))
    vecd = pl.BlockSpec((1, D), lambda i: (0, 0))
    vech = pl.BlockSpec((1, HD), lambda i: (0, 0))
    return pl.pallas_call(
        body, name=name, grid=(S // tm,),
        in_specs=[row, row, vecd, wide, pl.BlockSpec((tm, GDN_V_W), lambda i: (i, 2)), vech, pl.BlockSpec((GDN_V_W, D), lambda i: (0, 0))],
        out_specs=[row, vecd, wide, pl.BlockSpec((tm, GDN_V_W), lambda i: (i, 2)), vech],
        out_shape=[SDS((S, D), f32), SDS((1, D), f32), SDS((S, GDN_V_W), f32), SDS((S, GDN_IN_PAD), f32), SDS((1, HD), f32)],
        compiler_params=_cp(("arbitrary",)),
    )(dxn, y, gate, o, proj, nw, W)


def _rms_w(x, w):
    return (x * lax.rsqrt(jnp.mean(x * x, axis=-1, keepdims=True) + EPS)) * w


def _split3(c):
    hi = c.astype(bf16).astype(f32)
    r1 = c - hi
    mid = r1.astype(bf16).astype(f32)
    lo = (r1 - mid).astype(bf16).astype(f32)
    return hi, mid, lo


_FOX_F_BLK = 4 * FOX_W // 128


def fox_prep_fwd(proj, f_bias, qn_w, kn_w, name):
    S = proj.shape[0]
    tm = _rows(S, 256)

    def body(q_ref, k_ref, v_ref, f_ref, fb_ref, qw_ref, kw_ref, Q_ref, K_ref, V_ref, carry):
        @pl.when(pl.program_id(0) == 0)
        def _():
            carry[...] = jnp.zeros_like(carry)

        ii = lax.broadcasted_iota(jnp.int32, (tm, tm), 0)
        jj = lax.broadcasted_iota(jnp.int32, (tm, tm), 1)
        lf = jax.nn.log_sigmoid(f_ref[...] + fb_ref[...])
        cum = _hdot((ii >= jj).astype(f32), lf) + carry[...]
        carry[...] = cum[tm - 1:tm, :]
        lane = lax.broadcasted_iota(jnp.int32, (tm, 128), 1)
        lo = lane < FOX_D
        qw2 = jnp.concatenate([qw_ref[...], qw_ref[...]], axis=1) * FOX_D ** -0.5
        kw2 = jnp.concatenate([kw_ref[...], kw_ref[...]], axis=1)

        def norm_pair(x, w2):
            x2 = x * x
            s_all = jnp.sum(x2, axis=1, keepdims=True)
            s_lo = jnp.sum(jnp.where(lo, x2, 0.0), axis=1, keepdims=True)
            r = jnp.where(lo, lax.rsqrt(s_lo * (1.0 / FOX_D) + EPS), lax.rsqrt((s_all - s_lo) * (1.0 / FOX_D) + EPS))
            return x * r * w2

        for p in range(FOX_H // 2):
            ps = slice(p * 128, (p + 1) * 128)
            yq, yk, xv = norm_pair(q_ref[:, ps], qw2), norm_pair(k_ref[:, ps], kw2), v_ref[:, ps]
            for e in range(2):
                h = 2 * p + e
                hi, mid, lw = _split3(cum[:, h:h + 1])
                eq = jnp.where(lane == FOX_D, hi, jnp.where(lane == FOX_D + 1, mid, jnp.where(lane == FOX_D + 2, lw, jnp.where(lane < FOX_D + 6, 1.0, 0.0))))
                ek = jnp.where(lane < FOX_D + 3, 1.0, jnp.where(lane == FOX_D + 3, -hi, jnp.where(lane == FOX_D + 4, -mid, jnp.where(lane == FOX_D + 5, -lw, 0.0))))
                ev = jnp.where(lane == FOX_D, 1.0, 0.0)
                mv = (lambda a: a) if e == 0 else (lambda a: pltpu.roll(a, FOX_D, axis=1))
                Q_ref[:, h * 128:(h + 1) * 128] = jnp.where(lo, mv(yq), eq).astype(bf16)
                K_ref[:, h * 128:(h + 1) * 128] = jnp.where(lo, mv(yk), ek).astype(bf16)
                V_ref[:, h * 128:(h + 1) * 128] = jnp.where(lo, mv(xv), ev).astype(bf16)

    def colblk(c):
        return pl.BlockSpec((tm, FOX_W), lambda i: (i, c))

    pad = pl.BlockSpec((tm, FOX_PW), lambda i: (i, 0))
    return pl.pallas_call(
        body, name=name, grid=(S // tm,),
        in_specs=[colblk(0), colblk(1), colblk(2), pl.BlockSpec((tm, 128), lambda i: (i, _FOX_F_BLK)),
                  pl.BlockSpec((1, 128), lambda i: (0, 0)), pl.BlockSpec((1, FOX_D), lambda i: (0, 0)), pl.BlockSpec((1, FOX_D), lambda i: (0, 0))],
        out_specs=[pad, pad, pad],
        out_shape=[SDS((S, FOX_PW), bf16)] * 3,
        scratch_shapes=[pltpu.VMEM((1, 128), f32)],
        compiler_params=_cp(("arbitrary",)),
    )(proj, proj, proj, proj, f_bias, qn_w, kn_w)


def fox_prep_bwd(proj, f_bias, qn_w, kn_w, dQ, dK, dV, dz, name):
    S = proj.shape[0]
    tm = _rows(S, 256)
    NB = S // tm

    def body(q_ref, k_ref, f_ref, fb_ref, qw_ref, kw_ref, dQ_ref, dK_ref, dV_ref, dz_ref,
             dp_ref, dfb_ref, dqw_ref, dkw_ref, carry):
        @pl.when(pl.program_id(0) == 0)
        def _():
            carry[...] = jnp.zeros_like(carry)
            dfb_ref[...] = jnp.zeros_like(dfb_ref)
            dqw_ref[...] = jnp.zeros_like(dqw_ref)
            dkw_ref[...] = jnp.zeros_like(dkw_ref)

        lane = lax.broadcasted_iota(jnp.int32, (tm, 128), 1)
        lo = lane < FOX_D
        qw2 = jnp.concatenate([qw_ref[...], qw_ref[...]], axis=1) * FOX_D ** -0.5
        kw2 = jnp.concatenate([kw_ref[...], kw_ref[...]], axis=1)

        def pair(ref, p):
            return jnp.where(lo, ref[:, 2 * p * 128:(2 * p + 1) * 128], pltpu.roll(ref[:, (2 * p + 1) * 128:(2 * p + 2) * 128], FOX_D, axis=1))

        def norm_pair_bwd(x, w2, dy):
            x2 = x * x
            s_all = jnp.sum(x2, axis=1, keepdims=True)
            s_lo = jnp.sum(jnp.where(lo, x2, 0.0), axis=1, keepdims=True)
            r = jnp.where(lo, lax.rsqrt(s_lo * (1.0 / FOX_D) + EPS), lax.rsqrt((s_all - s_lo) * (1.0 / FOX_D) + EPS))
            t = dy * w2 * x
            t_all = jnp.sum(t, axis=1, keepdims=True)
            t_lo = jnp.sum(jnp.where(lo, t, 0.0), axis=1, keepdims=True)
            dx = r * (w2 * dy - x * (r * r) * (jnp.where(lo, t_lo, t_all - t_lo) * (1.0 / FOX_D)))
            return dx, jnp.sum(dy * x * r, axis=0, keepdims=True)

        dcum = jnp.zeros((tm, 128), f32)
        dqw2 = jnp.zeros((1, 128), f32)
        dkw2 = jnp.zeros((1, 128), f32)
        for p in range(FOX_H // 2):
            ps = slice(p * 128, (p + 1) * 128)
            dxq, dw1 = norm_pair_bwd(q_ref[:, ps], qw2, pair(dQ_ref, p))
            dxk, dw2 = norm_pair_bwd(k_ref[:, ps], kw2, pair(dK_ref, p))
            dp_ref[:, p * 128:(p + 1) * 128] = dxq
            dp_ref[:, FOX_W + p * 128:FOX_W + (p + 1) * 128] = dxk
            dp_ref[:, 2 * FOX_W + p * 128:2 * FOX_W + (p + 1) * 128] = pair(dV_ref, p)
            dqw2 = dqw2 + dw1
            dkw2 = dkw2 + dw2
            for e in range(2):
                h = 2 * p + e
                dcum = dcum + jnp.where(lane == h, dQ_ref[:, h * 128 + FOX_D:h * 128 + FOX_D + 1]
                                        - dK_ref[:, h * 128 + FOX_D + 3:h * 128 + FOX_D + 4], 0.0)
        dp_ref[:, 3 * FOX_W:4 * FOX_W] = dz_ref[...]
        ii = lax.broadcasted_iota(jnp.int32, (tm, tm), 0)
        jj = lax.broadcasted_iota(jnp.int32, (tm, tm), 1)
        dlf = _hdot((ii <= jj).astype(f32), dcum) + carry[...]
        carry[...] += jnp.sum(dcum, axis=0, keepdims=True)
        df = dlf * jax.nn.sigmoid(-(f_ref[...] + fb_ref[...]))
        dp_ref[:, 4 * FOX_W:FOX_IN_PAD] = df
        dfb_ref[...] += jnp.sum(df, axis=0, keepdims=True)
        dqw_ref[...] += (dqw2[:, :FOX_D] + dqw2[:, FOX_D:]) * FOX_D ** -0.5
        dkw_ref[...] += dkw2[:, :FOX_D] + dkw2[:, FOX_D:]

    rv = lambda i: NB - 1 - i

    def colblk(c):
        return pl.BlockSpec((tm, FOX_W), lambda i: (rv(i), c))

    pad = pl.BlockSpec((tm, FOX_PW), lambda i: (rv(i), 0))
    cmp_ = pl.BlockSpec((tm, FOX_W), lambda i: (rv(i), 0))
    v128 = pl.BlockSpec((1, 128), lambda i: (0, 0))
    v64 = pl.BlockSpec((1, FOX_D), lambda i: (0, 0))
    return pl.pallas_call(
        body, name=name, grid=(NB,),
        in_specs=[colblk(0), colblk(1), pl.BlockSpec((tm, 128), lambda i: (rv(i), _FOX_F_BLK)), v128, v64, v64, pad, pad, pad, cmp_],
        out_specs=[pl.BlockSpec((tm, FOX_IN_PAD), lambda i: (rv(i), 0)), v128, v64, v64],
        out_shape=[SDS((S, FOX_IN_PAD), f32), SDS((1, 128), f32), SDS((1, FOX_D), f32), SDS((1, FOX_D), f32)],
        scratch_shapes=[pltpu.VMEM((1, 128), f32)],
        compiler_params=_cp(("arbitrary",)),
    )(proj, proj, proj, f_bias, qn_w, kn_w, dQ, dK, dV, dz)


FOX_HB = 2


def _diag_mask(t):
    return lax.broadcasted_iota(jnp.int32, (t, t), 1) <= lax.broadcasted_iota(jnp.int32, (t, t), 0)


def fox_attn_fwd(Q, K, V, name, side=None):
    S = Q.shape[0]
    t = _rows(S, 512)

    HB = FOX_HB
    HS = [slice(h * 128, (h + 1) * 128) for h in range(HB)]

    def body(q_ref, k_ref, v_ref, o_ref, m_sc, acc_sc, s_sc):
        i = pl.program_id(1)
        qs = [q_ref[:, sl] for sl in HS]
        m_sc[...] = jnp.full_like(m_sc, NEG)
        acc_sc[...] = jnp.zeros_like(acc_sc)

        def scores(j):
            j0 = pl.multiple_of(j * t, t)
            return [_nt(qs[h], k_ref[pl.ds(j0, t), HS[h]]) for h in range(HB)]

        def tile(j, diag):
            j0 = pl.multiple_of(j * t, t)
            ss = [s_sc[h] for h in range(HB)]
            if diag:
                ss = [jnp.where(_diag_mask(t), s, NEG) for s in ss]
            else:
                nxt = scores(j + 1)
            ms = [m_sc[h] for h in range(HB)]
            m_new = [jnp.maximum(ms[h], jnp.max(ss[h], axis=1, keepdims=True)) for h in range(HB)]
            ps = [jnp.exp(ss[h] - m_new[h]) for h in range(HB)]
            pv = [_nn(ps[h], v_ref[pl.ds(j0, t), HS[h]]) for h in range(HB)]
            for h in range(HB):
                acc_sc[h] = acc_sc[h] * jnp.exp(ms[h] - m_new[h]) + pv[h]
                m_sc[h] = m_new[h]
                if not diag:
                    s_sc[h] = nxt[h]

        def off_diag(j, c):
            tile(j, False)
            return c

        first = scores(0)
        for h in range(HB):
            s_sc[h] = first[h]
        lax.fori_loop(0, i, off_diag, 0)
        tile(i, True)
        lane = lax.broadcasted_iota(jnp.int32, (t, 128), 1)
        for h in range(HB):
            acc = acc_sc[h]
            l = acc[:, FOX_D:FOX_D + 1]
            o_ref[:, HS[h]] = jnp.where(lane == FOX_D, m_sc[h] + jnp.log(l), acc / l)

    blk = pl.BlockSpec((t, HB * 128), lambda h, i: (i, h))
    seq = pl.BlockSpec((S, HB * 128), lambda h, i: (0, h))
    return _call(
        body, name=name, grid=(FOX_H // HB, S // t),
        in_specs=[blk, seq, seq], out_specs=[blk], out_shape=[SDS((S, FOX_PW), f32)],
        scratch=[pltpu.VMEM((HB, t, 1), f32), pltpu.VMEM((HB, t, 128), f32), pltpu.VMEM((HB, t, t), f32)],
        args=(Q, K, V), side=side)


def fox_attn_bwd(Q, K, V, dO, O, name, side=None):
    S = Q.shape[0]
    t = _rows(S, 512)
    nq = S // t

    HB = FOX_HB
    HS = [slice(h * 128, (h + 1) * 128) for h in range(HB)]

    def body(k_ref, v_ref, q_ref, do_ref, o_ref, dq_ref, dk_ref, dv_ref):
        j = pl.program_id(1)

        @pl.when(j == 0)
        def _():
            dq_ref[...] = jnp.zeros_like(dq_ref)

        dk_ref[...] = jnp.zeros_like(dk_ref)
        dv_ref[...] = jnp.zeros_like(dv_ref)
        ks = [k_ref[:, sl] for sl in HS]
        vs = [v_ref[:, sl] for sl in HS]

        def tile(i, diag):
            i0 = pl.multiple_of(i * t, t)
            R = range(HB)
            qs = [q_ref[pl.ds(i0, t), HS[h]] for h in R]
            dos = [do_ref[pl.ds(i0, t), HS[h]] for h in R]
            ss = [_nt(qs[h], ks[h]) - o_ref[pl.ds(i0, t), h * 128 + FOX_D:h * 128 + FOX_D + 1] for h in R]
            if diag:
                ss = [jnp.where(_diag_mask(t), s, NEG) for s in ss]
            ps = [jnp.exp(s) for s in ss]
            dps = [_nt(dos[h], vs[h]) for h in R]
            dvs = [_tn(ps[h], dos[h]) for h in R]
            dss = [(ps[h] * dps[h]).astype(bf16) for h in R]
            dks = [_tn(dss[h], qs[h]) for h in R]
            dqs = [_nn(dss[h], ks[h]) for h in R]
            for h in R:
                dv_ref[:, HS[h]] += dvs[h]
                dk_ref[:, HS[h]] += dks[h]
                dq_ref[pl.ds(i0, t), HS[h]] += dqs[h]

        tile(j, True)

        def off_diag(i, c):
            tile(i, False)
            return c

        lax.fori_loop(j + 1, nq, off_diag, 0)

    blk = pl.BlockSpec((t, HB * 128), lambda h, j: (j, h))
    seq = pl.BlockSpec((S, HB * 128), lambda h, j: (0, h))
    return _call(
        body, name=name, grid=(FOX_H // HB, nq),
        in_specs=[blk, blk, seq, seq, seq], out_specs=[seq, blk, blk],
        out_shape=[SDS((S, FOX_PW), f32)] * 3, args=(K, V, Q, dO, O), side=side)


def fox_out_fwd(O, proj, W, x, gate, name):
    S, D = x.shape
    tm = _rows(S, 256)

    def body(o_ref, z_ref, w_ref, x_ref, g_ref, xn_ref, y_ref, og_ref):
        z = z_ref[...]
        og = jnp.concatenate([o_ref[:, h * 128:h * 128 + FOX_D] * _silu(z[:, h * FOX_D:(h + 1) * FOX_D]) for h in range(FOX_H)],
                             axis=1).astype(bf16)
        y = jnp.dot(og, w_ref[...], preferred_element_type=f32)
        og_ref[...] = og
        y_ref[...] = y
        xn_ref[...] = x_ref[...] + g_ref[...] * y

    row = pl.BlockSpec((tm, D), lambda i: (i, 0))
    cmp_ = pl.BlockSpec((tm, FOX_W), lambda i: (i, 0))
    return pl.pallas_call(
        body, name=name, grid=(S // tm,),
        in_specs=[pl.BlockSpec((tm, FOX_PW), lambda i: (i, 0)), pl.BlockSpec((tm, FOX_W), lambda i: (i, 3)),
                  pl.BlockSpec((FOX_W, D), lambda i: (0, 0)), row, pl.BlockSpec((1, D), lambda i: (0, 0))],
        out_specs=[row, row, cmp_],
        out_shape=[SDS((S, D), f32), SDS((S, D), f32), SDS((S, FOX_W), bf16)],
        compiler_params=_cp(("arbitrary",)),
    )(O, proj, W, x, gate)


def fox_out_bwd(dxn, y, gate, O, proj, W, name):
    S, D = dxn.shape
    tm = _rows(S, 256)

    def body(dx_ref, y_ref, g_ref, o_ref, z_ref, w_ref, dy_ref, dg_ref, dO_ref, dz_ref):
        @pl.when(pl.program_id(0) == 0)
        def _():
            dg_ref[...] = jnp.zeros_like(dg_ref)

        dx = dx_ref[...]
        dy = dx * g_ref[...]
        dy_ref[...] = dy
        dg_ref[...] += jnp.sum(dx * y_ref[...], axis=0, keepdims=True)
        dog = _nt(dy, w_ref[...])
        z = z_ref[...]
        lane = lax.broadcasted_iota(jnp.int32, (tm, FOX_D), 1)
        dzs = []
        for h in range(FOX_H):
            sl = slice(h * FOX_D, (h + 1) * FOX_D)
            zh = z[:, sl]
            sg = jax.nn.sigmoid(zh)
            oh = o_ref[:, h * 128:h * 128 + FOX_D]
            doh = dog[:, sl] * (zh * sg)
            delta = jnp.sum(doh * oh, axis=1, keepdims=True)
            dO_ref[:, h * 128:(h + 1) * 128] = jnp.concatenate([doh, jnp.where(lane == 0, -delta, 0.0)], axis=1).astype(bf16)
            dzs.append(dog[:, sl] * oh * (sg * (1.0 + zh * (1.0 - sg))))
        dz_ref[...] = jnp.concatenate(dzs, axis=1)

    row = pl.BlockSpec((tm, D), lambda i: (i, 0))
    vecd = pl.BlockSpec((1, D), lambda i: (0, 0))
    pad = pl.BlockSpec((tm, FOX_PW), lambda i: (i, 0))
    return pl.pallas_call(
        body, name=name, grid=(S // tm,),
        in_specs=[row, row, vecd, pad, pl.BlockSpec((tm, FOX_W), lambda i: (i, 3)), pl.BlockSpec((FOX_W, D), lambda i: (0, 0))],
        out_specs=[row, vecd, pad, pl.BlockSpec((tm, FOX_W), lambda i: (i, 0))],
        out_shape=[SDS((S, D), f32), SDS((1, D), f32), SDS((S, FOX_PW), bf16), SDS((S, FOX_W), f32)],
        compiler_params=_cp(("arbitrary",)),
    )(dxn, y, gate, O, proj, W)


def final_loss(x, fw, target, name):
    S, D = x.shape
    tm = _rows(S, 512)

    def body(x_ref, w_ref, t_ref, l_ref, dx_ref, dw_ref):
        @pl.when(pl.program_id(0) == 0)
        def _():
            l_ref[...] = jnp.zeros_like(l_ref)
            dw_ref[...] = jnp.zeros_like(dw_ref)

        out, vjp = jax.vjp(_rms_w, x_ref[...], w_ref[...])
        err = out - t_ref[...]
        l_ref[...] += 0.5 * jnp.sum(jnp.sum(err * err, axis=1, keepdims=True) * (1.0 / D), axis=0, keepdims=True)
        dx, dw = vjp(err * (1.0 / D))
        dx_ref[...] = dx
        dw_ref[...] += dw

    row = pl.BlockSpec((tm, D), lambda i: (i, 0))
    vec = pl.BlockSpec((1, D), lambda i: (0, 0))
    return pl.pallas_call(
        body, name=name, grid=(S // tm,),
        in_specs=[row, vec, row], out_specs=[pl.BlockSpec((1, 128), lambda i: (0, 0)), row, vec],
        out_shape=[SDS((1, 128), f32), SDS((S, D), f32), SDS((1, D), f32)],
        compiler_params=_cp(("arbitrary",)),
    )(x, fw, target)


def ada_fwd(c_all, ada_w, name):
    L, D, n = ada_w.shape

    def body(c_ref, w_ref, o_ref):
        cond = jnp.concatenate([_silu(c_ref[...]), jnp.zeros((8, D), f32)], axis=0)
        o_ref[0] = _nn(cond, w_ref[0])[0:8]

    return pl.pallas_call(
        body, name=name, grid=(L,),
        in_specs=[pl.BlockSpec((NDEV, D), lambda l: (0, 0)), pl.BlockSpec((1, D, n), lambda l: (l, 0, 0))],
        out_specs=pl.BlockSpec((1, NDEV, n), lambda l: (l, 0, 0)),
        out_shape=SDS((L, NDEV, n), f32),
        compiler_params=_cp(("arbitrary",)),
    )(c_all, ada_w)


def ada_grad(c_all, dmod, name):
    L, _, n = dmod.shape
    D = c_all.shape[1]

    def body(c_ref, d_ref, o_ref):
        cond = jnp.concatenate([_silu(c_ref[...]), jnp.zeros((8, D), f32)], axis=0)
        dm = jnp.concatenate([d_ref[0], jnp.zeros((8, n), f32)], axis=0)
        o_ref[0] = _tn(cond, dm)

    return pl.pallas_call(
        body, name=name, grid=(L,),
        in_specs=[pl.BlockSpec((NDEV, D), lambda l: (0, 0)), pl.BlockSpec((1, NDEV, n), lambda l: (l, 0, 0))],
        out_specs=pl.BlockSpec((1, D, n), lambda l: (l, 0, 0)),
        out_shape=SDS((L, D, n), f32),
        compiler_params=_cp(("arbitrary",)),
    )(c_all, dmod)


def reduce_adam(parts, w, m, v, tr, name):
    n, R, C = parts.shape
    c1 = 1.0 / (1.0 - ADAM_B1 ** ADAM_STEP)
    c2 = 1.0 / (1.0 - ADAM_B2 ** ADAM_STEP)

    def body(p_ref, w_ref, m_ref, v_ref, g_ref, d_ref, nm_ref, nv_ref):
        g = p_ref[0].astype(f32)
        for s in range(1, n):
            g = g + p_ref[s].astype(f32)
        nm = ADAM_B1 * m_ref[...] + (1.0 - ADAM_B1) * g
        nv = ADAM_B2 * v_ref[...] + (1.0 - ADAM_B2) * (g * g)
        g_ref[...] = g
        nm_ref[...] = nm
        nv_ref[...] = nv
        d_ref[...] = -ADAM_LR * ((nm * c1) / (jnp.sqrt(nv * c2) + ADAM_EPS) + ADAM_WD * w_ref[...])

    blk = pl.BlockSpec((tr, C), lambda i: (i, 0))
    return pl.pallas_call(
        body, name=name, grid=(R // tr,),
        in_specs=[pl.BlockSpec((n, tr, C), lambda i: (0, i, 0)), blk, blk, blk],
        out_specs=[blk] * 4, out_shape=[SDS((R, C), f32)] * 4,
        compiler_params=_cp(("arbitrary",)),
    )(parts, w, m, v)


def all_gather(xs, name):
    n = len(xs)

    def body(*refs):
        x_refs, out_refs = refs[:n], refs[n:2 * n]
        send_sems, recv_sems, local_sems = refs[2 * n:]
        x_, y_, c_ = _my_pos()
        me, sibling = (x_, y_, c_), (x_, y_, 1 - c_)
        chips = [(1 - x_, y_), (x_, 1 - y_), (1 - x_, 1 - y_)]

        def rows(a, px, py, pc):
            return out_refs[a].at[4 * px + 2 * py + pc]

        def copy(a, k, block, to, own=False):
            return pltpu.make_async_remote_copy(
                src_ref=x_refs[a] if own else rows(a, *block), dst_ref=rows(a, *block),
                send_sem=send_sems.at[k, a], recv_sem=recv_sems.at[k, a], device_id=to, device_id_type=pl.DeviceIdType.MESH)

        mine = [pltpu.make_async_copy(x_refs[a], rows(a, *me), local_sems.at[a]) for a in range(n)]
        for cp in mine:
            cp.start()
        first = []
        for a in range(n):
            first.append(copy(a, 0, me, sibling, own=True))
            first += [copy(a, 1 + j, me, (*chip, c_), own=True) for j, chip in enumerate(chips)]
        for cp in first:
            cp.start()
        passed = []
        for j, chip in enumerate(chips):
            for a in range(n):
                copy(a, 1 + j, (*chip, c_), me).wait_recv()
                cp = copy(a, 4 + j, (*chip, c_), sibling)
                cp.start()
                passed.append(cp)
        for a in range(n):
            copy(a, 0, sibling, me).wait_recv()
            for j, chip in enumerate(chips):
                copy(a, 4 + j, (*chip, 1 - c_), me).wait_recv()
        for cp in first + passed:
            cp.wait_send()
        for cp in mine:
            cp.wait()

    any_ = pl.BlockSpec(memory_space=pl.ANY)
    return pl.pallas_call(
        body, name=name, out_shape=[SDS((NDEV,) + x.shape, x.dtype) for x in xs],
        in_specs=[any_] * n, out_specs=[any_] * n,
        scratch_shapes=[pltpu.SemaphoreType.DMA((7, n)), pltpu.SemaphoreType.DMA((7, n)), pltpu.SemaphoreType.DMA((n,))],
    )(*xs)


GDN_COLS = ((0, GDN_CONV + GDN_V_W, 0), (GDN_CONV + GDN_V_W, GDN_CONV + GDN_V_W + 16, GDN_CONV + GDN_V_W),
            (GDN_CONV + GDN_V_W + 16, GDN_IN, GDN_CONV + GDN_V_W + 128))
FOX_COLS = ((0, FOX_IN, 0),)


def _col_pieces(d, per, cols):
    lo, hi = per * d, per * (d + 1)
    out = []
    for a, b, dst in cols:
        s, e = max(lo, a), min(hi, b)
        if s < e:
            out.append((s - lo, e - s, dst + s - a))
    return out


def cols_from_blocks(g, cols, n_out, name):
    _, L, R, C = g.shape
    tr = min(256, R)

    def body(g_ref, o_ref):
        o_ref[...] = jnp.zeros_like(o_ref)
        for d in range(NDEV):
            for off, ln, dst in _col_pieces(d, C, cols):
                o_ref[0, :, dst:dst + ln] = g_ref[d, 0, :, off:off + ln]

    return pl.pallas_call(
        body, name=name, grid=(L, R // tr),
        in_specs=[pl.BlockSpec((NDEV, 1, tr, C), lambda l, i: (0, l, i, 0))],
        out_specs=pl.BlockSpec((1, tr, n_out), lambda l, i: (l, i, 0)),
        out_shape=SDS((L, R, n_out), g.dtype),
        compiler_params=_cp(("arbitrary", "arbitrary")),
    )(g)


def blocks_from_cols(dw, C, cols, name):
    R, n_in = dw.shape
    tr = min(256, R)

    def body(x_ref, o_ref):
        for d in range(NDEV):
            for off, ln, src in _col_pieces(d, C, cols):
                o_ref[d, :, off:off + ln] = x_ref[:, src:src + ln].astype(bf16)

    return pl.pallas_call(
        body, name=name, grid=(R // tr,),
        in_specs=[pl.BlockSpec((tr, n_in), lambda i: (i, 0))],
        out_specs=pl.BlockSpec((NDEV, tr, C), lambda i: (0, i, 0)),
        out_shape=SDS((NDEV, R, C), bf16),
        compiler_params=_cp(("arbitrary",)),
    )(dw)


BIG = ("a_w_in", "a_conv_w", "a_w_out", "b_w_in", "b_w_out")
SMALL = ("norm_w", "ada_b", "a_A_log", "a_dt_bias", "a_norm_w", "b_f_bias", "b_qn_w", "b_kn_w", "final_norm_w")


def _pack_small(arrs):
    rows = []
    for a in arrs:
        fl = a.reshape(-1)
        pad = (-fl.shape[0]) % 128
        if pad:
            fl = jnp.concatenate([fl, jnp.zeros((pad,), fl.dtype)])
        rows.append(fl)
    flat = jnp.concatenate(rows)
    pad = (-flat.shape[0]) % (8 * 128)
    if pad:
        flat = jnp.concatenate([flat, jnp.zeros((pad,), flat.dtype)])
    return flat.reshape(-1, 128)


def _unpack(packed, shapes, align):
    flat = packed.reshape(-1)
    out, off = [], 0
    for shp in shapes:
        n = 1
        for d in shp:
            n *= d
        out.append(flat[off:off + n].reshape(shp))
        off += n + ((-n) % align)
    return out


def _full_from_gathered(g, shard_shape, axis):
    g = jnp.moveaxis(g, 0, axis)
    shp = list(shard_shape)
    shp[axis] *= NDEV
    return g.reshape(shp)


def _pad_lanes(v, n=128):
    v = v.reshape(1, -1)
    return jnp.concatenate([v, jnp.zeros((1, n - v.shape[1]), v.dtype)], axis=1)


def _carried(fn, *args, side=None, **grads):
    if callable(side):
        side = side(**grads)
    res = fn(*args, side)
    return res if side is not None else (res, None)


def gdn_layer_fwd(x, mod, nw, weights, tag, sides):
    W_in, conv_w, A_log, dt_bias, a_nw, W_out = weights
    shift, scale, gate = mod
    got = {}
    (proj, h), got["inproj"] = _carried(inproj_fwd, x, nw, scale, shift, W_in, GDN_TN, f"{tag}_inproj", side=sides.get("inproj"))
    qkvc = gdn_prep_fwd(proj, conv_w, f"{tag}_prep")
    gc, beta = gdn_gates_fwd(proj, A_log, dt_bias, f"{tag}_gates")
    (u, w, attn, Ts), got["intra"] = _carried(gdn_intra_fwd, qkvc, gc, beta, f"{tag}_intra", side=sides.get("intra"))
    o, states = gdn_scan_fwd(qkvc, gc, u, w, attn, f"{tag}_scan")
    x_new, y, og = gdn_out_fwd(o, proj, a_nw, W_out, x, gate, f"{tag}_out")
    return x_new, (x, proj, h, qkvc, gc, beta, o, states, Ts, y, og, u, w, attn), got


def gdn_layer_bwd(dxn, saved, mod, nw, weights, tag, sides):
    W_in, conv_w, A_log, dt_bias, a_nw, W_out = weights
    shift, scale, gate = mod
    x, proj, h, qkvc, gc, beta, o, states, Ts, y, og, u, w, attn = saved
    got = {}
    dy, dgate, do, dproj, da_nw = gdn_out_bwd(dxn, y, gate, o, proj, a_nw, W_out, f"{tag}_out_bwd")
    dW_out, = matmul_tn(og, dy, 512, f"{tag}_dwout")
    (dq_s, dk_s, dgc_s, du, dw, dattn), got["sbwd"] = _carried(
        gdn_scan_bwd, qkvc, gc, u, w, attn, states, do, f"{tag}_scan_bwd", side=sides.get("sbwd"), dW_out=dW_out)
    (dqkvc, dgc, dbeta), got["intrab"] = _carried(
        gdn_intra_bwd, qkvc, gc, beta, Ts, du, dw, dattn, dq_s, dk_s, dgc_s, f"{tag}_intra_bwd", side=sides.get("intrab"), dW_out=dW_out)
    dproj, dA_log, ddt = gdn_gates_bwd(proj, A_log, dt_bias, dgc, dbeta, dproj, f"{tag}_gates_bwd")
    dproj, dconv_w = gdn_prep_bwd(proj, conv_w, dqkvc, dproj, f"{tag}_prep_bwd")
    (dW_in,), got["dwin"] = _carried(matmul_tn, h, dproj, GDN_TN, f"{tag}_dwin", side=sides.get("dwin"), dW_out=dW_out)
    (dx, dnw, dscale, dshift), got["ibwd"] = _carried(
        inproj_bwd_x, x, nw, scale, shift, W_in, dproj, dxn, GDN_TN, f"{tag}_inproj_bwd", side=sides.get("ibwd"),
        dW_out=dW_out, dW_in=dW_in, dconv_w=dconv_w)
    grads = dict(norm_w=dnw, W_in=dW_in, conv_w=dconv_w, A_log=dA_log[:, :16], dt_bias=ddt[:, :16], a_nw=da_nw, W_out=dW_out,
                 dmod=jnp.concatenate([dshift, dscale, dgate], axis=1))
    return dx, grads, got


def fox_layer_fwd(x, mod, nw, weights, tag, sides):
    W_in, f_bias, qn_w, kn_w, W_out = weights
    shift, scale, gate = mod
    got = {}
    (proj, h), got["inproj"] = _carried(inproj_fwd, x, nw, scale, shift, W_in, FOX_TN, f"{tag}_inproj", side=sides.get("inproj"))
    Q, K, V = fox_prep_fwd(proj, f_bias, qn_w, kn_w, f"{tag}_prep")
    (O,), got["attn"] = _carried(fox_attn_fwd, Q, K, V, f"{tag}_attn", side=sides.get("attn"))
    x_new, y, og = fox_out_fwd(O, proj, W_out, x, gate, f"{tag}_out")
    return x_new, (x, proj, h, Q, K, V, O, y, og), got


def fox_layer_bwd(dxn, saved, mod, nw, weights, tag, sides):
    W_in, f_bias, qn_w, kn_w, W_out = weights
    shift, scale, gate = mod
    x, proj, h, Q, K, V, O, y, og = saved
    got = {}
    dy, dgate, dO, dz = fox_out_bwd(dxn, y, gate, O, proj, W_out, f"{tag}_out_bwd")
    dW_out, = matmul_tn(og, dy, 512, f"{tag}_dwout")
    (dQ, dK, dV), got["abwd"] = _carried(fox_attn_bwd, Q, K, V, dO, O, f"{tag}_attn_bwd", side=sides.get("abwd"))
    dproj, dfb, dqw, dkw = fox_prep_bwd(proj, f_bias, qn_w, kn_w, dQ, dK, dV, dz, f"{tag}_prep_bwd")
    (dW_in,), got["dwin"] = _carried(matmul_tn, h, dproj, FOX_TN, f"{tag}_dwin", side=sides.get("dwin"))
    dx, dnw, dscale, dshift = inproj_bwd_x(x, nw, scale, shift, W_in, dproj, dxn, FOX_TN, f"{tag}_inproj_bwd")
    grads = dict(norm_w=dnw, W_in=dW_in, f_bias=dfb[:, :16], qn_w=dqw, kn_w=dkw, W_out=dW_out,
                 dmod=jnp.concatenate([dshift, dscale, dgate], axis=1))
    return dx, grads, got


class LocalPlan:
    def __init__(self, full):
        self.full = full

    def layer_weights(self, i):
        j, f = i // 2, self.full
        return (f["a_w_in"][j], f["a_w_out"][j], f["a_conv_w"][j]) if i % 2 == 0 else (f["b_w_in"][j], f["b_w_out"][j])

    def fwd_sides(self, i):
        return {}

    def fwd_got(self, i, got):
        pass

    def bwd_sides(self, i):
        return {}

    def bwd_got(self, i, grads, got):
        pass


def device_step(x, mod_all, norm_w, small, final_norm_w, target, plan):
    D = x.shape[1]
    mods = [(mod_all[i:i + 1, 0:D], mod_all[i:i + 1, D:2 * D], mod_all[i:i + 1, 2 * D:3 * D]) for i in range(4)]

    def weights(i):
        j = i // 2
        if i % 2 == 0:
            W_in, W_out, conv_w = plan.layer_weights(i)
            return (W_in, conv_w, _pad_lanes(small["a_A_log"][j]), _pad_lanes(small["a_dt_bias"][j]), small["a_norm_w"][j:j + 1], W_out)
        W_in, W_out = plan.layer_weights(i)
        return (W_in, _pad_lanes(small["b_f_bias"][j]), small["b_qn_w"][j:j + 1], small["b_kn_w"][j:j + 1], W_out)

    saved, wts = [], []
    for i in range(4):
        wts.append(weights(i))
        fwd = gdn_layer_fwd if i % 2 == 0 else fox_layer_fwd
        x, sv, got = fwd(x, mods[i], norm_w[i:i + 1], wts[i], f"L{i}", plan.fwd_sides(i))
        plan.fwd_got(i, got)
        saved.append(sv)
    loss, dx, dfw = final_loss(x, final_norm_w.reshape(1, D), target, "final_loss")
    lg = [None] * 4
    for i in reversed(range(4)):
        bwd = gdn_layer_bwd if i % 2 == 0 else fox_layer_bwd
        dx, lg[i], got = bwd(dx, saved[i], mods[i], norm_w[i:i + 1], wts[i], f"L{i}", plan.bwd_sides(i))
        plan.bwd_got(i, lg[i], got)
    g = dict(
        norm_w=jnp.concatenate([lg[i]["norm_w"] for i in range(4)], axis=0),
        dmod=jnp.concatenate([lg[i]["dmod"] for i in range(4)], axis=0),
        a_w_in=[lg[i]["W_in"] for i in (0, 2)],
        a_conv_w=jnp.stack([lg[i]["conv_w"] for i in (0, 2)]),
        a_A_log=jnp.concatenate([lg[i]["A_log"] for i in (0, 2)], axis=0),
        a_dt_bias=jnp.concatenate([lg[i]["dt_bias"] for i in (0, 2)], axis=0),
        a_norm_w=jnp.concatenate([lg[i]["a_nw"] for i in (0, 2)], axis=0),
        a_w_out=[lg[i]["W_out"] for i in (0, 2)],
        b_w_in=[lg[i]["W_in"] for i in (1, 3)],
        b_f_bias=jnp.concatenate([lg[i]["f_bias"] for i in (1, 3)], axis=0),
        b_qn_w=jnp.concatenate([lg[i]["qn_w"] for i in (1, 3)], axis=0),
        b_kn_w=jnp.concatenate([lg[i]["kn_w"] for i in (1, 3)], axis=0),
        b_w_out=[lg[i]["W_out"] for i in (1, 3)],
        final_norm_w=dfw.reshape(-1),
    )
    return loss[0, 0], dx, g


class MeshPlan:
    def __init__(self, shards, w0, conv_full):
        self.shards = shards
        self.w = {0: w0}
        self.conv = conv_full
        self.recv = {}
        self.pending = {}
        self.names = {}

    def layer_weights(self, i):
        return self.w[i]

    def _gather_side(self, layer):
        names = ("a_w_in", "a_w_out") if layer % 2 == 0 else ("b_w_in", "b_w_out")
        out = []
        for n in names:
            sh = self.shards[n][layer // 2]
            out.append(sh.reshape(-1, sh.shape[-1]))
        return ("gather", out)

    def fwd_sides(self, i):
        if i == 0:
            return {"inproj": self._gather_side(1), "intra": self._gather_side(2)}
        if i == 1:
            return {"attn": self._gather_side(3)}
        return {}

    def fwd_got(self, i, got):
        for key, layer in (("inproj", 1), ("intra", 2), ("attn", 3)):
            if got.get(key) is None:
                continue
            g_in, g_out = got[key]
            D = g_out.shape[-1]
            j = layer // 2
            if layer % 2 == 1:
                W_in = cols_from_blocks(g_in[:, None], FOX_COLS, FOX_IN_PAD, f"b_w_in_cols{j}")[0]
                self.w[layer] = (W_in, g_out.reshape(-1, D))
            else:
                W_in = cols_from_blocks(g_in[:, None], GDN_COLS, GDN_IN_PAD, f"a_w_in_cols{j}")[0]
                self.w[layer] = (W_in, g_out.reshape(-1, D), self.conv[j])

    @staticmethod
    def _out_blocks(dW_out):
        return dW_out.astype(bf16).reshape(NDEV, -1, dW_out.shape[-1])

    def _in_blocks(self, name, j, dW_in):
        cols = GDN_COLS if name == "a_w_in" else FOX_COLS
        return blocks_from_cols(dW_in, self.shards[name].shape[-1], cols, f"{name}_blocks{j}")

    def bwd_sides(self, i):
        self.names = {}
        sides = {}
        for (layer, key) in [k for k in self.pending if k[0] == i]:
            self.names[key], arrs = self.pending.pop((layer, key))
            sides[key] = ("scatter", arrs)
        if i == 0:
            def sbwd(dW_out):
                self.names["sbwd"] = [("a_w_out", 0)]
                return ("scatter", [self._out_blocks(dW_out)])

            def ibwd(dW_out, dW_in, dconv_w):
                conv = jnp.stack([dconv_w, self._dconv1])
                n = conv.shape[-1] // NDEV
                self.names["ibwd"] = [("a_w_in", 0), ("a_conv_w", None)]
                return ("scatter", [self._in_blocks("a_w_in", 0, dW_in),
                                    jnp.moveaxis(conv.reshape(2, 4, NDEV, n), 2, 0).reshape(NDEV, 8, n)])

            sides["sbwd"], sides["ibwd"] = sbwd, ibwd
        return sides

    def bwd_got(self, i, grads, got):
        for key, arrs in got.items():
            if arrs is not None:
                self.recv.update(zip(self.names[key], arrs))
        j = i // 2
        if i % 2 == 1:
            self.pending[(i - 1, "sbwd" if i == 3 else "intrab")] = (
                [("b_w_in", j), ("b_w_out", j)], [self._in_blocks("b_w_in", j, grads["W_in"]), self._out_blocks(grads["W_out"])])
        elif i == 2:
            self.pending[(1, "abwd")] = (
                [("a_w_in", 1), ("a_w_out", 1)], [self._in_blocks("a_w_in", 1, grads["W_in"]), self._out_blocks(grads["W_out"])])
            self._dconv1 = grads["conv_w"]


def kernel(x, c, norm_w, ada_w, ada_b, a_w_in, a_conv_w, a_A_log, a_dt_bias, a_norm_w, a_w_out, b_w_in, b_f_bias, b_qn_w, b_kn_w, b_w_out, final_norm_w, loss_target, m_norm_w, m_ada_w, m_ada_b, m_a_w_in, m_a_conv_w, m_a_A_log, m_a_dt_bias, m_a_norm_w, m_a_w_out, m_b_w_in, m_b_f_bias, m_b_qn_w, m_b_kn_w, m_b_w_out, m_final_norm_w, v_norm_w, v_ada_w, v_ada_b, v_a_w_in, v_a_conv_w, v_a_A_log, v_a_dt_bias, v_a_norm_w, v_a_w_out, v_b_w_in, v_b_f_bias, v_b_qn_w, v_b_kn_w, v_b_w_out, v_final_norm_w):
    W = dict(norm_w=norm_w, ada_w=ada_w, ada_b=ada_b, a_w_in=a_w_in, a_conv_w=a_conv_w, a_A_log=a_A_log, a_dt_bias=a_dt_bias,
             a_norm_w=a_norm_w, a_w_out=a_w_out, b_w_in=b_w_in, b_f_bias=b_f_bias, b_qn_w=b_qn_w, b_kn_w=b_kn_w, b_w_out=b_w_out,
             final_norm_w=final_norm_w)
    M = dict(norm_w=m_norm_w, ada_w=m_ada_w, ada_b=m_ada_b, a_w_in=m_a_w_in, a_conv_w=m_a_conv_w, a_A_log=m_a_A_log,
             a_dt_bias=m_a_dt_bias, a_norm_w=m_a_norm_w, a_w_out=m_a_w_out, b_w_in=m_b_w_in, b_f_bias=m_b_f_bias, b_qn_w=m_b_qn_w,
             b_kn_w=m_b_kn_w, b_w_out=m_b_w_out, final_norm_w=m_final_norm_w)
    V = dict(norm_w=v_norm_w, ada_w=v_ada_w, ada_b=v_ada_b, a_w_in=v_a_w_in, a_conv_w=v_a_conv_w, a_A_log=v_a_A_log,
             a_dt_bias=v_a_dt_bias, a_norm_w=v_a_norm_w, a_w_out=v_a_w_out, b_w_in=v_b_w_in, b_f_bias=v_b_f_bias, b_qn_w=v_b_qn_w,
             b_kn_w=v_b_kn_w, b_w_out=v_b_w_out, final_norm_w=v_final_norm_w)
    S, D = x.shape[1], x.shape[2]
    me = 4 * lax.axis_index("x") + 2 * lax.axis_index("y") + lax.axis_index("c")
    small_shapes = [W[n].shape for n in SMALL]

    shards = {n: W[n].astype(bf16) for n in ("a_w_in", "a_w_out", "b_w_in", "b_w_out")}
    gath = all_gather([shards["a_w_in"][0], shards["a_w_out"][0], a_conv_w.reshape(8, -1), c.reshape(8, D // 8)], "gather_w0")
    conv_full = _full_from_gathered(gath[2].reshape((NDEV,) + a_conv_w.shape), a_conv_w.shape, 2)
    w0 = (cols_from_blocks(gath[0][:, None], GDN_COLS, GDN_IN_PAD, "a_w_in_cols0")[0], gath[1].reshape(-1, D), conv_full[0])
    plan = MeshPlan(shards, w0, conv_full)
    c_all = gath[3].reshape(NDEV, D)

    mod_part = ada_fwd(c_all, ada_w, "ada_fwd")
    n_ada = ada_w.shape[2]
    mod_g = all_gather([mod_part.reshape(4 * NDEV, n_ada)], "gather_mod")[0].reshape(NDEV, 4, NDEV, n_ada)
    mod_mine = lax.dynamic_index_in_dim(mod_g, me, axis=2, keepdims=False)
    mod_all = jnp.moveaxis(mod_mine, 0, 1).reshape(4, NDEV * n_ada) + ada_b

    loss, dx, g = device_step(x[0], mod_all, norm_w, W, final_norm_w, loss_target[0], plan)
    loss = lax.psum(loss, MESH_AXES)

    g_small = dict(g, ada_b=g["dmod"])
    sp = _pack_small([g_small[n] for n in SMALL])
    sp_all = all_gather([sp], "gather_small")[0]
    sw, sm, sv = (_pack_small([T[n] for n in SMALL]) for T in (W, M, V))
    sg, sd, snm, snv = (_unpack(t, small_shapes, 128) for t in reduce_adam(sp_all, sw, sm, sv, sp.shape[0], "adam_small"))

    off_b = 0
    for n, shp in zip(SMALL, small_shapes):
        if n == "ada_b":
            break
        cnt = 1
        for d in shp:
            cnt *= d
        off_b += cnt + ((-cnt) % 128)
    dmod_all = sp_all.reshape(NDEV, -1)[:, off_b:off_b + 4 * 3 * D].reshape(NDEV, 4, 3 * D)
    dmod_cols = lax.dynamic_slice_in_dim(dmod_all, me * n_ada, n_ada, axis=2)
    g_ada = ada_grad(c_all, jnp.moveaxis(dmod_cols, 0, 1), "ada_grad")
    r_ada = reduce_adam(g_ada.reshape(1, 4 * D, n_ada), *(T["ada_w"].reshape(4 * D, n_ada) for T in (W, M, V)), 512, "adam_ada")
    ag, ad, anm, anv = (t.reshape(ada_w.shape) for t in r_ada)

    big = {}
    for n in BIG:
        C = W[n].shape[-1]
        parts = plan.recv[(n, None)] if n == "a_conv_w" else jnp.stack([plan.recv[(n, 0)], plan.recv[(n, 1)]], axis=1).reshape(NDEV, -1, C)
        res = reduce_adam(parts, *(T[n].reshape(parts.shape[1:]) for T in (W, M, V)), min(256, parts.shape[1]), f"adam_{n}")
        big[n] = [t.reshape(W[n].shape) for t in res]

    outs = {}
    for idx, (k, sm_l, ada_t) in enumerate((("grad", sg, ag), ("delta", sd, ad), ("new_m", snm, anm), ("new_v", snv, anv))):
        d = dict(zip(SMALL, sm_l))
        d.update({n: big[n][idx] for n in BIG})
        d["ada_w"] = ada_t
        outs[k] = d
    order = ("norm_w", "ada_w", "ada_b", "a_w_in", "a_conv_w", "a_A_log", "a_dt_bias", "a_norm_w", "a_w_out", "b_w_in", "b_f_bias",
             "b_qn_w", "b_kn_w", "b_w_out", "final_norm_w")
    return (loss, dx[None], *[outs["grad"][n] for n in order], *[outs["delta"][n] for n in order],
            *[outs["new_m"][n] for n in order], *[outs["new_v"][n] for n in order])
```

```python
import functools

import jax
import jax.numpy as jnp
from jax import lax
from jax.experimental import pallas as pl
from jax.experimental.pallas import tpu as pltpu

f32 = jnp.float32
bf16 = jnp.bfloat16
SDS = jax.ShapeDtypeStruct

EPS = 1e-6
CHUNK = 64
HD = 128
GDN_QK_HEADS = 8
GDN_V_HEADS = 16
GDN_QK_W = GDN_QK_HEADS * HD
GDN_V_W = GDN_V_HEADS * HD
GDN_CONV = 2 * GDN_QK_W + GDN_V_W
GDN_IN = GDN_CONV + GDN_V_W + 2 * GDN_V_HEADS
GDN_IN_PAD = GDN_CONV + GDN_V_W + 256
GDN_TN = 640
FOX_H = 16
FOX_D = 64
FOX_W = FOX_H * FOX_D
FOX_IN = 4 * FOX_W + FOX_H
FOX_IN_PAD = 4 * FOX_W + 128
FOX_TN = 1408
FOX_PW = FOX_H * 128
NDEV = 8
MESH_AXES = ("x", "y", "c")
NEG = -1e30

ADAM_LR = 0.001
ADAM_B1 = 0.9
ADAM_B2 = 0.999
ADAM_EPS = 1e-08
ADAM_WD = 0.01
ADAM_STEP = 10

VMEM_LIMIT = 56 * 1024 * 1024


def _cp(sem=None):
    return pltpu.CompilerParams(dimension_semantics=sem, vmem_limit_bytes=VMEM_LIMIT)


def _bdot(a, b, dims):
    return lax.dot_general(a.astype(bf16), b.astype(bf16), (dims, ((), ())), preferred_element_type=f32)


def _nn(a, b):
    return _bdot(a, b, ((1,), (0,)))


def _nt(a, b):
    return _bdot(a, b, ((1,), (1,)))


def _tn(a, b):
    return _bdot(a, b, ((0,), (0,)))


def _hdot(a, b, dims=((1,), (0,))):
    return lax.dot_general(a, b, (dims, ((), ())), precision=lax.Precision.HIGHEST, preferred_element_type=f32)


def _split2(a):
    hi = a.astype(bf16)
    return hi, (a - hi.astype(f32)).astype(bf16)


def _dot3(a, b):
    (ah, al), (bh, bl) = a, b
    n = ah.shape[0]
    both = jnp.dot(jnp.concatenate([ah, al], axis=0), bh, preferred_element_type=f32)
    return both[:n] + both[n:] + jnp.dot(ah, bl, preferred_element_type=f32)


@jax.custom_vjp
def _mm(a, b):
    return _nn(a, b)


_mm.defvjp(lambda a, b: (_nn(a, b), (a, b)), lambda r, g: (_nt(g, r[1]), _tn(r[0], g)))


@jax.custom_vjp
def _mm_nt(a, b):
    return _nt(a, b)


_mm_nt.defvjp(lambda a, b: (_nt(a, b), (a, b)), lambda r, g: (_nn(g, r[1]), _tn(g, r[0])))


@jax.custom_vjp
def _mm_tn(a, b):
    return _tn(a, b)


_mm_tn.defvjp(lambda a, b: (_tn(a, b), (a, b)), lambda r, g: (_nt(r[1], g), _nn(r[0], g)))


def _silu(x):
    return x * jax.nn.sigmoid(x)


def _rms_mod(x, nw, scale, shift):
    r = lax.rsqrt(jnp.mean(x * x, axis=-1, keepdims=True) + EPS)
    return (x * r * nw) * (1.0 + scale) + shift


def _rows(S, want):
    return min(want, S)


def _my_pos():
    return lax.axis_index("x"), lax.axis_index("y"), lax.axis_index("c")


def _exchange_copies(kind, x_refs, out_refs, send_sems, recv_sems, local_sems):
    x_, y_, c_ = _my_pos()
    me = 4 * x_ + 2 * y_ + c_
    own = kind == "gather"
    cps = [pltpu.make_async_copy(x_refs[a] if own else x_refs[a].at[me], out_refs[a].at[me], local_sems.at[a])
           for a in range(len(x_refs))]
    for rel in range(1, NDEV):
        px = (x_ + ((rel >> 2) & 1)) % 2
        py = (y_ + ((rel >> 1) & 1)) % 2
        pc = (c_ + (rel & 1)) % 2
        for a in range(len(x_refs)):
            cps.append(pltpu.make_async_remote_copy(
                src_ref=x_refs[a] if own else x_refs[a].at[4 * px + 2 * py + pc], dst_ref=out_refs[a].at[me],
                send_sem=send_sems.at[rel - 1, a], recv_sem=recv_sems.at[rel - 1, a],
                device_id=(px, py, pc), device_id_type=pl.DeviceIdType.MESH))
    return cps


def _exchange_scratch(n):
    return [pltpu.SemaphoreType.DMA((NDEV - 1, n)), pltpu.SemaphoreType.DMA((NDEV - 1, n)), pltpu.SemaphoreType.DMA((n,))]


def _call(body, *, name, grid, in_specs, out_specs, out_shape, args, scratch=(), side=None):
    params = _cp(("arbitrary",) * len(grid))
    if side is None:
        return pl.pallas_call(body, name=name, grid=grid, in_specs=in_specs, out_specs=out_specs, out_shape=out_shape,
                              scratch_shapes=list(scratch), compiler_params=params)(*args)
    kind, xs = side
    n_in, n_out, n_scr, ns = len(in_specs), len(out_shape), len(scratch), len(xs)
    steps = 1
    for g in grid:
        steps *= g

    def wrapped(*refs):
        o0 = n_in + ns
        s0 = o0 + n_out + ns
        step = pl.program_id(0)
        for d in range(1, len(grid)):
            step = step * grid[d] + pl.program_id(d)

        def copies():
            return _exchange_copies(kind, refs[n_in:o0], refs[o0 + n_out:s0], *refs[s0 + n_scr:])

        @pl.when(step == 0)
        def _():
            for cp in copies():
                cp.start()

        body(*refs[:n_in], *refs[o0:o0 + n_out], *refs[s0:s0 + n_scr])

        @pl.when(step == steps - 1)
        def _():
            for cp in copies():
                cp.wait()

    any_ = pl.BlockSpec(memory_space=pl.ANY)
    side_shapes = [SDS((NDEV,) + x.shape if kind == "gather" else x.shape, x.dtype) for x in xs]
    outs = pl.pallas_call(wrapped, name=name, grid=grid, in_specs=list(in_specs) + [any_] * ns,
                          out_specs=list(out_specs) + [any_] * ns, out_shape=list(out_shape) + side_shapes,
                          scratch_shapes=list(scratch) + _exchange_scratch(ns), compiler_params=params)(*args, *xs)
    return outs[:n_out], outs[n_out:]


def inproj_fwd(x, nw, scale, shift, W, tn, name, side=None):
    S, D = x.shape
    N = W.shape[1]
    tm = _rows(S, 1024)

    def body(x_ref, nw_ref, sc_ref, sh_ref, w_ref, proj_ref, h_ref):
        @pl.when(pl.program_id(1) == 0)
        def _():
            h_ref[...] = _rms_mod(x_ref[...], nw_ref[...], sc_ref[...], sh_ref[...]).astype(bf16)

        proj_ref[...] = jnp.dot(h_ref[...], w_ref[...], preferred_element_type=f32)

    vec = pl.BlockSpec((1, D), lambda i, j: (0, 0))
    return _call(
        body, name=name, grid=(S // tm, N // tn),
        in_specs=[pl.BlockSpec((tm, D), lambda i, j: (i, 0)), vec, vec, vec, pl.BlockSpec((D, tn), lambda i, j: (0, j))],
        out_specs=[pl.BlockSpec((tm, tn), lambda i, j: (i, j)), pl.BlockSpec((tm, D), lambda i, j: (i, 0))],
        out_shape=[SDS((S, N), f32), SDS((S, D), bf16)], args=(x, nw, scale, shift, W), side=side)


def inproj_bwd_x(x, nw, scale, shift, W, dproj, dx_res, tn, name, side=None):
    S, D = x.shape
    N = W.shape[1]
    tm = _rows(S, 1024)
    nj = N // tn

    def body(x_ref, nw_ref, sc_ref, sh_ref, w_ref, dp_ref, dxr_ref, dx_ref, dnw_ref, dsc_ref, dsh_ref, acc):
        i, j = pl.program_id(0), pl.program_id(1)

        @pl.when(j == 0)
        def _():
            acc[...] = jnp.zeros_like(acc)

        @pl.when((i == 0) & (j == 0))
        def _():
            dnw_ref[...] = jnp.zeros_like(dnw_ref)
            dsc_ref[...] = jnp.zeros_like(dsc_ref)
            dsh_ref[...] = jnp.zeros_like(dsh_ref)

        acc[...] += _nt(dp_ref[...], w_ref[...])

        @pl.when(j == nj - 1)
        def _():
            _, vjp = jax.vjp(_rms_mod, x_ref[...], nw_ref[...], sc_ref[...], sh_ref[...])
            dx, dnw, dsc, dsh = vjp(acc[...])
            dx_ref[...] = dxr_ref[...] + dx
            dnw_ref[...] += dnw
            dsc_ref[...] += dsc
            dsh_ref[...] += dsh

    vec = pl.BlockSpec((1, D), lambda i, j: (0, 0))
    row = pl.BlockSpec((tm, D), lambda i, j: (i, 0))
    return _call(
        body, name=name, grid=(S // tm, nj),
        in_specs=[row, vec, vec, vec, pl.BlockSpec((D, tn), lambda i, j: (0, j)), pl.BlockSpec((tm, tn), lambda i, j: (i, j)), row],
        out_specs=[row, vec, vec, vec],
        out_shape=[SDS((S, D), f32), SDS((1, D), f32), SDS((1, D), f32), SDS((1, D), f32)],
        scratch=[pltpu.VMEM((tm, D), f32)], args=(x, nw, scale, shift, W, dproj, dx_res), side=side)


def matmul_tn(a, b, tn, name, side=None):
    S, K = a.shape
    N = b.shape[1]
    tm = _rows(S, 1024)
    ni = S // tm

    def body(a_ref, b_ref, o_ref):
        @pl.when(pl.program_id(1) == 0)
        def _():
            o_ref[...] = jnp.zeros_like(o_ref)

        o_ref[...] += _tn(a_ref[...], b_ref[...])

    return _call(
        body, name=name, grid=(N // tn, ni),
        in_specs=[pl.BlockSpec((tm, K), lambda j, i: (i, 0)), pl.BlockSpec((tm, tn), lambda j, i: (i, j))],
        out_specs=[pl.BlockSpec((K, tn), lambda j, i: (0, j))],
        out_shape=[SDS((K, N), f32)], args=(a, b), side=side)


def _conv_taps(xs, w, n_out):
    taps = []
    for j in range(4):
        s = 3 - j
        sh = xs if s == 0 else pltpu.roll(xs, s, axis=0)
        taps.append(sh[8:8 + n_out])
    conv = taps[0] * w[0] + taps[1] * w[1] + taps[2] * w[2] + taps[3] * w[3]
    return taps, conv


def _act_norm(conv, mul):
    s = _silu(conv)
    return s * (mul * lax.rsqrt(jnp.sum(s * s, axis=-1, keepdims=True) + EPS))


def gdn_prep_fwd(proj, conv_w, name):
    S = proj.shape[0]
    R = _rows(S, 512)

    def body(x_ref, w_ref, o_ref):
        j = pl.program_id(0)
        w = [w_ref[t:t + 1, :] for t in range(4)]

        def sweep(act):
            def piece(r, c):
                t0 = pl.multiple_of(r * R, R)
                cur = x_ref[pl.ds(t0, R), :]
                prev = x_ref[pl.ds(pl.multiple_of(jnp.maximum(t0 - 8, 0), 8), 8), :]
                prev = jnp.where(r == 0, 0.0, prev)
                _, conv = _conv_taps(jnp.concatenate([prev, cur], axis=0), w, R)
                o_ref[pl.ds(t0, R), :] = act(conv)
                return c

            lax.fori_loop(0, S // R, piece, 0)

        @pl.when(j < 2 * GDN_QK_HEADS)
        def _():
            sweep(lambda c: _act_norm(c, jnp.where(j < GDN_QK_HEADS, HD ** -0.5, 1.0).astype(f32)))

        @pl.when(j >= 2 * GDN_QK_HEADS)
        def _():
            sweep(_silu)

    return pl.pallas_call(
        body, name=name, grid=(GDN_CONV // 128,),
        in_specs=[pl.BlockSpec((S, 128), lambda j: (0, j)), pl.BlockSpec((4, 128), lambda j: (0, j))],
        out_specs=pl.BlockSpec((S, 128), lambda j: (0, j)),
        out_shape=SDS((S, GDN_CONV), f32),
        compiler_params=_cp(("arbitrary",)),
    )(proj, conv_w)


def gdn_prep_bwd(proj, conv_w, dqkvc, dproj, name):
    S = proj.shape[0]
    R = _rows(S, 512)
    NP = S // R

    def body(x_ref, w_ref, dn_ref, _, dx_ref, dw_ref):
        jb = pl.program_id(0)
        w = [w_ref[j:j + 1, :] for j in range(4)]

        def piece(act, r, dw):
            t0 = pl.multiple_of(r * R, R)
            cur = x_ref[pl.ds(t0, R), :]
            prev = x_ref[pl.ds(pl.multiple_of(jnp.maximum(t0 - 8, 0), 8), 8), :]
            prev = jnp.where(r == 0, 0.0, prev)
            nxt0 = pl.multiple_of(jnp.minimum(t0 + R, S - 8), 8)
            nxt = x_ref[pl.ds(nxt0, 8), :]
            dn_cur = dn_ref[pl.ds(t0, R), :]
            dn_nxt = jnp.where(r == NP - 1, 0.0, dn_ref[pl.ds(nxt0, 8), :])
            xs = jnp.concatenate([prev, cur, nxt], axis=0)
            taps, conv = _conv_taps(xs, w, R + 8)
            dn = jnp.concatenate([dn_cur, dn_nxt], axis=0)
            _, vjp = jax.vjp(act, conv)
            dxc = vjp(dn)[0]
            n = R + 8
            dx = dxc[0:R] * w[3]
            for j in range(3):
                s = 3 - j
                dx = dx + pltpu.roll(dxc, n - s, axis=0)[0:R] * w[j]
            dx_ref[pl.ds(t0, R), :] = dx
            return tuple(dw[j] + jnp.sum(dxc[0:R] * taps[j][0:R], axis=0, keepdims=True) for j in range(4))

        def sweep(act):
            dw = lax.fori_loop(0, NP, functools.partial(piece, act), tuple(jnp.zeros((1, 128), f32) for _ in range(4)))
            for j in range(4):
                dw_ref[j:j + 1, :] = dw[j]

        @pl.when(jb < 2 * GDN_QK_HEADS)
        def _():
            sweep(lambda c: _act_norm(c, jnp.where(jb < GDN_QK_HEADS, HD ** -0.5, 1.0).astype(f32)))

        @pl.when(jb >= 2 * GDN_QK_HEADS)
        def _():
            sweep(_silu)

    col = pl.BlockSpec((S, 128), lambda j: (0, j))
    wsp = pl.BlockSpec((4, 128), lambda j: (0, j))
    return pl.pallas_call(
        body, name=name, grid=(GDN_CONV // 128,),
        in_specs=[col, wsp, col, pl.BlockSpec(memory_space=pl.ANY)], out_specs=[col, wsp],
        out_shape=[SDS(dproj.shape, f32), SDS((4, GDN_CONV), f32)],
        input_output_aliases={3: 0},
        compiler_params=_cp(("arbitrary",)),
    )(proj, conv_w, dqkvc, dproj)


def _chunk_tril(R):
    ii = lax.broadcasted_iota(jnp.int32, (R, R), 0)
    jj = lax.broadcasted_iota(jnp.int32, (R, R), 1)
    return ((ii // CHUNK == jj // CHUNK) & (ii >= jj)).astype(f32)


def _gdn_gates(b, a, A_log, dt_bias, tril):
    beta = jax.nn.sigmoid(b)
    g = -jnp.exp(A_log) * jax.nn.softplus(a + dt_bias)
    return _hdot(tril, g), beta


_GDN_B_BLK = (GDN_CONV + GDN_V_W) // 128
_GDN_A_BLK = _GDN_B_BLK + 1


def gdn_gates_fwd(proj, A_log, dt_bias, name):
    S = proj.shape[0]
    R = _rows(S, 512)

    def body(b_ref, a_ref, al_ref, dt_ref, gc_ref, be_ref):
        gc, be = _gdn_gates(b_ref[...], a_ref[...], al_ref[...], dt_ref[...], _chunk_tril(R))
        gc_ref[...] = gc
        be_ref[...] = be

    vec = pl.BlockSpec((1, 128), lambda i: (0, 0))
    blk = pl.BlockSpec((R, 128), lambda i: (i, 0))
    return pl.pallas_call(
        body, name=name, grid=(S // R,),
        in_specs=[pl.BlockSpec((R, 128), lambda i: (i, _GDN_B_BLK)), pl.BlockSpec((R, 128), lambda i: (i, _GDN_A_BLK)), vec, vec],
        out_specs=[blk, blk], out_shape=[SDS((S, 128), f32), SDS((S, 128), f32)],
        compiler_params=_cp(("arbitrary",)),
    )(proj, proj, A_log, dt_bias)


def gdn_gates_bwd(proj, A_log, dt_bias, dgc, dbeta, dproj, name):
    S = proj.shape[0]
    R = _rows(S, 512)

    def body(b_ref, a_ref, al_ref, dt_ref, dgc_ref, dbe_ref, _, dp_ref, dal_ref, ddt_ref):
        @pl.when(pl.program_id(0) == 0)
        def _():
            dal_ref[...] = jnp.zeros_like(dal_ref)
            ddt_ref[...] = jnp.zeros_like(ddt_ref)

        tril = _chunk_tril(R)
        _, vjp = jax.vjp(lambda b, a, al, dt: _gdn_gates(b, a, al, dt, tril), b_ref[...], a_ref[...], al_ref[...], dt_ref[...])
        db, da, dal, ddt = vjp((dgc_ref[...], dbe_ref[...]))
        dp_ref[:, 0:128] = db
        dp_ref[:, 128:256] = da
        dal_ref[...] += dal
        ddt_ref[...] += ddt

    vec = pl.BlockSpec((1, 128), lambda i: (0, 0))
    blk = pl.BlockSpec((R, 128), lambda i: (i, 0))
    return pl.pallas_call(
        body, name=name, grid=(S // R,),
        in_specs=[pl.BlockSpec((R, 128), lambda i: (i, _GDN_B_BLK)), pl.BlockSpec((R, 128), lambda i: (i, _GDN_A_BLK)), vec, vec, blk, blk,
                  pl.BlockSpec(memory_space=pl.ANY)],
        out_specs=[pl.BlockSpec((R, 256), lambda i: (i, _GDN_B_BLK // 2)), vec, vec],
        out_shape=[SDS(dproj.shape, f32), SDS((1, 128), f32), SDS((1, 128), f32)],
        input_output_aliases={6: 0},
        compiler_params=_cp(("arbitrary",)),
    )(proj, proj, A_log, dt_bias, dgc, dbeta, dproj)


@jax.custom_vjp
def _inv_given(L, T):
    return T


def _inv_given_bwd(T, ct):
    dL = -_nt(_tn(T, ct), T)
    return dL, jnp.zeros_like(T)


_inv_given.defvjp(lambda L, T: (T, T), _inv_given_bwd)


REP = GDN_V_HEADS // GDN_QK_HEADS


def _gdn_intra_all(qs, ks, vs, gcols, bcols, Ts=None):
    H = len(vs)
    C = vs[0].shape[0]
    ii = lax.broadcasted_iota(jnp.int32, (C, C), 0)
    jj = lax.broadcasted_iota(jnp.int32, (C, C), 1)
    grows = [jnp.sum(jnp.where(ii == jj, g, 0.0), axis=0, keepdims=True) for g in gcols]
    decs = [jnp.exp(jnp.where(ii >= jj, gcols[h] - grows[h], NEG)) for h in range(H)]
    kbs = [ks[h // REP] * bcols[h] for h in range(H)]
    As = [_mm_nt(kbs[h], ks[h // REP]) for h in range(H)]
    Ls = [jnp.where(ii > jj, As[h] * decs[h], 0.0) for h in range(H)]
    if Ts is None:
        T = _neumann_inv_batched(Ls)
    else:
        T = [_inv_given(Ls[h], Ts[h]) for h in range(H)]
    us = [_mm(T[h], vs[h] * bcols[h]) for h in range(H)]
    ws = [_mm(T[h], kbs[h] * jnp.exp(gcols[h])) for h in range(H)]
    qk = [_mm_nt(qs[p], ks[p]) for p in range(H // REP)]
    return us, ws, [qk[h // REP] * decs[h] for h in range(H)], T


def _neumann_inv_batched(Ls):
    n, C = 4, Ls[0].shape[0]
    r0 = lax.broadcasted_iota(jnp.int32, (n * C, n * C), 0)
    c0 = lax.broadcasted_iota(jnp.int32, (n * C, n * C), 1)
    same = (r0 // C) == (c0 // C)

    def blockdiag(split):
        return tuple(jnp.where(same, jnp.concatenate([x] * n, axis=0), jnp.zeros((), bf16)) for x in split)

    Ms = [jnp.concatenate(Ls[b:b + n], axis=1) for b in range(0, len(Ls), n)]
    eye = (lax.broadcasted_iota(jnp.int32, (C, n * C), 0) == (lax.broadcasted_iota(jnp.int32, (C, n * C), 1) & (C - 1))).astype(f32)
    Ps = [eye - M for M in Ms]
    Ss = [_split2(M) for M in Ms]
    Bs = [blockdiag(S) for S in Ss]
    k = 1
    while 2 * k < C:
        Ss = [_split2(_dot3(S, B)) for S, B in zip(Ss, Bs)]
        Bs = [blockdiag(S) for S in Ss]
        Ps = [P + _dot3(_split2(P), B) for P, B in zip(Ps, Bs)]
        k *= 2
    return [P[:, h * C:(h + 1) * C] for P in Ps for h in range(n)]


def _gdn_scan_all(qs, ks, gcols, us, ws, attns, S0s):
    H = len(us)
    C = us[0].shape[0]
    last = lax.broadcasted_iota(jnp.int32, (C, 1), 0) == C - 1
    glast = [jnp.sum(jnp.where(last, g, 0.0), axis=0, keepdims=True) for g in gcols]
    wS = [_mm(ws[h], S0s[h]) for h in range(H)]
    qS = [_mm(qs[h // REP] * jnp.exp(gcols[h]), S0s[h]) for h in range(H)]
    vn = [us[h] - wS[h] for h in range(H)]
    av = [_mm(attns[h], vn[h]) for h in range(H)]
    kv = [_mm_tn(ks[h // REP] * jnp.exp(glast[h] - gcols[h]), vn[h]) for h in range(H)]
    return [qS[h] + av[h] for h in range(H)], [S0s[h] * jnp.exp(glast[h]) + kv[h] for h in range(H)]


def _head_cols(blk):
    lane = lax.broadcasted_iota(jnp.int32, blk.shape, 1)
    return [jnp.sum(jnp.where(lane == h, blk, 0.0), axis=1, keepdims=True) for h in range(GDN_V_HEADS)]


def _head_lanes(cols):
    lane = lax.broadcasted_iota(jnp.int32, (cols[0].shape[0], 128), 1)
    out = jnp.zeros((cols[0].shape[0], 128), f32)
    for h, c in enumerate(cols):
        out = out + jnp.where(lane == h, c, 0.0)
    return out


def _heads(ref, n):
    return [ref[:, h * HD:(h + 1) * HD].astype(f32) for h in range(n)]


def _mats(ref):
    return [ref[0, h].astype(f32) for h in range(GDN_V_HEADS)]


def _gdn_specs(NC, rv=None):
    ix = (lambda n: n) if rv is None else rv
    qs = pl.BlockSpec((CHUNK, GDN_QK_W), lambda n: (ix(n), 0))
    ks = pl.BlockSpec((CHUNK, GDN_QK_W), lambda n: (ix(n), 1))
    vs = pl.BlockSpec((CHUNK, GDN_V_W), lambda n: (ix(n), 1))
    g1 = pl.BlockSpec((CHUNK, 128), lambda n: (ix(n), 0))
    wide = pl.BlockSpec((CHUNK, GDN_V_W), lambda n: (ix(n), 0))
    sq = pl.BlockSpec((1, GDN_V_HEADS, CHUNK, CHUNK), lambda n: (ix(n), 0, 0, 0))
    st = pl.BlockSpec((1, GDN_V_HEADS, HD, HD), lambda n: (ix(n), 0, 0, 0))
    return qs, ks, vs, g1, wide, sq, st


def gdn_intra_fwd(qkvc, gc, beta, name, side=None):
    S = qkvc.shape[0]
    NC = S // CHUNK

    def body(q_ref, k_ref, v_ref, gc_ref, be_ref, u_ref, w_ref, at_ref, T_ref):
        us, ws, attns, Ts = _gdn_intra_all(_heads(q_ref, GDN_QK_HEADS), _heads(k_ref, GDN_QK_HEADS), _heads(v_ref, GDN_V_HEADS),
                                           _head_cols(gc_ref[...]), _head_cols(be_ref[...]))
        for h in range(GDN_V_HEADS):
            u_ref[:, h * HD:(h + 1) * HD] = us[h]
            w_ref[:, h * HD:(h + 1) * HD] = ws[h].astype(bf16)
            at_ref[0, h] = attns[h].astype(bf16)
            T_ref[0, h] = Ts[h].astype(bf16)

    qs, ks, vs, g1, wide, sq, _ = _gdn_specs(NC)
    return _call(
        body, name=name, grid=(NC,),
        in_specs=[qs, ks, vs, g1, g1], out_specs=[wide, wide, sq, sq],
        out_shape=[SDS((S, GDN_V_W), f32), SDS((S, GDN_V_W), bf16),
                   SDS((NC, GDN_V_HEADS, CHUNK, CHUNK), bf16), SDS((NC, GDN_V_HEADS, CHUNK, CHUNK), bf16)],
        args=(qkvc, qkvc, qkvc, gc, beta), side=side)


def gdn_scan_fwd(qkvc, gc, u, w, attn, name, side=None):
    S = qkvc.shape[0]
    NC = S // CHUNK

    def body(q_ref, k_ref, gc_ref, u_ref, w_ref, at_ref, o_ref, st_ref, state):
        @pl.when(pl.program_id(0) == 0)
        def _():
            state[...] = jnp.zeros_like(state)

        S0s = [state[h] for h in range(GDN_V_HEADS)]
        os_, S1s = _gdn_scan_all(_heads(q_ref, GDN_QK_HEADS), _heads(k_ref, GDN_QK_HEADS), _head_cols(gc_ref[...]),
                                 _heads(u_ref, GDN_V_HEADS), _heads(w_ref, GDN_V_HEADS), _mats(at_ref), S0s)
        for h in range(GDN_V_HEADS):
            o_ref[:, h * HD:(h + 1) * HD] = os_[h]
            st_ref[0, h] = S0s[h].astype(bf16)
            state[h] = S1s[h]

    qs, ks, _, g1, wide, sq, st = _gdn_specs(NC)
    return _call(
        body, name=name, grid=(NC,),
        in_specs=[qs, ks, g1, wide, wide, sq], out_specs=[wide, st],
        out_shape=[SDS((S, GDN_V_W), f32), SDS((NC, GDN_V_HEADS, HD, HD), bf16)],
        scratch=[pltpu.VMEM((GDN_V_HEADS, HD, HD), f32)], args=(qkvc, qkvc, gc, u, w, attn), side=side)


def gdn_scan_bwd(qkvc, gc, u, w, attn, states, do, name, side=None):
    S = qkvc.shape[0]
    NC = S // CHUNK

    def body(q_ref, k_ref, gc_ref, u_ref, w_ref, at_ref, st_ref, do_ref,
             dq_ref, dk_ref, dgc_ref, du_ref, dw_ref, dat_ref, dstate):
        @pl.when(pl.program_id(0) == 0)
        def _():
            dstate[...] = jnp.zeros_like(dstate)

        VH = range(GDN_V_HEADS)
        _, vjp = jax.vjp(_gdn_scan_all, _heads(q_ref, GDN_QK_HEADS), _heads(k_ref, GDN_QK_HEADS), _head_cols(gc_ref[...]),
                         _heads(u_ref, GDN_V_HEADS), _heads(w_ref, GDN_V_HEADS), _mats(at_ref), _mats(st_ref))
        dqs, dks, dgs, dus, dws, dats, dS0s = vjp((_heads(do_ref, GDN_V_HEADS), [dstate[h] for h in VH]))
        for p in range(GDN_QK_HEADS):
            dq_ref[:, p * HD:(p + 1) * HD] = dqs[p]
            dk_ref[:, p * HD:(p + 1) * HD] = dks[p]
        for h in VH:
            du_ref[:, h * HD:(h + 1) * HD] = dus[h].astype(bf16)
            dw_ref[:, h * HD:(h + 1) * HD] = dws[h].astype(bf16)
            dat_ref[0, h] = dats[h].astype(bf16)
            dstate[h] = dS0s[h]
        dgc_ref[...] = _head_lanes(dgs)

    qs, ks, _, g1, wide, sq, st = _gdn_specs(NC, lambda n: NC - 1 - n)
    dqs = pl.BlockSpec((CHUNK, GDN_QK_W), lambda n: (NC - 1 - n, 0))
    return _call(
        body, name=name, grid=(NC,),
        in_specs=[qs, ks, g1, wide, wide, sq, st, wide],
        out_specs=[dqs, dqs, g1, wide, wide, sq],
        out_shape=[SDS((S, GDN_QK_W), f32), SDS((S, GDN_QK_W), f32), SDS((S, 128), f32), SDS((S, GDN_V_W), bf16),
                   SDS((S, GDN_V_W), bf16), SDS((NC, GDN_V_HEADS, CHUNK, CHUNK), bf16)],
        scratch=[pltpu.VMEM((GDN_V_HEADS, HD, HD), f32)], args=(qkvc, qkvc, gc, u, w, attn, states, do), side=side)


def gdn_intra_bwd(qkvc, gc, beta, Ts, du, dw, dattn, dq_s, dk_s, dgc_s, name, side=None):
    S = qkvc.shape[0]
    NC = S // CHUNK

    def body(q_ref, k_ref, v_ref, gc_ref, be_ref, T_ref, du_ref, dw_ref, dat_ref, dqs_ref, dks_ref, dgs_ref,
             dqkv_ref, dgc_ref, dbe_ref):
        VH = range(GDN_V_HEADS)
        Ts = _mats(T_ref)
        _, vjp = jax.vjp(lambda q_, k_, v_, g_, b_: _gdn_intra_all(q_, k_, v_, g_, b_, Ts)[:3],
                         _heads(q_ref, GDN_QK_HEADS), _heads(k_ref, GDN_QK_HEADS), _heads(v_ref, GDN_V_HEADS),
                         _head_cols(gc_ref[...]), _head_cols(be_ref[...]))
        dqs, dks, dvs, dgs, dbs = vjp((_heads(du_ref, GDN_V_HEADS), _heads(dw_ref, GDN_V_HEADS), _mats(dat_ref)))
        for p in range(GDN_QK_HEADS):
            dqkv_ref[:, p * HD:(p + 1) * HD] = dqs[p] + dqs_ref[:, p * HD:(p + 1) * HD]
            dqkv_ref[:, GDN_QK_W + p * HD:GDN_QK_W + (p + 1) * HD] = dks[p] + dks_ref[:, p * HD:(p + 1) * HD]
        for h in VH:
            dqkv_ref[:, 2 * GDN_QK_W + h * HD:2 * GDN_QK_W + (h + 1) * HD] = dvs[h]
        dgc_ref[...] = _head_lanes(dgs) + dgs_ref[...]
        dbe_ref[...] = _head_lanes(dbs)

    qs, ks, vs, g1, wide, sq, _ = _gdn_specs(NC)
    dqs = pl.BlockSpec((CHUNK, GDN_QK_W), lambda n: (n, 0))
    return _call(
        body, name=name, grid=(NC,),
        in_specs=[qs, ks, vs, g1, g1, sq, wide, wide, sq, dqs, dqs, g1],
        out_specs=[pl.BlockSpec((CHUNK, GDN_CONV), lambda n: (n, 0)), g1, g1],
        out_shape=[SDS((S, GDN_CONV), f32), SDS((S, 128), f32), SDS((S, 128), f32)],
        args=(qkvc, qkvc, qkvc, gc, beta, Ts, du, dw, dattn, dq_s, dk_s, dgc_s), side=side)


def _gated_norm(o, z, nw):
    parts = []
    for h in range(GDN_V_HEADS):
        oh = o[:, h * HD:(h + 1) * HD]
        r = lax.rsqrt(jnp.mean(oh * oh, axis=-1, keepdims=True) + EPS)
        parts.append((oh * r * nw) * _silu(z[:, h * HD:(h + 1) * HD]))
    return jnp.concatenate(parts, axis=1)


def gdn_out_fwd(o, proj, nw, W, x, gate, name):
    S, D = x.shape
    tm = _rows(S, 256)

    def body(o_ref, z_ref, nw_ref, w_ref, x_ref, g_ref, xn_ref, y_ref, og_ref):
        og = _gated_norm(o_ref[...], z_ref[...], nw_ref[...]).astype(bf16)
        y = jnp.dot(og, w_ref[...], preferred_element_type=f32)
        og_ref[...] = og
        y_ref[...] = y
        xn_ref[...] = x_ref[...] + g_ref[...] * y

    row = pl.BlockSpec((tm, D), lambda i: (i, 0))
    wide = pl.BlockSpec((tm, GDN_V_W), lambda i: (i, 0))
    return pl.pallas_call(
        body, name=name, grid=(S // tm,),
        in_specs=[wide, pl.BlockSpec((tm, GDN_V_W), lambda i: (i, 2)), pl.BlockSpec((1, HD), lambda i: (0, 0)),
                  pl.BlockSpec((GDN_V_W, D), lambda i: (0, 0)), row, pl.BlockSpec((1, D), lambda i: (0, 0))],
        out_specs=[row, row, wide],
        out_shape=[SDS((S, D), f32), SDS((S, D), f32), SDS((S, GDN_V_W), bf16)],
        compiler_params=_cp(("arbitrary",)),
    )(o, proj, nw, W, x, gate)


def gdn_out_bwd(dxn, y, gate, o, proj, nw, W, name):
    S, D = dxn.shape
    tm = _rows(S, 256)

    def body(dx_ref, y_ref, g_ref, o_ref, z_ref, nw_ref, w_ref, dy_ref, dg_ref, do_ref, dz_ref, dnw_ref):
        @pl.when(pl.program_id(0) == 0)
        def _():
            dg_ref[...] = jnp.zeros_like(dg_ref)
            dnw_ref[...] = jnp.zeros_like(dnw_ref)

        dx = dx_ref[...]
        dy = dx * g_ref[...]
        dy_ref[...] = dy
        dg_ref[...] += jnp.sum(dx * y_ref[...], axis=0, keepdims=True)
        dog = _nt(dy, w_ref[...])
        _, vjp = jax.vjp(_gated_norm, o_ref[...], z_ref[...], nw_ref[...])
        do, dz, dnw = vjp(dog)
        do_ref[...] = do
        dz_ref[...] = dz
        dnw_ref[...] += dnw

    row = pl.BlockSpec((tm, D), lambda i: (i, 0))
    wide = pl.BlockSpec((tm, GDN_V_W), lambda i: (i, 0))
    vecd = pl.BlockSpec((1, D), lambda i: (0, 0))
    vech = pl.BlockSpec((1, HD), lambda i: (0, 0))
    return pl.pallas_call(
        body, name=name, grid=(S // tm,),
        in_specs=[row, row, vecd, wide, pl.BlockSpec((tm, GDN_V_W), lambda i: (i, 2)), vech, pl.BlockSpec((GDN_V_W, D), lambda i: (0, 0))],
        out_specs=[row, vecd, wide, pl.BlockSpec((tm, GDN_V_W), lambda i: (i, 2)), vech],
        out_shape=[SDS((S, D), f32), SDS((1, D), f32), SDS((S, GDN_V_W), f32), SDS((S, GDN_IN_PAD), f32), SDS((1, HD), f32)],
        compiler_params=_cp(("arbitrary",)),
    )(dxn, y, gate, o, proj, nw, W)


def _rms_w(x, w):
    return (x * lax.rsqrt(jnp.mean(x * x, axis=-1, keepdims=True) + EPS)) * w


def _split3(c):
    hi = c.astype(bf16).astype(f32)
    r1 = c - hi
    mid = r1.astype(bf16).astype(f32)
    lo = (r1 - mid).astype(bf16).astype(f32)
    return hi, mid, lo


_FOX_F_BLK = 4 * FOX_W // 128


def fox_prep_fwd(proj, f_bias, qn_w, kn_w, name):
    S = proj.shape[0]
    tm = _rows(S, 256)

    def body(q_ref, k_ref, v_ref, f_ref, fb_ref, qw_ref, kw_ref, Q_ref, K_ref, V_ref, carry):
        @pl.when(pl.program_id(0) == 0)
        def _():
            carry[...] = jnp.zeros_like(carry)

        ii = lax.broadcasted_iota(jnp.int32, (tm, tm), 0)
        jj = lax.broadcasted_iota(jnp.int32, (tm, tm), 1)
        lf = jax.nn.log_sigmoid(f_ref[...] + fb_ref[...])
        cum = _hdot((ii >= jj).astype(f32), lf) + carry[...]
        carry[...] = cum[tm - 1:tm, :]
        lane = lax.broadcasted_iota(jnp.int32, (tm, 128), 1)
        lo = lane < FOX_D
        qw2 = jnp.concatenate([qw_ref[...], qw_ref[...]], axis=1) * FOX_D ** -0.5
        kw2 = jnp.concatenate([kw_ref[...], kw_ref[...]], axis=1)

        def norm_pair(x, w2):
            x2 = x * x
            s_all = jnp.sum(x2, axis=1, keepdims=True)
            s_lo = jnp.sum(jnp.where(lo, x2, 0.0), axis=1, keepdims=True)
            r = jnp.where(lo, lax.rsqrt(s_lo * (1.0 / FOX_D) + EPS), lax.rsqrt((s_all - s_lo) * (1.0 / FOX_D) + EPS))
            return x * r * w2

        for p in range(FOX_H // 2):
            ps = slice(p * 128, (p + 1) * 128)
            yq, yk, xv = norm_pair(q_ref[:, ps], qw2), norm_pair(k_ref[:, ps], kw2), v_ref[:, ps]
            for e in range(2):
                h = 2 * p + e
                hi, mid, lw = _split3(cum[:, h:h + 1])
                eq = jnp.where(lane == FOX_D, hi, jnp.where(lane == FOX_D + 1, mid, jnp.where(lane == FOX_D + 2, lw, jnp.where(lane < FOX_D + 6, 1.0, 0.0))))
                ek = jnp.where(lane < FOX_D + 3, 1.0, jnp.where(lane == FOX_D + 3, -hi, jnp.where(lane == FOX_D + 4, -mid, jnp.where(lane == FOX_D + 5, -lw, 0.0))))
                ev = jnp.where(lane == FOX_D, 1.0, 0.0)
                mv = (lambda a: a) if e == 0 else (lambda a: pltpu.roll(a, FOX_D, axis=1))
                Q_ref[:, h * 128:(h + 1) * 128] = jnp.where(lo, mv(yq), eq).astype(bf16)
                K_ref[:, h * 128:(h + 1) * 128] = jnp.where(lo, mv(yk), ek).astype(bf16)
                V_ref[:, h * 128:(h + 1) * 128] = jnp.where(lo, mv(xv), ev).astype(bf16)

    def colblk(c):
        return pl.BlockSpec((tm, FOX_W), lambda i: (i, c))

    pad = pl.BlockSpec((tm, FOX_PW), lambda i: (i, 0))
    return pl.pallas_call(
        body, name=name, grid=(S // tm,),
        in_specs=[colblk(0), colblk(1), colblk(2), pl.BlockSpec((tm, 128), lambda i: (i, _FOX_F_BLK)),
                  pl.BlockSpec((1, 128), lambda i: (0, 0)), pl.BlockSpec((1, FOX_D), lambda i: (0, 0)), pl.BlockSpec((1, FOX_D), lambda i: (0, 0))],
        out_specs=[pad, pad, pad],
        out_shape=[SDS((S, FOX_PW), bf16)] * 3,
        scratch_shapes=[pltpu.VMEM((1, 128), f32)],
        compiler_params=_cp(("arbitrary",)),
    )(proj, proj, proj, proj, f_bias, qn_w, kn_w)


def fox_prep_bwd(proj, f_bias, qn_w, kn_w, dQ, dK, dV, dz, name):
    S = proj.shape[0]
    tm = _rows(S, 256)
    NB = S // tm

    def body(q_ref, k_ref, f_ref, fb_ref, qw_ref, kw_ref, dQ_ref, dK_ref, dV_ref, dz_ref,
             dp_ref, dfb_ref, dqw_ref, dkw_ref, carry):
        @pl.when(pl.program_id(0) == 0)
        def _():
            carry[...] = jnp.zeros_like(carry)
            dfb_ref[...] = jnp.zeros_like(dfb_ref)
            dqw_ref[...] = jnp.zeros_like(dqw_ref)
            dkw_ref[...] = jnp.zeros_like(dkw_ref)

        lane = lax.broadcasted_iota(jnp.int32, (tm, 128), 1)
        lo = lane < FOX_D
        qw2 = jnp.concatenate([qw_ref[...], qw_ref[...]], axis=1) * FOX_D ** -0.5
        kw2 = jnp.concatenate([kw_ref[...], kw_ref[...]], axis=1)

        def pair(ref, p):
            return jnp.where(lo, ref[:, 2 * p * 128:(2 * p + 1) * 128], pltpu.roll(ref[:, (2 * p + 1) * 128:(2 * p + 2) * 128], FOX_D, axis=1))

        def norm_pair_bwd(x, w2, dy):
            x2 = x * x
            s_all = jnp.sum(x2, axis=1, keepdims=True)
            s_lo = jnp.sum(jnp.where(lo, x2, 0.0), axis=1, keepdims=True)
            r = jnp.where(lo, lax.rsqrt(s_lo * (1.0 / FOX_D) + EPS), lax.rsqrt((s_all - s_lo) * (1.0 / FOX_D) + EPS))
            t = dy * w2 * x
            t_all = jnp.sum(t, axis=1, keepdims=True)
            t_lo = jnp.sum(jnp.where(lo, t, 0.0), axis=1, keepdims=True)
            dx = r * (w2 * dy - x * (r * r) * (jnp.where(lo, t_lo, t_all - t_lo) * (1.0 / FOX_D)))
            return dx, jnp.sum(dy * x * r, axis=0, keepdims=True)

        dcum = jnp.zeros((tm, 128), f32)
        dqw2 = jnp.zeros((1, 128), f32)
        dkw2 = jnp.zeros((1, 128), f32)
        for p in range(FOX_H // 2):
            ps = slice(p * 128, (p + 1) * 128)
            dxq, dw1 = norm_pair_bwd(q_ref[:, ps], qw2, pair(dQ_ref, p))
            dxk, dw2 = norm_pair_bwd(k_ref[:, ps], kw2, pair(dK_ref, p))
            dp_ref[:, p * 128:(p + 1) * 128] = dxq
            dp_ref[:, FOX_W + p * 128:FOX_W + (p + 1) * 128] = dxk
            dp_ref[:, 2 * FOX_W + p * 128:2 * FOX_W + (p + 1) * 128] = pair(dV_ref, p)
            dqw2 = dqw2 + dw1
            dkw2 = dkw2 + dw2
            for e in range(2):
                h = 2 * p + e
                dcum = dcum + jnp.where(lane == h, dQ_ref[:, h * 128 + FOX_D:h * 128 + FOX_D + 1]
                                        - dK_ref[:, h * 128 + FOX_D + 3:h * 128 + FOX_D + 4], 0.0)
        dp_ref[:, 3 * FOX_W:4 * FOX_W] = dz_ref[...]
        ii = lax.broadcasted_iota(jnp.int32, (tm, tm), 0)
        jj = lax.broadcasted_iota(jnp.int32, (tm, tm), 1)
        dlf = _hdot((ii <= jj).astype(f32), dcum) + carry[...]
        carry[...] += jnp.sum(dcum, axis=0, keepdims=True)
        df = dlf * jax.nn.sigmoid(-(f_ref[...] + fb_ref[...]))
        dp_ref[:, 4 * FOX_W:FOX_IN_PAD] = df
        dfb_ref[...] += jnp.sum(df, axis=0, keepdims=True)
        dqw_ref[...] += (dqw2[:, :FOX_D] + dqw2[:, FOX_D:]) * FOX_D ** -0.5
        dkw_ref[...] += dkw2[:, :FOX_D] + dkw2[:, FOX_D:]

    rv = lambda i: NB - 1 - i

    def colblk(c):
        return pl.BlockSpec((tm, FOX_W), lambda i: (rv(i), c))

    pad = pl.BlockSpec((tm, FOX_PW), lambda i: (rv(i), 0))
    cmp_ = pl.BlockSpec((tm, FOX_W), lambda i: (rv(i), 0))
    v128 = pl.BlockSpec((1, 128), lambda i: (0, 0))
    v64 = pl.BlockSpec((1, FOX_D), lambda i: (0, 0))
    return pl.pallas_call(
        body, name=name, grid=(NB,),
        in_specs=[colblk(0), colblk(1), pl.BlockSpec((tm, 128), lambda i: (rv(i), _FOX_F_BLK)), v128, v64, v64, pad, pad, pad, cmp_],
        out_specs=[pl.BlockSpec((tm, FOX_IN_PAD), lambda i: (rv(i), 0)), v128, v64, v64],
        out_shape=[SDS((S, FOX_IN_PAD), f32), SDS((1, 128), f32), SDS((1, FOX_D), f32), SDS((1, FOX_D), f32)],
        scratch_shapes=[pltpu.VMEM((1, 128), f32)],
        compiler_params=_cp(("arbitrary",)),
    )(proj, proj, proj, f_bias, qn_w, kn_w, dQ, dK, dV, dz)


FOX_HB = 2


def _diag_mask(t):
    return lax.broadcasted_iota(jnp.int32, (t, t), 1) <= lax.broadcasted_iota(jnp.int32, (t, t), 0)


def fox_attn_fwd(Q, K, V, name, side=None):
    S = Q.shape[0]
    t = _rows(S, 512)

    HB = FOX_HB
    HS = [slice(h * 128, (h + 1) * 128) for h in range(HB)]

    def body(q_ref, k_ref, v_ref, o_ref, m_sc, acc_sc, s_sc):
        i = pl.program_id(1)
        qs = [q_ref[:, sl] for sl in HS]
        m_sc[...] = jnp.full_like(m_sc, NEG)
        acc_sc[...] = jnp.zeros_like(acc_sc)

        def scores(j):
            j0 = pl.multiple_of(j * t, t)
            return [_nt(qs[h], k_ref[pl.ds(j0, t), HS[h]]) for h in range(HB)]

        def tile(j, diag):
            j0 = pl.multiple_of(j * t, t)
            ss = [s_sc[h] for h in range(HB)]
            if diag:
                ss = [jnp.where(_diag_mask(t), s, NEG) for s in ss]
            else:
                nxt = scores(j + 1)
            ms = [m_sc[h] for h in range(HB)]
            m_new = [jnp.maximum(ms[h], jnp.max(ss[h], axis=1, keepdims=True)) for h in range(HB)]
            ps = [jnp.exp(ss[h] - m_new[h]) for h in range(HB)]
            pv = [_nn(ps[h], v_ref[pl.ds(j0, t), HS[h]]) for h in range(HB)]
            for h in range(HB):
                acc_sc[h] = acc_sc[h] * jnp.exp(ms[h] - m_new[h]) + pv[h]
                m_sc[h] = m_new[h]
                if not diag:
                    s_sc[h] = nxt[h]

        def off_diag(j, c):
            tile(j, False)
            return c

        first = scores(0)
        for h in range(HB):
            s_sc[h] = first[h]
        lax.fori_loop(0, i, off_diag, 0)
        tile(i, True)
        lane = lax.broadcasted_iota(jnp.int32, (t, 128), 1)
        for h in range(HB):
            acc = acc_sc[h]
            l = acc[:, FOX_D:FOX_D + 1]
            o_ref[:, HS[h]] = jnp.where(lane == FOX_D, m_sc[h] + jnp.log(l), acc / l)

    blk = pl.BlockSpec((t, HB * 128), lambda h, i: (i, h))
    seq = pl.BlockSpec((S, HB * 128), lambda h, i: (0, h))
    return _call(
        body, name=name, grid=(FOX_H // HB, S // t),
        in_specs=[blk, seq, seq], out_specs=[blk], out_shape=[SDS((S, FOX_PW), f32)],
        scratch=[pltpu.VMEM((HB, t, 1), f32), pltpu.VMEM((HB, t, 128), f32), pltpu.VMEM((HB, t, t), f32)],
        args=(Q, K, V), side=side)


def fox_attn_bwd(Q, K, V, dO, O, name, side=None):
    S = Q.shape[0]
    t = _rows(S, 512)
    nq = S // t

    HB = FOX_HB
    HS = [slice(h * 128, (h + 1) * 128) for h in range(HB)]

    def body(k_ref, v_ref, q_ref, do_ref, o_ref, dq_ref, dk_ref, dv_ref):
        j = pl.program_id(1)

        @pl.when(j == 0)
        def _():
            dq_ref[...] = jnp.zeros_like(dq_ref)

        dk_ref[...] = jnp.zeros_like(dk_ref)
        dv_ref[...] = jnp.zeros_like(dv_ref)
        ks = [k_ref[:, sl] for sl in HS]
        vs = [v_ref[:, sl] for sl in HS]

        def tile(i, diag):
            i0 = pl.multiple_of(i * t, t)
            R = range(HB)
            qs = [q_ref[pl.ds(i0, t), HS[h]] for h in R]
            dos = [do_ref[pl.ds(i0, t), HS[h]] for h in R]
            ss = [_nt(qs[h], ks[h]) - o_ref[pl.ds(i0, t), h * 128 + FOX_D:h * 128 + FOX_D + 1] for h in R]
            if diag:
                ss = [jnp.where(_diag_mask(t), s, NEG) for s in ss]
            ps = [jnp.exp(s) for s in ss]
            dps = [_nt(dos[h], vs[h]) for h in R]
            dvs = [_tn(ps[h], dos[h]) for h in R]
            dss = [(ps[h] * dps[h]).astype(bf16) for h in R]
            dks = [_tn(dss[h], qs[h]) for h in R]
            dqs = [_nn(dss[h], ks[h]) for h in R]
            for h in R:
                dv_ref[:, HS[h]] += dvs[h]
                dk_ref[:, HS[h]] += dks[h]
                dq_ref[pl.ds(i0, t), HS[h]] += dqs[h]

        tile(j, True)

        def off_diag(i, c):
            tile(i, False)
            return c

        lax.fori_loop(j + 1, nq, off_diag, 0)

    blk = pl.BlockSpec((t, HB * 128), lambda h, j: (j, h))
    seq = pl.BlockSpec((S, HB * 128), lambda h, j: (0, h))
    return _call(
        body, name=name, grid=(FOX_H // HB, nq),
        in_specs=[blk, blk, seq, seq, seq], out_specs=[seq, blk, blk],
        out_shape=[SDS((S, FOX_PW), f32)] * 3, args=(K, V, Q, dO, O), side=side)


def fox_out_fwd(O, proj, W, x, gate, name):
    S, D = x.shape
    tm = _rows(S, 256)

    def body(o_ref, z_ref, w_ref, x_ref, g_ref, xn_ref, y_ref, og_ref):
        z = z_ref[...]
        og = jnp.concatenate([o_ref[:, h * 128:h * 128 + FOX_D] * _silu(z[:, h * FOX_D:(h + 1) * FOX_D]) for h in range(FOX_H)],
                             axis=1).astype(bf16)
        y = jnp.dot(og, w_ref[...], preferred_element_type=f32)
        og_ref[...] = og
        y_ref[...] = y
        xn_ref[...] = x_ref[...] + g_ref[...] * y

    row = pl.BlockSpec((tm, D), lambda i: (i, 0))
    cmp_ = pl.BlockSpec((tm, FOX_W), lambda i: (i, 0))
    return pl.pallas_call(
        body, name=name, grid=(S // tm,),
        in_specs=[pl.BlockSpec((tm, FOX_PW), lambda i: (i, 0)), pl.BlockSpec((tm, FOX_W), lambda i: (i, 3)),
                  pl.BlockSpec((FOX_W, D), lambda i: (0, 0)), row, pl.BlockSpec((1, D), lambda i: (0, 0))],
        out_specs=[row, row, cmp_],
        out_shape=[SDS((S, D), f32), SDS((S, D), f32), SDS((S, FOX_W), bf16)],
        compiler_params=_cp(("arbitrary",)),
    )(O, proj, W, x, gate)


def fox_out_bwd(dxn, y, gate, O, proj, W, name):
    S, D = dxn.shape
    tm = _rows(S, 256)

    def body(dx_ref, y_ref, g_ref, o_ref, z_ref, w_ref, dy_ref, dg_ref, dO_ref, dz_ref):
        @pl.when(pl.program_id(0) == 0)
        def _():
            dg_ref[...] = jnp.zeros_like(dg_ref)

        dx = dx_ref[...]
        dy = dx * g_ref[...]
        dy_ref[...] = dy
        dg_ref[...] += jnp.sum(dx * y_ref[...], axis=0, keepdims=True)
        dog = _nt(dy, w_ref[...])
        z = z_ref[...]
        lane = lax.broadcasted_iota(jnp.int32, (tm, FOX_D), 1)
        dzs = []
        for h in range(FOX_H):
            sl = slice(h * FOX_D, (h + 1) * FOX_D)
            zh = z[:, sl]
            sg = jax.nn.sigmoid(zh)
            oh = o_ref[:, h * 128:h * 128 + FOX_D]
            doh = dog[:, sl] * (zh * sg)
            delta = jnp.sum(doh * oh, axis=1, keepdims=True)
            dO_ref[:, h * 128:(h + 1) * 128] = jnp.concatenate([doh, jnp.where(lane == 0, -delta, 0.0)], axis=1).astype(bf16)
            dzs.append(dog[:, sl] * oh * (sg * (1.0 + zh * (1.0 - sg))))
        dz_ref[...] = jnp.concatenate(dzs, axis=1)

    row = pl.BlockSpec((tm, D), lambda i: (i, 0))
    vecd = pl.BlockSpec((1, D), lambda i: (0, 0))
    pad = pl.BlockSpec((tm, FOX_PW), lambda i: (i, 0))
    return pl.pallas_call(
        body, name=name, grid=(S // tm,),
        in_specs=[row, row, vecd, pad, pl.BlockSpec((tm, FOX_W), lambda i: (i, 3)), pl.BlockSpec((FOX_W, D), lambda i: (0, 0))],
        out_specs=[row, vecd, pad, pl.BlockSpec((tm, FOX_W), lambda i: (i, 0))],
        out_shape=[SDS((S, D), f32), SDS((1, D), f32), SDS((S, FOX_PW), bf16), SDS((S, FOX_W), f32)],
        compiler_params=_cp(("arbitrary",)),
    )(dxn, y, gate, O, proj, W)


def final_loss(x, fw, target, name):
    S, D = x.shape
    tm = _rows(S, 512)

    def body(x_ref, w_ref, t_ref, l_ref, dx_ref, dw_ref):
        @pl.when(pl.program_id(0) == 0)
        def _():
            l_ref[...] = jnp.zeros_like(l_ref)
            dw_ref[...] = jnp.zeros_like(dw_ref)

        out, vjp = jax.vjp(_rms_w, x_ref[...], w_ref[...])
        err = out - t_ref[...]
        l_ref[...] += 0.5 * jnp.sum(jnp.sum(err * err, axis=1, keepdims=True) * (1.0 / D), axis=0, keepdims=True)
        dx, dw = vjp(err * (1.0 / D))
        dx_ref[...] = dx
        dw_ref[...] += dw

    row = pl.BlockSpec((tm, D), lambda i: (i, 0))
    vec = pl.BlockSpec((1, D), lambda i: (0, 0))
    return pl.pallas_call(
        body, name=name, grid=(S // tm,),
        in_specs=[row, vec, row], out_specs=[pl.BlockSpec((1, 128), lambda i: (0, 0)), row, vec],
        out_shape=[SDS((1, 128), f32), SDS((S, D), f32), SDS((1, D), f32)],
        compiler_params=_cp(("arbitrary",)),
    )(x, fw, target)


def ada_fwd(c_all, ada_w, name):
    L, D, n = ada_w.shape

    def body(c_ref, w_ref, o_ref):
        cond = jnp.concatenate([_silu(c_ref[...]), jnp.zeros((8, D), f32)], axis=0)
        o_ref[0] = _nn(cond, w_ref[0])[0:8]

    return pl.pallas_call(
        body, name=name, grid=(L,),
        in_specs=[pl.BlockSpec((NDEV, D), lambda l: (0, 0)), pl.BlockSpec((1, D, n), lambda l: (l, 0, 0))],
        out_specs=pl.BlockSpec((1, NDEV, n), lambda l: (l, 0, 0)),
        out_shape=SDS((L, NDEV, n), f32),
        compiler_params=_cp(("arbitrary",)),
    )(c_all, ada_w)


def ada_grad(c_all, dmod, name):
    L, _, n = dmod.shape
    D = c_all.shape[1]

    def body(c_ref, d_ref, o_ref):
        cond = jnp.concatenate([_silu(c_ref[...]), jnp.zeros((8, D), f32)], axis=0)
        dm = jnp.concatenate([d_ref[0], jnp.zeros((8, n), f32)], axis=0)
        o_ref[0] = _tn(cond, dm)

    return pl.pallas_call(
        body, name=name, grid=(L,),
        in_specs=[pl.BlockSpec((NDEV, D), lambda l: (0, 0)), pl.BlockSpec((1, NDEV, n), lambda l: (l, 0, 0))],
        out_specs=pl.BlockSpec((1, D, n), lambda l: (l, 0, 0)),
        out_shape=SDS((L, D, n), f32),
        compiler_params=_cp(("arbitrary",)),
    )(c_all, dmod)


def reduce_adam(parts, w, m, v, tr, name):
    n, R, C = parts.shape
    c1 = 1.0 / (1.0 - ADAM_B1 ** ADAM_STEP)
    c2 = 1.0 / (1.0 - ADAM_B2 ** ADAM_STEP)

    def body(p_ref, w_ref, m_ref, v_ref, g_ref, d_ref, nm_ref, nv_ref):
        g = p_ref[0].astype(f32)
        for s in range(1, n):
            g = g + p_ref[s].astype(f32)
        nm = ADAM_B1 * m_ref[...] + (1.0 - ADAM_B1) * g
        nv = ADAM_B2 * v_ref[...] + (1.0 - ADAM_B2) * (g * g)
        g_ref[...] = g
        nm_ref[...] = nm
        nv_ref[...] = nv
        d_ref[...] = -ADAM_LR * ((nm * c1) / (jnp.sqrt(nv * c2) + ADAM_EPS) + ADAM_WD * w_ref[...])

    blk = pl.BlockSpec((tr, C), lambda i: (i, 0))
    return pl.pallas_call(
        body, name=name, grid=(R // tr,),
        in_specs=[pl.BlockSpec((n, tr, C), lambda i: (0, i, 0)), blk, blk, blk],
        out_specs=[blk] * 4, out_shape=[SDS((R, C), f32)] * 4,
        compiler_params=_cp(("arbitrary",)),
    )(parts, w, m, v)


def all_gather(xs, name):
    n = len(xs)

    def body(*refs):
        x_refs, out_refs = refs[:n], refs[n:2 * n]
        send_sems, recv_sems, local_sems = refs[2 * n:]
        x_, y_, c_ = _my_pos()
        me, sibling = (x_, y_, c_), (x_, y_, 1 - c_)
        chips = [(1 - x_, y_), (x_, 1 - y_), (1 - x_, 1 - y_)]

        def rows(a, px, py, pc):
            return out_refs[a].at[4 * px + 2 * py + pc]

        def copy(a, k, block, to, own=False):
            return pltpu.make_async_remote_copy(
                src_ref=x_refs[a] if own else rows(a, *block), dst_ref=rows(a, *block),
                send_sem=send_sems.at[k, a], recv_sem=recv_sems.at[k, a], device_id=to, device_id_type=pl.DeviceIdType.MESH)

        mine = [pltpu.make_async_copy(x_refs[a], rows(a, *me), local_sems.at[a]) for a in range(n)]
        for cp in mine:
            cp.start()
        first = []
        for a in range(n):
            first.append(copy(a, 0, me, sibling, own=True))
            first += [copy(a, 1 + j, me, (*chip, c_), own=True) for j, chip in enumerate(chips)]
        for cp in first:
            cp.start()
        passed = []
        for j, chip in enumerate(chips):
            for a in range(n):
                copy(a, 1 + j, (*chip, c_), me).wait_recv()
                cp = copy(a, 4 + j, (*chip, c_), sibling)
                cp.start()
                passed.append(cp)
        for a in range(n):
            copy(a, 0, sibling, me).wait_recv()
            for j, chip in enumerate(chips):
                copy(a, 4 + j, (*chip, 1 - c_), me).wait_recv()
        for cp in first + passed:
            cp.wait_send()
        for cp in mine:
            cp.wait()

    any_ = pl.BlockSpec(memory_space=pl.ANY)
    return pl.pallas_call(
        body, name=name, out_shape=[SDS((NDEV,) + x.shape, x.dtype) for x in xs],
        in_specs=[any_] * n, out_specs=[any_] * n,
        scratch_shapes=[pltpu.SemaphoreType.DMA((7, n)), pltpu.SemaphoreType.DMA((7, n)), pltpu.SemaphoreType.DMA((n,))],
    )(*xs)


GDN_COLS = ((0, GDN_CONV + GDN_V_W, 0), (GDN_CONV + GDN_V_W, GDN_CONV + GDN_V_W + 16, GDN_CONV + GDN_V_W),
            (GDN_CONV + GDN_V_W + 16, GDN_IN, GDN_CONV + GDN_V_W + 128))
FOX_COLS = ((0, FOX_IN, 0),)


def _col_pieces(d, per, cols):
    lo, hi = per * d, per * (d + 1)
    out = []
    for a, b, dst in cols:
        s, e = max(lo, a), min(hi, b)
        if s < e:
            out.append((s - lo, e - s, dst + s - a))
    return out


def cols_from_blocks(g, cols, n_out, name):
    _, L, R, C = g.shape
    tr = min(256, R)

    def body(g_ref, o_ref):
        o_ref[...] = jnp.zeros_like(o_ref)
        for d in range(NDEV):
            for off, ln, dst in _col_pieces(d, C, cols):
                o_ref[0, :, dst:dst + ln] = g_ref[d, 0, :, off:off + ln]

    return pl.pallas_call(
        body, name=name, grid=(L, R // tr),
        in_specs=[pl.BlockSpec((NDEV, 1, tr, C), lambda l, i: (0, l, i, 0))],
        out_specs=pl.BlockSpec((1, tr, n_out), lambda l, i: (l, i, 0)),
        out_shape=SDS((L, R, n_out), g.dtype),
        compiler_params=_cp(("arbitrary", "arbitrary")),
    )(g)


def blocks_from_cols(dw, C, cols, name):
    R, n_in = dw.shape
    tr = min(256, R)

    def body(x_ref, o_ref):
        for d in range(NDEV):
            for off, ln, src in _col_pieces(d, C, cols):
                o_ref[d, :, off:off + ln] = x_ref[:, src:src + ln].astype(bf16)

    return pl.pallas_call(
        body, name=name, grid=(R // tr,),
        in_specs=[pl.BlockSpec((tr, n_in), lambda i: (i, 0))],
        out_specs=pl.BlockSpec((NDEV, tr, C), lambda i: (0, i, 0)),
        out_shape=SDS((NDEV, R, C), bf16),
        compiler_params=_cp(("arbitrary",)),
    )(dw)


BIG = ("a_w_in", "a_conv_w", "a_w_out", "b_w_in", "b_w_out")
SMALL = ("norm_w", "ada_b", "a_A_log", "a_dt_bias", "a_norm_w", "b_f_bias", "b_qn_w", "b_kn_w", "final_norm_w")


def _pack_small(arrs):
    rows = []
    for a in arrs:
        fl = a.reshape(-1)
        pad = (-fl.shape[0]) % 128
        if pad:
            fl = jnp.concatenate([fl, jnp.zeros((pad,), fl.dtype)])
        rows.append(fl)
    flat = jnp.concatenate(rows)
    pad = (-flat.shape[0]) % (8 * 128)
    if pad:
        flat = jnp.concatenate([flat, jnp.zeros((pad,), flat.dtype)])
    return flat.reshape(-1, 128)


def _unpack(packed, shapes, align):
    flat = packed.reshape(-1)
    out, off = [], 0
    for shp in shapes:
        n = 1
        for d in shp:
            n *= d
        out.append(flat[off:off + n].reshape(shp))
        off += n + ((-n) % align)
    return out


def _full_from_gathered(g, shard_shape, axis):
    g = jnp.moveaxis(g, 0, axis)
    shp = list(shard_shape)
    shp[axis] *= NDEV
    return g.reshape(shp)


def _pad_lanes(v, n=128):
    v = v.reshape(1, -1)
    return jnp.concatenate([v, jnp.zeros((1, n - v.shape[1]), v.dtype)], axis=1)


def _carried(fn, *args, side=None, **grads):
    if callable(side):
        side = side(**grads)
    res = fn(*args, side)
    return res if side is not None else (res, None)


def gdn_layer_fwd(x, mod, nw, weights, tag, sides):
    W_in, conv_w, A_log, dt_bias, a_nw, W_out = weights
    shift, scale, gate = mod
    got = {}
    (proj, h), got["inproj"] = _carried(inproj_fwd, x, nw, scale, shift, W_in, GDN_TN, f"{tag}_inproj", side=sides.get("inproj"))
    qkvc = gdn_prep_fwd(proj, conv_w, f"{tag}_prep")
    gc, beta = gdn_gates_fwd(proj, A_log, dt_bias, f"{tag}_gates")
    (u, w, attn, Ts), got["intra"] = _carried(gdn_intra_fwd, qkvc, gc, beta, f"{tag}_intra", side=sides.get("intra"))
    (o, states), got["scan"] = _carried(gdn_scan_fwd, qkvc, gc, u, w, attn, f"{tag}_scan", side=sides.get("scan"))
    x_new, y, og = gdn_out_fwd(o, proj, a_nw, W_out, x, gate, f"{tag}_out")
    return x_new, (x, proj, h, qkvc, gc, beta, o, states, Ts, y, og, u, w, attn), got


def gdn_layer_bwd(dxn, saved, mod, nw, weights, tag, sides):
    W_in, conv_w, A_log, dt_bias, a_nw, W_out = weights
    shift, scale, gate = mod
    x, proj, h, qkvc, gc, beta, o, states, Ts, y, og, u, w, attn = saved
    got = {}
    dy, dgate, do, dproj, da_nw = gdn_out_bwd(dxn, y, gate, o, proj, a_nw, W_out, f"{tag}_out_bwd")
    dW_out, = matmul_tn(og, dy, 512, f"{tag}_dwout")
    (dq_s, dk_s, dgc_s, du, dw, dattn), got["sbwd"] = _carried(
        gdn_scan_bwd, qkvc, gc, u, w, attn, states, do, f"{tag}_scan_bwd", side=sides.get("sbwd"), dW_out=dW_out)
    (dqkvc, dgc, dbeta), got["intrab"] = _carried(
        gdn_intra_bwd, qkvc, gc, beta, Ts, du, dw, dattn, dq_s, dk_s, dgc_s, f"{tag}_intra_bwd", side=sides.get("intrab"), dW_out=dW_out)
    dproj, dA_log, ddt = gdn_gates_bwd(proj, A_log, dt_bias, dgc, dbeta, dproj, f"{tag}_gates_bwd")
    dproj, dconv_w = gdn_prep_bwd(proj, conv_w, dqkvc, dproj, f"{tag}_prep_bwd")
    (dW_in,), got["dwin"] = _carried(matmul_tn, h, dproj, GDN_TN, f"{tag}_dwin", side=sides.get("dwin"), dW_out=dW_out)
    (dx, dnw, dscale, dshift), got["ibwd"] = _carried(
        inproj_bwd_x, x, nw, scale, shift, W_in, dproj, dxn, GDN_TN, f"{tag}_inproj_bwd", side=sides.get("ibwd"),
        dW_out=dW_out, dW_in=dW_in, dconv_w=dconv_w)
    grads = dict(norm_w=dnw, W_in=dW_in, conv_w=dconv_w, A_log=dA_log[:, :16], dt_bias=ddt[:, :16], a_nw=da_nw, W_out=dW_out,
                 dmod=jnp.concatenate([dshift, dscale, dgate], axis=1))
    return dx, grads, got


def fox_layer_fwd(x, mod, nw, weights, tag, sides):
    W_in, f_bias, qn_w, kn_w, W_out = weights
    shift, scale, gate = mod
    got = {}
    (proj, h), got["inproj"] = _carried(inproj_fwd, x, nw, scale, shift, W_in, FOX_TN, f"{tag}_inproj", side=sides.get("inproj"))
    Q, K, V = fox_prep_fwd(proj, f_bias, qn_w, kn_w, f"{tag}_prep")
    (O,), got["attn"] = _carried(fox_attn_fwd, Q, K, V, f"{tag}_attn", side=sides.get("attn"))
    x_new, y, og = fox_out_fwd(O, proj, W_out, x, gate, f"{tag}_out")
    return x_new, (x, proj, h, Q, K, V, O, y, og), got


def fox_layer_bwd(dxn, saved, mod, nw, weights, tag, sides):
    W_in, f_bias, qn_w, kn_w, W_out = weights
    shift, scale, gate = mod
    x, proj, h, Q, K, V, O, y, og = saved
    got = {}
    dy, dgate, dO, dz = fox_out_bwd(dxn, y, gate, O, proj, W_out, f"{tag}_out_bwd")
    dW_out, = matmul_tn(og, dy, 512, f"{tag}_dwout")
    (dQ, dK, dV), got["abwd"] = _carried(fox_attn_bwd, Q, K, V, dO, O, f"{tag}_attn_bwd", side=sides.get("abwd"))
    dproj, dfb, dqw, dkw = fox_prep_bwd(proj, f_bias, qn_w, kn_w, dQ, dK, dV, dz, f"{tag}_prep_bwd")
    (dW_in,), got["dwin"] = _carried(matmul_tn, h, dproj, FOX_TN, f"{tag}_dwin", side=sides.get("dwin"))
    dx, dnw, dscale, dshift = inproj_bwd_x(x, nw, scale, shift, W_in, dproj, dxn, FOX_TN, f"{tag}_inproj_bwd")
    grads = dict(norm_w=dnw, W_in=dW_in, f_bias=dfb[:, :16], qn_w=dqw, kn_w=dkw, W_out=dW_out,
                 dmod=jnp.concatenate([dshift, dscale, dgate], axis=1))
    return dx, grads, got


class LocalPlan:
    def __init__(self, full):
        self.full = full

    def layer_weights(self, i):
        j, f = i // 2, self.full
        return (f["a_w_in"][j], f["a_w_out"][j], f["a_conv_w"][j]) if i % 2 == 0 else (f["b_w_in"][j], f["b_w_out"][j])

    def fwd_sides(self, i):
        return {}

    def fwd_got(self, i, got):
        pass

    def bwd_sides(self, i):
        return {}

    def bwd_got(self, i, grads, got):
        pass


def device_step(x, mod_all, norm_w, small, final_norm_w, target, plan):
    D = x.shape[1]
    mods = [(mod_all[i:i + 1, 0:D], mod_all[i:i + 1, D:2 * D], mod_all[i:i + 1, 2 * D:3 * D]) for i in range(4)]

    def weights(i):
        j = i // 2
        if i % 2 == 0:
            W_in, W_out, conv_w = plan.layer_weights(i)
            return (W_in, conv_w, _pad_lanes(small["a_A_log"][j]), _pad_lanes(small["a_dt_bias"][j]), small["a_norm_w"][j:j + 1], W_out)
        W_in, W_out = plan.layer_weights(i)
        return (W_in, _pad_lanes(small["b_f_bias"][j]), small["b_qn_w"][j:j + 1], small["b_kn_w"][j:j + 1], W_out)

    saved, wts = [], []
    for i in range(4):
        wts.append(weights(i))
        fwd = gdn_layer_fwd if i % 2 == 0 else fox_layer_fwd
        x, sv, got = fwd(x, mods[i], norm_w[i:i + 1], wts[i], f"L{i}", plan.fwd_sides(i))
        plan.fwd_got(i, got)
        saved.append(sv)
    loss, dx, dfw = final_loss(x, final_norm_w.reshape(1, D), target, "final_loss")
    lg = [None] * 4
    for i in reversed(range(4)):
        bwd = gdn_layer_bwd if i % 2 == 0 else fox_layer_bwd
        dx, lg[i], got = bwd(dx, saved[i], mods[i], norm_w[i:i + 1], wts[i], f"L{i}", plan.bwd_sides(i))
        plan.bwd_got(i, lg[i], got)
    g = dict(
        norm_w=jnp.concatenate([lg[i]["norm_w"] for i in range(4)], axis=0),
        dmod=jnp.concatenate([lg[i]["dmod"] for i in range(4)], axis=0),
        a_w_in=[lg[i]["W_in"] for i in (0, 2)],
        a_conv_w=jnp.stack([lg[i]["conv_w"] for i in (0, 2)]),
        a_A_log=jnp.concatenate([lg[i]["A_log"] for i in (0, 2)], axis=0),
        a_dt_bias=jnp.concatenate([lg[i]["dt_bias"] for i in (0, 2)], axis=0),
        a_norm_w=jnp.concatenate([lg[i]["a_nw"] for i in (0, 2)], axis=0),
        a_w_out=[lg[i]["W_out"] for i in (0, 2)],
        b_w_in=[lg[i]["W_in"] for i in (1, 3)],
        b_f_bias=jnp.concatenate([lg[i]["f_bias"] for i in (1, 3)], axis=0),
        b_qn_w=jnp.concatenate([lg[i]["qn_w"] for i in (1, 3)], axis=0),
        b_kn_w=jnp.concatenate([lg[i]["kn_w"] for i in (1, 3)], axis=0),
        b_w_out=[lg[i]["W_out"] for i in (1, 3)],
        final_norm_w=dfw.reshape(-1),
    )
    return loss[0, 0], dx, g


class MeshPlan:
    def __init__(self, shards, w0, conv_full):
        self.shards = shards
        self.w = {0: w0}
        self.conv = conv_full
        self.recv = {}
        self.pending = {}
        self.names = {}

    def layer_weights(self, i):
        return self.w[i]

    def _gather_side(self, layer):
        names = ("a_w_in", "a_w_out") if layer % 2 == 0 else ("b_w_in", "b_w_out")
        out = []
        for n in names:
            sh = self.shards[n][layer // 2]
            out.append(sh.reshape(-1, sh.shape[-1]))
        return ("gather", out)

    def fwd_sides(self, i):
        if i == 0:
            return {"scan": self._gather_side(1), "intra": self._gather_side(2)}
        if i == 1:
            return {"attn": self._gather_side(3)}
        return {}

    def fwd_got(self, i, got):
        for key, layer in (("scan", 1), ("intra", 2), ("attn", 3)):
            if got.get(key) is None:
                continue
            g_in, g_out = got[key]
            D = g_out.shape[-1]
            j = layer // 2
            if layer % 2 == 1:
                W_in = cols_from_blocks(g_in[:, None], FOX_COLS, FOX_IN_PAD, f"b_w_in_cols{j}")[0]
                self.w[layer] = (W_in, g_out.reshape(-1, D))
            else:
                W_in = cols_from_blocks(g_in[:, None], GDN_COLS, GDN_IN_PAD, f"a_w_in_cols{j}")[0]
                self.w[layer] = (W_in, g_out.reshape(-1, D), self.conv[j])

    @staticmethod
    def _out_blocks(dW_out):
        return dW_out.astype(bf16).reshape(NDEV, -1, dW_out.shape[-1])

    def _in_blocks(self, name, j, dW_in):
        cols = GDN_COLS if name == "a_w_in" else FOX_COLS
        return blocks_from_cols(dW_in, self.shards[name].shape[-1], cols, f"{name}_blocks{j}")

    def bwd_sides(self, i):
        self.names = {}
        sides = {}
        for (layer, key) in [k for k in self.pending if k[0] == i]:
            self.names[key], arrs = self.pending.pop((layer, key))
            sides[key] = ("scatter", arrs)
        if i == 0:
            def sbwd(dW_out):
                self.names["sbwd"] = [("a_w_out", 0)]
                return ("scatter", [self._out_blocks(dW_out)])

            def ibwd(dW_out, dW_in, dconv_w):
                conv = jnp.stack([dconv_w, self._dconv1])
                n = conv.shape[-1] // NDEV
                self.names["ibwd"] = [("a_w_in", 0), ("a_conv_w", None)]
                return ("scatter", [self._in_blocks("a_w_in", 0, dW_in),
                                    jnp.moveaxis(conv.reshape(2, 4, NDEV, n), 2, 0).reshape(NDEV, 8, n)])

            sides["sbwd"], sides["ibwd"] = sbwd, ibwd
        return sides

    def bwd_got(self, i, grads, got):
        for key, arrs in got.items():
            if arrs is not None:
                self.recv.update(zip(self.names[key], arrs))
        j = i // 2
        if i % 2 == 1:
            self.pending[(i - 1, "sbwd" if i == 3 else "intrab")] = (
                [("b_w_in", j), ("b_w_out", j)], [self._in_blocks("b_w_in", j, grads["W_in"]), self._out_blocks(grads["W_out"])])
        elif i == 2:
            self.pending[(1, "abwd")] = (
                [("a_w_in", 1), ("a_w_out", 1)], [self._in_blocks("a_w_in", 1, grads["W_in"]), self._out_blocks(grads["W_out"])])
            self._dconv1 = grads["conv_w"]


def kernel(x, c, norm_w, ada_w, ada_b, a_w_in, a_conv_w, a_A_log, a_dt_bias, a_norm_w, a_w_out, b_w_in, b_f_bias, b_qn_w, b_kn_w, b_w_out, final_norm_w, loss_target, m_norm_w, m_ada_w, m_ada_b, m_a_w_in, m_a_conv_w, m_a_A_log, m_a_dt_bias, m_a_norm_w, m_a_w_out, m_b_w_in, m_b_f_bias, m_b_qn_w, m_b_kn_w, m_b_w_out, m_final_norm_w, v_norm_w, v_ada_w, v_ada_b, v_a_w_in, v_a_conv_w, v_a_A_log, v_a_dt_bias, v_a_norm_w, v_a_w_out, v_b_w_in, v_b_f_bias, v_b_qn_w, v_b_kn_w, v_b_w_out, v_final_norm_w):
    W = dict(norm_w=norm_w, ada_w=ada_w, ada_b=ada_b, a_w_in=a_w_in, a_conv_w=a_conv_w, a_A_log=a_A_log, a_dt_bias=a_dt_bias,
             a_norm_w=a_norm_w, a_w_out=a_w_out, b_w_in=b_w_in, b_f_bias=b_f_bias, b_qn_w=b_qn_w, b_kn_w=b_kn_w, b_w_out=b_w_out,
             final_norm_w=final_norm_w)
    M = dict(norm_w=m_norm_w, ada_w=m_ada_w, ada_b=m_ada_b, a_w_in=m_a_w_in, a_conv_w=m_a_conv_w, a_A_log=m_a_A_log,
             a_dt_bias=m_a_dt_bias, a_norm_w=m_a_norm_w, a_w_out=m_a_w_out, b_w_in=m_b_w_in, b_f_bias=m_b_f_bias, b_qn_w=m_b_qn_w,
             b_kn_w=m_b_kn_w, b_w_out=m_b_w_out, final_norm_w=m_final_norm_w)
    V = dict(norm_w=v_norm_w, ada_w=v_ada_w, ada_b=v_ada_b, a_w_in=v_a_w_in, a_conv_w=v_a_conv_w, a_A_log=v_a_A_log,
             a_dt_bias=v_a_dt_bias, a_norm_w=v_a_norm_w, a_w_out=v_a_w_out, b_w_in=v_b_w_in, b_f_bias=v_b_f_bias, b_qn_w=v_b_qn_w,
             b_kn_w=v_b_kn_w, b_w_out=v_b_w_out, final_norm_w=v_final_norm_w)
    S, D = x.shape[1], x.shape[2]
    me = 4 * lax.axis_index("x") + 2 * lax.axis_index("y") + lax.axis_index("c")
    small_shapes = [W[n].shape for n in SMALL]

    shards = {n: W[n].astype(bf16) for n in ("a_w_in", "a_w_out", "b_w_in", "b_w_out")}
    gath = all_gather([shards["a_w_in"][0], shards["a_w_out"][0], a_conv_w.reshape(8, -1), c.reshape(8, D // 8)], "gather_w0")
    conv_full = _full_from_gathered(gath[2].reshape((NDEV,) + a_conv_w.shape), a_conv_w.shape, 2)
    w0 = (cols_from_blocks(gath[0][:, None], GDN_COLS, GDN_IN_PAD, "a_w_in_cols0")[0], gath[1].reshape(-1, D), conv_full[0])
    plan = MeshPlan(shards, w0, conv_full)
    c_all = gath[3].reshape(NDEV, D)

    mod_part = ada_fwd(c_all, ada_w, "ada_fwd")
    n_ada = ada_w.shape[2]
    mod_g = all_gather([mod_part.reshape(4 * NDEV, n_ada)], "gather_mod")[0].reshape(NDEV, 4, NDEV, n_ada)
    mod_mine = lax.dynamic_index_in_dim(mod_g, me, axis=2, keepdims=False)
    mod_all = jnp.moveaxis(mod_mine, 0, 1).reshape(4, NDEV * n_ada) + ada_b

    loss, dx, g = device_step(x[0], mod_all, norm_w, W, final_norm_w, loss_target[0], plan)
    loss = lax.psum(loss, MESH_AXES)

    g_small = dict(g, ada_b=g["dmod"])
    sp = _pack_small([g_small[n] for n in SMALL])
    sp_all = all_gather([sp], "gather_small")[0]
    sw, sm, sv = (_pack_small([T[n] for n in SMALL]) for T in (W, M, V))
    sg, sd, snm, snv = (_unpack(t, small_shapes, 128) for t in reduce_adam(sp_all, sw, sm, sv, sp.shape[0], "adam_small"))

    off_b = 0
    for n, shp in zip(SMALL, small_shapes):
        if n == "ada_b":
            break
        cnt = 1
        for d in shp:
            cnt *= d
        off_b += cnt + ((-cnt) % 128)
    dmod_all = sp_all.reshape(NDEV, -1)[:, off_b:off_b + 4 * 3 * D].reshape(NDEV, 4, 3 * D)
    dmod_cols = lax.dynamic_slice_in_dim(dmod_all, me * n_ada, n_ada, axis=2)
    g_ada = ada_grad(c_all, jnp.moveaxis(dmod_cols, 0, 1), "ada_grad")
    r_ada = reduce_adam(g_ada.reshape(1, 4 * D, n_ada), *(T["ada_w"].reshape(4 * D, n_ada) for T in (W, M, V)), 512, "adam_ada")
    ag, ad, anm, anv = (t.reshape(ada_w.shape) for t in r_ada)

    big = {}
    for n in BIG:
        C = W[n].shape[-1]
        parts = plan.recv[(n, None)] if n == "a_conv_w" else jnp.stack([plan.recv[(n, 0)], plan.recv[(n, 1)]], axis=1).reshape(NDEV, -1, C)
        res = reduce_adam(parts, *(T[n].reshape(parts.shape[1:]) for T in (W, M, V)), min(256, parts.shape[1]), f"adam_{n}")
        big[n] = [t.reshape(W[n].shape) for t in res]

    outs = {}
    for idx, (k, sm_l, ada_t) in enumerate((("grad", sg, ag), ("delta", sd, ad), ("new_m", snm, anm), ("new_v", snv, anv))):
        d = dict(zip(SMALL, sm_l))
        d.update({n: big[n][idx] for n in BIG})
        d["ada_w"] = ada_t
        outs[k] = d
    order = ("norm_w", "ada_w", "ada_b", "a_w_in", "a_conv_w", "a_A_log", "a_dt_bias", "a_norm_w", "a_w_out", "b_w_in", "b_f_bias",
             "b_qn_w", "b_kn_w", "b_w_out", "final_norm_w")
    return (loss, dx[None], *[outs["grad"][n] for n in order], *[outs["delta"][n] for n in order],
            *[outs["new_m"][n] for n in order], *[outs["new_v"][n] for n in order])
```

```python
import functools

import jax
import jax.numpy as jnp
from jax import lax
from jax.experimental import pallas as pl
from jax.experimental.pallas import tpu as pltpu

f32 = jnp.float32
bf16 = jnp.bfloat16
SDS = jax.ShapeDtypeStruct

EPS = 1e-6
CHUNK = 64
HD = 128
GDN_QK_HEADS = 8
GDN_V_HEADS = 16
GDN_QK_W = GDN_QK_HEADS * HD
GDN_V_W = GDN_V_HEADS * HD
GDN_CONV = 2 * GDN_QK_W + GDN_V_W
GDN_IN = GDN_CONV + GDN_V_W + 2 * GDN_V_HEADS
GDN_IN_PAD = GDN_CONV + GDN_V_W + 256
GDN_TN = 1280
FOX_H = 16
FOX_D = 64
FOX_W = FOX_H * FOX_D
FOX_IN = 4 * FOX_W + FOX_H
FOX_IN_PAD = 4 * FOX_W + 128
FOX_TN = 1408
FOX_PW = FOX_H * 128
NDEV = 8
MESH_AXES = ("x", "y", "c")
NEG = -1e30

ADAM_LR = 0.001
ADAM_B1 = 0.9
ADAM_B2 = 0.999
ADAM_EPS = 1e-08
ADAM_WD = 0.01
ADAM_STEP = 10

VMEM_LIMIT = 56 * 1024 * 1024


def _cp(sem=None):
    return pltpu.CompilerParams(dimension_semantics=sem, vmem_limit_bytes=VMEM_LIMIT)


def _bdot(a, b, dims):
    return lax.dot_general(a.astype(bf16), b.astype(bf16), (dims, ((), ())), preferred_element_type=f32)


def _nn(a, b):
    return _bdot(a, b, ((1,), (0,)))


def _nt(a, b):
    return _bdot(a, b, ((1,), (1,)))


def _tn(a, b):
    return _bdot(a, b, ((0,), (0,)))


def _hdot(a, b, dims=((1,), (0,))):
    return lax.dot_general(a, b, (dims, ((), ())), precision=lax.Precision.HIGHEST, preferred_element_type=f32)


def _split2(a):
    hi = a.astype(bf16)
    return hi, (a - hi.astype(f32)).astype(bf16)


def _dot3(a, b):
    (ah, al), (bh, bl) = a, b
    n = ah.shape[0]
    both = jnp.dot(jnp.concatenate([ah, al], axis=0), bh, preferred_element_type=f32)
    return both[:n] + both[n:] + jnp.dot(ah, bl, preferred_element_type=f32)


@jax.custom_vjp
def _mm(a, b):
    return _nn(a, b)


_mm.defvjp(lambda a, b: (_nn(a, b), (a, b)), lambda r, g: (_nt(g, r[1]), _tn(r[0], g)))


@jax.custom_vjp
def _mm_nt(a, b):
    return _nt(a, b)


_mm_nt.defvjp(lambda a, b: (_nt(a, b), (a, b)), lambda r, g: (_nn(g, r[1]), _tn(g, r[0])))


@jax.custom_vjp
def _mm_tn(a, b):
    return _tn(a, b)


_mm_tn.defvjp(lambda a, b: (_tn(a, b), (a, b)), lambda r, g: (_nt(r[1], g), _nn(r[0], g)))


def _silu(x):
    return x * jax.nn.sigmoid(x)


def _rms_mod(x, nw, scale, shift):
    r = lax.rsqrt(jnp.mean(x * x, axis=-1, keepdims=True) + EPS)
    return (x * r * nw) * (1.0 + scale) + shift


def _rows(S, want):
    return min(want, S)


def _my_pos():
    return lax.axis_index("x"), lax.axis_index("y"), lax.axis_index("c")


def _exchange_copies(kind, x_refs, out_refs, send_sems, recv_sems, local_sems):
    x_, y_, c_ = _my_pos()
    me = 4 * x_ + 2 * y_ + c_
    own = kind == "gather"
    cps = [pltpu.make_async_copy(x_refs[a] if own else x_refs[a].at[me], out_refs[a].at[me], local_sems.at[a])
           for a in range(len(x_refs))]
    for rel in range(1, NDEV):
        px = (x_ + ((rel >> 2) & 1)) % 2
        py = (y_ + ((rel >> 1) & 1)) % 2
        pc = (c_ + (rel & 1)) % 2
        for a in range(len(x_refs)):
            cps.append(pltpu.make_async_remote_copy(
                src_ref=x_refs[a] if own else x_refs[a].at[4 * px + 2 * py + pc], dst_ref=out_refs[a].at[me],
                send_sem=send_sems.at[rel - 1, a], recv_sem=recv_sems.at[rel - 1, a],
                device_id=(px, py, pc), device_id_type=pl.DeviceIdType.MESH))
    return cps


def _exchange_scratch(n):
    return [pltpu.SemaphoreType.DMA((NDEV - 1, n)), pltpu.SemaphoreType.DMA((NDEV - 1, n)), pltpu.SemaphoreType.DMA((n,))]


def _call(body, *, name, grid, in_specs, out_specs, out_shape, args, scratch=(), side=None):
    params = _cp(("arbitrary",) * len(grid))
    if side is None:
        return pl.pallas_call(body, name=name, grid=grid, in_specs=in_specs, out_specs=out_specs, out_shape=out_shape,
                              scratch_shapes=list(scratch), compiler_params=params)(*args)
    kind, xs = side
    n_in, n_out, n_scr, ns = len(in_specs), len(out_shape), len(scratch), len(xs)
    steps = 1
    for g in grid:
        steps *= g

    def wrapped(*refs):
        o0 = n_in + ns
        s0 = o0 + n_out + ns
        step = pl.program_id(0)
        for d in range(1, len(grid)):
            step = step * grid[d] + pl.program_id(d)

        def copies():
            return _exchange_copies(kind, refs[n_in:o0], refs[o0 + n_out:s0], *refs[s0 + n_scr:])

        @pl.when(step == 0)
        def _():
            for cp in copies():
                cp.start()

        body(*refs[:n_in], *refs[o0:o0 + n_out], *refs[s0:s0 + n_scr])

        @pl.when(step == steps - 1)
        def _():
            for cp in copies():
                cp.wait()

    any_ = pl.BlockSpec(memory_space=pl.ANY)
    side_shapes = [SDS((NDEV,) + x.shape if kind == "gather" else x.shape, x.dtype) for x in xs]
    outs = pl.pallas_call(wrapped, name=name, grid=grid, in_specs=list(in_specs) + [any_] * ns,
                          out_specs=list(out_specs) + [any_] * ns, out_shape=list(out_shape) + side_shapes,
                          scratch_shapes=list(scratch) + _exchange_scratch(ns), compiler_params=params)(*args, *xs)
    return outs[:n_out], outs[n_out:]


def inproj_fwd(x, nw, scale, shift, W, tn, name, side=None):
    S, D = x.shape
    N = W.shape[1]
    tm = _rows(S, 1024)

    def body(x_ref, nw_ref, sc_ref, sh_ref, w_ref, proj_ref, h_ref):
        @pl.when(pl.program_id(1) == 0)
        def _():
            h_ref[...] = _rms_mod(x_ref[...], nw_ref[...], sc_ref[...], sh_ref[...]).astype(bf16)

        proj_ref[...] = jnp.dot(h_ref[...], w_ref[...], preferred_element_type=f32)

    vec = pl.BlockSpec((1, D), lambda i, j: (0, 0))
    return _call(
        body, name=name, grid=(S // tm, N // tn),
        in_specs=[pl.BlockSpec((tm, D), lambda i, j: (i, 0)), vec, vec, vec, pl.BlockSpec((D, tn), lambda i, j: (0, j))],
        out_specs=[pl.BlockSpec((tm, tn), lambda i, j: (i, j)), pl.BlockSpec((tm, D), lambda i, j: (i, 0))],
        out_shape=[SDS((S, N), f32), SDS((S, D), bf16)], args=(x, nw, scale, shift, W), side=side)


def inproj_bwd_x(x, nw, scale, shift, W, dproj, dx_res, tn, name, side=None):
    S, D = x.shape
    N = W.shape[1]
    tm = _rows(S, 1024)
    nj = N // tn

    def body(x_ref, nw_ref, sc_ref, sh_ref, w_ref, dp_ref, dxr_ref, dx_ref, dnw_ref, dsc_ref, dsh_ref, acc):
        i, j = pl.program_id(0), pl.program_id(1)

        @pl.when(j == 0)
        def _():
            acc[...] = jnp.zeros_like(acc)

        @pl.when((i == 0) & (j == 0))
        def _():
            dnw_ref[...] = jnp.zeros_like(dnw_ref)
            dsc_ref[...] = jnp.zeros_like(dsc_ref)
            dsh_ref[...] = jnp.zeros_like(dsh_ref)

        acc[...] += _nt(dp_ref[...], w_ref[...])

        @pl.when(j == nj - 1)
        def _():
            _, vjp = jax.vjp(_rms_mod, x_ref[...], nw_ref[...], sc_ref[...], sh_ref[...])
            dx, dnw, dsc, dsh = vjp(acc[...])
            dx_ref[...] = dxr_ref[...] + dx
            dnw_ref[...] += dnw
            dsc_ref[...] += dsc
            dsh_ref[...] += dsh

    vec = pl.BlockSpec((1, D), lambda i, j: (0, 0))
    row = pl.BlockSpec((tm, D), lambda i, j: (i, 0))
    return _call(
        body, name=name, grid=(S // tm, nj),
        in_specs=[row, vec, vec, vec, pl.BlockSpec((D, tn), lambda i, j: (0, j)), pl.BlockSpec((tm, tn), lambda i, j: (i, j)), row],
        out_specs=[row, vec, vec, vec],
        out_shape=[SDS((S, D), f32), SDS((1, D), f32), SDS((1, D), f32), SDS((1, D), f32)],
        scratch=[pltpu.VMEM((tm, D), f32)], args=(x, nw, scale, shift, W, dproj, dx_res), side=side)


def matmul_tn(a, b, tn, name, side=None):
    S, K = a.shape
    N = b.shape[1]
    tm = _rows(S, 1024)
    ni = S // tm

    def body(a_ref, b_ref, o_ref):
        @pl.when(pl.program_id(1) == 0)
        def _():
            o_ref[...] = jnp.zeros_like(o_ref)

        o_ref[...] += _tn(a_ref[...], b_ref[...])

    return _call(
        body, name=name, grid=(N // tn, ni),
        in_specs=[pl.BlockSpec((tm, K), lambda j, i: (i, 0)), pl.BlockSpec((tm, tn), lambda j, i: (i, j))],
        out_specs=[pl.BlockSpec((K, tn), lambda j, i: (0, j))],
        out_shape=[SDS((K, N), f32)], args=(a, b), side=side)


def _conv_taps(xs, w, n_out):
    taps = []
    for j in range(4):
        s = 3 - j
        sh = xs if s == 0 else pltpu.roll(xs, s, axis=0)
        taps.append(sh[8:8 + n_out])
    conv = taps[0] * w[0] + taps[1] * w[1] + taps[2] * w[2] + taps[3] * w[3]
    return taps, conv


def _act_norm(conv, mul):
    s = _silu(conv)
    return s * (mul * lax.rsqrt(jnp.sum(s * s, axis=-1, keepdims=True) + EPS))


def gdn_prep_fwd(proj, conv_w, name):
    S = proj.shape[0]
    R = _rows(S, 512)

    def body(x_ref, w_ref, o_ref):
        j = pl.program_id(0)
        w = [w_ref[t:t + 1, :] for t in range(4)]

        def sweep(act):
            def piece(r, c):
                t0 = pl.multiple_of(r * R, R)
                cur = x_ref[pl.ds(t0, R), :]
                prev = x_ref[pl.ds(pl.multiple_of(jnp.maximum(t0 - 8, 0), 8), 8), :]
                prev = jnp.where(r == 0, 0.0, prev)
                _, conv = _conv_taps(jnp.concatenate([prev, cur], axis=0), w, R)
                o_ref[pl.ds(t0, R), :] = act(conv)
                return c

            lax.fori_loop(0, S // R, piece, 0)

        @pl.when(j < 2 * GDN_QK_HEADS)
        def _():
            sweep(lambda c: _act_norm(c, jnp.where(j < GDN_QK_HEADS, HD ** -0.5, 1.0).astype(f32)))

        @pl.when(j >= 2 * GDN_QK_HEADS)
        def _():
            sweep(_silu)

    return pl.pallas_call(
        body, name=name, grid=(GDN_CONV // 128,),
        in_specs=[pl.BlockSpec((S, 128), lambda j: (0, j)), pl.BlockSpec((4, 128), lambda j: (0, j))],
        out_specs=pl.BlockSpec((S, 128), lambda j: (0, j)),
        out_shape=SDS((S, GDN_CONV), f32),
        compiler_params=_cp(("arbitrary",)),
    )(proj, conv_w)


def gdn_prep_bwd(proj, conv_w, dqkvc, dproj, name):
    S = proj.shape[0]
    R = _rows(S, 512)
    NP = S // R

    def body(x_ref, w_ref, dn_ref, _, dx_ref, dw_ref):
        jb = pl.program_id(0)
        w = [w_ref[j:j + 1, :] for j in range(4)]

        def piece(act, r, dw):
            t0 = pl.multiple_of(r * R, R)
            cur = x_ref[pl.ds(t0, R), :]
            prev = x_ref[pl.ds(pl.multiple_of(jnp.maximum(t0 - 8, 0), 8), 8), :]
            prev = jnp.where(r == 0, 0.0, prev)
            nxt0 = pl.multiple_of(jnp.minimum(t0 + R, S - 8), 8)
            nxt = x_ref[pl.ds(nxt0, 8), :]
            dn_cur = dn_ref[pl.ds(t0, R), :]
            dn_nxt = jnp.where(r == NP - 1, 0.0, dn_ref[pl.ds(nxt0, 8), :])
            xs = jnp.concatenate([prev, cur, nxt], axis=0)
            taps, conv = _conv_taps(xs, w, R + 8)
            dn = jnp.concatenate([dn_cur, dn_nxt], axis=0)
            _, vjp = jax.vjp(act, conv)
            dxc = vjp(dn)[0]
            n = R + 8
            dx = dxc[0:R] * w[3]
            for j in range(3):
                s = 3 - j
                dx = dx + pltpu.roll(dxc, n - s, axis=0)[0:R] * w[j]
            dx_ref[pl.ds(t0, R), :] = dx
            return tuple(dw[j] + jnp.sum(dxc[0:R] * taps[j][0:R], axis=0, keepdims=True) for j in range(4))

        def sweep(act):
            dw = lax.fori_loop(0, NP, functools.partial(piece, act), tuple(jnp.zeros((1, 128), f32) for _ in range(4)))
            for j in range(4):
                dw_ref[j:j + 1, :] = dw[j]

        @pl.when(jb < 2 * GDN_QK_HEADS)
        def _():
            sweep(lambda c: _act_norm(c, jnp.where(jb < GDN_QK_HEADS, HD ** -0.5, 1.0).astype(f32)))

        @pl.when(jb >= 2 * GDN_QK_HEADS)
        def _():
            sweep(_silu)

    col = pl.BlockSpec((S, 128), lambda j: (0, j))
    wsp = pl.BlockSpec((4, 128), lambda j: (0, j))
    return pl.pallas_call(
        body, name=name, grid=(GDN_CONV // 128,),
        in_specs=[col, wsp, col, pl.BlockSpec(memory_space=pl.ANY)], out_specs=[col, wsp],
        out_shape=[SDS(dproj.shape, f32), SDS((4, GDN_CONV), f32)],
        input_output_aliases={3: 0},
        compiler_params=_cp(("arbitrary",)),
    )(proj, conv_w, dqkvc, dproj)


def _chunk_tril(R):
    ii = lax.broadcasted_iota(jnp.int32, (R, R), 0)
    jj = lax.broadcasted_iota(jnp.int32, (R, R), 1)
    return ((ii // CHUNK == jj // CHUNK) & (ii >= jj)).astype(f32)


def _gdn_gates(b, a, A_log, dt_bias, tril):
    beta = jax.nn.sigmoid(b)
    g = -jnp.exp(A_log) * jax.nn.softplus(a + dt_bias)
    return _hdot(tril, g), beta


_GDN_B_BLK = (GDN_CONV + GDN_V_W) // 128
_GDN_A_BLK = _GDN_B_BLK + 1


def gdn_gates_fwd(proj, A_log, dt_bias, name):
    S = proj.shape[0]
    R = _rows(S, 512)

    def body(b_ref, a_ref, al_ref, dt_ref, gc_ref, be_ref):
        gc, be = _gdn_gates(b_ref[...], a_ref[...], al_ref[...], dt_ref[...], _chunk_tril(R))
        gc_ref[...] = gc
        be_ref[...] = be

    vec = pl.BlockSpec((1, 128), lambda i: (0, 0))
    blk = pl.BlockSpec((R, 128), lambda i: (i, 0))
    return pl.pallas_call(
        body, name=name, grid=(S // R,),
        in_specs=[pl.BlockSpec((R, 128), lambda i: (i, _GDN_B_BLK)), pl.BlockSpec((R, 128), lambda i: (i, _GDN_A_BLK)), vec, vec],
        out_specs=[blk, blk], out_shape=[SDS((S, 128), f32), SDS((S, 128), f32)],
        compiler_params=_cp(("arbitrary",)),
    )(proj, proj, A_log, dt_bias)


def gdn_gates_bwd(proj, A_log, dt_bias, dgc, dbeta, dproj, name):
    S = proj.shape[0]
    R = _rows(S, 512)

    def body(b_ref, a_ref, al_ref, dt_ref, dgc_ref, dbe_ref, _, dp_ref, dal_ref, ddt_ref):
        @pl.when(pl.program_id(0) == 0)
        def _():
            dal_ref[...] = jnp.zeros_like(dal_ref)
            ddt_ref[...] = jnp.zeros_like(ddt_ref)

        tril = _chunk_tril(R)
        _, vjp = jax.vjp(lambda b, a, al, dt: _gdn_gates(b, a, al, dt, tril), b_ref[...], a_ref[...], al_ref[...], dt_ref[...])
        db, da, dal, ddt = vjp((dgc_ref[...], dbe_ref[...]))
        dp_ref[:, 0:128] = db
        dp_ref[:, 128:256] = da
        dal_ref[...] += dal
        ddt_ref[...] += ddt

    vec = pl.BlockSpec((1, 128), lambda i: (0, 0))
    blk = pl.BlockSpec((R, 128), lambda i: (i, 0))
    return pl.pallas_call(
        body, name=name, grid=(S // R,),
        in_specs=[pl.BlockSpec((R, 128), lambda i: (i, _GDN_B_BLK)), pl.BlockSpec((R, 128), lambda i: (i, _GDN_A_BLK)), vec, vec, blk, blk,
                  pl.BlockSpec(memory_space=pl.ANY)],
        out_specs=[pl.BlockSpec((R, 256), lambda i: (i, _GDN_B_BLK // 2)), vec, vec],
        out_shape=[SDS(dproj.shape, f32), SDS((1, 128), f32), SDS((1, 128), f32)],
        input_output_aliases={6: 0},
        compiler_params=_cp(("arbitrary",)),
    )(proj, proj, A_log, dt_bias, dgc, dbeta, dproj)


@jax.custom_vjp
def _inv_given(L, T):
    return T


def _inv_given_bwd(T, ct):
    dL = -_nt(_tn(T, ct), T)
    return dL, jnp.zeros_like(T)


_inv_given.defvjp(lambda L, T: (T, T), _inv_given_bwd)


REP = GDN_V_HEADS // GDN_QK_HEADS


def _gdn_intra_all(qs, ks, vs, gcols, bcols, Ts=None):
    H = len(vs)
    C = vs[0].shape[0]
    ii = lax.broadcasted_iota(jnp.int32, (C, C), 0)
    jj = lax.broadcasted_iota(jnp.int32, (C, C), 1)
    grows = [jnp.sum(jnp.where(ii == jj, g, 0.0), axis=0, keepdims=True) for g in gcols]
    decs = [jnp.exp(jnp.where(ii >= jj, gcols[h] - grows[h], NEG)) for h in range(H)]
    kbs = [ks[h // REP] * bcols[h] for h in range(H)]
    As = [_mm_nt(kbs[h], ks[h // REP]) for h in range(H)]
    Ls = [jnp.where(ii > jj, As[h] * decs[h], 0.0) for h in range(H)]
    if Ts is None:
        T = _neumann_inv_batched(Ls)
    else:
        T = [_inv_given(Ls[h], Ts[h]) for h in range(H)]
    us = [_mm(T[h], vs[h] * bcols[h]) for h in range(H)]
    ws = [_mm(T[h], kbs[h] * jnp.exp(gcols[h])) for h in range(H)]
    qk = [_mm_nt(qs[p], ks[p]) for p in range(H // REP)]
    return us, ws, [qk[h // REP] * decs[h] for h in range(H)], T


def _neumann_inv_batched(Ls):
    n, C = 4, Ls[0].shape[0]
    r0 = lax.broadcasted_iota(jnp.int32, (n * C, n * C), 0)
    c0 = lax.broadcasted_iota(jnp.int32, (n * C, n * C), 1)
    same = (r0 // C) == (c0 // C)

    def blockdiag(split):
        return tuple(jnp.where(same, jnp.concatenate([x] * n, axis=0), jnp.zeros((), bf16)) for x in split)

    Ms = [jnp.concatenate(Ls[b:b + n], axis=1) for b in range(0, len(Ls), n)]
    eye = (lax.broadcasted_iota(jnp.int32, (C, n * C), 0) == (lax.broadcasted_iota(jnp.int32, (C, n * C), 1) & (C - 1))).astype(f32)
    Ps = [eye - M for M in Ms]
    Ss = [_split2(M) for M in Ms]
    Bs = [blockdiag(S) for S in Ss]
    k = 1
    while 2 * k < C:
        Ss = [_split2(_dot3(S, B)) for S, B in zip(Ss, Bs)]
        Bs = [blockdiag(S) for S in Ss]
        Ps = [P + _dot3(_split2(P), B) for P, B in zip(Ps, Bs)]
        k *= 2
    return [P[:, h * C:(h + 1) * C] for P in Ps for h in range(n)]


def _gdn_scan_all(qs, ks, gcols, us, ws, attns, S0s):
    H = len(us)
    C = us[0].shape[0]
    last = lax.broadcasted_iota(jnp.int32, (C, 1), 0) == C - 1
    glast = [jnp.sum(jnp.where(last, g, 0.0), axis=0, keepdims=True) for g in gcols]
    wS = [_mm(ws[h], S0s[h]) for h in range(H)]
    qS = [_mm(qs[h // REP] * jnp.exp(gcols[h]), S0s[h]) for h in range(H)]
    vn = [us[h] - wS[h] for h in range(H)]
    av = [_mm(attns[h], vn[h]) for h in range(H)]
    kv = [_mm_tn(ks[h // REP] * jnp.exp(glast[h] - gcols[h]), vn[h]) for h in range(H)]
    return [qS[h] + av[h] for h in range(H)], [S0s[h] * jnp.exp(glast[h]) + kv[h] for h in range(H)]


def _head_cols(blk):
    lane = lax.broadcasted_iota(jnp.int32, blk.shape, 1)
    return [jnp.sum(jnp.where(lane == h, blk, 0.0), axis=1, keepdims=True) for h in range(GDN_V_HEADS)]


def _head_lanes(cols):
    lane = lax.broadcasted_iota(jnp.int32, (cols[0].shape[0], 128), 1)
    out = jnp.zeros((cols[0].shape[0], 128), f32)
    for h, c in enumerate(cols):
        out = out + jnp.where(lane == h, c, 0.0)
    return out


def _heads(ref, n):
    return [ref[:, h * HD:(h + 1) * HD].astype(f32) for h in range(n)]


def _mats(ref):
    return [ref[0, h].astype(f32) for h in range(GDN_V_HEADS)]


def _gdn_specs(NC, rv=None):
    ix = (lambda n: n) if rv is None else rv
    qs = pl.BlockSpec((CHUNK, GDN_QK_W), lambda n: (ix(n), 0))
    ks = pl.BlockSpec((CHUNK, GDN_QK_W), lambda n: (ix(n), 1))
    vs = pl.BlockSpec((CHUNK, GDN_V_W), lambda n: (ix(n), 1))
    g1 = pl.BlockSpec((CHUNK, 128), lambda n: (ix(n), 0))
    wide = pl.BlockSpec((CHUNK, GDN_V_W), lambda n: (ix(n), 0))
    sq = pl.BlockSpec((1, GDN_V_HEADS, CHUNK, CHUNK), lambda n: (ix(n), 0, 0, 0))
    st = pl.BlockSpec((1, GDN_V_HEADS, HD, HD), lambda n: (ix(n), 0, 0, 0))
    return qs, ks, vs, g1, wide, sq, st


def gdn_intra_fwd(qkvc, gc, beta, name, side=None):
    S = qkvc.shape[0]
    NC = S // CHUNK

    def body(q_ref, k_ref, v_ref, gc_ref, be_ref, u_ref, w_ref, at_ref, T_ref):
        us, ws, attns, Ts = _gdn_intra_all(_heads(q_ref, GDN_QK_HEADS), _heads(k_ref, GDN_QK_HEADS), _heads(v_ref, GDN_V_HEADS),
                                           _head_cols(gc_ref[...]), _head_cols(be_ref[...]))
        for h in range(GDN_V_HEADS):
            u_ref[:, h * HD:(h + 1) * HD] = us[h]
            w_ref[:, h * HD:(h + 1) * HD] = ws[h].astype(bf16)
            at_ref[0, h] = attns[h].astype(bf16)
            T_ref[0, h] = Ts[h].astype(bf16)

    qs, ks, vs, g1, wide, sq, _ = _gdn_specs(NC)
    return _call(
        body, name=name, grid=(NC,),
        in_specs=[qs, ks, vs, g1, g1], out_specs=[wide, wide, sq, sq],
        out_shape=[SDS((S, GDN_V_W), f32), SDS((S, GDN_V_W), bf16),
                   SDS((NC, GDN_V_HEADS, CHUNK, CHUNK), bf16), SDS((NC, GDN_V_HEADS, CHUNK, CHUNK), bf16)],
        args=(qkvc, qkvc, qkvc, gc, beta), side=side)


def gdn_scan_fwd(qkvc, gc, u, w, attn, name, side=None):
    S = qkvc.shape[0]
    NC = S // CHUNK

    def body(q_ref, k_ref, gc_ref, u_ref, w_ref, at_ref, o_ref, st_ref, state):
        @pl.when(pl.program_id(0) == 0)
        def _():
            state[...] = jnp.zeros_like(state)

        S0s = [state[h] for h in range(GDN_V_HEADS)]
        os_, S1s = _gdn_scan_all(_heads(q_ref, GDN_QK_HEADS), _heads(k_ref, GDN_QK_HEADS), _head_cols(gc_ref[...]),
                                 _heads(u_ref, GDN_V_HEADS), _heads(w_ref, GDN_V_HEADS), _mats(at_ref), S0s)
        for h in range(GDN_V_HEADS):
            o_ref[:, h * HD:(h + 1) * HD] = os_[h]
            st_ref[0, h] = S0s[h].astype(bf16)
            state[h] = S1s[h]

    qs, ks, _, g1, wide, sq, st = _gdn_specs(NC)
    return _call(
        body, name=name, grid=(NC,),
        in_specs=[qs, ks, g1, wide, wide, sq], out_specs=[wide, st],
        out_shape=[SDS((S, GDN_V_W), f32), SDS((NC, GDN_V_HEADS, HD, HD), bf16)],
        scratch=[pltpu.VMEM((GDN_V_HEADS, HD, HD), f32)], args=(qkvc, qkvc, gc, u, w, attn), side=side)


def gdn_scan_bwd(qkvc, gc, u, w, attn, states, do, name, side=None):
    S = qkvc.shape[0]
    NC = S // CHUNK

    def body(q_ref, k_ref, gc_ref, u_ref, w_ref, at_ref, st_ref, do_ref,
             dq_ref, dk_ref, dgc_ref, du_ref, dw_ref, dat_ref, dstate):
        @pl.when(pl.program_id(0) == 0)
        def _():
            dstate[...] = jnp.zeros_like(dstate)

        VH = range(GDN_V_HEADS)
        _, vjp = jax.vjp(_gdn_scan_all, _heads(q_ref, GDN_QK_HEADS), _heads(k_ref, GDN_QK_HEADS), _head_cols(gc_ref[...]),
                         _heads(u_ref, GDN_V_HEADS), _heads(w_ref, GDN_V_HEADS), _mats(at_ref), _mats(st_ref))
        dqs, dks, dgs, dus, dws, dats, dS0s = vjp((_heads(do_ref, GDN_V_HEADS), [dstate[h] for h in VH]))
        for p in range(GDN_QK_HEADS):
            dq_ref[:, p * HD:(p + 1) * HD] = dqs[p]
            dk_ref[:, p * HD:(p + 1) * HD] = dks[p]
        for h in VH:
            du_ref[:, h * HD:(h + 1) * HD] = dus[h].astype(bf16)
            dw_ref[:, h * HD:(h + 1) * HD] = dws[h].astype(bf16)
            dat_ref[0, h] = dats[h].astype(bf16)
            dstate[h] = dS0s[h]
        dgc_ref[...] = _head_lanes(dgs)

    qs, ks, _, g1, wide, sq, st = _gdn_specs(NC, lambda n: NC - 1 - n)
    dqs = pl.BlockSpec((CHUNK, GDN_QK_W), lambda n: (NC - 1 - n, 0))
    return _call(
        body, name=name, grid=(NC,),
        in_specs=[qs, ks, g1, wide, wide, sq, st, wide],
        out_specs=[dqs, dqs, g1, wide, wide, sq],
        out_shape=[SDS((S, GDN_QK_W), f32), SDS((S, GDN_QK_W), f32), SDS((S, 128), f32), SDS((S, GDN_V_W), bf16),
                   SDS((S, GDN_V_W), bf16), SDS((NC, GDN_V_HEADS, CHUNK, CHUNK), bf16)],
        scratch=[pltpu.VMEM((GDN_V_HEADS, HD, HD), f32)], args=(qkvc, qkvc, gc, u, w, attn, states, do), side=side)


def gdn_intra_bwd(qkvc, gc, beta, Ts, du, dw, dattn, dq_s, dk_s, dgc_s, name, side=None):
    S = qkvc.shape[0]
    NC = S // CHUNK

    def body(q_ref, k_ref, v_ref, gc_ref, be_ref, T_ref, du_ref, dw_ref, dat_ref, dqs_ref, dks_ref, dgs_ref,
             dqkv_ref, dgc_ref, dbe_ref):
        VH = range(GDN_V_HEADS)
        Ts = _mats(T_ref)
        _, vjp = jax.vjp(lambda q_, k_, v_, g_, b_: _gdn_intra_all(q_, k_, v_, g_, b_, Ts)[:3],
                         _heads(q_ref, GDN_QK_HEADS), _heads(k_ref, GDN_QK_HEADS), _heads(v_ref, GDN_V_HEADS),
                         _head_cols(gc_ref[...]), _head_cols(be_ref[...]))
        dqs, dks, dvs, dgs, dbs = vjp((_heads(du_ref, GDN_V_HEADS), _heads(dw_ref, GDN_V_HEADS), _mats(dat_ref)))
        for p in range(GDN_QK_HEADS):
            dqkv_ref[:, p * HD:(p + 1) * HD] = dqs[p] + dqs_ref[:, p * HD:(p + 1) * HD]
            dqkv_ref[:, GDN_QK_W + p * HD:GDN_QK_W + (p + 1) * HD] = dks[p] + dks_ref[:, p * HD:(p + 1) * HD]
        for h in VH:
            dqkv_ref[:, 2 * GDN_QK_W + h * HD:2 * GDN_QK_W + (h + 1) * HD] = dvs[h]
        dgc_ref[...] = _head_lanes(dgs) + dgs_ref[...]
        dbe_ref[...] = _head_lanes(dbs)

    qs, ks, vs, g1, wide, sq, _ = _gdn_specs(NC)
    dqs = pl.BlockSpec((CHUNK, GDN_QK_W), lambda n: (n, 0))
    return _call(
        body, name=name, grid=(NC,),
        in_specs=[qs, ks, vs, g1, g1, sq, wide, wide, sq, dqs, dqs, g1],
        out_specs=[pl.BlockSpec((CHUNK, GDN_CONV), lambda n: (n, 0)), g1, g1],
        out_shape=[SDS((S, GDN_CONV), f32), SDS((S, 128), f32), SDS((S, 128), f32)],
        args=(qkvc, qkvc, qkvc, gc, beta, Ts, du, dw, dattn, dq_s, dk_s, dgc_s), side=side)


def _gated_norm(o, z, nw):
    parts = []
    for h in range(GDN_V_HEADS):
        oh = o[:, h * HD:(h + 1) * HD]
        r = lax.rsqrt(jnp.mean(oh * oh, axis=-1, keepdims=True) + EPS)
        parts.append((oh * r * nw) * _silu(z[:, h * HD:(h + 1) * HD]))
    return jnp.concatenate(parts, axis=1)


def gdn_out_fwd(o, proj, nw, W, x, gate, name):
    S, D = x.shape
    tm = _rows(S, 256)

    def body(o_ref, z_ref, nw_ref, w_ref, x_ref, g_ref, xn_ref, y_ref, og_ref):
        og = _gated_norm(o_ref[...], z_ref[...], nw_ref[...]).astype(bf16)
        y = jnp.dot(og, w_ref[...], preferred_element_type=f32)
        og_ref[...] = og
        y_ref[...] = y
        xn_ref[...] = x_ref[...] + g_ref[...] * y

    row = pl.BlockSpec((tm, D), lambda i: (i, 0))
    wide = pl.BlockSpec((tm, GDN_V_W), lambda i: (i, 0))
    return pl.pallas_call(
        body, name=name, grid=(S // tm,),
        in_specs=[wide, pl.BlockSpec((tm, GDN_V_W), lambda i: (i, 2)), pl.BlockSpec((1, HD), lambda i: (0, 0)),
                  pl.BlockSpec((GDN_V_W, D), lambda i: (0, 0)), row, pl.BlockSpec((1, D), lambda i: (0, 0))],
        out_specs=[row, row, wide],
        out_shape=[SDS((S, D), f32), SDS((S, D), f32), SDS((S, GDN_V_W), bf16)],
        compiler_params=_cp(("arbitrary",)),
    )(o, proj, nw, W, x, gate)


def gdn_out_bwd(dxn, y, gate, o, proj, nw, W, name):
    S, D = dxn.shape
    tm = _rows(S, 256)

    def body(dx_ref, y_ref, g_ref, o_ref, z_ref, nw_ref, w_ref, dy_ref, dg_ref, do_ref, dz_ref, dnw_ref):
        @pl.when(pl.program_id(0) == 0)
        def _():
            dg_ref[...] = jnp.zeros_like(dg_ref)
            dnw_ref[...] = jnp.zeros_like(dnw_ref)

        dx = dx_ref[...]
        dy = dx * g_ref[...]
        dy_ref[...] = dy
        dg_ref[...] += jnp.sum(dx * y_ref[...], axis=0, keepdims=True)
        dog = _nt(dy, w_ref[...])
        _, vjp = jax.vjp(_gated_norm, o_ref[...], z_ref[...], nw_ref[...])
        do, dz, dnw = vjp(dog)
        do_ref[...] = do
        dz_ref[...] = dz
        dnw_ref[...] += dnw

    row = pl.BlockSpec((tm, D), lambda i: (i, 0))
    wide = pl.BlockSpec((tm, GDN_V_W), lambda i: (i, 0))
    vecd = pl.BlockSpec((1, D), lambda i: (0, 0))
    vech = pl.BlockSpec((1, HD), lambda i: (0, 0))
    return pl.pallas_call(
        body, name=name, grid=(S // tm,),
        in_specs=[row, row, vecd, wide, pl.BlockSpec((tm, GDN_V_W), lambda i: (i, 2)), vech, pl.BlockSpec((GDN_V_W, D), lambda i: (0, 0))],
        out_specs=[row, vecd, wide, pl.BlockSpec((tm, GDN_V_W), lambda i: (i, 2)), vech],
        out_shape=[SDS((S, D), f32), SDS((1, D), f32), SDS((S, GDN_V_W), f32), SDS((S, GDN_IN_PAD), f32), SDS((1, HD), f32)],
        compiler_params=_cp(("arbitrary",)),
    )(dxn, y, gate, o, proj, nw, W)


def _rms_w(x, w):
    return (x * lax.rsqrt(jnp.mean(x * x, axis=-1, keepdims=True) + EPS)) * w


def _split3(c):
    hi = c.astype(bf16).astype(f32)
    r1 = c - hi
    mid = r1.astype(bf16).astype(f32)
    lo = (r1 - mid).astype(bf16).astype(f32)
    return hi, mid, lo


_FOX_F_BLK = 4 * FOX_W // 128


def fox_prep_fwd(proj, f_bias, qn_w, kn_w, name):
    S = proj.shape[0]
    tm = _rows(S, 256)

    def body(q_ref, k_ref, v_ref, f_ref, fb_ref, qw_ref, kw_ref, Q_ref, K_ref, V_ref, carry):
        @pl.when(pl.program_id(0) == 0)
        def _():
            carry[...] = jnp.zeros_like(carry)

        ii = lax.broadcasted_iota(jnp.int32, (tm, tm), 0)
        jj = lax.broadcasted_iota(jnp.int32, (tm, tm), 1)
        lf = jax.nn.log_sigmoid(f_ref[...] + fb_ref[...])
        cum = _hdot((ii >= jj).astype(f32), lf) + carry[...]
        carry[...] = cum[tm - 1:tm, :]
        lane = lax.broadcasted_iota(jnp.int32, (tm, 128), 1)
        lo = lane < FOX_D
        qw2 = jnp.concatenate([qw_ref[...], qw_ref[...]], axis=1) * FOX_D ** -0.5
        kw2 = jnp.concatenate([kw_ref[...], kw_ref[...]], axis=1)

        def norm_pair(x, w2):
            x2 = x * x
            s_all = jnp.sum(x2, axis=1, keepdims=True)
            s_lo = jnp.sum(jnp.where(lo, x2, 0.0), axis=1, keepdims=True)
            r = jnp.where(lo, lax.rsqrt(s_lo * (1.0 / FOX_D) + EPS), lax.rsqrt((s_all - s_lo) * (1.0 / FOX_D) + EPS))
            return x * r * w2

        for p in range(FOX_H // 2):
            ps = slice(p * 128, (p + 1) * 128)
            yq, yk, xv = norm_pair(q_ref[:, ps], qw2), norm_pair(k_ref[:, ps], kw2), v_ref[:, ps]
            for e in range(2):
                h = 2 * p + e
                hi, mid, lw = _split3(cum[:, h:h + 1])
                eq = jnp.where(lane == FOX_D, hi, jnp.where(lane == FOX_D + 1, mid, jnp.where(lane == FOX_D + 2, lw, jnp.where(lane < FOX_D + 6, 1.0, 0.0))))
                ek = jnp.where(lane < FOX_D + 3, 1.0, jnp.where(lane == FOX_D + 3, -hi, jnp.where(lane == FOX_D + 4, -mid, jnp.where(lane == FOX_D + 5, -lw, 0.0))))
                ev = jnp.where(lane == FOX_D, 1.0, 0.0)
                mv = (lambda a: a) if e == 0 else (lambda a: pltpu.roll(a, FOX_D, axis=1))
                Q_ref[:, h * 128:(h + 1) * 128] = jnp.where(lo, mv(yq), eq).astype(bf16)
                K_ref[:, h * 128:(h + 1) * 128] = jnp.where(lo, mv(yk), ek).astype(bf16)
                V_ref[:, h * 128:(h + 1) * 128] = jnp.where(lo, mv(xv), ev).astype(bf16)

    def colblk(c):
        return pl.BlockSpec((tm, FOX_W), lambda i: (i, c))

    pad = pl.BlockSpec((tm, FOX_PW), lambda i: (i, 0))
    return pl.pallas_call(
        body, name=name, grid=(S // tm,),
        in_specs=[colblk(0), colblk(1), colblk(2), pl.BlockSpec((tm, 128), lambda i: (i, _FOX_F_BLK)),
                  pl.BlockSpec((1, 128), lambda i: (0, 0)), pl.BlockSpec((1, FOX_D), lambda i: (0, 0)), pl.BlockSpec((1, FOX_D), lambda i: (0, 0))],
        out_specs=[pad, pad, pad],
        out_shape=[SDS((S, FOX_PW), bf16)] * 3,
        scratch_shapes=[pltpu.VMEM((1, 128), f32)],
        compiler_params=_cp(("arbitrary",)),
    )(proj, proj, proj, proj, f_bias, qn_w, kn_w)


def fox_prep_bwd(proj, f_bias, qn_w, kn_w, dQ, dK, dV, dz, name):
    S = proj.shape[0]
    tm = _rows(S, 256)
    NB = S // tm

    def body(q_ref, k_ref, f_ref, fb_ref, qw_ref, kw_ref, dQ_ref, dK_ref, dV_ref, dz_ref,
             dp_ref, dfb_ref, dqw_ref, dkw_ref, carry):
        @pl.when(pl.program_id(0) == 0)
        def _():
            carry[...] = jnp.zeros_like(carry)
            dfb_ref[...] = jnp.zeros_like(dfb_ref)
            dqw_ref[...] = jnp.zeros_like(dqw_ref)
            dkw_ref[...] = jnp.zeros_like(dkw_ref)

        lane = lax.broadcasted_iota(jnp.int32, (tm, 128), 1)
        lo = lane < FOX_D
        qw2 = jnp.concatenate([qw_ref[...], qw_ref[...]], axis=1) * FOX_D ** -0.5
        kw2 = jnp.concatenate([kw_ref[...], kw_ref[...]], axis=1)

        def pair(ref, p):
            return jnp.where(lo, ref[:, 2 * p * 128:(2 * p + 1) * 128], pltpu.roll(ref[:, (2 * p + 1) * 128:(2 * p + 2) * 128], FOX_D, axis=1))

        def norm_pair_bwd(x, w2, dy):
            x2 = x * x
            s_all = jnp.sum(x2, axis=1, keepdims=True)
            s_lo = jnp.sum(jnp.where(lo, x2, 0.0), axis=1, keepdims=True)
            r = jnp.where(lo, lax.rsqrt(s_lo * (1.0 / FOX_D) + EPS), lax.rsqrt((s_all - s_lo) * (1.0 / FOX_D) + EPS))
            t = dy * w2 * x
            t_all = jnp.sum(t, axis=1, keepdims=True)
            t_lo = jnp.sum(jnp.where(lo, t, 0.0), axis=1, keepdims=True)
            dx = r * (w2 * dy - x * (r * r) * (jnp.where(lo, t_lo, t_all - t_lo) * (1.0 / FOX_D)))
            return dx, jnp.sum(dy * x * r, axis=0, keepdims=True)

        dcum = jnp.zeros((tm, 128), f32)
        dqw2 = jnp.zeros((1, 128), f32)
        dkw2 = jnp.zeros((1, 128), f32)
        for p in range(FOX_H // 2):
            ps = slice(p * 128, (p + 1) * 128)
            dxq, dw1 = norm_pair_bwd(q_ref[:, ps], qw2, pair(dQ_ref, p))
            dxk, dw2 = norm_pair_bwd(k_ref[:, ps], kw2, pair(dK_ref, p))
            dp_ref[:, p * 128:(p + 1) * 128] = dxq
            dp_ref[:, FOX_W + p * 128:FOX_W + (p + 1) * 128] = dxk
            dp_ref[:, 2 * FOX_W + p * 128:2 * FOX_W + (p + 1) * 128] = pair(dV_ref, p)
            dqw2 = dqw2 + dw1
            dkw2 = dkw2 + dw2
            for e in range(2):
                h = 2 * p + e
                dcum = dcum + jnp.where(lane == h, dQ_ref[:, h * 128 + FOX_D:h * 128 + FOX_D + 1]
                                        - dK_ref[:, h * 128 + FOX_D + 3:h * 128 + FOX_D + 4], 0.0)
        dp_ref[:, 3 * FOX_W:4 * FOX_W] = dz_ref[...]
        ii = lax.broadcasted_iota(jnp.int32, (tm, tm), 0)
        jj = lax.broadcasted_iota(jnp.int32, (tm, tm), 1)
        dlf = _hdot((ii <= jj).astype(f32), dcum) + carry[...]
        carry[...] += jnp.sum(dcum, axis=0, keepdims=True)
        df = dlf * jax.nn.sigmoid(-(f_ref[...] + fb_ref[...]))
        dp_ref[:, 4 * FOX_W:FOX_IN_PAD] = df
        dfb_ref[...] += jnp.sum(df, axis=0, keepdims=True)
        dqw_ref[...] += (dqw2[:, :FOX_D] + dqw2[:, FOX_D:]) * FOX_D ** -0.5
        dkw_ref[...] += dkw2[:, :FOX_D] + dkw2[:, FOX_D:]

    rv = lambda i: NB - 1 - i

    def colblk(c):
        return pl.BlockSpec((tm, FOX_W), lambda i: (rv(i), c))

    pad = pl.BlockSpec((tm, FOX_PW), lambda i: (rv(i), 0))
    cmp_ = pl.BlockSpec((tm, FOX_W), lambda i: (rv(i), 0))
    v128 = pl.BlockSpec((1, 128), lambda i: (0, 0))
    v64 = pl.BlockSpec((1, FOX_D), lambda i: (0, 0))
    return pl.pallas_call(
        body, name=name, grid=(NB,),
        in_specs=[colblk(0), colblk(1), pl.BlockSpec((tm, 128), lambda i: (rv(i), _FOX_F_BLK)), v128, v64, v64, pad, pad, pad, cmp_],
        out_specs=[pl.BlockSpec((tm, FOX_IN_PAD), lambda i: (rv(i), 0)), v128, v64, v64],
        out_shape=[SDS((S, FOX_IN_PAD), f32), SDS((1, 128), f32), SDS((1, FOX_D), f32), SDS((1, FOX_D), f32)],
        scratch_shapes=[pltpu.VMEM((1, 128), f32)],
        compiler_params=_cp(("arbitrary",)),
    )(proj, proj, proj, f_bias, qn_w, kn_w, dQ, dK, dV, dz)


FOX_HB = 2


def _diag_mask(t):
    return lax.broadcasted_iota(jnp.int32, (t, t), 1) <= lax.broadcasted_iota(jnp.int32, (t, t), 0)


def fox_attn_fwd(Q, K, V, name, side=None):
    S = Q.shape[0]
    t = _rows(S, 512)

    HB = FOX_HB
    HS = [slice(h * 128, (h + 1) * 128) for h in range(HB)]

    def body(q_ref, k_ref, v_ref, o_ref, m_sc, acc_sc, s_sc):
        i = pl.program_id(1)
        qs = [q_ref[:, sl] for sl in HS]
        m_sc[...] = jnp.full_like(m_sc, NEG)
        acc_sc[...] = jnp.zeros_like(acc_sc)

        def scores(j):
            j0 = pl.multiple_of(j * t, t)
            return [_nt(qs[h], k_ref[pl.ds(j0, t), HS[h]]) for h in range(HB)]

        def tile(j, diag):
            j0 = pl.multiple_of(j * t, t)
            ss = [s_sc[h] for h in range(HB)]
            if diag:
                ss = [jnp.where(_diag_mask(t), s, NEG) for s in ss]
            else:
                nxt = scores(j + 1)
            ms = [m_sc[h] for h in range(HB)]
            m_new = [jnp.maximum(ms[h], jnp.max(ss[h], axis=1, keepdims=True)) for h in range(HB)]
            ps = [jnp.exp(ss[h] - m_new[h]) for h in range(HB)]
            pv = [_nn(ps[h], v_ref[pl.ds(j0, t), HS[h]]) for h in range(HB)]
            for h in range(HB):
                acc_sc[h] = acc_sc[h] * jnp.exp(ms[h] - m_new[h]) + pv[h]
                m_sc[h] = m_new[h]
                if not diag:
                    s_sc[h] = nxt[h]

        def off_diag(j, c):
            tile(j, False)
            return c

        first = scores(0)
        for h in range(HB):
            s_sc[h] = first[h]
        lax.fori_loop(0, i, off_diag, 0)
        tile(i, True)
        lane = lax.broadcasted_iota(jnp.int32, (t, 128), 1)
        for h in range(HB):
            acc = acc_sc[h]
            l = acc[:, FOX_D:FOX_D + 1]
            o_ref[:, HS[h]] = jnp.where(lane == FOX_D, m_sc[h] + jnp.log(l), acc / l)

    blk = pl.BlockSpec((t, HB * 128), lambda h, i: (i, h))
    seq = pl.BlockSpec((S, HB * 128), lambda h, i: (0, h))
    return _call(
        body, name=name, grid=(FOX_H // HB, S // t),
        in_specs=[blk, seq, seq], out_specs=[blk], out_shape=[SDS((S, FOX_PW), f32)],
        scratch=[pltpu.VMEM((HB, t, 1), f32), pltpu.VMEM((HB, t, 128), f32), pltpu.VMEM((HB, t, t), f32)],
        args=(Q, K, V), side=side)


def fox_attn_bwd(Q, K, V, dO, O, name, side=None):
    S = Q.shape[0]
    t = _rows(S, 512)
    nq = S // t

    HB = FOX_HB
    HS = [slice(h * 128, (h + 1) * 128) for h in range(HB)]

    def body(k_ref, v_ref, q_ref, do_ref, o_ref, dq_ref, dk_ref, dv_ref):
        j = pl.program_id(1)

        @pl.when(j == 0)
        def _():
            dq_ref[...] = jnp.zeros_like(dq_ref)

        dk_ref[...] = jnp.zeros_like(dk_ref)
        dv_ref[...] = jnp.zeros_like(dv_ref)
        ks = [k_ref[:, sl] for sl in HS]
        vs = [v_ref[:, sl] for sl in HS]

        def tile(i, diag):
            i0 = pl.multiple_of(i * t, t)
            R = range(HB)
            qs = [q_ref[pl.ds(i0, t), HS[h]] for h in R]
            dos = [do_ref[pl.ds(i0, t), HS[h]] for h in R]
            ss = [_nt(qs[h], ks[h]) - o_ref[pl.ds(i0, t), h * 128 + FOX_D:h * 128 + FOX_D + 1] for h in R]
            if diag:
                ss = [jnp.where(_diag_mask(t), s, NEG) for s in ss]
            ps = [jnp.exp(s) for s in ss]
            dps = [_nt(dos[h], vs[h]) for h in R]
            dvs = [_tn(ps[h], dos[h]) for h in R]
            dss = [(ps[h] * dps[h]).astype(bf16) for h in R]
            dks = [_tn(dss[h], qs[h]) for h in R]
            dqs = [_nn(dss[h], ks[h]) for h in R]
            for h in R:
                dv_ref[:, HS[h]] += dvs[h]
                dk_ref[:, HS[h]] += dks[h]
                dq_ref[pl.ds(i0, t), HS[h]] += dqs[h]

        tile(j, True)

        def off_diag(i, c):
            tile(i, False)
            return c

        lax.fori_loop(j + 1, nq, off_diag, 0)

    blk = pl.BlockSpec((t, HB * 128), lambda h, j: (j, h))
    seq = pl.BlockSpec((S, HB * 128), lambda h, j: (0, h))
    return _call(
        body, name=name, grid=(FOX_H // HB, nq),
        in_specs=[blk, blk, seq, seq, seq], out_specs=[seq, blk, blk],
        out_shape=[SDS((S, FOX_PW), f32)] * 3, args=(K, V, Q, dO, O), side=side)


def fox_out_fwd(O, proj, W, x, gate, name):
    S, D = x.shape
    tm = _rows(S, 256)

    def body(o_ref, z_ref, w_ref, x_ref, g_ref, xn_ref, y_ref, og_ref):
        z = z_ref[...]
        og = jnp.concatenate([o_ref[:, h * 128:h * 128 + FOX_D] * _silu(z[:, h * FOX_D:(h + 1) * FOX_D]) for h in range(FOX_H)],
                             axis=1).astype(bf16)
        y = jnp.dot(og, w_ref[...], preferred_element_type=f32)
        og_ref[...] = og
        y_ref[...] = y
        xn_ref[...] = x_ref[...] + g_ref[...] * y

    row = pl.BlockSpec((tm, D), lambda i: (i, 0))
    cmp_ = pl.BlockSpec((tm, FOX_W), lambda i: (i, 0))
    return pl.pallas_call(
        body, name=name, grid=(S // tm,),
        in_specs=[pl.BlockSpec((tm, FOX_PW), lambda i: (i, 0)), pl.BlockSpec((tm, FOX_W), lambda i: (i, 3)),
                  pl.BlockSpec((FOX_W, D), lambda i: (0, 0)), row, pl.BlockSpec((1, D), lambda i: (0, 0))],
        out_specs=[row, row, cmp_],
        out_shape=[SDS((S, D), f32), SDS((S, D), f32), SDS((S, FOX_W), bf16)],
        compiler_params=_cp(("arbitrary",)),
    )(O, proj, W, x, gate)


def fox_out_bwd(dxn, y, gate, O, proj, W, name):
    S, D = dxn.shape
    tm = _rows(S, 256)

    def body(dx_ref, y_ref, g_ref, o_ref, z_ref, w_ref, dy_ref, dg_ref, dO_ref, dz_ref):
        @pl.when(pl.program_id(0) == 0)
        def _():
            dg_ref[...] = jnp.zeros_like(dg_ref)

        dx = dx_ref[...]
        dy = dx * g_ref[...]
        dy_ref[...] = dy
        dg_ref[...] += jnp.sum(dx * y_ref[...], axis=0, keepdims=True)
        dog = _nt(dy, w_ref[...])
        z = z_ref[...]
        lane = lax.broadcasted_iota(jnp.int32, (tm, FOX_D), 1)
        dzs = []
        for h in range(FOX_H):
            sl = slice(h * FOX_D, (h + 1) * FOX_D)
            zh = z[:, sl]
            sg = jax.nn.sigmoid(zh)
            oh = o_ref[:, h * 128:h * 128 + FOX_D]
            doh = dog[:, sl] * (zh * sg)
            delta = jnp.sum(doh * oh, axis=1, keepdims=True)
            dO_ref[:, h * 128:(h + 1) * 128] = jnp.concatenate([doh, jnp.where(lane == 0, -delta, 0.0)], axis=1).astype(bf16)
            dzs.append(dog[:, sl] * oh * (sg * (1.0 + zh * (1.0 - sg))))
        dz_ref[...] = jnp.concatenate(dzs, axis=1)

    row = pl.BlockSpec((tm, D), lambda i: (i, 0))
    vecd = pl.BlockSpec((1, D), lambda i: (0, 0))
    pad = pl.BlockSpec((tm, FOX_PW), lambda i: (i, 0))
    return pl.pallas_call(
        body, name=name, grid=(S // tm,),
        in_specs=[row, row, vecd, pad, pl.BlockSpec((tm, FOX_W), lambda i: (i, 3)), pl.BlockSpec((FOX_W, D), lambda i: (0, 0))],
        out_specs=[row, vecd, pad, pl.BlockSpec((tm, FOX_W), lambda i: (i, 0))],
        out_shape=[SDS((S, D), f32), SDS((1, D), f32), SDS((S, FOX_PW), bf16), SDS((S, FOX_W), f32)],
        compiler_params=_cp(("arbitrary",)),
    )(dxn, y, gate, O, proj, W)


def final_loss(x, fw, target, name):
    S, D = x.shape
    tm = _rows(S, 512)

    def body(x_ref, w_ref, t_ref, l_ref, dx_ref, dw_ref):
        @pl.when(pl.program_id(0) == 0)
        def _():
            l_ref[...] = jnp.zeros_like(l_ref)
            dw_ref[...] = jnp.zeros_like(dw_ref)

        out, vjp = jax.vjp(_rms_w, x_ref[...], w_ref[...])
        err = out - t_ref[...]
        l_ref[...] += 0.5 * jnp.sum(jnp.sum(err * err, axis=1, keepdims=True) * (1.0 / D), axis=0, keepdims=True)
        dx, dw = vjp(err * (1.0 / D))
        dx_ref[...] = dx
        dw_ref[...] += dw

    row = pl.BlockSpec((tm, D), lambda i: (i, 0))
    vec = pl.BlockSpec((1, D), lambda i: (0, 0))
    return pl.pallas_call(
        body, name=name, grid=(S // tm,),
        in_specs=[row, vec, row], out_specs=[pl.BlockSpec((1, 128), lambda i: (0, 0)), row, vec],
        out_shape=[SDS((1, 128), f32), SDS((S, D), f32), SDS((1, D), f32)],
        compiler_params=_cp(("arbitrary",)),
    )(x, fw, target)


def ada_fwd(c_all, ada_w, name):
    L, D, n = ada_w.shape

    def body(c_ref, w_ref, o_ref):
        cond = jnp.concatenate([_silu(c_ref[...]), jnp.zeros((8, D), f32)], axis=0)
        o_ref[0] = _nn(cond, w_ref[0])[0:8]

    return pl.pallas_call(
        body, name=name, grid=(L,),
        in_specs=[pl.BlockSpec((NDEV, D), lambda l: (0, 0)), pl.BlockSpec((1, D, n), lambda l: (l, 0, 0))],
        out_specs=pl.BlockSpec((1, NDEV, n), lambda l: (l, 0, 0)),
        out_shape=SDS((L, NDEV, n), f32),
        compiler_params=_cp(("arbitrary",)),
    )(c_all, ada_w)


def ada_grad(c_all, dmod, name):
    L, _, n = dmod.shape
    D = c_all.shape[1]

    def body(c_ref, d_ref, o_ref):
        cond = jnp.concatenate([_silu(c_ref[...]), jnp.zeros((8, D), f32)], axis=0)
        dm = jnp.concatenate([d_ref[0], jnp.zeros((8, n), f32)], axis=0)
        o_ref[0] = _tn(cond, dm)

    return pl.pallas_call(
        body, name=name, grid=(L,),
        in_specs=[pl.BlockSpec((NDEV, D), lambda l: (0, 0)), pl.BlockSpec((1, NDEV, n), lambda l: (l, 0, 0))],
        out_specs=pl.BlockSpec((1, D, n), lambda l: (l, 0, 0)),
        out_shape=SDS((L, D, n), f32),
        compiler_params=_cp(("arbitrary",)),
    )(c_all, dmod)


def reduce_adam(parts, w, m, v, tr, name):
    n, R, C = parts.shape
    c1 = 1.0 / (1.0 - ADAM_B1 ** ADAM_STEP)
    c2 = 1.0 / (1.0 - ADAM_B2 ** ADAM_STEP)

    def body(p_ref, w_ref, m_ref, v_ref, g_ref, d_ref, nm_ref, nv_ref):
        g = p_ref[0].astype(f32)
        for s in range(1, n):
            g = g + p_ref[s].astype(f32)
        nm = ADAM_B1 * m_ref[...] + (1.0 - ADAM_B1) * g
        nv = ADAM_B2 * v_ref[...] + (1.0 - ADAM_B2) * (g * g)
        g_ref[...] = g
        nm_ref[...] = nm
        nv_ref[...] = nv
        d_ref[...] = -ADAM_LR * ((nm * c1) / (jnp.sqrt(nv * c2) + ADAM_EPS) + ADAM_WD * w_ref[...])

    blk = pl.BlockSpec((tr, C), lambda i: (i, 0))
    return pl.pallas_call(
        body, name=name, grid=(R // tr,),
        in_specs=[pl.BlockSpec((n, tr, C), lambda i: (0, i, 0)), blk, blk, blk],
        out_specs=[blk] * 4, out_shape=[SDS((R, C), f32)] * 4,
        compiler_params=_cp(("arbitrary",)),
    )(parts, w, m, v)


def all_gather(xs, name):
    n = len(xs)

    def body(*refs):
        x_refs, out_refs = refs[:n], refs[n:2 * n]
        send_sems, recv_sems, local_sems = refs[2 * n:]
        x_, y_, c_ = _my_pos()
        me, sibling = (x_, y_, c_), (x_, y_, 1 - c_)
        chips = [(1 - x_, y_), (x_, 1 - y_), (1 - x_, 1 - y_)]

        def rows(a, px, py, pc):
            return out_refs[a].at[4 * px + 2 * py + pc]

        def copy(a, k, block, to, own=False):
            return pltpu.make_async_remote_copy(
                src_ref=x_refs[a] if own else rows(a, *block), dst_ref=rows(a, *block),
                send_sem=send_sems.at[k, a], recv_sem=recv_sems.at[k, a], device_id=to, device_id_type=pl.DeviceIdType.MESH)

        mine = [pltpu.make_async_copy(x_refs[a], rows(a, *me), local_sems.at[a]) for a in range(n)]
        for cp in mine:
            cp.start()
        first = []
        for a in range(n):
            first.append(copy(a, 0, me, sibling, own=True))
            first += [copy(a, 1 + j, me, (*chip, c_), own=True) for j, chip in enumerate(chips)]
        for cp in first:
            cp.start()
        passed = []
        for j, chip in enumerate(chips):
            for a in range(n):
                copy(a, 1 + j, (*chip, c_), me).wait_recv()
                cp = copy(a, 4 + j, (*chip, c_), sibling)
                cp.start()
                passed.append(cp)
        for a in range(n):
            copy(a, 0, sibling, me).wait_recv()
            for j, chip in enumerate(chips):
                copy(a, 4 + j, (*chip, 1 - c_), me).wait_recv()
        for cp in first + passed:
            cp.wait_send()
        for cp in mine:
            cp.wait()

    any_ = pl.BlockSpec(memory_space=pl.ANY)
    return pl.pallas_call(
        body, name=name, out_shape=[SDS((NDEV,) + x.shape, x.dtype) for x in xs],
        in_specs=[any_] * n, out_specs=[any_] * n,
        scratch_shapes=[pltpu.SemaphoreType.DMA((7, n)), pltpu.SemaphoreType.DMA((7, n)), pltpu.SemaphoreType.DMA((n,))],
    )(*xs)


GDN_COLS = ((0, GDN_CONV + GDN_V_W, 0), (GDN_CONV + GDN_V_W, GDN_CONV + GDN_V_W + 16, GDN_CONV + GDN_V_W),
            (GDN_CONV + GDN_V_W + 16, GDN_IN, GDN_CONV + GDN_V_W + 128))
FOX_COLS = ((0, FOX_IN, 0),)


def _col_pieces(d, per, cols):
    lo, hi = per * d, per * (d + 1)
    out = []
    for a, b, dst in cols:
        s, e = max(lo, a), min(hi, b)
        if s < e:
            out.append((s - lo, e - s, dst + s - a))
    return out


def cols_from_blocks(g, cols, n_out, name):
    _, L, R, C = g.shape
    tr = min(256, R)

    def body(g_ref, o_ref):
        o_ref[...] = jnp.zeros_like(o_ref)
        for d in range(NDEV):
            for off, ln, dst in _col_pieces(d, C, cols):
                o_ref[0, :, dst:dst + ln] = g_ref[d, 0, :, off:off + ln]

    return pl.pallas_call(
        body, name=name, grid=(L, R // tr),
        in_specs=[pl.BlockSpec((NDEV, 1, tr, C), lambda l, i: (0, l, i, 0))],
        out_specs=pl.BlockSpec((1, tr, n_out), lambda l, i: (l, i, 0)),
        out_shape=SDS((L, R, n_out), g.dtype),
        compiler_params=_cp(("arbitrary", "arbitrary")),
    )(g)


def blocks_from_cols(dw, C, cols, name):
    R, n_in = dw.shape
    tr = min(256, R)

    def body(x_ref, o_ref):
        for d in range(NDEV):
            for off, ln, src in _col_pieces(d, C, cols):
                o_ref[d, :, off:off + ln] = x_ref[:, src:src + ln].astype(bf16)

    return pl.pallas_call(
        body, name=name, grid=(R // tr,),
        in_specs=[pl.BlockSpec((tr, n_in), lambda i: (i, 0))],
        out_specs=pl.BlockSpec((NDEV, tr, C), lambda i: (0, i, 0)),
        out_shape=SDS((NDEV, R, C), bf16),
        compiler_params=_cp(("arbitrary",)),
    )(dw)


BIG = ("a_w_in", "a_conv_w", "a_w_out", "b_w_in", "b_w_out")
SMALL = ("norm_w", "ada_b", "a_A_log", "a_dt_bias", "a_norm_w", "b_f_bias", "b_qn_w", "b_kn_w", "final_norm_w")


def _pack_small(arrs):
    rows = []
    for a in arrs:
        fl = a.reshape(-1)
        pad = (-fl.shape[0]) % 128
        if pad:
            fl = jnp.concatenate([fl, jnp.zeros((pad,), fl.dtype)])
        rows.append(fl)
    flat = jnp.concatenate(rows)
    pad = (-flat.shape[0]) % (8 * 128)
    if pad:
        flat = jnp.concatenate([flat, jnp.zeros((pad,), flat.dtype)])
    return flat.reshape(-1, 128)


def _unpack(packed, shapes, align):
    flat = packed.reshape(-1)
    out, off = [], 0
    for shp in shapes:
        n = 1
        for d in shp:
            n *= d
        out.append(flat[off:off + n].reshape(shp))
        off += n + ((-n) % align)
    return out


def _full_from_gathered(g, shard_shape, axis):
    g = jnp.moveaxis(g, 0, axis)
    shp = list(shard_shape)
    shp[axis] *= NDEV
    return g.reshape(shp)


def _pad_lanes(v, n=128):
    v = v.reshape(1, -1)
    return jnp.concatenate([v, jnp.zeros((1, n - v.shape[1]), v.dtype)], axis=1)


def _carried(fn, *args, side=None, **grads):
    if callable(side):
        side = side(**grads)
    res = fn(*args, side)
    return res if side is not None else (res, None)


def gdn_layer_fwd(x, mod, nw, weights, tag, sides):
    W_in, conv_w, A_log, dt_bias, a_nw, W_out = weights
    shift, scale, gate = mod
    got = {}
    (proj, h), got["inproj"] = _carried(inproj_fwd, x, nw, scale, shift, W_in, GDN_TN, f"{tag}_inproj", side=sides.get("inproj"))
    qkvc = gdn_prep_fwd(proj, conv_w, f"{tag}_prep")
    gc, beta = gdn_gates_fwd(proj, A_log, dt_bias, f"{tag}_gates")
    (u, w, attn, Ts), got["intra"] = _carried(gdn_intra_fwd, qkvc, gc, beta, f"{tag}_intra", side=sides.get("intra"))
    (o, states), got["scan"] = _carried(gdn_scan_fwd, qkvc, gc, u, w, attn, f"{tag}_scan", side=sides.get("scan"))
    x_new, y, og = gdn_out_fwd(o, proj, a_nw, W_out, x, gate, f"{tag}_out")
    return x_new, (x, proj, h, qkvc, gc, beta, o, states, Ts, y, og, u, w, attn), got


def gdn_layer_bwd(dxn, saved, mod, nw, weights, tag, sides):
    W_in, conv_w, A_log, dt_bias, a_nw, W_out = weights
    shift, scale, gate = mod
    x, proj, h, qkvc, gc, beta, o, states, Ts, y, og, u, w, attn = saved
    got = {}
    dy, dgate, do, dproj, da_nw = gdn_out_bwd(dxn, y, gate, o, proj, a_nw, W_out, f"{tag}_out_bwd")
    dW_out, = matmul_tn(og, dy, 512, f"{tag}_dwout")
    (dq_s, dk_s, dgc_s, du, dw, dattn), got["sbwd"] = _carried(
        gdn_scan_bwd, qkvc, gc, u, w, attn, states, do, f"{tag}_scan_bwd", side=sides.get("sbwd"), dW_out=dW_out)
    (dqkvc, dgc, dbeta), got["intrab"] = _carried(
        gdn_intra_bwd, qkvc, gc, beta, Ts, du, dw, dattn, dq_s, dk_s, dgc_s, f"{tag}_intra_bwd", side=sides.get("intrab"), dW_out=dW_out)
    dproj, dA_log, ddt = gdn_gates_bwd(proj, A_log, dt_bias, dgc, dbeta, dproj, f"{tag}_gates_bwd")
    dproj, dconv_w = gdn_prep_bwd(proj, conv_w, dqkvc, dproj, f"{tag}_prep_bwd")
    (dW_in,), got["dwin"] = _carried(matmul_tn, h, dproj, GDN_TN, f"{tag}_dwin", side=sides.get("dwin"), dW_out=dW_out)
    (dx, dnw, dscale, dshift), got["ibwd"] = _carried(
        inproj_bwd_x, x, nw, scale, shift, W_in, dproj, dxn, GDN_TN, f"{tag}_inproj_bwd", side=sides.get("ibwd"),
        dW_out=dW_out, dW_in=dW_in, dconv_w=dconv_w)
    grads = dict(norm_w=dnw, W_in=dW_in, conv_w=dconv_w, A_log=dA_log[:, :16], dt_bias=ddt[:, :16], a_nw=da_nw, W_out=dW_out,
                 dmod=jnp.concatenate([dshift, dscale, dgate], axis=1))
    return dx, grads, got


def fox_layer_fwd(x, mod, nw, weights, tag, sides):
    W_in, f_bias, qn_w, kn_w, W_out = weights
    shift, scale, gate = mod
    got = {}
    (proj, h), got["inproj"] = _carried(inproj_fwd, x, nw, scale, shift, W_in, FOX_TN, f"{tag}_inproj", side=sides.get("inproj"))
    Q, K, V = fox_prep_fwd(proj, f_bias, qn_w, kn_w, f"{tag}_prep")
    (O,), got["attn"] = _carried(fox_attn_fwd, Q, K, V, f"{tag}_attn", side=sides.get("attn"))
    x_new, y, og = fox_out_fwd(O, proj, W_out, x, gate, f"{tag}_out")
    return x_new, (x, proj, h, Q, K, V, O, y, og), got


def fox_layer_bwd(dxn, saved, mod, nw, weights, tag, sides):
    W_in, f_bias, qn_w, kn_w, W_out = weights
    shift, scale, gate = mod
    x, proj, h, Q, K, V, O, y, og = saved
    got = {}
    dy, dgate, dO, dz = fox_out_bwd(dxn, y, gate, O, proj, W_out, f"{tag}_out_bwd")
    dW_out, = matmul_tn(og, dy, 512, f"{tag}_dwout")
    (dQ, dK, dV), got["abwd"] = _carried(fox_attn_bwd, Q, K, V, dO, O, f"{tag}_attn_bwd", side=sides.get("abwd"))
    dproj, dfb, dqw, dkw = fox_prep_bwd(proj, f_bias, qn_w, kn_w, dQ, dK, dV, dz, f"{tag}_prep_bwd")
    (dW_in,), got["dwin"] = _carried(matmul_tn, h, dproj, FOX_TN, f"{tag}_dwin", side=sides.get("dwin"))
    dx, dnw, dscale, dshift = inproj_bwd_x(x, nw, scale, shift, W_in, dproj, dxn, FOX_TN, f"{tag}_inproj_bwd")
    grads = dict(norm_w=dnw, W_in=dW_in, f_bias=dfb[:, :16], qn_w=dqw, kn_w=dkw, W_out=dW_out,
                 dmod=jnp.concatenate([dshift, dscale, dgate], axis=1))
    return dx, grads, got


class LocalPlan:
    def __init__(self, full):
        self.full = full

    def layer_weights(self, i):
        j, f = i // 2, self.full
        return (f["a_w_in"][j], f["a_w_out"][j], f["a_conv_w"][j]) if i % 2 == 0 else (f["b_w_in"][j], f["b_w_out"][j])

    def fwd_sides(self, i):
        return {}

    def fwd_got(self, i, got):
        pass

    def bwd_sides(self, i):
        return {}

    def bwd_got(self, i, grads, got):
        pass


def device_step(x, mod_all, norm_w, small, final_norm_w, target, plan):
    D = x.shape[1]
    mods = [(mod_all[i:i + 1, 0:D], mod_all[i:i + 1, D:2 * D], mod_all[i:i + 1, 2 * D:3 * D]) for i in range(4)]

    def weights(i):
        j = i // 2
        if i % 2 == 0:
            W_in, W_out, conv_w = plan.layer_weights(i)
            return (W_in, conv_w, _pad_lanes(small["a_A_log"][j]), _pad_lanes(small["a_dt_bias"][j]), small["a_norm_w"][j:j + 1], W_out)
        W_in, W_out = plan.layer_weights(i)
        return (W_in, _pad_lanes(small["b_f_bias"][j]), small["b_qn_w"][j:j + 1], small["b_kn_w"][j:j + 1], W_out)

    saved, wts = [], []
    for i in range(4):
        wts.append(weights(i))
        fwd = gdn_layer_fwd if i % 2 == 0 else fox_layer_fwd
        x, sv, got = fwd(x, mods[i], norm_w[i:i + 1], wts[i], f"L{i}", plan.fwd_sides(i))
        plan.fwd_got(i, got)
        saved.append(sv)
    loss, dx, dfw = final_loss(x, final_norm_w.reshape(1, D), target, "final_loss")
    lg = [None] * 4
    for i in reversed(range(4)):
        bwd = gdn_layer_bwd if i % 2 == 0 else fox_layer_bwd
        dx, lg[i], got = bwd(dx, saved[i], mods[i], norm_w[i:i + 1], wts[i], f"L{i}", plan.bwd_sides(i))
        plan.bwd_got(i, lg[i], got)
    g = dict(
        norm_w=jnp.concatenate([lg[i]["norm_w"] for i in range(4)], axis=0),
        dmod=jnp.concatenate([lg[i]["dmod"] for i in range(4)], axis=0),
        a_w_in=[lg[i]["W_in"] for i in (0, 2)],
        a_conv_w=jnp.stack([lg[i]["conv_w"] for i in (0, 2)]),
        a_A_log=jnp.concatenate([lg[i]["A_log"] for i in (0, 2)], axis=0),
        a_dt_bias=jnp.concatenate([lg[i]["dt_bias"] for i in (0, 2)], axis=0),
        a_norm_w=jnp.concatenate([lg[i]["a_nw"] for i in (0, 2)], axis=0),
        a_w_out=[lg[i]["W_out"] for i in (0, 2)],
        b_w_in=[lg[i]["W_in"] for i in (1, 3)],
        b_f_bias=jnp.concatenate([lg[i]["f_bias"] for i in (1, 3)], axis=0),
        b_qn_w=jnp.concatenate([lg[i]["qn_w"] for i in (1, 3)], axis=0),
        b_kn_w=jnp.concatenate([lg[i]["kn_w"] for i in (1, 3)], axis=0),
        b_w_out=[lg[i]["W_out"] for i in (1, 3)],
        final_norm_w=dfw.reshape(-1),
    )
    return loss[0, 0], dx, g


class MeshPlan:
    def __init__(self, shards, w0, conv_full):
        self.shards = shards
        self.w = {0: w0}
        self.conv = conv_full
        self.recv = {}
        self.pending = {}
        self.names = {}

    def layer_weights(self, i):
        return self.w[i]

    def _gather_side(self, layer):
        names = ("a_w_in", "a_w_out") if layer % 2 == 0 else ("b_w_in", "b_w_out")
        out = []
        for n in names:
            sh = self.shards[n][layer // 2]
            out.append(sh.reshape(-1, sh.shape[-1]))
        return ("gather", out)

    def fwd_sides(self, i):
        if i == 0:
            kind, (b_in, b_out) = self._gather_side(1)
            return {"inproj": (kind, [b_out]), "scan": (kind, [b_in]), "intra": self._gather_side(2)}
        if i == 1:
            return {"attn": self._gather_side(3)}
        return {}

    def fwd_got(self, i, got):
        if got.get("inproj") is not None:
            self._b_out0, = got["inproj"]
        for key, layer in (("scan", 1), ("intra", 2), ("attn", 3)):
            if got.get(key) is None:
                continue
            g_in, g_out = (got[key][0], self._b_out0) if key == "scan" else got[key]
            D = g_out.shape[-1]
            j = layer // 2
            if layer % 2 == 1:
                W_in = cols_from_blocks(g_in[:, None], FOX_COLS, FOX_IN_PAD, f"b_w_in_cols{j}")[0]
                self.w[layer] = (W_in, g_out.reshape(-1, D))
            else:
                W_in = cols_from_blocks(g_in[:, None], GDN_COLS, GDN_IN_PAD, f"a_w_in_cols{j}")[0]
                self.w[layer] = (W_in, g_out.reshape(-1, D), self.conv[j])

    @staticmethod
    def _out_blocks(dW_out):
        return dW_out.astype(bf16).reshape(NDEV, -1, dW_out.shape[-1])

    def _in_blocks(self, name, j, dW_in):
        cols = GDN_COLS if name == "a_w_in" else FOX_COLS
        return blocks_from_cols(dW_in, self.shards[name].shape[-1], cols, f"{name}_blocks{j}")

    def bwd_sides(self, i):
        self.names = {}
        sides = {}
        for (layer, key) in [k for k in self.pending if k[0] == i]:
            self.names[key], arrs = self.pending.pop((layer, key))
            sides[key] = ("scatter", arrs)
        if i == 0:
            def sbwd(dW_out):
                self.names["sbwd"] = [("a_w_out", 0)]
                return ("scatter", [self._out_blocks(dW_out)])

            def ibwd(dW_out, dW_in, dconv_w):
                conv = jnp.stack([dconv_w, self._dconv1])
                n = conv.shape[-1] // NDEV
                self.names["ibwd"] = [("a_w_in", 0), ("a_conv_w", None)]
                return ("scatter", [self._in_blocks("a_w_in", 0, dW_in),
                                    jnp.moveaxis(conv.reshape(2, 4, NDEV, n), 2, 0).reshape(NDEV, 8, n)])

            sides["sbwd"], sides["ibwd"] = sbwd, ibwd
        return sides

    def bwd_got(self, i, grads, got):
        for key, arrs in got.items():
            if arrs is not None:
                self.recv.update(zip(self.names[key], arrs))
        j = i // 2
        if i % 2 == 1:
            self.pending[(i - 1, "sbwd" if i == 3 else "intrab")] = (
                [("b_w_in", j), ("b_w_out", j)], [self._in_blocks("b_w_in", j, grads["W_in"]), self._out_blocks(grads["W_out"])])
        elif i == 2:
            self.pending[(1, "abwd")] = (
                [("a_w_in", 1), ("a_w_out", 1)], [self._in_blocks("a_w_in", 1, grads["W_in"]), self._out_blocks(grads["W_out"])])
            self._dconv1 = grads["conv_w"]


def kernel(x, c, norm_w, ada_w, ada_b, a_w_in, a_conv_w, a_A_log, a_dt_bias, a_norm_w, a_w_out, b_w_in, b_f_bias, b_qn_w, b_kn_w, b_w_out, final_norm_w, loss_target, m_norm_w, m_ada_w, m_ada_b, m_a_w_in, m_a_conv_w, m_a_A_log, m_a_dt_bias, m_a_norm_w, m_a_w_out, m_b_w_in, m_b_f_bias, m_b_qn_w, m_b_kn_w, m_b_w_out, m_final_norm_w, v_norm_w, v_ada_w, v_ada_b, v_a_w_in, v_a_conv_w, v_a_A_log, v_a_dt_bias, v_a_norm_w, v_a_w_out, v_b_w_in, v_b_f_bias, v_b_qn_w, v_b_kn_w, v_b_w_out, v_final_norm_w):
    W = dict(norm_w=norm_w, ada_w=ada_w, ada_b=ada_b, a_w_in=a_w_in, a_conv_w=a_conv_w, a_A_log=a_A_log, a_dt_bias=a_dt_bias,
             a_norm_w=a_norm_w, a_w_out=a_w_out, b_w_in=b_w_in, b_f_bias=b_f_bias, b_qn_w=b_qn_w, b_kn_w=b_kn_w, b_w_out=b_w_out,
             final_norm_w=final_norm_w)
    M = dict(norm_w=m_norm_w, ada_w=m_ada_w, ada_b=m_ada_b, a_w_in=m_a_w_in, a_conv_w=m_a_conv_w, a_A_log=m_a_A_log,
             a_dt_bias=m_a_dt_bias, a_norm_w=m_a_norm_w, a_w_out=m_a_w_out, b_w_in=m_b_w_in, b_f_bias=m_b_f_bias, b_qn_w=m_b_qn_w,
             b_kn_w=m_b_kn_w, b_w_out=m_b_w_out, final_norm_w=m_final_norm_w)
    V = dict(norm_w=v_norm_w, ada_w=v_ada_w, ada_b=v_ada_b, a_w_in=v_a_w_in, a_conv_w=v_a_conv_w, a_A_log=v_a_A_log,
             a_dt_bias=v_a_dt_bias, a_norm_w=v_a_norm_w, a_w_out=v_a_w_out, b_w_in=v_b_w_in, b_f_bias=v_b_f_bias, b_qn_w=v_b_qn_w,
             b_kn_w=v_b_kn_w, b_w_out=v_b_w_out, final_norm_w=v_final_norm_w)
    S, D = x.shape[1], x.shape[2]
    me = 4 * lax.axis_index("x") + 2 * lax.axis_index("y") + lax.axis_index("c")
    small_shapes = [W[n].shape for n in SMALL]

    shards = {n: W[n].astype(bf16) for n in ("a_w_in", "a_w_out", "b_w_in", "b_w_out")}
    gath = all_gather([shards["a_w_in"][0], shards["a_w_out"][0], a_conv_w.reshape(8, -1), c.reshape(8, D // 8)], "gather_w0")
    conv_full = _full_from_gathered(gath[2].reshape((NDEV,) + a_conv_w.shape), a_conv_w.shape, 2)
    w0 = (cols_from_blocks(gath[0][:, None], GDN_COLS, GDN_IN_PAD, "a_w_in_cols0")[0], gath[1].reshape(-1, D), conv_full[0])
    plan = MeshPlan(shards, w0, conv_full)
    c_all = gath[3].reshape(NDEV, D)

    mod_part = ada_fwd(c_all, ada_w, "ada_fwd")
    n_ada = ada_w.shape[2]
    mod_g = all_gather([mod_part.reshape(4 * NDEV, n_ada)], "gather_mod")[0].reshape(NDEV, 4, NDEV, n_ada)
    mod_mine = lax.dynamic_index_in_dim(mod_g, me, axis=2, keepdims=False)
    mod_all = jnp.moveaxis(mod_mine, 0, 1).reshape(4, NDEV * n_ada) + ada_b

    loss, dx, g = device_step(x[0], mod_all, norm_w, W, final_norm_w, loss_target[0], plan)
    loss = lax.psum(loss, MESH_AXES)

    g_small = dict(g, ada_b=g["dmod"])
    sp = _pack_small([g_small[n] for n in SMALL])
    sp_all = all_gather([sp], "gather_small")[0]
    sw, sm, sv = (_pack_small([T[n] for n in SMALL]) for T in (W, M, V))
    sg, sd, snm, snv = (_unpack(t, small_shapes, 128) for t in reduce_adam(sp_all, sw, sm, sv, sp.shape[0], "adam_small"))

    off_b = 0
    for n, shp in zip(SMALL, small_shapes):
        if n == "ada_b":
            break
        cnt = 1
        for d in shp:
            cnt *= d
        off_b += cnt + ((-cnt) % 128)
    dmod_all = sp_all.reshape(NDEV, -1)[:, off_b:off_b + 4 * 3 * D].reshape(NDEV, 4, 3 * D)
    dmod_cols = lax.dynamic_slice_in_dim(dmod_all, me * n_ada, n_ada, axis=2)
    g_ada = ada_grad(c_all, jnp.moveaxis(dmod_cols, 0, 1), "ada_grad")
    r_ada = reduce_adam(g_ada.reshape(1, 4 * D, n_ada), *(T["ada_w"].reshape(4 * D, n_ada) for T in (W, M, V)), 512, "adam_ada")
    ag, ad, anm, anv = (t.reshape(ada_w.shape) for t in r_ada)

    big = {}
    for n in BIG:
        C = W[n].shape[-1]
        parts = plan.recv[(n, None)] if n == "a_conv_w" else jnp.stack([plan.recv[(n, 0)], plan.recv[(n, 1)]], axis=1).reshape(NDEV, -1, C)
        res = reduce_adam(parts, *(T[n].reshape(parts.shape[1:]) for T in (W, M, V)), min(256, parts.shape[1]), f"adam_{n}")
        big[n] = [t.reshape(W[n].shape) for t in res]

    outs = {}
    for idx, (k, sm_l, ada_t) in enumerate((("grad", sg, ag), ("delta", sd, ad), ("new_m", snm, anm), ("new_v", snv, anv))):
        d = dict(zip(SMALL, sm_l))
        d.update({n: big[n][idx] for n in BIG})
        d["ada_w"] = ada_t
        outs[k] = d
    order = ("norm_w", "ada_w", "ada_b", "a_w_in", "a_conv_w", "a_A_log", "a_dt_bias", "a_norm_w", "a_w_out", "b_w_in", "b_f_bias",
             "b_qn_w", "b_kn_w", "b_w_out", "final_norm_w")
    return (loss, dx[None], *[outs["grad"][n] for n in order], *[outs["delta"][n] for n in order],
            *[outs["new_m"][n] for n in order], *[outs["new_v"][n] for n in order])
```

```python
import functools

import jax
import jax.numpy as jnp
from jax import lax
from jax.experimental import pallas as pl
from jax.experimental.pallas import tpu as pltpu

f32 = jnp.float32
bf16 = jnp.bfloat16
SDS = jax.ShapeDtypeStruct

EPS = 1e-6
CHUNK = 64
HD = 128
GDN_QK_HEADS = 8
GDN_V_HEADS = 16
GDN_QK_W = GDN_QK_HEADS * HD
GDN_V_W = GDN_V_HEADS * HD
GDN_CONV = 2 * GDN_QK_W + GDN_V_W
GDN_IN = GDN_CONV + GDN_V_W + 2 * GDN_V_HEADS
GDN_IN_PAD = GDN_CONV + GDN_V_W + 256
GDN_TN = 1280
FOX_H = 16
FOX_D = 64
FOX_W = FOX_H * FOX_D
FOX_IN = 4 * FOX_W + FOX_H
FOX_IN_PAD = 4 * FOX_W + 128
FOX_TN = 1408
FOX_PW = FOX_H * 128
NDEV = 8
MESH_AXES = ("x", "y", "c")
NEG = -1e30

ADAM_LR = 0.001
ADAM_B1 = 0.9
ADAM_B2 = 0.999
ADAM_EPS = 1e-08
ADAM_WD = 0.01
ADAM_STEP = 10

VMEM_LIMIT = 56 * 1024 * 1024


def _cp(sem=None):
    return pltpu.CompilerParams(dimension_semantics=sem, vmem_limit_bytes=VMEM_LIMIT)


def _bdot(a, b, dims):
    return lax.dot_general(a.astype(bf16), b.astype(bf16), (dims, ((), ())), preferred_element_type=f32)


def _nn(a, b):
    return _bdot(a, b, ((1,), (0,)))


def _nt(a, b):
    return _bdot(a, b, ((1,), (1,)))


def _tn(a, b):
    return _bdot(a, b, ((0,), (0,)))


def _hdot(a, b, dims=((1,), (0,))):
    return lax.dot_general(a, b, (dims, ((), ())), precision=lax.Precision.HIGHEST, preferred_element_type=f32)


def _split2(a):
    hi = a.astype(bf16)
    return hi, (a - hi.astype(f32)).astype(bf16)


def _dot3(a, b):
    (ah, al), (bh, bl) = a, b
    n = ah.shape[0]
    both = jnp.dot(jnp.concatenate([ah, al], axis=0), bh, preferred_element_type=f32)
    return both[:n] + both[n:] + jnp.dot(ah, bl, preferred_element_type=f32)


@jax.custom_vjp
def _mm(a, b):
    return _nn(a, b)


_mm.defvjp(lambda a, b: (_nn(a, b), (a, b)), lambda r, g: (_nt(g, r[1]), _tn(r[0], g)))


@jax.custom_vjp
def _mm_nt(a, b):
    return _nt(a, b)


_mm_nt.defvjp(lambda a, b: (_nt(a, b), (a, b)), lambda r, g: (_nn(g, r[1]), _tn(g, r[0])))


@jax.custom_vjp
def _mm_tn(a, b):
    return _tn(a, b)


_mm_tn.defvjp(lambda a, b: (_tn(a, b), (a, b)), lambda r, g: (_nt(r[1], g), _nn(r[0], g)))


def _silu(x):
    return x * jax.nn.sigmoid(x)


def _rms_mod(x, nw, scale, shift):
    r = lax.rsqrt(jnp.mean(x * x, axis=-1, keepdims=True) + EPS)
    return (x * r * nw) * (1.0 + scale) + shift


def _rows(S, want):
    return min(want, S)


def _my_pos():
    return lax.axis_index("x"), lax.axis_index("y"), lax.axis_index("c")


def _exchange_copies(kind, x_refs, out_refs, send_sems, recv_sems, local_sems):
    x_, y_, c_ = _my_pos()
    me = 4 * x_ + 2 * y_ + c_
    own = kind == "gather"
    cps = [pltpu.make_async_copy(x_refs[a] if own else x_refs[a].at[me], out_refs[a].at[me], local_sems.at[a])
           for a in range(len(x_refs))]
    for rel in range(1, NDEV):
        px = (x_ + ((rel >> 2) & 1)) % 2
        py = (y_ + ((rel >> 1) & 1)) % 2
        pc = (c_ + (rel & 1)) % 2
        for a in range(len(x_refs)):
            cps.append(pltpu.make_async_remote_copy(
                src_ref=x_refs[a] if own else x_refs[a].at[4 * px + 2 * py + pc], dst_ref=out_refs[a].at[me],
                send_sem=send_sems.at[rel - 1, a], recv_sem=recv_sems.at[rel - 1, a],
                device_id=(px, py, pc), device_id_type=pl.DeviceIdType.MESH))
    return cps


def _exchange_scratch(n):
    return [pltpu.SemaphoreType.DMA((NDEV - 1, n)), pltpu.SemaphoreType.DMA((NDEV - 1, n)), pltpu.SemaphoreType.DMA((n,))]


def _call(body, *, name, grid, in_specs, out_specs, out_shape, args, scratch=(), side=None):
    params = _cp(("arbitrary",) * len(grid))
    if side is None:
        return pl.pallas_call(body, name=name, grid=grid, in_specs=in_specs, out_specs=out_specs, out_shape=out_shape,
                              scratch_shapes=list(scratch), compiler_params=params)(*args)
    kind, xs = side
    n_in, n_out, n_scr, ns = len(in_specs), len(out_shape), len(scratch), len(xs)
    steps = 1
    for g in grid:
        steps *= g

    def wrapped(*refs):
        o0 = n_in + ns
        s0 = o0 + n_out + ns
        step = pl.program_id(0)
        for d in range(1, len(grid)):
            step = step * grid[d] + pl.program_id(d)

        def copies():
            return _exchange_copies(kind, refs[n_in:o0], refs[o0 + n_out:s0], *refs[s0 + n_scr:])

        @pl.when(step == 0)
        def _():
            for cp in copies():
                cp.start()

        body(*refs[:n_in], *refs[o0:o0 + n_out], *refs[s0:s0 + n_scr])

        @pl.when(step == steps - 1)
        def _():
            for cp in copies():
                cp.wait()

    any_ = pl.BlockSpec(memory_space=pl.ANY)
    side_shapes = [SDS((NDEV,) + x.shape if kind == "gather" else x.shape, x.dtype) for x in xs]
    outs = pl.pallas_call(wrapped, name=name, grid=grid, in_specs=list(in_specs) + [any_] * ns,
                          out_specs=list(out_specs) + [any_] * ns, out_shape=list(out_shape) + side_shapes,
                          scratch_shapes=list(scratch) + _exchange_scratch(ns), compiler_params=params)(*args, *xs)
    return outs[:n_out], outs[n_out:]


def inproj_fwd(x, nw, scale, shift, W, tn, name, side=None):
    S, D = x.shape
    N = W.shape[1]
    tm = _rows(S, 1024)

    def body(x_ref, nw_ref, sc_ref, sh_ref, w_ref, proj_ref, h_ref):
        @pl.when(pl.program_id(1) == 0)
        def _():
            h_ref[...] = _rms_mod(x_ref[...], nw_ref[...], sc_ref[...], sh_ref[...]).astype(bf16)

        proj_ref[...] = jnp.dot(h_ref[...], w_ref[...], preferred_element_type=f32)

    vec = pl.BlockSpec((1, D), lambda i, j: (0, 0))
    return _call(
        body, name=name, grid=(S // tm, N // tn),
        in_specs=[pl.BlockSpec((tm, D), lambda i, j: (i, 0)), vec, vec, vec, pl.BlockSpec((D, tn), lambda i, j: (0, j))],
        out_specs=[pl.BlockSpec((tm, tn), lambda i, j: (i, j)), pl.BlockSpec((tm, D), lambda i, j: (i, 0))],
        out_shape=[SDS((S, N), f32), SDS((S, D), bf16)], args=(x, nw, scale, shift, W), side=side)


def inproj_bwd_x(x, nw, scale, shift, W, dproj, dx_res, tn, name, side=None):
    S, D = x.shape
    N = W.shape[1]
    tm = _rows(S, 1024)
    nj = N // tn

    def body(x_ref, nw_ref, sc_ref, sh_ref, w_ref, dp_ref, dxr_ref, dx_ref, dnw_ref, dsc_ref, dsh_ref, acc):
        i, j = pl.program_id(0), pl.program_id(1)

        @pl.when(j == 0)
        def _():
            acc[...] = jnp.zeros_like(acc)

        @pl.when((i == 0) & (j == 0))
        def _():
            dnw_ref[...] = jnp.zeros_like(dnw_ref)
            dsc_ref[...] = jnp.zeros_like(dsc_ref)
            dsh_ref[...] = jnp.zeros_like(dsh_ref)

        acc[...] += _nt(dp_ref[...], w_ref[...])

        @pl.when(j == nj - 1)
        def _():
            _, vjp = jax.vjp(_rms_mod, x_ref[...], nw_ref[...], sc_ref[...], sh_ref[...])
            dx, dnw, dsc, dsh = vjp(acc[...])
            dx_ref[...] = dxr_ref[...] + dx
            dnw_ref[...] += dnw
            dsc_ref[...] += dsc
            dsh_ref[...] += dsh

    vec = pl.BlockSpec((1, D), lambda i, j: (0, 0))
    row = pl.BlockSpec((tm, D), lambda i, j: (i, 0))
    return _call(
        body, name=name, grid=(S // tm, nj),
        in_specs=[row, vec, vec, vec, pl.BlockSpec((D, tn), lambda i, j: (0, j)), pl.BlockSpec((tm, tn), lambda i, j: (i, j)), row],
        out_specs=[row, vec, vec, vec],
        out_shape=[SDS((S, D), f32), SDS((1, D), f32), SDS((1, D), f32), SDS((1, D), f32)],
        scratch=[pltpu.VMEM((tm, D), f32)], args=(x, nw, scale, shift, W, dproj, dx_res), side=side)


def matmul_tn(a, b, tn, name, side=None):
    S, K = a.shape
    N = b.shape[1]
    tm = _rows(S, 1024)
    ni = S // tm

    def body(a_ref, b_ref, o_ref):
        @pl.when(pl.program_id(1) == 0)
        def _():
            o_ref[...] = jnp.zeros_like(o_ref)

        o_ref[...] += _tn(a_ref[...], b_ref[...])

    return _call(
        body, name=name, grid=(N // tn, ni),
        in_specs=[pl.BlockSpec((tm, K), lambda j, i: (i, 0)), pl.BlockSpec((tm, tn), lambda j, i: (i, j))],
        out_specs=[pl.BlockSpec((K, tn), lambda j, i: (0, j))],
        out_shape=[SDS((K, N), f32)], args=(a, b), side=side)


def _conv_taps(xs, w, n_out):
    taps = []
    for j in range(4):
        s = 3 - j
        sh = xs if s == 0 else pltpu.roll(xs, s, axis=0)
        taps.append(sh[8:8 + n_out])
    conv = taps[0] * w[0] + taps[1] * w[1] + taps[2] * w[2] + taps[3] * w[3]
    return taps, conv


def _act_norm(conv, mul):
    s = _silu(conv)
    return s * (mul * lax.rsqrt(jnp.sum(s * s, axis=-1, keepdims=True) + EPS))


def gdn_prep_fwd(proj, conv_w, name):
    S = proj.shape[0]
    R = _rows(S, 512)

    def body(x_ref, w_ref, o_ref):
        j = pl.program_id(0)
        w = [w_ref[t:t + 1, :] for t in range(4)]

        def sweep(act):
            def piece(r, c):
                t0 = pl.multiple_of(r * R, R)
                cur = x_ref[pl.ds(t0, R), :]
                prev = x_ref[pl.ds(pl.multiple_of(jnp.maximum(t0 - 8, 0), 8), 8), :]
                prev = jnp.where(r == 0, 0.0, prev)
                _, conv = _conv_taps(jnp.concatenate([prev, cur], axis=0), w, R)
                o_ref[pl.ds(t0, R), :] = act(conv)
                return c

            lax.fori_loop(0, S // R, piece, 0)

        @pl.when(j < 2 * GDN_QK_HEADS)
        def _():
            sweep(lambda c: _act_norm(c, jnp.where(j < GDN_QK_HEADS, HD ** -0.5, 1.0).astype(f32)))

        @pl.when(j >= 2 * GDN_QK_HEADS)
        def _():
            sweep(_silu)

    return pl.pallas_call(
        body, name=name, grid=(GDN_CONV // 128,),
        in_specs=[pl.BlockSpec((S, 128), lambda j: (0, j)), pl.BlockSpec((4, 128), lambda j: (0, j))],
        out_specs=pl.BlockSpec((S, 128), lambda j: (0, j)),
        out_shape=SDS((S, GDN_CONV), f32),
        compiler_params=_cp(("arbitrary",)),
    )(proj, conv_w)


def gdn_prep_bwd(proj, conv_w, dqkvc, dproj, name):
    S = proj.shape[0]
    R = _rows(S, 512)
    NP = S // R

    def body(x_ref, w_ref, dn_ref, _, dx_ref, dw_ref):
        jb = pl.program_id(0)
        w = [w_ref[j:j + 1, :] for j in range(4)]

        def piece(act, r, dw):
            t0 = pl.multiple_of(r * R, R)
            cur = x_ref[pl.ds(t0, R), :]
            prev = x_ref[pl.ds(pl.multiple_of(jnp.maximum(t0 - 8, 0), 8), 8), :]
            prev = jnp.where(r == 0, 0.0, prev)
            nxt0 = pl.multiple_of(jnp.minimum(t0 + R, S - 8), 8)
            nxt = x_ref[pl.ds(nxt0, 8), :]
            dn_cur = dn_ref[pl.ds(t0, R), :]
            dn_nxt = jnp.where(r == NP - 1, 0.0, dn_ref[pl.ds(nxt0, 8), :])
            xs = jnp.concatenate([prev, cur, nxt], axis=0)
            taps, conv = _conv_taps(xs, w, R + 8)
            dn = jnp.concatenate([dn_cur, dn_nxt], axis=0)
            _, vjp = jax.vjp(act, conv)
            dxc = vjp(dn)[0]
            n = R + 8
            dx = dxc[0:R] * w[3]
            for j in range(3):
                s = 3 - j
                dx = dx + pltpu.roll(dxc, n - s, axis=0)[0:R] * w[j]
            dx_ref[pl.ds(t0, R), :] = dx
            return tuple(dw[j] + jnp.sum(dxc[0:R] * taps[j][0:R], axis=0, keepdims=True) for j in range(4))

        def sweep(act):
            dw = lax.fori_loop(0, NP, functools.partial(piece, act), tuple(jnp.zeros((1, 128), f32) for _ in range(4)))
            for j in range(4):
                dw_ref[j:j + 1, :] = dw[j]

        @pl.when(jb < 2 * GDN_QK_HEADS)
        def _():
            sweep(lambda c: _act_norm(c, jnp.where(jb < GDN_QK_HEADS, HD ** -0.5, 1.0).astype(f32)))

        @pl.when(jb >= 2 * GDN_QK_HEADS)
        def _():
            sweep(_silu)

    col = pl.BlockSpec((S, 128), lambda j: (0, j))
    wsp = pl.BlockSpec((4, 128), lambda j: (0, j))
    return pl.pallas_call(
        body, name=name, grid=(GDN_CONV // 128,),
        in_specs=[col, wsp, col, pl.BlockSpec(memory_space=pl.ANY)], out_specs=[col, wsp],
        out_shape=[SDS(dproj.shape, f32), SDS((4, GDN_CONV), f32)],
        input_output_aliases={3: 0},
        compiler_params=_cp(("arbitrary",)),
    )(proj, conv_w, dqkvc, dproj)


def _chunk_tril(R):
    ii = lax.broadcasted_iota(jnp.int32, (R, R), 0)
    jj = lax.broadcasted_iota(jnp.int32, (R, R), 1)
    return ((ii // CHUNK == jj // CHUNK) & (ii >= jj)).astype(f32)


def _gdn_gates(b, a, A_log, dt_bias, tril):
    beta = jax.nn.sigmoid(b)
    g = -jnp.exp(A_log) * jax.nn.softplus(a + dt_bias)
    return _hdot(tril, g), beta


_GDN_B_BLK = (GDN_CONV + GDN_V_W) // 128
_GDN_A_BLK = _GDN_B_BLK + 1


def gdn_gates_fwd(proj, A_log, dt_bias, name):
    S = proj.shape[0]
    R = _rows(S, 512)

    def body(b_ref, a_ref, al_ref, dt_ref, gc_ref, be_ref):
        gc, be = _gdn_gates(b_ref[...], a_ref[...], al_ref[...], dt_ref[...], _chunk_tril(R))
        gc_ref[...] = gc
        be_ref[...] = be

    vec = pl.BlockSpec((1, 128), lambda i: (0, 0))
    blk = pl.BlockSpec((R, 128), lambda i: (i, 0))
    return pl.pallas_call(
        body, name=name, grid=(S // R,),
        in_specs=[pl.BlockSpec((R, 128), lambda i: (i, _GDN_B_BLK)), pl.BlockSpec((R, 128), lambda i: (i, _GDN_A_BLK)), vec, vec],
        out_specs=[blk, blk], out_shape=[SDS((S, 128), f32), SDS((S, 128), f32)],
        compiler_params=_cp(("arbitrary",)),
    )(proj, proj, A_log, dt_bias)


def gdn_gates_bwd(proj, A_log, dt_bias, dgc, dbeta, dproj, name):
    S = proj.shape[0]
    R = _rows(S, 512)

    def body(b_ref, a_ref, al_ref, dt_ref, dgc_ref, dbe_ref, _, dp_ref, dal_ref, ddt_ref):
        @pl.when(pl.program_id(0) == 0)
        def _():
            dal_ref[...] = jnp.zeros_like(dal_ref)
            ddt_ref[...] = jnp.zeros_like(ddt_ref)

        tril = _chunk_tril(R)
        _, vjp = jax.vjp(lambda b, a, al, dt: _gdn_gates(b, a, al, dt, tril), b_ref[...], a_ref[...], al_ref[...], dt_ref[...])
        db, da, dal, ddt = vjp((dgc_ref[...], dbe_ref[...]))
        dp_ref[:, 0:128] = db
        dp_ref[:, 128:256] = da
        dal_ref[...] += dal
        ddt_ref[...] += ddt

    vec = pl.BlockSpec((1, 128), lambda i: (0, 0))
    blk = pl.BlockSpec((R, 128), lambda i: (i, 0))
    return pl.pallas_call(
        body, name=name, grid=(S // R,),
        in_specs=[pl.BlockSpec((R, 128), lambda i: (i, _GDN_B_BLK)), pl.BlockSpec((R, 128), lambda i: (i, _GDN_A_BLK)), vec, vec, blk, blk,
                  pl.BlockSpec(memory_space=pl.ANY)],
        out_specs=[pl.BlockSpec((R, 256), lambda i: (i, _GDN_B_BLK // 2)), vec, vec],
        out_shape=[SDS(dproj.shape, f32), SDS((1, 128), f32), SDS((1, 128), f32)],
        input_output_aliases={6: 0},
        compiler_params=_cp(("arbitrary",)),
    )(proj, proj, A_log, dt_bias, dgc, dbeta, dproj)


@jax.custom_vjp
def _inv_given(L, T):
    return T


def _inv_given_bwd(T, ct):
    dL = -_nt(_tn(T, ct), T)
    return dL, jnp.zeros_like(T)


_inv_given.defvjp(lambda L, T: (T, T), _inv_given_bwd)


REP = GDN_V_HEADS // GDN_QK_HEADS


def _gdn_intra_all(qs, ks, vs, gcols, bcols, Ts=None):
    H = len(vs)
    C = vs[0].shape[0]
    ii = lax.broadcasted_iota(jnp.int32, (C, C), 0)
    jj = lax.broadcasted_iota(jnp.int32, (C, C), 1)
    grows = [jnp.sum(jnp.where(ii == jj, g, 0.0), axis=0, keepdims=True) for g in gcols]
    decs = [jnp.exp(jnp.where(ii >= jj, gcols[h] - grows[h], NEG)) for h in range(H)]
    kbs = [ks[h // REP] * bcols[h] for h in range(H)]
    As = [_mm_nt(kbs[h], ks[h // REP]) for h in range(H)]
    Ls = [jnp.where(ii > jj, As[h] * decs[h], 0.0) for h in range(H)]
    if Ts is None:
        T = _neumann_inv_batched(Ls)
    else:
        T = [_inv_given(Ls[h], Ts[h]) for h in range(H)]
    us = [_mm(T[h], vs[h] * bcols[h]) for h in range(H)]
    ws = [_mm(T[h], kbs[h] * jnp.exp(gcols[h])) for h in range(H)]
    qk = [_mm_nt(qs[p], ks[p]) for p in range(H // REP)]
    return us, ws, [qk[h // REP] * decs[h] for h in range(H)], T


def _neumann_inv_batched(Ls):
    n, C = 4, Ls[0].shape[0]
    r0 = lax.broadcasted_iota(jnp.int32, (n * C, n * C), 0)
    c0 = lax.broadcasted_iota(jnp.int32, (n * C, n * C), 1)
    same = (r0 // C) == (c0 // C)

    def blockdiag(split):
        return tuple(jnp.where(same, jnp.concatenate([x] * n, axis=0), jnp.zeros((), bf16)) for x in split)

    Ms = [jnp.concatenate(Ls[b:b + n], axis=1) for b in range(0, len(Ls), n)]
    eye = (lax.broadcasted_iota(jnp.int32, (C, n * C), 0) == (lax.broadcasted_iota(jnp.int32, (C, n * C), 1) & (C - 1))).astype(f32)
    Ps = [eye - M for M in Ms]
    Ss = [_split2(M) for M in Ms]
    Bs = [blockdiag(S) for S in Ss]
    k = 1
    while 2 * k < C:
        Ss = [_split2(_dot3(S, B)) for S, B in zip(Ss, Bs)]
        Bs = [blockdiag(S) for S in Ss]
        Ps = [P + _dot3(_split2(P), B) for P, B in zip(Ps, Bs)]
        k *= 2
    return [P[:, h * C:(h + 1) * C] for P in Ps for h in range(n)]


def _gdn_scan_all(qs, ks, gcols, us, ws, attns, S0s):
    H = len(us)
    C = us[0].shape[0]
    last = lax.broadcasted_iota(jnp.int32, (C, 1), 0) == C - 1
    glast = [jnp.sum(jnp.where(last, g, 0.0), axis=0, keepdims=True) for g in gcols]
    wS = [_mm(ws[h], S0s[h]) for h in range(H)]
    qS = [_mm(qs[h // REP] * jnp.exp(gcols[h]), S0s[h]) for h in range(H)]
    vn = [us[h] - wS[h] for h in range(H)]
    av = [_mm(attns[h], vn[h]) for h in range(H)]
    kv = [_mm_tn(ks[h // REP] * jnp.exp(glast[h] - gcols[h]), vn[h]) for h in range(H)]
    return [qS[h] + av[h] for h in range(H)], [S0s[h] * jnp.exp(glast[h]) + kv[h] for h in range(H)]


def _head_cols(blk):
    lane = lax.broadcasted_iota(jnp.int32, blk.shape, 1)
    return [jnp.sum(jnp.where(lane == h, blk, 0.0), axis=1, keepdims=True) for h in range(GDN_V_HEADS)]


def _head_lanes(cols):
    lane = lax.broadcasted_iota(jnp.int32, (cols[0].shape[0], 128), 1)
    out = jnp.zeros((cols[0].shape[0], 128), f32)
    for h, c in enumerate(cols):
        out = out + jnp.where(lane == h, c, 0.0)
    return out


CPS = 2


def _rows_of(c):
    return slice(c * CHUNK, (c + 1) * CHUNK)


def _heads(ref, n, c):
    return [ref[_rows_of(c), h * HD:(h + 1) * HD].astype(f32) for h in range(n)]


def _mats(ref, c):
    return [ref[c, h].astype(f32) for h in range(GDN_V_HEADS)]


def _gdn_specs(NB, rv=None):
    ix = (lambda n: n) if rv is None else rv
    R = CPS * CHUNK
    qs = pl.BlockSpec((R, GDN_QK_W), lambda n: (ix(n), 0))
    ks = pl.BlockSpec((R, GDN_QK_W), lambda n: (ix(n), 1))
    vs = pl.BlockSpec((R, GDN_V_W), lambda n: (ix(n), 1))
    g1 = pl.BlockSpec((R, 128), lambda n: (ix(n), 0))
    wide = pl.BlockSpec((R, GDN_V_W), lambda n: (ix(n), 0))
    sq = pl.BlockSpec((CPS, GDN_V_HEADS, CHUNK, CHUNK), lambda n: (ix(n), 0, 0, 0))
    st = pl.BlockSpec((CPS, GDN_V_HEADS, HD, HD), lambda n: (ix(n), 0, 0, 0))
    return qs, ks, vs, g1, wide, sq, st


def gdn_intra_fwd(qkvc, gc, beta, name, side=None):
    S = qkvc.shape[0]
    NC = S // CHUNK

    def body(q_ref, k_ref, v_ref, gc_ref, be_ref, u_ref, w_ref, at_ref, T_ref):
        for c in range(CPS):
            rows = _rows_of(c)
            us, ws, attns, Ts = _gdn_intra_all(_heads(q_ref, GDN_QK_HEADS, c), _heads(k_ref, GDN_QK_HEADS, c), _heads(v_ref, GDN_V_HEADS, c),
                                               _head_cols(gc_ref[rows, :]), _head_cols(be_ref[rows, :]))
            for h in range(GDN_V_HEADS):
                u_ref[rows, h * HD:(h + 1) * HD] = us[h]
                w_ref[rows, h * HD:(h + 1) * HD] = ws[h].astype(bf16)
                at_ref[c, h] = attns[h].astype(bf16)
                T_ref[c, h] = Ts[h].astype(bf16)

    qs, ks, vs, g1, wide, sq, _ = _gdn_specs(NC // CPS)
    return _call(
        body, name=name, grid=(NC // CPS,),
        in_specs=[qs, ks, vs, g1, g1], out_specs=[wide, wide, sq, sq],
        out_shape=[SDS((S, GDN_V_W), f32), SDS((S, GDN_V_W), bf16),
                   SDS((NC, GDN_V_HEADS, CHUNK, CHUNK), bf16), SDS((NC, GDN_V_HEADS, CHUNK, CHUNK), bf16)],
        args=(qkvc, qkvc, qkvc, gc, beta), side=side)


def gdn_scan_fwd(qkvc, gc, u, w, attn, name, side=None):
    S = qkvc.shape[0]
    NC = S // CHUNK

    def body(q_ref, k_ref, gc_ref, u_ref, w_ref, at_ref, o_ref, st_ref, state):
        @pl.when(pl.program_id(0) == 0)
        def _():
            state[...] = jnp.zeros_like(state)

        for c in range(CPS):
            rows = _rows_of(c)
            S0s = [state[h] for h in range(GDN_V_HEADS)]
            os_, S1s = _gdn_scan_all(_heads(q_ref, GDN_QK_HEADS, c), _heads(k_ref, GDN_QK_HEADS, c), _head_cols(gc_ref[rows, :]),
                                     _heads(u_ref, GDN_V_HEADS, c), _heads(w_ref, GDN_V_HEADS, c), _mats(at_ref, c), S0s)
            for h in range(GDN_V_HEADS):
                o_ref[rows, h * HD:(h + 1) * HD] = os_[h]
                st_ref[c, h] = S0s[h].astype(bf16)
                state[h] = S1s[h]

    qs, ks, _, g1, wide, sq, st = _gdn_specs(NC // CPS)
    return _call(
        body, name=name, grid=(NC // CPS,),
        in_specs=[qs, ks, g1, wide, wide, sq], out_specs=[wide, st],
        out_shape=[SDS((S, GDN_V_W), f32), SDS((NC, GDN_V_HEADS, HD, HD), bf16)],
        scratch=[pltpu.VMEM((GDN_V_HEADS, HD, HD), f32)], args=(qkvc, qkvc, gc, u, w, attn), side=side)


def gdn_scan_bwd(qkvc, gc, u, w, attn, states, do, name, side=None):
    S = qkvc.shape[0]
    NC = S // CHUNK
    NB = NC // CPS

    def body(q_ref, k_ref, gc_ref, u_ref, w_ref, at_ref, st_ref, do_ref,
             dq_ref, dk_ref, dgc_ref, du_ref, dw_ref, dat_ref, dstate):
        @pl.when(pl.program_id(0) == 0)
        def _():
            dstate[...] = jnp.zeros_like(dstate)

        VH = range(GDN_V_HEADS)
        for c in reversed(range(CPS)):
            rows = _rows_of(c)
            _, vjp = jax.vjp(_gdn_scan_all, _heads(q_ref, GDN_QK_HEADS, c), _heads(k_ref, GDN_QK_HEADS, c), _head_cols(gc_ref[rows, :]),
                             _heads(u_ref, GDN_V_HEADS, c), _heads(w_ref, GDN_V_HEADS, c), _mats(at_ref, c), _mats(st_ref, c))
            dqs, dks, dgs, dus, dws, dats, dS0s = vjp((_heads(do_ref, GDN_V_HEADS, c), [dstate[h] for h in VH]))
            for p in range(GDN_QK_HEADS):
                dq_ref[rows, p * HD:(p + 1) * HD] = dqs[p]
                dk_ref[rows, p * HD:(p + 1) * HD] = dks[p]
            for h in VH:
                du_ref[rows, h * HD:(h + 1) * HD] = dus[h].astype(bf16)
                dw_ref[rows, h * HD:(h + 1) * HD] = dws[h].astype(bf16)
                dat_ref[c, h] = dats[h].astype(bf16)
                dstate[h] = dS0s[h]
            dgc_ref[rows, :] = _head_lanes(dgs)

    qs, ks, _, g1, wide, sq, st = _gdn_specs(NB, lambda n: NB - 1 - n)
    dqs = pl.BlockSpec((CPS * CHUNK, GDN_QK_W), lambda n: (NB - 1 - n, 0))
    return _call(
        body, name=name, grid=(NB,),
        in_specs=[qs, ks, g1, wide, wide, sq, st, wide],
        out_specs=[dqs, dqs, g1, wide, wide, sq],
        out_shape=[SDS((S, GDN_QK_W), f32), SDS((S, GDN_QK_W), f32), SDS((S, 128), f32), SDS((S, GDN_V_W), bf16),
                   SDS((S, GDN_V_W), bf16), SDS((NC, GDN_V_HEADS, CHUNK, CHUNK), bf16)],
        scratch=[pltpu.VMEM((GDN_V_HEADS, HD, HD), f32)], args=(qkvc, qkvc, gc, u, w, attn, states, do), side=side)


def gdn_intra_bwd(qkvc, gc, beta, Ts, du, dw, dattn, dq_s, dk_s, dgc_s, name, side=None):
    S = qkvc.shape[0]
    NC = S // CHUNK

    def body(q_ref, k_ref, v_ref, gc_ref, be_ref, T_ref, du_ref, dw_ref, dat_ref, dqs_ref, dks_ref, dgs_ref,
             dqkv_ref, dgc_ref, dbe_ref):
        VH = range(GDN_V_HEADS)
        for c in range(CPS):
            rows = _rows_of(c)
            Ts = _mats(T_ref, c)
            _, vjp = jax.vjp(lambda q_, k_, v_, g_, b_: _gdn_intra_all(q_, k_, v_, g_, b_, Ts)[:3],
                             _heads(q_ref, GDN_QK_HEADS, c), _heads(k_ref, GDN_QK_HEADS, c), _heads(v_ref, GDN_V_HEADS, c),
                             _head_cols(gc_ref[rows, :]), _head_cols(be_ref[rows, :]))
            dqs, dks, dvs, dgs, dbs = vjp((_heads(du_ref, GDN_V_HEADS, c), _heads(dw_ref, GDN_V_HEADS, c), _mats(dat_ref, c)))
            for p in range(GDN_QK_HEADS):
                dqkv_ref[rows, p * HD:(p + 1) * HD] = dqs[p] + dqs_ref[rows, p * HD:(p + 1) * HD]
                dqkv_ref[rows, GDN_QK_W + p * HD:GDN_QK_W + (p + 1) * HD] = dks[p] + dks_ref[rows, p * HD:(p + 1) * HD]
            for h in VH:
                dqkv_ref[rows, 2 * GDN_QK_W + h * HD:2 * GDN_QK_W + (h + 1) * HD] = dvs[h]
            dgc_ref[rows, :] = _head_lanes(dgs) + dgs_ref[rows, :]
            dbe_ref[rows, :] = _head_lanes(dbs)

    qs, ks, vs, g1, wide, sq, _ = _gdn_specs(NC // CPS)
    dqs = pl.BlockSpec((CPS * CHUNK, GDN_QK_W), lambda n: (n, 0))
    return _call(
        body, name=name, grid=(NC // CPS,),
        in_specs=[qs, ks, vs, g1, g1, sq, wide, wide, sq, dqs, dqs, g1],
        out_specs=[pl.BlockSpec((CPS * CHUNK, GDN_CONV), lambda n: (n, 0)), g1, g1],
        out_shape=[SDS((S, GDN_CONV), f32), SDS((S, 128), f32), SDS((S, 128), f32)],
        args=(qkvc, qkvc, qkvc, gc, beta, Ts, du, dw, dattn, dq_s, dk_s, dgc_s), side=side)


def _gated_norm(o, z, nw):
    parts = []
    for h in range(GDN_V_HEADS):
        oh = o[:, h * HD:(h + 1) * HD]
        r = lax.rsqrt(jnp.mean(oh * oh, axis=-1, keepdims=True) + EPS)
        parts.append((oh * r * nw) * _silu(z[:, h * HD:(h + 1) * HD]))
    return jnp.concatenate(parts, axis=1)


def gdn_out_fwd(o, proj, nw, W, x, gate, name):
    S, D = x.shape
    tm = _rows(S, 256)

    def body(o_ref, z_ref, nw_ref, w_ref, x_ref, g_ref, xn_ref, y_ref, og_ref):
        og = _gated_norm(o_ref[...], z_ref[...], nw_ref[...]).astype(bf16)
        y = jnp.dot(og, w_ref[...], preferred_element_type=f32)
        og_ref[...] = og
        y_ref[...] = y
        xn_ref[...] = x_ref[...] + g_ref[...] * y

    row = pl.BlockSpec((tm, D), lambda i: (i, 0))
    wide = pl.BlockSpec((tm, GDN_V_W), lambda i: (i, 0))
    return pl.pallas_call(
        body, name=name, grid=(S // tm,),
        in_specs=[wide, pl.BlockSpec((tm, GDN_V_W), lambda i: (i, 2)), pl.BlockSpec((1, HD), lambda i: (0, 0)),
                  pl.BlockSpec((GDN_V_W, D), lambda i: (0, 0)), row, pl.BlockSpec((1, D), lambda i: (0, 0))],
        out_specs=[row, row, wide],
        out_shape=[SDS((S, D), f32), SDS((S, D), f32), SDS((S, GDN_V_W), bf16)],
        compiler_params=_cp(("arbitrary",)),
    )(o, proj, nw, W, x, gate)


def gdn_out_bwd(dxn, y, gate, o, proj, nw, W, name):
    S, D = dxn.shape
    tm = _rows(S, 256)

    def body(dx_ref, y_ref, g_ref, o_ref, z_ref, nw_ref, w_ref, dy_ref, dg_ref, do_ref, dz_ref, dnw_ref):
        @pl.when(pl.program_id(0) == 0)
        def _():
            dg_ref[...] = jnp.zeros_like(dg_ref)
            dnw_ref[...] = jnp.zeros_like(dnw_ref)

        dx = dx_ref[...]
        dy = dx * g_ref[...]
        dy_ref[...] = dy
        dg_ref[...] += jnp.sum(dx * y_ref[...], axis=0, keepdims=True)
        dog = _nt(dy, w_ref[...])
        _, vjp = jax.vjp(_gated_norm, o_ref[...], z_ref[...], nw_ref[...])
        do, dz, dnw = vjp(dog)
        do_ref[...] = do
        dz_ref[...] = dz
        dnw_ref[...] += dnw

    row = pl.BlockSpec((tm, D), lambda i: (i, 0))
    wide = pl.BlockSpec((tm, GDN_V_W), lambda i: (i, 0))
    vecd = pl.BlockSpec((1, D), lambda i: (0, 0))
    vech = pl.BlockSpec((1, HD), lambda i: (0, 0))
    return pl.pallas_call(
        body, name=name, grid=(S // tm,),
        in_specs=[row, row, vecd, wide, pl.BlockSpec((tm, GDN_V_W), lambda i: (i, 2)), vech, pl.BlockSpec((GDN_V_W, D), lambda i: (0, 0))],
        out_specs=[row, vecd, wide, pl.BlockSpec((tm, GDN_V_W), lambda i: (i, 2)), vech],
        out_shape=[SDS((S, D), f32), SDS((1, D), f32), SDS((S, GDN_V_W), f32), SDS((S, GDN_IN_PAD), f32), SDS((1, HD), f32)],
        compiler_params=_cp(("arbitrary",)),
    )(dxn, y, gate, o, proj, nw, W)


def _rms_w(x, w):
    return (x * lax.rsqrt(jnp.mean(x * x, axis=-1, keepdims=True) + EPS)) * w


def _split3(c):
    hi = c.astype(bf16).astype(f32)
    r1 = c - hi
    mid = r1.astype(bf16).astype(f32)
    lo = (r1 - mid).astype(bf16).astype(f32)
    return hi, mid, lo


_FOX_F_BLK = 4 * FOX_W // 128


def fox_prep_fwd(proj, f_bias, qn_w, kn_w, name):
    S = proj.shape[0]
    tm = _rows(S, 256)

    def body(q_ref, k_ref, v_ref, f_ref, fb_ref, qw_ref, kw_ref, Q_ref, K_ref, V_ref, carry):
        @pl.when(pl.program_id(0) == 0)
        def _():
            carry[...] = jnp.zeros_like(carry)

        ii = lax.broadcasted_iota(jnp.int32, (tm, tm), 0)
        jj = lax.broadcasted_iota(jnp.int32, (tm, tm), 1)
        lf = jax.nn.log_sigmoid(f_ref[...] + fb_ref[...])
        cum = _hdot((ii >= jj).astype(f32), lf) + carry[...]
        carry[...] = cum[tm - 1:tm, :]
        lane = lax.broadcasted_iota(jnp.int32, (tm, 128), 1)
        lo = lane < FOX_D
        qw2 = jnp.concatenate([qw_ref[...], qw_ref[...]], axis=1) * FOX_D ** -0.5
        kw2 = jnp.concatenate([kw_ref[...], kw_ref[...]], axis=1)

        def norm_pair(x, w2):
            x2 = x * x
            s_all = jnp.sum(x2, axis=1, keepdims=True)
            s_lo = jnp.sum(jnp.where(lo, x2, 0.0), axis=1, keepdims=True)
            r = jnp.where(lo, lax.rsqrt(s_lo * (1.0 / FOX_D) + EPS), lax.rsqrt((s_all - s_lo) * (1.0 / FOX_D) + EPS))
            return x * r * w2

        for p in range(FOX_H // 2):
            ps = slice(p * 128, (p + 1) * 128)
            yq, yk, xv = norm_pair(q_ref[:, ps], qw2), norm_pair(k_ref[:, ps], kw2), v_ref[:, ps]
            for e in range(2):
                h = 2 * p + e
                hi, mid, lw = _split3(cum[:, h:h + 1])
                eq = jnp.where(lane == FOX_D, hi, jnp.where(lane == FOX_D + 1, mid, jnp.where(lane == FOX_D + 2, lw, jnp.where(lane < FOX_D + 6, 1.0, 0.0))))
                ek = jnp.where(lane < FOX_D + 3, 1.0, jnp.where(lane == FOX_D + 3, -hi, jnp.where(lane == FOX_D + 4, -mid, jnp.where(lane == FOX_D + 5, -lw, 0.0))))
                ev = jnp.where(lane == FOX_D, 1.0, 0.0)
                mv = (lambda a: a) if e == 0 else (lambda a: pltpu.roll(a, FOX_D, axis=1))
                Q_ref[:, h * 128:(h + 1) * 128] = jnp.where(lo, mv(yq), eq).astype(bf16)
                K_ref[:, h * 128:(h + 1) * 128] = jnp.where(lo, mv(yk), ek).astype(bf16)
                V_ref[:, h * 128:(h + 1) * 128] = jnp.where(lo, mv(xv), ev).astype(bf16)

    def colblk(c):
        return pl.BlockSpec((tm, FOX_W), lambda i: (i, c))

    pad = pl.BlockSpec((tm, FOX_PW), lambda i: (i, 0))
    return pl.pallas_call(
        body, name=name, grid=(S // tm,),
        in_specs=[colblk(0), colblk(1), colblk(2), pl.BlockSpec((tm, 128), lambda i: (i, _FOX_F_BLK)),
                  pl.BlockSpec((1, 128), lambda i: (0, 0)), pl.BlockSpec((1, FOX_D), lambda i: (0, 0)), pl.BlockSpec((1, FOX_D), lambda i: (0, 0))],
        out_specs=[pad, pad, pad],
        out_shape=[SDS((S, FOX_PW), bf16)] * 3,
        scratch_shapes=[pltpu.VMEM((1, 128), f32)],
        compiler_params=_cp(("arbitrary",)),
    )(proj, proj, proj, proj, f_bias, qn_w, kn_w)


def fox_prep_bwd(proj, f_bias, qn_w, kn_w, dQ, dK, dV, dz, name):
    S = proj.shape[0]
    tm = _rows(S, 256)
    NB = S // tm

    def body(q_ref, k_ref, f_ref, fb_ref, qw_ref, kw_ref, dQ_ref, dK_ref, dV_ref, dz_ref,
             dp_ref, dfb_ref, dqw_ref, dkw_ref, carry):
        @pl.when(pl.program_id(0) == 0)
        def _():
            carry[...] = jnp.zeros_like(carry)
            dfb_ref[...] = jnp.zeros_like(dfb_ref)
            dqw_ref[...] = jnp.zeros_like(dqw_ref)
            dkw_ref[...] = jnp.zeros_like(dkw_ref)

        lane = lax.broadcasted_iota(jnp.int32, (tm, 128), 1)
        lo = lane < FOX_D
        qw2 = jnp.concatenate([qw_ref[...], qw_ref[...]], axis=1) * FOX_D ** -0.5
        kw2 = jnp.concatenate([kw_ref[...], kw_ref[...]], axis=1)

        def pair(ref, p):
            return jnp.where(lo, ref[:, 2 * p * 128:(2 * p + 1) * 128], pltpu.roll(ref[:, (2 * p + 1) * 128:(2 * p + 2) * 128], FOX_D, axis=1))

        def norm_pair_bwd(x, w2, dy):
            x2 = x * x
            s_all = jnp.sum(x2, axis=1, keepdims=True)
            s_lo = jnp.sum(jnp.where(lo, x2, 0.0), axis=1, keepdims=True)
            r = jnp.where(lo, lax.rsqrt(s_lo * (1.0 / FOX_D) + EPS), lax.rsqrt((s_all - s_lo) * (1.0 / FOX_D) + EPS))
            t = dy * w2 * x
            t_all = jnp.sum(t, axis=1, keepdims=True)
            t_lo = jnp.sum(jnp.where(lo, t, 0.0), axis=1, keepdims=True)
            dx = r * (w2 * dy - x * (r * r) * (jnp.where(lo, t_lo, t_all - t_lo) * (1.0 / FOX_D)))
            return dx, jnp.sum(dy * x * r, axis=0, keepdims=True)

        dcum = jnp.zeros((tm, 128), f32)
        dqw2 = jnp.zeros((1, 128), f32)
        dkw2 = jnp.zeros((1, 128), f32)
        for p in range(FOX_H // 2):
            ps = slice(p * 128, (p + 1) * 128)
            dxq, dw1 = norm_pair_bwd(q_ref[:, ps], qw2, pair(dQ_ref, p))
            dxk, dw2 = norm_pair_bwd(k_ref[:, ps], kw2, pair(dK_ref, p))
            dp_ref[:, p * 128:(p + 1) * 128] = dxq
            dp_ref[:, FOX_W + p * 128:FOX_W + (p + 1) * 128] = dxk
            dp_ref[:, 2 * FOX_W + p * 128:2 * FOX_W + (p + 1) * 128] = pair(dV_ref, p)
            dqw2 = dqw2 + dw1
            dkw2 = dkw2 + dw2
            for e in range(2):
                h = 2 * p + e
                dcum = dcum + jnp.where(lane == h, dQ_ref[:, h * 128 + FOX_D:h * 128 + FOX_D + 1]
                                        - dK_ref[:, h * 128 + FOX_D + 3:h * 128 + FOX_D + 4], 0.0)
        dp_ref[:, 3 * FOX_W:4 * FOX_W] = dz_ref[...]
        ii = lax.broadcasted_iota(jnp.int32, (tm, tm), 0)
        jj = lax.broadcasted_iota(jnp.int32, (tm, tm), 1)
        dlf = _hdot((ii <= jj).astype(f32), dcum) + carry[...]
        carry[...] += jnp.sum(dcum, axis=0, keepdims=True)
        df = dlf * jax.nn.sigmoid(-(f_ref[...] + fb_ref[...]))
        dp_ref[:, 4 * FOX_W:FOX_IN_PAD] = df
        dfb_ref[...] += jnp.sum(df, axis=0, keepdims=True)
        dqw_ref[...] += (dqw2[:, :FOX_D] + dqw2[:, FOX_D:]) * FOX_D ** -0.5
        dkw_ref[...] += dkw2[:, :FOX_D] + dkw2[:, FOX_D:]

    rv = lambda i: NB - 1 - i

    def colblk(c):
        return pl.BlockSpec((tm, FOX_W), lambda i: (rv(i), c))

    pad = pl.BlockSpec((tm, FOX_PW), lambda i: (rv(i), 0))
    cmp_ = pl.BlockSpec((tm, FOX_W), lambda i: (rv(i), 0))
    v128 = pl.BlockSpec((1, 128), lambda i: (0, 0))
    v64 = pl.BlockSpec((1, FOX_D), lambda i: (0, 0))
    return pl.pallas_call(
        body, name=name, grid=(NB,),
        in_specs=[colblk(0), colblk(1), pl.BlockSpec((tm, 128), lambda i: (rv(i), _FOX_F_BLK)), v128, v64, v64, pad, pad, pad, cmp_],
        out_specs=[pl.BlockSpec((tm, FOX_IN_PAD), lambda i: (rv(i), 0)), v128, v64, v64],
        out_shape=[SDS((S, FOX_IN_PAD), f32), SDS((1, 128), f32), SDS((1, FOX_D), f32), SDS((1, FOX_D), f32)],
        scratch_shapes=[pltpu.VMEM((1, 128), f32)],
        compiler_params=_cp(("arbitrary",)),
    )(proj, proj, proj, f_bias, qn_w, kn_w, dQ, dK, dV, dz)


FOX_HB = 2


def _diag_mask(t):
    return lax.broadcasted_iota(jnp.int32, (t, t), 1) <= lax.broadcasted_iota(jnp.int32, (t, t), 0)


def fox_attn_fwd(Q, K, V, name, side=None):
    S = Q.shape[0]
    t = _rows(S, 512)

    HB = FOX_HB
    HS = [slice(h * 128, (h + 1) * 128) for h in range(HB)]

    def body(q_ref, k_ref, v_ref, o_ref, m_sc, acc_sc, s_sc):
        i = pl.program_id(1)
        qs = [q_ref[:, sl] for sl in HS]
        m_sc[...] = jnp.full_like(m_sc, NEG)
        acc_sc[...] = jnp.zeros_like(acc_sc)

        def scores(j):
            j0 = pl.multiple_of(j * t, t)
            return [_nt(qs[h], k_ref[pl.ds(j0, t), HS[h]]) for h in range(HB)]

        def tile(j, diag):
            j0 = pl.multiple_of(j * t, t)
            ss = [s_sc[h] for h in range(HB)]
            if diag:
                ss = [jnp.where(_diag_mask(t), s, NEG) for s in ss]
            else:
                nxt = scores(j + 1)
            ms = [m_sc[h] for h in range(HB)]
            m_new = [jnp.maximum(ms[h], jnp.max(ss[h], axis=1, keepdims=True)) for h in range(HB)]
            ps = [jnp.exp(ss[h] - m_new[h]) for h in range(HB)]
            pv = [_nn(ps[h], v_ref[pl.ds(j0, t), HS[h]]) for h in range(HB)]
            for h in range(HB):
                acc_sc[h] = acc_sc[h] * jnp.exp(ms[h] - m_new[h]) + pv[h]
                m_sc[h] = m_new[h]
                if not diag:
                    s_sc[h] = nxt[h]

        def off_diag(j, c):
            tile(j, False)
            return c

        first = scores(0)
        for h in range(HB):
            s_sc[h] = first[h]
        lax.fori_loop(0, i, off_diag, 0)
        tile(i, True)
        lane = lax.broadcasted_iota(jnp.int32, (t, 128), 1)
        for h in range(HB):
            acc = acc_sc[h]
            l = acc[:, FOX_D:FOX_D + 1]
            o_ref[:, HS[h]] = jnp.where(lane == FOX_D, m_sc[h] + jnp.log(l), acc / l)

    blk = pl.BlockSpec((t, HB * 128), lambda h, i: (i, h))
    seq = pl.BlockSpec((S, HB * 128), lambda h, i: (0, h))
    return _call(
        body, name=name, grid=(FOX_H // HB, S // t),
        in_specs=[blk, seq, seq], out_specs=[blk], out_shape=[SDS((S, FOX_PW), f32)],
        scratch=[pltpu.VMEM((HB, t, 1), f32), pltpu.VMEM((HB, t, 128), f32), pltpu.VMEM((HB, t, t), f32)],
        args=(Q, K, V), side=side)


def fox_attn_bwd(Q, K, V, dO, O, name, side=None):
    S = Q.shape[0]
    t = _rows(S, 512)
    nq = S // t

    HB = FOX_HB
    HS = [slice(h * 128, (h + 1) * 128) for h in range(HB)]

    def body(k_ref, v_ref, q_ref, do_ref, o_ref, dq_ref, dk_ref, dv_ref):
        j = pl.program_id(1)

        @pl.when(j == 0)
        def _():
            dq_ref[...] = jnp.zeros_like(dq_ref)

        dk_ref[...] = jnp.zeros_like(dk_ref)
        dv_ref[...] = jnp.zeros_like(dv_ref)
        ks = [k_ref[:, sl] for sl in HS]
        vs = [v_ref[:, sl] for sl in HS]

        def tile(i, diag):
            i0 = pl.multiple_of(i * t, t)
            R = range(HB)
            qs = [q_ref[pl.ds(i0, t), HS[h]] for h in R]
            dos = [do_ref[pl.ds(i0, t), HS[h]] for h in R]
            ss = [_nt(qs[h], ks[h]) - o_ref[pl.ds(i0, t), h * 128 + FOX_D:h * 128 + FOX_D + 1] for h in R]
            if diag:
                ss = [jnp.where(_diag_mask(t), s, NEG) for s in ss]
            ps = [jnp.exp(s) for s in ss]
            dps = [_nt(dos[h], vs[h]) for h in R]
            dvs = [_tn(ps[h], dos[h]) for h in R]
            dss = [(ps[h] * dps[h]).astype(bf16) for h in R]
            dks = [_tn(dss[h], qs[h]) for h in R]
            dqs = [_nn(dss[h], ks[h]) for h in R]
            for h in R:
                dv_ref[:, HS[h]] += dvs[h]
                dk_ref[:, HS[h]] += dks[h]
                dq_ref[pl.ds(i0, t), HS[h]] += dqs[h]

        tile(j, True)

        def off_diag(i, c):
            tile(i, False)
            return c

        lax.fori_loop(j + 1, nq, off_diag, 0)

    blk = pl.BlockSpec((t, HB * 128), lambda h, j: (j, h))
    seq = pl.BlockSpec((S, HB * 128), lambda h, j: (0, h))
    return _call(
        body, name=name, grid=(FOX_H // HB, nq),
        in_specs=[blk, blk, seq, seq, seq], out_specs=[seq, blk, blk],
        out_shape=[SDS((S, FOX_PW), f32)] * 3, args=(K, V, Q, dO, O), side=side)


def fox_out_fwd(O, proj, W, x, gate, name):
    S, D = x.shape
    tm = _rows(S, 256)

    def body(o_ref, z_ref, w_ref, x_ref, g_ref, xn_ref, y_ref, og_ref):
        z = z_ref[...]
        og = jnp.concatenate([o_ref[:, h * 128:h * 128 + FOX_D] * _silu(z[:, h * FOX_D:(h + 1) * FOX_D]) for h in range(FOX_H)],
                             axis=1).astype(bf16)
        y = jnp.dot(og, w_ref[...], preferred_element_type=f32)
        og_ref[...] = og
        y_ref[...] = y
        xn_ref[...] = x_ref[...] + g_ref[...] * y

    row = pl.BlockSpec((tm, D), lambda i: (i, 0))
    cmp_ = pl.BlockSpec((tm, FOX_W), lambda i: (i, 0))
    return pl.pallas_call(
        body, name=name, grid=(S // tm,),
        in_specs=[pl.BlockSpec((tm, FOX_PW), lambda i: (i, 0)), pl.BlockSpec((tm, FOX_W), lambda i: (i, 3)),
                  pl.BlockSpec((FOX_W, D), lambda i: (0, 0)), row, pl.BlockSpec((1, D), lambda i: (0, 0))],
        out_specs=[row, row, cmp_],
        out_shape=[SDS((S, D), f32), SDS((S, D), f32), SDS((S, FOX_W), bf16)],
        compiler_params=_cp(("arbitrary",)),
    )(O, proj, W, x, gate)


def fox_out_bwd(dxn, y, gate, O, proj, W, name):
    S, D = dxn.shape
    tm = _rows(S, 256)

    def body(dx_ref, y_ref, g_ref, o_ref, z_ref, w_ref, dy_ref, dg_ref, dO_ref, dz_ref):
        @pl.when(pl.program_id(0) == 0)
        def _():
            dg_ref[...] = jnp.zeros_like(dg_ref)

        dx = dx_ref[...]
        dy = dx * g_ref[...]
        dy_ref[...] = dy
        dg_ref[...] += jnp.sum(dx * y_ref[...], axis=0, keepdims=True)
        dog = _nt(dy, w_ref[...])
        z = z_ref[...]
        lane = lax.broadcasted_iota(jnp.int32, (tm, FOX_D), 1)
        dzs = []
        for h in range(FOX_H):
            sl = slice(h * FOX_D, (h + 1) * FOX_D)
            zh = z[:, sl]
            sg = jax.nn.sigmoid(zh)
            oh = o_ref[:, h * 128:h * 128 + FOX_D]
            doh = dog[:, sl] * (zh * sg)
            delta = jnp.sum(doh * oh, axis=1, keepdims=True)
            dO_ref[:, h * 128:(h + 1) * 128] = jnp.concatenate([doh, jnp.where(lane == 0, -delta, 0.0)], axis=1).astype(bf16)
            dzs.append(dog[:, sl] * oh * (sg * (1.0 + zh * (1.0 - sg))))
        dz_ref[...] = jnp.concatenate(dzs, axis=1)

    row = pl.BlockSpec((tm, D), lambda i: (i, 0))
    vecd = pl.BlockSpec((1, D), lambda i: (0, 0))
    pad = pl.BlockSpec((tm, FOX_PW), lambda i: (i, 0))
    return pl.pallas_call(
        body, name=name, grid=(S // tm,),
        in_specs=[row, row, vecd, pad, pl.BlockSpec((tm, FOX_W), lambda i: (i, 3)), pl.BlockSpec((FOX_W, D), lambda i: (0, 0))],
        out_specs=[row, vecd, pad, pl.BlockSpec((tm, FOX_W), lambda i: (i, 0))],
        out_shape=[SDS((S, D), f32), SDS((1, D), f32), SDS((S, FOX_PW), bf16), SDS((S, FOX_W), f32)],
        compiler_params=_cp(("arbitrary",)),
    )(dxn, y, gate, O, proj, W)


def final_loss(x, fw, target, name):
    S, D = x.shape
    tm = _rows(S, 512)

    def body(x_ref, w_ref, t_ref, l_ref, dx_ref, dw_ref):
        @pl.when(pl.program_id(0) == 0)
        def _():
            l_ref[...] = jnp.zeros_like(l_ref)
            dw_ref[...] = jnp.zeros_like(dw_ref)

        out, vjp = jax.vjp(_rms_w, x_ref[...], w_ref[...])
        err = out - t_ref[...]
        l_ref[...] += 0.5 * jnp.sum(jnp.sum(err * err, axis=1, keepdims=True) * (1.0 / D), axis=0, keepdims=True)
        dx, dw = vjp(err * (1.0 / D))
        dx_ref[...] = dx
        dw_ref[...] += dw

    row = pl.BlockSpec((tm, D), lambda i: (i, 0))
    vec = pl.BlockSpec((1, D), lambda i: (0, 0))
    return pl.pallas_call(
        body, name=name, grid=(S // tm,),
        in_specs=[row, vec, row], out_specs=[pl.BlockSpec((1, 128), lambda i: (0, 0)), row, vec],
        out_shape=[SDS((1, 128), f32), SDS((S, D), f32), SDS((1, D), f32)],
        compiler_params=_cp(("arbitrary",)),
    )(x, fw, target)


def ada_fwd(c_all, ada_w, name):
    L, D, n = ada_w.shape

    def body(c_ref, w_ref, o_ref):
        cond = jnp.concatenate([_silu(c_ref[...]), jnp.zeros((8, D), f32)], axis=0)
        o_ref[0] = _nn(cond, w_ref[0])[0:8]

    return pl.pallas_call(
        body, name=name, grid=(L,),
        in_specs=[pl.BlockSpec((NDEV, D), lambda l: (0, 0)), pl.BlockSpec((1, D, n), lambda l: (l, 0, 0))],
        out_specs=pl.BlockSpec((1, NDEV, n), lambda l: (l, 0, 0)),
        out_shape=SDS((L, NDEV, n), f32),
        compiler_params=_cp(("arbitrary",)),
    )(c_all, ada_w)


def ada_grad(c_all, dmod, name):
    L, _, n = dmod.shape
    D = c_all.shape[1]

    def body(c_ref, d_ref, o_ref):
        cond = jnp.concatenate([_silu(c_ref[...]), jnp.zeros((8, D), f32)], axis=0)
        dm = jnp.concatenate([d_ref[0], jnp.zeros((8, n), f32)], axis=0)
        o_ref[0] = _tn(cond, dm)

    return pl.pallas_call(
        body, name=name, grid=(L,),
        in_specs=[pl.BlockSpec((NDEV, D), lambda l: (0, 0)), pl.BlockSpec((1, NDEV, n), lambda l: (l, 0, 0))],
        out_specs=pl.BlockSpec((1, D, n), lambda l: (l, 0, 0)),
        out_shape=SDS((L, D, n), f32),
        compiler_params=_cp(("arbitrary",)),
    )(c_all, dmod)


def reduce_adam(parts, w, m, v, tr, name):
    n, R, C = parts.shape
    c1 = 1.0 / (1.0 - ADAM_B1 ** ADAM_STEP)
    c2 = 1.0 / (1.0 - ADAM_B2 ** ADAM_STEP)

    def body(p_ref, w_ref, m_ref, v_ref, g_ref, d_ref, nm_ref, nv_ref):
        g = p_ref[0].astype(f32)
        for s in range(1, n):
            g = g + p_ref[s].astype(f32)
        nm = ADAM_B1 * m_ref[...] + (1.0 - ADAM_B1) * g
        nv = ADAM_B2 * v_ref[...] + (1.0 - ADAM_B2) * (g * g)
        g_ref[...] = g
        nm_ref[...] = nm
        nv_ref[...] = nv
        d_ref[...] = -ADAM_LR * ((nm * c1) / (jnp.sqrt(nv * c2) + ADAM_EPS) + ADAM_WD * w_ref[...])

    blk = pl.BlockSpec((tr, C), lambda i: (i, 0))
    return pl.pallas_call(
        body, name=name, grid=(R // tr,),
        in_specs=[pl.BlockSpec((n, tr, C), lambda i: (0, i, 0)), blk, blk, blk],
        out_specs=[blk] * 4, out_shape=[SDS((R, C), f32)] * 4,
        compiler_params=_cp(("arbitrary",)),
    )(parts, w, m, v)


def all_gather(xs, name):
    n = len(xs)

    def body(*refs):
        x_refs, out_refs = refs[:n], refs[n:2 * n]
        send_sems, recv_sems, local_sems = refs[2 * n:]
        x_, y_, c_ = _my_pos()
        me, sibling = (x_, y_, c_), (x_, y_, 1 - c_)
        chips = [(1 - x_, y_), (x_, 1 - y_), (1 - x_, 1 - y_)]

        def rows(a, px, py, pc):
            return out_refs[a].at[4 * px + 2 * py + pc]

        def copy(a, k, block, to, own=False):
            return pltpu.make_async_remote_copy(
                src_ref=x_refs[a] if own else rows(a, *block), dst_ref=rows(a, *block),
                send_sem=send_sems.at[k, a], recv_sem=recv_sems.at[k, a], device_id=to, device_id_type=pl.DeviceIdType.MESH)

        mine = [pltpu.make_async_copy(x_refs[a], rows(a, *me), local_sems.at[a]) for a in range(n)]
        for cp in mine:
            cp.start()
        first = []
        for a in range(n):
            first.append(copy(a, 0, me, sibling, own=True))
            first += [copy(a, 1 + j, me, (*chip, c_), own=True) for j, chip in enumerate(chips)]
        for cp in first:
            cp.start()
        passed = []
        for j, chip in enumerate(chips):
            for a in range(n):
                copy(a, 1 + j, (*chip, c_), me).wait_recv()
                cp = copy(a, 4 + j, (*chip, c_), sibling)
                cp.start()
                passed.append(cp)
        for a in range(n):
            copy(a, 0, sibling, me).wait_recv()
            for j, chip in enumerate(chips):
                copy(a, 4 + j, (*chip, 1 - c_), me).wait_recv()
        for cp in first + passed:
            cp.wait_send()
        for cp in mine:
            cp.wait()

    any_ = pl.BlockSpec(memory_space=pl.ANY)
    return pl.pallas_call(
        body, name=name, out_shape=[SDS((NDEV,) + x.shape, x.dtype) for x in xs],
        in_specs=[any_] * n, out_specs=[any_] * n,
        scratch_shapes=[pltpu.SemaphoreType.DMA((7, n)), pltpu.SemaphoreType.DMA((7, n)), pltpu.SemaphoreType.DMA((n,))],
    )(*xs)


GDN_COLS = ((0, GDN_CONV + GDN_V_W, 0), (GDN_CONV + GDN_V_W, GDN_CONV + GDN_V_W + 16, GDN_CONV + GDN_V_W),
            (GDN_CONV + GDN_V_W + 16, GDN_IN, GDN_CONV + GDN_V_W + 128))
FOX_COLS = ((0, FOX_IN, 0),)


def _col_pieces(d, per, cols):
    lo, hi = per * d, per * (d + 1)
    out = []
    for a, b, dst in cols:
        s, e = max(lo, a), min(hi, b)
        if s < e:
            out.append((s - lo, e - s, dst + s - a))
    return out


def cols_from_blocks(g, cols, n_out, name):
    _, L, R, C = g.shape
    tr = min(256, R)

    def body(g_ref, o_ref):
        o_ref[...] = jnp.zeros_like(o_ref)
        for d in range(NDEV):
            for off, ln, dst in _col_pieces(d, C, cols):
                o_ref[0, :, dst:dst + ln] = g_ref[d, 0, :, off:off + ln]

    return pl.pallas_call(
        body, name=name, grid=(L, R // tr),
        in_specs=[pl.BlockSpec((NDEV, 1, tr, C), lambda l, i: (0, l, i, 0))],
        out_specs=pl.BlockSpec((1, tr, n_out), lambda l, i: (l, i, 0)),
        out_shape=SDS((L, R, n_out), g.dtype),
        compiler_params=_cp(("arbitrary", "arbitrary")),
    )(g)


def blocks_from_cols(dw, C, cols, name):
    R, n_in = dw.shape
    tr = min(256, R)

    def body(x_ref, o_ref):
        for d in range(NDEV):
            for off, ln, src in _col_pieces(d, C, cols):
                o_ref[d, :, off:off + ln] = x_ref[:, src:src + ln].astype(bf16)

    return pl.pallas_call(
        body, name=name, grid=(R // tr,),
        in_specs=[pl.BlockSpec((tr, n_in), lambda i: (i, 0))],
        out_specs=pl.BlockSpec((NDEV, tr, C), lambda i: (0, i, 0)),
        out_shape=SDS((NDEV, R, C), bf16),
        compiler_params=_cp(("arbitrary",)),
    )(dw)


BIG = ("a_w_in", "a_conv_w", "a_w_out", "b_w_in", "b_w_out")
SMALL = ("norm_w", "ada_b", "a_A_log", "a_dt_bias", "a_norm_w", "b_f_bias", "b_qn_w", "b_kn_w", "final_norm_w")


def _pack_small(arrs):
    rows = []
    for a in arrs:
        fl = a.reshape(-1)
        pad = (-fl.shape[0]) % 128
        if pad:
            fl = jnp.concatenate([fl, jnp.zeros((pad,), fl.dtype)])
        rows.append(fl)
    flat = jnp.concatenate(rows)
    pad = (-flat.shape[0]) % (8 * 128)
    if pad:
        flat = jnp.concatenate([flat, jnp.zeros((pad,), flat.dtype)])
    return flat.reshape(-1, 128)


def _unpack(packed, shapes, align):
    flat = packed.reshape(-1)
    out, off = [], 0
    for shp in shapes:
        n = 1
        for d in shp:
            n *= d
        out.append(flat[off:off + n].reshape(shp))
        off += n + ((-n) % align)
    return out


def _full_from_gathered(g, shard_shape, axis):
    g = jnp.moveaxis(g, 0, axis)
    shp = list(shard_shape)
    shp[axis] *= NDEV
    return g.reshape(shp)


def _pad_lanes(v, n=128):
    v = v.reshape(1, -1)
    return jnp.concatenate([v, jnp.zeros((1, n - v.shape[1]), v.dtype)], axis=1)


def _carried(fn, *args, side=None, **grads):
    if callable(side):
        side = side(**grads)
    res = fn(*args, side)
    return res if side is not None else (res, None)


def gdn_layer_fwd(x, mod, nw, weights, tag, sides):
    W_in, conv_w, A_log, dt_bias, a_nw, W_out = weights
    shift, scale, gate = mod
    got = {}
    (proj, h), got["inproj"] = _carried(inproj_fwd, x, nw, scale, shift, W_in, GDN_TN, f"{tag}_inproj", side=sides.get("inproj"))
    qkvc = gdn_prep_fwd(proj, conv_w, f"{tag}_prep")
    gc, beta = gdn_gates_fwd(proj, A_log, dt_bias, f"{tag}_gates")
    (u, w, attn, Ts), got["intra"] = _carried(gdn_intra_fwd, qkvc, gc, beta, f"{tag}_intra", side=sides.get("intra"))
    (o, states), got["scan"] = _carried(gdn_scan_fwd, qkvc, gc, u, w, attn, f"{tag}_scan", side=sides.get("scan"))
    x_new, y, og = gdn_out_fwd(o, proj, a_nw, W_out, x, gate, f"{tag}_out")
    return x_new, (x, proj, h, qkvc, gc, beta, o, states, Ts, y, og, u, w, attn), got


def gdn_layer_bwd(dxn, saved, mod, nw, weights, tag, sides):
    W_in, conv_w, A_log, dt_bias, a_nw, W_out = weights
    shift, scale, gate = mod
    x, proj, h, qkvc, gc, beta, o, states, Ts, y, og, u, w, attn = saved
    got = {}
    dy, dgate, do, dproj, da_nw = gdn_out_bwd(dxn, y, gate, o, proj, a_nw, W_out, f"{tag}_out_bwd")
    dW_out, = matmul_tn(og, dy, 512, f"{tag}_dwout")
    (dq_s, dk_s, dgc_s, du, dw, dattn), got["sbwd"] = _carried(
        gdn_scan_bwd, qkvc, gc, u, w, attn, states, do, f"{tag}_scan_bwd", side=sides.get("sbwd"), dW_out=dW_out)
    (dqkvc, dgc, dbeta), got["intrab"] = _carried(
        gdn_intra_bwd, qkvc, gc, beta, Ts, du, dw, dattn, dq_s, dk_s, dgc_s, f"{tag}_intra_bwd", side=sides.get("intrab"), dW_out=dW_out)
    dproj, dA_log, ddt = gdn_gates_bwd(proj, A_log, dt_bias, dgc, dbeta, dproj, f"{tag}_gates_bwd")
    dproj, dconv_w = gdn_prep_bwd(proj, conv_w, dqkvc, dproj, f"{tag}_prep_bwd")
    (dW_in,), got["dwin"] = _carried(matmul_tn, h, dproj, GDN_TN, f"{tag}_dwin", side=sides.get("dwin"), dW_out=dW_out)
    (dx, dnw, dscale, dshift), got["ibwd"] = _carried(
        inproj_bwd_x, x, nw, scale, shift, W_in, dproj, dxn, GDN_TN, f"{tag}_inproj_bwd", side=sides.get("ibwd"),
        dW_out=dW_out, dW_in=dW_in, dconv_w=dconv_w)
    grads = dict(norm_w=dnw, W_in=dW_in, conv_w=dconv_w, A_log=dA_log[:, :16], dt_bias=ddt[:, :16], a_nw=da_nw, W_out=dW_out,
                 dmod=jnp.concatenate([dshift, dscale, dgate], axis=1))
    return dx, grads, got


def fox_layer_fwd(x, mod, nw, weights, tag, sides):
    W_in, f_bias, qn_w, kn_w, W_out = weights
    shift, scale, gate = mod
    got = {}
    (proj, h), got["inproj"] = _carried(inproj_fwd, x, nw, scale, shift, W_in, FOX_TN, f"{tag}_inproj", side=sides.get("inproj"))
    Q, K, V = fox_prep_fwd(proj, f_bias, qn_w, kn_w, f"{tag}_prep")
    (O,), got["attn"] = _carried(fox_attn_fwd, Q, K, V, f"{tag}_attn", side=sides.get("attn"))
    x_new, y, og = fox_out_fwd(O, proj, W_out, x, gate, f"{tag}_out")
    return x_new, (x, proj, h, Q, K, V, O, y, og), got


def fox_layer_bwd(dxn, saved, mod, nw, weights, tag, sides):
    W_in, f_bias, qn_w, kn_w, W_out = weights
    shift, scale, gate = mod
    x, proj, h, Q, K, V, O, y, og = saved
    got = {}
    dy, dgate, dO, dz = fox_out_bwd(dxn, y, gate, O, proj, W_out, f"{tag}_out_bwd")
    dW_out, = matmul_tn(og, dy, 512, f"{tag}_dwout")
    (dQ, dK, dV), got["abwd"] = _carried(fox_attn_bwd, Q, K, V, dO, O, f"{tag}_attn_bwd", side=sides.get("abwd"))
    dproj, dfb, dqw, dkw = fox_prep_bwd(proj, f_bias, qn_w, kn_w, dQ, dK, dV, dz, f"{tag}_prep_bwd")
    (dW_in,), got["dwin"] = _carried(matmul_tn, h, dproj, FOX_TN, f"{tag}_dwin", side=sides.get("dwin"))
    dx, dnw, dscale, dshift = inproj_bwd_x(x, nw, scale, shift, W_in, dproj, dxn, FOX_TN, f"{tag}_inproj_bwd")
    grads = dict(norm_w=dnw, W_in=dW_in, f_bias=dfb[:, :16], qn_w=dqw, kn_w=dkw, W_out=dW_out,
                 dmod=jnp.concatenate([dshift, dscale, dgate], axis=1))
    return dx, grads, got


class LocalPlan:
    def __init__(self, full):
        self.full = full

    def layer_weights(self, i):
        j, f = i // 2, self.full
        return (f["a_w_in"][j], f["a_w_out"][j], f["a_conv_w"][j]) if i % 2 == 0 else (f["b_w_in"][j], f["b_w_out"][j])

    def fwd_sides(self, i):
        return {}

    def fwd_got(self, i, got):
        pass

    def bwd_sides(self, i):
        return {}

    def bwd_got(self, i, grads, got):
        pass


def device_step(x, mod_all, norm_w, small, final_norm_w, target, plan):
    D = x.shape[1]
    mods = [(mod_all[i:i + 1, 0:D], mod_all[i:i + 1, D:2 * D], mod_all[i:i + 1, 2 * D:3 * D]) for i in range(4)]

    def weights(i):
        j = i // 2
        if i % 2 == 0:
            W_in, W_out, conv_w = plan.layer_weights(i)
            return (W_in, conv_w, _pad_lanes(small["a_A_log"][j]), _pad_lanes(small["a_dt_bias"][j]), small["a_norm_w"][j:j + 1], W_out)
        W_in, W_out = plan.layer_weights(i)
        return (W_in, _pad_lanes(small["b_f_bias"][j]), small["b_qn_w"][j:j + 1], small["b_kn_w"][j:j + 1], W_out)

    saved, wts = [], []
    for i in range(4):
        wts.append(weights(i))
        fwd = gdn_layer_fwd if i % 2 == 0 else fox_layer_fwd
        x, sv, got = fwd(x, mods[i], norm_w[i:i + 1], wts[i], f"L{i}", plan.fwd_sides(i))
        plan.fwd_got(i, got)
        saved.append(sv)
    loss, dx, dfw = final_loss(x, final_norm_w.reshape(1, D), target, "final_loss")
    lg = [None] * 4
    for i in reversed(range(4)):
        bwd = gdn_layer_bwd if i % 2 == 0 else fox_layer_bwd
        dx, lg[i], got = bwd(dx, saved[i], mods[i], norm_w[i:i + 1], wts[i], f"L{i}", plan.bwd_sides(i))
        plan.bwd_got(i, lg[i], got)
    g = dict(
        norm_w=jnp.concatenate([lg[i]["norm_w"] for i in range(4)], axis=0),
        dmod=jnp.concatenate([lg[i]["dmod"] for i in range(4)], axis=0),
        a_w_in=[lg[i]["W_in"] for i in (0, 2)],
        a_conv_w=jnp.stack([lg[i]["conv_w"] for i in (0, 2)]),
        a_A_log=jnp.concatenate([lg[i]["A_log"] for i in (0, 2)], axis=0),
        a_dt_bias=jnp.concatenate([lg[i]["dt_bias"] for i in (0, 2)], axis=0),
        a_norm_w=jnp.concatenate([lg[i]["a_nw"] for i in (0, 2)], axis=0),
        a_w_out=[lg[i]["W_out"] for i in (0, 2)],
        b_w_in=[lg[i]["W_in"] for i in (1, 3)],
        b_f_bias=jnp.concatenate([lg[i]["f_bias"] for i in (1, 3)], axis=0),
        b_qn_w=jnp.concatenate([lg[i]["qn_w"] for i in (1, 3)], axis=0),
        b_kn_w=jnp.concatenate([lg[i]["kn_w"] for i in (1, 3)], axis=0),
        b_w_out=[lg[i]["W_out"] for i in (1, 3)],
        final_norm_w=dfw.reshape(-1),
    )
    return loss[0, 0], dx, g


class MeshPlan:
    def __init__(self, shards, w0, conv_full):
        self.shards = shards
        self.w = {0: w0}
        self.conv = conv_full
        self.recv = {}
        self.pending = {}
        self.names = {}

    def layer_weights(self, i):
        return self.w[i]

    def _gather_side(self, layer):
        names = ("a_w_in", "a_w_out") if layer % 2 == 0 else ("b_w_in", "b_w_out")
        out = []
        for n in names:
            sh = self.shards[n][layer // 2]
            out.append(sh.reshape(-1, sh.shape[-1]))
        return ("gather", out)

    def fwd_sides(self, i):
        if i == 0:
            kind, (b_in, b_out) = self._gather_side(1)
            return {"inproj": (kind, [b_out]), "scan": (kind, [b_in]), "intra": self._gather_side(2)}
        if i == 1:
            return {"attn": self._gather_side(3)}
        return {}

    def fwd_got(self, i, got):
        if got.get("inproj") is not None:
            self._b_out0, = got["inproj"]
        for key, layer in (("scan", 1), ("intra", 2), ("attn", 3)):
            if got.get(key) is None:
                continue
            g_in, g_out = (got[key][0], self._b_out0) if key == "scan" else got[key]
            D = g_out.shape[-1]
            j = layer // 2
            if layer % 2 == 1:
                W_in = cols_from_blocks(g_in[:, None], FOX_COLS, FOX_IN_PAD, f"b_w_in_cols{j}")[0]
                self.w[layer] = (W_in, g_out.reshape(-1, D))
            else:
                W_in = cols_from_blocks(g_in[:, None], GDN_COLS, GDN_IN_PAD, f"a_w_in_cols{j}")[0]
                self.w[layer] = (W_in, g_out.reshape(-1, D), self.conv[j])

    @staticmethod
    def _out_blocks(dW_out):
        return dW_out.astype(bf16).reshape(NDEV, -1, dW_out.shape[-1])

    def _in_blocks(self, name, j, dW_in):
        cols = GDN_COLS if name == "a_w_in" else FOX_COLS
        return blocks_from_cols(dW_in, self.shards[name].shape[-1], cols, f"{name}_blocks{j}")

    def bwd_sides(self, i):
        self.names = {}
        sides = {}
        for (layer, key) in [k for k in self.pending if k[0] == i]:
            self.names[key], arrs = self.pending.pop((layer, key))
            sides[key] = ("scatter", arrs)
        if i == 0:
            def sbwd(dW_out):
                self.names["sbwd"] = [("a_w_out", 0)]
                return ("scatter", [self._out_blocks(dW_out)])

            def ibwd(dW_out, dW_in, dconv_w):
                conv = jnp.stack([dconv_w, self._dconv1])
                n = conv.shape[-1] // NDEV
                self.names["ibwd"] = [("a_w_in", 0), ("a_conv_w", None)]
                return ("scatter", [self._in_blocks("a_w_in", 0, dW_in),
                                    jnp.moveaxis(conv.reshape(2, 4, NDEV, n), 2, 0).reshape(NDEV, 8, n)])

            sides["sbwd"], sides["ibwd"] = sbwd, ibwd
        return sides

    def bwd_got(self, i, grads, got):
        for key, arrs in got.items():
            if arrs is not None:
                self.recv.update(zip(self.names[key], arrs))
        j = i // 2
        if i % 2 == 1:
            self.pending[(i - 1, "sbwd" if i == 3 else "intrab")] = (
                [("b_w_in", j), ("b_w_out", j)], [self._in_blocks("b_w_in", j, grads["W_in"]), self._out_blocks(grads["W_out"])])
        elif i == 2:
            self.pending[(1, "abwd")] = (
                [("a_w_in", 1), ("a_w_out", 1)], [self._in_blocks("a_w_in", 1, grads["W_in"]), self._out_blocks(grads["W_out"])])
            self._dconv1 = grads["conv_w"]


def kernel(x, c, norm_w, ada_w, ada_b, a_w_in, a_conv_w, a_A_log, a_dt_bias, a_norm_w, a_w_out, b_w_in, b_f_bias, b_qn_w, b_kn_w, b_w_out, final_norm_w, loss_target, m_norm_w, m_ada_w, m_ada_b, m_a_w_in, m_a_conv_w, m_a_A_log, m_a_dt_bias, m_a_norm_w, m_a_w_out, m_b_w_in, m_b_f_bias, m_b_qn_w, m_b_kn_w, m_b_w_out, m_final_norm_w, v_norm_w, v_ada_w, v_ada_b, v_a_w_in, v_a_conv_w, v_a_A_log, v_a_dt_bias, v_a_norm_w, v_a_w_out, v_b_w_in, v_b_f_bias, v_b_qn_w, v_b_kn_w, v_b_w_out, v_final_norm_w):
    W = dict(norm_w=norm_w, ada_w=ada_w, ada_b=ada_b, a_w_in=a_w_in, a_conv_w=a_conv_w, a_A_log=a_A_log, a_dt_bias=a_dt_bias,
             a_norm_w=a_norm_w, a_w_out=a_w_out, b_w_in=b_w_in, b_f_bias=b_f_bias, b_qn_w=b_qn_w, b_kn_w=b_kn_w, b_w_out=b_w_out,
             final_norm_w=final_norm_w)
    M = dict(norm_w=m_norm_w, ada_w=m_ada_w, ada_b=m_ada_b, a_w_in=m_a_w_in, a_conv_w=m_a_conv_w, a_A_log=m_a_A_log,
             a_dt_bias=m_a_dt_bias, a_norm_w=m_a_norm_w, a_w_out=m_a_w_out, b_w_in=m_b_w_in, b_f_bias=m_b_f_bias, b_qn_w=m_b_qn_w,
             b_kn_w=m_b_kn_w, b_w_out=m_b_w_out, final_norm_w=m_final_norm_w)
    V = dict(norm_w=v_norm_w, ada_w=v_ada_w, ada_b=v_ada_b, a_w_in=v_a_w_in, a_conv_w=v_a_conv_w, a_A_log=v_a_A_log,
             a_dt_bias=v_a_dt_bias, a_norm_w=v_a_norm_w, a_w_out=v_a_w_out, b_w_in=v_b_w_in, b_f_bias=v_b_f_bias, b_qn_w=v_b_qn_w,
             b_kn_w=v_b_kn_w, b_w_out=v_b_w_out, final_norm_w=v_final_norm_w)
    S, D = x.shape[1], x.shape[2]
    me = 4 * lax.axis_index("x") + 2 * lax.axis_index("y") + lax.axis_index("c")
    small_shapes = [W[n].shape for n in SMALL]

    shards = {n: W[n].astype(bf16) for n in ("a_w_in", "a_w_out", "b_w_in", "b_w_out")}
    gath = all_gather([shards["a_w_in"][0], shards["a_w_out"][0], a_conv_w.reshape(8, -1), c.reshape(8, D // 8)], "gather_w0")
    conv_full = _full_from_gathered(gath[2].reshape((NDEV,) + a_conv_w.shape), a_conv_w.shape, 2)
    w0 = (cols_from_blocks(gath[0][:, None], GDN_COLS, GDN_IN_PAD, "a_w_in_cols0")[0], gath[1].reshape(-1, D), conv_full[0])
    plan = MeshPlan(shards, w0, conv_full)
    c_all = gath[3].reshape(NDEV, D)

    mod_part = ada_fwd(c_all, ada_w, "ada_fwd")
    n_ada = ada_w.shape[2]
    mod_g = all_gather([mod_part.reshape(4 * NDEV, n_ada)], "gather_mod")[0].reshape(NDEV, 4, NDEV, n_ada)
    mod_mine = lax.dynamic_index_in_dim(mod_g, me, axis=2, keepdims=False)
    mod_all = jnp.moveaxis(mod_mine, 0, 1).reshape(4, NDEV * n_ada) + ada_b

    loss, dx, g = device_step(x[0], mod_all, norm_w, W, final_norm_w, loss_target[0], plan)
    loss = lax.psum(loss, MESH_AXES)

    g_small = dict(g, ada_b=g["dmod"])
    sp = _pack_small([g_small[n] for n in SMALL])
    sp_all = all_gather([sp], "gather_small")[0]
    sw, sm, sv = (_pack_small([T[n] for n in SMALL]) for T in (W, M, V))
    sg, sd, snm, snv = (_unpack(t, small_shapes, 128) for t in reduce_adam(sp_all, sw, sm, sv, sp.shape[0], "adam_small"))

    off_b = 0
    for n, shp in zip(SMALL, small_shapes):
        if n == "ada_b":
            break
        cnt = 1
        for d in shp:
            cnt *= d
        off_b += cnt + ((-cnt) % 128)
    dmod_all = sp_all.reshape(NDEV, -1)[:, off_b:off_b + 4 * 3 * D].reshape(NDEV, 4, 3 * D)
    dmod_cols = lax.dynamic_slice_in_dim(dmod_all, me * n_ada, n_ada, axis=2)
    g_ada = ada_grad(c_all, jnp.moveaxis(dmod_cols, 0, 1), "ada_grad")
    r_ada = reduce_adam(g_ada.reshape(1, 4 * D, n_ada), *(T["ada_w"].reshape(4 * D, n_ada) for T in (W, M, V)), 512, "adam_ada")
    ag, ad, anm, anv = (t.reshape(ada_w.shape) for t in r_ada)

    big = {}
    for n in BIG:
        C = W[n].shape[-1]
        parts = plan.recv[(n, None)] if n == "a_conv_w" else jnp.stack([plan.recv[(n, 0)], plan.recv[(n, 1)]], axis=1).reshape(NDEV, -1, C)
        res = reduce_adam(parts, *(T[n].reshape(parts.shape[1:]) for T in (W, M, V)), min(256, parts.shape[1]), f"adam_{n}")
        big[n] = [t.reshape(W[n].shape) for t in res]

    outs = {}
    for idx, (k, sm_l, ada_t) in enumerate((("grad", sg, ag), ("delta", sd, ad), ("new_m", snm, anm), ("new_v", snv, anv))):
        d = dict(zip(SMALL, sm_l))
        d.update({n: big[n][idx] for n in BIG})
        d["ada_w"] = ada_t
        outs[k] = d
    order = ("norm_w", "ada_w", "ada_b", "a_w_in", "a_conv_w", "a_A_log", "a_dt_bias", "a_norm_w", "a_w_out", "b_w_in", "b_f_bias",
             "b_qn_w", "b_kn_w", "b_w_out", "final_norm_w")
    return (loss, dx[None], *[outs["grad"][n] for n in order], *[outs["delta"][n] for n in order],
            *[outs["new_m"][n] for n in order], *[outs["new_v"][n] for n in order])
```

```python
import functools

import jax
import jax.numpy as jnp
from jax import lax
from jax.experimental import pallas as pl
from jax.experimental.pallas import tpu as pltpu

f32 = jnp.float32
bf16 = jnp.bfloat16
SDS = jax.ShapeDtypeStruct

EPS = 1e-6
CHUNK = 64
HD = 128
GDN_QK_HEADS = 8
GDN_V_HEADS = 16
GDN_QK_W = GDN_QK_HEADS * HD
GDN_V_W = GDN_V_HEADS * HD
GDN_CONV = 2 * GDN_QK_W + GDN_V_W
GDN_IN = GDN_CONV + GDN_V_W + 2 * GDN_V_HEADS
GDN_IN_PAD = GDN_CONV + GDN_V_W + 256
GDN_TN = 1280
FOX_H = 16
FOX_D = 64
FOX_W = FOX_H * FOX_D
FOX_IN = 4 * FOX_W + FOX_H
FOX_IN_PAD = 4 * FOX_W + 128
FOX_TN = 1408
FOX_PW = FOX_H * 128
NDEV = 8
MESH_AXES = ("x", "y", "c")
NEG = -1e30

ADAM_LR = 0.001
ADAM_B1 = 0.9
ADAM_B2 = 0.999
ADAM_EPS = 1e-08
ADAM_WD = 0.01
ADAM_STEP = 10

VMEM_LIMIT = 56 * 1024 * 1024


def _cp(sem=None):
    return pltpu.CompilerParams(dimension_semantics=sem, vmem_limit_bytes=VMEM_LIMIT)


def _bdot(a, b, dims):
    return lax.dot_general(a.astype(bf16), b.astype(bf16), (dims, ((), ())), preferred_element_type=f32)


def _nn(a, b):
    return _bdot(a, b, ((1,), (0,)))


def _nt(a, b):
    return _bdot(a, b, ((1,), (1,)))


def _tn(a, b):
    return _bdot(a, b, ((0,), (0,)))


def _hdot(a, b, dims=((1,), (0,))):
    return lax.dot_general(a, b, (dims, ((), ())), precision=lax.Precision.HIGHEST, preferred_element_type=f32)


def _split2(a):
    hi = a.astype(bf16)
    return hi, (a - hi.astype(f32)).astype(bf16)


def _dot3(a, b):
    (ah, al), (bh, bl) = a, b
    n = ah.shape[0]
    both = jnp.dot(jnp.concatenate([ah, al], axis=0), bh, preferred_element_type=f32)
    return both[:n] + both[n:] + jnp.dot(ah, bl, preferred_element_type=f32)


@jax.custom_vjp
def _mm(a, b):
    return _nn(a, b)


_mm.defvjp(lambda a, b: (_nn(a, b), (a, b)), lambda r, g: (_nt(g, r[1]), _tn(r[0], g)))


@jax.custom_vjp
def _mm_nt(a, b):
    return _nt(a, b)


_mm_nt.defvjp(lambda a, b: (_nt(a, b), (a, b)), lambda r, g: (_nn(g, r[1]), _tn(g, r[0])))


@jax.custom_vjp
def _mm_tn(a, b):
    return _tn(a, b)


_mm_tn.defvjp(lambda a, b: (_tn(a, b), (a, b)), lambda r, g: (_nt(r[1], g), _nn(r[0], g)))


def _silu(x):
    return x * jax.nn.sigmoid(x)


def _rms_mod(x, nw, scale, shift):
    r = lax.rsqrt(jnp.mean(x * x, axis=-1, keepdims=True) + EPS)
    return (x * r * nw) * (1.0 + scale) + shift


def _rows(S, want):
    return min(want, S)


def _my_pos():
    return lax.axis_index("x"), lax.axis_index("y"), lax.axis_index("c")


def _exchange_copies(kind, x_refs, out_refs, send_sems, recv_sems, local_sems):
    x_, y_, c_ = _my_pos()
    me = 4 * x_ + 2 * y_ + c_
    own = kind == "gather"
    cps = [pltpu.make_async_copy(x_refs[a] if own else x_refs[a].at[me], out_refs[a].at[me], local_sems.at[a])
           for a in range(len(x_refs))]
    for rel in range(1, NDEV):
        px = (x_ + ((rel >> 2) & 1)) % 2
        py = (y_ + ((rel >> 1) & 1)) % 2
        pc = (c_ + (rel & 1)) % 2
        for a in range(len(x_refs)):
            cps.append(pltpu.make_async_remote_copy(
                src_ref=x_refs[a] if own else x_refs[a].at[4 * px + 2 * py + pc], dst_ref=out_refs[a].at[me],
                send_sem=send_sems.at[rel - 1, a], recv_sem=recv_sems.at[rel - 1, a],
                device_id=(px, py, pc), device_id_type=pl.DeviceIdType.MESH))
    return cps


def _exchange_scratch(n):
    return [pltpu.SemaphoreType.DMA((NDEV - 1, n)), pltpu.SemaphoreType.DMA((NDEV - 1, n)), pltpu.SemaphoreType.DMA((n,))]


def _call(body, *, name, grid, in_specs, out_specs, out_shape, args, scratch=(), side=None):
    params = _cp(("arbitrary",) * len(grid))
    if side is None:
        return pl.pallas_call(body, name=name, grid=grid, in_specs=in_specs, out_specs=out_specs, out_shape=out_shape,
                              scratch_shapes=list(scratch), compiler_params=params)(*args)
    kind, xs = side
    n_in, n_out, n_scr, ns = len(in_specs), len(out_shape), len(scratch), len(xs)
    steps = 1
    for g in grid:
        steps *= g

    def wrapped(*refs):
        o0 = n_in + ns
        s0 = o0 + n_out + ns
        step = pl.program_id(0)
        for d in range(1, len(grid)):
            step = step * grid[d] + pl.program_id(d)

        def copies():
            return _exchange_copies(kind, refs[n_in:o0], refs[o0 + n_out:s0], *refs[s0 + n_scr:])

        @pl.when(step == 0)
        def _():
            for cp in copies():
                cp.start()

        body(*refs[:n_in], *refs[o0:o0 + n_out], *refs[s0:s0 + n_scr])

        @pl.when(step == steps - 1)
        def _():
            for cp in copies():
                cp.wait()

    any_ = pl.BlockSpec(memory_space=pl.ANY)
    side_shapes = [SDS((NDEV,) + x.shape if kind == "gather" else x.shape, x.dtype) for x in xs]
    outs = pl.pallas_call(wrapped, name=name, grid=grid, in_specs=list(in_specs) + [any_] * ns,
                          out_specs=list(out_specs) + [any_] * ns, out_shape=list(out_shape) + side_shapes,
                          scratch_shapes=list(scratch) + _exchange_scratch(ns), compiler_params=params)(*args, *xs)
    return outs[:n_out], outs[n_out:]


def inproj_fwd(x, nw, scale, shift, W, tn, name, side=None):
    S, D = x.shape
    N = W.shape[1]
    tm = _rows(S, 1024)

    def body(x_ref, nw_ref, sc_ref, sh_ref, w_ref, proj_ref, h_ref):
        @pl.when(pl.program_id(1) == 0)
        def _():
            h_ref[...] = _rms_mod(x_ref[...], nw_ref[...], sc_ref[...], sh_ref[...]).astype(bf16)

        proj_ref[...] = jnp.dot(h_ref[...], w_ref[...], preferred_element_type=f32)

    vec = pl.BlockSpec((1, D), lambda i, j: (0, 0))
    return _call(
        body, name=name, grid=(S // tm, N // tn),
        in_specs=[pl.BlockSpec((tm, D), lambda i, j: (i, 0)), vec, vec, vec, pl.BlockSpec((D, tn), lambda i, j: (0, j))],
        out_specs=[pl.BlockSpec((tm, tn), lambda i, j: (i, j)), pl.BlockSpec((tm, D), lambda i, j: (i, 0))],
        out_shape=[SDS((S, N), f32), SDS((S, D), bf16)], args=(x, nw, scale, shift, W), side=side)


def inproj_bwd_x(x, nw, scale, shift, W, dproj, dx_res, tn, name, side=None):
    S, D = x.shape
    N = W.shape[1]
    tm = _rows(S, 1024)
    nj = N // tn

    def body(x_ref, nw_ref, sc_ref, sh_ref, w_ref, dp_ref, dxr_ref, dx_ref, dnw_ref, dsc_ref, dsh_ref, acc):
        i, j = pl.program_id(0), pl.program_id(1)

        @pl.when(j == 0)
        def _():
            acc[...] = jnp.zeros_like(acc)

        @pl.when((i == 0) & (j == 0))
        def _():
            dnw_ref[...] = jnp.zeros_like(dnw_ref)
            dsc_ref[...] = jnp.zeros_like(dsc_ref)
            dsh_ref[...] = jnp.zeros_like(dsh_ref)

        acc[...] += _nt(dp_ref[...], w_ref[...])

        @pl.when(j == nj - 1)
        def _():
            _, vjp = jax.vjp(_rms_mod, x_ref[...], nw_ref[...], sc_ref[...], sh_ref[...])
            dx, dnw, dsc, dsh = vjp(acc[...])
            dx_ref[...] = dxr_ref[...] + dx
            dnw_ref[...] += dnw
            dsc_ref[...] += dsc
            dsh_ref[...] += dsh

    vec = pl.BlockSpec((1, D), lambda i, j: (0, 0))
    row = pl.BlockSpec((tm, D), lambda i, j: (i, 0))
    return _call(
        body, name=name, grid=(S // tm, nj),
        in_specs=[row, vec, vec, vec, pl.BlockSpec((D, tn), lambda i, j: (0, j)), pl.BlockSpec((tm, tn), lambda i, j: (i, j)), row],
        out_specs=[row, vec, vec, vec],
        out_shape=[SDS((S, D), f32), SDS((1, D), f32), SDS((1, D), f32), SDS((1, D), f32)],
        scratch=[pltpu.VMEM((tm, D), f32)], args=(x, nw, scale, shift, W, dproj, dx_res), side=side)


def matmul_tn(a, b, tn, name, side=None):
    S, K = a.shape
    N = b.shape[1]
    tm = _rows(S, 1024)
    ni = S // tm

    def body(a_ref, b_ref, o_ref):
        @pl.when(pl.program_id(1) == 0)
        def _():
            o_ref[...] = jnp.zeros_like(o_ref)

        o_ref[...] += _tn(a_ref[...], b_ref[...])

    return _call(
        body, name=name, grid=(N // tn, ni),
        in_specs=[pl.BlockSpec((tm, K), lambda j, i: (i, 0)), pl.BlockSpec((tm, tn), lambda j, i: (i, j))],
        out_specs=[pl.BlockSpec((K, tn), lambda j, i: (0, j))],
        out_shape=[SDS((K, N), f32)], args=(a, b), side=side)


def _conv_taps(xs, w, n_out):
    taps = []
    for j in range(4):
        s = 3 - j
        sh = xs if s == 0 else pltpu.roll(xs, s, axis=0)
        taps.append(sh[8:8 + n_out])
    conv = taps[0] * w[0] + taps[1] * w[1] + taps[2] * w[2] + taps[3] * w[3]
    return taps, conv


def _act_norm(conv, mul):
    s = _silu(conv)
    return s * (mul * lax.rsqrt(jnp.sum(s * s, axis=-1, keepdims=True) + EPS))


def gdn_prep_fwd(proj, conv_w, name):
    S = proj.shape[0]
    R = _rows(S, 512)

    def body(x_ref, w_ref, o_ref):
        j = pl.program_id(0)
        w = [w_ref[t:t + 1, :] for t in range(4)]

        def sweep(act):
            def piece(r, c):
                t0 = pl.multiple_of(r * R, R)
                cur = x_ref[pl.ds(t0, R), :]
                prev = x_ref[pl.ds(pl.multiple_of(jnp.maximum(t0 - 8, 0), 8), 8), :]
                prev = jnp.where(r == 0, 0.0, prev)
                _, conv = _conv_taps(jnp.concatenate([prev, cur], axis=0), w, R)
                o_ref[pl.ds(t0, R), :] = act(conv)
                return c

            lax.fori_loop(0, S // R, piece, 0)

        @pl.when(j < 2 * GDN_QK_HEADS)
        def _():
            sweep(lambda c: _act_norm(c, jnp.where(j < GDN_QK_HEADS, HD ** -0.5, 1.0).astype(f32)))

        @pl.when(j >= 2 * GDN_QK_HEADS)
        def _():
            sweep(_silu)

    return pl.pallas_call(
        body, name=name, grid=(GDN_CONV // 128,),
        in_specs=[pl.BlockSpec((S, 128), lambda j: (0, j)), pl.BlockSpec((4, 128), lambda j: (0, j))],
        out_specs=pl.BlockSpec((S, 128), lambda j: (0, j)),
        out_shape=SDS((S, GDN_CONV), f32),
        compiler_params=_cp(("arbitrary",)),
    )(proj, conv_w)


def gdn_prep_bwd(proj, conv_w, dqkvc, dproj, name):
    S = proj.shape[0]
    R = _rows(S, 512)
    NP = S // R

    def body(x_ref, w_ref, dn_ref, _, dx_ref, dw_ref):
        jb = pl.program_id(0)
        w = [w_ref[j:j + 1, :] for j in range(4)]

        def piece(act, r, dw):
            t0 = pl.multiple_of(r * R, R)
            cur = x_ref[pl.ds(t0, R), :]
            prev = x_ref[pl.ds(pl.multiple_of(jnp.maximum(t0 - 8, 0), 8), 8), :]
            prev = jnp.where(r == 0, 0.0, prev)
            nxt0 = pl.multiple_of(jnp.minimum(t0 + R, S - 8), 8)
            nxt = x_ref[pl.ds(nxt0, 8), :]
            dn_cur = dn_ref[pl.ds(t0, R), :]
            dn_nxt = jnp.where(r == NP - 1, 0.0, dn_ref[pl.ds(nxt0, 8), :])
            xs = jnp.concatenate([prev, cur, nxt], axis=0)
            taps, conv = _conv_taps(xs, w, R + 8)
            dn = jnp.concatenate([dn_cur, dn_nxt], axis=0)
            _, vjp = jax.vjp(act, conv)
            dxc = vjp(dn)[0]
            n = R + 8
            dx = dxc[0:R] * w[3]
            for j in range(3):
                s = 3 - j
                dx = dx + pltpu.roll(dxc, n - s, axis=0)[0:R] * w[j]
            dx_ref[pl.ds(t0, R), :] = dx
            return tuple(dw[j] + jnp.sum(dxc[0:R] * taps[j][0:R], axis=0, keepdims=True) for j in range(4))

        def sweep(act):
            dw = lax.fori_loop(0, NP, functools.partial(piece, act), tuple(jnp.zeros((1, 128), f32) for _ in range(4)))
            for j in range(4):
                dw_ref[j:j + 1, :] = dw[j]

        @pl.when(jb < 2 * GDN_QK_HEADS)
        def _():
            sweep(lambda c: _act_norm(c, jnp.where(jb < GDN_QK_HEADS, HD ** -0.5, 1.0).astype(f32)))

        @pl.when(jb >= 2 * GDN_QK_HEADS)
        def _():
            sweep(_silu)

    col = pl.BlockSpec((S, 128), lambda j: (0, j))
    wsp = pl.BlockSpec((4, 128), lambda j: (0, j))
    return pl.pallas_call(
        body, name=name, grid=(GDN_CONV // 128,),
        in_specs=[col, wsp, col, pl.BlockSpec(memory_space=pl.ANY)], out_specs=[col, wsp],
        out_shape=[SDS(dproj.shape, f32), SDS((4, GDN_CONV), f32)],
        input_output_aliases={3: 0},
        compiler_params=_cp(("arbitrary",)),
    )(proj, conv_w, dqkvc, dproj)


def _chunk_tril(R):
    ii = lax.broadcasted_iota(jnp.int32, (R, R), 0)
    jj = lax.broadcasted_iota(jnp.int32, (R, R), 1)
    return ((ii // CHUNK == jj // CHUNK) & (ii >= jj)).astype(f32)


def _gdn_gates(b, a, A_log, dt_bias, tril):
    beta = jax.nn.sigmoid(b)
    g = -jnp.exp(A_log) * jax.nn.softplus(a + dt_bias)
    return _hdot(tril, g), beta


_GDN_B_BLK = (GDN_CONV + GDN_V_W) // 128
_GDN_A_BLK = _GDN_B_BLK + 1


def gdn_gates_fwd(proj, A_log, dt_bias, name):
    S = proj.shape[0]
    R = _rows(S, 512)

    def body(b_ref, a_ref, al_ref, dt_ref, gc_ref, be_ref):
        gc, be = _gdn_gates(b_ref[...], a_ref[...], al_ref[...], dt_ref[...], _chunk_tril(R))
        gc_ref[...] = gc
        be_ref[...] = be

    vec = pl.BlockSpec((1, 128), lambda i: (0, 0))
    blk = pl.BlockSpec((R, 128), lambda i: (i, 0))
    return pl.pallas_call(
        body, name=name, grid=(S // R,),
        in_specs=[pl.BlockSpec((R, 128), lambda i: (i, _GDN_B_BLK)), pl.BlockSpec((R, 128), lambda i: (i, _GDN_A_BLK)), vec, vec],
        out_specs=[blk, blk], out_shape=[SDS((S, 128), f32), SDS((S, 128), f32)],
        compiler_params=_cp(("arbitrary",)),
    )(proj, proj, A_log, dt_bias)


def gdn_gates_bwd(proj, A_log, dt_bias, dgc, dbeta, dproj, name):
    S = proj.shape[0]
    R = _rows(S, 512)

    def body(b_ref, a_ref, al_ref, dt_ref, dgc_ref, dbe_ref, _, dp_ref, dal_ref, ddt_ref):
        @pl.when(pl.program_id(0) == 0)
        def _():
            dal_ref[...] = jnp.zeros_like(dal_ref)
            ddt_ref[...] = jnp.zeros_like(ddt_ref)

        tril = _chunk_tril(R)
        _, vjp = jax.vjp(lambda b, a, al, dt: _gdn_gates(b, a, al, dt, tril), b_ref[...], a_ref[...], al_ref[...], dt_ref[...])
        db, da, dal, ddt = vjp((dgc_ref[...], dbe_ref[...]))
        dp_ref[:, 0:128] = db
        dp_ref[:, 128:256] = da
        dal_ref[...] += dal
        ddt_ref[...] += ddt

    vec = pl.BlockSpec((1, 128), lambda i: (0, 0))
    blk = pl.BlockSpec((R, 128), lambda i: (i, 0))
    return pl.pallas_call(
        body, name=name, grid=(S // R,),
        in_specs=[pl.BlockSpec((R, 128), lambda i: (i, _GDN_B_BLK)), pl.BlockSpec((R, 128), lambda i: (i, _GDN_A_BLK)), vec, vec, blk, blk,
                  pl.BlockSpec(memory_space=pl.ANY)],
        out_specs=[pl.BlockSpec((R, 256), lambda i: (i, _GDN_B_BLK // 2)), vec, vec],
        out_shape=[SDS(dproj.shape, f32), SDS((1, 128), f32), SDS((1, 128), f32)],
        input_output_aliases={6: 0},
        compiler_params=_cp(("arbitrary",)),
    )(proj, proj, A_log, dt_bias, dgc, dbeta, dproj)


@jax.custom_vjp
def _inv_given(L, T):
    return T


def _inv_given_bwd(T, ct):
    dL = -_nt(_tn(T, ct), T)
    return dL, jnp.zeros_like(T)


_inv_given.defvjp(lambda L, T: (T, T), _inv_given_bwd)


REP = GDN_V_HEADS // GDN_QK_HEADS


def _gdn_intra_all(qs, ks, vs, gcols, bcols, Ts=None):
    H = len(vs)
    C = vs[0].shape[0]
    ii = lax.broadcasted_iota(jnp.int32, (C, C), 0)
    jj = lax.broadcasted_iota(jnp.int32, (C, C), 1)
    grows = [jnp.sum(jnp.where(ii == jj, g, 0.0), axis=0, keepdims=True) for g in gcols]
    decs = [jnp.exp(jnp.where(ii >= jj, gcols[h] - grows[h], NEG)) for h in range(H)]
    kbs = [ks[h // REP] * bcols[h] for h in range(H)]
    As = [_mm_nt(kbs[h], ks[h // REP]) for h in range(H)]
    Ls = [jnp.where(ii > jj, As[h] * decs[h], 0.0) for h in range(H)]
    if Ts is None:
        T = _neumann_inv_batched(Ls)
    else:
        T = [_inv_given(Ls[h], Ts[h]) for h in range(H)]
    us = [_mm(T[h], vs[h] * bcols[h]) for h in range(H)]
    ws = [_mm(T[h], kbs[h] * jnp.exp(gcols[h])) for h in range(H)]
    qk = [_mm_nt(qs[p], ks[p]) for p in range(H // REP)]
    return us, ws, [qk[h // REP] * decs[h] for h in range(H)], T


def _neumann_inv_batched(Ls):
    n, C = 4, Ls[0].shape[0]
    r0 = lax.broadcasted_iota(jnp.int32, (n * C, n * C), 0)
    c0 = lax.broadcasted_iota(jnp.int32, (n * C, n * C), 1)
    same = (r0 // C) == (c0 // C)

    def blockdiag(split):
        return tuple(jnp.where(same, jnp.concatenate([x] * n, axis=0), jnp.zeros((), bf16)) for x in split)

    Ms = [jnp.concatenate(Ls[b:b + n], axis=1) for b in range(0, len(Ls), n)]
    eye = (lax.broadcasted_iota(jnp.int32, (C, n * C), 0) == (lax.broadcasted_iota(jnp.int32, (C, n * C), 1) & (C - 1))).astype(f32)
    Ps = [eye - M for M in Ms]
    Ss = [_split2(M) for M in Ms]
    Bs = [blockdiag(S) for S in Ss]
    k = 1
    while 2 * k < C:
        Ss = [_split2(_dot3(S, B)) for S, B in zip(Ss, Bs)]
        Bs = [blockdiag(S) for S in Ss]
        Ps = [P + _dot3(_split2(P), B) for P, B in zip(Ps, Bs)]
        k *= 2
    return [P[:, h * C:(h + 1) * C] for P in Ps for h in range(n)]


def _gdn_scan_all(qs, ks, gcols, us, ws, attns, S0s):
    H = len(us)
    C = us[0].shape[0]
    last = lax.broadcasted_iota(jnp.int32, (C, 1), 0) == C - 1
    glast = [jnp.sum(jnp.where(last, g, 0.0), axis=0, keepdims=True) for g in gcols]
    wS = [_mm(ws[h], S0s[h]) for h in range(H)]
    qS = [_mm(qs[h // REP] * jnp.exp(gcols[h]), S0s[h]) for h in range(H)]
    vn = [us[h] - wS[h] for h in range(H)]
    av = [_mm(attns[h], vn[h]) for h in range(H)]
    kv = [_mm_tn(ks[h // REP] * jnp.exp(glast[h] - gcols[h]), vn[h]) for h in range(H)]
    return [qS[h] + av[h] for h in range(H)], [S0s[h] * jnp.exp(glast[h]) + kv[h] for h in range(H)]


def _head_cols(blk):
    lane = lax.broadcasted_iota(jnp.int32, blk.shape, 1)
    return [jnp.sum(jnp.where(lane == h, blk, 0.0), axis=1, keepdims=True) for h in range(GDN_V_HEADS)]


def _head_lanes(cols):
    lane = lax.broadcasted_iota(jnp.int32, (cols[0].shape[0], 128), 1)
    out = jnp.zeros((cols[0].shape[0], 128), f32)
    for h, c in enumerate(cols):
        out = out + jnp.where(lane == h, c, 0.0)
    return out


CPS = 4


def _rows_of(c):
    return slice(c * CHUNK, (c + 1) * CHUNK)


def _heads(ref, n, c):
    return [ref[_rows_of(c), h * HD:(h + 1) * HD].astype(f32) for h in range(n)]


def _mats(ref, c):
    return [ref[c, h].astype(f32) for h in range(GDN_V_HEADS)]


def _gdn_specs(NB, rv=None):
    ix = (lambda n: n) if rv is None else rv
    R = CPS * CHUNK
    qs = pl.BlockSpec((R, GDN_QK_W), lambda n: (ix(n), 0))
    ks = pl.BlockSpec((R, GDN_QK_W), lambda n: (ix(n), 1))
    vs = pl.BlockSpec((R, GDN_V_W), lambda n: (ix(n), 1))
    g1 = pl.BlockSpec((R, 128), lambda n: (ix(n), 0))
    wide = pl.BlockSpec((R, GDN_V_W), lambda n: (ix(n), 0))
    sq = pl.BlockSpec((CPS, GDN_V_HEADS, CHUNK, CHUNK), lambda n: (ix(n), 0, 0, 0))
    st = pl.BlockSpec((CPS, GDN_V_HEADS, HD, HD), lambda n: (ix(n), 0, 0, 0))
    return qs, ks, vs, g1, wide, sq, st


def gdn_intra_fwd(qkvc, gc, beta, name, side=None):
    S = qkvc.shape[0]
    NC = S // CHUNK

    def body(q_ref, k_ref, v_ref, gc_ref, be_ref, u_ref, w_ref, at_ref, T_ref):
        for c in range(CPS):
            rows = _rows_of(c)
            us, ws, attns, Ts = _gdn_intra_all(_heads(q_ref, GDN_QK_HEADS, c), _heads(k_ref, GDN_QK_HEADS, c), _heads(v_ref, GDN_V_HEADS, c),
                                               _head_cols(gc_ref[rows, :]), _head_cols(be_ref[rows, :]))
            for h in range(GDN_V_HEADS):
                u_ref[rows, h * HD:(h + 1) * HD] = us[h]
                w_ref[rows, h * HD:(h + 1) * HD] = ws[h].astype(bf16)
                at_ref[c, h] = attns[h].astype(bf16)
                T_ref[c, h] = Ts[h].astype(bf16)

    qs, ks, vs, g1, wide, sq, _ = _gdn_specs(NC // CPS)
    return _call(
        body, name=name, grid=(NC // CPS,),
        in_specs=[qs, ks, vs, g1, g1], out_specs=[wide, wide, sq, sq],
        out_shape=[SDS((S, GDN_V_W), f32), SDS((S, GDN_V_W), bf16),
                   SDS((NC, GDN_V_HEADS, CHUNK, CHUNK), bf16), SDS((NC, GDN_V_HEADS, CHUNK, CHUNK), bf16)],
        args=(qkvc, qkvc, qkvc, gc, beta), side=side)


def gdn_scan_fwd(qkvc, gc, u, w, attn, name, side=None):
    S = qkvc.shape[0]
    NC = S // CHUNK

    def body(q_ref, k_ref, gc_ref, u_ref, w_ref, at_ref, o_ref, st_ref, state):
        @pl.when(pl.program_id(0) == 0)
        def _():
            state[...] = jnp.zeros_like(state)

        for c in range(CPS):
            rows = _rows_of(c)
            S0s = [state[h] for h in range(GDN_V_HEADS)]
            os_, S1s = _gdn_scan_all(_heads(q_ref, GDN_QK_HEADS, c), _heads(k_ref, GDN_QK_HEADS, c), _head_cols(gc_ref[rows, :]),
                                     _heads(u_ref, GDN_V_HEADS, c), _heads(w_ref, GDN_V_HEADS, c), _mats(at_ref, c), S0s)
            for h in range(GDN_V_HEADS):
                o_ref[rows, h * HD:(h + 1) * HD] = os_[h]
                st_ref[c, h] = S0s[h].astype(bf16)
                state[h] = S1s[h]

    qs, ks, _, g1, wide, sq, st = _gdn_specs(NC // CPS)
    return _call(
        body, name=name, grid=(NC // CPS,),
        in_specs=[qs, ks, g1, wide, wide, sq], out_specs=[wide, st],
        out_shape=[SDS((S, GDN_V_W), f32), SDS((NC, GDN_V_HEADS, HD, HD), bf16)],
        scratch=[pltpu.VMEM((GDN_V_HEADS, HD, HD), f32)], args=(qkvc, qkvc, gc, u, w, attn), side=side)


def gdn_scan_bwd(qkvc, gc, u, w, attn, states, do, name, side=None):
    S = qkvc.shape[0]
    NC = S // CHUNK
    NB = NC // CPS

    def body(q_ref, k_ref, gc_ref, u_ref, w_ref, at_ref, st_ref, do_ref,
             dq_ref, dk_ref, dgc_ref, du_ref, dw_ref, dat_ref, dstate):
        @pl.when(pl.program_id(0) == 0)
        def _():
            dstate[...] = jnp.zeros_like(dstate)

        VH = range(GDN_V_HEADS)
        for c in reversed(range(CPS)):
            rows = _rows_of(c)
            _, vjp = jax.vjp(_gdn_scan_all, _heads(q_ref, GDN_QK_HEADS, c), _heads(k_ref, GDN_QK_HEADS, c), _head_cols(gc_ref[rows, :]),
                             _heads(u_ref, GDN_V_HEADS, c), _heads(w_ref, GDN_V_HEADS, c), _mats(at_ref, c), _mats(st_ref, c))
            dqs, dks, dgs, dus, dws, dats, dS0s = vjp((_heads(do_ref, GDN_V_HEADS, c), [dstate[h] for h in VH]))
            for p in range(GDN_QK_HEADS):
                dq_ref[rows, p * HD:(p + 1) * HD] = dqs[p]
                dk_ref[rows, p * HD:(p + 1) * HD] = dks[p]
            for h in VH:
                du_ref[rows, h * HD:(h + 1) * HD] = dus[h].astype(bf16)
                dw_ref[rows, h * HD:(h + 1) * HD] = dws[h].astype(bf16)
                dat_ref[c, h] = dats[h].astype(bf16)
                dstate[h] = dS0s[h]
            dgc_ref[rows, :] = _head_lanes(dgs)

    qs, ks, _, g1, wide, sq, st = _gdn_specs(NB, lambda n: NB - 1 - n)
    dqs = pl.BlockSpec((CPS * CHUNK, GDN_QK_W), lambda n: (NB - 1 - n, 0))
    return _call(
        body, name=name, grid=(NB,),
        in_specs=[qs, ks, g1, wide, wide, sq, st, wide],
        out_specs=[dqs, dqs, g1, wide, wide, sq],
        out_shape=[SDS((S, GDN_QK_W), f32), SDS((S, GDN_QK_W), f32), SDS((S, 128), f32), SDS((S, GDN_V_W), bf16),
                   SDS((S, GDN_V_W), bf16), SDS((NC, GDN_V_HEADS, CHUNK, CHUNK), bf16)],
        scratch=[pltpu.VMEM((GDN_V_HEADS, HD, HD), f32)], args=(qkvc, qkvc, gc, u, w, attn, states, do), side=side)


def gdn_intra_bwd(qkvc, gc, beta, Ts, du, dw, dattn, dq_s, dk_s, dgc_s, name, side=None):
    S = qkvc.shape[0]
    NC = S // CHUNK

    def body(q_ref, k_ref, v_ref, gc_ref, be_ref, T_ref, du_ref, dw_ref, dat_ref, dqs_ref, dks_ref, dgs_ref,
             dqkv_ref, dgc_ref, dbe_ref):
        VH = range(GDN_V_HEADS)
        for c in range(CPS):
            rows = _rows_of(c)
            Ts = _mats(T_ref, c)
            _, vjp = jax.vjp(lambda q_, k_, v_, g_, b_: _gdn_intra_all(q_, k_, v_, g_, b_, Ts)[:3],
                             _heads(q_ref, GDN_QK_HEADS, c), _heads(k_ref, GDN_QK_HEADS, c), _heads(v_ref, GDN_V_HEADS, c),
                             _head_cols(gc_ref[rows, :]), _head_cols(be_ref[rows, :]))
            dqs, dks, dvs, dgs, dbs = vjp((_heads(du_ref, GDN_V_HEADS, c), _heads(dw_ref, GDN_V_HEADS, c), _mats(dat_ref, c)))
            for p in range(GDN_QK_HEADS):
                dqkv_ref[rows, p * HD:(p + 1) * HD] = dqs[p] + dqs_ref[rows, p * HD:(p + 1) * HD]
                dqkv_ref[rows, GDN_QK_W + p * HD:GDN_QK_W + (p + 1) * HD] = dks[p] + dks_ref[rows, p * HD:(p + 1) * HD]
            for h in VH:
                dqkv_ref[rows, 2 * GDN_QK_W + h * HD:2 * GDN_QK_W + (h + 1) * HD] = dvs[h]
            dgc_ref[rows, :] = _head_lanes(dgs) + dgs_ref[rows, :]
            dbe_ref[rows, :] = _head_lanes(dbs)

    qs, ks, vs, g1, wide, sq, _ = _gdn_specs(NC // CPS)
    dqs = pl.BlockSpec((CPS * CHUNK, GDN_QK_W), lambda n: (n, 0))
    return _call(
        body, name=name, grid=(NC // CPS,),
        in_specs=[qs, ks, vs, g1, g1, sq, wide, wide, sq, dqs, dqs, g1],
        out_specs=[pl.BlockSpec((CPS * CHUNK, GDN_CONV), lambda n: (n, 0)), g1, g1],
        out_shape=[SDS((S, GDN_CONV), f32), SDS((S, 128), f32), SDS((S, 128), f32)],
        args=(qkvc, qkvc, qkvc, gc, beta, Ts, du, dw, dattn, dq_s, dk_s, dgc_s), side=side)


def _gated_norm(o, z, nw):
    parts = []
    for h in range(GDN_V_HEADS):
        oh = o[:, h * HD:(h + 1) * HD]
        r = lax.rsqrt(jnp.mean(oh * oh, axis=-1, keepdims=True) + EPS)
        parts.append((oh * r * nw) * _silu(z[:, h * HD:(h + 1) * HD]))
    return jnp.concatenate(parts, axis=1)


def gdn_out_fwd(o, proj, nw, W, x, gate, name):
    S, D = x.shape
    tm = _rows(S, 256)

    def body(o_ref, z_ref, nw_ref, w_ref, x_ref, g_ref, xn_ref, y_ref, og_ref):
        og = _gated_norm(o_ref[...], z_ref[...], nw_ref[...]).astype(bf16)
        y = jnp.dot(og, w_ref[...], preferred_element_type=f32)
        og_ref[...] = og
        y_ref[...] = y
        xn_ref[...] = x_ref[...] + g_ref[...] * y

    row = pl.BlockSpec((tm, D), lambda i: (i, 0))
    wide = pl.BlockSpec((tm, GDN_V_W), lambda i: (i, 0))
    return pl.pallas_call(
        body, name=name, grid=(S // tm,),
        in_specs=[wide, pl.BlockSpec((tm, GDN_V_W), lambda i: (i, 2)), pl.BlockSpec((1, HD), lambda i: (0, 0)),
                  pl.BlockSpec((GDN_V_W, D), lambda i: (0, 0)), row, pl.BlockSpec((1, D), lambda i: (0, 0))],
        out_specs=[row, row, wide],
        out_shape=[SDS((S, D), f32), SDS((S, D), f32), SDS((S, GDN_V_W), bf16)],
        compiler_params=_cp(("arbitrary",)),
    )(o, proj, nw, W, x, gate)


def gdn_out_bwd(dxn, y, gate, o, proj, nw, W, name):
    S, D = dxn.shape
    tm = _rows(S, 256)

    def body(dx_ref, y_ref, g_ref, o_ref, z_ref, nw_ref, w_ref, dy_ref, dg_ref, do_ref, dz_ref, dnw_ref):
        @pl.when(pl.program_id(0) == 0)
        def _():
            dg_ref[...] = jnp.zeros_like(dg_ref)
            dnw_ref[...] = jnp.zeros_like(dnw_ref)

        dx = dx_ref[...]
        dy = dx * g_ref[...]
        dy_ref[...] = dy
        dg_ref[...] += jnp.sum(dx * y_ref[...], axis=0, keepdims=True)
        dog = _nt(dy, w_ref[...])
        _, vjp = jax.vjp(_gated_norm, o_ref[...], z_ref[...], nw_ref[...])
        do, dz, dnw = vjp(dog)
        do_ref[...] = do
        dz_ref[...] = dz
        dnw_ref[...] += dnw

    row = pl.BlockSpec((tm, D), lambda i: (i, 0))
    wide = pl.BlockSpec((tm, GDN_V_W), lambda i: (i, 0))
    vecd = pl.BlockSpec((1, D), lambda i: (0, 0))
    vech = pl.BlockSpec((1, HD), lambda i: (0, 0))
    return pl.pallas_call(
        body, name=name, grid=(S // tm,),
        in_specs=[row, row, vecd, wide, pl.BlockSpec((tm, GDN_V_W), lambda i: (i, 2)), vech, pl.BlockSpec((GDN_V_W, D), lambda i: (0, 0))],
        out_specs=[row, vecd, wide, pl.BlockSpec((tm, GDN_V_W), lambda i: (i, 2)), vech],
        out_shape=[SDS((S, D), f32), SDS((1, D), f32), SDS((S, GDN_V_W), f32), SDS((S, GDN_IN_PAD), f32), SDS((1, HD), f32)],
        compiler_params=_cp(("arbitrary",)),
    )(dxn, y, gate, o, proj, nw, W)


def _rms_w(x, w):
    return (x * lax.rsqrt(jnp.mean(x * x, axis=-1, keepdims=True) + EPS)) * w


def _split3(c):
    hi = c.astype(bf16).astype(f32)
    r1 = c - hi
    mid = r1.astype(bf16).astype(f32)
    lo = (r1 - mid).astype(bf16).astype(f32)
    return hi, mid, lo


_FOX_F_BLK = 4 * FOX_W // 128


def fox_prep_fwd(proj, f_bias, qn_w, kn_w, name):
    S = proj.shape[0]
    tm = _rows(S, 256)

    def body(q_ref, k_ref, v_ref, f_ref, fb_ref, qw_ref, kw_ref, Q_ref, K_ref, V_ref, carry):
        @pl.when(pl.program_id(0) == 0)
        def _():
            carry[...] = jnp.zeros_like(carry)

        ii = lax.broadcasted_iota(jnp.int32, (tm, tm), 0)
        jj = lax.broadcasted_iota(jnp.int32, (tm, tm), 1)
        lf = jax.nn.log_sigmoid(f_ref[...] + fb_ref[...])
        cum = _hdot((ii >= jj).astype(f32), lf) + carry[...]
        carry[...] = cum[tm - 1:tm, :]
        lane = lax.broadcasted_iota(jnp.int32, (tm, 128), 1)
        lo = lane < FOX_D
        qw2 = jnp.concatenate([qw_ref[...], qw_ref[...]], axis=1) * FOX_D ** -0.5
        kw2 = jnp.concatenate([kw_ref[...], kw_ref[...]], axis=1)

        def norm_pair(x, w2):
            x2 = x * x
            s_all = jnp.sum(x2, axis=1, keepdims=True)
            s_lo = jnp.sum(jnp.where(lo, x2, 0.0), axis=1, keepdims=True)
            r = jnp.where(lo, lax.rsqrt(s_lo * (1.0 / FOX_D) + EPS), lax.rsqrt((s_all - s_lo) * (1.0 / FOX_D) + EPS))
            return x * r * w2

        for p in range(FOX_H // 2):
            ps = slice(p * 128, (p + 1) * 128)
            yq, yk, xv = norm_pair(q_ref[:, ps], qw2), norm_pair(k_ref[:, ps], kw2), v_ref[:, ps]
            for e in range(2):
                h = 2 * p + e
                hi, mid, lw = _split3(cum[:, h:h + 1])
                eq = jnp.where(lane == FOX_D, hi, jnp.where(lane == FOX_D + 1, mid, jnp.where(lane == FOX_D + 2, lw, jnp.where(lane < FOX_D + 6, 1.0, 0.0))))
                ek = jnp.where(lane < FOX_D + 3, 1.0, jnp.where(lane == FOX_D + 3, -hi, jnp.where(lane == FOX_D + 4, -mid, jnp.where(lane == FOX_D + 5, -lw, 0.0))))
                ev = jnp.where(lane == FOX_D, 1.0, 0.0)
                mv = (lambda a: a) if e == 0 else (lambda a: pltpu.roll(a, FOX_D, axis=1))
                Q_ref[:, h * 128:(h + 1) * 128] = jnp.where(lo, mv(yq), eq).astype(bf16)
                K_ref[:, h * 128:(h + 1) * 128] = jnp.where(lo, mv(yk), ek).astype(bf16)
                V_ref[:, h * 128:(h + 1) * 128] = jnp.where(lo, mv(xv), ev).astype(bf16)

    def colblk(c):
        return pl.BlockSpec((tm, FOX_W), lambda i: (i, c))

    pad = pl.BlockSpec((tm, FOX_PW), lambda i: (i, 0))
    return pl.pallas_call(
        body, name=name, grid=(S // tm,),
        in_specs=[colblk(0), colblk(1), colblk(2), pl.BlockSpec((tm, 128), lambda i: (i, _FOX_F_BLK)),
                  pl.BlockSpec((1, 128), lambda i: (0, 0)), pl.BlockSpec((1, FOX_D), lambda i: (0, 0)), pl.BlockSpec((1, FOX_D), lambda i: (0, 0))],
        out_specs=[pad, pad, pad],
        out_shape=[SDS((S, FOX_PW), bf16)] * 3,
        scratch_shapes=[pltpu.VMEM((1, 128), f32)],
        compiler_params=_cp(("arbitrary",)),
    )(proj, proj, proj, proj, f_bias, qn_w, kn_w)


def fox_prep_bwd(proj, f_bias, qn_w, kn_w, dQ, dK, dV, dz, name):
    S = proj.shape[0]
    tm = _rows(S, 256)
    NB = S // tm

    def body(q_ref, k_ref, f_ref, fb_ref, qw_ref, kw_ref, dQ_ref, dK_ref, dV_ref, dz_ref,
             dp_ref, dfb_ref, dqw_ref, dkw_ref, carry):
        @pl.when(pl.program_id(0) == 0)
        def _():
            carry[...] = jnp.zeros_like(carry)
            dfb_ref[...] = jnp.zeros_like(dfb_ref)
            dqw_ref[...] = jnp.zeros_like(dqw_ref)
            dkw_ref[...] = jnp.zeros_like(dkw_ref)

        lane = lax.broadcasted_iota(jnp.int32, (tm, 128), 1)
        lo = lane < FOX_D
        qw2 = jnp.concatenate([qw_ref[...], qw_ref[...]], axis=1) * FOX_D ** -0.5
        kw2 = jnp.concatenate([kw_ref[...], kw_ref[...]], axis=1)

        def pair(ref, p):
            return jnp.where(lo, ref[:, 2 * p * 128:(2 * p + 1) * 128], pltpu.roll(ref[:, (2 * p + 1) * 128:(2 * p + 2) * 128], FOX_D, axis=1))

        def norm_pair_bwd(x, w2, dy):
            x2 = x * x
            s_all = jnp.sum(x2, axis=1, keepdims=True)
            s_lo = jnp.sum(jnp.where(lo, x2, 0.0), axis=1, keepdims=True)
            r = jnp.where(lo, lax.rsqrt(s_lo * (1.0 / FOX_D) + EPS), lax.rsqrt((s_all - s_lo) * (1.0 / FOX_D) + EPS))
            t = dy * w2 * x
            t_all = jnp.sum(t, axis=1, keepdims=True)
            t_lo = jnp.sum(jnp.where(lo, t, 0.0), axis=1, keepdims=True)
            dx = r * (w2 * dy - x * (r * r) * (jnp.where(lo, t_lo, t_all - t_lo) * (1.0 / FOX_D)))
            return dx, jnp.sum(dy * x * r, axis=0, keepdims=True)

        dcum = jnp.zeros((tm, 128), f32)
        dqw2 = jnp.zeros((1, 128), f32)
        dkw2 = jnp.zeros((1, 128), f32)
        for p in range(FOX_H // 2):
            ps = slice(p * 128, (p + 1) * 128)
            dxq, dw1 = norm_pair_bwd(q_ref[:, ps], qw2, pair(dQ_ref, p))
            dxk, dw2 = norm_pair_bwd(k_ref[:, ps], kw2, pair(dK_ref, p))
            dp_ref[:, p * 128:(p + 1) * 128] = dxq
            dp_ref[:, FOX_W + p * 128:FOX_W + (p + 1) * 128] = dxk
            dp_ref[:, 2 * FOX_W + p * 128:2 * FOX_W + (p + 1) * 128] = pair(dV_ref, p)
            dqw2 = dqw2 + dw1
            dkw2 = dkw2 + dw2
            for e in range(2):
                h = 2 * p + e
                dcum = dcum + jnp.where(lane == h, dQ_ref[:, h * 128 + FOX_D:h * 128 + FOX_D + 1]
                                        - dK_ref[:, h * 128 + FOX_D + 3:h * 128 + FOX_D + 4], 0.0)
        dp_ref[:, 3 * FOX_W:4 * FOX_W] = dz_ref[...]
        ii = lax.broadcasted_iota(jnp.int32, (tm, tm), 0)
        jj = lax.broadcasted_iota(jnp.int32, (tm, tm), 1)
        dlf = _hdot((ii <= jj).astype(f32), dcum) + carry[...]
        carry[...] += jnp.sum(dcum, axis=0, keepdims=True)
        df = dlf * jax.nn.sigmoid(-(f_ref[...] + fb_ref[...]))
        dp_ref[:, 4 * FOX_W:FOX_IN_PAD] = df
        dfb_ref[...] += jnp.sum(df, axis=0, keepdims=True)
        dqw_ref[...] += (dqw2[:, :FOX_D] + dqw2[:, FOX_D:]) * FOX_D ** -0.5
        dkw_ref[...] += dkw2[:, :FOX_D] + dkw2[:, FOX_D:]

    rv = lambda i: NB - 1 - i

    def colblk(c):
        return pl.BlockSpec((tm, FOX_W), lambda i: (rv(i), c))

    pad = pl.BlockSpec((tm, FOX_PW), lambda i: (rv(i), 0))
    cmp_ = pl.BlockSpec((tm, FOX_W), lambda i: (rv(i), 0))
    v128 = pl.BlockSpec((1, 128), lambda i: (0, 0))
    v64 = pl.BlockSpec((1, FOX_D), lambda i: (0, 0))
    return pl.pallas_call(
        body, name=name, grid=(NB,),
        in_specs=[colblk(0), colblk(1), pl.BlockSpec((tm, 128), lambda i: (rv(i), _FOX_F_BLK)), v128, v64, v64, pad, pad, pad, cmp_],
        out_specs=[pl.BlockSpec((tm, FOX_IN_PAD), lambda i: (rv(i), 0)), v128, v64, v64],
        out_shape=[SDS((S, FOX_IN_PAD), f32), SDS((1, 128), f32), SDS((1, FOX_D), f32), SDS((1, FOX_D), f32)],
        scratch_shapes=[pltpu.VMEM((1, 128), f32)],
        compiler_params=_cp(("arbitrary",)),
    )(proj, proj, proj, f_bias, qn_w, kn_w, dQ, dK, dV, dz)


FOX_HB = 2


def _diag_mask(t):
    return lax.broadcasted_iota(jnp.int32, (t, t), 1) <= lax.broadcasted_iota(jnp.int32, (t, t), 0)


def fox_attn_fwd(Q, K, V, name, side=None):
    S = Q.shape[0]
    t = _rows(S, 512)

    HB = FOX_HB
    HS = [slice(h * 128, (h + 1) * 128) for h in range(HB)]

    def body(q_ref, k_ref, v_ref, o_ref, m_sc, acc_sc, s_sc):
        i = pl.program_id(1)
        qs = [q_ref[:, sl] for sl in HS]
        m_sc[...] = jnp.full_like(m_sc, NEG)
        acc_sc[...] = jnp.zeros_like(acc_sc)

        def scores(j):
            j0 = pl.multiple_of(j * t, t)
            return [_nt(qs[h], k_ref[pl.ds(j0, t), HS[h]]) for h in range(HB)]

        def tile(j, diag):
            j0 = pl.multiple_of(j * t, t)
            ss = [s_sc[h] for h in range(HB)]
            if diag:
                ss = [jnp.where(_diag_mask(t), s, NEG) for s in ss]
            else:
                nxt = scores(j + 1)
            ms = [m_sc[h] for h in range(HB)]
            m_new = [jnp.maximum(ms[h], jnp.max(ss[h], axis=1, keepdims=True)) for h in range(HB)]
            ps = [jnp.exp(ss[h] - m_new[h]) for h in range(HB)]
            pv = [_nn(ps[h], v_ref[pl.ds(j0, t), HS[h]]) for h in range(HB)]
            for h in range(HB):
                acc_sc[h] = acc_sc[h] * jnp.exp(ms[h] - m_new[h]) + pv[h]
                m_sc[h] = m_new[h]
                if not diag:
                    s_sc[h] = nxt[h]

        def off_diag(j, c):
            tile(j, False)
            return c

        first = scores(0)
        for h in range(HB):
            s_sc[h] = first[h]
        lax.fori_loop(0, i, off_diag, 0)
        tile(i, True)
        lane = lax.broadcasted_iota(jnp.int32, (t, 128), 1)
        for h in range(HB):
            acc = acc_sc[h]
            l = acc[:, FOX_D:FOX_D + 1]
            o_ref[:, HS[h]] = jnp.where(lane == FOX_D, m_sc[h] + jnp.log(l), acc / l)

    blk = pl.BlockSpec((t, HB * 128), lambda h, i: (i, h))
    seq = pl.BlockSpec((S, HB * 128), lambda h, i: (0, h))
    return _call(
        body, name=name, grid=(FOX_H // HB, S // t),
        in_specs=[blk, seq, seq], out_specs=[blk], out_shape=[SDS((S, FOX_PW), f32)],
        scratch=[pltpu.VMEM((HB, t, 1), f32), pltpu.VMEM((HB, t, 128), f32), pltpu.VMEM((HB, t, t), f32)],
        args=(Q, K, V), side=side)


def fox_attn_bwd(Q, K, V, dO, O, name, side=None):
    S = Q.shape[0]
    t = _rows(S, 512)
    nq = S // t

    HB = FOX_HB
    HS = [slice(h * 128, (h + 1) * 128) for h in range(HB)]

    def body(k_ref, v_ref, q_ref, do_ref, o_ref, dq_ref, dk_ref, dv_ref):
        j = pl.program_id(1)

        @pl.when(j == 0)
        def _():
            dq_ref[...] = jnp.zeros_like(dq_ref)

        dk_ref[...] = jnp.zeros_like(dk_ref)
        dv_ref[...] = jnp.zeros_like(dv_ref)
        ks = [k_ref[:, sl] for sl in HS]
        vs = [v_ref[:, sl] for sl in HS]

        def tile(i, diag):
            i0 = pl.multiple_of(i * t, t)
            R = range(HB)
            qs = [q_ref[pl.ds(i0, t), HS[h]] for h in R]
            dos = [do_ref[pl.ds(i0, t), HS[h]] for h in R]
            ss = [_nt(qs[h], ks[h]) - o_ref[pl.ds(i0, t), h * 128 + FOX_D:h * 128 + FOX_D + 1] for h in R]
            if diag:
                ss = [jnp.where(_diag_mask(t), s, NEG) for s in ss]
            ps = [jnp.exp(s) for s in ss]
            dps = [_nt(dos[h], vs[h]) for h in R]
            dvs = [_tn(ps[h], dos[h]) for h in R]
            dss = [(ps[h] * dps[h]).astype(bf16) for h in R]
            dks = [_tn(dss[h], qs[h]) for h in R]
            dqs = [_nn(dss[h], ks[h]) for h in R]
            for h in R:
                dv_ref[:, HS[h]] += dvs[h]
                dk_ref[:, HS[h]] += dks[h]
                dq_ref[pl.ds(i0, t), HS[h]] += dqs[h]

        tile(j, True)

        def off_diag(i, c):
            tile(i, False)
            return c

        lax.fori_loop(j + 1, nq, off_diag, 0)

    blk = pl.BlockSpec((t, HB * 128), lambda h, j: (j, h))
    seq = pl.BlockSpec((S, HB * 128), lambda h, j: (0, h))
    return _call(
        body, name=name, grid=(FOX_H // HB, nq),
        in_specs=[blk, blk, seq, seq, seq], out_specs=[seq, blk, blk],
        out_shape=[SDS((S, FOX_PW), f32)] * 3, args=(K, V, Q, dO, O), side=side)


def fox_out_fwd(O, proj, W, x, gate, name):
    S, D = x.shape
    tm = _rows(S, 256)

    def body(o_ref, z_ref, w_ref, x_ref, g_ref, xn_ref, y_ref, og_ref):
        z = z_ref[...]
        og = jnp.concatenate([o_ref[:, h * 128:h * 128 + FOX_D] * _silu(z[:, h * FOX_D:(h + 1) * FOX_D]) for h in range(FOX_H)],
                             axis=1).astype(bf16)
        y = jnp.dot(og, w_ref[...], preferred_element_type=f32)
        og_ref[...] = og
        y_ref[...] = y
        xn_ref[...] = x_ref[...] + g_ref[...] * y

    row = pl.BlockSpec((tm, D), lambda i: (i, 0))
    cmp_ = pl.BlockSpec((tm, FOX_W), lambda i: (i, 0))
    return pl.pallas_call(
        body, name=name, grid=(S // tm,),
        in_specs=[pl.BlockSpec((tm, FOX_PW), lambda i: (i, 0)), pl.BlockSpec((tm, FOX_W), lambda i: (i, 3)),
                  pl.BlockSpec((FOX_W, D), lambda i: (0, 0)), row, pl.BlockSpec((1, D), lambda i: (0, 0))],
        out_specs=[row, row, cmp_],
        out_shape=[SDS((S, D), f32), SDS((S, D), f32), SDS((S, FOX_W), bf16)],
        compiler_params=_cp(("arbitrary",)),
    )(O, proj, W, x, gate)


def fox_out_bwd(dxn, y, gate, O, proj, W, name):
    S, D = dxn.shape
    tm = _rows(S, 256)

    def body(dx_ref, y_ref, g_ref, o_ref, z_ref, w_ref, dy_ref, dg_ref, dO_ref, dz_ref):
        @pl.when(pl.program_id(0) == 0)
        def _():
            dg_ref[...] = jnp.zeros_like(dg_ref)

        dx = dx_ref[...]
        dy = dx * g_ref[...]
        dy_ref[...] = dy
        dg_ref[...] += jnp.sum(dx * y_ref[...], axis=0, keepdims=True)
        dog = _nt(dy, w_ref[...])
        z = z_ref[...]
        lane = lax.broadcasted_iota(jnp.int32, (tm, FOX_D), 1)
        dzs = []
        for h in range(FOX_H):
            sl = slice(h * FOX_D, (h + 1) * FOX_D)
            zh = z[:, sl]
            sg = jax.nn.sigmoid(zh)
            oh = o_ref[:, h * 128:h * 128 + FOX_D]
            doh = dog[:, sl] * (zh * sg)
            delta = jnp.sum(doh * oh, axis=1, keepdims=True)
            dO_ref[:, h * 128:(h + 1) * 128] = jnp.concatenate([doh, jnp.where(lane == 0, -delta, 0.0)], axis=1).astype(bf16)
            dzs.append(dog[:, sl] * oh * (sg * (1.0 + zh * (1.0 - sg))))
        dz_ref[...] = jnp.concatenate(dzs, axis=1)

    row = pl.BlockSpec((tm, D), lambda i: (i, 0))
    vecd = pl.BlockSpec((1, D), lambda i: (0, 0))
    pad = pl.BlockSpec((tm, FOX_PW), lambda i: (i, 0))
    return pl.pallas_call(
        body, name=name, grid=(S // tm,),
        in_specs=[row, row, vecd, pad, pl.BlockSpec((tm, FOX_W), lambda i: (i, 3)), pl.BlockSpec((FOX_W, D), lambda i: (0, 0))],
        out_specs=[row, vecd, pad, pl.BlockSpec((tm, FOX_W), lambda i: (i, 0))],
        out_shape=[SDS((S, D), f32), SDS((1, D), f32), SDS((S, FOX_PW), bf16), SDS((S, FOX_W), f32)],
        compiler_params=_cp(("arbitrary",)),
    )(dxn, y, gate, O, proj, W)


def final_loss(x, fw, target, name):
    S, D = x.shape
    tm = _rows(S, 512)

    def body(x_ref, w_ref, t_ref, l_ref, dx_ref, dw_ref):
        @pl.when(pl.program_id(0) == 0)
        def _():
            l_ref[...] = jnp.zeros_like(l_ref)
            dw_ref[...] = jnp.zeros_like(dw_ref)

        out, vjp = jax.vjp(_rms_w, x_ref[...], w_ref[...])
        err = out - t_ref[...]
        l_ref[...] += 0.5 * jnp.sum(jnp.sum(err * err, axis=1, keepdims=True) * (1.0 / D), axis=0, keepdims=True)
        dx, dw = vjp(err * (1.0 / D))
        dx_ref[...] = dx
        dw_ref[...] += dw

    row = pl.BlockSpec((tm, D), lambda i: (i, 0))
    vec = pl.BlockSpec((1, D), lambda i: (0, 0))
    return pl.pallas_call(
        body, name=name, grid=(S // tm,),
        in_specs=[row, vec, row], out_specs=[pl.BlockSpec((1, 128), lambda i: (0, 0)), row, vec],
        out_shape=[SDS((1, 128), f32), SDS((S, D), f32), SDS((1, D), f32)],
        compiler_params=_cp(("arbitrary",)),
    )(x, fw, target)


def ada_fwd(c_all, ada_w, name):
    L, D, n = ada_w.shape

    def body(c_ref, w_ref, o_ref):
        cond = jnp.concatenate([_silu(c_ref[...]), jnp.zeros((8, D), f32)], axis=0)
        o_ref[0] = _nn(cond, w_ref[0])[0:8]

    return pl.pallas_call(
        body, name=name, grid=(L,),
        in_specs=[pl.BlockSpec((NDEV, D), lambda l: (0, 0)), pl.BlockSpec((1, D, n), lambda l: (l, 0, 0))],
        out_specs=pl.BlockSpec((1, NDEV, n), lambda l: (l, 0, 0)),
        out_shape=SDS((L, NDEV, n), f32),
        compiler_params=_cp(("arbitrary",)),
    )(c_all, ada_w)


def ada_grad(c_all, dmod, name):
    L, _, n = dmod.shape
    D = c_all.shape[1]

    def body(c_ref, d_ref, o_ref):
        cond = jnp.concatenate([_silu(c_ref[...]), jnp.zeros((8, D), f32)], axis=0)
        dm = jnp.concatenate([d_ref[0], jnp.zeros((8, n), f32)], axis=0)
        o_ref[0] = _tn(cond, dm)

    return pl.pallas_call(
        body, name=name, grid=(L,),
        in_specs=[pl.BlockSpec((NDEV, D), lambda l: (0, 0)), pl.BlockSpec((1, NDEV, n), lambda l: (l, 0, 0))],
        out_specs=pl.BlockSpec((1, D, n), lambda l: (l, 0, 0)),
        out_shape=SDS((L, D, n), f32),
        compiler_params=_cp(("arbitrary",)),
    )(c_all, dmod)


def reduce_adam(parts, w, m, v, tr, name):
    n, R, C = parts.shape
    c1 = 1.0 / (1.0 - ADAM_B1 ** ADAM_STEP)
    c2 = 1.0 / (1.0 - ADAM_B2 ** ADAM_STEP)

    def body(p_ref, w_ref, m_ref, v_ref, g_ref, d_ref, nm_ref, nv_ref):
        g = p_ref[0].astype(f32)
        for s in range(1, n):
            g = g + p_ref[s].astype(f32)
        nm = ADAM_B1 * m_ref[...] + (1.0 - ADAM_B1) * g
        nv = ADAM_B2 * v_ref[...] + (1.0 - ADAM_B2) * (g * g)
        g_ref[...] = g
        nm_ref[...] = nm
        nv_ref[...] = nv
        d_ref[...] = -ADAM_LR * ((nm * c1) / (jnp.sqrt(nv * c2) + ADAM_EPS) + ADAM_WD * w_ref[...])

    blk = pl.BlockSpec((tr, C), lambda i: (i, 0))
    return pl.pallas_call(
        body, name=name, grid=(R // tr,),
        in_specs=[pl.BlockSpec((n, tr, C), lambda i: (0, i, 0)), blk, blk, blk],
        out_specs=[blk] * 4, out_shape=[SDS((R, C), f32)] * 4,
        compiler_params=_cp(("arbitrary",)),
    )(parts, w, m, v)


def all_gather(xs, name):
    n = len(xs)

    def body(*refs):
        x_refs, out_refs = refs[:n], refs[n:2 * n]
        send_sems, recv_sems, local_sems = refs[2 * n:]
        x_, y_, c_ = _my_pos()
        me, sibling = (x_, y_, c_), (x_, y_, 1 - c_)
        chips = [(1 - x_, y_), (x_, 1 - y_), (1 - x_, 1 - y_)]

        def rows(a, px, py, pc):
            return out_refs[a].at[4 * px + 2 * py + pc]

        def copy(a, k, block, to, own=False):
            return pltpu.make_async_remote_copy(
                src_ref=x_refs[a] if own else rows(a, *block), dst_ref=rows(a, *block),
                send_sem=send_sems.at[k, a], recv_sem=recv_sems.at[k, a], device_id=to, device_id_type=pl.DeviceIdType.MESH)

        mine = [pltpu.make_async_copy(x_refs[a], rows(a, *me), local_sems.at[a]) for a in range(n)]
        for cp in mine:
            cp.start()
        first = []
        for a in range(n):
            first.append(copy(a, 0, me, sibling, own=True))
            first += [copy(a, 1 + j, me, (*chip, c_), own=True) for j, chip in enumerate(chips)]
        for cp in first:
            cp.start()
        passed = []
        for j, chip in enumerate(chips):
            for a in range(n):
                copy(a, 1 + j, (*chip, c_), me).wait_recv()
                cp = copy(a, 4 + j, (*chip, c_), sibling)
                cp.start()
                passed.append(cp)
        for a in range(n):
            copy(a, 0, sibling, me).wait_recv()
            for j, chip in enumerate(chips):
                copy(a, 4 + j, (*chip, 1 - c_), me).wait_recv()
        for cp in first + passed:
            cp.wait_send()
        for cp in mine:
            cp.wait()

    any_ = pl.BlockSpec(memory_space=pl.ANY)
    return pl.pallas_call(
        body, name=name, out_shape=[SDS((NDEV,) + x.shape, x.dtype) for x in xs],
        in_specs=[any_] * n, out_specs=[any_] * n,
        scratch_shapes=[pltpu.SemaphoreType.DMA((7, n)), pltpu.SemaphoreType.DMA((7, n)), pltpu.SemaphoreType.DMA((n,))],
    )(*xs)


GDN_COLS = ((0, GDN_CONV + GDN_V_W, 0), (GDN_CONV + GDN_V_W, GDN_CONV + GDN_V_W + 16, GDN_CONV + GDN_V_W),
            (GDN_CONV + GDN_V_W + 16, GDN_IN, GDN_CONV + GDN_V_W + 128))
FOX_COLS = ((0, FOX_IN, 0),)


def _col_pieces(d, per, cols):
    lo, hi = per * d, per * (d + 1)
    out = []
    for a, b, dst in cols:
        s, e = max(lo, a), min(hi, b)
        if s < e:
            out.append((s - lo, e - s, dst + s - a))
    return out


def cols_from_blocks(g, cols, n_out, name):
    _, L, R, C = g.shape
    tr = min(256, R)

    def body(g_ref, o_ref):
        o_ref[...] = jnp.zeros_like(o_ref)
        for d in range(NDEV):
            for off, ln, dst in _col_pieces(d, C, cols):
                o_ref[0, :, dst:dst + ln] = g_ref[d, 0, :, off:off + ln]

    return pl.pallas_call(
        body, name=name, grid=(L, R // tr),
        in_specs=[pl.BlockSpec((NDEV, 1, tr, C), lambda l, i: (0, l, i, 0))],
        out_specs=pl.BlockSpec((1, tr, n_out), lambda l, i: (l, i, 0)),
        out_shape=SDS((L, R, n_out), g.dtype),
        compiler_params=_cp(("arbitrary", "arbitrary")),
    )(g)


def blocks_from_cols(dw, C, cols, name):
    R, n_in = dw.shape
    tr = min(256, R)

    def body(x_ref, o_ref):
        for d in range(NDEV):
            for off, ln, src in _col_pieces(d, C, cols):
                o_ref[d, :, off:off + ln] = x_ref[:, src:src + ln].astype(bf16)

    return pl.pallas_call(
        body, name=name, grid=(R // tr,),
        in_specs=[pl.BlockSpec((tr, n_in), lambda i: (i, 0))],
        out_specs=pl.BlockSpec((NDEV, tr, C), lambda i: (0, i, 0)),
        out_shape=SDS((NDEV, R, C), bf16),
        compiler_params=_cp(("arbitrary",)),
    )(dw)


BIG = ("a_w_in", "a_conv_w", "a_w_out", "b_w_in", "b_w_out")
SMALL = ("norm_w", "ada_b", "a_A_log", "a_dt_bias", "a_norm_w", "b_f_bias", "b_qn_w", "b_kn_w", "final_norm_w")


def _pack_small(arrs):
    rows = []
    for a in arrs:
        fl = a.reshape(-1)
        pad = (-fl.shape[0]) % 128
        if pad:
            fl = jnp.concatenate([fl, jnp.zeros((pad,), fl.dtype)])
        rows.append(fl)
    flat = jnp.concatenate(rows)
    pad = (-flat.shape[0]) % (8 * 128)
    if pad:
        flat = jnp.concatenate([flat, jnp.zeros((pad,), flat.dtype)])
    return flat.reshape(-1, 128)


def _unpack(packed, shapes, align):
    flat = packed.reshape(-1)
    out, off = [], 0
    for shp in shapes:
        n = 1
        for d in shp:
            n *= d
        out.append(flat[off:off + n].reshape(shp))
        off += n + ((-n) % align)
    return out


def _full_from_gathered(g, shard_shape, axis):
    g = jnp.moveaxis(g, 0, axis)
    shp = list(shard_shape)
    shp[axis] *= NDEV
    return g.reshape(shp)


def _pad_lanes(v, n=128):
    v = v.reshape(1, -1)
    return jnp.concatenate([v, jnp.zeros((1, n - v.shape[1]), v.dtype)], axis=1)


def _carried(fn, *args, side=None, **grads):
    if callable(side):
        side = side(**grads)
    res = fn(*args, side)
    return res if side is not None else (res, None)


def gdn_layer_fwd(x, mod, nw, weights, tag, sides):
    W_in, conv_w, A_log, dt_bias, a_nw, W_out = weights
    shift, scale, gate = mod
    got = {}
    (proj, h), got["inproj"] = _carried(inproj_fwd, x, nw, scale, shift, W_in, GDN_TN, f"{tag}_inproj", side=sides.get("inproj"))
    qkvc = gdn_prep_fwd(proj, conv_w, f"{tag}_prep")
    gc, beta = gdn_gates_fwd(proj, A_log, dt_bias, f"{tag}_gates")
    (u, w, attn, Ts), got["intra"] = _carried(gdn_intra_fwd, qkvc, gc, beta, f"{tag}_intra", side=sides.get("intra"))
    (o, states), got["scan"] = _carried(gdn_scan_fwd, qkvc, gc, u, w, attn, f"{tag}_scan", side=sides.get("scan"))
    x_new, y, og = gdn_out_fwd(o, proj, a_nw, W_out, x, gate, f"{tag}_out")
    return x_new, (x, proj, h, qkvc, gc, beta, o, states, Ts, y, og, u, w, attn), got


def gdn_layer_bwd(dxn, saved, mod, nw, weights, tag, sides):
    W_in, conv_w, A_log, dt_bias, a_nw, W_out = weights
    shift, scale, gate = mod
    x, proj, h, qkvc, gc, beta, o, states, Ts, y, og, u, w, attn = saved
    got = {}
    dy, dgate, do, dproj, da_nw = gdn_out_bwd(dxn, y, gate, o, proj, a_nw, W_out, f"{tag}_out_bwd")
    dW_out, = matmul_tn(og, dy, 512, f"{tag}_dwout")
    (dq_s, dk_s, dgc_s, du, dw, dattn), got["sbwd"] = _carried(
        gdn_scan_bwd, qkvc, gc, u, w, attn, states, do, f"{tag}_scan_bwd", side=sides.get("sbwd"), dW_out=dW_out)
    (dqkvc, dgc, dbeta), got["intrab"] = _carried(
        gdn_intra_bwd, qkvc, gc, beta, Ts, du, dw, dattn, dq_s, dk_s, dgc_s, f"{tag}_intra_bwd", side=sides.get("intrab"), dW_out=dW_out)
    dproj, dA_log, ddt = gdn_gates_bwd(proj, A_log, dt_bias, dgc, dbeta, dproj, f"{tag}_gates_bwd")
    dproj, dconv_w = gdn_prep_bwd(proj, conv_w, dqkvc, dproj, f"{tag}_prep_bwd")
    (dW_in,), got["dwin"] = _carried(matmul_tn, h, dproj, GDN_TN, f"{tag}_dwin", side=sides.get("dwin"), dW_out=dW_out)
    (dx, dnw, dscale, dshift), got["ibwd"] = _carried(
        inproj_bwd_x, x, nw, scale, shift, W_in, dproj, dxn, GDN_TN, f"{tag}_inproj_bwd", side=sides.get("ibwd"),
        dW_out=dW_out, dW_in=dW_in, dconv_w=dconv_w)
    grads = dict(norm_w=dnw, W_in=dW_in, conv_w=dconv_w, A_log=dA_log[:, :16], dt_bias=ddt[:, :16], a_nw=da_nw, W_out=dW_out,
                 dmod=jnp.concatenate([dshift, dscale, dgate], axis=1))
    return dx, grads, got


def fox_layer_fwd(x, mod, nw, weights, tag, sides):
    W_in, f_bias, qn_w, kn_w, W_out = weights
    shift, scale, gate = mod
    got = {}
    (proj, h), got["inproj"] = _carried(inproj_fwd, x, nw, scale, shift, W_in, FOX_TN, f"{tag}_inproj", side=sides.get("inproj"))
    Q, K, V = fox_prep_fwd(proj, f_bias, qn_w, kn_w, f"{tag}_prep")
    (O,), got["attn"] = _carried(fox_attn_fwd, Q, K, V, f"{tag}_attn", side=sides.get("attn"))
    x_new, y, og = fox_out_fwd(O, proj, W_out, x, gate, f"{tag}_out")
    return x_new, (x, proj, h, Q, K, V, O, y, og), got


def fox_layer_bwd(dxn, saved, mod, nw, weights, tag, sides):
    W_in, f_bias, qn_w, kn_w, W_out = weights
    shift, scale, gate = mod
    x, proj, h, Q, K, V, O, y, og = saved
    got = {}
    dy, dgate, dO, dz = fox_out_bwd(dxn, y, gate, O, proj, W_out, f"{tag}_out_bwd")
    dW_out, = matmul_tn(og, dy, 512, f"{tag}_dwout")
    (dQ, dK, dV), got["abwd"] = _carried(fox_attn_bwd, Q, K, V, dO, O, f"{tag}_attn_bwd", side=sides.get("abwd"))
    dproj, dfb, dqw, dkw = fox_prep_bwd(proj, f_bias, qn_w, kn_w, dQ, dK, dV, dz, f"{tag}_prep_bwd")
    (dW_in,), got["dwin"] = _carried(matmul_tn, h, dproj, FOX_TN, f"{tag}_dwin", side=sides.get("dwin"))
    dx, dnw, dscale, dshift = inproj_bwd_x(x, nw, scale, shift, W_in, dproj, dxn, FOX_TN, f"{tag}_inproj_bwd")
    grads = dict(norm_w=dnw, W_in=dW_in, f_bias=dfb[:, :16], qn_w=dqw, kn_w=dkw, W_out=dW_out,
                 dmod=jnp.concatenate([dshift, dscale, dgate], axis=1))
    return dx, grads, got


class LocalPlan:
    def __init__(self, full):
        self.full = full

    def layer_weights(self, i):
        j, f = i // 2, self.full
        return (f["a_w_in"][j], f["a_w_out"][j], f["a_conv_w"][j]) if i % 2 == 0 else (f["b_w_in"][j], f["b_w_out"][j])

    def fwd_sides(self, i):
        return {}

    def fwd_got(self, i, got):
        pass

    def bwd_sides(self, i):
        return {}

    def bwd_got(self, i, grads, got):
        pass


def device_step(x, mod_all, norm_w, small, final_norm_w, target, plan):
    D = x.shape[1]
    mods = [(mod_all[i:i + 1, 0:D], mod_all[i:i + 1, D:2 * D], mod_all[i:i + 1, 2 * D:3 * D]) for i in range(4)]

    def weights(i):
        j = i // 2
        if i % 2 == 0:
            W_in, W_out, conv_w = plan.layer_weights(i)
            return (W_in, conv_w, _pad_lanes(small["a_A_log"][j]), _pad_lanes(small["a_dt_bias"][j]), small["a_norm_w"][j:j + 1], W_out)
        W_in, W_out = plan.layer_weights(i)
        return (W_in, _pad_lanes(small["b_f_bias"][j]), small["b_qn_w"][j:j + 1], small["b_kn_w"][j:j + 1], W_out)

    saved, wts = [], []
    for i in range(4):
        wts.append(weights(i))
        fwd = gdn_layer_fwd if i % 2 == 0 else fox_layer_fwd
        x, sv, got = fwd(x, mods[i], norm_w[i:i + 1], wts[i], f"L{i}", plan.fwd_sides(i))
        plan.fwd_got(i, got)
        saved.append(sv)
    loss, dx, dfw = final_loss(x, final_norm_w.reshape(1, D), target, "final_loss")
    lg = [None] * 4
    for i in reversed(range(4)):
        bwd = gdn_layer_bwd if i % 2 == 0 else fox_layer_bwd
        dx, lg[i], got = bwd(dx, saved[i], mods[i], norm_w[i:i + 1], wts[i], f"L{i}", plan.bwd_sides(i))
        plan.bwd_got(i, lg[i], got)
    g = dict(
        norm_w=jnp.concatenate([lg[i]["norm_w"] for i in range(4)], axis=0),
        dmod=jnp.concatenate([lg[i]["dmod"] for i in range(4)], axis=0),
        a_w_in=[lg[i]["W_in"] for i in (0, 2)],
        a_conv_w=jnp.stack([lg[i]["conv_w"] for i in (0, 2)]),
        a_A_log=jnp.concatenate([lg[i]["A_log"] for i in (0, 2)], axis=0),
        a_dt_bias=jnp.concatenate([lg[i]["dt_bias"] for i in (0, 2)], axis=0),
        a_norm_w=jnp.concatenate([lg[i]["a_nw"] for i in (0, 2)], axis=0),
        a_w_out=[lg[i]["W_out"] for i in (0, 2)],
        b_w_in=[lg[i]["W_in"] for i in (1, 3)],
        b_f_bias=jnp.concatenate([lg[i]["f_bias"] for i in (1, 3)], axis=0),
        b_qn_w=jnp.concatenate([lg[i]["qn_w"] for i in (1, 3)], axis=0),
        b_kn_w=jnp.concatenate([lg[i]["kn_w"] for i in (1, 3)], axis=0),
        b_w_out=[lg[i]["W_out"] for i in (1, 3)],
        final_norm_w=dfw.reshape(-1),
    )
    return loss[0, 0], dx, g


class MeshPlan:
    def __init__(self, shards, w0, conv_full):
        self.shards = shards
        self.w = {0: w0}
        self.conv = conv_full
        self.recv = {}
        self.pending = {}
        self.names = {}

    def layer_weights(self, i):
        return self.w[i]

    def _gather_side(self, layer):
        names = ("a_w_in", "a_w_out") if layer % 2 == 0 else ("b_w_in", "b_w_out")
        out = []
        for n in names:
            sh = self.shards[n][layer // 2]
            out.append(sh.reshape(-1, sh.shape[-1]))
        return ("gather", out)

    def fwd_sides(self, i):
        if i == 0:
            kind, (b_in, b_out) = self._gather_side(1)
            return {"inproj": (kind, [b_out]), "scan": (kind, [b_in]), "intra": self._gather_side(2)}
        if i == 1:
            return {"attn": self._gather_side(3)}
        return {}

    def fwd_got(self, i, got):
        if got.get("inproj") is not None:
            self._b_out0, = got["inproj"]
        for key, layer in (("scan", 1), ("intra", 2), ("attn", 3)):
            if got.get(key) is None:
                continue
            g_in, g_out = (got[key][0], self._b_out0) if key == "scan" else got[key]
            D = g_out.shape[-1]
            j = layer // 2
            if layer % 2 == 1:
                W_in = cols_from_blocks(g_in[:, None], FOX_COLS, FOX_IN_PAD, f"b_w_in_cols{j}")[0]
                self.w[layer] = (W_in, g_out.reshape(-1, D))
            else:
                W_in = cols_from_blocks(g_in[:, None], GDN_COLS, GDN_IN_PAD, f"a_w_in_cols{j}")[0]
                self.w[layer] = (W_in, g_out.reshape(-1, D), self.conv[j])

    @staticmethod
    def _out_blocks(dW_out):
        return dW_out.astype(bf16).reshape(NDEV, -1, dW_out.shape[-1])

    def _in_blocks(self, name, j, dW_in):
        cols = GDN_COLS if name == "a_w_in" else FOX_COLS
        return blocks_from_cols(dW_in, self.shards[name].shape[-1], cols, f"{name}_blocks{j}")

    def bwd_sides(self, i):
        self.names = {}
        sides = {}
        for (layer, key) in [k for k in self.pending if k[0] == i]:
            self.names[key], arrs = self.pending.pop((layer, key))
            sides[key] = ("scatter", arrs)
        if i == 0:
            def sbwd(dW_out):
                self.names["sbwd"] = [("a_w_out", 0)]
                return ("scatter", [self._out_blocks(dW_out)])

            def ibwd(dW_out, dW_in, dconv_w):
                conv = jnp.stack([dconv_w, self._dconv1])
                n = conv.shape[-1] // NDEV
                self.names["ibwd"] = [("a_w_in", 0), ("a_conv_w", None)]
                return ("scatter", [self._in_blocks("a_w_in", 0, dW_in),
                                    jnp.moveaxis(conv.reshape(2, 4, NDEV, n), 2, 0).reshape(NDEV, 8, n)])

            sides["sbwd"], sides["ibwd"] = sbwd, ibwd
        return sides

    def bwd_got(self, i, grads, got):
        for key, arrs in got.items():
            if arrs is not None:
                self.recv.update(zip(self.names[key], arrs))
        j = i // 2
        if i % 2 == 1:
            self.pending[(i - 1, "sbwd" if i == 3 else "intrab")] = (
                [("b_w_in", j), ("b_w_out", j)], [self._in_blocks("b_w_in", j, grads["W_in"]), self._out_blocks(grads["W_out"])])
        elif i == 2:
            self.pending[(1, "abwd")] = (
                [("a_w_in", 1), ("a_w_out", 1)], [self._in_blocks("a_w_in", 1, grads["W_in"]), self._out_blocks(grads["W_out"])])
            self._dconv1 = grads["conv_w"]


def kernel(x, c, norm_w, ada_w, ada_b, a_w_in, a_conv_w, a_A_log, a_dt_bias, a_norm_w, a_w_out, b_w_in, b_f_bias, b_qn_w, b_kn_w, b_w_out, final_norm_w, loss_target, m_norm_w, m_ada_w, m_ada_b, m_a_w_in, m_a_conv_w, m_a_A_log, m_a_dt_bias, m_a_norm_w, m_a_w_out, m_b_w_in, m_b_f_bias, m_b_qn_w, m_b_kn_w, m_b_w_out, m_final_norm_w, v_norm_w, v_ada_w, v_ada_b, v_a_w_in, v_a_conv_w, v_a_A_log, v_a_dt_bias, v_a_norm_w, v_a_w_out, v_b_w_in, v_b_f_bias, v_b_qn_w, v_b_kn_w, v_b_w_out, v_final_norm_w):
    W = dict(norm_w=norm_w, ada_w=ada_w, ada_b=ada_b, a_w_in=a_w_in, a_conv_w=a_conv_w, a_A_log=a_A_log, a_dt_bias=a_dt_bias,
             a_norm_w=a_norm_w, a_w_out=a_w_out, b_w_in=b_w_in, b_f_bias=b_f_bias, b_qn_w=b_qn_w, b_kn_w=b_kn_w, b_w_out=b_w_out,
             final_norm_w=final_norm_w)
    M = dict(norm_w=m_norm_w, ada_w=m_ada_w, ada_b=m_ada_b, a_w_in=m_a_w_in, a_conv_w=m_a_conv_w, a_A_log=m_a_A_log,
             a_dt_bias=m_a_dt_bias, a_norm_w=m_a_norm_w, a_w_out=m_a_w_out, b_w_in=m_b_w_in, b_f_bias=m_b_f_bias, b_qn_w=m_b_qn_w,
             b_kn_w=m_b_kn_w, b_w_out=m_b_w_out, final_norm_w=m_final_norm_w)
    V = dict(norm_w=v_norm_w, ada_w=v_ada_w, ada_b=v_ada_b, a_w_in=v_a_w_in, a_conv_w=v_a_conv_w, a_A_log=v_a_A_log,
             a_dt_bias=v_a_dt_bias, a_norm_w=v_a_norm_w, a_w_out=v_a_w_out, b_w_in=v_b_w_in, b_f_bias=v_b_f_bias, b_qn_w=v_b_qn_w,
             b_kn_w=v_b_kn_w, b_w_out=v_b_w_out, final_norm_w=v_final_norm_w)
    S, D = x.shape[1], x.shape[2]
    me = 4 * lax.axis_index("x") + 2 * lax.axis_index("y") + lax.axis_index("c")
    small_shapes = [W[n].shape for n in SMALL]

    shards = {n: W[n].astype(bf16) for n in ("a_w_in", "a_w_out", "b_w_in", "b_w_out")}
    gath = all_gather([shards["a_w_in"][0], shards["a_w_out"][0], a_conv_w.reshape(8, -1), c.reshape(8, D // 8)], "gather_w0")
    conv_full = _full_from_gathered(gath[2].reshape((NDEV,) + a_conv_w.shape), a_conv_w.shape, 2)
    w0 = (cols_from_blocks(gath[0][:, None], GDN_COLS, GDN_IN_PAD, "a_w_in_cols0")[0], gath[1].reshape(-1, D), conv_full[0])
    plan = MeshPlan(shards, w0, conv_full)
    c_all = gath[3].reshape(NDEV, D)

    mod_part = ada_fwd(c_all, ada_w, "ada_fwd")
    n_ada = ada_w.shape[2]
    mod_g = all_gather([mod_part.reshape(4 * NDEV, n_ada)], "gather_mod")[0].reshape(NDEV, 4, NDEV, n_ada)
    mod_mine = lax.dynamic_index_in_dim(mod_g, me, axis=2, keepdims=False)
    mod_all = jnp.moveaxis(mod_mine, 0, 1).reshape(4, NDEV * n_ada) + ada_b

    loss, dx, g = device_step(x[0], mod_all, norm_w, W, final_norm_w, loss_target[0], plan)
    loss = lax.psum(loss, MESH_AXES)

    g_small = dict(g, ada_b=g["dmod"])
    sp = _pack_small([g_small[n] for n in SMALL])
    sp_all = all_gather([sp], "gather_small")[0]
    sw, sm, sv = (_pack_small([T[n] for n in SMALL]) for T in (W, M, V))
    sg, sd, snm, snv = (_unpack(t, small_shapes, 128) for t in reduce_adam(sp_all, sw, sm, sv, sp.shape[0], "adam_small"))

    off_b = 0
    for n, shp in zip(SMALL, small_shapes):
        if n == "ada_b":
            break
        cnt = 1
        for d in shp:
            cnt *= d
        off_b += cnt + ((-cnt) % 128)
    dmod_all = sp_all.reshape(NDEV, -1)[:, off_b:off_b + 4 * 3 * D].reshape(NDEV, 4, 3 * D)
    dmod_cols = lax.dynamic_slice_in_dim(dmod_all, me * n_ada, n_ada, axis=2)
    g_ada = ada_grad(c_all, jnp.moveaxis(dmod_cols, 0, 1), "ada_grad")
    r_ada = reduce_adam(g_ada.reshape(1, 4 * D, n_ada), *(T["ada_w"].reshape(4 * D, n_ada) for T in (W, M, V)), 512, "adam_ada")
    ag, ad, anm, anv = (t.reshape(ada_w.shape) for t in r_ada)

    big = {}
    for n in BIG:
        C = W[n].shape[-1]
        parts = plan.recv[(n, None)] if n == "a_conv_w" else jnp.stack([plan.recv[(n, 0)], plan.recv[(n, 1)]], axis=1).reshape(NDEV, -1, C)
        res = reduce_adam(parts, *(T[n].reshape(parts.shape[1:]) for T in (W, M, V)), min(256, parts.shape[1]), f"adam_{n}")
        big[n] = [t.reshape(W[n].shape) for t in res]

    outs = {}
    for idx, (k, sm_l, ada_t) in enumerate((("grad", sg, ag), ("delta", sd, ad), ("new_m", snm, anm), ("new_v", snv, anv))):
        d = dict(zip(SMALL, sm_l))
        d.update({n: big[n][idx] for n in BIG})
        d["ada_w"] = ada_t
        outs[k] = d
    order = ("norm_w", "ada_w", "ada_b", "a_w_in", "a_conv_w", "a_A_log", "a_dt_bias", "a_norm_w", "a_w_out", "b_w_in", "b_f_bias",
             "b_qn_w", "b_kn_w", "b_w_out", "final_norm_w")
    return (loss, dx[None], *[outs["grad"][n] for n in order], *[outs["delta"][n] for n in order],
            *[outs["new_m"][n] for n in order], *[outs["new_v"][n] for n in order])
```

```python
import functools

import jax
import jax.numpy as jnp
from jax import lax
from jax.experimental import pallas as pl
from jax.experimental.pallas import tpu as pltpu

f32 = jnp.float32
bf16 = jnp.bfloat16
SDS = jax.ShapeDtypeStruct

EPS = 1e-6
CHUNK = 64
HD = 128
GDN_QK_HEADS = 8
GDN_V_HEADS = 16
GDN_QK_W = GDN_QK_HEADS * HD
GDN_V_W = GDN_V_HEADS * HD
GDN_CONV = 2 * GDN_QK_W + GDN_V_W
GDN_IN = GDN_CONV + GDN_V_W + 2 * GDN_V_HEADS
GDN_IN_PAD = GDN_CONV + GDN_V_W + 256
GDN_TN = 1280
FOX_H = 16
FOX_D = 64
FOX_W = FOX_H * FOX_D
FOX_IN = 4 * FOX_W + FOX_H
FOX_IN_PAD = 4 * FOX_W + 128
FOX_TN = 1408
FOX_PW = FOX_H * 128
NDEV = 8
MESH_AXES = ("x", "y", "c")
NEG = -1e30

ADAM_LR = 0.001
ADAM_B1 = 0.9
ADAM_B2 = 0.999
ADAM_EPS = 1e-08
ADAM_WD = 0.01
ADAM_STEP = 10

VMEM_LIMIT = 56 * 1024 * 1024


def _cp(sem=None):
    return pltpu.CompilerParams(dimension_semantics=sem, vmem_limit_bytes=VMEM_LIMIT)


def _bdot(a, b, dims):
    return lax.dot_general(a.astype(bf16), b.astype(bf16), (dims, ((), ())), preferred_element_type=f32)


def _nn(a, b):
    return _bdot(a, b, ((1,), (0,)))


def _nt(a, b):
    return _bdot(a, b, ((1,), (1,)))


def _tn(a, b):
    return _bdot(a, b, ((0,), (0,)))


def _hdot(a, b, dims=((1,), (0,))):
    return lax.dot_general(a, b, (dims, ((), ())), precision=lax.Precision.HIGHEST, preferred_element_type=f32)


def _split2(a):
    hi = a.astype(bf16)
    return hi, (a - hi.astype(f32)).astype(bf16)


def _dot3(a, b):
    (ah, al), (bh, bl) = a, b
    n = ah.shape[0]
    both = jnp.dot(jnp.concatenate([ah, al], axis=0), bh, preferred_element_type=f32)
    return both[:n] + both[n:] + jnp.dot(ah, bl, preferred_element_type=f32)


@jax.custom_vjp
def _mm(a, b):
    return _nn(a, b)


_mm.defvjp(lambda a, b: (_nn(a, b), (a, b)), lambda r, g: (_nt(g, r[1]), _tn(r[0], g)))


@jax.custom_vjp
def _mm_nt(a, b):
    return _nt(a, b)


_mm_nt.defvjp(lambda a, b: (_nt(a, b), (a, b)), lambda r, g: (_nn(g, r[1]), _tn(g, r[0])))


@jax.custom_vjp
def _mm_tn(a, b):
    return _tn(a, b)


_mm_tn.defvjp(lambda a, b: (_tn(a, b), (a, b)), lambda r, g: (_nt(r[1], g), _nn(r[0], g)))


def _silu(x):
    return x * jax.nn.sigmoid(x)


def _rms_mod(x, nw, scale, shift):
    r = lax.rsqrt(jnp.mean(x * x, axis=-1, keepdims=True) + EPS)
    return (x * r * nw) * (1.0 + scale) + shift


def _rows(S, want):
    return min(want, S)


def _my_pos():
    return lax.axis_index("x"), lax.axis_index("y"), lax.axis_index("c")


def _exchange_copies(kind, x_refs, out_refs, send_sems, recv_sems, local_sems):
    x_, y_, c_ = _my_pos()
    me = 4 * x_ + 2 * y_ + c_
    own = kind == "gather"
    cps = [pltpu.make_async_copy(x_refs[a] if own else x_refs[a].at[me], out_refs[a].at[me], local_sems.at[a])
           for a in range(len(x_refs))]
    for rel in range(1, NDEV):
        px = (x_ + ((rel >> 2) & 1)) % 2
        py = (y_ + ((rel >> 1) & 1)) % 2
        pc = (c_ + (rel & 1)) % 2
        for a in range(len(x_refs)):
            cps.append(pltpu.make_async_remote_copy(
                src_ref=x_refs[a] if own else x_refs[a].at[4 * px + 2 * py + pc], dst_ref=out_refs[a].at[me],
                send_sem=send_sems.at[rel - 1, a], recv_sem=recv_sems.at[rel - 1, a],
                device_id=(px, py, pc), device_id_type=pl.DeviceIdType.MESH))
    return cps


def _exchange_scratch(n):
    return [pltpu.SemaphoreType.DMA((NDEV - 1, n)), pltpu.SemaphoreType.DMA((NDEV - 1, n)), pltpu.SemaphoreType.DMA((n,))]


def _call(body, *, name, grid, in_specs, out_specs, out_shape, args, scratch=(), side=None):
    params = _cp(("arbitrary",) * len(grid))
    if side is None:
        return pl.pallas_call(body, name=name, grid=grid, in_specs=in_specs, out_specs=out_specs, out_shape=out_shape,
                              scratch_shapes=list(scratch), compiler_params=params)(*args)
    kind, xs = side
    n_in, n_out, n_scr, ns = len(in_specs), len(out_shape), len(scratch), len(xs)
    steps = 1
    for g in grid:
        steps *= g

    def wrapped(*refs):
        o0 = n_in + ns
        s0 = o0 + n_out + ns
        step = pl.program_id(0)
        for d in range(1, len(grid)):
            step = step * grid[d] + pl.program_id(d)

        def copies():
            return _exchange_copies(kind, refs[n_in:o0], refs[o0 + n_out:s0], *refs[s0 + n_scr:])

        @pl.when(step == 0)
        def _():
            for cp in copies():
                cp.start()

        body(*refs[:n_in], *refs[o0:o0 + n_out], *refs[s0:s0 + n_scr])

        @pl.when(step == steps - 1)
        def _():
            for cp in copies():
                cp.wait()

    any_ = pl.BlockSpec(memory_space=pl.ANY)
    side_shapes = [SDS((NDEV,) + x.shape if kind == "gather" else x.shape, x.dtype) for x in xs]
    outs = pl.pallas_call(wrapped, name=name, grid=grid, in_specs=list(in_specs) + [any_] * ns,
                          out_specs=list(out_specs) + [any_] * ns, out_shape=list(out_shape) + side_shapes,
                          scratch_shapes=list(scratch) + _exchange_scratch(ns), compiler_params=params)(*args, *xs)
    return outs[:n_out], outs[n_out:]


def inproj_fwd(x, nw, scale, shift, W, tn, name, side=None):
    S, D = x.shape
    N = W.shape[1]
    tm = _rows(S, 1024)

    def body(x_ref, nw_ref, sc_ref, sh_ref, w_ref, proj_ref, h_ref):
        @pl.when(pl.program_id(1) == 0)
        def _():
            h_ref[...] = _rms_mod(x_ref[...], nw_ref[...], sc_ref[...], sh_ref[...]).astype(bf16)

        proj_ref[...] = jnp.dot(h_ref[...], w_ref[...], preferred_element_type=f32)

    vec = pl.BlockSpec((1, D), lambda i, j: (0, 0))
    return _call(
        body, name=name, grid=(S // tm, N // tn),
        in_specs=[pl.BlockSpec((tm, D), lambda i, j: (i, 0)), vec, vec, vec, pl.BlockSpec((D, tn), lambda i, j: (0, j))],
        out_specs=[pl.BlockSpec((tm, tn), lambda i, j: (i, j)), pl.BlockSpec((tm, D), lambda i, j: (i, 0))],
        out_shape=[SDS((S, N), f32), SDS((S, D), bf16)], args=(x, nw, scale, shift, W), side=side)


def inproj_bwd_x(x, nw, scale, shift, W, dproj, dx_res, tn, name, side=None):
    S, D = x.shape
    N = W.shape[1]
    tm = _rows(S, 1024)
    nj = N // tn

    def body(x_ref, nw_ref, sc_ref, sh_ref, w_ref, dp_ref, dxr_ref, dx_ref, dnw_ref, dsc_ref, dsh_ref, acc):
        i, j = pl.program_id(0), pl.program_id(1)

        @pl.when(j == 0)
        def _():
            acc[...] = jnp.zeros_like(acc)

        @pl.when((i == 0) & (j == 0))
        def _():
            dnw_ref[...] = jnp.zeros_like(dnw_ref)
            dsc_ref[...] = jnp.zeros_like(dsc_ref)
            dsh_ref[...] = jnp.zeros_like(dsh_ref)

        acc[...] += _nt(dp_ref[...], w_ref[...])

        @pl.when(j == nj - 1)
        def _():
            _, vjp = jax.vjp(_rms_mod, x_ref[...], nw_ref[...], sc_ref[...], sh_ref[...])
            dx, dnw, dsc, dsh = vjp(acc[...])
            dx_ref[...] = dxr_ref[...] + dx
            dnw_ref[...] += dnw
            dsc_ref[...] += dsc
            dsh_ref[...] += dsh

    vec = pl.BlockSpec((1, D), lambda i, j: (0, 0))
    row = pl.BlockSpec((tm, D), lambda i, j: (i, 0))
    return _call(
        body, name=name, grid=(S // tm, nj),
        in_specs=[row, vec, vec, vec, pl.BlockSpec((D, tn), lambda i, j: (0, j)), pl.BlockSpec((tm, tn), lambda i, j: (i, j)), row],
        out_specs=[row, vec, vec, vec],
        out_shape=[SDS((S, D), f32), SDS((1, D), f32), SDS((1, D), f32), SDS((1, D), f32)],
        scratch=[pltpu.VMEM((tm, D), f32)], args=(x, nw, scale, shift, W, dproj, dx_res), side=side)


def matmul_tn(a, b, tn, name, side=None):
    S, K = a.shape
    N = b.shape[1]
    tm = _rows(S, 1024)
    ni = S // tm

    def body(a_ref, b_ref, o_ref):
        @pl.when(pl.program_id(1) == 0)
        def _():
            o_ref[...] = jnp.zeros_like(o_ref)

        o_ref[...] += _tn(a_ref[...], b_ref[...])

    return _call(
        body, name=name, grid=(N // tn, ni),
        in_specs=[pl.BlockSpec((tm, K), lambda j, i: (i, 0)), pl.BlockSpec((tm, tn), lambda j, i: (i, j))],
        out_specs=[pl.BlockSpec((K, tn), lambda j, i: (0, j))],
        out_shape=[SDS((K, N), f32)], args=(a, b), side=side)


def _conv_taps(xs, w, n_out):
    taps = []
    for j in range(4):
        s = 3 - j
        sh = xs if s == 0 else pltpu.roll(xs, s, axis=0)
        taps.append(sh[8:8 + n_out])
    conv = taps[0] * w[0] + taps[1] * w[1] + taps[2] * w[2] + taps[3] * w[3]
    return taps, conv


def _act_norm(conv, mul):
    s = _silu(conv)
    return s * (mul * lax.rsqrt(jnp.sum(s * s, axis=-1, keepdims=True) + EPS))


def gdn_prep_fwd(proj, conv_w, name):
    S = proj.shape[0]
    R = _rows(S, 512)

    def body(x_ref, w_ref, o_ref):
        j = pl.program_id(0)
        w = [w_ref[t:t + 1, :] for t in range(4)]

        def sweep(act):
            def piece(r, c):
                t0 = pl.multiple_of(r * R, R)
                cur = x_ref[pl.ds(t0, R), :]
                prev = x_ref[pl.ds(pl.multiple_of(jnp.maximum(t0 - 8, 0), 8), 8), :]
                prev = jnp.where(r == 0, 0.0, prev)
                _, conv = _conv_taps(jnp.concatenate([prev, cur], axis=0), w, R)
                o_ref[pl.ds(t0, R), :] = act(conv)
                return c

            lax.fori_loop(0, S // R, piece, 0)

        @pl.when(j < 2 * GDN_QK_HEADS)
        def _():
            sweep(lambda c: _act_norm(c, jnp.where(j < GDN_QK_HEADS, HD ** -0.5, 1.0).astype(f32)))

        @pl.when(j >= 2 * GDN_QK_HEADS)
        def _():
            sweep(_silu)

    return pl.pallas_call(
        body, name=name, grid=(GDN_CONV // 128,),
        in_specs=[pl.BlockSpec((S, 128), lambda j: (0, j)), pl.BlockSpec((4, 128), lambda j: (0, j))],
        out_specs=pl.BlockSpec((S, 128), lambda j: (0, j)),
        out_shape=SDS((S, GDN_CONV), f32),
        compiler_params=_cp(("arbitrary",)),
    )(proj, conv_w)


def gdn_prep_bwd(proj, conv_w, dqkvc, dproj, name):
    S = proj.shape[0]
    R = _rows(S, 512)
    NP = S // R

    def body(x_ref, w_ref, dn_ref, _, dx_ref, dw_ref):
        jb = pl.program_id(0)
        w = [w_ref[j:j + 1, :] for j in range(4)]

        def piece(act, r, dw):
            t0 = pl.multiple_of(r * R, R)
            cur = x_ref[pl.ds(t0, R), :]
            prev = x_ref[pl.ds(pl.multiple_of(jnp.maximum(t0 - 8, 0), 8), 8), :]
            prev = jnp.where(r == 0, 0.0, prev)
            nxt0 = pl.multiple_of(jnp.minimum(t0 + R, S - 8), 8)
            nxt = x_ref[pl.ds(nxt0, 8), :]
            dn_cur = dn_ref[pl.ds(t0, R), :]
            dn_nxt = jnp.where(r == NP - 1, 0.0, dn_ref[pl.ds(nxt0, 8), :])
            xs = jnp.concatenate([prev, cur, nxt], axis=0)
            taps, conv = _conv_taps(xs, w, R + 8)
            dn = jnp.concatenate([dn_cur, dn_nxt], axis=0)
            _, vjp = jax.vjp(act, conv)
            dxc = vjp(dn)[0]
            n = R + 8
            dx = dxc[0:R] * w[3]
            for j in range(3):
                s = 3 - j
                dx = dx + pltpu.roll(dxc, n - s, axis=0)[0:R] * w[j]
            dx_ref[pl.ds(t0, R), :] = dx
            return tuple(dw[j] + jnp.sum(dxc[0:R] * taps[j][0:R], axis=0, keepdims=True) for j in range(4))

        def sweep(act):
            dw = lax.fori_loop(0, NP, functools.partial(piece, act), tuple(jnp.zeros((1, 128), f32) for _ in range(4)))
            for j in range(4):
                dw_ref[j:j + 1, :] = dw[j]

        @pl.when(jb < 2 * GDN_QK_HEADS)
        def _():
            sweep(lambda c: _act_norm(c, jnp.where(jb < GDN_QK_HEADS, HD ** -0.5, 1.0).astype(f32)))

        @pl.when(jb >= 2 * GDN_QK_HEADS)
        def _():
            sweep(_silu)

    col = pl.BlockSpec((S, 128), lambda j: (0, j))
    wsp = pl.BlockSpec((4, 128), lambda j: (0, j))
    return pl.pallas_call(
        body, name=name, grid=(GDN_CONV // 128,),
        in_specs=[col, wsp, col, pl.BlockSpec(memory_space=pl.ANY)], out_specs=[col, wsp],
        out_shape=[SDS(dproj.shape, f32), SDS((4, GDN_CONV), f32)],
        input_output_aliases={3: 0},
        compiler_params=_cp(("arbitrary",)),
    )(proj, conv_w, dqkvc, dproj)


def _chunk_tril(R):
    ii = lax.broadcasted_iota(jnp.int32, (R, R), 0)
    jj = lax.broadcasted_iota(jnp.int32, (R, R), 1)
    return ((ii // CHUNK == jj // CHUNK) & (ii >= jj)).astype(f32)


def _gdn_gates(b, a, A_log, dt_bias, tril):
    beta = jax.nn.sigmoid(b)
    g = -jnp.exp(A_log) * jax.nn.softplus(a + dt_bias)
    return _hdot(tril, g), beta


_GDN_B_BLK = (GDN_CONV + GDN_V_W) // 128
_GDN_A_BLK = _GDN_B_BLK + 1


def gdn_gates_fwd(proj, A_log, dt_bias, name):
    S = proj.shape[0]
    R = _rows(S, 512)

    def body(b_ref, a_ref, al_ref, dt_ref, gc_ref, be_ref):
        gc, be = _gdn_gates(b_ref[...], a_ref[...], al_ref[...], dt_ref[...], _chunk_tril(R))
        gc_ref[...] = gc
        be_ref[...] = be

    vec = pl.BlockSpec((1, 128), lambda i: (0, 0))
    blk = pl.BlockSpec((R, 128), lambda i: (i, 0))
    return pl.pallas_call(
        body, name=name, grid=(S // R,),
        in_specs=[pl.BlockSpec((R, 128), lambda i: (i, _GDN_B_BLK)), pl.BlockSpec((R, 128), lambda i: (i, _GDN_A_BLK)), vec, vec],
        out_specs=[blk, blk], out_shape=[SDS((S, 128), f32), SDS((S, 128), f32)],
        compiler_params=_cp(("arbitrary",)),
    )(proj, proj, A_log, dt_bias)


def gdn_gates_bwd(proj, A_log, dt_bias, dgc, dbeta, dproj, name):
    S = proj.shape[0]
    R = _rows(S, 512)

    def body(b_ref, a_ref, al_ref, dt_ref, dgc_ref, dbe_ref, _, dp_ref, dal_ref, ddt_ref):
        @pl.when(pl.program_id(0) == 0)
        def _():
            dal_ref[...] = jnp.zeros_like(dal_ref)
            ddt_ref[...] = jnp.zeros_like(ddt_ref)

        tril = _chunk_tril(R)
        _, vjp = jax.vjp(lambda b, a, al, dt: _gdn_gates(b, a, al, dt, tril), b_ref[...], a_ref[...], al_ref[...], dt_ref[...])
        db, da, dal, ddt = vjp((dgc_ref[...], dbe_ref[...]))
        dp_ref[:, 0:128] = db
        dp_ref[:, 128:256] = da
        dal_ref[...] += dal
        ddt_ref[...] += ddt

    vec = pl.BlockSpec((1, 128), lambda i: (0, 0))
    blk = pl.BlockSpec((R, 128), lambda i: (i, 0))
    return pl.pallas_call(
        body, name=name, grid=(S // R,),
        in_specs=[pl.BlockSpec((R, 128), lambda i: (i, _GDN_B_BLK)), pl.BlockSpec((R, 128), lambda i: (i, _GDN_A_BLK)), vec, vec, blk, blk,
                  pl.BlockSpec(memory_space=pl.ANY)],
        out_specs=[pl.BlockSpec((R, 256), lambda i: (i, _GDN_B_BLK // 2)), vec, vec],
        out_shape=[SDS(dproj.shape, f32), SDS((1, 128), f32), SDS((1, 128), f32)],
        input_output_aliases={6: 0},
        compiler_params=_cp(("arbitrary",)),
    )(proj, proj, A_log, dt_bias, dgc, dbeta, dproj)


@jax.custom_vjp
def _inv_given(L, T):
    return T


def _inv_given_bwd(T, ct):
    dL = -_nt(_tn(T, ct), T)
    return dL, jnp.zeros_like(T)


_inv_given.defvjp(lambda L, T: (T, T), _inv_given_bwd)


REP = GDN_V_HEADS // GDN_QK_HEADS


def _gdn_intra_all(qs, ks, vs, gcols, bcols, Ts=None):
    H = len(vs)
    C = vs[0].shape[0]
    ii = lax.broadcasted_iota(jnp.int32, (C, C), 0)
    jj = lax.broadcasted_iota(jnp.int32, (C, C), 1)
    grows = [jnp.sum(jnp.where(ii == jj, g, 0.0), axis=0, keepdims=True) for g in gcols]
    decs = [jnp.exp(jnp.where(ii >= jj, gcols[h] - grows[h], NEG)) for h in range(H)]
    kbs = [ks[h // REP] * bcols[h] for h in range(H)]
    As = [_mm_nt(kbs[h], ks[h // REP]) for h in range(H)]
    Ls = [jnp.where(ii > jj, As[h] * decs[h], 0.0) for h in range(H)]
    if Ts is None:
        T = _neumann_inv_batched(Ls)
    else:
        T = [_inv_given(Ls[h], Ts[h]) for h in range(H)]
    us = [_mm(T[h], vs[h] * bcols[h]) for h in range(H)]
    ws = [_mm(T[h], kbs[h] * jnp.exp(gcols[h])) for h in range(H)]
    qk = [_mm_nt(qs[p], ks[p]) for p in range(H // REP)]
    return us, ws, [qk[h // REP] * decs[h] for h in range(H)], T


def _neumann_inv_batched(Ls):
    n, C = 4, Ls[0].shape[0]
    r0 = lax.broadcasted_iota(jnp.int32, (n * C, n * C), 0)
    c0 = lax.broadcasted_iota(jnp.int32, (n * C, n * C), 1)
    same = (r0 // C) == (c0 // C)

    def blockdiag(split):
        return tuple(jnp.where(same, jnp.concatenate([x] * n, axis=0), jnp.zeros((), bf16)) for x in split)

    Ms = [jnp.concatenate(Ls[b:b + n], axis=1) for b in range(0, len(Ls), n)]
    eye = (lax.broadcasted_iota(jnp.int32, (C, n * C), 0) == (lax.broadcasted_iota(jnp.int32, (C, n * C), 1) & (C - 1))).astype(f32)
    Ps = [eye - M for M in Ms]
    Ss = [_split2(M) for M in Ms]
    Bs = [blockdiag(S) for S in Ss]
    k = 1
    while 2 * k < C:
        Ss = [_split2(_dot3(S, B)) for S, B in zip(Ss, Bs)]
        Bs = [blockdiag(S) for S in Ss]
        Ps = [P + _dot3(_split2(P), B) for P, B in zip(Ps, Bs)]
        k *= 2
    return [P[:, h * C:(h + 1) * C] for P in Ps for h in range(n)]


def _gdn_scan_all(qs, ks, gcols, us, ws, attns, S0s):
    H = len(us)
    C = us[0].shape[0]
    last = lax.broadcasted_iota(jnp.int32, (C, 1), 0) == C - 1
    glast = [jnp.sum(jnp.where(last, g, 0.0), axis=0, keepdims=True) for g in gcols]
    wS = [_mm(ws[h], S0s[h]) for h in range(H)]
    qS = [_mm(qs[h // REP] * jnp.exp(gcols[h]), S0s[h]) for h in range(H)]
    vn = [us[h] - wS[h] for h in range(H)]
    av = [_mm(attns[h], vn[h]) for h in range(H)]
    kv = [_mm_tn(ks[h // REP] * jnp.exp(glast[h] - gcols[h]), vn[h]) for h in range(H)]
    return [qS[h] + av[h] for h in range(H)], [S0s[h] * jnp.exp(glast[h]) + kv[h] for h in range(H)]


def _head_cols(blk):
    lane = lax.broadcasted_iota(jnp.int32, blk.shape, 1)
    return [jnp.sum(jnp.where(lane == h, blk, 0.0), axis=1, keepdims=True) for h in range(GDN_V_HEADS)]


def _head_lanes(cols):
    lane = lax.broadcasted_iota(jnp.int32, (cols[0].shape[0], 128), 1)
    out = jnp.zeros((cols[0].shape[0], 128), f32)
    for h, c in enumerate(cols):
        out = out + jnp.where(lane == h, c, 0.0)
    return out


CPS = 2


def _rows_of(c):
    return slice(c * CHUNK, (c + 1) * CHUNK)


def _heads(ref, n, c):
    return [ref[_rows_of(c), h * HD:(h + 1) * HD].astype(f32) for h in range(n)]


def _mats(ref, c):
    return [ref[c, h].astype(f32) for h in range(GDN_V_HEADS)]


def _gdn_specs(NB, rv=None):
    ix = (lambda n: n) if rv is None else rv
    R = CPS * CHUNK
    qs = pl.BlockSpec((R, GDN_QK_W), lambda n: (ix(n), 0))
    ks = pl.BlockSpec((R, GDN_QK_W), lambda n: (ix(n), 1))
    vs = pl.BlockSpec((R, GDN_V_W), lambda n: (ix(n), 1))
    g1 = pl.BlockSpec((R, 128), lambda n: (ix(n), 0))
    wide = pl.BlockSpec((R, GDN_V_W), lambda n: (ix(n), 0))
    sq = pl.BlockSpec((CPS, GDN_V_HEADS, CHUNK, CHUNK), lambda n: (ix(n), 0, 0, 0))
    st = pl.BlockSpec((CPS, GDN_V_HEADS, HD, HD), lambda n: (ix(n), 0, 0, 0))
    return qs, ks, vs, g1, wide, sq, st


def gdn_intra_fwd(qkvc, gc, beta, name, side=None):
    S = qkvc.shape[0]
    NC = S // CHUNK

    def body(q_ref, k_ref, v_ref, gc_ref, be_ref, u_ref, w_ref, at_ref, T_ref):
        for c in range(CPS):
            rows = _rows_of(c)
            us, ws, attns, Ts = _gdn_intra_all(_heads(q_ref, GDN_QK_HEADS, c), _heads(k_ref, GDN_QK_HEADS, c), _heads(v_ref, GDN_V_HEADS, c),
                                               _head_cols(gc_ref[rows, :]), _head_cols(be_ref[rows, :]))
            for h in range(GDN_V_HEADS):
                u_ref[rows, h * HD:(h + 1) * HD] = us[h]
                w_ref[rows, h * HD:(h + 1) * HD] = ws[h].astype(bf16)
                at_ref[c, h] = attns[h].astype(bf16)
                T_ref[c, h] = Ts[h].astype(bf16)

    qs, ks, vs, g1, wide, sq, _ = _gdn_specs(NC // CPS)
    return _call(
        body, name=name, grid=(NC // CPS,),
        in_specs=[qs, ks, vs, g1, g1], out_specs=[wide, wide, sq, sq],
        out_shape=[SDS((S, GDN_V_W), f32), SDS((S, GDN_V_W), bf16),
                   SDS((NC, GDN_V_HEADS, CHUNK, CHUNK), bf16), SDS((NC, GDN_V_HEADS, CHUNK, CHUNK), bf16)],
        args=(qkvc, qkvc, qkvc, gc, beta), side=side)


def gdn_scan_fwd(qkvc, gc, u, w, attn, name, side=None):
    S = qkvc.shape[0]
    NC = S // CHUNK

    def body(q_ref, k_ref, gc_ref, u_ref, w_ref, at_ref, o_ref, st_ref, state):
        @pl.when(pl.program_id(0) == 0)
        def _():
            state[...] = jnp.zeros_like(state)

        for c in range(CPS):
            rows = _rows_of(c)
            S0s = [state[h] for h in range(GDN_V_HEADS)]
            os_, S1s = _gdn_scan_all(_heads(q_ref, GDN_QK_HEADS, c), _heads(k_ref, GDN_QK_HEADS, c), _head_cols(gc_ref[rows, :]),
                                     _heads(u_ref, GDN_V_HEADS, c), _heads(w_ref, GDN_V_HEADS, c), _mats(at_ref, c), S0s)
            for h in range(GDN_V_HEADS):
                o_ref[rows, h * HD:(h + 1) * HD] = os_[h]
                st_ref[c, h] = S0s[h].astype(bf16)
                state[h] = S1s[h]

    qs, ks, _, g1, wide, sq, st = _gdn_specs(NC // CPS)
    return _call(
        body, name=name, grid=(NC // CPS,),
        in_specs=[qs, ks, g1, wide, wide, sq], out_specs=[wide, st],
        out_shape=[SDS((S, GDN_V_W), f32), SDS((NC, GDN_V_HEADS, HD, HD), bf16)],
        scratch=[pltpu.VMEM((GDN_V_HEADS, HD, HD), f32)], args=(qkvc, qkvc, gc, u, w, attn), side=side)


def gdn_scan_bwd(qkvc, gc, u, w, attn, states, do, name, side=None):
    S = qkvc.shape[0]
    NC = S // CHUNK
    NB = NC // CPS

    def body(q_ref, k_ref, gc_ref, u_ref, w_ref, at_ref, st_ref, do_ref,
             dq_ref, dk_ref, dgc_ref, du_ref, dw_ref, dat_ref, dstate):
        @pl.when(pl.program_id(0) == 0)
        def _():
            dstate[...] = jnp.zeros_like(dstate)

        VH = range(GDN_V_HEADS)
        for c in reversed(range(CPS)):
            rows = _rows_of(c)
            _, vjp = jax.vjp(_gdn_scan_all, _heads(q_ref, GDN_QK_HEADS, c), _heads(k_ref, GDN_QK_HEADS, c), _head_cols(gc_ref[rows, :]),
                             _heads(u_ref, GDN_V_HEADS, c), _heads(w_ref, GDN_V_HEADS, c), _mats(at_ref, c), _mats(st_ref, c))
            dqs, dks, dgs, dus, dws, dats, dS0s = vjp((_heads(do_ref, GDN_V_HEADS, c), [dstate[h] for h in VH]))
            for p in range(GDN_QK_HEADS):
                dq_ref[rows, p * HD:(p + 1) * HD] = dqs[p]
                dk_ref[rows, p * HD:(p + 1) * HD] = dks[p]
            for h in VH:
                du_ref[rows, h * HD:(h + 1) * HD] = dus[h].astype(bf16)
                dw_ref[rows, h * HD:(h + 1) * HD] = dws[h].astype(bf16)
                dat_ref[c, h] = dats[h].astype(bf16)
                dstate[h] = dS0s[h]
            dgc_ref[rows, :] = _head_lanes(dgs)

    qs, ks, _, g1, wide, sq, st = _gdn_specs(NB, lambda n: NB - 1 - n)
    dqs = pl.BlockSpec((CPS * CHUNK, GDN_QK_W), lambda n: (NB - 1 - n, 0))
    return _call(
        body, name=name, grid=(NB,),
        in_specs=[qs, ks, g1, wide, wide, sq, st, wide],
        out_specs=[dqs, dqs, g1, wide, wide, sq],
        out_shape=[SDS((S, GDN_QK_W), f32), SDS((S, GDN_QK_W), f32), SDS((S, 128), f32), SDS((S, GDN_V_W), bf16),
                   SDS((S, GDN_V_W), bf16), SDS((NC, GDN_V_HEADS, CHUNK, CHUNK), bf16)],
        scratch=[pltpu.VMEM((GDN_V_HEADS, HD, HD), f32)], args=(qkvc, qkvc, gc, u, w, attn, states, do), side=side)


def gdn_intra_bwd(qkvc, gc, beta, Ts, du, dw, dattn, dq_s, dk_s, dgc_s, name, side=None):
    S = qkvc.shape[0]
    NC = S // CHUNK

    def body(q_ref, k_ref, v_ref, gc_ref, be_ref, T_ref, du_ref, dw_ref, dat_ref, dqs_ref, dks_ref, dgs_ref,
             dqkv_ref, dgc_ref, dbe_ref):
        VH = range(GDN_V_HEADS)
        for c in range(CPS):
            rows = _rows_of(c)
            Ts = _mats(T_ref, c)
            _, vjp = jax.vjp(lambda q_, k_, v_, g_, b_: _gdn_intra_all(q_, k_, v_, g_, b_, Ts)[:3],
                             _heads(q_ref, GDN_QK_HEADS, c), _heads(k_ref, GDN_QK_HEADS, c), _heads(v_ref, GDN_V_HEADS, c),
                             _head_cols(gc_ref[rows, :]), _head_cols(be_ref[rows, :]))
            dqs, dks, dvs, dgs, dbs = vjp((_heads(du_ref, GDN_V_HEADS, c), _heads(dw_ref, GDN_V_HEADS, c), _mats(dat_ref, c)))
            for p in range(GDN_QK_HEADS):
                dqkv_ref[rows, p * HD:(p + 1) * HD] = dqs[p] + dqs_ref[rows, p * HD:(p + 1) * HD]
                dqkv_ref[rows, GDN_QK_W + p * HD:GDN_QK_W + (p + 1) * HD] = dks[p] + dks_ref[rows, p * HD:(p + 1) * HD]
            for h in VH:
                dqkv_ref[rows, 2 * GDN_QK_W + h * HD:2 * GDN_QK_W + (h + 1) * HD] = dvs[h]
            dgc_ref[rows, :] = _head_lanes(dgs) + dgs_ref[rows, :]
            dbe_ref[rows, :] = _head_lanes(dbs)

    qs, ks, vs, g1, wide, sq, _ = _gdn_specs(NC // CPS)
    dqs = pl.BlockSpec((CPS * CHUNK, GDN_QK_W), lambda n: (n, 0))
    return _call(
        body, name=name, grid=(NC // CPS,),
        in_specs=[qs, ks, vs, g1, g1, sq, wide, wide, sq, dqs, dqs, g1],
        out_specs=[pl.BlockSpec((CPS * CHUNK, GDN_CONV), lambda n: (n, 0)), g1, g1],
        out_shape=[SDS((S, GDN_CONV), f32), SDS((S, 128), f32), SDS((S, 128), f32)],
        args=(qkvc, qkvc, qkvc, gc, beta, Ts, du, dw, dattn, dq_s, dk_s, dgc_s), side=side)


def _gated_norm(o, z, nw):
    parts = []
    for h in range(GDN_V_HEADS):
        oh = o[:, h * HD:(h + 1) * HD]
        r = lax.rsqrt(jnp.mean(oh * oh, axis=-1, keepdims=True) + EPS)
        parts.append((oh * r * nw) * _silu(z[:, h * HD:(h + 1) * HD]))
    return jnp.concatenate(parts, axis=1)


def gdn_out_fwd(o, proj, nw, W, x, gate, name):
    S, D = x.shape
    tm = _rows(S, 256)

    def body(o_ref, z_ref, nw_ref, w_ref, x_ref, g_ref, xn_ref, y_ref, og_ref):
        og = _gated_norm(o_ref[...], z_ref[...], nw_ref[...]).astype(bf16)
        y = jnp.dot(og, w_ref[...], preferred_element_type=f32)
        og_ref[...] = og
        y_ref[...] = y
        xn_ref[...] = x_ref[...] + g_ref[...] * y

    row = pl.BlockSpec((tm, D), lambda i: (i, 0))
    wide = pl.BlockSpec((tm, GDN_V_W), lambda i: (i, 0))
    return pl.pallas_call(
        body, name=name, grid=(S // tm,),
        in_specs=[wide, pl.BlockSpec((tm, GDN_V_W), lambda i: (i, 2)), pl.BlockSpec((1, HD), lambda i: (0, 0)),
                  pl.BlockSpec((GDN_V_W, D), lambda i: (0, 0)), row, pl.BlockSpec((1, D), lambda i: (0, 0))],
        out_specs=[row, row, wide],
        out_shape=[SDS((S, D), f32), SDS((S, D), f32), SDS((S, GDN_V_W), bf16)],
        compiler_params=_cp(("arbitrary",)),
    )(o, proj, nw, W, x, gate)


def gdn_out_bwd(dxn, y, gate, o, proj, nw, W, name):
    S, D = dxn.shape
    tm = _rows(S, 256)

    def body(dx_ref, y_ref, g_ref, o_ref, z_ref, nw_ref, w_ref, dy_ref, dg_ref, do_ref, dz_ref, dnw_ref):
        @pl.when(pl.program_id(0) == 0)
        def _():
            dg_ref[...] = jnp.zeros_like(dg_ref)
            dnw_ref[...] = jnp.zeros_like(dnw_ref)

        dx = dx_ref[...]
        dy = dx * g_ref[...]
        dy_ref[...] = dy
        dg_ref[...] += jnp.sum(dx * y_ref[...], axis=0, keepdims=True)
        dog = _nt(dy, w_ref[...])
        _, vjp = jax.vjp(_gated_norm, o_ref[...], z_ref[...], nw_ref[...])
        do, dz, dnw = vjp(dog)
        do_ref[...] = do
        dz_ref[...] = dz
        dnw_ref[...] += dnw

    row = pl.BlockSpec((tm, D), lambda i: (i, 0))
    wide = pl.BlockSpec((tm, GDN_V_W), lambda i: (i, 0))
    vecd = pl.BlockSpec((1, D), lambda i: (0, 0))
    vech = pl.BlockSpec((1, HD), lambda i: (0, 0))
    return pl.pallas_call(
        body, name=name, grid=(S // tm,),
        in_specs=[row, row, vecd, wide, pl.BlockSpec((tm, GDN_V_W), lambda i: (i, 2)), vech, pl.BlockSpec((GDN_V_W, D), lambda i: (0, 0))],
        out_specs=[row, vecd, wide, pl.BlockSpec((tm, GDN_V_W), lambda i: (i, 2)), vech],
        out_shape=[SDS((S, D), f32), SDS((1, D), f32), SDS((S, GDN_V_W), f32), SDS((S, GDN_IN_PAD), f32), SDS((1, HD), f32)],
        compiler_params=_cp(("arbitrary",)),
    )(dxn, y, gate, o, proj, nw, W)


def _rms_w(x, w):
    return (x * lax.rsqrt(jnp.mean(x * x, axis=-1, keepdims=True) + EPS)) * w


def _split3(c):
    hi = c.astype(bf16).astype(f32)
    r1 = c - hi
    mid = r1.astype(bf16).astype(f32)
    lo = (r1 - mid).astype(bf16).astype(f32)
    return hi, mid, lo


_FOX_F_BLK = 4 * FOX_W // 128


def fox_prep_fwd(proj, f_bias, qn_w, kn_w, name):
    S = proj.shape[0]
    tm = _rows(S, 256)

    def body(q_ref, k_ref, v_ref, f_ref, fb_ref, qw_ref, kw_ref, Q_ref, K_ref, V_ref, carry):
        @pl.when(pl.program_id(0) == 0)
        def _():
            carry[...] = jnp.zeros_like(carry)

        ii = lax.broadcasted_iota(jnp.int32, (tm, tm), 0)
        jj = lax.broadcasted_iota(jnp.int32, (tm, tm), 1)
        lf = jax.nn.log_sigmoid(f_ref[...] + fb_ref[...])
        cum = _hdot((ii >= jj).astype(f32), lf) + carry[...]
        carry[...] = cum[tm - 1:tm, :]
        lane = lax.broadcasted_iota(jnp.int32, (tm, 128), 1)
        lo = lane < FOX_D
        qw2 = jnp.concatenate([qw_ref[...], qw_ref[...]], axis=1) * FOX_D ** -0.5
        kw2 = jnp.concatenate([kw_ref[...], kw_ref[...]], axis=1)

        def norm_pair(x, w2):
            x2 = x * x
            s_all = jnp.sum(x2, axis=1, keepdims=True)
            s_lo = jnp.sum(jnp.where(lo, x2, 0.0), axis=1, keepdims=True)
            r = jnp.where(lo, lax.rsqrt(s_lo * (1.0 / FOX_D) + EPS), lax.rsqrt((s_all - s_lo) * (1.0 / FOX_D) + EPS))
            return x * r * w2

        for p in range(FOX_H // 2):
            ps = slice(p * 128, (p + 1) * 128)
            yq, yk, xv = norm_pair(q_ref[:, ps], qw2), norm_pair(k_ref[:, ps], kw2), v_ref[:, ps]
            for e in range(2):
                h = 2 * p + e
                hi, mid, lw = _split3(cum[:, h:h + 1])
                eq = jnp.where(lane == FOX_D, hi, jnp.where(lane == FOX_D + 1, mid, jnp.where(lane == FOX_D + 2, lw, jnp.where(lane < FOX_D + 6, 1.0, 0.0))))
                ek = jnp.where(lane < FOX_D + 3, 1.0, jnp.where(lane == FOX_D + 3, -hi, jnp.where(lane == FOX_D + 4, -mid, jnp.where(lane == FOX_D + 5, -lw, 0.0))))
                ev = jnp.where(lane == FOX_D, 1.0, 0.0)
                mv = (lambda a: a) if e == 0 else (lambda a: pltpu.roll(a, FOX_D, axis=1))
                Q_ref[:, h * 128:(h + 1) * 128] = jnp.where(lo, mv(yq), eq).astype(bf16)
                K_ref[:, h * 128:(h + 1) * 128] = jnp.where(lo, mv(yk), ek).astype(bf16)
                V_ref[:, h * 128:(h + 1) * 128] = jnp.where(lo, mv(xv), ev).astype(bf16)

    def colblk(c):
        return pl.BlockSpec((tm, FOX_W), lambda i: (i, c))

    pad = pl.BlockSpec((tm, FOX_PW), lambda i: (i, 0))
    return pl.pallas_call(
        body, name=name, grid=(S // tm,),
        in_specs=[colblk(0), colblk(1), colblk(2), pl.BlockSpec((tm, 128), lambda i: (i, _FOX_F_BLK)),
                  pl.BlockSpec((1, 128), lambda i: (0, 0)), pl.BlockSpec((1, FOX_D), lambda i: (0, 0)), pl.BlockSpec((1, FOX_D), lambda i: (0, 0))],
        out_specs=[pad, pad, pad],
        out_shape=[SDS((S, FOX_PW), bf16)] * 3,
        scratch_shapes=[pltpu.VMEM((1, 128), f32)],
        compiler_params=_cp(("arbitrary",)),
    )(proj, proj, proj, proj, f_bias, qn_w, kn_w)


def fox_prep_bwd(proj, f_bias, qn_w, kn_w, dQ, dK, dV, dz, name):
    S = proj.shape[0]
    tm = _rows(S, 256)
    NB = S // tm

    def body(q_ref, k_ref, f_ref, fb_ref, qw_ref, kw_ref, dQ_ref, dK_ref, dV_ref, dz_ref,
             dp_ref, dfb_ref, dqw_ref, dkw_ref, carry):
        @pl.when(pl.program_id(0) == 0)
        def _():
            carry[...] = jnp.zeros_like(carry)
            dfb_ref[...] = jnp.zeros_like(dfb_ref)
            dqw_ref[...] = jnp.zeros_like(dqw_ref)
            dkw_ref[...] = jnp.zeros_like(dkw_ref)

        lane = lax.broadcasted_iota(jnp.int32, (tm, 128), 1)
        lo = lane < FOX_D
        qw2 = jnp.concatenate([qw_ref[...], qw_ref[...]], axis=1) * FOX_D ** -0.5
        kw2 = jnp.concatenate([kw_ref[...], kw_ref[...]], axis=1)

        def pair(ref, p):
            return jnp.where(lo, ref[:, 2 * p * 128:(2 * p + 1) * 128], pltpu.roll(ref[:, (2 * p + 1) * 128:(2 * p + 2) * 128], FOX_D, axis=1))

        def norm_pair_bwd(x, w2, dy):
            x2 = x * x
            s_all = jnp.sum(x2, axis=1, keepdims=True)
            s_lo = jnp.sum(jnp.where(lo, x2, 0.0), axis=1, keepdims=True)
            r = jnp.where(lo, lax.rsqrt(s_lo * (1.0 / FOX_D) + EPS), lax.rsqrt((s_all - s_lo) * (1.0 / FOX_D) + EPS))
            t = dy * w2 * x
            t_all = jnp.sum(t, axis=1, keepdims=True)
            t_lo = jnp.sum(jnp.where(lo, t, 0.0), axis=1, keepdims=True)
            dx = r * (w2 * dy - x * (r * r) * (jnp.where(lo, t_lo, t_all - t_lo) * (1.0 / FOX_D)))
            return dx, jnp.sum(dy * x * r, axis=0, keepdims=True)

        dcum = jnp.zeros((tm, 128), f32)
        dqw2 = jnp.zeros((1, 128), f32)
        dkw2 = jnp.zeros((1, 128), f32)
        for p in range(FOX_H // 2):
            ps = slice(p * 128, (p + 1) * 128)
            dxq, dw1 = norm_pair_bwd(q_ref[:, ps], qw2, pair(dQ_ref, p))
            dxk, dw2 = norm_pair_bwd(k_ref[:, ps], kw2, pair(dK_ref, p))
            dp_ref[:, p * 128:(p + 1) * 128] = dxq
            dp_ref[:, FOX_W + p * 128:FOX_W + (p + 1) * 128] = dxk
            dp_ref[:, 2 * FOX_W + p * 128:2 * FOX_W + (p + 1) * 128] = pair(dV_ref, p)
            dqw2 = dqw2 + dw1
            dkw2 = dkw2 + dw2
            for e in range(2):
                h = 2 * p + e
                dcum = dcum + jnp.where(lane == h, dQ_ref[:, h * 128 + FOX_D:h * 128 + FOX_D + 1]
                                        - dK_ref[:, h * 128 + FOX_D + 3:h * 128 + FOX_D + 4], 0.0)
        dp_ref[:, 3 * FOX_W:4 * FOX_W] = dz_ref[...]
        ii = lax.broadcasted_iota(jnp.int32, (tm, tm), 0)
        jj = lax.broadcasted_iota(jnp.int32, (tm, tm), 1)
        dlf = _hdot((ii <= jj).astype(f32), dcum) + carry[...]
        carry[...] += jnp.sum(dcum, axis=0, keepdims=True)
        df = dlf * jax.nn.sigmoid(-(f_ref[...] + fb_ref[...]))
        dp_ref[:, 4 * FOX_W:FOX_IN_PAD] = df
        dfb_ref[...] += jnp.sum(df, axis=0, keepdims=True)
        dqw_ref[...] += (dqw2[:, :FOX_D] + dqw2[:, FOX_D:]) * FOX_D ** -0.5
        dkw_ref[...] += dkw2[:, :FOX_D] + dkw2[:, FOX_D:]

    rv = lambda i: NB - 1 - i

    def colblk(c):
        return pl.BlockSpec((tm, FOX_W), lambda i: (rv(i), c))

    pad = pl.BlockSpec((tm, FOX_PW), lambda i: (rv(i), 0))
    cmp_ = pl.BlockSpec((tm, FOX_W), lambda i: (rv(i), 0))
    v128 = pl.BlockSpec((1, 128), lambda i: (0, 0))
    v64 = pl.BlockSpec((1, FOX_D), lambda i: (0, 0))
    return pl.pallas_call(
        body, name=name, grid=(NB,),
        in_specs=[colblk(0), colblk(1), pl.BlockSpec((tm, 128), lambda i: (rv(i), _FOX_F_BLK)), v128, v64, v64, pad, pad, pad, cmp_],
        out_specs=[pl.BlockSpec((tm, FOX_IN_PAD), lambda i: (rv(i), 0)), v128, v64, v64],
        out_shape=[SDS((S, FOX_IN_PAD), f32), SDS((1, 128), f32), SDS((1, FOX_D), f32), SDS((1, FOX_D), f32)],
        scratch_shapes=[pltpu.VMEM((1, 128), f32)],
        compiler_params=_cp(("arbitrary",)),
    )(proj, proj, proj, f_bias, qn_w, kn_w, dQ, dK, dV, dz)


FOX_HB = 2


def _diag_mask(t):
    return lax.broadcasted_iota(jnp.int32, (t, t), 1) <= lax.broadcasted_iota(jnp.int32, (t, t), 0)


def fox_attn_fwd(Q, K, V, name, side=None):
    S = Q.shape[0]
    t = _rows(S, 512)

    HB = FOX_HB
    HS = [slice(h * 128, (h + 1) * 128) for h in range(HB)]

    def body(q_ref, k_ref, v_ref, o_ref, m_sc, acc_sc, s_sc):
        i = pl.program_id(1)
        qs = [q_ref[:, sl] for sl in HS]
        m_sc[...] = jnp.full_like(m_sc, NEG)
        acc_sc[...] = jnp.zeros_like(acc_sc)

        def scores(j):
            j0 = pl.multiple_of(j * t, t)
            return [_nt(qs[h], k_ref[pl.ds(j0, t), HS[h]]) for h in range(HB)]

        def tile(j, diag):
            j0 = pl.multiple_of(j * t, t)
            ss = [s_sc[h] for h in range(HB)]
            if diag:
                ss = [jnp.where(_diag_mask(t), s, NEG) for s in ss]
            else:
                nxt = scores(j + 1)
            ms = [m_sc[h] for h in range(HB)]
            m_new = [jnp.maximum(ms[h], jnp.max(ss[h], axis=1, keepdims=True)) for h in range(HB)]
            ps = [jnp.exp(ss[h] - m_new[h]) for h in range(HB)]
            pv = [_nn(ps[h], v_ref[pl.ds(j0, t), HS[h]]) for h in range(HB)]
            for h in range(HB):
                acc_sc[h] = acc_sc[h] * jnp.exp(ms[h] - m_new[h]) + pv[h]
                m_sc[h] = m_new[h]
                if not diag:
                    s_sc[h] = nxt[h]

        def off_diag(j, c):
            tile(j, False)
            return c

        first = scores(0)
        for h in range(HB):
            s_sc[h] = first[h]
        lax.fori_loop(0, i, off_diag, 0)
        tile(i, True)
        lane = lax.broadcasted_iota(jnp.int32, (t, 128), 1)
        for h in range(HB):
            acc = acc_sc[h]
            l = acc[:, FOX_D:FOX_D + 1]
            o_ref[:, HS[h]] = jnp.where(lane == FOX_D, m_sc[h] + jnp.log(l), acc / l)

    blk = pl.BlockSpec((t, HB * 128), lambda h, i: (i, h))
    seq = pl.BlockSpec((S, HB * 128), lambda h, i: (0, h))
    return _call(
        body, name=name, grid=(FOX_H // HB, S // t),
        in_specs=[blk, seq, seq], out_specs=[blk], out_shape=[SDS((S, FOX_PW), f32)],
        scratch=[pltpu.VMEM((HB, t, 1), f32), pltpu.VMEM((HB, t, 128), f32), pltpu.VMEM((HB, t, t), f32)],
        args=(Q, K, V), side=side)


def fox_attn_bwd(Q, K, V, dO, O, name, side=None):
    S = Q.shape[0]
    t = _rows(S, 512)
    nq = S // t

    HB = FOX_HB
    HS = [slice(h * 128, (h + 1) * 128) for h in range(HB)]

    def body(k_ref, v_ref, q_ref, do_ref, o_ref, dq_ref, dk_ref, dv_ref):
        j = pl.program_id(1)

        @pl.when(j == 0)
        def _():
            dq_ref[...] = jnp.zeros_like(dq_ref)

        dk_ref[...] = jnp.zeros_like(dk_ref)
        dv_ref[...] = jnp.zeros_like(dv_ref)
        ks = [k_ref[:, sl] for sl in HS]
        vs = [v_ref[:, sl] for sl in HS]

        def tile(i, diag):
            i0 = pl.multiple_of(i * t, t)
            R = range(HB)
            qs = [q_ref[pl.ds(i0, t), HS[h]] for h in R]
            dos = [do_ref[pl.ds(i0, t), HS[h]] for h in R]
            ss = [_nt(qs[h], ks[h]) - o_ref[pl.ds(i0, t), h * 128 + FOX_D:h * 128 + FOX_D + 1] for h in R]
            if diag:
                ss = [jnp.where(_diag_mask(t), s, NEG) for s in ss]
            ps = [jnp.exp(s) for s in ss]
            dps = [_nt(dos[h], vs[h]) for h in R]
            dvs = [_tn(ps[h], dos[h]) for h in R]
            dss = [(ps[h] * dps[h]).astype(bf16) for h in R]
            dks = [_tn(dss[h], qs[h]) for h in R]
            dqs = [_nn(dss[h], ks[h]) for h in R]
            for h in R:
                dv_ref[:, HS[h]] += dvs[h]
                dk_ref[:, HS[h]] += dks[h]
                dq_ref[pl.ds(i0, t), HS[h]] += dqs[h]

        tile(j, True)

        def off_diag(i, c):
            tile(i, False)
            return c

        lax.fori_loop(j + 1, nq, off_diag, 0)

    blk = pl.BlockSpec((t, HB * 128), lambda h, j: (j, h))
    seq = pl.BlockSpec((S, HB * 128), lambda h, j: (0, h))
    return _call(
        body, name=name, grid=(FOX_H // HB, nq),
        in_specs=[blk, blk, seq, seq, seq], out_specs=[seq, blk, blk],
        out_shape=[SDS((S, FOX_PW), f32)] * 3, args=(K, V, Q, dO, O), side=side)


def fox_out_fwd(O, proj, W, x, gate, name):
    S, D = x.shape
    tm = _rows(S, 256)

    def body(o_ref, z_ref, w_ref, x_ref, g_ref, xn_ref, y_ref, og_ref):
        z = z_ref[...]
        og = jnp.concatenate([o_ref[:, h * 128:h * 128 + FOX_D] * _silu(z[:, h * FOX_D:(h + 1) * FOX_D]) for h in range(FOX_H)],
                             axis=1).astype(bf16)
        y = jnp.dot(og, w_ref[...], preferred_element_type=f32)
        og_ref[...] = og
        y_ref[...] = y
        xn_ref[...] = x_ref[...] + g_ref[...] * y

    row = pl.BlockSpec((tm, D), lambda i: (i, 0))
    cmp_ = pl.BlockSpec((tm, FOX_W), lambda i: (i, 0))
    return pl.pallas_call(
        body, name=name, grid=(S // tm,),
        in_specs=[pl.BlockSpec((tm, FOX_PW), lambda i: (i, 0)), pl.BlockSpec((tm, FOX_W), lambda i: (i, 3)),
                  pl.BlockSpec((FOX_W, D), lambda i: (0, 0)), row, pl.BlockSpec((1, D), lambda i: (0, 0))],
        out_specs=[row, row, cmp_],
        out_shape=[SDS((S, D), f32), SDS((S, D), f32), SDS((S, FOX_W), bf16)],
        compiler_params=_cp(("arbitrary",)),
    )(O, proj, W, x, gate)


def fox_out_bwd(dxn, y, gate, O, proj, W, name):
    S, D = dxn.shape
    tm = _rows(S, 256)

    def body(dx_ref, y_ref, g_ref, o_ref, z_ref, w_ref, dy_ref, dg_ref, dO_ref, dz_ref):
        @pl.when(pl.program_id(0) == 0)
        def _():
            dg_ref[...] = jnp.zeros_like(dg_ref)

        dx = dx_ref[...]
        dy = dx * g_ref[...]
        dy_ref[...] = dy
        dg_ref[...] += jnp.sum(dx * y_ref[...], axis=0, keepdims=True)
        dog = _nt(dy, w_ref[...])
        z = z_ref[...]
        lane = lax.broadcasted_iota(jnp.int32, (tm, FOX_D), 1)
        dzs = []
        for h in range(FOX_H):
            sl = slice(h * FOX_D, (h + 1) * FOX_D)
            zh = z[:, sl]
            sg = jax.nn.sigmoid(zh)
            oh = o_ref[:, h * 128:h * 128 + FOX_D]
            doh = dog[:, sl] * (zh * sg)
            delta = jnp.sum(doh * oh, axis=1, keepdims=True)
            dO_ref[:, h * 128:(h + 1) * 128] = jnp.concatenate([doh, jnp.where(lane == 0, -delta, 0.0)], axis=1).astype(bf16)
            dzs.append(dog[:, sl] * oh * (sg * (1.0 + zh * (1.0 - sg))))
        dz_ref[...] = jnp.concatenate(dzs, axis=1)

    row = pl.BlockSpec((tm, D), lambda i: (i, 0))
    vecd = pl.BlockSpec((1, D), lambda i: (0, 0))
    pad = pl.BlockSpec((tm, FOX_PW), lambda i: (i, 0))
    return pl.pallas_call(
        body, name=name, grid=(S // tm,),
        in_specs=[row, row, vecd, pad, pl.BlockSpec((tm, FOX_W), lambda i: (i, 3)), pl.BlockSpec((FOX_W, D), lambda i: (0, 0))],
        out_specs=[row, vecd, pad, pl.BlockSpec((tm, FOX_W), lambda i: (i, 0))],
        out_shape=[SDS((S, D), f32), SDS((1, D), f32), SDS((S, FOX_PW), bf16), SDS((S, FOX_W), f32)],
        compiler_params=_cp(("arbitrary",)),
    )(dxn, y, gate, O, proj, W)


def final_loss(x, fw, target, name):
    S, D = x.shape
    tm = _rows(S, 512)

    def body(x_ref, w_ref, t_ref, l_ref, dx_ref, dw_ref):
        @pl.when(pl.program_id(0) == 0)
        def _():
            l_ref[...] = jnp.zeros_like(l_ref)
            dw_ref[...] = jnp.zeros_like(dw_ref)

        out, vjp = jax.vjp(_rms_w, x_ref[...], w_ref[...])
        err = out - t_ref[...]
        l_ref[...] += 0.5 * jnp.sum(jnp.sum(err * err, axis=1, keepdims=True) * (1.0 / D), axis=0, keepdims=True)
        dx, dw = vjp(err * (1.0 / D))
        dx_ref[...] = dx
        dw_ref[...] += dw

    row = pl.BlockSpec((tm, D), lambda i: (i, 0))
    vec = pl.BlockSpec((1, D), lambda i: (0, 0))
    return pl.pallas_call(
        body, name=name, grid=(S // tm,),
        in_specs=[row, vec, row], out_specs=[pl.BlockSpec((1, 128), lambda i: (0, 0)), row, vec],
        out_shape=[SDS((1, 128), f32), SDS((S, D), f32), SDS((1, D), f32)],
        compiler_params=_cp(("arbitrary",)),
    )(x, fw, target)


def ada_fwd(c_all, ada_w, name):
    L, D, n = ada_w.shape

    def body(c_ref, w_ref, o_ref):
        cond = jnp.concatenate([_silu(c_ref[...]), jnp.zeros((8, D), f32)], axis=0)
        o_ref[0] = _nn(cond, w_ref[0])[0:8]

    return pl.pallas_call(
        body, name=name, grid=(L,),
        in_specs=[pl.BlockSpec((NDEV, D), lambda l: (0, 0)), pl.BlockSpec((1, D, n), lambda l: (l, 0, 0))],
        out_specs=pl.BlockSpec((1, NDEV, n), lambda l: (l, 0, 0)),
        out_shape=SDS((L, NDEV, n), f32),
        compiler_params=_cp(("arbitrary",)),
    )(c_all, ada_w)


def ada_grad(c_all, dmod, name):
    L, _, n = dmod.shape
    D = c_all.shape[1]

    def body(c_ref, d_ref, o_ref):
        cond = jnp.concatenate([_silu(c_ref[...]), jnp.zeros((8, D), f32)], axis=0)
        dm = jnp.concatenate([d_ref[0], jnp.zeros((8, n), f32)], axis=0)
        o_ref[0] = _tn(cond, dm)

    return pl.pallas_call(
        body, name=name, grid=(L,),
        in_specs=[pl.BlockSpec((NDEV, D), lambda l: (0, 0)), pl.BlockSpec((1, NDEV, n), lambda l: (l, 0, 0))],
        out_specs=pl.BlockSpec((1, D, n), lambda l: (l, 0, 0)),
        out_shape=SDS((L, D, n), f32),
        compiler_params=_cp(("arbitrary",)),
    )(c_all, dmod)


def reduce_adam(parts, w, m, v, tr, name):
    n, R, C = parts.shape
    c1 = 1.0 / (1.0 - ADAM_B1 ** ADAM_STEP)
    c2 = 1.0 / (1.0 - ADAM_B2 ** ADAM_STEP)

    def body(p_ref, w_ref, m_ref, v_ref, g_ref, d_ref, nm_ref, nv_ref):
        g = p_ref[0].astype(f32)
        for s in range(1, n):
            g = g + p_ref[s].astype(f32)
        nm = ADAM_B1 * m_ref[...] + (1.0 - ADAM_B1) * g
        nv = ADAM_B2 * v_ref[...] + (1.0 - ADAM_B2) * (g * g)
        g_ref[...] = g
        nm_ref[...] = nm
        nv_ref[...] = nv
        d_ref[...] = -ADAM_LR * ((nm * c1) / (jnp.sqrt(nv * c2) + ADAM_EPS) + ADAM_WD * w_ref[...])

    blk = pl.BlockSpec((tr, C), lambda i: (i, 0))
    return pl.pallas_call(
        body, name=name, grid=(R // tr,),
        in_specs=[pl.BlockSpec((n, tr, C), lambda i: (0, i, 0)), blk, blk, blk],
        out_specs=[blk] * 4, out_shape=[SDS((R, C), f32)] * 4,
        compiler_params=_cp(("arbitrary",)),
    )(parts, w, m, v)


def all_gather(xs, name):
    n = len(xs)

    def body(*refs):
        x_refs, out_refs = refs[:n], refs[n:2 * n]
        send_sems, recv_sems, local_sems = refs[2 * n:]
        x_, y_, c_ = _my_pos()
        me, sibling = (x_, y_, c_), (x_, y_, 1 - c_)
        chips = [(1 - x_, y_), (x_, 1 - y_), (1 - x_, 1 - y_)]

        def rows(a, px, py, pc):
            return out_refs[a].at[4 * px + 2 * py + pc]

        def copy(a, k, block, to, own=False):
            return pltpu.make_async_remote_copy(
                src_ref=x_refs[a] if own else rows(a, *block), dst_ref=rows(a, *block),
                send_sem=send_sems.at[k, a], recv_sem=recv_sems.at[k, a], device_id=to, device_id_type=pl.DeviceIdType.MESH)

        mine = [pltpu.make_async_copy(x_refs[a], rows(a, *me), local_sems.at[a]) for a in range(n)]
        for cp in mine:
            cp.start()
        first = []
        for a in range(n):
            first.append(copy(a, 0, me, sibling, own=True))
            first += [copy(a, 1 + j, me, (*chip, c_), own=True) for j, chip in enumerate(chips)]
        for cp in first:
            cp.start()
        passed = []
        for j, chip in enumerate(chips):
            for a in range(n):
                copy(a, 1 + j, (*chip, c_), me).wait_recv()
                cp = copy(a, 4 + j, (*chip, c_), sibling)
                cp.start()
                passed.append(cp)
        for a in range(n):
            copy(a, 0, sibling, me).wait_recv()
            for j, chip in enumerate(chips):
                copy(a, 4 + j, (*chip, 1 - c_), me).wait_recv()
        for cp in first + passed:
            cp.wait_send()
        for cp in mine:
            cp.wait()

    any_ = pl.BlockSpec(memory_space=pl.ANY)
    return pl.pallas_call(
        body, name=name, out_shape=[SDS((NDEV,) + x.shape, x.dtype) for x in xs],
        in_specs=[any_] * n, out_specs=[any_] * n,
        scratch_shapes=[pltpu.SemaphoreType.DMA((7, n)), pltpu.SemaphoreType.DMA((7, n)), pltpu.SemaphoreType.DMA((n,))],
    )(*xs)


GDN_COLS = ((0, GDN_CONV + GDN_V_W, 0), (GDN_CONV + GDN_V_W, GDN_CONV + GDN_V_W + 16, GDN_CONV + GDN_V_W),
            (GDN_CONV + GDN_V_W + 16, GDN_IN, GDN_CONV + GDN_V_W + 128))
FOX_COLS = ((0, FOX_IN, 0),)


def _col_pieces(d, per, cols):
    lo, hi = per * d, per * (d + 1)
    out = []
    for a, b, dst in cols:
        s, e = max(lo, a), min(hi, b)
        if s < e:
            out.append((s - lo, e - s, dst + s - a))
    return out


def cols_from_blocks(g, cols, n_out, name):
    _, L, R, C = g.shape
    tr = min(256, R)

    def body(g_ref, o_ref):
        o_ref[...] = jnp.zeros_like(o_ref)
        for d in range(NDEV):
            for off, ln, dst in _col_pieces(d, C, cols):
                o_ref[0, :, dst:dst + ln] = g_ref[d, 0, :, off:off + ln]

    return pl.pallas_call(
        body, name=name, grid=(L, R // tr),
        in_specs=[pl.BlockSpec((NDEV, 1, tr, C), lambda l, i: (0, l, i, 0))],
        out_specs=pl.BlockSpec((1, tr, n_out), lambda l, i: (l, i, 0)),
        out_shape=SDS((L, R, n_out), g.dtype),
        compiler_params=_cp(("arbitrary", "arbitrary")),
    )(g)


def blocks_from_cols(dw, C, cols, name):
    R, n_in = dw.shape
    tr = min(256, R)

    def body(x_ref, o_ref):
        for d in range(NDEV):
            for off, ln, src in _col_pieces(d, C, cols):
                o_ref[d, :, off:off + ln] = x_ref[:, src:src + ln].astype(bf16)

    return pl.pallas_call(
        body, name=name, grid=(R // tr,),
        in_specs=[pl.BlockSpec((tr, n_in), lambda i: (i, 0))],
        out_specs=pl.BlockSpec((NDEV, tr, C), lambda i: (0, i, 0)),
        out_shape=SDS((NDEV, R, C), bf16),
        compiler_params=_cp(("arbitrary",)),
    )(dw)


BIG = ("a_w_in", "a_conv_w", "a_w_out", "b_w_in", "b_w_out")
SMALL = ("norm_w", "ada_b", "a_A_log", "a_dt_bias", "a_norm_w", "b_f_bias", "b_qn_w", "b_kn_w", "final_norm_w")


def _pack_small(arrs):
    rows = []
    for a in arrs:
        fl = a.reshape(-1)
        pad = (-fl.shape[0]) % 128
        if pad:
            fl = jnp.concatenate([fl, jnp.zeros((pad,), fl.dtype)])
        rows.append(fl)
    flat = jnp.concatenate(rows)
    pad = (-flat.shape[0]) % (8 * 128)
    if pad:
        flat = jnp.concatenate([flat, jnp.zeros((pad,), flat.dtype)])
    return flat.reshape(-1, 128)


def _unpack(packed, shapes, align):
    flat = packed.reshape(-1)
    out, off = [], 0
    for shp in shapes:
        n = 1
        for d in shp:
            n *= d
        out.append(flat[off:off + n].reshape(shp))
        off += n + ((-n) % align)
    return out


def _full_from_gathered(g, shard_shape, axis):
    g = jnp.moveaxis(g, 0, axis)
    shp = list(shard_shape)
    shp[axis] *= NDEV
    return g.reshape(shp)


def _pad_lanes(v, n=128):
    v = v.reshape(1, -1)
    return jnp.concatenate([v, jnp.zeros((1, n - v.shape[1]), v.dtype)], axis=1)


def _carried(fn, *args, side=None, **grads):
    if callable(side):
        side = side(**grads)
    res = fn(*args, side)
    return res if side is not None else (res, None)


def gdn_layer_fwd(x, mod, nw, weights, tag, sides):
    W_in, conv_w, A_log, dt_bias, a_nw, W_out = weights
    shift, scale, gate = mod
    got = {}
    (proj, h), got["inproj"] = _carried(inproj_fwd, x, nw, scale, shift, W_in, GDN_TN, f"{tag}_inproj", side=sides.get("inproj"))
    qkvc = gdn_prep_fwd(proj, conv_w, f"{tag}_prep")
    gc, beta = gdn_gates_fwd(proj, A_log, dt_bias, f"{tag}_gates")
    (u, w, attn, Ts), got["intra"] = _carried(gdn_intra_fwd, qkvc, gc, beta, f"{tag}_intra", side=sides.get("intra"))
    (o, states), got["scan"] = _carried(gdn_scan_fwd, qkvc, gc, u, w, attn, f"{tag}_scan", side=sides.get("scan"))
    x_new, y, og = gdn_out_fwd(o, proj, a_nw, W_out, x, gate, f"{tag}_out")
    return x_new, (x, proj, h, qkvc, gc, beta, o, states, Ts, y, og, u, w, attn), got


def gdn_layer_bwd(dxn, saved, mod, nw, weights, tag, sides):
    W_in, conv_w, A_log, dt_bias, a_nw, W_out = weights
    shift, scale, gate = mod
    x, proj, h, qkvc, gc, beta, o, states, Ts, y, og, u, w, attn = saved
    got = {}
    dy, dgate, do, dproj, da_nw = gdn_out_bwd(dxn, y, gate, o, proj, a_nw, W_out, f"{tag}_out_bwd")
    dW_out, = matmul_tn(og, dy, 512, f"{tag}_dwout")
    (dq_s, dk_s, dgc_s, du, dw, dattn), got["sbwd"] = _carried(
        gdn_scan_bwd, qkvc, gc, u, w, attn, states, do, f"{tag}_scan_bwd", side=sides.get("sbwd"), dW_out=dW_out)
    (dqkvc, dgc, dbeta), got["intrab"] = _carried(
        gdn_intra_bwd, qkvc, gc, beta, Ts, du, dw, dattn, dq_s, dk_s, dgc_s, f"{tag}_intra_bwd", side=sides.get("intrab"), dW_out=dW_out)
    dproj, dA_log, ddt = gdn_gates_bwd(proj, A_log, dt_bias, dgc, dbeta, dproj, f"{tag}_gates_bwd")
    dproj, dconv_w = gdn_prep_bwd(proj, conv_w, dqkvc, dproj, f"{tag}_prep_bwd")
    (dW_in,), got["dwin"] = _carried(matmul_tn, h, dproj, GDN_TN, f"{tag}_dwin", side=sides.get("dwin"), dW_out=dW_out)
    (dx, dnw, dscale, dshift), got["ibwd"] = _carried(
        inproj_bwd_x, x, nw, scale, shift, W_in, dproj, dxn, GDN_TN, f"{tag}_inproj_bwd", side=sides.get("ibwd"),
        dW_out=dW_out, dW_in=dW_in, dconv_w=dconv_w)
    grads = dict(norm_w=dnw, W_in=dW_in, conv_w=dconv_w, A_log=dA_log[:, :16], dt_bias=ddt[:, :16], a_nw=da_nw, W_out=dW_out,
                 dmod=jnp.concatenate([dshift, dscale, dgate], axis=1))
    return dx, grads, got


def fox_layer_fwd(x, mod, nw, weights, tag, sides):
    W_in, f_bias, qn_w, kn_w, W_out = weights
    shift, scale, gate = mod
    got = {}
    (proj, h), got["inproj"] = _carried(inproj_fwd, x, nw, scale, shift, W_in, FOX_TN, f"{tag}_inproj", side=sides.get("inproj"))
    Q, K, V = fox_prep_fwd(proj, f_bias, qn_w, kn_w, f"{tag}_prep")
    (O,), got["attn"] = _carried(fox_attn_fwd, Q, K, V, f"{tag}_attn", side=sides.get("attn"))
    x_new, y, og = fox_out_fwd(O, proj, W_out, x, gate, f"{tag}_out")
    return x_new, (x, proj, h, Q, K, V, O, y, og), got


def fox_layer_bwd(dxn, saved, mod, nw, weights, tag, sides):
    W_in, f_bias, qn_w, kn_w, W_out = weights
    shift, scale, gate = mod
    x, proj, h, Q, K, V, O, y, og = saved
    got = {}
    dy, dgate, dO, dz = fox_out_bwd(dxn, y, gate, O, proj, W_out, f"{tag}_out_bwd")
    dW_out, = matmul_tn(og, dy, 512, f"{tag}_dwout")
    (dQ, dK, dV), got["abwd"] = _carried(fox_attn_bwd, Q, K, V, dO, O, f"{tag}_attn_bwd", side=sides.get("abwd"))
    dproj, dfb, dqw, dkw = fox_prep_bwd(proj, f_bias, qn_w, kn_w, dQ, dK, dV, dz, f"{tag}_prep_bwd")
    (dW_in,), got["dwin"] = _carried(matmul_tn, h, dproj, FOX_TN, f"{tag}_dwin", side=sides.get("dwin"))
    dx, dnw, dscale, dshift = inproj_bwd_x(x, nw, scale, shift, W_in, dproj, dxn, FOX_TN, f"{tag}_inproj_bwd")
    grads = dict(norm_w=dnw, W_in=dW_in, f_bias=dfb[:, :16], qn_w=dqw, kn_w=dkw, W_out=dW_out,
                 dmod=jnp.concatenate([dshift, dscale, dgate], axis=1))
    return dx, grads, got


class LocalPlan:
    def __init__(self, full):
        self.full = full

    def layer_weights(self, i):
        j, f = i // 2, self.full
        return (f["a_w_in"][j], f["a_w_out"][j], f["a_conv_w"][j]) if i % 2 == 0 else (f["b_w_in"][j], f["b_w_out"][j])

    def fwd_sides(self, i):
        return {}

    def fwd_got(self, i, got):
        pass

    def bwd_sides(self, i):
        return {}

    def bwd_got(self, i, grads, got):
        pass


def device_step(x, mod_all, norm_w, small, final_norm_w, target, plan):
    D = x.shape[1]
    mods = [(mod_all[i:i + 1, 0:D], mod_all[i:i + 1, D:2 * D], mod_all[i:i + 1, 2 * D:3 * D]) for i in range(4)]

    def weights(i):
        j = i // 2
        if i % 2 == 0:
            W_in, W_out, conv_w = plan.layer_weights(i)
            return (W_in, conv_w, _pad_lanes(small["a_A_log"][j]), _pad_lanes(small["a_dt_bias"][j]), small["a_norm_w"][j:j + 1], W_out)
        W_in, W_out = plan.layer_weights(i)
        return (W_in, _pad_lanes(small["b_f_bias"][j]), small["b_qn_w"][j:j + 1], small["b_kn_w"][j:j + 1], W_out)

    saved, wts = [], []
    for i in range(4):
        wts.append(weights(i))
        fwd = gdn_layer_fwd if i % 2 == 0 else fox_layer_fwd
        x, sv, got = fwd(x, mods[i], norm_w[i:i + 1], wts[i], f"L{i}", plan.fwd_sides(i))
        plan.fwd_got(i, got)
        saved.append(sv)
    loss, dx, dfw = final_loss(x, final_norm_w.reshape(1, D), target, "final_loss")
    lg = [None] * 4
    for i in reversed(range(4)):
        bwd = gdn_layer_bwd if i % 2 == 0 else fox_layer_bwd
        dx, lg[i], got = bwd(dx, saved[i], mods[i], norm_w[i:i + 1], wts[i], f"L{i}", plan.bwd_sides(i))
        plan.bwd_got(i, lg[i], got)
    g = dict(
        norm_w=jnp.concatenate([lg[i]["norm_w"] for i in range(4)], axis=0),
        dmod=jnp.concatenate([lg[i]["dmod"] for i in range(4)], axis=0),
        a_w_in=[lg[i]["W_in"] for i in (0, 2)],
        a_conv_w=jnp.stack([lg[i]["conv_w"] for i in (0, 2)]),
        a_A_log=jnp.concatenate([lg[i]["A_log"] for i in (0, 2)], axis=0),
        a_dt_bias=jnp.concatenate([lg[i]["dt_bias"] for i in (0, 2)], axis=0),
        a_norm_w=jnp.concatenate([lg[i]["a_nw"] for i in (0, 2)], axis=0),
        a_w_out=[lg[i]["W_out"] for i in (0, 2)],
        b_w_in=[lg[i]["W_in"] for i in (1, 3)],
        b_f_bias=jnp.concatenate([lg[i]["f_bias"] for i in (1, 3)], axis=0),
        b_qn_w=jnp.concatenate([lg[i]["qn_w"] for i in (1, 3)], axis=0),
        b_kn_w=jnp.concatenate([lg[i]["kn_w"] for i in (1, 3)], axis=0),
        b_w_out=[lg[i]["W_out"] for i in (1, 3)],
        final_norm_w=dfw.reshape(-1),
    )
    return loss[0, 0], dx, g


class MeshPlan:
    def __init__(self, shards, w0, conv_full):
        self.shards = shards
        self.w = {0: w0}
        self.conv = conv_full
        self.recv = {}
        self.pending = {}
        self.names = {}

    def layer_weights(self, i):
        return self.w[i]

    def _gather_side(self, layer):
        names = ("a_w_in", "a_w_out") if layer % 2 == 0 else ("b_w_in", "b_w_out")
        out = []
        for n in names:
            sh = self.shards[n][layer // 2]
            out.append(sh.reshape(-1, sh.shape[-1]))
        return ("gather", out)

    def fwd_sides(self, i):
        if i == 0:
            kind, (b_in, b_out) = self._gather_side(1)
            kind, (a_in, a_out) = self._gather_side(2)
            return {"inproj": (kind, [b_out, a_out]), "scan": (kind, [b_in]), "intra": (kind, [a_in])}
        if i == 1:
            return {"attn": self._gather_side(3)}
        return {}

    def fwd_got(self, i, got):
        if got.get("inproj") is not None:
            self._outs = dict(zip(("scan", "intra"), got["inproj"]))
        for key, layer in (("scan", 1), ("intra", 2), ("attn", 3)):
            if got.get(key) is None:
                continue
            g_in, g_out = got[key] if key == "attn" else (got[key][0], self._outs[key])
            D = g_out.shape[-1]
            j = layer // 2
            if layer % 2 == 1:
                W_in = cols_from_blocks(g_in[:, None], FOX_COLS, FOX_IN_PAD, f"b_w_in_cols{j}")[0]
                self.w[layer] = (W_in, g_out.reshape(-1, D))
            else:
                W_in = cols_from_blocks(g_in[:, None], GDN_COLS, GDN_IN_PAD, f"a_w_in_cols{j}")[0]
                self.w[layer] = (W_in, g_out.reshape(-1, D), self.conv[j])

    @staticmethod
    def _out_blocks(dW_out):
        return dW_out.astype(bf16).reshape(NDEV, -1, dW_out.shape[-1])

    def _in_blocks(self, name, j, dW_in):
        cols = GDN_COLS if name == "a_w_in" else FOX_COLS
        return blocks_from_cols(dW_in, self.shards[name].shape[-1], cols, f"{name}_blocks{j}")

    def bwd_sides(self, i):
        self.names = {}
        sides = {}
        for (layer, key) in [k for k in self.pending if k[0] == i]:
            self.names[key], arrs = self.pending.pop((layer, key))
            sides[key] = ("scatter", arrs)
        if i == 0:
            def sbwd(dW_out):
                self.names["sbwd"] = [("a_w_out", 0)]
                return ("scatter", [self._out_blocks(dW_out)])

            def ibwd(dW_out, dW_in, dconv_w):
                conv = jnp.stack([dconv_w, self._dconv1])
                n = conv.shape[-1] // NDEV
                self.names["ibwd"] = [("a_w_in", 0), ("a_conv_w", None)]
                return ("scatter", [self._in_blocks("a_w_in", 0, dW_in),
                                    jnp.moveaxis(conv.reshape(2, 4, NDEV, n), 2, 0).reshape(NDEV, 8, n)])

            sides["sbwd"], sides["ibwd"] = sbwd, ibwd
        return sides

    def bwd_got(self, i, grads, got):
        for key, arrs in got.items():
            if arrs is not None:
                self.recv.update(zip(self.names[key], arrs))
        j = i // 2
        if i % 2 == 1:
            self.pending[(i - 1, "sbwd" if i == 3 else "intrab")] = (
                [("b_w_in", j), ("b_w_out", j)], [self._in_blocks("b_w_in", j, grads["W_in"]), self._out_blocks(grads["W_out"])])
        elif i == 2:
            self.pending[(1, "abwd")] = (
                [("a_w_in", 1), ("a_w_out", 1)], [self._in_blocks("a_w_in", 1, grads["W_in"]), self._out_blocks(grads["W_out"])])
            self._dconv1 = grads["conv_w"]


def kernel(x, c, norm_w, ada_w, ada_b, a_w_in, a_conv_w, a_A_log, a_dt_bias, a_norm_w, a_w_out, b_w_in, b_f_bias, b_qn_w, b_kn_w, b_w_out, final_norm_w, loss_target, m_norm_w, m_ada_w, m_ada_b, m_a_w_in, m_a_conv_w, m_a_A_log, m_a_dt_bias, m_a_norm_w, m_a_w_out, m_b_w_in, m_b_f_bias, m_b_qn_w, m_b_kn_w, m_b_w_out, m_final_norm_w, v_norm_w, v_ada_w, v_ada_b, v_a_w_in, v_a_conv_w, v_a_A_log, v_a_dt_bias, v_a_norm_w, v_a_w_out, v_b_w_in, v_b_f_bias, v_b_qn_w, v_b_kn_w, v_b_w_out, v_final_norm_w):
    W = dict(norm_w=norm_w, ada_w=ada_w, ada_b=ada_b, a_w_in=a_w_in, a_conv_w=a_conv_w, a_A_log=a_A_log, a_dt_bias=a_dt_bias,
             a_norm_w=a_norm_w, a_w_out=a_w_out, b_w_in=b_w_in, b_f_bias=b_f_bias, b_qn_w=b_qn_w, b_kn_w=b_kn_w, b_w_out=b_w_out,
             final_norm_w=final_norm_w)
    M = dict(norm_w=m_norm_w, ada_w=m_ada_w, ada_b=m_ada_b, a_w_in=m_a_w_in, a_conv_w=m_a_conv_w, a_A_log=m_a_A_log,
             a_dt_bias=m_a_dt_bias, a_norm_w=m_a_norm_w, a_w_out=m_a_w_out, b_w_in=m_b_w_in, b_f_bias=m_b_f_bias, b_qn_w=m_b_qn_w,
             b_kn_w=m_b_kn_w, b_w_out=m_b_w_out, final_norm_w=m_final_norm_w)
    V = dict(norm_w=v_norm_w, ada_w=v_ada_w, ada_b=v_ada_b, a_w_in=v_a_w_in, a_conv_w=v_a_conv_w, a_A_log=v_a_A_log,
             a_dt_bias=v_a_dt_bias, a_norm_w=v_a_norm_w, a_w_out=v_a_w_out, b_w_in=v_b_w_in, b_f_bias=v_b_f_bias, b_qn_w=v_b_qn_w,
             b_kn_w=v_b_kn_w, b_w_out=v_b_w_out, final_norm_w=v_final_norm_w)
    S, D = x.shape[1], x.shape[2]
    me = 4 * lax.axis_index("x") + 2 * lax.axis_index("y") + lax.axis_index("c")
    small_shapes = [W[n].shape for n in SMALL]

    shards = {n: W[n].astype(bf16) for n in ("a_w_in", "a_w_out", "b_w_in", "b_w_out")}
    gath = all_gather([shards["a_w_in"][0], shards["a_w_out"][0], a_conv_w.reshape(8, -1), c.reshape(8, D // 8)], "gather_w0")
    conv_full = _full_from_gathered(gath[2].reshape((NDEV,) + a_conv_w.shape), a_conv_w.shape, 2)
    w0 = (cols_from_blocks(gath[0][:, None], GDN_COLS, GDN_IN_PAD, "a_w_in_cols0")[0], gath[1].reshape(-1, D), conv_full[0])
    plan = MeshPlan(shards, w0, conv_full)
    c_all = gath[3].reshape(NDEV, D)

    mod_part = ada_fwd(c_all, ada_w, "ada_fwd")
    n_ada = ada_w.shape[2]
    mod_g = all_gather([mod_part.reshape(4 * NDEV, n_ada)], "gather_mod")[0].reshape(NDEV, 4, NDEV, n_ada)
    mod_mine = lax.dynamic_index_in_dim(mod_g, me, axis=2, keepdims=False)
    mod_all = jnp.moveaxis(mod_mine, 0, 1).reshape(4, NDEV * n_ada) + ada_b

    loss, dx, g = device_step(x[0], mod_all, norm_w, W, final_norm_w, loss_target[0], plan)
    loss = lax.psum(loss, MESH_AXES)

    g_small = dict(g, ada_b=g["dmod"])
    sp = _pack_small([g_small[n] for n in SMALL])
    sp_all = all_gather([sp], "gather_small")[0]
    sw, sm, sv = (_pack_small([T[n] for n in SMALL]) for T in (W, M, V))
    sg, sd, snm, snv = (_unpack(t, small_shapes, 128) for t in reduce_adam(sp_all, sw, sm, sv, sp.shape[0], "adam_small"))

    off_b = 0
    for n, shp in zip(SMALL, small_shapes):
        if n == "ada_b":
            break
        cnt = 1
        for d in shp:
            cnt *= d
        off_b += cnt + ((-cnt) % 128)
    dmod_all = sp_all.reshape(NDEV, -1)[:, off_b:off_b + 4 * 3 * D].reshape(NDEV, 4, 3 * D)
    dmod_cols = lax.dynamic_slice_in_dim(dmod_all, me * n_ada, n_ada, axis=2)
    g_ada = ada_grad(c_all, jnp.moveaxis(dmod_cols, 0, 1), "ada_grad")
    r_ada = reduce_adam(g_ada.reshape(1, 4 * D, n_ada), *(T["ada_w"].reshape(4 * D, n_ada) for T in (W, M, V)), 512, "adam_ada")
    ag, ad, anm, anv = (t.reshape(ada_w.shape) for t in r_ada)

    big = {}
    for n in BIG:
        C = W[n].shape[-1]
        parts = plan.recv[(n, None)] if n == "a_conv_w" else jnp.stack([plan.recv[(n, 0)], plan.recv[(n, 1)]], axis=1).reshape(NDEV, -1, C)
        res = reduce_adam(parts, *(T[n].reshape(parts.shape[1:]) for T in (W, M, V)), min(256, parts.shape[1]), f"adam_{n}")
        big[n] = [t.reshape(W[n].shape) for t in res]

    outs = {}
    for idx, (k, sm_l, ada_t) in enumerate((("grad", sg, ag), ("delta", sd, ad), ("new_m", snm, anm), ("new_v", snv, anv))):
        d = dict(zip(SMALL, sm_l))
        d.update({n: big[n][idx] for n in BIG})
        d["ada_w"] = ada_t
        outs[k] = d
    order = ("norm_w", "ada_w", "ada_b", "a_w_in", "a_conv_w", "a_A_log", "a_dt_bias", "a_norm_w", "a_w_out", "b_w_in", "b_f_bias",
             "b_qn_w", "b_kn_w", "b_w_out", "final_norm_w")
    return (loss, dx[None], *[outs["grad"][n] for n in order], *[outs["delta"][n] for n in order],
            *[outs["new_m"][n] for n in order], *[outs["new_v"][n] for n in order])
```

```python
import functools

import jax
import jax.numpy as jnp
from jax import lax
from jax.experimental import pallas as pl
from jax.experimental.pallas import tpu as pltpu

f32 = jnp.float32
bf16 = jnp.bfloat16
SDS = jax.ShapeDtypeStruct

EPS = 1e-6
CHUNK = 64
HD = 128
GDN_QK_HEADS = 8
GDN_V_HEADS = 16
GDN_QK_W = GDN_QK_HEADS * HD
GDN_V_W = GDN_V_HEADS * HD
GDN_CONV = 2 * GDN_QK_W + GDN_V_W
GDN_IN = GDN_CONV + GDN_V_W + 2 * GDN_V_HEADS
GDN_IN_PAD = GDN_CONV + GDN_V_W + 256
GDN_TN = 1280
FOX_H = 16
FOX_D = 64
FOX_W = FOX_H * FOX_D
FOX_IN = 4 * FOX_W + FOX_H
FOX_IN_PAD = 4 * FOX_W + 128
FOX_TN = 1408
FOX_PW = FOX_H * 128
NDEV = 8
MESH_AXES = ("x", "y", "c")
NEG = -1e30

ADAM_LR = 0.001
ADAM_B1 = 0.9
ADAM_B2 = 0.999
ADAM_EPS = 1e-08
ADAM_WD = 0.01
ADAM_STEP = 10

VMEM_LIMIT = 56 * 1024 * 1024


def _cp(sem=None):
    return pltpu.CompilerParams(dimension_semantics=sem, vmem_limit_bytes=VMEM_LIMIT)


def _bdot(a, b, dims):
    return lax.dot_general(a.astype(bf16), b.astype(bf16), (dims, ((), ())), preferred_element_type=f32)


def _nn(a, b):
    return _bdot(a, b, ((1,), (0,)))


def _nt(a, b):
    return _bdot(a, b, ((1,), (1,)))


def _tn(a, b):
    return _bdot(a, b, ((0,), (0,)))


def _hdot(a, b, dims=((1,), (0,))):
    return lax.dot_general(a, b, (dims, ((), ())), precision=lax.Precision.HIGHEST, preferred_element_type=f32)


def _split2(a):
    hi = a.astype(bf16)
    return hi, (a - hi.astype(f32)).astype(bf16)


def _dot3(a, b):
    (ah, al), (bh, bl) = a, b
    n = ah.shape[0]
    both = jnp.dot(jnp.concatenate([ah, al], axis=0), bh, preferred_element_type=f32)
    return both[:n] + both[n:] + jnp.dot(ah, bl, preferred_element_type=f32)


@jax.custom_vjp
def _mm(a, b):
    return _nn(a, b)


_mm.defvjp(lambda a, b: (_nn(a, b), (a, b)), lambda r, g: (_nt(g, r[1]), _tn(r[0], g)))


@jax.custom_vjp
def _mm_nt(a, b):
    return _nt(a, b)


_mm_nt.defvjp(lambda a, b: (_nt(a, b), (a, b)), lambda r, g: (_nn(g, r[1]), _tn(g, r[0])))


@jax.custom_vjp
def _mm_tn(a, b):
    return _tn(a, b)


_mm_tn.defvjp(lambda a, b: (_tn(a, b), (a, b)), lambda r, g: (_nt(r[1], g), _nn(r[0], g)))


def _silu(x):
    return x * jax.nn.sigmoid(x)


def _rms_mod(x, nw, scale, shift):
    r = lax.rsqrt(jnp.mean(x * x, axis=-1, keepdims=True) + EPS)
    return (x * r * nw) * (1.0 + scale) + shift


def _rows(S, want):
    return min(want, S)


def _my_pos():
    return lax.axis_index("x"), lax.axis_index("y"), lax.axis_index("c")


def _exchange_copies(kind, x_refs, out_refs, send_sems, recv_sems, local_sems):
    x_, y_, c_ = _my_pos()
    me = 4 * x_ + 2 * y_ + c_
    own = kind == "gather"
    cps = [pltpu.make_async_copy(x_refs[a] if own else x_refs[a].at[me], out_refs[a].at[me], local_sems.at[a])
           for a in range(len(x_refs))]
    for rel in range(1, NDEV):
        px = (x_ + ((rel >> 2) & 1)) % 2
        py = (y_ + ((rel >> 1) & 1)) % 2
        pc = (c_ + (rel & 1)) % 2
        for a in range(len(x_refs)):
            cps.append(pltpu.make_async_remote_copy(
                src_ref=x_refs[a] if own else x_refs[a].at[4 * px + 2 * py + pc], dst_ref=out_refs[a].at[me],
                send_sem=send_sems.at[rel - 1, a], recv_sem=recv_sems.at[rel - 1, a],
                device_id=(px, py, pc), device_id_type=pl.DeviceIdType.MESH))
    return cps


def _exchange_scratch(n):
    return [pltpu.SemaphoreType.DMA((NDEV - 1, n)), pltpu.SemaphoreType.DMA((NDEV - 1, n)), pltpu.SemaphoreType.DMA((n,))]


def _call(body, *, name, grid, in_specs, out_specs, out_shape, args, scratch=(), side=None):
    params = _cp(("arbitrary",) * len(grid))
    if side is None:
        return pl.pallas_call(body, name=name, grid=grid, in_specs=in_specs, out_specs=out_specs, out_shape=out_shape,
                              scratch_shapes=list(scratch), compiler_params=params)(*args)
    kind, xs = side
    n_in, n_out, n_scr, ns = len(in_specs), len(out_shape), len(scratch), len(xs)
    steps = 1
    for g in grid:
        steps *= g

    def wrapped(*refs):
        o0 = n_in + ns
        s0 = o0 + n_out + ns
        step = pl.program_id(0)
        for d in range(1, len(grid)):
            step = step * grid[d] + pl.program_id(d)

        def copies():
            return _exchange_copies(kind, refs[n_in:o0], refs[o0 + n_out:s0], *refs[s0 + n_scr:])

        @pl.when(step == 0)
        def _():
            for cp in copies():
                cp.start()

        body(*refs[:n_in], *refs[o0:o0 + n_out], *refs[s0:s0 + n_scr])

        @pl.when(step == steps - 1)
        def _():
            for cp in copies():
                cp.wait()

    any_ = pl.BlockSpec(memory_space=pl.ANY)
    side_shapes = [SDS((NDEV,) + x.shape if kind == "gather" else x.shape, x.dtype) for x in xs]
    outs = pl.pallas_call(wrapped, name=name, grid=grid, in_specs=list(in_specs) + [any_] * ns,
                          out_specs=list(out_specs) + [any_] * ns, out_shape=list(out_shape) + side_shapes,
                          scratch_shapes=list(scratch) + _exchange_scratch(ns), compiler_params=params)(*args, *xs)
    return outs[:n_out], outs[n_out:]


def inproj_fwd(x, nw, scale, shift, W, tn, name, side=None):
    S, D = x.shape
    N = W.shape[1]
    tm = _rows(S, 1024)

    def body(x_ref, nw_ref, sc_ref, sh_ref, w_ref, proj_ref, h_ref):
        @pl.when(pl.program_id(1) == 0)
        def _():
            h_ref[...] = _rms_mod(x_ref[...], nw_ref[...], sc_ref[...], sh_ref[...]).astype(bf16)

        proj_ref[...] = jnp.dot(h_ref[...], w_ref[...], preferred_element_type=f32)

    vec = pl.BlockSpec((1, D), lambda i, j: (0, 0))
    return _call(
        body, name=name, grid=(S // tm, N // tn),
        in_specs=[pl.BlockSpec((tm, D), lambda i, j: (i, 0)), vec, vec, vec, pl.BlockSpec((D, tn), lambda i, j: (0, j))],
        out_specs=[pl.BlockSpec((tm, tn), lambda i, j: (i, j)), pl.BlockSpec((tm, D), lambda i, j: (i, 0))],
        out_shape=[SDS((S, N), f32), SDS((S, D), bf16)], args=(x, nw, scale, shift, W), side=side)


def inproj_bwd_x(x, nw, scale, shift, W, dproj, dx_res, tn, name, side=None):
    S, D = x.shape
    N = W.shape[1]
    tm = _rows(S, 1024)
    nj = N // tn

    def body(x_ref, nw_ref, sc_ref, sh_ref, w_ref, dp_ref, dxr_ref, dx_ref, dnw_ref, dsc_ref, dsh_ref, acc):
        i, j = pl.program_id(0), pl.program_id(1)

        @pl.when(j == 0)
        def _():
            acc[...] = jnp.zeros_like(acc)

        @pl.when((i == 0) & (j == 0))
        def _():
            dnw_ref[...] = jnp.zeros_like(dnw_ref)
            dsc_ref[...] = jnp.zeros_like(dsc_ref)
            dsh_ref[...] = jnp.zeros_like(dsh_ref)

        acc[...] += _nt(dp_ref[...], w_ref[...])

        @pl.when(j == nj - 1)
        def _():
            _, vjp = jax.vjp(_rms_mod, x_ref[...], nw_ref[...], sc_ref[...], sh_ref[...])
            dx, dnw, dsc, dsh = vjp(acc[...])
            dx_ref[...] = dxr_ref[...] + dx
            dnw_ref[...] += dnw
            dsc_ref[...] += dsc
            dsh_ref[...] += dsh

    vec = pl.BlockSpec((1, D), lambda i, j: (0, 0))
    row = pl.BlockSpec((tm, D), lambda i, j: (i, 0))
    return _call(
        body, name=name, grid=(S // tm, nj),
        in_specs=[row, vec, vec, vec, pl.BlockSpec((D, tn), lambda i, j: (0, j)), pl.BlockSpec((tm, tn), lambda i, j: (i, j)), row],
        out_specs=[row, vec, vec, vec],
        out_shape=[SDS((S, D), f32), SDS((1, D), f32), SDS((1, D), f32), SDS((1, D), f32)],
        scratch=[pltpu.VMEM((tm, D), f32)], args=(x, nw, scale, shift, W, dproj, dx_res), side=side)


def matmul_tn(a, b, tn, name, side=None):
    S, K = a.shape
    N = b.shape[1]
    tm = _rows(S, 1024)
    ni = S // tm

    def body(a_ref, b_ref, o_ref):
        @pl.when(pl.program_id(1) == 0)
        def _():
            o_ref[...] = jnp.zeros_like(o_ref)

        o_ref[...] += _tn(a_ref[...], b_ref[...])

    return _call(
        body, name=name, grid=(N // tn, ni),
        in_specs=[pl.BlockSpec((tm, K), lambda j, i: (i, 0)), pl.BlockSpec((tm, tn), lambda j, i: (i, j))],
        out_specs=[pl.BlockSpec((K, tn), lambda j, i: (0, j))],
        out_shape=[SDS((K, N), f32)], args=(a, b), side=side)


def _conv_taps(xs, w, n_out):
    taps = []
    for j in range(4):
        s = 3 - j
        sh = xs if s == 0 else pltpu.roll(xs, s, axis=0)
        taps.append(sh[8:8 + n_out])
    conv = taps[0] * w[0] + taps[1] * w[1] + taps[2] * w[2] + taps[3] * w[3]
    return taps, conv


def _act_norm(conv, mul):
    s = _silu(conv)
    return s * (mul * lax.rsqrt(jnp.sum(s * s, axis=-1, keepdims=True) + EPS))


def gdn_prep_fwd(proj, conv_w, name):
    S = proj.shape[0]
    R = _rows(S, 512)

    def body(x_ref, w_ref, o_ref):
        j = pl.program_id(0)
        w = [w_ref[t:t + 1, :] for t in range(4)]

        def sweep(act):
            def piece(r, c):
                t0 = pl.multiple_of(r * R, R)
                cur = x_ref[pl.ds(t0, R), :]
                prev = x_ref[pl.ds(pl.multiple_of(jnp.maximum(t0 - 8, 0), 8), 8), :]
                prev = jnp.where(r == 0, 0.0, prev)
                _, conv = _conv_taps(jnp.concatenate([prev, cur], axis=0), w, R)
                o_ref[pl.ds(t0, R), :] = act(conv)
                return c

            lax.fori_loop(0, S // R, piece, 0)

        @pl.when(j < 2 * GDN_QK_HEADS)
        def _():
            sweep(lambda c: _act_norm(c, jnp.where(j < GDN_QK_HEADS, HD ** -0.5, 1.0).astype(f32)))

        @pl.when(j >= 2 * GDN_QK_HEADS)
        def _():
            sweep(_silu)

    return pl.pallas_call(
        body, name=name, grid=(GDN_CONV // 128,),
        in_specs=[pl.BlockSpec((S, 128), lambda j: (0, j)), pl.BlockSpec((4, 128), lambda j: (0, j))],
        out_specs=pl.BlockSpec((S, 128), lambda j: (0, j)),
        out_shape=SDS((S, GDN_CONV), f32),
        compiler_params=_cp(("arbitrary",)),
    )(proj, conv_w)


def gdn_prep_bwd(proj, conv_w, dqkvc, dproj, name):
    S = proj.shape[0]
    R = _rows(S, 512)
    NP = S // R

    def body(x_ref, w_ref, dn_ref, _, dx_ref, dw_ref):
        jb = pl.program_id(0)
        w = [w_ref[j:j + 1, :] for j in range(4)]

        def piece(act, r, dw):
            t0 = pl.multiple_of(r * R, R)
            cur = x_ref[pl.ds(t0, R), :]
            prev = x_ref[pl.ds(pl.multiple_of(jnp.maximum(t0 - 8, 0), 8), 8), :]
            prev = jnp.where(r == 0, 0.0, prev)
            nxt0 = pl.multiple_of(jnp.minimum(t0 + R, S - 8), 8)
            nxt = x_ref[pl.ds(nxt0, 8), :]
            dn_cur = dn_ref[pl.ds(t0, R), :]
            dn_nxt = jnp.where(r == NP - 1, 0.0, dn_ref[pl.ds(nxt0, 8), :])
            xs = jnp.concatenate([prev, cur, nxt], axis=0)
            taps, conv = _conv_taps(xs, w, R + 8)
            dn = jnp.concatenate([dn_cur, dn_nxt], axis=0)
            _, vjp = jax.vjp(act, conv)
            dxc = vjp(dn)[0]
            n = R + 8
            dx = dxc[0:R] * w[3]
            for j in range(3):
                s = 3 - j
                dx = dx + pltpu.roll(dxc, n - s, axis=0)[0:R] * w[j]
            dx_ref[pl.ds(t0, R), :] = dx
            return tuple(dw[j] + jnp.sum(dxc[0:R] * taps[j][0:R], axis=0, keepdims=True) for j in range(4))

        def sweep(act):
            dw = lax.fori_loop(0, NP, functools.partial(piece, act), tuple(jnp.zeros((1, 128), f32) for _ in range(4)))
            for j in range(4):
                dw_ref[j:j + 1, :] = dw[j]

        @pl.when(jb < 2 * GDN_QK_HEADS)
        def _():
            sweep(lambda c: _act_norm(c, jnp.where(jb < GDN_QK_HEADS, HD ** -0.5, 1.0).astype(f32)))

        @pl.when(jb >= 2 * GDN_QK_HEADS)
        def _():
            sweep(_silu)

    col = pl.BlockSpec((S, 128), lambda j: (0, j))
    wsp = pl.BlockSpec((4, 128), lambda j: (0, j))
    return pl.pallas_call(
        body, name=name, grid=(GDN_CONV // 128,),
        in_specs=[col, wsp, col, pl.BlockSpec(memory_space=pl.ANY)], out_specs=[col, wsp],
        out_shape=[SDS(dproj.shape, f32), SDS((4, GDN_CONV), f32)],
        input_output_aliases={3: 0},
        compiler_params=_cp(("arbitrary",)),
    )(proj, conv_w, dqkvc, dproj)


def _chunk_tril(R):
    ii = lax.broadcasted_iota(jnp.int32, (R, R), 0)
    jj = lax.broadcasted_iota(jnp.int32, (R, R), 1)
    return ((ii // CHUNK == jj // CHUNK) & (ii >= jj)).astype(f32)


def _gdn_gates(b, a, A_log, dt_bias, tril):
    beta = jax.nn.sigmoid(b)
    g = -jnp.exp(A_log) * jax.nn.softplus(a + dt_bias)
    return _hdot(tril, g), beta


_GDN_B_BLK = (GDN_CONV + GDN_V_W) // 128
_GDN_A_BLK = _GDN_B_BLK + 1


def gdn_gates_fwd(proj, A_log, dt_bias, name):
    S = proj.shape[0]
    R = _rows(S, 512)

    def body(b_ref, a_ref, al_ref, dt_ref, gc_ref, be_ref):
        gc, be = _gdn_gates(b_ref[...], a_ref[...], al_ref[...], dt_ref[...], _chunk_tril(R))
        gc_ref[...] = gc
        be_ref[...] = be

    vec = pl.BlockSpec((1, 128), lambda i: (0, 0))
    blk = pl.BlockSpec((R, 128), lambda i: (i, 0))
    return pl.pallas_call(
        body, name=name, grid=(S // R,),
        in_specs=[pl.BlockSpec((R, 128), lambda i: (i, _GDN_B_BLK)), pl.BlockSpec((R, 128), lambda i: (i, _GDN_A_BLK)), vec, vec],
        out_specs=[blk, blk], out_shape=[SDS((S, 128), f32), SDS((S, 128), f32)],
        compiler_params=_cp(("arbitrary",)),
    )(proj, proj, A_log, dt_bias)


def gdn_gates_bwd(proj, A_log, dt_bias, dgc, dbeta, dproj, name):
    S = proj.shape[0]
    R = _rows(S, 512)

    def body(b_ref, a_ref, al_ref, dt_ref, dgc_ref, dbe_ref, _, dp_ref, dal_ref, ddt_ref):
        @pl.when(pl.program_id(0) == 0)
        def _():
            dal_ref[...] = jnp.zeros_like(dal_ref)
            ddt_ref[...] = jnp.zeros_like(ddt_ref)

        tril = _chunk_tril(R)
        _, vjp = jax.vjp(lambda b, a, al, dt: _gdn_gates(b, a, al, dt, tril), b_ref[...], a_ref[...], al_ref[...], dt_ref[...])
        db, da, dal, ddt = vjp((dgc_ref[...], dbe_ref[...]))
        dp_ref[:, 0:128] = db
        dp_ref[:, 128:256] = da
        dal_ref[...] += dal
        ddt_ref[...] += ddt

    vec = pl.BlockSpec((1, 128), lambda i: (0, 0))
    blk = pl.BlockSpec((R, 128), lambda i: (i, 0))
    return pl.pallas_call(
        body, name=name, grid=(S // R,),
        in_specs=[pl.BlockSpec((R, 128), lambda i: (i, _GDN_B_BLK)), pl.BlockSpec((R, 128), lambda i: (i, _GDN_A_BLK)), vec, vec, blk, blk,
                  pl.BlockSpec(memory_space=pl.ANY)],
        out_specs=[pl.BlockSpec((R, 256), lambda i: (i, _GDN_B_BLK // 2)), vec, vec],
        out_shape=[SDS(dproj.shape, f32), SDS((1, 128), f32), SDS((1, 128), f32)],
        input_output_aliases={6: 0},
        compiler_params=_cp(("arbitrary",)),
    )(proj, proj, A_log, dt_bias, dgc, dbeta, dproj)


@jax.custom_vjp
def _inv_given(L, T):
    return T


def _inv_given_bwd(T, ct):
    dL = -_nt(_tn(T, ct), T)
    return dL, jnp.zeros_like(T)


_inv_given.defvjp(lambda L, T: (T, T), _inv_given_bwd)


REP = GDN_V_HEADS // GDN_QK_HEADS


def _gdn_intra_all(qs, ks, vs, gcols, bcols, Ts=None):
    H = len(vs)
    C = vs[0].shape[0]
    ii = lax.broadcasted_iota(jnp.int32, (C, C), 0)
    jj = lax.broadcasted_iota(jnp.int32, (C, C), 1)
    grows = [jnp.sum(jnp.where(ii == jj, g, 0.0), axis=0, keepdims=True) for g in gcols]
    decs = [jnp.exp(jnp.where(ii >= jj, gcols[h] - grows[h], NEG)) for h in range(H)]
    kbs = [ks[h // REP] * bcols[h] for h in range(H)]
    As = [_mm_nt(kbs[h], ks[h // REP]) for h in range(H)]
    Ls = [jnp.where(ii > jj, As[h] * decs[h], 0.0) for h in range(H)]
    if Ts is None:
        T = _neumann_inv_batched(Ls)
    else:
        T = [_inv_given(Ls[h], Ts[h]) for h in range(H)]
    us = [_mm(T[h], vs[h] * bcols[h]) for h in range(H)]
    ws = [_mm(T[h], kbs[h] * jnp.exp(gcols[h])) for h in range(H)]
    qk = [_mm_nt(qs[p], ks[p]) for p in range(H // REP)]
    return us, ws, [qk[h // REP] * decs[h] for h in range(H)], T


def _neumann_inv_batched(Ls):
    n, C = 4, Ls[0].shape[0]
    r0 = lax.broadcasted_iota(jnp.int32, (n * C, n * C), 0)
    c0 = lax.broadcasted_iota(jnp.int32, (n * C, n * C), 1)
    same = (r0 // C) == (c0 // C)

    def blockdiag(split):
        return tuple(jnp.where(same, jnp.concatenate([x] * n, axis=0), jnp.zeros((), bf16)) for x in split)

    Ms = [jnp.concatenate(Ls[b:b + n], axis=1) for b in range(0, len(Ls), n)]
    eye = (lax.broadcasted_iota(jnp.int32, (C, n * C), 0) == (lax.broadcasted_iota(jnp.int32, (C, n * C), 1) & (C - 1))).astype(f32)
    Ps = [eye - M for M in Ms]
    Ss = [_split2(M) for M in Ms]
    Bs = [blockdiag(S) for S in Ss]
    k = 1
    while 2 * k < C:
        Ss = [_split2(_dot3(S, B)) for S, B in zip(Ss, Bs)]
        Bs = [blockdiag(S) for S in Ss]
        Ps = [P + _dot3(_split2(P), B) for P, B in zip(Ps, Bs)]
        k *= 2
    return [P[:, h * C:(h + 1) * C] for P in Ps for h in range(n)]


def _gdn_scan_all(qs, ks, gcols, us, ws, attns, S0s):
    H = len(us)
    C = us[0].shape[0]
    last = lax.broadcasted_iota(jnp.int32, (C, 1), 0) == C - 1
    glast = [jnp.sum(jnp.where(last, g, 0.0), axis=0, keepdims=True) for g in gcols]
    wS = [_mm(ws[h], S0s[h]) for h in range(H)]
    qS = [_mm(qs[h // REP] * jnp.exp(gcols[h]), S0s[h]) for h in range(H)]
    vn = [us[h] - wS[h] for h in range(H)]
    av = [_mm(attns[h], vn[h]) for h in range(H)]
    kv = [_mm_tn(ks[h // REP] * jnp.exp(glast[h] - gcols[h]), vn[h]) for h in range(H)]
    return [qS[h] + av[h] for h in range(H)], [S0s[h] * jnp.exp(glast[h]) + kv[h] for h in range(H)]


def _head_cols(blk):
    lane = lax.broadcasted_iota(jnp.int32, blk.shape, 1)
    return [jnp.sum(jnp.where(lane == h, blk, 0.0), axis=1, keepdims=True) for h in range(GDN_V_HEADS)]


def _head_lanes(cols):
    lane = lax.broadcasted_iota(jnp.int32, (cols[0].shape[0], 128), 1)
    out = jnp.zeros((cols[0].shape[0], 128), f32)
    for h, c in enumerate(cols):
        out = out + jnp.where(lane == h, c, 0.0)
    return out


CPS = 2


def _rows_of(c):
    return slice(c * CHUNK, (c + 1) * CHUNK)


def _heads(ref, n, c):
    return [ref[_rows_of(c), h * HD:(h + 1) * HD].astype(f32) for h in range(n)]


def _mats(ref, c):
    return [ref[c, h].astype(f32) for h in range(GDN_V_HEADS)]


def _gdn_specs(NB, rv=None):
    ix = (lambda n: n) if rv is None else rv
    R = CPS * CHUNK
    qs = pl.BlockSpec((R, GDN_QK_W), lambda n: (ix(n), 0))
    ks = pl.BlockSpec((R, GDN_QK_W), lambda n: (ix(n), 1))
    vs = pl.BlockSpec((R, GDN_V_W), lambda n: (ix(n), 1))
    g1 = pl.BlockSpec((R, 128), lambda n: (ix(n), 0))
    wide = pl.BlockSpec((R, GDN_V_W), lambda n: (ix(n), 0))
    sq = pl.BlockSpec((CPS, GDN_V_HEADS, CHUNK, CHUNK), lambda n: (ix(n), 0, 0, 0))
    st = pl.BlockSpec((CPS, GDN_V_HEADS, HD, HD), lambda n: (ix(n), 0, 0, 0))
    return qs, ks, vs, g1, wide, sq, st


def gdn_intra_fwd(qkvc, gc, beta, name, side=None):
    S = qkvc.shape[0]
    NC = S // CHUNK

    def body(q_ref, k_ref, v_ref, gc_ref, be_ref, u_ref, w_ref, at_ref, T_ref):
        for c in range(CPS):
            rows = _rows_of(c)
            us, ws, attns, Ts = _gdn_intra_all(_heads(q_ref, GDN_QK_HEADS, c), _heads(k_ref, GDN_QK_HEADS, c), _heads(v_ref, GDN_V_HEADS, c),
                                               _head_cols(gc_ref[rows, :]), _head_cols(be_ref[rows, :]))
            for h in range(GDN_V_HEADS):
                u_ref[rows, h * HD:(h + 1) * HD] = us[h]
                w_ref[rows, h * HD:(h + 1) * HD] = ws[h].astype(bf16)
                at_ref[c, h] = attns[h].astype(bf16)
                T_ref[c, h] = Ts[h].astype(bf16)

    qs, ks, vs, g1, wide, sq, _ = _gdn_specs(NC // CPS)
    return _call(
        body, name=name, grid=(NC // CPS,),
        in_specs=[qs, ks, vs, g1, g1], out_specs=[wide, wide, sq, sq],
        out_shape=[SDS((S, GDN_V_W), f32), SDS((S, GDN_V_W), bf16),
                   SDS((NC, GDN_V_HEADS, CHUNK, CHUNK), bf16), SDS((NC, GDN_V_HEADS, CHUNK, CHUNK), bf16)],
        args=(qkvc, qkvc, qkvc, gc, beta), side=side)


def gdn_scan_fwd(qkvc, gc, u, w, attn, name, side=None):
    S = qkvc.shape[0]
    NC = S // CHUNK

    def body(q_ref, k_ref, gc_ref, u_ref, w_ref, at_ref, o_ref, st_ref, state):
        @pl.when(pl.program_id(0) == 0)
        def _():
            state[...] = jnp.zeros_like(state)

        for c in range(CPS):
            rows = _rows_of(c)
            S0s = [state[h] for h in range(GDN_V_HEADS)]
            os_, S1s = _gdn_scan_all(_heads(q_ref, GDN_QK_HEADS, c), _heads(k_ref, GDN_QK_HEADS, c), _head_cols(gc_ref[rows, :]),
                                     _heads(u_ref, GDN_V_HEADS, c), _heads(w_ref, GDN_V_HEADS, c), _mats(at_ref, c), S0s)
            for h in range(GDN_V_HEADS):
                o_ref[rows, h * HD:(h + 1) * HD] = os_[h]
                st_ref[c, h] = S0s[h].astype(bf16)
                state[h] = S1s[h]

    qs, ks, _, g1, wide, sq, st = _gdn_specs(NC // CPS)
    return _call(
        body, name=name, grid=(NC // CPS,),
        in_specs=[qs, ks, g1, wide, wide, sq], out_specs=[wide, st],
        out_shape=[SDS((S, GDN_V_W), f32), SDS((NC, GDN_V_HEADS, HD, HD), bf16)],
        scratch=[pltpu.VMEM((GDN_V_HEADS, HD, HD), f32)], args=(qkvc, qkvc, gc, u, w, attn), side=side)


def gdn_scan_bwd(qkvc, gc, u, w, attn, states, do, name, side=None):
    S = qkvc.shape[0]
    NC = S // CHUNK
    NB = NC // CPS

    def body(q_ref, k_ref, gc_ref, u_ref, w_ref, at_ref, st_ref, do_ref,
             dq_ref, dk_ref, dgc_ref, du_ref, dw_ref, dat_ref, dstate):
        @pl.when(pl.program_id(0) == 0)
        def _():
            dstate[...] = jnp.zeros_like(dstate)

        VH = range(GDN_V_HEADS)
        for c in reversed(range(CPS)):
            rows = _rows_of(c)
            _, vjp = jax.vjp(_gdn_scan_all, _heads(q_ref, GDN_QK_HEADS, c), _heads(k_ref, GDN_QK_HEADS, c), _head_cols(gc_ref[rows, :]),
                             _heads(u_ref, GDN_V_HEADS, c), _heads(w_ref, GDN_V_HEADS, c), _mats(at_ref, c), _mats(st_ref, c))
            dqs, dks, dgs, dus, dws, dats, dS0s = vjp((_heads(do_ref, GDN_V_HEADS, c), [dstate[h] for h in VH]))
            for p in range(GDN_QK_HEADS):
                dq_ref[rows, p * HD:(p + 1) * HD] = dqs[p]
                dk_ref[rows, p * HD:(p + 1) * HD] = dks[p]
            for h in VH:
                du_ref[rows, h * HD:(h + 1) * HD] = dus[h].astype(bf16)
                dw_ref[rows, h * HD:(h + 1) * HD] = dws[h].astype(bf16)
                dat_ref[c, h] = dats[h].astype(bf16)
                dstate[h] = dS0s[h]
            dgc_ref[rows, :] = _head_lanes(dgs)

    qs, ks, _, g1, wide, sq, st = _gdn_specs(NB, lambda n: NB - 1 - n)
    dqs = pl.BlockSpec((CPS * CHUNK, GDN_QK_W), lambda n: (NB - 1 - n, 0))
    return _call(
        body, name=name, grid=(NB,),
        in_specs=[qs, ks, g1, wide, wide, sq, st, wide],
        out_specs=[dqs, dqs, g1, wide, wide, sq],
        out_shape=[SDS((S, GDN_QK_W), f32), SDS((S, GDN_QK_W), f32), SDS((S, 128), f32), SDS((S, GDN_V_W), bf16),
                   SDS((S, GDN_V_W), bf16), SDS((NC, GDN_V_HEADS, CHUNK, CHUNK), bf16)],
        scratch=[pltpu.VMEM((GDN_V_HEADS, HD, HD), f32)], args=(qkvc, qkvc, gc, u, w, attn, states, do), side=side)


def gdn_intra_bwd(qkvc, gc, beta, Ts, du, dw, dattn, dq_s, dk_s, dgc_s, name, side=None):
    S = qkvc.shape[0]
    NC = S // CHUNK

    def body(q_ref, k_ref, v_ref, gc_ref, be_ref, T_ref, du_ref, dw_ref, dat_ref, dqs_ref, dks_ref, dgs_ref,
             dqkv_ref, dgc_ref, dbe_ref):
        VH = range(GDN_V_HEADS)
        for c in range(CPS):
            rows = _rows_of(c)
            Ts = _mats(T_ref, c)
            _, vjp = jax.vjp(lambda q_, k_, v_, g_, b_: _gdn_intra_all(q_, k_, v_, g_, b_, Ts)[:3],
                             _heads(q_ref, GDN_QK_HEADS, c), _heads(k_ref, GDN_QK_HEADS, c), _heads(v_ref, GDN_V_HEADS, c),
                             _head_cols(gc_ref[rows, :]), _head_cols(be_ref[rows, :]))
            dqs, dks, dvs, dgs, dbs = vjp((_heads(du_ref, GDN_V_HEADS, c), _heads(dw_ref, GDN_V_HEADS, c), _mats(dat_ref, c)))
            for p in range(GDN_QK_HEADS):
                dqkv_ref[rows, p * HD:(p + 1) * HD] = dqs[p] + dqs_ref[rows, p * HD:(p + 1) * HD]
                dqkv_ref[rows, GDN_QK_W + p * HD:GDN_QK_W + (p + 1) * HD] = dks[p] + dks_ref[rows, p * HD:(p + 1) * HD]
            for h in VH:
                dqkv_ref[rows, 2 * GDN_QK_W + h * HD:2 * GDN_QK_W + (h + 1) * HD] = dvs[h]
            dgc_ref[rows, :] = _head_lanes(dgs) + dgs_ref[rows, :]
            dbe_ref[rows, :] = _head_lanes(dbs)

    qs, ks, vs, g1, wide, sq, _ = _gdn_specs(NC // CPS)
    dqs = pl.BlockSpec((CPS * CHUNK, GDN_QK_W), lambda n: (n, 0))
    return _call(
        body, name=name, grid=(NC // CPS,),
        in_specs=[qs, ks, vs, g1, g1, sq, wide, wide, sq, dqs, dqs, g1],
        out_specs=[pl.BlockSpec((CPS * CHUNK, GDN_CONV), lambda n: (n, 0)), g1, g1],
        out_shape=[SDS((S, GDN_CONV), f32), SDS((S, 128), f32), SDS((S, 128), f32)],
        args=(qkvc, qkvc, qkvc, gc, beta, Ts, du, dw, dattn, dq_s, dk_s, dgc_s), side=side)


def _gated_norm(o, z, nw):
    parts = []
    for h in range(GDN_V_HEADS):
        oh = o[:, h * HD:(h + 1) * HD]
        r = lax.rsqrt(jnp.mean(oh * oh, axis=-1, keepdims=True) + EPS)
        parts.append((oh * r * nw) * _silu(z[:, h * HD:(h + 1) * HD]))
    return jnp.concatenate(parts, axis=1)


def gdn_out_fwd(o, proj, nw, W, x, gate, name, side=None):
    S, D = x.shape
    tm = _rows(S, 256)

    def body(o_ref, z_ref, nw_ref, w_ref, x_ref, g_ref, xn_ref, y_ref, og_ref):
        og = _gated_norm(o_ref[...], z_ref[...], nw_ref[...]).astype(bf16)
        y = jnp.dot(og, w_ref[...], preferred_element_type=f32)
        og_ref[...] = og
        y_ref[...] = y
        xn_ref[...] = x_ref[...] + g_ref[...] * y

    row = pl.BlockSpec((tm, D), lambda i: (i, 0))
    wide = pl.BlockSpec((tm, GDN_V_W), lambda i: (i, 0))
    return _call(
        body, name=name, grid=(S // tm,),
        in_specs=[wide, pl.BlockSpec((tm, GDN_V_W), lambda i: (i, 2)), pl.BlockSpec((1, HD), lambda i: (0, 0)),
                  pl.BlockSpec((GDN_V_W, D), lambda i: (0, 0)), row, pl.BlockSpec((1, D), lambda i: (0, 0))],
        out_specs=[row, row, wide],
        out_shape=[SDS((S, D), f32), SDS((S, D), f32), SDS((S, GDN_V_W), bf16)],
        args=(o, proj, nw, W, x, gate), side=side)


def gdn_out_bwd(dxn, y, gate, o, proj, nw, W, name):
    S, D = dxn.shape
    tm = _rows(S, 256)

    def body(dx_ref, y_ref, g_ref, o_ref, z_ref, nw_ref, w_ref, dy_ref, dg_ref, do_ref, dz_ref, dnw_ref):
        @pl.when(pl.program_id(0) == 0)
        def _():
            dg_ref[...] = jnp.zeros_like(dg_ref)
            dnw_ref[...] = jnp.zeros_like(dnw_ref)

        dx = dx_ref[...]
        dy = dx * g_ref[...]
        dy_ref[...] = dy
        dg_ref[...] += jnp.sum(dx * y_ref[...], axis=0, keepdims=True)
        dog = _nt(dy, w_ref[...])
        _, vjp = jax.vjp(_gated_norm, o_ref[...], z_ref[...], nw_ref[...])
        do, dz, dnw = vjp(dog)
        do_ref[...] = do
        dz_ref[...] = dz
        dnw_ref[...] += dnw

    row = pl.BlockSpec((tm, D), lambda i: (i, 0))
    wide = pl.BlockSpec((tm, GDN_V_W), lambda i: (i, 0))
    vecd = pl.BlockSpec((1, D), lambda i: (0, 0))
    vech = pl.BlockSpec((1, HD), lambda i: (0, 0))
    return pl.pallas_call(
        body, name=name, grid=(S // tm,),
        in_specs=[row, row, vecd, wide, pl.BlockSpec((tm, GDN_V_W), lambda i: (i, 2)), vech, pl.BlockSpec((GDN_V_W, D), lambda i: (0, 0))],
        out_specs=[row, vecd, wide, pl.BlockSpec((tm, GDN_V_W), lambda i: (i, 2)), vech],
        out_shape=[SDS((S, D), f32), SDS((1, D), f32), SDS((S, GDN_V_W), f32), SDS((S, GDN_IN_PAD), f32), SDS((1, HD), f32)],
        compiler_params=_cp(("arbitrary",)),
    )(dxn, y, gate, o, proj, nw, W)


def _rms_w(x, w):
    return (x * lax.rsqrt(jnp.mean(x * x, axis=-1, keepdims=True) + EPS)) * w


def _split3(c):
    hi = c.astype(bf16).astype(f32)
    r1 = c - hi
    mid = r1.astype(bf16).astype(f32)
    lo = (r1 - mid).astype(bf16).astype(f32)
    return hi, mid, lo


_FOX_F_BLK = 4 * FOX_W // 128


def fox_prep_fwd(proj, f_bias, qn_w, kn_w, name):
    S = proj.shape[0]
    tm = _rows(S, 256)

    def body(q_ref, k_ref, v_ref, f_ref, fb_ref, qw_ref, kw_ref, Q_ref, K_ref, V_ref, carry):
        @pl.when(pl.program_id(0) == 0)
        def _():
            carry[...] = jnp.zeros_like(carry)

        ii = lax.broadcasted_iota(jnp.int32, (tm, tm), 0)
        jj = lax.broadcasted_iota(jnp.int32, (tm, tm), 1)
        lf = jax.nn.log_sigmoid(f_ref[...] + fb_ref[...])
        cum = _hdot((ii >= jj).astype(f32), lf) + carry[...]
        carry[...] = cum[tm - 1:tm, :]
        lane = lax.broadcasted_iota(jnp.int32, (tm, 128), 1)
        lo = lane < FOX_D
        qw2 = jnp.concatenate([qw_ref[...], qw_ref[...]], axis=1) * FOX_D ** -0.5
        kw2 = jnp.concatenate([kw_ref[...], kw_ref[...]], axis=1)

        def norm_pair(x, w2):
            x2 = x * x
            s_all = jnp.sum(x2, axis=1, keepdims=True)
            s_lo = jnp.sum(jnp.where(lo, x2, 0.0), axis=1, keepdims=True)
            r = jnp.where(lo, lax.rsqrt(s_lo * (1.0 / FOX_D) + EPS), lax.rsqrt((s_all - s_lo) * (1.0 / FOX_D) + EPS))
            return x * r * w2

        for p in range(FOX_H // 2):
            ps = slice(p * 128, (p + 1) * 128)
            yq, yk, xv = norm_pair(q_ref[:, ps], qw2), norm_pair(k_ref[:, ps], kw2), v_ref[:, ps]
            for e in range(2):
                h = 2 * p + e
                hi, mid, lw = _split3(cum[:, h:h + 1])
                eq = jnp.where(lane == FOX_D, hi, jnp.where(lane == FOX_D + 1, mid, jnp.where(lane == FOX_D + 2, lw, jnp.where(lane < FOX_D + 6, 1.0, 0.0))))
                ek = jnp.where(lane < FOX_D + 3, 1.0, jnp.where(lane == FOX_D + 3, -hi, jnp.where(lane == FOX_D + 4, -mid, jnp.where(lane == FOX_D + 5, -lw, 0.0))))
                ev = jnp.where(lane == FOX_D, 1.0, 0.0)
                mv = (lambda a: a) if e == 0 else (lambda a: pltpu.roll(a, FOX_D, axis=1))
                Q_ref[:, h * 128:(h + 1) * 128] = jnp.where(lo, mv(yq), eq).astype(bf16)
                K_ref[:, h * 128:(h + 1) * 128] = jnp.where(lo, mv(yk), ek).astype(bf16)
                V_ref[:, h * 128:(h + 1) * 128] = jnp.where(lo, mv(xv), ev).astype(bf16)

    def colblk(c):
        return pl.BlockSpec((tm, FOX_W), lambda i: (i, c))

    pad = pl.BlockSpec((tm, FOX_PW), lambda i: (i, 0))
    return pl.pallas_call(
        body, name=name, grid=(S // tm,),
        in_specs=[colblk(0), colblk(1), colblk(2), pl.BlockSpec((tm, 128), lambda i: (i, _FOX_F_BLK)),
                  pl.BlockSpec((1, 128), lambda i: (0, 0)), pl.BlockSpec((1, FOX_D), lambda i: (0, 0)), pl.BlockSpec((1, FOX_D), lambda i: (0, 0))],
        out_specs=[pad, pad, pad],
        out_shape=[SDS((S, FOX_PW), bf16)] * 3,
        scratch_shapes=[pltpu.VMEM((1, 128), f32)],
        compiler_params=_cp(("arbitrary",)),
    )(proj, proj, proj, proj, f_bias, qn_w, kn_w)


def fox_prep_bwd(proj, f_bias, qn_w, kn_w, dQ, dK, dV, dz, name):
    S = proj.shape[0]
    tm = _rows(S, 256)
    NB = S // tm

    def body(q_ref, k_ref, f_ref, fb_ref, qw_ref, kw_ref, dQ_ref, dK_ref, dV_ref, dz_ref,
             dp_ref, dfb_ref, dqw_ref, dkw_ref, carry):
        @pl.when(pl.program_id(0) == 0)
        def _():
            carry[...] = jnp.zeros_like(carry)
            dfb_ref[...] = jnp.zeros_like(dfb_ref)
            dqw_ref[...] = jnp.zeros_like(dqw_ref)
            dkw_ref[...] = jnp.zeros_like(dkw_ref)

        lane = lax.broadcasted_iota(jnp.int32, (tm, 128), 1)
        lo = lane < FOX_D
        qw2 = jnp.concatenate([qw_ref[...], qw_ref[...]], axis=1) * FOX_D ** -0.5
        kw2 = jnp.concatenate([kw_ref[...], kw_ref[...]], axis=1)

        def pair(ref, p):
            return jnp.where(lo, ref[:, 2 * p * 128:(2 * p + 1) * 128], pltpu.roll(ref[:, (2 * p + 1) * 128:(2 * p + 2) * 128], FOX_D, axis=1))

        def norm_pair_bwd(x, w2, dy):
            x2 = x * x
            s_all = jnp.sum(x2, axis=1, keepdims=True)
            s_lo = jnp.sum(jnp.where(lo, x2, 0.0), axis=1, keepdims=True)
            r = jnp.where(lo, lax.rsqrt(s_lo * (1.0 / FOX_D) + EPS), lax.rsqrt((s_all - s_lo) * (1.0 / FOX_D) + EPS))
            t = dy * w2 * x
            t_all = jnp.sum(t, axis=1, keepdims=True)
            t_lo = jnp.sum(jnp.where(lo, t, 0.0), axis=1, keepdims=True)
            dx = r * (w2 * dy - x * (r * r) * (jnp.where(lo, t_lo, t_all - t_lo) * (1.0 / FOX_D)))
            return dx, jnp.sum(dy * x * r, axis=0, keepdims=True)

        dcum = jnp.zeros((tm, 128), f32)
        dqw2 = jnp.zeros((1, 128), f32)
        dkw2 = jnp.zeros((1, 128), f32)
        for p in range(FOX_H // 2):
            ps = slice(p * 128, (p + 1) * 128)
            dxq, dw1 = norm_pair_bwd(q_ref[:, ps], qw2, pair(dQ_ref, p))
            dxk, dw2 = norm_pair_bwd(k_ref[:, ps], kw2, pair(dK_ref, p))
            dp_ref[:, p * 128:(p + 1) * 128] = dxq
            dp_ref[:, FOX_W + p * 128:FOX_W + (p + 1) * 128] = dxk
            dp_ref[:, 2 * FOX_W + p * 128:2 * FOX_W + (p + 1) * 128] = pair(dV_ref, p)
            dqw2 = dqw2 + dw1
            dkw2 = dkw2 + dw2
            for e in range(2):
                h = 2 * p + e
                dcum = dcum + jnp.where(lane == h, dQ_ref[:, h * 128 + FOX_D:h * 128 + FOX_D + 1]
                                        - dK_ref[:, h * 128 + FOX_D + 3:h * 128 + FOX_D + 4], 0.0)
        dp_ref[:, 3 * FOX_W:4 * FOX_W] = dz_ref[...]
        ii = lax.broadcasted_iota(jnp.int32, (tm, tm), 0)
        jj = lax.broadcasted_iota(jnp.int32, (tm, tm), 1)
        dlf = _hdot((ii <= jj).astype(f32), dcum) + carry[...]
        carry[...] += jnp.sum(dcum, axis=0, keepdims=True)
        df = dlf * jax.nn.sigmoid(-(f_ref[...] + fb_ref[...]))
        dp_ref[:, 4 * FOX_W:FOX_IN_PAD] = df
        dfb_ref[...] += jnp.sum(df, axis=0, keepdims=True)
        dqw_ref[...] += (dqw2[:, :FOX_D] + dqw2[:, FOX_D:]) * FOX_D ** -0.5
        dkw_ref[...] += dkw2[:, :FOX_D] + dkw2[:, FOX_D:]

    rv = lambda i: NB - 1 - i

    def colblk(c):
        return pl.BlockSpec((tm, FOX_W), lambda i: (rv(i), c))

    pad = pl.BlockSpec((tm, FOX_PW), lambda i: (rv(i), 0))
    cmp_ = pl.BlockSpec((tm, FOX_W), lambda i: (rv(i), 0))
    v128 = pl.BlockSpec((1, 128), lambda i: (0, 0))
    v64 = pl.BlockSpec((1, FOX_D), lambda i: (0, 0))
    return pl.pallas_call(
        body, name=name, grid=(NB,),
        in_specs=[colblk(0), colblk(1), pl.BlockSpec((tm, 128), lambda i: (rv(i), _FOX_F_BLK)), v128, v64, v64, pad, pad, pad, cmp_],
        out_specs=[pl.BlockSpec((tm, FOX_IN_PAD), lambda i: (rv(i), 0)), v128, v64, v64],
        out_shape=[SDS((S, FOX_IN_PAD), f32), SDS((1, 128), f32), SDS((1, FOX_D), f32), SDS((1, FOX_D), f32)],
        scratch_shapes=[pltpu.VMEM((1, 128), f32)],
        compiler_params=_cp(("arbitrary",)),
    )(proj, proj, proj, f_bias, qn_w, kn_w, dQ, dK, dV, dz)


FOX_HB = 2


def _diag_mask(t):
    return lax.broadcasted_iota(jnp.int32, (t, t), 1) <= lax.broadcasted_iota(jnp.int32, (t, t), 0)


def fox_attn_fwd(Q, K, V, name, side=None):
    S = Q.shape[0]
    t = _rows(S, 512)

    HB = FOX_HB
    HS = [slice(h * 128, (h + 1) * 128) for h in range(HB)]

    def body(q_ref, k_ref, v_ref, o_ref, m_sc, acc_sc, s_sc):
        i = pl.program_id(1)
        qs = [q_ref[:, sl] for sl in HS]
        m_sc[...] = jnp.full_like(m_sc, NEG)
        acc_sc[...] = jnp.zeros_like(acc_sc)

        def scores(j):
            j0 = pl.multiple_of(j * t, t)
            return [_nt(qs[h], k_ref[pl.ds(j0, t), HS[h]]) for h in range(HB)]

        def tile(j, diag):
            j0 = pl.multiple_of(j * t, t)
            ss = [s_sc[h] for h in range(HB)]
            if diag:
                ss = [jnp.where(_diag_mask(t), s, NEG) for s in ss]
            else:
                nxt = scores(j + 1)
            ms = [m_sc[h] for h in range(HB)]
            m_new = [jnp.maximum(ms[h], jnp.max(ss[h], axis=1, keepdims=True)) for h in range(HB)]
            ps = [jnp.exp(ss[h] - m_new[h]) for h in range(HB)]
            pv = [_nn(ps[h], v_ref[pl.ds(j0, t), HS[h]]) for h in range(HB)]
            for h in range(HB):
                acc_sc[h] = acc_sc[h] * jnp.exp(ms[h] - m_new[h]) + pv[h]
                m_sc[h] = m_new[h]
                if not diag:
                    s_sc[h] = nxt[h]

        def off_diag(j, c):
            tile(j, False)
            return c

        first = scores(0)
        for h in range(HB):
            s_sc[h] = first[h]
        lax.fori_loop(0, i, off_diag, 0)
        tile(i, True)
        lane = lax.broadcasted_iota(jnp.int32, (t, 128), 1)
        for h in range(HB):
            acc = acc_sc[h]
            l = acc[:, FOX_D:FOX_D + 1]
            o_ref[:, HS[h]] = jnp.where(lane == FOX_D, m_sc[h] + jnp.log(l), acc / l)

    blk = pl.BlockSpec((t, HB * 128), lambda h, i: (i, h))
    seq = pl.BlockSpec((S, HB * 128), lambda h, i: (0, h))
    return _call(
        body, name=name, grid=(FOX_H // HB, S // t),
        in_specs=[blk, seq, seq], out_specs=[blk], out_shape=[SDS((S, FOX_PW), f32)],
        scratch=[pltpu.VMEM((HB, t, 1), f32), pltpu.VMEM((HB, t, 128), f32), pltpu.VMEM((HB, t, t), f32)],
        args=(Q, K, V), side=side)


def fox_attn_bwd(Q, K, V, dO, O, name, side=None):
    S = Q.shape[0]
    t = _rows(S, 512)
    nq = S // t

    HB = FOX_HB
    HS = [slice(h * 128, (h + 1) * 128) for h in range(HB)]

    def body(k_ref, v_ref, q_ref, do_ref, o_ref, dq_ref, dk_ref, dv_ref):
        j = pl.program_id(1)

        @pl.when(j == 0)
        def _():
            dq_ref[...] = jnp.zeros_like(dq_ref)

        dk_ref[...] = jnp.zeros_like(dk_ref)
        dv_ref[...] = jnp.zeros_like(dv_ref)
        ks = [k_ref[:, sl] for sl in HS]
        vs = [v_ref[:, sl] for sl in HS]

        def tile(i, diag):
            i0 = pl.multiple_of(i * t, t)
            R = range(HB)
            qs = [q_ref[pl.ds(i0, t), HS[h]] for h in R]
            dos = [do_ref[pl.ds(i0, t), HS[h]] for h in R]
            ss = [_nt(qs[h], ks[h]) - o_ref[pl.ds(i0, t), h * 128 + FOX_D:h * 128 + FOX_D + 1] for h in R]
            if diag:
                ss = [jnp.where(_diag_mask(t), s, NEG) for s in ss]
            ps = [jnp.exp(s) for s in ss]
            dps = [_nt(dos[h], vs[h]) for h in R]
            dvs = [_tn(ps[h], dos[h]) for h in R]
            dss = [(ps[h] * dps[h]).astype(bf16) for h in R]
            dks = [_tn(dss[h], qs[h]) for h in R]
            dqs = [_nn(dss[h], ks[h]) for h in R]
            for h in R:
                dv_ref[:, HS[h]] += dvs[h]
                dk_ref[:, HS[h]] += dks[h]
                dq_ref[pl.ds(i0, t), HS[h]] += dqs[h]

        tile(j, True)

        def off_diag(i, c):
            tile(i, False)
            return c

        lax.fori_loop(j + 1, nq, off_diag, 0)

    blk = pl.BlockSpec((t, HB * 128), lambda h, j: (j, h))
    seq = pl.BlockSpec((S, HB * 128), lambda h, j: (0, h))
    return _call(
        body, name=name, grid=(FOX_H // HB, nq),
        in_specs=[blk, blk, seq, seq, seq], out_specs=[seq, blk, blk],
        out_shape=[SDS((S, FOX_PW), f32)] * 3, args=(K, V, Q, dO, O), side=side)


def fox_out_fwd(O, proj, W, x, gate, name):
    S, D = x.shape
    tm = _rows(S, 256)

    def body(o_ref, z_ref, w_ref, x_ref, g_ref, xn_ref, y_ref, og_ref):
        z = z_ref[...]
        og = jnp.concatenate([o_ref[:, h * 128:h * 128 + FOX_D] * _silu(z[:, h * FOX_D:(h + 1) * FOX_D]) for h in range(FOX_H)],
                             axis=1).astype(bf16)
        y = jnp.dot(og, w_ref[...], preferred_element_type=f32)
        og_ref[...] = og
        y_ref[...] = y
        xn_ref[...] = x_ref[...] + g_ref[...] * y

    row = pl.BlockSpec((tm, D), lambda i: (i, 0))
    cmp_ = pl.BlockSpec((tm, FOX_W), lambda i: (i, 0))
    return pl.pallas_call(
        body, name=name, grid=(S // tm,),
        in_specs=[pl.BlockSpec((tm, FOX_PW), lambda i: (i, 0)), pl.BlockSpec((tm, FOX_W), lambda i: (i, 3)),
                  pl.BlockSpec((FOX_W, D), lambda i: (0, 0)), row, pl.BlockSpec((1, D), lambda i: (0, 0))],
        out_specs=[row, row, cmp_],
        out_shape=[SDS((S, D), f32), SDS((S, D), f32), SDS((S, FOX_W), bf16)],
        compiler_params=_cp(("arbitrary",)),
    )(O, proj, W, x, gate)


def fox_out_bwd(dxn, y, gate, O, proj, W, name):
    S, D = dxn.shape
    tm = _rows(S, 256)

    def body(dx_ref, y_ref, g_ref, o_ref, z_ref, w_ref, dy_ref, dg_ref, dO_ref, dz_ref):
        @pl.when(pl.program_id(0) == 0)
        def _():
            dg_ref[...] = jnp.zeros_like(dg_ref)

        dx = dx_ref[...]
        dy = dx * g_ref[...]
        dy_ref[...] = dy
        dg_ref[...] += jnp.sum(dx * y_ref[...], axis=0, keepdims=True)
        dog = _nt(dy, w_ref[...])
        z = z_ref[...]
        lane = lax.broadcasted_iota(jnp.int32, (tm, FOX_D), 1)
        dzs = []
        for h in range(FOX_H):
            sl = slice(h * FOX_D, (h + 1) * FOX_D)
            zh = z[:, sl]
            sg = jax.nn.sigmoid(zh)
            oh = o_ref[:, h * 128:h * 128 + FOX_D]
            doh = dog[:, sl] * (zh * sg)
            delta = jnp.sum(doh * oh, axis=1, keepdims=True)
            dO_ref[:, h * 128:(h + 1) * 128] = jnp.concatenate([doh, jnp.where(lane == 0, -delta, 0.0)], axis=1).astype(bf16)
            dzs.append(dog[:, sl] * oh * (sg * (1.0 + zh * (1.0 - sg))))
        dz_ref[...] = jnp.concatenate(dzs, axis=1)

    row = pl.BlockSpec((tm, D), lambda i: (i, 0))
    vecd = pl.BlockSpec((1, D), lambda i: (0, 0))
    pad = pl.BlockSpec((tm, FOX_PW), lambda i: (i, 0))
    return pl.pallas_call(
        body, name=name, grid=(S // tm,),
        in_specs=[row, row, vecd, pad, pl.BlockSpec((tm, FOX_W), lambda i: (i, 3)), pl.BlockSpec((FOX_W, D), lambda i: (0, 0))],
        out_specs=[row, vecd, pad, pl.BlockSpec((tm, FOX_W), lambda i: (i, 0))],
        out_shape=[SDS((S, D), f32), SDS((1, D), f32), SDS((S, FOX_PW), bf16), SDS((S, FOX_W), f32)],
        compiler_params=_cp(("arbitrary",)),
    )(dxn, y, gate, O, proj, W)


def final_loss(x, fw, target, name):
    S, D = x.shape
    tm = _rows(S, 512)

    def body(x_ref, w_ref, t_ref, l_ref, dx_ref, dw_ref):
        @pl.when(pl.program_id(0) == 0)
        def _():
            l_ref[...] = jnp.zeros_like(l_ref)
            dw_ref[...] = jnp.zeros_like(dw_ref)

        out, vjp = jax.vjp(_rms_w, x_ref[...], w_ref[...])
        err = out - t_ref[...]
        l_ref[...] += 0.5 * jnp.sum(jnp.sum(err * err, axis=1, keepdims=True) * (1.0 / D), axis=0, keepdims=True)
        dx, dw = vjp(err * (1.0 / D))
        dx_ref[...] = dx
        dw_ref[...] += dw

    row = pl.BlockSpec((tm, D), lambda i: (i, 0))
    vec = pl.BlockSpec((1, D), lambda i: (0, 0))
    return pl.pallas_call(
        body, name=name, grid=(S // tm,),
        in_specs=[row, vec, row], out_specs=[pl.BlockSpec((1, 128), lambda i: (0, 0)), row, vec],
        out_shape=[SDS((1, 128), f32), SDS((S, D), f32), SDS((1, D), f32)],
        compiler_params=_cp(("arbitrary",)),
    )(x, fw, target)


def ada_fwd(c_all, ada_w, name):
    L, D, n = ada_w.shape

    def body(c_ref, w_ref, o_ref):
        cond = jnp.concatenate([_silu(c_ref[...]), jnp.zeros((8, D), f32)], axis=0)
        o_ref[0] = _nn(cond, w_ref[0])[0:8]

    return pl.pallas_call(
        body, name=name, grid=(L,),
        in_specs=[pl.BlockSpec((NDEV, D), lambda l: (0, 0)), pl.BlockSpec((1, D, n), lambda l: (l, 0, 0))],
        out_specs=pl.BlockSpec((1, NDEV, n), lambda l: (l, 0, 0)),
        out_shape=SDS((L, NDEV, n), f32),
        compiler_params=_cp(("arbitrary",)),
    )(c_all, ada_w)


def ada_grad(c_all, dmod, name):
    L, _, n = dmod.shape
    D = c_all.shape[1]

    def body(c_ref, d_ref, o_ref):
        cond = jnp.concatenate([_silu(c_ref[...]), jnp.zeros((8, D), f32)], axis=0)
        dm = jnp.concatenate([d_ref[0], jnp.zeros((8, n), f32)], axis=0)
        o_ref[0] = _tn(cond, dm)

    return pl.pallas_call(
        body, name=name, grid=(L,),
        in_specs=[pl.BlockSpec((NDEV, D), lambda l: (0, 0)), pl.BlockSpec((1, NDEV, n), lambda l: (l, 0, 0))],
        out_specs=pl.BlockSpec((1, D, n), lambda l: (l, 0, 0)),
        out_shape=SDS((L, D, n), f32),
        compiler_params=_cp(("arbitrary",)),
    )(c_all, dmod)


def reduce_adam(parts, w, m, v, tr, name):
    n, R, C = parts.shape
    c1 = 1.0 / (1.0 - ADAM_B1 ** ADAM_STEP)
    c2 = 1.0 / (1.0 - ADAM_B2 ** ADAM_STEP)

    def body(p_ref, w_ref, m_ref, v_ref, g_ref, d_ref, nm_ref, nv_ref):
        g = p_ref[0].astype(f32)
        for s in range(1, n):
            g = g + p_ref[s].astype(f32)
        nm = ADAM_B1 * m_ref[...] + (1.0 - ADAM_B1) * g
        nv = ADAM_B2 * v_ref[...] + (1.0 - ADAM_B2) * (g * g)
        g_ref[...] = g
        nm_ref[...] = nm
        nv_ref[...] = nv
        d_ref[...] = -ADAM_LR * ((nm * c1) / (jnp.sqrt(nv * c2) + ADAM_EPS) + ADAM_WD * w_ref[...])

    blk = pl.BlockSpec((tr, C), lambda i: (i, 0))
    return pl.pallas_call(
        body, name=name, grid=(R // tr,),
        in_specs=[pl.BlockSpec((n, tr, C), lambda i: (0, i, 0)), blk, blk, blk],
        out_specs=[blk] * 4, out_shape=[SDS((R, C), f32)] * 4,
        compiler_params=_cp(("arbitrary",)),
    )(parts, w, m, v)


def all_gather(xs, name):
    n = len(xs)

    def body(*refs):
        x_refs, out_refs = refs[:n], refs[n:2 * n]
        send_sems, recv_sems, local_sems = refs[2 * n:]
        x_, y_, c_ = _my_pos()
        me, sibling = (x_, y_, c_), (x_, y_, 1 - c_)
        chips = [(1 - x_, y_), (x_, 1 - y_), (1 - x_, 1 - y_)]

        def rows(a, px, py, pc):
            return out_refs[a].at[4 * px + 2 * py + pc]

        def copy(a, k, block, to, own=False):
            return pltpu.make_async_remote_copy(
                src_ref=x_refs[a] if own else rows(a, *block), dst_ref=rows(a, *block),
                send_sem=send_sems.at[k, a], recv_sem=recv_sems.at[k, a], device_id=to, device_id_type=pl.DeviceIdType.MESH)

        mine = [pltpu.make_async_copy(x_refs[a], rows(a, *me), local_sems.at[a]) for a in range(n)]
        for cp in mine:
            cp.start()
        first = []
        for a in range(n):
            first.append(copy(a, 0, me, sibling, own=True))
            first += [copy(a, 1 + j, me, (*chip, c_), own=True) for j, chip in enumerate(chips)]
        for cp in first:
            cp.start()
        passed = []
        for j, chip in enumerate(chips):
            for a in range(n):
                copy(a, 1 + j, (*chip, c_), me).wait_recv()
                cp = copy(a, 4 + j, (*chip, c_), sibling)
                cp.start()
                passed.append(cp)
        for a in range(n):
            copy(a, 0, sibling, me).wait_recv()
            for j, chip in enumerate(chips):
                copy(a, 4 + j, (*chip, 1 - c_), me).wait_recv()
        for cp in first + passed:
            cp.wait_send()
        for cp in mine:
            cp.wait()

    any_ = pl.BlockSpec(memory_space=pl.ANY)
    return pl.pallas_call(
        body, name=name, out_shape=[SDS((NDEV,) + x.shape, x.dtype) for x in xs],
        in_specs=[any_] * n, out_specs=[any_] * n,
        scratch_shapes=[pltpu.SemaphoreType.DMA((7, n)), pltpu.SemaphoreType.DMA((7, n)), pltpu.SemaphoreType.DMA((n,))],
    )(*xs)


GDN_COLS = ((0, GDN_CONV + GDN_V_W, 0), (GDN_CONV + GDN_V_W, GDN_CONV + GDN_V_W + 16, GDN_CONV + GDN_V_W),
            (GDN_CONV + GDN_V_W + 16, GDN_IN, GDN_CONV + GDN_V_W + 128))
FOX_COLS = ((0, FOX_IN, 0),)


def _col_pieces(d, per, cols):
    lo, hi = per * d, per * (d + 1)
    out = []
    for a, b, dst in cols:
        s, e = max(lo, a), min(hi, b)
        if s < e:
            out.append((s - lo, e - s, dst + s - a))
    return out


def cols_from_blocks(g, cols, n_out, name):
    _, L, R, C = g.shape
    tr = min(256, R)

    def body(g_ref, o_ref):
        o_ref[...] = jnp.zeros_like(o_ref)
        for d in range(NDEV):
            for off, ln, dst in _col_pieces(d, C, cols):
                o_ref[0, :, dst:dst + ln] = g_ref[d, 0, :, off:off + ln]

    return pl.pallas_call(
        body, name=name, grid=(L, R // tr),
        in_specs=[pl.BlockSpec((NDEV, 1, tr, C), lambda l, i: (0, l, i, 0))],
        out_specs=pl.BlockSpec((1, tr, n_out), lambda l, i: (l, i, 0)),
        out_shape=SDS((L, R, n_out), g.dtype),
        compiler_params=_cp(("arbitrary", "arbitrary")),
    )(g)


def blocks_from_cols(dw, C, cols, name):
    R, n_in = dw.shape
    tr = min(256, R)

    def body(x_ref, o_ref):
        for d in range(NDEV):
            for off, ln, src in _col_pieces(d, C, cols):
                o_ref[d, :, off:off + ln] = x_ref[:, src:src + ln].astype(bf16)

    return pl.pallas_call(
        body, name=name, grid=(R // tr,),
        in_specs=[pl.BlockSpec((tr, n_in), lambda i: (i, 0))],
        out_specs=pl.BlockSpec((NDEV, tr, C), lambda i: (0, i, 0)),
        out_shape=SDS((NDEV, R, C), bf16),
        compiler_params=_cp(("arbitrary",)),
    )(dw)


BIG = ("a_w_in", "a_conv_w", "a_w_out", "b_w_in", "b_w_out")
SMALL = ("norm_w", "ada_b", "a_A_log", "a_dt_bias", "a_norm_w", "b_f_bias", "b_qn_w", "b_kn_w", "final_norm_w")


def _pack_small(arrs):
    rows = []
    for a in arrs:
        fl = a.reshape(-1)
        pad = (-fl.shape[0]) % 128
        if pad:
            fl = jnp.concatenate([fl, jnp.zeros((pad,), fl.dtype)])
        rows.append(fl)
    flat = jnp.concatenate(rows)
    pad = (-flat.shape[0]) % (8 * 128)
    if pad:
        flat = jnp.concatenate([flat, jnp.zeros((pad,), flat.dtype)])
    return flat.reshape(-1, 128)


def _unpack(packed, shapes, align):
    flat = packed.reshape(-1)
    out, off = [], 0
    for shp in shapes:
        n = 1
        for d in shp:
            n *= d
        out.append(flat[off:off + n].reshape(shp))
        off += n + ((-n) % align)
    return out


def _full_from_gathered(g, shard_shape, axis):
    g = jnp.moveaxis(g, 0, axis)
    shp = list(shard_shape)
    shp[axis] *= NDEV
    return g.reshape(shp)


def _pad_lanes(v, n=128):
    v = v.reshape(1, -1)
    return jnp.concatenate([v, jnp.zeros((1, n - v.shape[1]), v.dtype)], axis=1)


def _carried(fn, *args, side=None, **grads):
    if callable(side):
        side = side(**grads)
    res = fn(*args, side)
    return res if side is not None else (res, None)


def gdn_layer_fwd(x, mod, nw, weights, tag, sides):
    W_in, conv_w, A_log, dt_bias, a_nw, W_out = weights
    shift, scale, gate = mod
    got = {}
    (proj, h), got["inproj"] = _carried(inproj_fwd, x, nw, scale, shift, W_in, GDN_TN, f"{tag}_inproj", side=sides.get("inproj"))
    qkvc = gdn_prep_fwd(proj, conv_w, f"{tag}_prep")
    gc, beta = gdn_gates_fwd(proj, A_log, dt_bias, f"{tag}_gates")
    (u, w, attn, Ts), got["intra"] = _carried(gdn_intra_fwd, qkvc, gc, beta, f"{tag}_intra", side=sides.get("intra"))
    (o, states), got["scan"] = _carried(gdn_scan_fwd, qkvc, gc, u, w, attn, f"{tag}_scan", side=sides.get("scan"))
    (x_new, y, og), got["out"] = _carried(gdn_out_fwd, o, proj, a_nw, W_out, x, gate, f"{tag}_out", side=sides.get("out"))
    return x_new, (x, proj, h, qkvc, gc, beta, o, states, Ts, y, og, u, w, attn), got


def gdn_layer_bwd(dxn, saved, mod, nw, weights, tag, sides):
    W_in, conv_w, A_log, dt_bias, a_nw, W_out = weights
    shift, scale, gate = mod
    x, proj, h, qkvc, gc, beta, o, states, Ts, y, og, u, w, attn = saved
    got = {}
    dy, dgate, do, dproj, da_nw = gdn_out_bwd(dxn, y, gate, o, proj, a_nw, W_out, f"{tag}_out_bwd")
    dW_out, = matmul_tn(og, dy, 512, f"{tag}_dwout")
    (dq_s, dk_s, dgc_s, du, dw, dattn), got["sbwd"] = _carried(
        gdn_scan_bwd, qkvc, gc, u, w, attn, states, do, f"{tag}_scan_bwd", side=sides.get("sbwd"), dW_out=dW_out)
    (dqkvc, dgc, dbeta), got["intrab"] = _carried(
        gdn_intra_bwd, qkvc, gc, beta, Ts, du, dw, dattn, dq_s, dk_s, dgc_s, f"{tag}_intra_bwd", side=sides.get("intrab"), dW_out=dW_out)
    dproj, dA_log, ddt = gdn_gates_bwd(proj, A_log, dt_bias, dgc, dbeta, dproj, f"{tag}_gates_bwd")
    dproj, dconv_w = gdn_prep_bwd(proj, conv_w, dqkvc, dproj, f"{tag}_prep_bwd")
    (dW_in,), got["dwin"] = _carried(matmul_tn, h, dproj, GDN_TN, f"{tag}_dwin", side=sides.get("dwin"), dW_out=dW_out)
    (dx, dnw, dscale, dshift), got["ibwd"] = _carried(
        inproj_bwd_x, x, nw, scale, shift, W_in, dproj, dxn, GDN_TN, f"{tag}_inproj_bwd", side=sides.get("ibwd"),
        dW_out=dW_out, dW_in=dW_in, dconv_w=dconv_w)
    grads = dict(norm_w=dnw, W_in=dW_in, conv_w=dconv_w, A_log=dA_log[:, :16], dt_bias=ddt[:, :16], a_nw=da_nw, W_out=dW_out,
                 dmod=jnp.concatenate([dshift, dscale, dgate], axis=1))
    return dx, grads, got


def fox_layer_fwd(x, mod, nw, weights, tag, sides):
    W_in, f_bias, qn_w, kn_w, W_out = weights
    shift, scale, gate = mod
    got = {}
    (proj, h), got["inproj"] = _carried(inproj_fwd, x, nw, scale, shift, W_in, FOX_TN, f"{tag}_inproj", side=sides.get("inproj"))
    Q, K, V = fox_prep_fwd(proj, f_bias, qn_w, kn_w, f"{tag}_prep")
    (O,), got["attn"] = _carried(fox_attn_fwd, Q, K, V, f"{tag}_attn", side=sides.get("attn"))
    x_new, y, og = fox_out_fwd(O, proj, W_out, x, gate, f"{tag}_out")
    return x_new, (x, proj, h, Q, K, V, O, y, og), got


def fox_layer_bwd(dxn, saved, mod, nw, weights, tag, sides):
    W_in, f_bias, qn_w, kn_w, W_out = weights
    shift, scale, gate = mod
    x, proj, h, Q, K, V, O, y, og = saved
    got = {}
    dy, dgate, dO, dz = fox_out_bwd(dxn, y, gate, O, proj, W_out, f"{tag}_out_bwd")
    dW_out, = matmul_tn(og, dy, 512, f"{tag}_dwout")
    (dQ, dK, dV), got["abwd"] = _carried(fox_attn_bwd, Q, K, V, dO, O, f"{tag}_attn_bwd", side=sides.get("abwd"))
    dproj, dfb, dqw, dkw = fox_prep_bwd(proj, f_bias, qn_w, kn_w, dQ, dK, dV, dz, f"{tag}_prep_bwd")
    (dW_in,), got["dwin"] = _carried(matmul_tn, h, dproj, FOX_TN, f"{tag}_dwin", side=sides.get("dwin"))
    dx, dnw, dscale, dshift = inproj_bwd_x(x, nw, scale, shift, W_in, dproj, dxn, FOX_TN, f"{tag}_inproj_bwd")
    grads = dict(norm_w=dnw, W_in=dW_in, f_bias=dfb[:, :16], qn_w=dqw, kn_w=dkw, W_out=dW_out,
                 dmod=jnp.concatenate([dshift, dscale, dgate], axis=1))
    return dx, grads, got


class LocalPlan:
    def __init__(self, full):
        self.full = full

    def layer_weights(self, i):
        j, f = i // 2, self.full
        return (f["a_w_in"][j], f["a_w_out"][j], f["a_conv_w"][j]) if i % 2 == 0 else (f["b_w_in"][j], f["b_w_out"][j])

    def fwd_sides(self, i):
        return {}

    def fwd_got(self, i, got):
        pass

    def bwd_sides(self, i):
        return {}

    def bwd_got(self, i, grads, got):
        pass


def device_step(x, mod_all, norm_w, small, final_norm_w, target, plan):
    D = x.shape[1]
    mods = [(mod_all[i:i + 1, 0:D], mod_all[i:i + 1, D:2 * D], mod_all[i:i + 1, 2 * D:3 * D]) for i in range(4)]

    def weights(i):
        j = i // 2
        if i % 2 == 0:
            W_in, W_out, conv_w = plan.layer_weights(i)
            return (W_in, conv_w, _pad_lanes(small["a_A_log"][j]), _pad_lanes(small["a_dt_bias"][j]), small["a_norm_w"][j:j + 1], W_out)
        W_in, W_out = plan.layer_weights(i)
        return (W_in, _pad_lanes(small["b_f_bias"][j]), small["b_qn_w"][j:j + 1], small["b_kn_w"][j:j + 1], W_out)

    saved, wts = [], []
    for i in range(4):
        wts.append(weights(i))
        fwd = gdn_layer_fwd if i % 2 == 0 else fox_layer_fwd
        x, sv, got = fwd(x, mods[i], norm_w[i:i + 1], wts[i], f"L{i}", plan.fwd_sides(i))
        plan.fwd_got(i, got)
        saved.append(sv)
    loss, dx, dfw = final_loss(x, final_norm_w.reshape(1, D), target, "final_loss")
    lg = [None] * 4
    for i in reversed(range(4)):
        bwd = gdn_layer_bwd if i % 2 == 0 else fox_layer_bwd
        dx, lg[i], got = bwd(dx, saved[i], mods[i], norm_w[i:i + 1], wts[i], f"L{i}", plan.bwd_sides(i))
        plan.bwd_got(i, lg[i], got)
    g = dict(
        norm_w=jnp.concatenate([lg[i]["norm_w"] for i in range(4)], axis=0),
        dmod=jnp.concatenate([lg[i]["dmod"] for i in range(4)], axis=0),
        a_w_in=[lg[i]["W_in"] for i in (0, 2)],
        a_conv_w=jnp.stack([lg[i]["conv_w"] for i in (0, 2)]),
        a_A_log=jnp.concatenate([lg[i]["A_log"] for i in (0, 2)], axis=0),
        a_dt_bias=jnp.concatenate([lg[i]["dt_bias"] for i in (0, 2)], axis=0),
        a_norm_w=jnp.concatenate([lg[i]["a_nw"] for i in (0, 2)], axis=0),
        a_w_out=[lg[i]["W_out"] for i in (0, 2)],
        b_w_in=[lg[i]["W_in"] for i in (1, 3)],
        b_f_bias=jnp.concatenate([lg[i]["f_bias"] for i in (1, 3)], axis=0),
        b_qn_w=jnp.concatenate([lg[i]["qn_w"] for i in (1, 3)], axis=0),
        b_kn_w=jnp.concatenate([lg[i]["kn_w"] for i in (1, 3)], axis=0),
        b_w_out=[lg[i]["W_out"] for i in (1, 3)],
        final_norm_w=dfw.reshape(-1),
    )
    return loss[0, 0], dx, g


class MeshPlan:
    def __init__(self, shards, w0, conv_full):
        self.shards = shards
        self.w = {0: w0}
        self.conv = conv_full
        self.recv = {}
        self.pending = {}
        self.names = {}

    def layer_weights(self, i):
        return self.w[i]

    def _gather_side(self, layer):
        names = ("a_w_in", "a_w_out") if layer % 2 == 0 else ("b_w_in", "b_w_out")
        out = []
        for n in names:
            sh = self.shards[n][layer // 2]
            out.append(sh.reshape(-1, sh.shape[-1]))
        return ("gather", out)

    def fwd_sides(self, i):
        if i == 0:
            kind, (b_in, b_out) = self._gather_side(1)
            kind, (a_in, a_out) = self._gather_side(2)
            half = b_in.shape[0] // 2
            return {"inproj": (kind, [b_out, a_out]), "scan": (kind, [b_in[:half]]), "out": (kind, [b_in[half:]]),
                    "intra": (kind, [a_in])}
        if i == 1:
            return {"attn": self._gather_side(3)}
        return {}

    def fwd_got(self, i, got):
        if got.get("inproj") is not None:
            self._outs = dict(zip(("scan", "intra"), got["inproj"]))
        for key, layer in (("scan", 1), ("intra", 2), ("attn", 3)):
            if got.get(key) is None:
                continue
            g_in, g_out = got[key] if key == "attn" else (got[key][0], self._outs[key])
            if key == "scan":
                g_in = jnp.concatenate([g_in, got["out"][0]], axis=1)
            D = g_out.shape[-1]
            j = layer // 2
            if layer % 2 == 1:
                W_in = cols_from_blocks(g_in[:, None], FOX_COLS, FOX_IN_PAD, f"b_w_in_cols{j}")[0]
                self.w[layer] = (W_in, g_out.reshape(-1, D))
            else:
                W_in = cols_from_blocks(g_in[:, None], GDN_COLS, GDN_IN_PAD, f"a_w_in_cols{j}")[0]
                self.w[layer] = (W_in, g_out.reshape(-1, D), self.conv[j])

    @staticmethod
    def _out_blocks(dW_out):
        return dW_out.astype(bf16).reshape(NDEV, -1, dW_out.shape[-1])

    def _in_blocks(self, name, j, dW_in):
        cols = GDN_COLS if name == "a_w_in" else FOX_COLS
        return blocks_from_cols(dW_in, self.shards[name].shape[-1], cols, f"{name}_blocks{j}")

    def bwd_sides(self, i):
        self.names = {}
        sides = {}
        for (layer, key) in [k for k in self.pending if k[0] == i]:
            self.names[key], arrs = self.pending.pop((layer, key))
            sides[key] = ("scatter", arrs)
        if i == 0:
            def sbwd(dW_out):
                self.names["sbwd"] = [("a_w_out", 0)]
                return ("scatter", [self._out_blocks(dW_out)])

            def ibwd(dW_out, dW_in, dconv_w):
                conv = jnp.stack([dconv_w, self._dconv1])
                n = conv.shape[-1] // NDEV
                self.names["ibwd"] = [("a_w_in", 0), ("a_conv_w", None)]
                return ("scatter", [self._in_blocks("a_w_in", 0, dW_in),
                                    jnp.moveaxis(conv.reshape(2, 4, NDEV, n), 2, 0).reshape(NDEV, 8, n)])

            sides["sbwd"], sides["ibwd"] = sbwd, ibwd
        return sides

    def bwd_got(self, i, grads, got):
        for key, arrs in got.items():
            if arrs is not None:
                self.recv.update(zip(self.names[key], arrs))
        j = i // 2
        if i % 2 == 1:
            self.pending[(i - 1, "sbwd" if i == 3 else "intrab")] = (
                [("b_w_in", j), ("b_w_out", j)], [self._in_blocks("b_w_in", j, grads["W_in"]), self._out_blocks(grads["W_out"])])
        elif i == 2:
            self.pending[(1, "abwd")] = (
                [("a_w_in", 1), ("a_w_out", 1)], [self._in_blocks("a_w_in", 1, grads["W_in"]), self._out_blocks(grads["W_out"])])
            self._dconv1 = grads["conv_w"]


def kernel(x, c, norm_w, ada_w, ada_b, a_w_in, a_conv_w, a_A_log, a_dt_bias, a_norm_w, a_w_out, b_w_in, b_f_bias, b_qn_w, b_kn_w, b_w_out, final_norm_w, loss_target, m_norm_w, m_ada_w, m_ada_b, m_a_w_in, m_a_conv_w, m_a_A_log, m_a_dt_bias, m_a_norm_w, m_a_w_out, m_b_w_in, m_b_f_bias, m_b_qn_w, m_b_kn_w, m_b_w_out, m_final_norm_w, v_norm_w, v_ada_w, v_ada_b, v_a_w_in, v_a_conv_w, v_a_A_log, v_a_dt_bias, v_a_norm_w, v_a_w_out, v_b_w_in, v_b_f_bias, v_b_qn_w, v_b_kn_w, v_b_w_out, v_final_norm_w):
    W = dict(norm_w=norm_w, ada_w=ada_w, ada_b=ada_b, a_w_in=a_w_in, a_conv_w=a_conv_w, a_A_log=a_A_log, a_dt_bias=a_dt_bias,
             a_norm_w=a_norm_w, a_w_out=a_w_out, b_w_in=b_w_in, b_f_bias=b_f_bias, b_qn_w=b_qn_w, b_kn_w=b_kn_w, b_w_out=b_w_out,
             final_norm_w=final_norm_w)
    M = dict(norm_w=m_norm_w, ada_w=m_ada_w, ada_b=m_ada_b, a_w_in=m_a_w_in, a_conv_w=m_a_conv_w, a_A_log=m_a_A_log,
             a_dt_bias=m_a_dt_bias, a_norm_w=m_a_norm_w, a_w_out=m_a_w_out, b_w_in=m_b_w_in, b_f_bias=m_b_f_bias, b_qn_w=m_b_qn_w,
             b_kn_w=m_b_kn_w, b_w_out=m_b_w_out, final_norm_w=m_final_norm_w)
    V = dict(norm_w=v_norm_w, ada_w=v_ada_w, ada_b=v_ada_b, a_w_in=v_a_w_in, a_conv_w=v_a_conv_w, a_A_log=v_a_A_log,
             a_dt_bias=v_a_dt_bias, a_norm_w=v_a_norm_w, a_w_out=v_a_w_out, b_w_in=v_b_w_in, b_f_bias=v_b_f_bias, b_qn_w=v_b_qn_w,
             b_kn_w=v_b_kn_w, b_w_out=v_b_w_out, final_norm_w=v_final_norm_w)
    S, D = x.shape[1], x.shape[2]
    me = 4 * lax.axis_index("x") + 2 * lax.axis_index("y") + lax.axis_index("c")
    small_shapes = [W[n].shape for n in SMALL]

    shards = {n: W[n].astype(bf16) for n in ("a_w_in", "a_w_out", "b_w_in", "b_w_out")}
    gath = all_gather([shards["a_w_in"][0], shards["a_w_out"][0], a_conv_w.reshape(8, -1), c.reshape(8, D // 8)], "gather_w0")
    conv_full = _full_from_gathered(gath[2].reshape((NDEV,) + a_conv_w.shape), a_conv_w.shape, 2)
    w0 = (cols_from_blocks(gath[0][:, None], GDN_COLS, GDN_IN_PAD, "a_w_in_cols0")[0], gath[1].reshape(-1, D), conv_full[0])
    plan = MeshPlan(shards, w0, conv_full)
    c_all = gath[3].reshape(NDEV, D)

    mod_part = ada_fwd(c_all, ada_w, "ada_fwd")
    n_ada = ada_w.shape[2]
    mod_g = all_gather([mod_part.reshape(4 * NDEV, n_ada)], "gather_mod")[0].reshape(NDEV, 4, NDEV, n_ada)
    mod_mine = lax.dynamic_index_in_dim(mod_g, me, axis=2, keepdims=False)
    mod_all = jnp.moveaxis(mod_mine, 0, 1).reshape(4, NDEV * n_ada) + ada_b

    loss, dx, g = device_step(x[0], mod_all, norm_w, W, final_norm_w, loss_target[0], plan)
    loss = lax.psum(loss, MESH_AXES)

    g_small = dict(g, ada_b=g["dmod"])
    sp = _pack_small([g_small[n] for n in SMALL])
    sp_all = all_gather([sp], "gather_small")[0]
    sw, sm, sv = (_pack_small([T[n] for n in SMALL]) for T in (W, M, V))
    sg, sd, snm, snv = (_unpack(t, small_shapes, 128) for t in reduce_adam(sp_all, sw, sm, sv, sp.shape[0], "adam_small"))

    off_b = 0
    for n, shp in zip(SMALL, small_shapes):
        if n == "ada_b":
            break
        cnt = 1
        for d in shp:
            cnt *= d
        off_b += cnt + ((-cnt) % 128)
    dmod_all = sp_all.reshape(NDEV, -1)[:, off_b:off_b + 4 * 3 * D].reshape(NDEV, 4, 3 * D)
    dmod_cols = lax.dynamic_slice_in_dim(dmod_all, me * n_ada, n_ada, axis=2)
    g_ada = ada_grad(c_all, jnp.moveaxis(dmod_cols, 0, 1), "ada_grad")
    r_ada = reduce_adam(g_ada.reshape(1, 4 * D, n_ada), *(T["ada_w"].reshape(4 * D, n_ada) for T in (W, M, V)), 512, "adam_ada")
    ag, ad, anm, anv = (t.reshape(ada_w.shape) for t in r_ada)

    big = {}
    for n in BIG:
        C = W[n].shape[-1]
        parts = plan.recv[(n, None)] if n == "a_conv_w" else jnp.stack([plan.recv[(n, 0)], plan.recv[(n, 1)]], axis=1).reshape(NDEV, -1, C)
        res = reduce_adam(parts, *(T[n].reshape(parts.shape[1:]) for T in (W, M, V)), min(256, parts.shape[1]), f"adam_{n}")
        big[n] = [t.reshape(W[n].shape) for t in res]

    outs = {}
    for idx, (k, sm_l, ada_t) in enumerate((("grad", sg, ag), ("delta", sd, ad), ("new_m", snm, anm), ("new_v", snv, anv))):
        d = dict(zip(SMALL, sm_l))
        d.update({n: big[n][idx] for n in BIG})
        d["ada_w"] = ada_t
        outs[k] = d
    order = ("norm_w", "ada_w", "ada_b", "a_w_in", "a_conv_w", "a_A_log", "a_dt_bias", "a_norm_w", "a_w_out", "b_w_in", "b_f_bias",
             "b_qn_w", "b_kn_w", "b_w_out", "final_norm_w")
    return (loss, dx[None], *[outs["grad"][n] for n in order], *[outs["delta"][n] for n in order],
            *[outs["new_m"][n] for n in order], *[outs["new_v"][n] for n in order])
```

```python
import functools

import jax
import jax.numpy as jnp
from jax import lax
from jax.experimental import pallas as pl
from jax.experimental.pallas import tpu as pltpu

f32 = jnp.float32
bf16 = jnp.bfloat16
SDS = jax.ShapeDtypeStruct

EPS = 1e-6
CHUNK = 64
HD = 128
GDN_QK_HEADS = 8
GDN_V_HEADS = 16
GDN_QK_W = GDN_QK_HEADS * HD
GDN_V_W = GDN_V_HEADS * HD
GDN_CONV = 2 * GDN_QK_W + GDN_V_W
GDN_IN = GDN_CONV + GDN_V_W + 2 * GDN_V_HEADS
GDN_IN_PAD = GDN_CONV + GDN_V_W + 256
GDN_TN = 1280
FOX_H = 16
FOX_D = 64
FOX_W = FOX_H * FOX_D
FOX_IN = 4 * FOX_W + FOX_H
FOX_IN_PAD = 4 * FOX_W + 128
FOX_TN = 1408
FOX_PW = FOX_H * 128
NDEV = 8
MESH_AXES = ("x", "y", "c")
NEG = -1e30

ADAM_LR = 0.001
ADAM_B1 = 0.9
ADAM_B2 = 0.999
ADAM_EPS = 1e-08
ADAM_WD = 0.01
ADAM_STEP = 10

VMEM_LIMIT = 56 * 1024 * 1024


def _cp(sem=None):
    return pltpu.CompilerParams(dimension_semantics=sem, vmem_limit_bytes=VMEM_LIMIT)


def _bdot(a, b, dims):
    return lax.dot_general(a.astype(bf16), b.astype(bf16), (dims, ((), ())), preferred_element_type=f32)


def _nn(a, b):
    return _bdot(a, b, ((1,), (0,)))


def _nt(a, b):
    return _bdot(a, b, ((1,), (1,)))


def _tn(a, b):
    return _bdot(a, b, ((0,), (0,)))


def _hdot(a, b, dims=((1,), (0,))):
    return lax.dot_general(a, b, (dims, ((), ())), precision=lax.Precision.HIGHEST, preferred_element_type=f32)


def _split2(a):
    hi = a.astype(bf16)
    return hi, (a - hi.astype(f32)).astype(bf16)


def _dot3(a, b):
    (ah, al), (bh, bl) = a, b
    n = ah.shape[0]
    both = jnp.dot(jnp.concatenate([ah, al], axis=0), bh, preferred_element_type=f32)
    return both[:n] + both[n:] + jnp.dot(ah, bl, preferred_element_type=f32)


@jax.custom_vjp
def _mm(a, b):
    return _nn(a, b)


_mm.defvjp(lambda a, b: (_nn(a, b), (a, b)), lambda r, g: (_nt(g, r[1]), _tn(r[0], g)))


@jax.custom_vjp
def _mm_nt(a, b):
    return _nt(a, b)


_mm_nt.defvjp(lambda a, b: (_nt(a, b), (a, b)), lambda r, g: (_nn(g, r[1]), _tn(g, r[0])))


@jax.custom_vjp
def _mm_tn(a, b):
    return _tn(a, b)


_mm_tn.defvjp(lambda a, b: (_tn(a, b), (a, b)), lambda r, g: (_nt(r[1], g), _nn(r[0], g)))


def _silu(x):
    return x * jax.nn.sigmoid(x)


def _rms_mod(x, nw, scale, shift):
    r = lax.rsqrt(jnp.mean(x * x, axis=-1, keepdims=True) + EPS)
    return (x * r * nw) * (1.0 + scale) + shift


def _rows(S, want):
    return min(want, S)


def _my_pos():
    return lax.axis_index("x"), lax.axis_index("y"), lax.axis_index("c")


def _exchange_copies(kind, x_refs, out_refs, send_sems, recv_sems, local_sems):
    x_, y_, c_ = _my_pos()
    me = 4 * x_ + 2 * y_ + c_
    own = kind == "gather"
    cps = [pltpu.make_async_copy(x_refs[a] if own else x_refs[a].at[me], out_refs[a].at[me], local_sems.at[a])
           for a in range(len(x_refs))]
    for rel in range(1, NDEV):
        px = (x_ + ((rel >> 2) & 1)) % 2
        py = (y_ + ((rel >> 1) & 1)) % 2
        pc = (c_ + (rel & 1)) % 2
        for a in range(len(x_refs)):
            cps.append(pltpu.make_async_remote_copy(
                src_ref=x_refs[a] if own else x_refs[a].at[4 * px + 2 * py + pc], dst_ref=out_refs[a].at[me],
                send_sem=send_sems.at[rel - 1, a], recv_sem=recv_sems.at[rel - 1, a],
                device_id=(px, py, pc), device_id_type=pl.DeviceIdType.MESH))
    return cps


def _exchange_scratch(n):
    return [pltpu.SemaphoreType.DMA((NDEV - 1, n)), pltpu.SemaphoreType.DMA((NDEV - 1, n)), pltpu.SemaphoreType.DMA((n,))]


def _call(body, *, name, grid, in_specs, out_specs, out_shape, args, scratch=(), side=None):
    params = _cp(("arbitrary",) * len(grid))
    if side is None:
        return pl.pallas_call(body, name=name, grid=grid, in_specs=in_specs, out_specs=out_specs, out_shape=out_shape,
                              scratch_shapes=list(scratch), compiler_params=params)(*args)
    kind, xs = side
    n_in, n_out, n_scr, ns = len(in_specs), len(out_shape), len(scratch), len(xs)
    steps = 1
    for g in grid:
        steps *= g

    def wrapped(*refs):
        o0 = n_in + ns
        s0 = o0 + n_out + ns
        step = pl.program_id(0)
        for d in range(1, len(grid)):
            step = step * grid[d] + pl.program_id(d)

        def copies():
            return _exchange_copies(kind, refs[n_in:o0], refs[o0 + n_out:s0], *refs[s0 + n_scr:])

        @pl.when(step == 0)
        def _():
            for cp in copies():
                cp.start()

        body(*refs[:n_in], *refs[o0:o0 + n_out], *refs[s0:s0 + n_scr])

        @pl.when(step == steps - 1)
        def _():
            for cp in copies():
                cp.wait()

    any_ = pl.BlockSpec(memory_space=pl.ANY)
    side_shapes = [SDS((NDEV,) + x.shape if kind == "gather" else x.shape, x.dtype) for x in xs]
    outs = pl.pallas_call(wrapped, name=name, grid=grid, in_specs=list(in_specs) + [any_] * ns,
                          out_specs=list(out_specs) + [any_] * ns, out_shape=list(out_shape) + side_shapes,
                          scratch_shapes=list(scratch) + _exchange_scratch(ns), compiler_params=params)(*args, *xs)
    return outs[:n_out], outs[n_out:]


def inproj_fwd(x, nw, scale, shift, W, tn, name, side=None):
    S, D = x.shape
    N = W.shape[1]
    tm = _rows(S, 1024)

    def body(x_ref, nw_ref, sc_ref, sh_ref, w_ref, proj_ref, h_ref):
        @pl.when(pl.program_id(1) == 0)
        def _():
            h_ref[...] = _rms_mod(x_ref[...], nw_ref[...], sc_ref[...], sh_ref[...]).astype(bf16)

        proj_ref[...] = jnp.dot(h_ref[...], w_ref[...], preferred_element_type=f32)

    vec = pl.BlockSpec((1, D), lambda i, j: (0, 0))
    return _call(
        body, name=name, grid=(S // tm, N // tn),
        in_specs=[pl.BlockSpec((tm, D), lambda i, j: (i, 0)), vec, vec, vec, pl.BlockSpec((D, tn), lambda i, j: (0, j))],
        out_specs=[pl.BlockSpec((tm, tn), lambda i, j: (i, j)), pl.BlockSpec((tm, D), lambda i, j: (i, 0))],
        out_shape=[SDS((S, N), f32), SDS((S, D), bf16)], args=(x, nw, scale, shift, W), side=side)


def inproj_bwd_x(x, nw, scale, shift, W, dproj, dx_res, tn, name, side=None):
    S, D = x.shape
    N = W.shape[1]
    tm = _rows(S, 1024)
    nj = N // tn

    def body(x_ref, nw_ref, sc_ref, sh_ref, w_ref, dp_ref, dxr_ref, dx_ref, dnw_ref, dsc_ref, dsh_ref, acc):
        i, j = pl.program_id(0), pl.program_id(1)

        @pl.when(j == 0)
        def _():
            acc[...] = jnp.zeros_like(acc)

        @pl.when((i == 0) & (j == 0))
        def _():
            dnw_ref[...] = jnp.zeros_like(dnw_ref)
            dsc_ref[...] = jnp.zeros_like(dsc_ref)
            dsh_ref[...] = jnp.zeros_like(dsh_ref)

        acc[...] += _nt(dp_ref[...], w_ref[...])

        @pl.when(j == nj - 1)
        def _():
            _, vjp = jax.vjp(_rms_mod, x_ref[...], nw_ref[...], sc_ref[...], sh_ref[...])
            dx, dnw, dsc, dsh = vjp(acc[...])
            dx_ref[...] = dxr_ref[...] + dx
            dnw_ref[...] += dnw
            dsc_ref[...] += dsc
            dsh_ref[...] += dsh

    vec = pl.BlockSpec((1, D), lambda i, j: (0, 0))
    row = pl.BlockSpec((tm, D), lambda i, j: (i, 0))
    return _call(
        body, name=name, grid=(S // tm, nj),
        in_specs=[row, vec, vec, vec, pl.BlockSpec((D, tn), lambda i, j: (0, j)), pl.BlockSpec((tm, tn), lambda i, j: (i, j)), row],
        out_specs=[row, vec, vec, vec],
        out_shape=[SDS((S, D), f32), SDS((1, D), f32), SDS((1, D), f32), SDS((1, D), f32)],
        scratch=[pltpu.VMEM((tm, D), f32)], args=(x, nw, scale, shift, W, dproj, dx_res), side=side)


def matmul_tn(a, b, tn, name, side=None):
    S, K = a.shape
    N = b.shape[1]
    tm = _rows(S, 1024)
    ni = S // tm

    def body(a_ref, b_ref, o_ref):
        @pl.when(pl.program_id(1) == 0)
        def _():
            o_ref[...] = jnp.zeros_like(o_ref)

        o_ref[...] += _tn(a_ref[...], b_ref[...])

    return _call(
        body, name=name, grid=(N // tn, ni),
        in_specs=[pl.BlockSpec((tm, K), lambda j, i: (i, 0)), pl.BlockSpec((tm, tn), lambda j, i: (i, j))],
        out_specs=[pl.BlockSpec((K, tn), lambda j, i: (0, j))],
        out_shape=[SDS((K, N), f32)], args=(a, b), side=side)


def _conv_taps(xs, w, n_out):
    taps = []
    for j in range(4):
        s = 3 - j
        sh = xs if s == 0 else pltpu.roll(xs, s, axis=0)
        taps.append(sh[8:8 + n_out])
    conv = taps[0] * w[0] + taps[1] * w[1] + taps[2] * w[2] + taps[3] * w[3]
    return taps, conv


def _act_norm(conv, mul):
    s = _silu(conv)
    return s * (mul * lax.rsqrt(jnp.sum(s * s, axis=-1, keepdims=True) + EPS))


def gdn_prep_fwd(proj, conv_w, name, side=None):
    S = proj.shape[0]
    R = _rows(S, 512)

    def body(x_ref, w_ref, o_ref):
        j = pl.program_id(0)
        w = [w_ref[t:t + 1, :] for t in range(4)]

        def sweep(act):
            def piece(r, c):
                t0 = pl.multiple_of(r * R, R)
                cur = x_ref[pl.ds(t0, R), :]
                prev = x_ref[pl.ds(pl.multiple_of(jnp.maximum(t0 - 8, 0), 8), 8), :]
                prev = jnp.where(r == 0, 0.0, prev)
                _, conv = _conv_taps(jnp.concatenate([prev, cur], axis=0), w, R)
                o_ref[pl.ds(t0, R), :] = act(conv)
                return c

            lax.fori_loop(0, S // R, piece, 0)

        @pl.when(j < 2 * GDN_QK_HEADS)
        def _():
            sweep(lambda c: _act_norm(c, jnp.where(j < GDN_QK_HEADS, HD ** -0.5, 1.0).astype(f32)))

        @pl.when(j >= 2 * GDN_QK_HEADS)
        def _():
            sweep(_silu)

    return _call(
        body, name=name, grid=(GDN_CONV // 128,),
        in_specs=[pl.BlockSpec((S, 128), lambda j: (0, j)), pl.BlockSpec((4, 128), lambda j: (0, j))],
        out_specs=[pl.BlockSpec((S, 128), lambda j: (0, j))],
        out_shape=[SDS((S, GDN_CONV), f32)], args=(proj, conv_w), side=side)


def gdn_prep_bwd(proj, conv_w, dqkvc, dproj, name):
    S = proj.shape[0]
    R = _rows(S, 512)
    NP = S // R

    def body(x_ref, w_ref, dn_ref, _, dx_ref, dw_ref):
        jb = pl.program_id(0)
        w = [w_ref[j:j + 1, :] for j in range(4)]

        def piece(act, r, dw):
            t0 = pl.multiple_of(r * R, R)
            cur = x_ref[pl.ds(t0, R), :]
            prev = x_ref[pl.ds(pl.multiple_of(jnp.maximum(t0 - 8, 0), 8), 8), :]
            prev = jnp.where(r == 0, 0.0, prev)
            nxt0 = pl.multiple_of(jnp.minimum(t0 + R, S - 8), 8)
            nxt = x_ref[pl.ds(nxt0, 8), :]
            dn_cur = dn_ref[pl.ds(t0, R), :]
            dn_nxt = jnp.where(r == NP - 1, 0.0, dn_ref[pl.ds(nxt0, 8), :])
            xs = jnp.concatenate([prev, cur, nxt], axis=0)
            taps, conv = _conv_taps(xs, w, R + 8)
            dn = jnp.concatenate([dn_cur, dn_nxt], axis=0)
            _, vjp = jax.vjp(act, conv)
            dxc = vjp(dn)[0]
            n = R + 8
            dx = dxc[0:R] * w[3]
            for j in range(3):
                s = 3 - j
                dx = dx + pltpu.roll(dxc, n - s, axis=0)[0:R] * w[j]
            dx_ref[pl.ds(t0, R), :] = dx
            return tuple(dw[j] + jnp.sum(dxc[0:R] * taps[j][0:R], axis=0, keepdims=True) for j in range(4))

        def sweep(act):
            dw = lax.fori_loop(0, NP, functools.partial(piece, act), tuple(jnp.zeros((1, 128), f32) for _ in range(4)))
            for j in range(4):
                dw_ref[j:j + 1, :] = dw[j]

        @pl.when(jb < 2 * GDN_QK_HEADS)
        def _():
            sweep(lambda c: _act_norm(c, jnp.where(jb < GDN_QK_HEADS, HD ** -0.5, 1.0).astype(f32)))

        @pl.when(jb >= 2 * GDN_QK_HEADS)
        def _():
            sweep(_silu)

    col = pl.BlockSpec((S, 128), lambda j: (0, j))
    wsp = pl.BlockSpec((4, 128), lambda j: (0, j))
    return pl.pallas_call(
        body, name=name, grid=(GDN_CONV // 128,),
        in_specs=[col, wsp, col, pl.BlockSpec(memory_space=pl.ANY)], out_specs=[col, wsp],
        out_shape=[SDS(dproj.shape, f32), SDS((4, GDN_CONV), f32)],
        input_output_aliases={3: 0},
        compiler_params=_cp(("arbitrary",)),
    )(proj, conv_w, dqkvc, dproj)


def _chunk_tril(R):
    ii = lax.broadcasted_iota(jnp.int32, (R, R), 0)
    jj = lax.broadcasted_iota(jnp.int32, (R, R), 1)
    return ((ii // CHUNK == jj // CHUNK) & (ii >= jj)).astype(f32)


def _gdn_gates(b, a, A_log, dt_bias, tril):
    beta = jax.nn.sigmoid(b)
    g = -jnp.exp(A_log) * jax.nn.softplus(a + dt_bias)
    return _hdot(tril, g), beta


_GDN_B_BLK = (GDN_CONV + GDN_V_W) // 128
_GDN_A_BLK = _GDN_B_BLK + 1


def gdn_gates_fwd(proj, A_log, dt_bias, name):
    S = proj.shape[0]
    R = _rows(S, 512)

    def body(b_ref, a_ref, al_ref, dt_ref, gc_ref, be_ref):
        gc, be = _gdn_gates(b_ref[...], a_ref[...], al_ref[...], dt_ref[...], _chunk_tril(R))
        gc_ref[...] = gc
        be_ref[...] = be

    vec = pl.BlockSpec((1, 128), lambda i: (0, 0))
    blk = pl.BlockSpec((R, 128), lambda i: (i, 0))
    return pl.pallas_call(
        body, name=name, grid=(S // R,),
        in_specs=[pl.BlockSpec((R, 128), lambda i: (i, _GDN_B_BLK)), pl.BlockSpec((R, 128), lambda i: (i, _GDN_A_BLK)), vec, vec],
        out_specs=[blk, blk], out_shape=[SDS((S, 128), f32), SDS((S, 128), f32)],
        compiler_params=_cp(("arbitrary",)),
    )(proj, proj, A_log, dt_bias)


def gdn_gates_bwd(proj, A_log, dt_bias, dgc, dbeta, dproj, name):
    S = proj.shape[0]
    R = _rows(S, 512)

    def body(b_ref, a_ref, al_ref, dt_ref, dgc_ref, dbe_ref, _, dp_ref, dal_ref, ddt_ref):
        @pl.when(pl.program_id(0) == 0)
        def _():
            dal_ref[...] = jnp.zeros_like(dal_ref)
            ddt_ref[...] = jnp.zeros_like(ddt_ref)

        tril = _chunk_tril(R)
        _, vjp = jax.vjp(lambda b, a, al, dt: _gdn_gates(b, a, al, dt, tril), b_ref[...], a_ref[...], al_ref[...], dt_ref[...])
        db, da, dal, ddt = vjp((dgc_ref[...], dbe_ref[...]))
        dp_ref[:, 0:128] = db
        dp_ref[:, 128:256] = da
        dal_ref[...] += dal
        ddt_ref[...] += ddt

    vec = pl.BlockSpec((1, 128), lambda i: (0, 0))
    blk = pl.BlockSpec((R, 128), lambda i: (i, 0))
    return pl.pallas_call(
        body, name=name, grid=(S // R,),
        in_specs=[pl.BlockSpec((R, 128), lambda i: (i, _GDN_B_BLK)), pl.BlockSpec((R, 128), lambda i: (i, _GDN_A_BLK)), vec, vec, blk, blk,
                  pl.BlockSpec(memory_space=pl.ANY)],
        out_specs=[pl.BlockSpec((R, 256), lambda i: (i, _GDN_B_BLK // 2)), vec, vec],
        out_shape=[SDS(dproj.shape, f32), SDS((1, 128), f32), SDS((1, 128), f32)],
        input_output_aliases={6: 0},
        compiler_params=_cp(("arbitrary",)),
    )(proj, proj, A_log, dt_bias, dgc, dbeta, dproj)


@jax.custom_vjp
def _inv_given(L, T):
    return T


def _inv_given_bwd(T, ct):
    dL = -_nt(_tn(T, ct), T)
    return dL, jnp.zeros_like(T)


_inv_given.defvjp(lambda L, T: (T, T), _inv_given_bwd)


REP = GDN_V_HEADS // GDN_QK_HEADS


def _gdn_intra_all(qs, ks, vs, gcols, bcols, Ts=None):
    H = len(vs)
    C = vs[0].shape[0]
    ii = lax.broadcasted_iota(jnp.int32, (C, C), 0)
    jj = lax.broadcasted_iota(jnp.int32, (C, C), 1)
    grows = [jnp.sum(jnp.where(ii == jj, g, 0.0), axis=0, keepdims=True) for g in gcols]
    decs = [jnp.exp(jnp.where(ii >= jj, gcols[h] - grows[h], NEG)) for h in range(H)]
    kbs = [ks[h // REP] * bcols[h] for h in range(H)]
    As = [_mm_nt(kbs[h], ks[h // REP]) for h in range(H)]
    Ls = [jnp.where(ii > jj, As[h] * decs[h], 0.0) for h in range(H)]
    if Ts is None:
        T = _neumann_inv_batched(Ls)
    else:
        T = [_inv_given(Ls[h], Ts[h]) for h in range(H)]
    us = [_mm(T[h], vs[h] * bcols[h]) for h in range(H)]
    ws = [_mm(T[h], kbs[h] * jnp.exp(gcols[h])) for h in range(H)]
    qk = [_mm_nt(qs[p], ks[p]) for p in range(H // REP)]
    return us, ws, [qk[h // REP] * decs[h] for h in range(H)], T


def _neumann_inv_batched(Ls):
    n, C = 4, Ls[0].shape[0]
    r0 = lax.broadcasted_iota(jnp.int32, (n * C, n * C), 0)
    c0 = lax.broadcasted_iota(jnp.int32, (n * C, n * C), 1)
    same = (r0 // C) == (c0 // C)

    def blockdiag(split):
        return tuple(jnp.where(same, jnp.concatenate([x] * n, axis=0), jnp.zeros((), bf16)) for x in split)

    Ms = [jnp.concatenate(Ls[b:b + n], axis=1) for b in range(0, len(Ls), n)]
    eye = (lax.broadcasted_iota(jnp.int32, (C, n * C), 0) == (lax.broadcasted_iota(jnp.int32, (C, n * C), 1) & (C - 1))).astype(f32)
    Ps = [eye - M for M in Ms]
    Ss = [_split2(M) for M in Ms]
    Bs = [blockdiag(S) for S in Ss]
    k = 1
    while 2 * k < C:
        Ss = [_split2(_dot3(S, B)) for S, B in zip(Ss, Bs)]
        Bs = [blockdiag(S) for S in Ss]
        Ps = [P + _dot3(_split2(P), B) for P, B in zip(Ps, Bs)]
        k *= 2
    return [P[:, h * C:(h + 1) * C] for P in Ps for h in range(n)]


def _gdn_scan_all(qs, ks, gcols, us, ws, attns, S0s):
    H = len(us)
    C = us[0].shape[0]
    last = lax.broadcasted_iota(jnp.int32, (C, 1), 0) == C - 1
    glast = [jnp.sum(jnp.where(last, g, 0.0), axis=0, keepdims=True) for g in gcols]
    wS = [_mm(ws[h], S0s[h]) for h in range(H)]
    qS = [_mm(qs[h // REP] * jnp.exp(gcols[h]), S0s[h]) for h in range(H)]
    vn = [us[h] - wS[h] for h in range(H)]
    av = [_mm(attns[h], vn[h]) for h in range(H)]
    kv = [_mm_tn(ks[h // REP] * jnp.exp(glast[h] - gcols[h]), vn[h]) for h in range(H)]
    return [qS[h] + av[h] for h in range(H)], [S0s[h] * jnp.exp(glast[h]) + kv[h] for h in range(H)]


def _head_cols(blk):
    lane = lax.broadcasted_iota(jnp.int32, blk.shape, 1)
    return [jnp.sum(jnp.where(lane == h, blk, 0.0), axis=1, keepdims=True) for h in range(GDN_V_HEADS)]


def _head_lanes(cols):
    lane = lax.broadcasted_iota(jnp.int32, (cols[0].shape[0], 128), 1)
    out = jnp.zeros((cols[0].shape[0], 128), f32)
    for h, c in enumerate(cols):
        out = out + jnp.where(lane == h, c, 0.0)
    return out


CPS = 2


def _rows_of(c):
    return slice(c * CHUNK, (c + 1) * CHUNK)


def _heads(ref, n, c):
    return [ref[_rows_of(c), h * HD:(h + 1) * HD].astype(f32) for h in range(n)]


def _mats(ref, c):
    return [ref[c, h].astype(f32) for h in range(GDN_V_HEADS)]


def _gdn_specs(NB, rv=None):
    ix = (lambda n: n) if rv is None else rv
    R = CPS * CHUNK
    qs = pl.BlockSpec((R, GDN_QK_W), lambda n: (ix(n), 0))
    ks = pl.BlockSpec((R, GDN_QK_W), lambda n: (ix(n), 1))
    vs = pl.BlockSpec((R, GDN_V_W), lambda n: (ix(n), 1))
    g1 = pl.BlockSpec((R, 128), lambda n: (ix(n), 0))
    wide = pl.BlockSpec((R, GDN_V_W), lambda n: (ix(n), 0))
    sq = pl.BlockSpec((CPS, GDN_V_HEADS, CHUNK, CHUNK), lambda n: (ix(n), 0, 0, 0))
    st = pl.BlockSpec((CPS, GDN_V_HEADS, HD, HD), lambda n: (ix(n), 0, 0, 0))
    return qs, ks, vs, g1, wide, sq, st


def gdn_intra_fwd(qkvc, gc, beta, name, side=None):
    S = qkvc.shape[0]
    NC = S // CHUNK

    def body(q_ref, k_ref, v_ref, gc_ref, be_ref, u_ref, w_ref, at_ref, T_ref):
        for c in range(CPS):
            rows = _rows_of(c)
            us, ws, attns, Ts = _gdn_intra_all(_heads(q_ref, GDN_QK_HEADS, c), _heads(k_ref, GDN_QK_HEADS, c), _heads(v_ref, GDN_V_HEADS, c),
                                               _head_cols(gc_ref[rows, :]), _head_cols(be_ref[rows, :]))
            for h in range(GDN_V_HEADS):
                u_ref[rows, h * HD:(h + 1) * HD] = us[h]
                w_ref[rows, h * HD:(h + 1) * HD] = ws[h].astype(bf16)
                at_ref[c, h] = attns[h].astype(bf16)
                T_ref[c, h] = Ts[h].astype(bf16)

    qs, ks, vs, g1, wide, sq, _ = _gdn_specs(NC // CPS)
    return _call(
        body, name=name, grid=(NC // CPS,),
        in_specs=[qs, ks, vs, g1, g1], out_specs=[wide, wide, sq, sq],
        out_shape=[SDS((S, GDN_V_W), f32), SDS((S, GDN_V_W), bf16),
                   SDS((NC, GDN_V_HEADS, CHUNK, CHUNK), bf16), SDS((NC, GDN_V_HEADS, CHUNK, CHUNK), bf16)],
        args=(qkvc, qkvc, qkvc, gc, beta), side=side)


def gdn_scan_fwd(qkvc, gc, u, w, attn, name, side=None):
    S = qkvc.shape[0]
    NC = S // CHUNK

    def body(q_ref, k_ref, gc_ref, u_ref, w_ref, at_ref, o_ref, st_ref, state):
        @pl.when(pl.program_id(0) == 0)
        def _():
            state[...] = jnp.zeros_like(state)

        for c in range(CPS):
            rows = _rows_of(c)
            S0s = [state[h] for h in range(GDN_V_HEADS)]
            os_, S1s = _gdn_scan_all(_heads(q_ref, GDN_QK_HEADS, c), _heads(k_ref, GDN_QK_HEADS, c), _head_cols(gc_ref[rows, :]),
                                     _heads(u_ref, GDN_V_HEADS, c), _heads(w_ref, GDN_V_HEADS, c), _mats(at_ref, c), S0s)
            for h in range(GDN_V_HEADS):
                o_ref[rows, h * HD:(h + 1) * HD] = os_[h]
                st_ref[c, h] = S0s[h].astype(bf16)
                state[h] = S1s[h]

    qs, ks, _, g1, wide, sq, st = _gdn_specs(NC // CPS)
    return _call(
        body, name=name, grid=(NC // CPS,),
        in_specs=[qs, ks, g1, wide, wide, sq], out_specs=[wide, st],
        out_shape=[SDS((S, GDN_V_W), f32), SDS((NC, GDN_V_HEADS, HD, HD), bf16)],
        scratch=[pltpu.VMEM((GDN_V_HEADS, HD, HD), f32)], args=(qkvc, qkvc, gc, u, w, attn), side=side)


def gdn_scan_bwd(qkvc, gc, u, w, attn, states, do, name, side=None):
    S = qkvc.shape[0]
    NC = S // CHUNK
    NB = NC // CPS

    def body(q_ref, k_ref, gc_ref, u_ref, w_ref, at_ref, st_ref, do_ref,
             dq_ref, dk_ref, dgc_ref, du_ref, dw_ref, dat_ref, dstate):
        @pl.when(pl.program_id(0) == 0)
        def _():
            dstate[...] = jnp.zeros_like(dstate)

        VH = range(GDN_V_HEADS)
        for c in reversed(range(CPS)):
            rows = _rows_of(c)
            _, vjp = jax.vjp(_gdn_scan_all, _heads(q_ref, GDN_QK_HEADS, c), _heads(k_ref, GDN_QK_HEADS, c), _head_cols(gc_ref[rows, :]),
                             _heads(u_ref, GDN_V_HEADS, c), _heads(w_ref, GDN_V_HEADS, c), _mats(at_ref, c), _mats(st_ref, c))
            dqs, dks, dgs, dus, dws, dats, dS0s = vjp((_heads(do_ref, GDN_V_HEADS, c), [dstate[h] for h in VH]))
            for p in range(GDN_QK_HEADS):
                dq_ref[rows, p * HD:(p + 1) * HD] = dqs[p]
                dk_ref[rows, p * HD:(p + 1) * HD] = dks[p]
            for h in VH:
                du_ref[rows, h * HD:(h + 1) * HD] = dus[h].astype(bf16)
                dw_ref[rows, h * HD:(h + 1) * HD] = dws[h].astype(bf16)
                dat_ref[c, h] = dats[h].astype(bf16)
                dstate[h] = dS0s[h]
            dgc_ref[rows, :] = _head_lanes(dgs)

    qs, ks, _, g1, wide, sq, st = _gdn_specs(NB, lambda n: NB - 1 - n)
    dqs = pl.BlockSpec((CPS * CHUNK, GDN_QK_W), lambda n: (NB - 1 - n, 0))
    return _call(
        body, name=name, grid=(NB,),
        in_specs=[qs, ks, g1, wide, wide, sq, st, wide],
        out_specs=[dqs, dqs, g1, wide, wide, sq],
        out_shape=[SDS((S, GDN_QK_W), f32), SDS((S, GDN_QK_W), f32), SDS((S, 128), f32), SDS((S, GDN_V_W), bf16),
                   SDS((S, GDN_V_W), bf16), SDS((NC, GDN_V_HEADS, CHUNK, CHUNK), bf16)],
        scratch=[pltpu.VMEM((GDN_V_HEADS, HD, HD), f32)], args=(qkvc, qkvc, gc, u, w, attn, states, do), side=side)


def gdn_intra_bwd(qkvc, gc, beta, Ts, du, dw, dattn, dq_s, dk_s, dgc_s, name, side=None):
    S = qkvc.shape[0]
    NC = S // CHUNK

    def body(q_ref, k_ref, v_ref, gc_ref, be_ref, T_ref, du_ref, dw_ref, dat_ref, dqs_ref, dks_ref, dgs_ref,
             dqkv_ref, dgc_ref, dbe_ref):
        VH = range(GDN_V_HEADS)
        for c in range(CPS):
            rows = _rows_of(c)
            Ts = _mats(T_ref, c)
            _, vjp = jax.vjp(lambda q_, k_, v_, g_, b_: _gdn_intra_all(q_, k_, v_, g_, b_, Ts)[:3],
                             _heads(q_ref, GDN_QK_HEADS, c), _heads(k_ref, GDN_QK_HEADS, c), _heads(v_ref, GDN_V_HEADS, c),
                             _head_cols(gc_ref[rows, :]), _head_cols(be_ref[rows, :]))
            dqs, dks, dvs, dgs, dbs = vjp((_heads(du_ref, GDN_V_HEADS, c), _heads(dw_ref, GDN_V_HEADS, c), _mats(dat_ref, c)))
            for p in range(GDN_QK_HEADS):
                dqkv_ref[rows, p * HD:(p + 1) * HD] = dqs[p] + dqs_ref[rows, p * HD:(p + 1) * HD]
                dqkv_ref[rows, GDN_QK_W + p * HD:GDN_QK_W + (p + 1) * HD] = dks[p] + dks_ref[rows, p * HD:(p + 1) * HD]
            for h in VH:
                dqkv_ref[rows, 2 * GDN_QK_W + h * HD:2 * GDN_QK_W + (h + 1) * HD] = dvs[h]
            dgc_ref[rows, :] = _head_lanes(dgs) + dgs_ref[rows, :]
            dbe_ref[rows, :] = _head_lanes(dbs)

    qs, ks, vs, g1, wide, sq, _ = _gdn_specs(NC // CPS)
    dqs = pl.BlockSpec((CPS * CHUNK, GDN_QK_W), lambda n: (n, 0))
    return _call(
        body, name=name, grid=(NC // CPS,),
        in_specs=[qs, ks, vs, g1, g1, sq, wide, wide, sq, dqs, dqs, g1],
        out_specs=[pl.BlockSpec((CPS * CHUNK, GDN_CONV), lambda n: (n, 0)), g1, g1],
        out_shape=[SDS((S, GDN_CONV), f32), SDS((S, 128), f32), SDS((S, 128), f32)],
        args=(qkvc, qkvc, qkvc, gc, beta, Ts, du, dw, dattn, dq_s, dk_s, dgc_s), side=side)


def _gated_norm(o, z, nw):
    parts = []
    for h in range(GDN_V_HEADS):
        oh = o[:, h * HD:(h + 1) * HD]
        r = lax.rsqrt(jnp.mean(oh * oh, axis=-1, keepdims=True) + EPS)
        parts.append((oh * r * nw) * _silu(z[:, h * HD:(h + 1) * HD]))
    return jnp.concatenate(parts, axis=1)


def gdn_out_fwd(o, proj, nw, W, x, gate, name, side=None):
    S, D = x.shape
    tm = _rows(S, 256)

    def body(o_ref, z_ref, nw_ref, w_ref, x_ref, g_ref, xn_ref, y_ref, og_ref):
        og = _gated_norm(o_ref[...], z_ref[...], nw_ref[...]).astype(bf16)
        y = jnp.dot(og, w_ref[...], preferred_element_type=f32)
        og_ref[...] = og
        y_ref[...] = y
        xn_ref[...] = x_ref[...] + g_ref[...] * y

    row = pl.BlockSpec((tm, D), lambda i: (i, 0))
    wide = pl.BlockSpec((tm, GDN_V_W), lambda i: (i, 0))
    return _call(
        body, name=name, grid=(S // tm,),
        in_specs=[wide, pl.BlockSpec((tm, GDN_V_W), lambda i: (i, 2)), pl.BlockSpec((1, HD), lambda i: (0, 0)),
                  pl.BlockSpec((GDN_V_W, D), lambda i: (0, 0)), row, pl.BlockSpec((1, D), lambda i: (0, 0))],
        out_specs=[row, row, wide],
        out_shape=[SDS((S, D), f32), SDS((S, D), f32), SDS((S, GDN_V_W), bf16)],
        args=(o, proj, nw, W, x, gate), side=side)


def gdn_out_bwd(dxn, y, gate, o, proj, nw, W, name):
    S, D = dxn.shape
    tm = _rows(S, 256)

    def body(dx_ref, y_ref, g_ref, o_ref, z_ref, nw_ref, w_ref, dy_ref, dg_ref, do_ref, dz_ref, dnw_ref):
        @pl.when(pl.program_id(0) == 0)
        def _():
            dg_ref[...] = jnp.zeros_like(dg_ref)
            dnw_ref[...] = jnp.zeros_like(dnw_ref)

        dx = dx_ref[...]
        dy = dx * g_ref[...]
        dy_ref[...] = dy
        dg_ref[...] += jnp.sum(dx * y_ref[...], axis=0, keepdims=True)
        dog = _nt(dy, w_ref[...])
        _, vjp = jax.vjp(_gated_norm, o_ref[...], z_ref[...], nw_ref[...])
        do, dz, dnw = vjp(dog)
        do_ref[...] = do
        dz_ref[...] = dz
        dnw_ref[...] += dnw

    row = pl.BlockSpec((tm, D), lambda i: (i, 0))
    wide = pl.BlockSpec((tm, GDN_V_W), lambda i: (i, 0))
    vecd = pl.BlockSpec((1, D), lambda i: (0, 0))
    vech = pl.BlockSpec((1, HD), lambda i: (0, 0))
    return pl.pallas_call(
        body, name=name, grid=(S // tm,),
        in_specs=[row, row, vecd, wide, pl.BlockSpec((tm, GDN_V_W), lambda i: (i, 2)), vech, pl.BlockSpec((GDN_V_W, D), lambda i: (0, 0))],
        out_specs=[row, vecd, wide, pl.BlockSpec((tm, GDN_V_W), lambda i: (i, 2)), vech],
        out_shape=[SDS((S, D), f32), SDS((1, D), f32), SDS((S, GDN_V_W), f32), SDS((S, GDN_IN_PAD), f32), SDS((1, HD), f32)],
        compiler_params=_cp(("arbitrary",)),
    )(dxn, y, gate, o, proj, nw, W)


def _rms_w(x, w):
    return (x * lax.rsqrt(jnp.mean(x * x, axis=-1, keepdims=True) + EPS)) * w


def _split3(c):
    hi = c.astype(bf16).astype(f32)
    r1 = c - hi
    mid = r1.astype(bf16).astype(f32)
    lo = (r1 - mid).astype(bf16).astype(f32)
    return hi, mid, lo


_FOX_F_BLK = 4 * FOX_W // 128


def fox_prep_fwd(proj, f_bias, qn_w, kn_w, name):
    S = proj.shape[0]
    tm = _rows(S, 256)

    def body(q_ref, k_ref, v_ref, f_ref, fb_ref, qw_ref, kw_ref, Q_ref, K_ref, V_ref, carry):
        @pl.when(pl.program_id(0) == 0)
        def _():
            carry[...] = jnp.zeros_like(carry)

        ii = lax.broadcasted_iota(jnp.int32, (tm, tm), 0)
        jj = lax.broadcasted_iota(jnp.int32, (tm, tm), 1)
        lf = jax.nn.log_sigmoid(f_ref[...] + fb_ref[...])
        cum = _hdot((ii >= jj).astype(f32), lf) + carry[...]
        carry[...] = cum[tm - 1:tm, :]
        lane = lax.broadcasted_iota(jnp.int32, (tm, 128), 1)
        lo = lane < FOX_D
        qw2 = jnp.concatenate([qw_ref[...], qw_ref[...]], axis=1) * FOX_D ** -0.5
        kw2 = jnp.concatenate([kw_ref[...], kw_ref[...]], axis=1)

        def norm_pair(x, w2):
            x2 = x * x
            s_all = jnp.sum(x2, axis=1, keepdims=True)
            s_lo = jnp.sum(jnp.where(lo, x2, 0.0), axis=1, keepdims=True)
            r = jnp.where(lo, lax.rsqrt(s_lo * (1.0 / FOX_D) + EPS), lax.rsqrt((s_all - s_lo) * (1.0 / FOX_D) + EPS))
            return x * r * w2

        for p in range(FOX_H // 2):
            ps = slice(p * 128, (p + 1) * 128)
            yq, yk, xv = norm_pair(q_ref[:, ps], qw2), norm_pair(k_ref[:, ps], kw2), v_ref[:, ps]
            for e in range(2):
                h = 2 * p + e
                hi, mid, lw = _split3(cum[:, h:h + 1])
                eq = jnp.where(lane == FOX_D, hi, jnp.where(lane == FOX_D + 1, mid, jnp.where(lane == FOX_D + 2, lw, jnp.where(lane < FOX_D + 6, 1.0, 0.0))))
                ek = jnp.where(lane < FOX_D + 3, 1.0, jnp.where(lane == FOX_D + 3, -hi, jnp.where(lane == FOX_D + 4, -mid, jnp.where(lane == FOX_D + 5, -lw, 0.0))))
                ev = jnp.where(lane == FOX_D, 1.0, 0.0)
                mv = (lambda a: a) if e == 0 else (lambda a: pltpu.roll(a, FOX_D, axis=1))
                Q_ref[:, h * 128:(h + 1) * 128] = jnp.where(lo, mv(yq), eq).astype(bf16)
                K_ref[:, h * 128:(h + 1) * 128] = jnp.where(lo, mv(yk), ek).astype(bf16)
                V_ref[:, h * 128:(h + 1) * 128] = jnp.where(lo, mv(xv), ev).astype(bf16)

    def colblk(c):
        return pl.BlockSpec((tm, FOX_W), lambda i: (i, c))

    pad = pl.BlockSpec((tm, FOX_PW), lambda i: (i, 0))
    return pl.pallas_call(
        body, name=name, grid=(S // tm,),
        in_specs=[colblk(0), colblk(1), colblk(2), pl.BlockSpec((tm, 128), lambda i: (i, _FOX_F_BLK)),
                  pl.BlockSpec((1, 128), lambda i: (0, 0)), pl.BlockSpec((1, FOX_D), lambda i: (0, 0)), pl.BlockSpec((1, FOX_D), lambda i: (0, 0))],
        out_specs=[pad, pad, pad],
        out_shape=[SDS((S, FOX_PW), bf16)] * 3,
        scratch_shapes=[pltpu.VMEM((1, 128), f32)],
        compiler_params=_cp(("arbitrary",)),
    )(proj, proj, proj, proj, f_bias, qn_w, kn_w)


def fox_prep_bwd(proj, f_bias, qn_w, kn_w, dQ, dK, dV, dz, name):
    S = proj.shape[0]
    tm = _rows(S, 256)
    NB = S // tm

    def body(q_ref, k_ref, f_ref, fb_ref, qw_ref, kw_ref, dQ_ref, dK_ref, dV_ref, dz_ref,
             dp_ref, dfb_ref, dqw_ref, dkw_ref, carry):
        @pl.when(pl.program_id(0) == 0)
        def _():
            carry[...] = jnp.zeros_like(carry)
            dfb_ref[...] = jnp.zeros_like(dfb_ref)
            dqw_ref[...] = jnp.zeros_like(dqw_ref)
            dkw_ref[...] = jnp.zeros_like(dkw_ref)

        lane = lax.broadcasted_iota(jnp.int32, (tm, 128), 1)
        lo = lane < FOX_D
        qw2 = jnp.concatenate([qw_ref[...], qw_ref[...]], axis=1) * FOX_D ** -0.5
        kw2 = jnp.concatenate([kw_ref[...], kw_ref[...]], axis=1)

        def pair(ref, p):
            return jnp.where(lo, ref[:, 2 * p * 128:(2 * p + 1) * 128], pltpu.roll(ref[:, (2 * p + 1) * 128:(2 * p + 2) * 128], FOX_D, axis=1))

        def norm_pair_bwd(x, w2, dy):
            x2 = x * x
            s_all = jnp.sum(x2, axis=1, keepdims=True)
            s_lo = jnp.sum(jnp.where(lo, x2, 0.0), axis=1, keepdims=True)
            r = jnp.where(lo, lax.rsqrt(s_lo * (1.0 / FOX_D) + EPS), lax.rsqrt((s_all - s_lo) * (1.0 / FOX_D) + EPS))
            t = dy * w2 * x
            t_all = jnp.sum(t, axis=1, keepdims=True)
            t_lo = jnp.sum(jnp.where(lo, t, 0.0), axis=1, keepdims=True)
            dx = r * (w2 * dy - x * (r * r) * (jnp.where(lo, t_lo, t_all - t_lo) * (1.0 / FOX_D)))
            return dx, jnp.sum(dy * x * r, axis=0, keepdims=True)

        dcum = jnp.zeros((tm, 128), f32)
        dqw2 = jnp.zeros((1, 128), f32)
        dkw2 = jnp.zeros((1, 128), f32)
        for p in range(FOX_H // 2):
            ps = slice(p * 128, (p + 1) * 128)
            dxq, dw1 = norm_pair_bwd(q_ref[:, ps], qw2, pair(dQ_ref, p))
            dxk, dw2 = norm_pair_bwd(k_ref[:, ps], kw2, pair(dK_ref, p))
            dp_ref[:, p * 128:(p + 1) * 128] = dxq
            dp_ref[:, FOX_W + p * 128:FOX_W + (p + 1) * 128] = dxk
            dp_ref[:, 2 * FOX_W + p * 128:2 * FOX_W + (p + 1) * 128] = pair(dV_ref, p)
            dqw2 = dqw2 + dw1
            dkw2 = dkw2 + dw2
            for e in range(2):
                h = 2 * p + e
                dcum = dcum + jnp.where(lane == h, dQ_ref[:, h * 128 + FOX_D:h * 128 + FOX_D + 1]
                                        - dK_ref[:, h * 128 + FOX_D + 3:h * 128 + FOX_D + 4], 0.0)
        dp_ref[:, 3 * FOX_W:4 * FOX_W] = dz_ref[...]
        ii = lax.broadcasted_iota(jnp.int32, (tm, tm), 0)
        jj = lax.broadcasted_iota(jnp.int32, (tm, tm), 1)
        dlf = _hdot((ii <= jj).astype(f32), dcum) + carry[...]
        carry[...] += jnp.sum(dcum, axis=0, keepdims=True)
        df = dlf * jax.nn.sigmoid(-(f_ref[...] + fb_ref[...]))
        dp_ref[:, 4 * FOX_W:FOX_IN_PAD] = df
        dfb_ref[...] += jnp.sum(df, axis=0, keepdims=True)
        dqw_ref[...] += (dqw2[:, :FOX_D] + dqw2[:, FOX_D:]) * FOX_D ** -0.5
        dkw_ref[...] += dkw2[:, :FOX_D] + dkw2[:, FOX_D:]

    rv = lambda i: NB - 1 - i

    def colblk(c):
        return pl.BlockSpec((tm, FOX_W), lambda i: (rv(i), c))

    pad = pl.BlockSpec((tm, FOX_PW), lambda i: (rv(i), 0))
    cmp_ = pl.BlockSpec((tm, FOX_W), lambda i: (rv(i), 0))
    v128 = pl.BlockSpec((1, 128), lambda i: (0, 0))
    v64 = pl.BlockSpec((1, FOX_D), lambda i: (0, 0))
    return pl.pallas_call(
        body, name=name, grid=(NB,),
        in_specs=[colblk(0), colblk(1), pl.BlockSpec((tm, 128), lambda i: (rv(i), _FOX_F_BLK)), v128, v64, v64, pad, pad, pad, cmp_],
        out_specs=[pl.BlockSpec((tm, FOX_IN_PAD), lambda i: (rv(i), 0)), v128, v64, v64],
        out_shape=[SDS((S, FOX_IN_PAD), f32), SDS((1, 128), f32), SDS((1, FOX_D), f32), SDS((1, FOX_D), f32)],
        scratch_shapes=[pltpu.VMEM((1, 128), f32)],
        compiler_params=_cp(("arbitrary",)),
    )(proj, proj, proj, f_bias, qn_w, kn_w, dQ, dK, dV, dz)


FOX_HB = 2


def _diag_mask(t):
    return lax.broadcasted_iota(jnp.int32, (t, t), 1) <= lax.broadcasted_iota(jnp.int32, (t, t), 0)


def fox_attn_fwd(Q, K, V, name, side=None):
    S = Q.shape[0]
    t = _rows(S, 512)

    HB = FOX_HB
    HS = [slice(h * 128, (h + 1) * 128) for h in range(HB)]

    def body(q_ref, k_ref, v_ref, o_ref, m_sc, acc_sc, s_sc):
        i = pl.program_id(1)
        qs = [q_ref[:, sl] for sl in HS]
        m_sc[...] = jnp.full_like(m_sc, NEG)
        acc_sc[...] = jnp.zeros_like(acc_sc)

        def scores(j):
            j0 = pl.multiple_of(j * t, t)
            return [_nt(qs[h], k_ref[pl.ds(j0, t), HS[h]]) for h in range(HB)]

        def tile(j, diag):
            j0 = pl.multiple_of(j * t, t)
            ss = [s_sc[h] for h in range(HB)]
            if diag:
                ss = [jnp.where(_diag_mask(t), s, NEG) for s in ss]
            else:
                nxt = scores(j + 1)
            ms = [m_sc[h] for h in range(HB)]
            m_new = [jnp.maximum(ms[h], jnp.max(ss[h], axis=1, keepdims=True)) for h in range(HB)]
            ps = [jnp.exp(ss[h] - m_new[h]) for h in range(HB)]
            pv = [_nn(ps[h], v_ref[pl.ds(j0, t), HS[h]]) for h in range(HB)]
            for h in range(HB):
                acc_sc[h] = acc_sc[h] * jnp.exp(ms[h] - m_new[h]) + pv[h]
                m_sc[h] = m_new[h]
                if not diag:
                    s_sc[h] = nxt[h]

        def off_diag(j, c):
            tile(j, False)
            return c

        first = scores(0)
        for h in range(HB):
            s_sc[h] = first[h]
        lax.fori_loop(0, i, off_diag, 0)
        tile(i, True)
        lane = lax.broadcasted_iota(jnp.int32, (t, 128), 1)
        for h in range(HB):
            acc = acc_sc[h]
            l = acc[:, FOX_D:FOX_D + 1]
            o_ref[:, HS[h]] = jnp.where(lane == FOX_D, m_sc[h] + jnp.log(l), acc / l)

    blk = pl.BlockSpec((t, HB * 128), lambda h, i: (i, h))
    seq = pl.BlockSpec((S, HB * 128), lambda h, i: (0, h))
    return _call(
        body, name=name, grid=(FOX_H // HB, S // t),
        in_specs=[blk, seq, seq], out_specs=[blk], out_shape=[SDS((S, FOX_PW), f32)],
        scratch=[pltpu.VMEM((HB, t, 1), f32), pltpu.VMEM((HB, t, 128), f32), pltpu.VMEM((HB, t, t), f32)],
        args=(Q, K, V), side=side)


def fox_attn_bwd(Q, K, V, dO, O, name, side=None):
    S = Q.shape[0]
    t = _rows(S, 512)
    nq = S // t

    HB = FOX_HB
    HS = [slice(h * 128, (h + 1) * 128) for h in range(HB)]

    def body(k_ref, v_ref, q_ref, do_ref, o_ref, dq_ref, dk_ref, dv_ref):
        j = pl.program_id(1)

        @pl.when(j == 0)
        def _():
            dq_ref[...] = jnp.zeros_like(dq_ref)

        dk_ref[...] = jnp.zeros_like(dk_ref)
        dv_ref[...] = jnp.zeros_like(dv_ref)
        ks = [k_ref[:, sl] for sl in HS]
        vs = [v_ref[:, sl] for sl in HS]

        def tile(i, diag):
            i0 = pl.multiple_of(i * t, t)
            R = range(HB)
            qs = [q_ref[pl.ds(i0, t), HS[h]] for h in R]
            dos = [do_ref[pl.ds(i0, t), HS[h]] for h in R]
            ss = [_nt(qs[h], ks[h]) - o_ref[pl.ds(i0, t), h * 128 + FOX_D:h * 128 + FOX_D + 1] for h in R]
            if diag:
                ss = [jnp.where(_diag_mask(t), s, NEG) for s in ss]
            ps = [jnp.exp(s) for s in ss]
            dps = [_nt(dos[h], vs[h]) for h in R]
            dvs = [_tn(ps[h], dos[h]) for h in R]
            dss = [(ps[h] * dps[h]).astype(bf16) for h in R]
            dks = [_tn(dss[h], qs[h]) for h in R]
            dqs = [_nn(dss[h], ks[h]) for h in R]
            for h in R:
                dv_ref[:, HS[h]] += dvs[h]
                dk_ref[:, HS[h]] += dks[h]
                dq_ref[pl.ds(i0, t), HS[h]] += dqs[h]

        tile(j, True)

        def off_diag(i, c):
            tile(i, False)
            return c

        lax.fori_loop(j + 1, nq, off_diag, 0)

    blk = pl.BlockSpec((t, HB * 128), lambda h, j: (j, h))
    seq = pl.BlockSpec((S, HB * 128), lambda h, j: (0, h))
    return _call(
        body, name=name, grid=(FOX_H // HB, nq),
        in_specs=[blk, blk, seq, seq, seq], out_specs=[seq, blk, blk],
        out_shape=[SDS((S, FOX_PW), f32)] * 3, args=(K, V, Q, dO, O), side=side)


def fox_out_fwd(O, proj, W, x, gate, name):
    S, D = x.shape
    tm = _rows(S, 256)

    def body(o_ref, z_ref, w_ref, x_ref, g_ref, xn_ref, y_ref, og_ref):
        z = z_ref[...]
        og = jnp.concatenate([o_ref[:, h * 128:h * 128 + FOX_D] * _silu(z[:, h * FOX_D:(h + 1) * FOX_D]) for h in range(FOX_H)],
                             axis=1).astype(bf16)
        y = jnp.dot(og, w_ref[...], preferred_element_type=f32)
        og_ref[...] = og
        y_ref[...] = y
        xn_ref[...] = x_ref[...] + g_ref[...] * y

    row = pl.BlockSpec((tm, D), lambda i: (i, 0))
    cmp_ = pl.BlockSpec((tm, FOX_W), lambda i: (i, 0))
    return pl.pallas_call(
        body, name=name, grid=(S // tm,),
        in_specs=[pl.BlockSpec((tm, FOX_PW), lambda i: (i, 0)), pl.BlockSpec((tm, FOX_W), lambda i: (i, 3)),
                  pl.BlockSpec((FOX_W, D), lambda i: (0, 0)), row, pl.BlockSpec((1, D), lambda i: (0, 0))],
        out_specs=[row, row, cmp_],
        out_shape=[SDS((S, D), f32), SDS((S, D), f32), SDS((S, FOX_W), bf16)],
        compiler_params=_cp(("arbitrary",)),
    )(O, proj, W, x, gate)


def fox_out_bwd(dxn, y, gate, O, proj, W, name):
    S, D = dxn.shape
    tm = _rows(S, 256)

    def body(dx_ref, y_ref, g_ref, o_ref, z_ref, w_ref, dy_ref, dg_ref, dO_ref, dz_ref):
        @pl.when(pl.program_id(0) == 0)
        def _():
            dg_ref[...] = jnp.zeros_like(dg_ref)

        dx = dx_ref[...]
        dy = dx * g_ref[...]
        dy_ref[...] = dy
        dg_ref[...] += jnp.sum(dx * y_ref[...], axis=0, keepdims=True)
        dog = _nt(dy, w_ref[...])
        z = z_ref[...]
        lane = lax.broadcasted_iota(jnp.int32, (tm, FOX_D), 1)
        dzs = []
        for h in range(FOX_H):
            sl = slice(h * FOX_D, (h + 1) * FOX_D)
            zh = z[:, sl]
            sg = jax.nn.sigmoid(zh)
            oh = o_ref[:, h * 128:h * 128 + FOX_D]
            doh = dog[:, sl] * (zh * sg)
            delta = jnp.sum(doh * oh, axis=1, keepdims=True)
            dO_ref[:, h * 128:(h + 1) * 128] = jnp.concatenate([doh, jnp.where(lane == 0, -delta, 0.0)], axis=1).astype(bf16)
            dzs.append(dog[:, sl] * oh * (sg * (1.0 + zh * (1.0 - sg))))
        dz_ref[...] = jnp.concatenate(dzs, axis=1)

    row = pl.BlockSpec((tm, D), lambda i: (i, 0))
    vecd = pl.BlockSpec((1, D), lambda i: (0, 0))
    pad = pl.BlockSpec((tm, FOX_PW), lambda i: (i, 0))
    return pl.pallas_call(
        body, name=name, grid=(S // tm,),
        in_specs=[row, row, vecd, pad, pl.BlockSpec((tm, FOX_W), lambda i: (i, 3)), pl.BlockSpec((FOX_W, D), lambda i: (0, 0))],
        out_specs=[row, vecd, pad, pl.BlockSpec((tm, FOX_W), lambda i: (i, 0))],
        out_shape=[SDS((S, D), f32), SDS((1, D), f32), SDS((S, FOX_PW), bf16), SDS((S, FOX_W), f32)],
        compiler_params=_cp(("arbitrary",)),
    )(dxn, y, gate, O, proj, W)


def final_loss(x, fw, target, name):
    S, D = x.shape
    tm = _rows(S, 512)

    def body(x_ref, w_ref, t_ref, l_ref, dx_ref, dw_ref):
        @pl.when(pl.program_id(0) == 0)
        def _():
            l_ref[...] = jnp.zeros_like(l_ref)
            dw_ref[...] = jnp.zeros_like(dw_ref)

        out, vjp = jax.vjp(_rms_w, x_ref[...], w_ref[...])
        err = out - t_ref[...]
        l_ref[...] += 0.5 * jnp.sum(jnp.sum(err * err, axis=1, keepdims=True) * (1.0 / D), axis=0, keepdims=True)
        dx, dw = vjp(err * (1.0 / D))
        dx_ref[...] = dx
        dw_ref[...] += dw

    row = pl.BlockSpec((tm, D), lambda i: (i, 0))
    vec = pl.BlockSpec((1, D), lambda i: (0, 0))
    return pl.pallas_call(
        body, name=name, grid=(S // tm,),
        in_specs=[row, vec, row], out_specs=[pl.BlockSpec((1, 128), lambda i: (0, 0)), row, vec],
        out_shape=[SDS((1, 128), f32), SDS((S, D), f32), SDS((1, D), f32)],
        compiler_params=_cp(("arbitrary",)),
    )(x, fw, target)


def ada_fwd(c_all, ada_w, name):
    L, D, n = ada_w.shape

    def body(c_ref, w_ref, o_ref):
        cond = jnp.concatenate([_silu(c_ref[...]), jnp.zeros((8, D), f32)], axis=0)
        o_ref[0] = _nn(cond, w_ref[0])[0:8]

    return pl.pallas_call(
        body, name=name, grid=(L,),
        in_specs=[pl.BlockSpec((NDEV, D), lambda l: (0, 0)), pl.BlockSpec((1, D, n), lambda l: (l, 0, 0))],
        out_specs=pl.BlockSpec((1, NDEV, n), lambda l: (l, 0, 0)),
        out_shape=SDS((L, NDEV, n), f32),
        compiler_params=_cp(("arbitrary",)),
    )(c_all, ada_w)


def ada_grad(c_all, dmod, name):
    L, _, n = dmod.shape
    D = c_all.shape[1]

    def body(c_ref, d_ref, o_ref):
        cond = jnp.concatenate([_silu(c_ref[...]), jnp.zeros((8, D), f32)], axis=0)
        dm = jnp.concatenate([d_ref[0], jnp.zeros((8, n), f32)], axis=0)
        o_ref[0] = _tn(cond, dm)

    return pl.pallas_call(
        body, name=name, grid=(L,),
        in_specs=[pl.BlockSpec((NDEV, D), lambda l: (0, 0)), pl.BlockSpec((1, NDEV, n), lambda l: (l, 0, 0))],
        out_specs=pl.BlockSpec((1, D, n), lambda l: (l, 0, 0)),
        out_shape=SDS((L, D, n), f32),
        compiler_params=_cp(("arbitrary",)),
    )(c_all, dmod)


def reduce_adam(parts, w, m, v, tr, name):
    n, R, C = parts.shape
    c1 = 1.0 / (1.0 - ADAM_B1 ** ADAM_STEP)
    c2 = 1.0 / (1.0 - ADAM_B2 ** ADAM_STEP)

    def body(p_ref, w_ref, m_ref, v_ref, g_ref, d_ref, nm_ref, nv_ref):
        g = p_ref[0].astype(f32)
        for s in range(1, n):
            g = g + p_ref[s].astype(f32)
        nm = ADAM_B1 * m_ref[...] + (1.0 - ADAM_B1) * g
        nv = ADAM_B2 * v_ref[...] + (1.0 - ADAM_B2) * (g * g)
        g_ref[...] = g
        nm_ref[...] = nm
        nv_ref[...] = nv
        d_ref[...] = -ADAM_LR * ((nm * c1) / (jnp.sqrt(nv * c2) + ADAM_EPS) + ADAM_WD * w_ref[...])

    blk = pl.BlockSpec((tr, C), lambda i: (i, 0))
    return pl.pallas_call(
        body, name=name, grid=(R // tr,),
        in_specs=[pl.BlockSpec((n, tr, C), lambda i: (0, i, 0)), blk, blk, blk],
        out_specs=[blk] * 4, out_shape=[SDS((R, C), f32)] * 4,
        compiler_params=_cp(("arbitrary",)),
    )(parts, w, m, v)


def all_gather(xs, name):
    n = len(xs)

    def body(*refs):
        x_refs, out_refs = refs[:n], refs[n:2 * n]
        send_sems, recv_sems, local_sems = refs[2 * n:]
        x_, y_, c_ = _my_pos()
        me, sibling = (x_, y_, c_), (x_, y_, 1 - c_)
        chips = [(1 - x_, y_), (x_, 1 - y_), (1 - x_, 1 - y_)]

        def rows(a, px, py, pc):
            return out_refs[a].at[4 * px + 2 * py + pc]

        def copy(a, k, block, to, own=False):
            return pltpu.make_async_remote_copy(
                src_ref=x_refs[a] if own else rows(a, *block), dst_ref=rows(a, *block),
                send_sem=send_sems.at[k, a], recv_sem=recv_sems.at[k, a], device_id=to, device_id_type=pl.DeviceIdType.MESH)

        mine = [pltpu.make_async_copy(x_refs[a], rows(a, *me), local_sems.at[a]) for a in range(n)]
        for cp in mine:
            cp.start()
        first = []
        for a in range(n):
            first.append(copy(a, 0, me, sibling, own=True))
            first += [copy(a, 1 + j, me, (*chip, c_), own=True) for j, chip in enumerate(chips)]
        for cp in first:
            cp.start()
        passed = []
        for j, chip in enumerate(chips):
            for a in range(n):
                copy(a, 1 + j, (*chip, c_), me).wait_recv()
                cp = copy(a, 4 + j, (*chip, c_), sibling)
                cp.start()
                passed.append(cp)
        for a in range(n):
            copy(a, 0, sibling, me).wait_recv()
            for j, chip in enumerate(chips):
                copy(a, 4 + j, (*chip, 1 - c_), me).wait_recv()
        for cp in first + passed:
            cp.wait_send()
        for cp in mine:
            cp.wait()

    any_ = pl.BlockSpec(memory_space=pl.ANY)
    return pl.pallas_call(
        body, name=name, out_shape=[SDS((NDEV,) + x.shape, x.dtype) for x in xs],
        in_specs=[any_] * n, out_specs=[any_] * n,
        scratch_shapes=[pltpu.SemaphoreType.DMA((7, n)), pltpu.SemaphoreType.DMA((7, n)), pltpu.SemaphoreType.DMA((n,))],
    )(*xs)


GDN_COLS = ((0, GDN_CONV + GDN_V_W, 0), (GDN_CONV + GDN_V_W, GDN_CONV + GDN_V_W + 16, GDN_CONV + GDN_V_W),
            (GDN_CONV + GDN_V_W + 16, GDN_IN, GDN_CONV + GDN_V_W + 128))
FOX_COLS = ((0, FOX_IN, 0),)


def _col_pieces(d, per, cols):
    lo, hi = per * d, per * (d + 1)
    out = []
    for a, b, dst in cols:
        s, e = max(lo, a), min(hi, b)
        if s < e:
            out.append((s - lo, e - s, dst + s - a))
    return out


def cols_from_blocks(g, cols, n_out, name):
    _, L, R, C = g.shape
    tr = min(256, R)

    def body(g_ref, o_ref):
        o_ref[...] = jnp.zeros_like(o_ref)
        for d in range(NDEV):
            for off, ln, dst in _col_pieces(d, C, cols):
                o_ref[0, :, dst:dst + ln] = g_ref[d, 0, :, off:off + ln]

    return pl.pallas_call(
        body, name=name, grid=(L, R // tr),
        in_specs=[pl.BlockSpec((NDEV, 1, tr, C), lambda l, i: (0, l, i, 0))],
        out_specs=pl.BlockSpec((1, tr, n_out), lambda l, i: (l, i, 0)),
        out_shape=SDS((L, R, n_out), g.dtype),
        compiler_params=_cp(("arbitrary", "arbitrary")),
    )(g)


def blocks_from_cols(dw, C, cols, name):
    R, n_in = dw.shape
    tr = min(256, R)

    def body(x_ref, o_ref):
        for d in range(NDEV):
            for off, ln, src in _col_pieces(d, C, cols):
                o_ref[d, :, off:off + ln] = x_ref[:, src:src + ln].astype(bf16)

    return pl.pallas_call(
        body, name=name, grid=(R // tr,),
        in_specs=[pl.BlockSpec((tr, n_in), lambda i: (i, 0))],
        out_specs=pl.BlockSpec((NDEV, tr, C), lambda i: (0, i, 0)),
        out_shape=SDS((NDEV, R, C), bf16),
        compiler_params=_cp(("arbitrary",)),
    )(dw)


BIG = ("a_w_in", "a_conv_w", "a_w_out", "b_w_in", "b_w_out")
SMALL = ("norm_w", "ada_b", "a_A_log", "a_dt_bias", "a_norm_w", "b_f_bias", "b_qn_w", "b_kn_w", "final_norm_w")


def _pack_small(arrs):
    rows = []
    for a in arrs:
        fl = a.reshape(-1)
        pad = (-fl.shape[0]) % 128
        if pad:
            fl = jnp.concatenate([fl, jnp.zeros((pad,), fl.dtype)])
        rows.append(fl)
    flat = jnp.concatenate(rows)
    pad = (-flat.shape[0]) % (8 * 128)
    if pad:
        flat = jnp.concatenate([flat, jnp.zeros((pad,), flat.dtype)])
    return flat.reshape(-1, 128)


def _unpack(packed, shapes, align):
    flat = packed.reshape(-1)
    out, off = [], 0
    for shp in shapes:
        n = 1
        for d in shp:
            n *= d
        out.append(flat[off:off + n].reshape(shp))
        off += n + ((-n) % align)
    return out


def _full_from_gathered(g, shard_shape, axis):
    g = jnp.moveaxis(g, 0, axis)
    shp = list(shard_shape)
    shp[axis] *= NDEV
    return g.reshape(shp)


def _pad_lanes(v, n=128):
    v = v.reshape(1, -1)
    return jnp.concatenate([v, jnp.zeros((1, n - v.shape[1]), v.dtype)], axis=1)


def _carried(fn, *args, side=None, **grads):
    if callable(side):
        side = side(**grads)
    res = fn(*args, side)
    return res if side is not None else (res, None)


def gdn_layer_fwd(x, mod, nw, weights, tag, sides):
    W_in, conv_w, A_log, dt_bias, a_nw, W_out = weights
    shift, scale, gate = mod
    got = {}
    (proj, h), got["inproj"] = _carried(inproj_fwd, x, nw, scale, shift, W_in, GDN_TN, f"{tag}_inproj", side=sides.get("inproj"))
    (qkvc,), got["prep"] = _carried(gdn_prep_fwd, proj, conv_w, f"{tag}_prep", side=sides.get("prep"))
    gc, beta = gdn_gates_fwd(proj, A_log, dt_bias, f"{tag}_gates")
    (u, w, attn, Ts), got["intra"] = _carried(gdn_intra_fwd, qkvc, gc, beta, f"{tag}_intra", side=sides.get("intra"))
    (o, states), got["scan"] = _carried(gdn_scan_fwd, qkvc, gc, u, w, attn, f"{tag}_scan", side=sides.get("scan"))
    (x_new, y, og), got["out"] = _carried(gdn_out_fwd, o, proj, a_nw, W_out, x, gate, f"{tag}_out", side=sides.get("out"))
    return x_new, (x, proj, h, qkvc, gc, beta, o, states, Ts, y, og, u, w, attn), got


def gdn_layer_bwd(dxn, saved, mod, nw, weights, tag, sides):
    W_in, conv_w, A_log, dt_bias, a_nw, W_out = weights
    shift, scale, gate = mod
    x, proj, h, qkvc, gc, beta, o, states, Ts, y, og, u, w, attn = saved
    got = {}
    dy, dgate, do, dproj, da_nw = gdn_out_bwd(dxn, y, gate, o, proj, a_nw, W_out, f"{tag}_out_bwd")
    dW_out, = matmul_tn(og, dy, 512, f"{tag}_dwout")
    (dq_s, dk_s, dgc_s, du, dw, dattn), got["sbwd"] = _carried(
        gdn_scan_bwd, qkvc, gc, u, w, attn, states, do, f"{tag}_scan_bwd", side=sides.get("sbwd"), dW_out=dW_out)
    (dqkvc, dgc, dbeta), got["intrab"] = _carried(
        gdn_intra_bwd, qkvc, gc, beta, Ts, du, dw, dattn, dq_s, dk_s, dgc_s, f"{tag}_intra_bwd", side=sides.get("intrab"), dW_out=dW_out)
    dproj, dA_log, ddt = gdn_gates_bwd(proj, A_log, dt_bias, dgc, dbeta, dproj, f"{tag}_gates_bwd")
    dproj, dconv_w = gdn_prep_bwd(proj, conv_w, dqkvc, dproj, f"{tag}_prep_bwd")
    (dW_in,), got["dwin"] = _carried(matmul_tn, h, dproj, GDN_TN, f"{tag}_dwin", side=sides.get("dwin"), dW_out=dW_out)
    (dx, dnw, dscale, dshift), got["ibwd"] = _carried(
        inproj_bwd_x, x, nw, scale, shift, W_in, dproj, dxn, GDN_TN, f"{tag}_inproj_bwd", side=sides.get("ibwd"),
        dW_out=dW_out, dW_in=dW_in, dconv_w=dconv_w)
    grads = dict(norm_w=dnw, W_in=dW_in, conv_w=dconv_w, A_log=dA_log[:, :16], dt_bias=ddt[:, :16], a_nw=da_nw, W_out=dW_out,
                 dmod=jnp.concatenate([dshift, dscale, dgate], axis=1))
    return dx, grads, got


def fox_layer_fwd(x, mod, nw, weights, tag, sides):
    W_in, f_bias, qn_w, kn_w, W_out = weights
    shift, scale, gate = mod
    got = {}
    (proj, h), got["inproj"] = _carried(inproj_fwd, x, nw, scale, shift, W_in, FOX_TN, f"{tag}_inproj", side=sides.get("inproj"))
    Q, K, V = fox_prep_fwd(proj, f_bias, qn_w, kn_w, f"{tag}_prep")
    (O,), got["attn"] = _carried(fox_attn_fwd, Q, K, V, f"{tag}_attn", side=sides.get("attn"))
    x_new, y, og = fox_out_fwd(O, proj, W_out, x, gate, f"{tag}_out")
    return x_new, (x, proj, h, Q, K, V, O, y, og), got


def fox_layer_bwd(dxn, saved, mod, nw, weights, tag, sides):
    W_in, f_bias, qn_w, kn_w, W_out = weights
    shift, scale, gate = mod
    x, proj, h, Q, K, V, O, y, og = saved
    got = {}
    dy, dgate, dO, dz = fox_out_bwd(dxn, y, gate, O, proj, W_out, f"{tag}_out_bwd")
    dW_out, = matmul_tn(og, dy, 512, f"{tag}_dwout")
    (dQ, dK, dV), got["abwd"] = _carried(fox_attn_bwd, Q, K, V, dO, O, f"{tag}_attn_bwd", side=sides.get("abwd"))
    dproj, dfb, dqw, dkw = fox_prep_bwd(proj, f_bias, qn_w, kn_w, dQ, dK, dV, dz, f"{tag}_prep_bwd")
    (dW_in,), got["dwin"] = _carried(matmul_tn, h, dproj, FOX_TN, f"{tag}_dwin", side=sides.get("dwin"))
    dx, dnw, dscale, dshift = inproj_bwd_x(x, nw, scale, shift, W_in, dproj, dxn, FOX_TN, f"{tag}_inproj_bwd")
    grads = dict(norm_w=dnw, W_in=dW_in, f_bias=dfb[:, :16], qn_w=dqw, kn_w=dkw, W_out=dW_out,
                 dmod=jnp.concatenate([dshift, dscale, dgate], axis=1))
    return dx, grads, got


class LocalPlan:
    def __init__(self, full):
        self.full = full

    def layer_weights(self, i):
        j, f = i // 2, self.full
        return (f["a_w_in"][j], f["a_w_out"][j], f["a_conv_w"][j]) if i % 2 == 0 else (f["b_w_in"][j], f["b_w_out"][j])

    def fwd_sides(self, i):
        return {}

    def fwd_got(self, i, got):
        pass

    def bwd_sides(self, i):
        return {}

    def bwd_got(self, i, grads, got):
        pass


def device_step(x, mod_all, norm_w, small, final_norm_w, target, plan):
    D = x.shape[1]
    mods = [(mod_all[i:i + 1, 0:D], mod_all[i:i + 1, D:2 * D], mod_all[i:i + 1, 2 * D:3 * D]) for i in range(4)]

    def weights(i):
        j = i // 2
        if i % 2 == 0:
            W_in, W_out, conv_w = plan.layer_weights(i)
            return (W_in, conv_w, _pad_lanes(small["a_A_log"][j]), _pad_lanes(small["a_dt_bias"][j]), small["a_norm_w"][j:j + 1], W_out)
        W_in, W_out = plan.layer_weights(i)
        return (W_in, _pad_lanes(small["b_f_bias"][j]), small["b_qn_w"][j:j + 1], small["b_kn_w"][j:j + 1], W_out)

    saved, wts = [], []
    for i in range(4):
        wts.append(weights(i))
        fwd = gdn_layer_fwd if i % 2 == 0 else fox_layer_fwd
        x, sv, got = fwd(x, mods[i], norm_w[i:i + 1], wts[i], f"L{i}", plan.fwd_sides(i))
        plan.fwd_got(i, got)
        saved.append(sv)
    loss, dx, dfw = final_loss(x, final_norm_w.reshape(1, D), target, "final_loss")
    lg = [None] * 4
    for i in reversed(range(4)):
        bwd = gdn_layer_bwd if i % 2 == 0 else fox_layer_bwd
        dx, lg[i], got = bwd(dx, saved[i], mods[i], norm_w[i:i + 1], wts[i], f"L{i}", plan.bwd_sides(i))
        plan.bwd_got(i, lg[i], got)
    g = dict(
        norm_w=jnp.concatenate([lg[i]["norm_w"] for i in range(4)], axis=0),
        dmod=jnp.concatenate([lg[i]["dmod"] for i in range(4)], axis=0),
        a_w_in=[lg[i]["W_in"] for i in (0, 2)],
        a_conv_w=jnp.stack([lg[i]["conv_w"] for i in (0, 2)]),
        a_A_log=jnp.concatenate([lg[i]["A_log"] for i in (0, 2)], axis=0),
        a_dt_bias=jnp.concatenate([lg[i]["dt_bias"] for i in (0, 2)], axis=0),
        a_norm_w=jnp.concatenate([lg[i]["a_nw"] for i in (0, 2)], axis=0),
        a_w_out=[lg[i]["W_out"] for i in (0, 2)],
        b_w_in=[lg[i]["W_in"] for i in (1, 3)],
        b_f_bias=jnp.concatenate([lg[i]["f_bias"] for i in (1, 3)], axis=0),
        b_qn_w=jnp.concatenate([lg[i]["qn_w"] for i in (1, 3)], axis=0),
        b_kn_w=jnp.concatenate([lg[i]["kn_w"] for i in (1, 3)], axis=0),
        b_w_out=[lg[i]["W_out"] for i in (1, 3)],
        final_norm_w=dfw.reshape(-1),
    )
    return loss[0, 0], dx, g


class MeshPlan:
    def __init__(self, shards, w0, conv_full):
        self.shards = shards
        self.w = {0: w0}
        self.conv = conv_full
        self.recv = {}
        self.pending = {}
        self.names = {}

    def layer_weights(self, i):
        return self.w[i]

    def _gather_side(self, layer):
        names = ("a_w_in", "a_w_out") if layer % 2 == 0 else ("b_w_in", "b_w_out")
        out = []
        for n in names:
            sh = self.shards[n][layer // 2]
            out.append(sh.reshape(-1, sh.shape[-1]))
        return ("gather", out)

    def fwd_sides(self, i):
        if i == 0:
            kind, (b_in, b_out) = self._gather_side(1)
            kind, (a_in, a_out) = self._gather_side(2)
            half = b_in.shape[0] // 2
            return {"inproj": (kind, [b_out, a_out]), "scan": (kind, [b_in[:half]]), "prep": (kind, [b_in[half:]]),
                    "intra": (kind, [a_in])}
        if i == 1:
            return {"attn": self._gather_side(3)}
        return {}

    def fwd_got(self, i, got):
        if got.get("inproj") is not None:
            self._outs = dict(zip(("scan", "intra"), got["inproj"]))
        for key, layer in (("scan", 1), ("intra", 2), ("attn", 3)):
            if got.get(key) is None:
                continue
            g_in, g_out = got[key] if key == "attn" else (got[key][0], self._outs[key])
            if key == "scan":
                g_in = jnp.concatenate([g_in, got["prep"][0]], axis=1)
            D = g_out.shape[-1]
            j = layer // 2
            if layer % 2 == 1:
                W_in = cols_from_blocks(g_in[:, None], FOX_COLS, FOX_IN_PAD, f"b_w_in_cols{j}")[0]
                self.w[layer] = (W_in, g_out.reshape(-1, D))
            else:
                W_in = cols_from_blocks(g_in[:, None], GDN_COLS, GDN_IN_PAD, f"a_w_in_cols{j}")[0]
                self.w[layer] = (W_in, g_out.reshape(-1, D), self.conv[j])

    @staticmethod
    def _out_blocks(dW_out):
        return dW_out.astype(bf16).reshape(NDEV, -1, dW_out.shape[-1])

    def _in_blocks(self, name, j, dW_in):
        cols = GDN_COLS if name == "a_w_in" else FOX_COLS
        return blocks_from_cols(dW_in, self.shards[name].shape[-1], cols, f"{name}_blocks{j}")

    def bwd_sides(self, i):
        self.names = {}
        sides = {}
        for (layer, key) in [k for k in self.pending if k[0] == i]:
            self.names[key], arrs = self.pending.pop((layer, key))
            sides[key] = ("scatter", arrs)
        if i == 0:
            def sbwd(dW_out):
                self.names["sbwd"] = [("a_w_out", 0)]
                return ("scatter", [self._out_blocks(dW_out)])

            def ibwd(dW_out, dW_in, dconv_w):
                conv = jnp.stack([dconv_w, self._dconv1])
                n = conv.shape[-1] // NDEV
                self.names["ibwd"] = [("a_w_in", 0), ("a_conv_w", None)]
                return ("scatter", [self._in_blocks("a_w_in", 0, dW_in),
                                    jnp.moveaxis(conv.reshape(2, 4, NDEV, n), 2, 0).reshape(NDEV, 8, n)])

            sides["sbwd"], sides["ibwd"] = sbwd, ibwd
        return sides

    def bwd_got(self, i, grads, got):
        for key, arrs in got.items():
            if arrs is not None:
                self.recv.update(zip(self.names[key], arrs))
        j = i // 2
        if i % 2 == 1:
            self.pending[(i - 1, "sbwd" if i == 3 else "intrab")] = (
                [("b_w_in", j), ("b_w_out", j)], [self._in_blocks("b_w_in", j, grads["W_in"]), self._out_blocks(grads["W_out"])])
        elif i == 2:
            self.pending[(1, "abwd")] = (
                [("a_w_in", 1), ("a_w_out", 1)], [self._in_blocks("a_w_in", 1, grads["W_in"]), self._out_blocks(grads["W_out"])])
            self._dconv1 = grads["conv_w"]


def kernel(x, c, norm_w, ada_w, ada_b, a_w_in, a_conv_w, a_A_log, a_dt_bias, a_norm_w, a_w_out, b_w_in, b_f_bias, b_qn_w, b_kn_w, b_w_out, final_norm_w, loss_target, m_norm_w, m_ada_w, m_ada_b, m_a_w_in, m_a_conv_w, m_a_A_log, m_a_dt_bias, m_a_norm_w, m_a_w_out, m_b_w_in, m_b_f_bias, m_b_qn_w, m_b_kn_w, m_b_w_out, m_final_norm_w, v_norm_w, v_ada_w, v_ada_b, v_a_w_in, v_a_conv_w, v_a_A_log, v_a_dt_bias, v_a_norm_w, v_a_w_out, v_b_w_in, v_b_f_bias, v_b_qn_w, v_b_kn_w, v_b_w_out, v_final_norm_w):
    W = dict(norm_w=norm_w, ada_w=ada_w, ada_b=ada_b, a_w_in=a_w_in, a_conv_w=a_conv_w, a_A_log=a_A_log, a_dt_bias=a_dt_bias,
             a_norm_w=a_norm_w, a_w_out=a_w_out, b_w_in=b_w_in, b_f_bias=b_f_bias, b_qn_w=b_qn_w, b_kn_w=b_kn_w, b_w_out=b_w_out,
             final_norm_w=final_norm_w)
    M = dict(norm_w=m_norm_w, ada_w=m_ada_w, ada_b=m_ada_b, a_w_in=m_a_w_in, a_conv_w=m_a_conv_w, a_A_log=m_a_A_log,
             a_dt_bias=m_a_dt_bias, a_norm_w=m_a_norm_w, a_w_out=m_a_w_out, b_w_in=m_b_w_in, b_f_bias=m_b_f_bias, b_qn_w=m_b_qn_w,
             b_kn_w=m_b_kn_w, b_w_out=m_b_w_out, final_norm_w=m_final_norm_w)
    V = dict(norm_w=v_norm_w, ada_w=v_ada_w, ada_b=v_ada_b, a_w_in=v_a_w_in, a_conv_w=v_a_conv_w, a_A_log=v_a_A_log,
             a_dt_bias=v_a_dt_bias, a_norm_w=v_a_norm_w, a_w_out=v_a_w_out, b_w_in=v_b_w_in, b_f_bias=v_b_f_bias, b_qn_w=v_b_qn_w,
             b_kn_w=v_b_kn_w, b_w_out=v_b_w_out, final_norm_w=v_final_norm_w)
    S, D = x.shape[1], x.shape[2]
    me = 4 * lax.axis_index("x") + 2 * lax.axis_index("y") + lax.axis_index("c")
    small_shapes = [W[n].shape for n in SMALL]

    shards = {n: W[n].astype(bf16) for n in ("a_w_in", "a_w_out", "b_w_in", "b_w_out")}
    gath = all_gather([shards["a_w_in"][0], shards["a_w_out"][0], a_conv_w.reshape(8, -1), c.reshape(8, D // 8)], "gather_w0")
    conv_full = _full_from_gathered(gath[2].reshape((NDEV,) + a_conv_w.shape), a_conv_w.shape, 2)
    w0 = (cols_from_blocks(gath[0][:, None], GDN_COLS, GDN_IN_PAD, "a_w_in_cols0")[0], gath[1].reshape(-1, D), conv_full[0])
    plan = MeshPlan(shards, w0, conv_full)
    c_all = gath[3].reshape(NDEV, D)

    mod_part = ada_fwd(c_all, ada_w, "ada_fwd")
    n_ada = ada_w.shape[2]
    mod_g = all_gather([mod_part.reshape(4 * NDEV, n_ada)], "gather_mod")[0].reshape(NDEV, 4, NDEV, n_ada)
    mod_mine = lax.dynamic_index_in_dim(mod_g, me, axis=2, keepdims=False)
    mod_all = jnp.moveaxis(mod_mine, 0, 1).reshape(4, NDEV * n_ada) + ada_b

    loss, dx, g = device_step(x[0], mod_all, norm_w, W, final_norm_w, loss_target[0], plan)
    loss = lax.psum(loss, MESH_AXES)

    g_small = dict(g, ada_b=g["dmod"])
    sp = _pack_small([g_small[n] for n in SMALL])
    sp_all = all_gather([sp], "gather_small")[0]
    sw, sm, sv = (_pack_small([T[n] for n in SMALL]) for T in (W, M, V))
    sg, sd, snm, snv = (_unpack(t, small_shapes, 128) for t in reduce_adam(sp_all, sw, sm, sv, sp.shape[0], "adam_small"))

    off_b = 0
    for n, shp in zip(SMALL, small_shapes):
        if n == "ada_b":
            break
        cnt = 1
        for d in shp:
            cnt *= d
        off_b += cnt + ((-cnt) % 128)
    dmod_all = sp_all.reshape(NDEV, -1)[:, off_b:off_b + 4 * 3 * D].reshape(NDEV, 4, 3 * D)
    dmod_cols = lax.dynamic_slice_in_dim(dmod_all, me * n_ada, n_ada, axis=2)
    g_ada = ada_grad(c_all, jnp.moveaxis(dmod_cols, 0, 1), "ada_grad")
    r_ada = reduce_adam(g_ada.reshape(1, 4 * D, n_ada), *(T["ada_w"].reshape(4 * D, n_ada) for T in (W, M, V)), 512, "adam_ada")
    ag, ad, anm, anv = (t.reshape(ada_w.shape) for t in r_ada)

    big = {}
    for n in BIG:
        C = W[n].shape[-1]
        parts = plan.recv[(n, None)] if n == "a_conv_w" else jnp.stack([plan.recv[(n, 0)], plan.recv[(n, 1)]], axis=1).reshape(NDEV, -1, C)
        res = reduce_adam(parts, *(T[n].reshape(parts.shape[1:]) for T in (W, M, V)), min(256, parts.shape[1]), f"adam_{n}")
        big[n] = [t.reshape(W[n].shape) for t in res]

    outs = {}
    for idx, (k, sm_l, ada_t) in enumerate((("grad", sg, ag), ("delta", sd, ad), ("new_m", snm, anm), ("new_v", snv, anv))):
        d = dict(zip(SMALL, sm_l))
        d.update({n: big[n][idx] for n in BIG})
        d["ada_w"] = ada_t
        outs[k] = d
    order = ("norm_w", "ada_w", "ada_b", "a_w_in", "a_conv_w", "a_A_log", "a_dt_bias", "a_norm_w", "a_w_out", "b_w_in", "b_f_bias",
             "b_qn_w", "b_kn_w", "b_w_out", "final_norm_w")
    return (loss, dx[None], *[outs["grad"][n] for n in order], *[outs["delta"][n] for n in order],
            *[outs["new_m"][n] for n in order], *[outs["new_v"][n] for n in order])
```
